```python
import jax, jax.numpy as jnp
from jax import lax
import numpy as np

D_MODEL = 1024
BATCH = 16
SEQ = 256
DEPTH = 2
DEC_BATCH = 8
DEC_SEQ = 4096
PAST_LEN = 256

GRID_W = 64
N_EVEN = (DEPTH + 1) // 2
N_ODD = DEPTH // 2
RG_W = 512
RG_BLOCKS = 8
RG_BW = RG_W // RG_BLOCKS
RG_C = 8.0
CONV_W = 4
CONV_LEFT = 2
MLA_HEADS = 8
QK_NOPE = 64
QK_ROPE = 32
V_HEAD = 64
Q_LORA = 768
KV_LORA = 256
Q_BLK = 128
ROPE_BASE = 10000.0
ATTN_SCALE = (QK_NOPE + QK_ROPE) ** -0.5
IN_AB = 2 * RG_W + Q_LORA + KV_LORA + QK_ROPE
MIX_AB = RG_W + MLA_HEADS * V_HEAD
RET_HEADS = 4
RET_DK = 256
RET_DV = 512
RET_CHUNK = 128
MIX_C = RET_HEADS * RET_DV
IN_C = 2 * RET_HEADS * RET_DK + 2 * MIX_C
N_EXPERTS = 64
TOP_K = 8
N_GROUPS = 8
TOPK_GROUPS = 4
D_EXPERT = 256
ROUTED_SCALE = 2.5
MOE_BLK = 128
ALPHA = (2 * DEPTH) ** 0.25
BETA = (8 * DEPTH) ** -0.25
EPS = 1e-6

kernel_name = "hybrid_rglru_mla_retention_moe_diffusion_step"


def layernorm(x, g, b):
    xf = x.astype(jnp.float32)
    mu = jnp.mean(xf, axis=-1, keepdims=True)
    var = jnp.mean(jnp.square(xf - mu), axis=-1, keepdims=True)
    return ((xf - mu) * lax.rsqrt(var + EPS)).astype(x.dtype) * g + b


def rmsnorm(x, g):
    xf = x.astype(jnp.float32)
    ms = jnp.mean(jnp.square(xf), axis=-1, keepdims=True)
    return (xf * lax.rsqrt(ms + EPS)).astype(x.dtype) * g


def ada_modulation(cond, w_ada, b_ada):
    m = (jax.nn.silu(cond) @ w_ada + b_ada).reshape(cond.shape[0], 1, 6, D_MODEL)
    return [m[:, :, i] for i in range(6)]


def modulate(x, shift, scale):
    return x * (1.0 + scale) + shift


def rot_half(x, ang):
    d2 = x.shape[-1] // 2
    cos = jnp.cos(ang).astype(x.dtype)
    sin = jnp.sin(ang).astype(x.dtype)
    x1, x2 = x[..., :d2], x[..., d2:]
    return jnp.concatenate([x1 * cos - x2 * sin, x2 * cos + x1 * sin], axis=-1)


def axial_angles(n_tok, axis_dim):
    rows = n_tok // GRID_W
    row = jnp.repeat(jnp.arange(rows), GRID_W).astype(jnp.float32)
    col = jnp.tile(jnp.arange(GRID_W), rows).astype(jnp.float32)
    inv = ROPE_BASE ** (-jnp.arange(axis_dim // 2, dtype=jnp.float32) / (axis_dim // 2))
    return row[:, None] * inv, col[:, None] * inv


def axial_rope(x, ang_row, ang_col):
    half = x.shape[-1] // 2
    return jnp.concatenate([rot_half(x[..., :half], ang_row), rot_half(x[..., half:], ang_col)], axis=-1)


def dwconv_centred(x, w, b):
    S = x.shape[1]
    xp = jnp.pad(x, ((0, 0), (CONV_LEFT, CONV_W - 1 - CONV_LEFT), (0, 0)))
    out = b
    for j in range(CONV_W):
        out = out + xp[:, j:j + S] * w[j]
    return out


def linear_scan(a, b, h0, reverse):
    def step(h, ab):
        at, bt = ab
        h = at * h + bt
        return h, h
    h_fin, hs = lax.scan(step, h0, (jnp.swapaxes(a, 0, 1), jnp.swapaxes(b, 0, 1)), reverse=reverse)
    return jnp.swapaxes(hs, 0, 1), h_fin


def rglru_mix(xr, h0, conv_w, conv_b, wa, ba, wx, bx, lam):
    B, S, R = xr.shape
    xc = dwconv_centred(xr, conv_w, conv_b)
    xb = xc.reshape(B, S, RG_BLOCKS, RG_BW)
    y = 0.0
    finals = []
    for d in range(2):
        r = jax.nn.sigmoid(jnp.einsum('bsgi,gij->bsgj', xb, wa[d]).reshape(B, S, R) + ba[d])
        i = jax.nn.sigmoid(jnp.einsum('bsgi,gij->bsgj', xb, wx[d]).reshape(B, S, R) + bx[d])
        log_a = (-RG_C * r * jax.nn.softplus(-lam[d])).astype(jnp.float32)
        a = jnp.exp(log_a)
        bt = jnp.sqrt(-jnp.expm1(2.0 * log_a)) * (i * xc).astype(jnp.float32)
        h, h_fin = linear_scan(a, bt, h0[:, d].astype(jnp.float32), reverse=(d == 1))
        y = y + h
        finals.append(h_fin)
    return y.astype(xr.dtype), jnp.stack(finals, axis=1).astype(xr.dtype)


def mla_expand(ckv_n, w_ukv):
    B, S, _ = ckv_n.shape
    kv = (ckv_n @ w_ukv).reshape(B, S, MLA_HEADS, QK_NOPE + V_HEAD)
    return kv[..., :QK_NOPE], kv[..., QK_NOPE:]


def mla_attention(q_nope, q_rope, key_sets):
    B, S, H, _ = q_nope.shape
    nq = S // Q_BLK

    def blocks(t):
        return jnp.swapaxes(t.reshape((B, nq, Q_BLK) + t.shape[2:]), 0, 1)

    def one_block(qb):
        qn, qr = qb
        s = jnp.concatenate(
            [jnp.einsum('bqhd,bkhd->bhqk', qn, kn) + jnp.einsum('bqhd,bkd->bhqk', qr, kr)
             for kn, kr, _ in key_sets], axis=-1)
        p = jax.nn.softmax(s.astype(jnp.float32) * ATTN_SCALE, axis=-1).astype(qn.dtype)
        out = 0.0
        off = 0
        for _, _, v in key_sets:
            n = v.shape[1]
            out = out + jnp.einsum('bhqk,bkhd->bqhd', p[..., off:off + n], v)
            off += n
        return out

    o = lax.map(one_block, (blocks(q_nope), blocks(q_rope)))
    return jnp.swapaxes(o, 0, 1).reshape(B, S, H, V_HEAD)


def mixer_ab(h, h0, ctx_ckv, ctx_krope, w_in, conv_w, conv_b, wa, ba, wx, bx, lam,
             q_norm, w_uq, kv_norm, w_ukv, w_out):
    B, S, _ = h.shape
    proj = h @ w_in
    xr, gr, cq, ckv, kr = jnp.split(
        proj, [RG_W, 2 * RG_W, 2 * RG_W + Q_LORA, 2 * RG_W + Q_LORA + KV_LORA], axis=-1)
    y_rg, h_fin = rglru_mix(xr, h0, conv_w, conv_b, wa, ba, wx, bx, lam)
    y_rg = y_rg * jax.nn.gelu(gr)
    q = (rmsnorm(cq, q_norm) @ w_uq).reshape(B, S, MLA_HEADS, QK_NOPE + QK_ROPE)
    q_nope, q_rope = q[..., :QK_NOPE], q[..., QK_NOPE:]
    ckv_n = rmsnorm(ckv, kv_norm)
    k_nope, v = mla_expand(ckv_n, w_ukv)
    if ctx_ckv is None:
        key_sets = [(k_nope, kr, v)]
    else:
        ang_r, ang_c = axial_angles(S, QK_ROPE // 2)
        q_rope = axial_rope(q_rope, ang_r[:, None, :], ang_c[:, None, :])
        kr_rot = axial_rope(kr, ang_r, ang_c)
        kc_nope, vc = mla_expand(ctx_ckv, w_ukv)
        key_sets = [(k_nope, kr_rot, v), (kc_nope, ctx_krope, vc)]
    o = mla_attention(q_nope, q_rope, key_sets)
    y = jnp.concatenate([y_rg, o.reshape(B, S, MLA_HEADS * V_HEAD)], axis=-1) @ w_out
    return y, h_fin, ckv_n, kr


def retention_scan(q, k, v, log_g, r0):
    B, S, H, dk = q.shape
    dv = v.shape[-1]
    C = RET_CHUNK
    n = S // C
    qc = q.reshape(B, n, C, H, dk).transpose(1, 0, 3, 2, 4)
    kc = k.reshape(B, n, C, H, dk).transpose(1, 0, 3, 2, 4)
    vc = v.reshape(B, n, C, H, dv).transpose(1, 0, 3, 2, 4)
    idx = jnp.arange(C, dtype=jnp.float32)
    diff = idx[:, None] - idx[None, :]
    lg = log_g[:, None, None]
    dmat = jnp.where(diff >= 0, jnp.exp(jnp.maximum(diff, 0.0) * lg), 0.0).astype(q.dtype)
    xi = jnp.exp((idx + 1.0) * log_g[:, None]).astype(q.dtype)
    zeta = jnp.exp((C - 1.0 - idx) * log_g[:, None]).astype(q.dtype)
    g_chunk = jnp.exp(C * log_g).astype(q.dtype)

    def body(r, inp):
        qb, kb, vb = inp
        inner = jnp.einsum('bhid,bhjd->bhij', qb, kb) * dmat
        o = jnp.einsum('bhij,bhje->bhie', inner, vb) + jnp.einsum('bhid,bhde->bhie', qb * xi[:, :, None], r)
        r = r * g_chunk[:, None, None] + jnp.einsum('bhjd,bhje->bhde', kb * zeta[:, :, None], vb)
        return r, o

    r_fin, o = lax.scan(body, r0.astype(q.dtype), (qc, kc, vc))
    return o.transpose(1, 0, 3, 2, 4).reshape(B, S, H, dv), r_fin


def mixer_c(h, r0, gamma_logit, w_in, w_out, latent):
    B, S, _ = h.shape
    proj = h @ w_in
    q, k, v, g = jnp.split(proj, [RET_HEADS * RET_DK, 2 * RET_HEADS * RET_DK,
                                  2 * RET_HEADS * RET_DK + MIX_C], axis=-1)
    q = q.reshape(B, S, RET_HEADS, RET_DK)
    k = k.reshape(B, S, RET_HEADS, RET_DK)
    v = v.reshape(B, S, RET_HEADS, RET_DV)
    if latent:
        theta = ROPE_BASE ** (-jnp.linspace(0.0, 1.0, RET_DK // 2, dtype=jnp.float32))
        ang = jnp.arange(S, dtype=jnp.float32)[:, None] * theta
        q = rot_half(q, ang[:, None, :])
        k = rot_half(k, ang[:, None, :])
    k = k * (RET_DK ** -0.5)
    log_g = jax.nn.log_sigmoid(gamma_logit.astype(jnp.float32))
    o_f, r_f = retention_scan(q, k, v, log_g[0], r0[:, 0])
    o_b, r_b = retention_scan(q[:, ::-1], k[:, ::-1], v[:, ::-1], log_g[1], r0[:, 1])
    o = (o_f + o_b[:, ::-1]).astype(jnp.float32)
    mu = jnp.mean(o, axis=-1, keepdims=True)
    var = jnp.mean(jnp.square(o - mu), axis=-1, keepdims=True)
    o = ((o - mu) * lax.rsqrt(var + EPS)).astype(h.dtype)
    y = (o.reshape(B, S, MIX_C) * jax.nn.silu(g)) @ w_out
    return y, jnp.stack([r_f, r_b], axis=1)


def moe_ffn(h, w_router, router_bias, w_gate, w_up, w_down, w_sh_gate, w_sh_up, w_sh_down):
    B, S, D = h.shape
    x2 = h.reshape(-1, D)
    N = x2.shape[0]
    scores = jax.nn.sigmoid((x2 @ w_router).astype(jnp.float32))
    sel = scores + router_bias.astype(jnp.float32)
    grp = sel.reshape(N, N_GROUPS, N_EXPERTS // N_GROUPS)
    grp_score = lax.top_k(grp, 2)[0].sum(-1)
    _, top_grp = lax.top_k(grp_score, TOPK_GROUPS)
    gmask = jnp.any(top_grp[:, :, None] == jnp.arange(N_GROUPS)[None, None, :], axis=1)
    sel = jnp.where(jnp.repeat(gmask, N_EXPERTS // N_GROUPS, axis=1), sel, -jnp.inf)
    _, ids = lax.top_k(sel, TOP_K)
    w = jnp.take_along_axis(scores, ids, axis=1)
    w = w / jnp.sum(w, axis=-1, keepdims=True) * ROUTED_SCALE
    flat_e = ids.reshape(-1)
    P = flat_e.shape[0]
    order = jnp.argsort(flat_e)
    sorted_e = flat_e[order]
    counts = jnp.bincount(flat_e, length=N_EXPERTS)
    padded = (counts + MOE_BLK - 1) // MOE_BLK * MOE_BLK
    pad_end = jnp.cumsum(padded)
    pad_start = pad_end - padded
    start = jnp.cumsum(counts) - counts
    dest = pad_start[sorted_e] + (jnp.arange(P) - start[sorted_e])
    n_blocks = -(-P // MOE_BLK) + N_EXPERTS
    n_rows = n_blocks * MOE_BLK
    row_tok = jnp.zeros((n_rows,), jnp.int32).at[dest].set((order // TOP_K).astype(jnp.int32))
    row_w = jnp.zeros((n_rows,), jnp.float32).at[dest].set(w.reshape(-1)[order])
    block_e = jnp.minimum(jnp.searchsorted(pad_end, jnp.arange(n_blocks) * MOE_BLK, side='right'),
                          N_EXPERTS - 1)

    def block_fn(args):
        tok, e = args
        xb = x2[tok]
        hb = jax.nn.silu(xb @ w_gate[e]) * (xb @ w_up[e])
        return hb @ w_down[e]

    yb = lax.map(block_fn, (row_tok.reshape(n_blocks, MOE_BLK), block_e)).reshape(n_rows, D)
    routed = jax.ops.segment_sum(yb * row_w[:, None].astype(yb.dtype), row_tok, num_segments=N)
    shared = (jax.nn.silu(x2 @ w_sh_gate) * (x2 @ w_sh_up)) @ w_sh_down
    return (routed + shared).reshape(B, S, D)


def setup_inputs(seed: int = 0) -> dict:
    key = jax.random.key(seed)
    ks = iter(jax.random.split(key, 48))
    f32 = jnp.float32

    def nrm(shape, scale):
        return jax.random.normal(next(ks), shape, f32) * scale

    x_prompt = nrm((BATCH, SEQ, D_MODEL), 1.0)
    x_sample = nrm((DEC_BATCH, DEC_SEQ, D_MODEL), 1.0)
    cache_mla_ckv = nrm((DEC_BATCH, N_EVEN, PAST_LEN, KV_LORA), 1.0)
    cache_mla_krope = nrm((DEC_BATCH, N_EVEN, PAST_LEN, QK_ROPE), 1.0)
    state_rglru = nrm((DEC_BATCH, N_EVEN, 2, RG_W), 0.5)
    state_ret = nrm((DEC_BATCH, N_ODD, 2, RET_HEADS, RET_DK, RET_DV), 0.05)
    c = nrm((DEC_BATCH, D_MODEL), 1.0)
    c_ctx = nrm((D_MODEL,), 1.0)
    w_ada = nrm((DEPTH, D_MODEL, 6 * D_MODEL), 0.5 * D_MODEL ** -0.5)
    b_ada = nrm((DEPTH, 6 * D_MODEL), 0.01)
    ln_g = 1.0 + nrm((DEPTH, 2, D_MODEL), 0.01)
    ln_b = nrm((DEPTH, 2, D_MODEL), 0.01)
    w_in_ab = nrm((N_EVEN, D_MODEL, IN_AB), D_MODEL ** -0.5)
    rg_conv_w = nrm((N_EVEN, CONV_W, RG_W), CONV_W ** -0.5)
    rg_conv_b = nrm((N_EVEN, RG_W), 0.01)
    rg_wa = nrm((N_EVEN, 2, RG_BLOCKS, RG_BW, RG_BW), RG_BW ** -0.5)
    rg_ba = nrm((N_EVEN, 2, RG_W), 0.01)
    rg_wx = nrm((N_EVEN, 2, RG_BLOCKS, RG_BW, RG_BW), RG_BW ** -0.5)
    rg_bx = nrm((N_EVEN, 2, RG_W), 0.01)
    a0 = jax.random.uniform(next(ks), (N_EVEN, 2, RG_W), f32, 0.9, 0.999)
    p0 = a0 ** (1.0 / RG_C)
    rg_lambda = jnp.log(p0) - jnp.log1p(-p0)
    mla_q_norm = 1.0 + nrm((N_EVEN, Q_LORA), 0.01)
    mla_w_uq = nrm((N_EVEN, Q_LORA, MLA_HEADS * (QK_NOPE + QK_ROPE)), Q_LORA ** -0.5)
    mla_kv_norm = 1.0 + nrm((N_EVEN, KV_LORA), 0.01)
    mla_w_ukv = nrm((N_EVEN, KV_LORA, MLA_HEADS * (QK_NOPE + V_HEAD)), KV_LORA ** -0.5)
    w_out_ab = nrm((N_EVEN, MIX_AB, D_MODEL), BETA * MIX_AB ** -0.5)
    w_in_c = nrm((N_ODD, D_MODEL, IN_C), D_MODEL ** -0.5)
    gamma0 = 1.0 - 2.0 ** (-5.0 - jnp.arange(RET_HEADS, dtype=f32))
    ret_gamma_logit = (jnp.log(gamma0) - jnp.log1p(-gamma0))[None, None, :] + nrm((N_ODD, 2, RET_HEADS), 0.1)
    w_out_c = nrm((N_ODD, MIX_C, D_MODEL), BETA * MIX_C ** -0.5)
    w_router = nrm((DEPTH, D_MODEL, N_EXPERTS), D_MODEL ** -0.5)
    router_bias = nrm((DEPTH, N_EXPERTS), 0.01)
    w_exp_gate = nrm((DEPTH, N_EXPERTS, D_MODEL, D_EXPERT), D_MODEL ** -0.5)
    w_exp_up = nrm((DEPTH, N_EXPERTS, D_MODEL, D_EXPERT), D_MODEL ** -0.5)
    w_exp_down = nrm((DEPTH, N_EXPERTS, D_EXPERT, D_MODEL), BETA * D_EXPERT ** -0.5)
    w_sh_gate = nrm((DEPTH, D_MODEL, D_EXPERT), D_MODEL ** -0.5)
    w_sh_up = nrm((DEPTH, D_MODEL, D_EXPERT), D_MODEL ** -0.5)
    w_sh_down = nrm((DEPTH, D_EXPERT, D_MODEL), BETA * D_EXPERT ** -0.5)
    return {"x_prompt": x_prompt, "x_sample": x_sample,
            "cache_mla_ckv": cache_mla_ckv, "cache_mla_krope": cache_mla_krope,
            "state_rglru": state_rglru, "state_ret": state_ret,
            "c": c, "c_ctx": c_ctx, "w_ada": w_ada, "b_ada": b_ada, "ln_g": ln_g, "ln_b": ln_b,
            "w_in_ab": w_in_ab, "rg_conv_w": rg_conv_w, "rg_conv_b": rg_conv_b,
            "rg_wa": rg_wa, "rg_ba": rg_ba, "rg_wx": rg_wx, "rg_bx": rg_bx, "rg_lambda": rg_lambda,
            "mla_q_norm": mla_q_norm, "mla_w_uq": mla_w_uq, "mla_kv_norm": mla_kv_norm,
            "mla_w_ukv": mla_w_ukv, "w_out_ab": w_out_ab,
            "w_in_c": w_in_c, "ret_gamma_logit": ret_gamma_logit, "w_out_c": w_out_c,
            "w_router": w_router, "router_bias": router_bias,
            "w_exp_gate": w_exp_gate, "w_exp_up": w_exp_up, "w_exp_down": w_exp_down,
            "w_sh_gate": w_sh_gate, "w_sh_up": w_sh_up, "w_sh_down": w_sh_down}


def reference(x_prompt, x_sample, cache_mla_ckv, cache_mla_krope, state_rglru, state_ret,
              c, c_ctx, w_ada, b_ada, ln_g, ln_b,
              w_in_ab, rg_conv_w, rg_conv_b, rg_wa, rg_ba, rg_wx, rg_bx, rg_lambda,
              mla_q_norm, mla_w_uq, mla_kv_norm, mla_w_ukv, w_out_ab,
              w_in_c, ret_gamma_logit, w_out_c,
              w_router, router_bias, w_exp_gate, w_exp_up, w_exp_down,
              w_sh_gate, w_sh_up, w_sh_down):
    xp, xs = x_prompt, x_sample
    ckv_new, krope_new, rg_new, ret_new = [], [], [], []
    for l in range(DEPTH):
        mc = ada_modulation(c_ctx[None, :], w_ada[l], b_ada[l])
        ms = ada_modulation(c, w_ada[l], b_ada[l])
        hp = modulate(xp, mc[0], mc[1])
        hs = modulate(xs, ms[0], ms[1])
        if l % 2 == 0:
            e = l // 2
            p_ab = (w_in_ab[e], rg_conv_w[e], rg_conv_b[e], rg_wa[e], rg_ba[e], rg_wx[e], rg_bx[e],
                    rg_lambda[e], mla_q_norm[e], mla_w_uq[e], mla_kv_norm[e], mla_w_ukv[e], w_out_ab[e])
            h0 = jnp.zeros((xp.shape[0], 2, RG_W), xp.dtype)
            yp, rg_fin, ckv_n, kr = mixer_ab(hp, h0, None, None, *p_ab)
            ys, _, _, _ = mixer_ab(hs, state_rglru[:, e], cache_mla_ckv[:, e], cache_mla_krope[:, e], *p_ab)
            ckv_new.append(ckv_n)
            krope_new.append(kr)
            rg_new.append(rg_fin)
        else:
            o = l // 2
            r0 = jnp.zeros((xp.shape[0], 2, RET_HEADS, RET_DK, RET_DV), xp.dtype)
            yp, r_fin = mixer_c(hp, r0, ret_gamma_logit[o], w_in_c[o], w_out_c[o], latent=False)
            ys, _ = mixer_c(hs, state_ret[:, o], ret_gamma_logit[o], w_in_c[o], w_out_c[o], latent=True)
            ret_new.append(r_fin)
        xp = layernorm(ALPHA * xp + mc[2] * yp, ln_g[l, 0], ln_b[l, 0])
        xs = layernorm(ALPHA * xs + ms[2] * ys, ln_g[l, 0], ln_b[l, 0])
        p_moe = (w_router[l], router_bias[l], w_exp_gate[l], w_exp_up[l], w_exp_down[l],
                 w_sh_gate[l], w_sh_up[l], w_sh_down[l])
        xp = layernorm(ALPHA * xp + mc[5] * moe_ffn(modulate(xp, mc[3], mc[4]), *p_moe), ln_g[l, 1], ln_b[l, 1])
        xs = layernorm(ALPHA * xs + ms[5] * moe_ffn(modulate(xs, ms[3], ms[4]), *p_moe), ln_g[l, 1], ln_b[l, 1])
    new_cache_mla_ckv = jnp.stack(ckv_new, axis=1)
    new_cache_mla_krope = jnp.stack(krope_new, axis=1)
    new_state_rglru = jnp.stack(rg_new, axis=1)
    new_state_ret = jnp.stack(ret_new, axis=1)
    return (xp, xs, new_cache_mla_ckv, new_cache_mla_krope, new_state_rglru, new_state_ret)
```

```python
import functools
import math

import jax
import jax.numpy as jnp
from jax import lax
from jax.experimental import pallas as pl
from jax.experimental.pallas import tpu as pltpu

F32 = jnp.float32
BF16 = jnp.bfloat16
I32 = jnp.int32

D_MODEL = 1024
BATCH, SEQ = 16, 256
DEC_BATCH, DEC_SEQ = 8, 4096
PAST_LEN = 256
DEPTH = 2
GRID_W = 64
RG_W, RG_BLOCKS = 512, 8
RG_BW = RG_W // RG_BLOCKS
RG_C = 8.0
CONV_W, CONV_LEFT = 4, 2
MLA_HEADS, QK_NOPE, QK_ROPE, V_HEAD = 8, 64, 32, 64
Q_LORA, KV_LORA = 768, 256
ROPE_BASE = 10000.0
ATTN_SCALE = (QK_NOPE + QK_ROPE) ** -0.5
RET_HEADS, RET_DK, RET_DV, RET_CHUNK = 4, 256, 512, 128
MIX_C = RET_HEADS * RET_DV
N_EXPERTS, TOP_K, N_GROUPS, TOPK_GROUPS = 64, 8, 8, 4
GROUP_SIZE = N_EXPERTS // N_GROUPS
D_EXPERT = 256
ROUTED_SCALE = 2.5
ALPHA = (2 * DEPTH) ** 0.25
EPS = 1e-6

N_PROMPT = BATCH * SEQ
N_SAMPLE = DEC_BATCH * DEC_SEQ
N_TOK = N_PROMPT + N_SAMPLE
N_COND = 1 + DEC_BATCH
MOD_ROWS = 8

LANE = 128
SUBLANE = 8
TM = 512
HEAD_PAD = 128
ROPE_LANE0 = QK_NOPE
TQ = 256
TK = 512
SCAN_ROWS = 64
GATE_ROWS = 256
TT = 512
TME = 512
N_PAIRS = N_TOK * TOP_K
N_EBLOCKS = N_PAIRS // TME + N_EXPERTS
N_EROWS = N_EBLOCKS * TME
TC_TOK = 128
NEG = -1e30


def _cparams(sem, vmem_mb=48):
    return pltpu.CompilerParams(dimension_semantics=sem, vmem_limit_bytes=vmem_mb * 1024 * 1024)


def _cond_block(i, tm):
    npb = N_PROMPT // tm
    return jnp.where(i < npb, 0, 1 + (i - npb) // (DEC_SEQ // tm))


def _pos_block(i, tm):
    npb = N_PROMPT // tm
    return jnp.where(i < npb, 0, 1 + (i - npb) % (DEC_SEQ // tm))


def _split_hi_lo(a):
    hi = a.astype(BF16)
    lo = (a - hi.astype(F32)).astype(BF16)
    return hi, lo


def _dot(a, b):
    return jnp.dot(a, b, preferred_element_type=F32)


def _dot_nt(a, b):
    return lax.dot_general(a, b, (((1,), (1,)), ((), ())), preferred_element_type=F32)


def _dot_tn(a, b):
    return lax.dot_general(a, b, (((0,), (0,)), ((), ())), preferred_element_type=F32)


def _silu(x):
    return x * jax.nn.sigmoid(x)


def _gelu_tanh(x):
    return 0.5 * x * (1.0 + jnp.tanh(math.sqrt(2.0 / math.pi) * (x + 0.044715 * (x * x * x))))


def _softplus(x):
    return jnp.maximum(x, 0.0) + jnp.log1p(jnp.exp(-jnp.abs(x)))


def _layernorm_rows(z, g, b):
    mu = jnp.mean(z, axis=-1, keepdims=True)
    zc = z - mu
    var = jnp.mean(zc * zc, axis=-1, keepdims=True)
    return (zc * lax.rsqrt(var + EPS)) * g + b


def _ada_kernel(c_ref, w_ref, b_ref, o_ref):
    s_hi, s_lo = _split_hi_lo(_silu(c_ref[...]))
    w_hi, w_lo = _split_hi_lo(w_ref[...])
    o_ref[...] = _dot(s_hi, w_hi) + _dot(s_hi, w_lo) + _dot(s_lo, w_hi) + b_ref[...]


def _ada_modulation(cond, w_ada, b_ada):
    n6 = 6 * D_MODEL
    tn = D_MODEL
    return pl.pallas_call(
        _ada_kernel,
        out_shape=jax.ShapeDtypeStruct((DEPTH, 16, n6), F32),
        grid=(DEPTH, n6 // tn),
        in_specs=[
            pl.BlockSpec((16, D_MODEL), lambda l, j: (0, 0)),
            pl.BlockSpec((None, D_MODEL, tn), lambda l, j: (l, 0, j)),
            pl.BlockSpec((None, 1, tn), lambda l, j: (l, 0, j)),
        ],
        out_specs=pl.BlockSpec((None, 16, tn), lambda l, j: (l, 0, j)),
        compiler_params=_cparams(("arbitrary", "arbitrary")),
        name="ada_modulation",
    )(cond, w_ada, b_ada.reshape(DEPTH, 1, n6))


def _modulated(x, mod_ref, shift_row, scale_row):
    return x * (1.0 + mod_ref[scale_row:scale_row + 1, :]) + mod_ref[shift_row:shift_row + 1, :]


def _proj_ab_kernel(x_ref, mod_ref, w_ref, wkr_ref, main_ref, kr_ref):
    h = _modulated(x_ref[...], mod_ref, 0, 1).astype(BF16)
    n = w_ref.shape[1]
    step = 512
    for j in range(n // step):
        main_ref[:, j * step:(j + 1) * step] = _dot(h, w_ref[:, j * step:(j + 1) * step])
    kr_ref[...] = _dot(h, wkr_ref[...])


def _proj_ab(x, mods, w_main, w_kr):
    n = w_main.shape[1]
    return pl.pallas_call(
        _proj_ab_kernel,
        out_shape=(jax.ShapeDtypeStruct((N_TOK, n), F32), jax.ShapeDtypeStruct((N_TOK, LANE), F32)),
        grid=(N_TOK // TM,),
        in_specs=[
            pl.BlockSpec((TM, D_MODEL), lambda i: (i, 0)),
            pl.BlockSpec((None, MOD_ROWS, D_MODEL), lambda i: (_cond_block(i, TM), 0, 0)),
            pl.BlockSpec((D_MODEL, n), lambda i: (0, 0)),
            pl.BlockSpec((D_MODEL, LANE), lambda i: (0, 0)),
        ],
        out_specs=(pl.BlockSpec((TM, n), lambda i: (i, 0)), pl.BlockSpec((TM, LANE), lambda i: (i, 0))),
        compiler_params=_cparams(("arbitrary",)),
        name="proj_ab",
    )(x, mods, w_main, w_kr)


def _rglru_kernel(xr_ref, gr_ref, cw_ref, cb_ref, wg_ref, bg_ref, lam_ref, h0_ref, yprev_ref,
                  y_ref, hfin_ref, xpad, a_s, b_s, *, seq):
    del yprev_ref
    pad = SUBLANE
    xpad[0:pad, :] = jnp.zeros((pad, LANE), F32)
    xpad[seq + pad:seq + 2 * pad, :] = jnp.zeros((pad, LANE), F32)
    xpad[pad:seq + pad, :] = xr_ref[...]

    sp = _softplus(-lam_ref[...])
    cw = cw_ref[...]
    cb = cb_ref[...]
    wg = wg_ref[...]
    bg = bg_ref[...]

    def gate_step(c, carry):
        t0 = pl.multiple_of(c * GATE_ROWS, GATE_ROWS)
        win = xpad[pl.ds(t0, GATE_ROWS + 2 * pad), :]
        xc = cb
        for j in range(CONV_W):
            off = pad - CONV_LEFT + j
            xc = xc + win[off:off + GATE_ROWS, :] * cw[j:j + 1, :]
        g = _dot(xc.astype(BF16), wg) + bg
        for d in range(2):
            r = jax.nn.sigmoid(g[:, (2 * d) * LANE:(2 * d + 1) * LANE])
            i = jax.nn.sigmoid(g[:, (2 * d + 1) * LANE:(2 * d + 2) * LANE])
            log_a = (-RG_C * r) * sp[d:d + 1, :]
            a = jnp.exp(log_a)
            t = jnp.tanh(log_a)
            bt = jnp.sqrt(2.0 * t / (t - 1.0)) * (i * xc)
            a_s[d, pl.ds(t0, GATE_ROWS), :] = a
            b_s[d, pl.ds(t0, GATE_ROWS), :] = bt
        return carry

    lax.fori_loop(0, seq // GATE_ROWS, gate_step, 0)

    row = lax.broadcasted_iota(I32, (SCAN_ROWS, LANE), 0) % SUBLANE
    n_steps = seq // SCAN_ROWS
    tiles = SCAN_ROWS // SUBLANE

    def local_scan(a, b, reverse):
        for k in (1, 2, 4):
            if reverse:
                ok = row < SUBLANE - k
                shift = SCAN_ROWS - k
            else:
                ok = row >= k
                shift = k
            a_sh = jnp.where(ok, pltpu.roll(a, shift, 0), 1.0)
            b_sh = jnp.where(ok, pltpu.roll(b, shift, 0), 0.0)
            b = a * b_sh + b
            a = a * a_sh
        return a, b

    def fwd_step(c, h):
        t0 = pl.multiple_of(c * SCAN_ROWS, SCAN_ROWS)
        a, b = local_scan(a_s[0, pl.ds(t0, SCAN_ROWS), :], b_s[0, pl.ds(t0, SCAN_ROWS), :], False)
        outs = []
        for j in range(tiles):
            hj = a[j * SUBLANE:(j + 1) * SUBLANE, :] * h + b[j * SUBLANE:(j + 1) * SUBLANE, :]
            outs.append(hj)
            h = hj[SUBLANE - 1:SUBLANE, :]
        y_ref[pl.ds(t0, SCAN_ROWS), :] = jnp.concatenate(outs, axis=0)
        return h

    h_f = lax.fori_loop(0, n_steps, fwd_step, h0_ref[0:1, :])

    def bwd_step(c, h):
        t0 = pl.multiple_of((n_steps - 1 - c) * SCAN_ROWS, SCAN_ROWS)
        a, b = local_scan(a_s[1, pl.ds(t0, SCAN_ROWS), :], b_s[1, pl.ds(t0, SCAN_ROWS), :], True)
        outs = [None] * tiles
        for j in reversed(range(tiles)):
            hj = a[j * SUBLANE:(j + 1) * SUBLANE, :] * h + b[j * SUBLANE:(j + 1) * SUBLANE, :]
            outs[j] = hj
            h = hj[0:1, :]
        hb = jnp.concatenate(outs, axis=0)
        y_ref[pl.ds(t0, SCAN_ROWS), :] = (y_ref[pl.ds(t0, SCAN_ROWS), :] + hb) * _gelu_tanh(gr_ref[pl.ds(t0, SCAN_ROWS), :])
        return h

    h_b = lax.fori_loop(0, n_steps, bwd_step, h0_ref[1:2, :])
    hfin_ref[0:1, :] = h_f
    hfin_ref[1:2, :] = h_b


def _rglru(main, y_prev, cw, cb, wg, bg, lam, h0, *, n_seq, seq, row_block0):
    n_ct = RG_W // LANE
    gr_col0 = RG_W // LANE
    kern = functools.partial(_rglru_kernel, seq=seq)
    return pl.pallas_call(
        kern,
        out_shape=(jax.ShapeDtypeStruct((N_TOK, RG_W), F32), jax.ShapeDtypeStruct((n_seq, 2, RG_W), F32)),
        grid=(n_seq, n_ct),
        in_specs=[
            pl.BlockSpec((seq, LANE), lambda b, c: (row_block0 + b, c)),
            pl.BlockSpec((seq, LANE), lambda b, c: (row_block0 + b, gr_col0 + c)),
            pl.BlockSpec((CONV_W, LANE), lambda b, c: (0, c)),
            pl.BlockSpec((1, LANE), lambda b, c: (0, c)),
            pl.BlockSpec((None, LANE, 4 * LANE), lambda b, c: (c, 0, 0)),
            pl.BlockSpec((None, 1, 4 * LANE), lambda b, c: (c, 0, 0)),
            pl.BlockSpec((2, LANE), lambda b, c: (0, c)),
            pl.BlockSpec((None, 2, LANE), lambda b, c: (b, 0, c)),
            pl.BlockSpec(memory_space=pl.ANY),
        ],
        out_specs=(
            pl.BlockSpec((seq, LANE), lambda b, c: (row_block0 + b, c)),
            pl.BlockSpec((None, 2, LANE), lambda b, c: (b, 0, c)),
        ),
        scratch_shapes=[
            pltpu.VMEM((seq + 2 * SUBLANE, LANE), F32),
            pltpu.VMEM((2, seq, LANE), F32),
            pltpu.VMEM((2, seq, LANE), F32),
        ],
        input_output_aliases={8: 0},
        compiler_params=_cparams(("arbitrary", "arbitrary")),
        name=f"rglru_s{seq}",
    )(main, main, cw, cb, wg, bg, lam, h0, y_prev)


def _rope_lanes(x, cos, sin_m, sin_p):
    n = x.shape[1] // LANE
    half = QK_ROPE // 4
    cos_t = jnp.concatenate([cos] * n, axis=1) if n > 1 else cos
    sm_t = jnp.concatenate([sin_m] * n, axis=1) if n > 1 else sin_m
    sp_t = jnp.concatenate([sin_p] * n, axis=1) if n > 1 else sin_p
    up = pltpu.roll(x, x.shape[1] - half, 1)
    dn = pltpu.roll(x, half, 1)
    return x * cos_t + up * sm_t + dn * sp_t


def _mla_prep_kernel(cq0_ref, cq1_ref, cq2_ref, ckv_ref, kr_ref, cos_ref, sm_ref, sp_ref,
                     qn_ref, wq_ref, kvn_ref, wuk_ref, wuv_ref,
                     q_ref, k_ref, v_ref, ckvn_ref):
    cq = [cq0_ref[...], cq1_ref[...], cq2_ref[...]]
    ms = (jnp.sum(cq[0] * cq[0], axis=-1, keepdims=True) + jnp.sum(cq[1] * cq[1], axis=-1, keepdims=True)
          + jnp.sum(cq[2] * cq[2], axis=-1, keepdims=True)) * (1.0 / Q_LORA)
    inv = lax.rsqrt(ms + EPS)
    blk = Q_LORA // 3
    q = None
    for j in range(3):
        cqn = ((cq[j] * inv) * qn_ref[:, j * blk:(j + 1) * blk]).astype(BF16)
        part = _dot(cqn, wq_ref[j * blk:(j + 1) * blk, :])
        q = part if q is None else q + part
    cos, sm, sp = cos_ref[...], sm_ref[...], sp_ref[...]
    q_ref[...] = _rope_lanes(q, cos, sm, sp).astype(BF16)

    ckv = ckv_ref[...]
    inv_kv = lax.rsqrt(jnp.mean(ckv * ckv, axis=-1, keepdims=True) + EPS)
    ckvn = (ckv * inv_kv) * kvn_ref[...]
    ckvn_ref[...] = ckvn
    ckvn_b = ckvn.astype(BF16)
    kr_rot = _rope_lanes(kr_ref[...], cos, sm, sp)
    k_ref[...] = (_dot(ckvn_b, wuk_ref[...]) + jnp.concatenate([kr_rot] * MLA_HEADS, axis=1)).astype(BF16)
    v_ref[...] = _dot(ckvn_b, wuv_ref[...]).astype(BF16)


def _mla_prep(main, krp, cos_t, sm_t, sp_t, q_norm, wq_p, kv_norm, wuk_p, wuv):
    cq_col0 = 2 * RG_W // 256
    hp = MLA_HEADS * HEAD_PAD
    full = lambda shape: pl.BlockSpec(shape, lambda i: (0,) * len(shape))
    tab = pl.BlockSpec((TM, LANE), lambda i: (_pos_block(i, TM), 0))
    return pl.pallas_call(
        _mla_prep_kernel,
        out_shape=(
            jax.ShapeDtypeStruct((N_TOK, hp), BF16),
            jax.ShapeDtypeStruct((N_TOK, hp), BF16),
            jax.ShapeDtypeStruct((N_TOK, MLA_HEADS * V_HEAD), BF16),
            jax.ShapeDtypeStruct((N_TOK, KV_LORA), F32),
        ),
        grid=(N_TOK // TM,),
        in_specs=[
            pl.BlockSpec((TM, 256), lambda i: (i, cq_col0)),
            pl.BlockSpec((TM, 256), lambda i: (i, cq_col0 + 1)),
            pl.BlockSpec((TM, 256), lambda i: (i, cq_col0 + 2)),
            pl.BlockSpec((TM, 256), lambda i: (i, cq_col0 + 3)),
            pl.BlockSpec((TM, LANE), lambda i: (i, 0)),
            tab, tab, tab,
            full((1, Q_LORA)), full((Q_LORA, hp)), full((1, KV_LORA)), full((KV_LORA, hp)),
            full((KV_LORA, MLA_HEADS * V_HEAD)),
        ],
        out_specs=(
            pl.BlockSpec((TM, hp), lambda i: (i, 0)),
            pl.BlockSpec((TM, hp), lambda i: (i, 0)),
            pl.BlockSpec((TM, MLA_HEADS * V_HEAD), lambda i: (i, 0)),
            pl.BlockSpec((TM, KV_LORA), lambda i: (i, 0)),
        ),
        compiler_params=_cparams(("arbitrary",)),
        name="mla_prep",
    )(main, main, main, main, krp, cos_t, sm_t, sp_t, q_norm, wq_p, kv_norm, wuk_p, wuv)


def _mla_ctx_kernel(ckv_ref, kr_ref, wuk_ref, wuv_ref, k_ref, v_ref):
    ckv_b = ckv_ref[...].astype(BF16)
    k_ref[...] = (_dot(ckv_b, wuk_ref[...]) + jnp.concatenate([kr_ref[...]] * MLA_HEADS, axis=1)).astype(BF16)
    v_ref[...] = _dot(ckv_b, wuv_ref[...]).astype(BF16)


def _mla_ctx(ctx_ckv, ctx_krp, wuk_p, wuv):
    n = ctx_ckv.shape[0]
    hp = MLA_HEADS * HEAD_PAD
    tm = 512
    full = lambda shape: pl.BlockSpec(shape, lambda i: (0,) * len(shape))
    return pl.pallas_call(
        _mla_ctx_kernel,
        out_shape=(jax.ShapeDtypeStruct((n, hp), BF16), jax.ShapeDtypeStruct((n, MLA_HEADS * V_HEAD), BF16)),
        grid=(n // tm,),
        in_specs=[
            pl.BlockSpec((tm, KV_LORA), lambda i: (i, 0)),
            pl.BlockSpec((tm, LANE), lambda i: (i, 0)),
            full((KV_LORA, hp)), full((KV_LORA, MLA_HEADS * V_HEAD)),
        ],
        out_specs=(pl.BlockSpec((tm, hp), lambda i: (i, 0)), pl.BlockSpec((tm, MLA_HEADS * V_HEAD), lambda i: (i, 0))),
        compiler_params=_cparams(("arbitrary",)),
        name="mla_ctx",
    )(ctx_ckv, ctx_krp, wuk_p, wuv)


def _attn_kernel(*refs, seq, tq, tk, n_ctx):
    if n_ctx:
        q_ref, k_ref, v_ref, kc_ref, vc_ref, oprev_ref, o_ref = refs
    else:
        q_ref, k_ref, v_ref, oprev_ref, o_ref = refs
    del oprev_ref
    for h in range(2):
        q = q_ref[:, h * HEAD_PAD:(h + 1) * HEAD_PAD]

        def update(carry, k, v):
            m, l, acc = carry
            s = _dot_nt(q, k)
            m_new = jnp.maximum(m, jnp.max(s, axis=-1, keepdims=True))
            alpha = jnp.exp((m - m_new) * ATTN_SCALE)
            p = jnp.exp((s - m_new) * ATTN_SCALE)
            l = alpha * l + jnp.sum(p, axis=-1, keepdims=True)
            acc = alpha * acc + _dot(p.astype(BF16), v)
            return m_new, l, acc

        carry = (jnp.full((tq, 1), NEG, F32), jnp.zeros((tq, 1), F32), jnp.zeros((tq, V_HEAD), F32))
        if n_ctx:
            carry = update(carry, kc_ref[:, h * HEAD_PAD:(h + 1) * HEAD_PAD], vc_ref[:, h * V_HEAD:(h + 1) * V_HEAD])

        def step(c, carry):
            t0 = pl.multiple_of(c * tk, tk)
            return update(carry, k_ref[pl.ds(t0, tk), h * HEAD_PAD:(h + 1) * HEAD_PAD],
                          v_ref[pl.ds(t0, tk), h * V_HEAD:(h + 1) * V_HEAD])

        m, l, acc = lax.fori_loop(0, seq // tk, step, carry)
        o_ref[:, h * V_HEAD:(h + 1) * V_HEAD] = (acc / l).astype(o_ref.dtype)


def _attention(q, k, v, o_prev, kc, vc, *, n_seq, seq, row_block0, tq, tk):
    n_ctx = 0 if kc is None else PAST_LEN
    n_hp = MLA_HEADS // 2
    nq = seq // tq
    kern = functools.partial(_attn_kernel, seq=seq, tq=tq, tk=tk, n_ctx=n_ctx)
    in_specs = [
        pl.BlockSpec((tq, 2 * HEAD_PAD), lambda b, j, i: ((row_block0 + b) * nq + i, j)),
        pl.BlockSpec((seq, 2 * HEAD_PAD), lambda b, j, i: (row_block0 + b, j)),
        pl.BlockSpec((seq, 2 * V_HEAD), lambda b, j, i: (row_block0 + b, j)),
    ]
    args = [q, k, v]
    if n_ctx:
        in_specs += [
            pl.BlockSpec((n_ctx, 2 * HEAD_PAD), lambda b, j, i: (b, j)),
            pl.BlockSpec((n_ctx, 2 * V_HEAD), lambda b, j, i: (b, j)),
        ]
        args += [kc, vc]
    in_specs.append(pl.BlockSpec(memory_space=pl.ANY))
    args.append(o_prev)
    return pl.pallas_call(
        kern,
        out_shape=jax.ShapeDtypeStruct((N_TOK, MLA_HEADS * V_HEAD), BF16),
        grid=(n_seq, n_hp, nq),
        in_specs=in_specs,
        out_specs=pl.BlockSpec((tq, 2 * V_HEAD), lambda b, j, i: ((row_block0 + b) * nq + i, j)),
        input_output_aliases={len(args) - 1: 0},
        compiler_params=_cparams(("arbitrary", "arbitrary", "arbitrary")),
        name=f"mla_attention_s{seq}",
    )(*args)


def _post_mixer(y, x_ref, mod_ref, lng_ref, lnb_ref, wrh_ref, wrl_ref, x1_ref, h2_ref, lgt_ref):
    z = ALPHA * x_ref[...] + mod_ref[2:3, :] * y
    x1 = _layernorm_rows(z, lng_ref[...], lnb_ref[...])
    x1_ref[...] = x1
    h2 = _modulated(x1, mod_ref, 3, 4)
    h2_ref[...] = h2
    h_hi, h_lo = _split_hi_lo(h2)
    w_hi, w_lo = wrh_ref[...], wrl_ref[...]
    lgt_ref[...] = _dot_nt(w_hi, h_hi) + _dot_nt(w_hi, h_lo) + _dot_nt(w_lo, h_hi)


def _out_ab_kernel(yrg_ref, o_ref, wa_ref, wb_ref, x_ref, mod_ref, lng_ref, lnb_ref, wrh_ref, wrl_ref,
                   x1_ref, h2_ref, lgt_ref):
    y = _dot(yrg_ref[...].astype(BF16), wa_ref[...]) + _dot(o_ref[...], wb_ref[...])
    _post_mixer(y, x_ref, mod_ref, lng_ref, lnb_ref, wrh_ref, wrl_ref, x1_ref, h2_ref, lgt_ref)


def _post_specs():
    full = lambda shape: pl.BlockSpec(shape, lambda i: (0,) * len(shape))
    in_specs = [
        pl.BlockSpec((TM, D_MODEL), lambda i: (i, 0)),
        pl.BlockSpec((None, MOD_ROWS, D_MODEL), lambda i: (_cond_block(i, TM), 0, 0)),
        full((1, D_MODEL)), full((1, D_MODEL)),
        full((N_EXPERTS, D_MODEL)), full((N_EXPERTS, D_MODEL)),
    ]
    out_shape = (
        jax.ShapeDtypeStruct((N_TOK, D_MODEL), F32),
        jax.ShapeDtypeStruct((N_TOK, D_MODEL), F32),
        jax.ShapeDtypeStruct((N_EXPERTS, N_TOK), F32),
    )
    out_specs = (
        pl.BlockSpec((TM, D_MODEL), lambda i: (i, 0)),
        pl.BlockSpec((TM, D_MODEL), lambda i: (i, 0)),
        pl.BlockSpec((N_EXPERTS, TM), lambda i: (0, i)),
    )
    return in_specs, out_shape, out_specs


def _out_ab(yrg, o, w_a, w_b, x, mods, lng, lnb, wr_hi, wr_lo):
    full = lambda shape: pl.BlockSpec(shape, lambda i: (0,) * len(shape))
    post_in, out_shape, out_specs = _post_specs()
    return pl.pallas_call(
        _out_ab_kernel,
        out_shape=out_shape,
        grid=(N_TOK // TM,),
        in_specs=[
            pl.BlockSpec((TM, RG_W), lambda i: (i, 0)),
            pl.BlockSpec((TM, MLA_HEADS * V_HEAD), lambda i: (i, 0)),
            full((RG_W, D_MODEL)), full((MLA_HEADS * V_HEAD, D_MODEL)),
        ] + post_in,
        out_specs=out_specs,
        compiler_params=_cparams(("arbitrary",)),
        name="out_ab",
    )(yrg, o, w_a, w_b, x, mods, lng, lnb, wr_hi, wr_lo)


def _proj_c_kernel(x_ref, mod_ref, cos_ref, sin_ref, wq_ref, wk_ref, wv_ref, wg_ref, q_ref, k_ref, v_ref, g_ref):
    h = _modulated(x_ref[...], mod_ref, 0, 1).astype(BF16)
    cos, sin = cos_ref[...], sin_ref[...]
    half = RET_DK // 2
    for hd in range(RET_HEADS):
        for w_ref, o_ref, scale in ((wq_ref, q_ref, 1.0), (wk_ref, k_ref, RET_DK ** -0.5)):
            p = _dot(h, w_ref[:, hd * RET_DK:(hd + 1) * RET_DK])
            x1, x2 = p[:, :half], p[:, half:]
            r1 = x1 * cos - x2 * sin
            r2 = x2 * cos + x1 * sin
            if scale != 1.0:
                r1, r2 = r1 * scale, r2 * scale
            o_ref[:, hd * RET_DK:hd * RET_DK + half] = r1.astype(BF16)
            o_ref[:, hd * RET_DK + half:(hd + 1) * RET_DK] = r2.astype(BF16)
    step = 512
    for j in range(MIX_C // step):
        v_ref[:, j * step:(j + 1) * step] = _dot(h, wv_ref[:, j * step:(j + 1) * step]).astype(BF16)
        g_ref[:, j * step:(j + 1) * step] = _dot(h, wg_ref[:, j * step:(j + 1) * step])


def _proj_c(x, mods, cos_t, sin_t, wq, wk, wv, wg):
    full = lambda shape: pl.BlockSpec(shape, lambda i: (0,) * len(shape))
    qk = RET_HEADS * RET_DK
    tab = pl.BlockSpec((TM, RET_DK // 2), lambda i: (_pos_block(i, TM), 0))
    return pl.pallas_call(
        _proj_c_kernel,
        out_shape=(
            jax.ShapeDtypeStruct((N_TOK, qk), BF16), jax.ShapeDtypeStruct((N_TOK, qk), BF16),
            jax.ShapeDtypeStruct((N_TOK, MIX_C), BF16), jax.ShapeDtypeStruct((N_TOK, MIX_C), F32),
        ),
        grid=(N_TOK // TM,),
        in_specs=[
            pl.BlockSpec((TM, D_MODEL), lambda i: (i, 0)),
            pl.BlockSpec((None, MOD_ROWS, D_MODEL), lambda i: (_cond_block(i, TM), 0, 0)),
            tab, tab,
            full((D_MODEL, qk)), full((D_MODEL, qk)), full((D_MODEL, MIX_C)), full((D_MODEL, MIX_C)),
        ],
        out_specs=(
            pl.BlockSpec((TM, qk), lambda i: (i, 0)), pl.BlockSpec((TM, qk), lambda i: (i, 0)),
            pl.BlockSpec((TM, MIX_C), lambda i: (i, 0)), pl.BlockSpec((TM, MIX_C), lambda i: (i, 0)),
        ),
        compiler_params=_cparams(("arbitrary",), 56),
        name="proj_c",
    )(x, mods, cos_t, sin_t, wq, wk, wv, wg)


def _retention_kernel(*refs, seq, with_state):
    if with_state:
        q_ref, k_ref, v_ref, gam_ref, r0_ref, oprev_ref, o_ref, rfin_ref, r_s = refs
    else:
        q_ref, k_ref, v_ref, gam_ref, r0_ref, oprev_ref, o_ref, r_s = refs
        rfin_ref = None
    del oprev_ref
    c = RET_CHUNK
    n = seq // c
    ii = lax.broadcasted_iota(I32, (c, c), 0).astype(F32)
    jj = lax.broadcasted_iota(I32, (c, c), 1).astype(F32)
    ci = lax.broadcasted_iota(I32, (c, 1), 0).astype(F32)

    for d in range(2):
        gam = gam_ref[d]
        lg_row = -_softplus(-gam[0:1, :])
        lg = jnp.broadcast_to(lg_row, (c, c))
        lg_col = jnp.broadcast_to(lg_row[:, 0:1], (c, 1))
        if d == 0:
            diff = ii - jj
            xi = jnp.exp((ci + 1.0) * lg_col)
            zeta = jnp.exp((c - 1.0 - ci) * lg_col)
        else:
            diff = jj - ii
            xi = jnp.exp((c - ci) * lg_col)
            zeta = jnp.exp(ci * lg_col)
        dmat = jnp.where(diff >= 0, jnp.exp(jnp.maximum(diff, 0.0) * lg), 0.0)
        g_chunk = jnp.exp(float(c) * lg_row[:, 0:1])
        r_s[...] = r0_ref[d]

        def chunk(s, carry, d=d, dmat=dmat, xi=xi, zeta=zeta, g_chunk=g_chunk):
            idx = s if d == 0 else n - 1 - s
            t0 = pl.multiple_of(idx * c, c)
            qb = q_ref[pl.ds(t0, c), :]
            kb = k_ref[pl.ds(t0, c), :]
            vb = v_ref[pl.ds(t0, c), :]
            r = r_s[...]
            inner = _dot_nt(qb, kb) * dmat
            o = _dot(inner.astype(BF16), vb) + _dot((qb.astype(F32) * xi).astype(BF16), r.astype(BF16))
            r_s[...] = r * g_chunk + _dot_tn((kb.astype(F32) * zeta).astype(BF16), vb)
            if d == 0:
                o_ref[pl.ds(t0, c), :] = o
            else:
                o_ref[pl.ds(t0, c), :] = o_ref[pl.ds(t0, c), :] + o
            return carry

        lax.fori_loop(0, n, chunk, 0)
        if with_state:
            rfin_ref[d] = r_s[...]


def _retention(q, k, v, gam, r0, o_prev, *, n_seq, seq, row_block0, with_state):
    kern = functools.partial(_retention_kernel, seq=seq, with_state=with_state)
    out_shape = [jax.ShapeDtypeStruct((N_TOK, MIX_C), F32)]
    out_specs = [pl.BlockSpec((seq, RET_DV), lambda b, h: (row_block0 + b, h))]
    if with_state:
        out_shape.append(jax.ShapeDtypeStruct((n_seq, 2, RET_HEADS, RET_DK, RET_DV), F32))
        out_specs.append(pl.BlockSpec((None, 2, None, RET_DK, RET_DV), lambda b, h: (b, 0, h, 0, 0)))
    return pl.pallas_call(
        kern,
        out_shape=tuple(out_shape),
        grid=(n_seq, RET_HEADS),
        in_specs=[
            pl.BlockSpec((seq, RET_DK), lambda b, h: (row_block0 + b, h)),
            pl.BlockSpec((seq, RET_DK), lambda b, h: (row_block0 + b, h)),
            pl.BlockSpec((seq, RET_DV), lambda b, h: (row_block0 + b, h)),
            pl.BlockSpec((2, None, SUBLANE, LANE), lambda b, h: (0, h, 0, 0)),
            pl.BlockSpec((None, 2, None, RET_DK, RET_DV), lambda b, h: (b, 0, h, 0, 0)),
            pl.BlockSpec(memory_space=pl.ANY),
        ],
        out_specs=tuple(out_specs),
        scratch_shapes=[pltpu.VMEM((RET_DK, RET_DV), F32)],
        input_output_aliases={5: 0},
        compiler_params=_cparams(("arbitrary", "arbitrary"), 56),
        name=f"retention_s{seq}",
    )(q, k, v, gam, r0, o_prev)


def _out_c_kernel(o_ref, g_ref, w_ref, x_ref, mod_ref, lng_ref, lnb_ref, wrh_ref, wrl_ref,
                  x1_ref, h2_ref, lgt_ref):
    y = None
    for hd in range(RET_HEADS):
        o = o_ref[:, hd * RET_DV:(hd + 1) * RET_DV]
        mu = jnp.mean(o, axis=-1, keepdims=True)
        oc = o - mu
        var = jnp.mean(oc * oc, axis=-1, keepdims=True)
        on = oc * lax.rsqrt(var + EPS)
        a = (on * _silu(g_ref[:, hd * RET_DV:(hd + 1) * RET_DV])).astype(BF16)
        part = _dot(a, w_ref[hd * RET_DV:(hd + 1) * RET_DV, :])
        y = part if y is None else y + part
    _post_mixer(y, x_ref, mod_ref, lng_ref, lnb_ref, wrh_ref, wrl_ref, x1_ref, h2_ref, lgt_ref)


def _out_c(o, g, w, x, mods, lng, lnb, wr_hi, wr_lo):
    full = lambda shape: pl.BlockSpec(shape, lambda i: (0,) * len(shape))
    post_in, out_shape, out_specs = _post_specs()
    return pl.pallas_call(
        _out_c_kernel,
        out_shape=out_shape,
        grid=(N_TOK // TM,),
        in_specs=[
            pl.BlockSpec((TM, MIX_C), lambda i: (i, 0)),
            pl.BlockSpec((TM, MIX_C), lambda i: (i, 0)),
            full((MIX_C, D_MODEL)),
        ] + post_in,
        out_specs=out_specs,
        compiler_params=_cparams(("arbitrary",), 56),
        name="out_c",
    )(o, g, w, x, mods, lng, lnb, wr_hi, wr_lo)


def _route_kernel(lgt_ref, bias_ref, tri_ref, ids_ref, w_ref, pos_ref, cnt_ref, carry):
    @pl.when(pl.program_id(0) == 0)
    def _():
        carry[...] = jnp.zeros_like(carry)

    tt = lgt_ref.shape[1]
    scores = jax.nn.sigmoid(lgt_ref[...])
    sel = scores + bias_ref[...]
    srow = lax.broadcasted_iota(I32, (GROUP_SIZE, tt), 0).astype(F32)
    ninf = -jnp.inf

    gs = []
    for g in range(N_GROUPS):
        sg = sel[g * GROUP_SIZE:(g + 1) * GROUP_SIZE, :]
        m1 = jnp.max(sg, axis=0, keepdims=True)
        i1 = jnp.min(jnp.where(sg == m1, srow, float(GROUP_SIZE)), axis=0, keepdims=True)
        m2 = jnp.max(jnp.where(srow == i1, ninf, sg), axis=0, keepdims=True)
        gs.append(m1 + m2)
    gs = jnp.concatenate(gs, axis=0)
    chosen = jnp.zeros((N_GROUPS, tt), F32)
    for _ in range(TOPK_GROUPS):
        mg = jnp.max(gs, axis=0, keepdims=True)
        gi = jnp.min(jnp.where(gs == mg, srow, float(N_GROUPS)), axis=0, keepdims=True)
        hit = srow == gi
        chosen = jnp.where(hit, 1.0, chosen)
        gs = jnp.where(hit, ninf, gs)
    sel = jnp.concatenate(
        [jnp.where(jnp.broadcast_to(chosen[g:g + 1, :], (GROUP_SIZE, tt)) > 0.5,
                   sel[g * GROUP_SIZE:(g + 1) * GROUP_SIZE, :], ninf) for g in range(N_GROUPS)], axis=0)

    erow = lax.broadcasted_iota(I32, (N_EXPERTS, tt), 0).astype(F32)
    ids, ws = [], []
    for _ in range(TOP_K):
        m = jnp.max(sel, axis=0, keepdims=True)
        ei = jnp.min(jnp.where(sel == m, erow, float(N_EXPERTS)), axis=0, keepdims=True)
        hit = erow == ei
        ids.append(ei)
        ws.append(jnp.sum(jnp.where(hit, scores, 0.0), axis=0, keepdims=True))
        sel = jnp.where(hit, ninf, sel)
    wsum = ws[0]
    for k in range(1, TOP_K):
        wsum = wsum + ws[k]
    ids_ref[...] = jnp.concatenate(ids, axis=0).astype(I32)
    w_ref[...] = jnp.concatenate([w / wsum * ROUTED_SCALE for w in ws], axis=0)

    member_f = jnp.zeros((N_EXPERTS, tt), F32)
    for k in range(TOP_K):
        member_f = jnp.where(erow == ids[k], 1.0, member_f)
    rank = _dot(member_f.astype(BF16), tri_ref[...]) + carry[:, 0:1]
    pos_ref[...] = jnp.concatenate(
        [jnp.sum(jnp.where(erow == ids[k], rank, 0.0), axis=0, keepdims=True) for k in range(TOP_K)],
        axis=0).astype(I32)
    total = carry[...] + jnp.sum(member_f, axis=1, keepdims=True)
    carry[...] = total
    cnt_ref[...] = total


def _route(lgt, bias, tri):
    nb = N_TOK // TT
    return pl.pallas_call(
        _route_kernel,
        out_shape=(
            jax.ShapeDtypeStruct((TOP_K, N_TOK), I32), jax.ShapeDtypeStruct((TOP_K, N_TOK), F32),
            jax.ShapeDtypeStruct((TOP_K, N_TOK), I32), jax.ShapeDtypeStruct((N_EXPERTS, LANE), F32),
        ),
        grid=(nb,),
        in_specs=[
            pl.BlockSpec((N_EXPERTS, TT), lambda i: (0, i)),
            pl.BlockSpec((N_EXPERTS, 1), lambda i: (0, 0)),
            pl.BlockSpec((TT, TT), lambda i: (0, 0)),
        ],
        out_specs=(
            pl.BlockSpec((TOP_K, TT), lambda i: (0, i)), pl.BlockSpec((TOP_K, TT), lambda i: (0, i)),
            pl.BlockSpec((TOP_K, TT), lambda i: (0, i)), pl.BlockSpec((N_EXPERTS, LANE), lambda i: (0, 0)),
        ),
        scratch_shapes=[pltpu.VMEM((N_EXPERTS, LANE), F32)],
        compiler_params=_cparams(("arbitrary",)),
        name="moe_route",
    )(lgt, bias, tri)


def _dest_kernel(start_ref, ids_ref, pos_ref, dest_ref):
    ids = ids_ref[...]
    dest = pos_ref[...]
    for e in range(N_EXPERTS):
        dest = dest + jnp.where(ids == e, start_ref[e], 0)
    dest_ref[...] = dest


def _dest_rows(start, ids, pos):
    tt = 2048
    return pl.pallas_call(
        _dest_kernel,
        out_shape=jax.ShapeDtypeStruct((TOP_K, N_TOK), I32),
        grid_spec=pltpu.PrefetchScalarGridSpec(
            num_scalar_prefetch=1,
            grid=(N_TOK // tt,),
            in_specs=[pl.BlockSpec((TOP_K, tt), lambda i, s: (0, i)), pl.BlockSpec((TOP_K, tt), lambda i, s: (0, i))],
            out_specs=pl.BlockSpec((TOP_K, tt), lambda i, s: (0, i)),
        ),
        compiler_params=_cparams(("arbitrary",)),
        name="moe_dest",
    )(start, ids, pos)


def _dispatch_kernel(dest_hbm, h_ref, xg_in, xg_out, dsm, sem_idx, sem_row):
    del xg_in
    i = pl.program_id(0)
    tt = h_ref.shape[0]
    cp = pltpu.make_async_copy(dest_hbm.at[:, pl.ds(pl.multiple_of(i * tt, tt), tt)], dsm, sem_idx)
    cp.start()
    cp.wait()

    def issue(t, carry):
        for k in range(TOP_K):
            pltpu.make_async_copy(h_ref.at[pl.ds(t, 1)], xg_out.at[pl.ds(dsm[k, t], 1)], sem_row).start()
        return carry

    lax.fori_loop(0, tt, issue, 0)

    def drain(t, carry):
        for k in range(TOP_K):
            pltpu.make_async_copy(h_ref.at[pl.ds(0, 1)], xg_out.at[pl.ds(0, 1)], sem_row).wait()
        return carry

    lax.fori_loop(0, tt, drain, 0)


def _dispatch(dest, h2, xg_zero):
    tt = 256
    return pl.pallas_call(
        _dispatch_kernel,
        out_shape=jax.ShapeDtypeStruct((N_EROWS, D_MODEL), F32),
        grid=(N_TOK // tt,),
        in_specs=[
            pl.BlockSpec(memory_space=pl.ANY),
            pl.BlockSpec((tt, D_MODEL), lambda i: (i, 0)),
            pl.BlockSpec(memory_space=pl.ANY),
        ],
        out_specs=pl.BlockSpec(memory_space=pl.ANY),
        scratch_shapes=[pltpu.SMEM((TOP_K, tt), I32), pltpu.SemaphoreType.DMA, pltpu.SemaphoreType.DMA],
        input_output_aliases={2: 0},
        compiler_params=_cparams(("arbitrary",)),
        name="moe_dispatch",
    )(dest, h2, xg_zero)


def _expert_kernel(be_ref, nb_ref, x_ref, wg_ref, wu_ref, wd_ref, y_ref):
    @pl.when(pl.program_id(0) < nb_ref[0])
    def _():
        x = x_ref[...].astype(BF16)
        hb = _silu(_dot(x, wg_ref[...])) * _dot(x, wu_ref[...])
        y_ref[...] = _dot(hb.astype(BF16), wd_ref[...])

    @pl.when(pl.program_id(0) >= nb_ref[0])
    def _():
        y_ref[...] = jnp.zeros_like(y_ref)


def _experts(block_e, n_used, xg, wg, wu, wd):
    def row_map(i, be, nb):
        return (jnp.minimum(i, nb[0] - 1), 0)

    def w_map(i, be, nb):
        return (be[jnp.minimum(i, nb[0] - 1)], 0, 0)

    return pl.pallas_call(
        _expert_kernel,
        out_shape=jax.ShapeDtypeStruct((N_EROWS, D_MODEL), F32),
        grid_spec=pltpu.PrefetchScalarGridSpec(
            num_scalar_prefetch=2,
            grid=(N_EBLOCKS,),
            in_specs=[
                pl.BlockSpec((TME, D_MODEL), row_map),
                pl.BlockSpec((None, D_MODEL, D_EXPERT), w_map),
                pl.BlockSpec((None, D_MODEL, D_EXPERT), w_map),
                pl.BlockSpec((None, D_EXPERT, D_MODEL), w_map),
            ],
            out_specs=pl.BlockSpec((TME, D_MODEL), lambda i, be, nb: (i, 0)),
        ),
        compiler_params=_cparams(("arbitrary",)),
        name="moe_experts",
    )(block_e, n_used, xg, wg, wu, wd)


def _combine_kernel(dest_hbm, yb_hbm, wt_ref, h_ref, x1_ref, mod_ref, lng_ref, lnb_ref,
                    wsg_ref, wsu_ref, wsd_ref, out_ref, dsm, buf, sem_idx, sem_row):
    i = pl.program_id(0)
    tt = h_ref.shape[0]
    cp = pltpu.make_async_copy(dest_hbm.at[:, pl.ds(pl.multiple_of(i * tt, tt), tt)], dsm, sem_idx)
    cp.start()
    cp.wait()

    def issue(t, carry):
        for k in range(TOP_K):
            pltpu.make_async_copy(yb_hbm.at[pl.ds(dsm[k, t], 1)], buf.at[k, pl.ds(t, 1)], sem_row).start()
        return carry

    lax.fori_loop(0, tt, issue, 0)

    hb = h_ref[...].astype(BF16)
    shared = _dot((_silu(_dot(hb, wsg_ref[...])) * _dot(hb, wsu_ref[...])).astype(BF16), wsd_ref[...])

    def drain(t, carry):
        for k in range(TOP_K):
            pltpu.make_async_copy(yb_hbm.at[pl.ds(0, 1)], buf.at[0, pl.ds(0, 1)], sem_row).wait()
        return carry

    lax.fori_loop(0, tt, drain, 0)

    wt = wt_ref[...]
    routed = buf[0] * wt[:, 0:1]
    for k in range(1, TOP_K):
        routed = routed + buf[k] * wt[:, k:k + 1]
    z = ALPHA * x1_ref[...] + mod_ref[5:6, :] * (routed + shared)
    out_ref[...] = _layernorm_rows(z, lng_ref[...], lnb_ref[...])


def _combine(dest, yb, wt, h2, x1, mods, lng, lnb, wsg, wsu, wsd):
    tt = TC_TOK
    full = lambda shape: pl.BlockSpec(shape, lambda i: (0,) * len(shape))
    return pl.pallas_call(
        _combine_kernel,
        out_shape=jax.ShapeDtypeStruct((N_TOK, D_MODEL), F32),
        grid=(N_TOK // tt,),
        in_specs=[
            pl.BlockSpec(memory_space=pl.ANY),
            pl.BlockSpec(memory_space=pl.ANY),
            pl.BlockSpec((tt, TOP_K), lambda i: (i, 0)),
            pl.BlockSpec((tt, D_MODEL), lambda i: (i, 0)),
            pl.BlockSpec((tt, D_MODEL), lambda i: (i, 0)),
            pl.BlockSpec((None, MOD_ROWS, D_MODEL), lambda i: (_cond_block(i, tt), 0, 0)),
            full((1, D_MODEL)), full((1, D_MODEL)),
            full((D_MODEL, D_EXPERT)), full((D_MODEL, D_EXPERT)), full((D_EXPERT, D_MODEL)),
        ],
        out_specs=pl.BlockSpec((tt, D_MODEL), lambda i: (i, 0)),
        scratch_shapes=[
            pltpu.SMEM((TOP_K, tt), I32), pltpu.VMEM((TOP_K, tt, D_MODEL), F32),
            pltpu.SemaphoreType.DMA, pltpu.SemaphoreType.DMA,
        ],
        compiler_params=_cparams(("arbitrary",)),
        name="moe_combine",
    )(dest, yb, wt, h2, x1, mods, lng, lnb, wsg, wsu, wsd)


def _moe_and_norm(x1, h2, lgt, mods, lng, lnb, router_bias, tri, wg, wu, wd, wsg, wsu, wsd):
    ids, wts, pos, cnt = _route(lgt, router_bias.reshape(N_EXPERTS, 1), tri)
    counts = cnt[:, 0].astype(I32)
    padded = (counts + TME - 1) // TME * TME
    pad_end = jnp.cumsum(padded)
    start = pad_end - padded
    dest = _dest_rows(start.astype(I32), ids, pos)
    n_used = (pad_end[-1] // TME).astype(I32).reshape(1)
    block_e = jnp.minimum(
        jnp.searchsorted(pad_end, jnp.arange(N_EBLOCKS, dtype=I32) * TME, side="right"), N_EXPERTS - 1).astype(I32)
    xg = _dispatch(dest, h2, jnp.zeros((N_EROWS, D_MODEL), F32))
    yb = _experts(block_e, n_used, xg, wg, wu, wd)
    return _combine(dest, yb, wts.T, h2, x1, mods, lng, lnb, wsg, wsu, wsd)


def _rope_tables_mla():
    t = jnp.arange(DEC_SEQ)
    row = (t // GRID_W).astype(F32)
    col = (t % GRID_W).astype(F32)
    n = QK_ROPE // 4
    inv = ROPE_BASE ** (-jnp.arange(n, dtype=F32) / n)
    ang_r = row[:, None] * inv
    ang_c = col[:, None] * inv
    cos = jnp.ones((DEC_SEQ, LANE), F32)
    sin_m = jnp.zeros((DEC_SEQ, LANE), F32)
    sin_p = jnp.zeros((DEC_SEQ, LANE), F32)
    l0 = ROPE_LANE0
    for base, ang in ((l0, ang_r), (l0 + 2 * n, ang_c)):
        c, s = jnp.cos(ang), jnp.sin(ang)
        cos = cos.at[:, base:base + n].set(c).at[:, base + n:base + 2 * n].set(c)
        sin_m = sin_m.at[:, base:base + n].set(-s)
        sin_p = sin_p.at[:, base + n:base + 2 * n].set(s)
    ident = (jnp.ones((TM, LANE), F32), jnp.zeros((TM, LANE), F32), jnp.zeros((TM, LANE), F32))
    return tuple(jnp.concatenate([i, tbl], axis=0) for i, tbl in zip(ident, (cos, sin_m, sin_p)))


def _rope_tables_ret():
    half = RET_DK // 2
    theta = ROPE_BASE ** (-jnp.linspace(0.0, 1.0, half, dtype=F32))
    ang = jnp.arange(DEC_SEQ, dtype=F32)[:, None] * theta
    cos = jnp.concatenate([jnp.ones((TM, half), F32), jnp.cos(ang)], axis=0)
    sin = jnp.concatenate([jnp.zeros((TM, half), F32), jnp.sin(ang)], axis=0)
    return cos, sin


def _pad_heads(w, width, lane0=0):
    k = w.shape[0]
    w = w.reshape(k, MLA_HEADS, width)
    out = jnp.zeros((k, MLA_HEADS, HEAD_PAD), w.dtype).at[:, :, lane0:lane0 + width].set(w)
    return out.reshape(k, MLA_HEADS * HEAD_PAD)


def _rg_gate_weights(wa, ba, wx, bx):
    n_ct = RG_W // LANE
    per = LANE // RG_BW
    tiles_w, tiles_b = [], []
    for c in range(n_ct):
        cols_w, cols_b = [], []
        for d in range(2):
            for w, b in ((wa, ba), (wx, bx)):
                m = jnp.zeros((LANE, LANE), F32)
                for p in range(per):
                    m = m.at[p * RG_BW:(p + 1) * RG_BW, p * RG_BW:(p + 1) * RG_BW].set(w[d, c * per + p])
                cols_w.append(m)
                cols_b.append(b[d, c * LANE:(c + 1) * LANE])
        tiles_w.append(jnp.concatenate(cols_w, axis=1))
        tiles_b.append(jnp.concatenate(cols_b, axis=0)[None, :])
    return jnp.stack(tiles_w).astype(BF16), jnp.stack(tiles_b)


def kernel(x_prompt, x_sample, cache_mla_ckv, cache_mla_krope, state_rglru, state_ret, c, c_ctx, w_ada, b_ada,
           ln_g, ln_b, w_in_ab, rg_conv_w, rg_conv_b, rg_wa, rg_ba, rg_wx, rg_bx, rg_lambda, mla_q_norm, mla_w_uq,
           mla_kv_norm, mla_w_ukv, w_out_ab, w_in_c, ret_gamma_logit, w_out_c, w_router, router_bias,
           w_exp_gate, w_exp_up, w_exp_down, w_sh_gate, w_sh_up, w_sh_down):
    x = jnp.concatenate([x_prompt.reshape(N_PROMPT, D_MODEL), x_sample.reshape(N_SAMPLE, D_MODEL)], axis=0)
    cond = jnp.zeros((16, D_MODEL), F32).at[0].set(c_ctx).at[1:1 + DEC_BATCH].set(c)
    mods_all = _ada_modulation(cond, w_ada, b_ada).reshape(DEPTH, 16, 6, D_MODEL)[:, :N_COND]
    mods_all = jnp.pad(mods_all, ((0, 0), (0, 0), (0, MOD_ROWS - 6), (0, 0)))

    tri = (jnp.arange(TT)[:, None] < jnp.arange(TT)[None, :]).astype(BF16)
    wr_t = jnp.swapaxes(w_router, 1, 2)
    wr_hi = wr_t.astype(BF16)
    wr_lo = (wr_t - wr_hi.astype(F32)).astype(BF16)
    wg_e, wu_e, wd_e = w_exp_gate.astype(BF16), w_exp_up.astype(BF16), w_exp_down.astype(BF16)
    wsg, wsu, wsd = w_sh_gate.astype(BF16), w_sh_up.astype(BF16), w_sh_down.astype(BF16)

    l, e = 0, 0
    mods = mods_all[l]
    n_main = 2 * RG_W + Q_LORA + KV_LORA
    w_main = w_in_ab[e][:, :n_main].astype(BF16)
    w_kr = jnp.zeros((D_MODEL, LANE), F32).at[:, ROPE_LANE0:ROPE_LANE0 + QK_ROPE].set(w_in_ab[e][:, n_main:]).astype(BF16)
    main, krp = _proj_ab(x, mods, w_main, w_kr)

    wg_rg, bg_rg = _rg_gate_weights(rg_wa[e], rg_ba[e], rg_wx[e], rg_bx[e])
    h0_p = jnp.zeros((BATCH, 2, RG_W), F32)
    rg_args = (rg_conv_w[e], rg_conv_b[e].reshape(1, RG_W), wg_rg, bg_rg, rg_lambda[e])
    yrg, rg_fin = _rglru(main, jnp.zeros((N_TOK, RG_W), F32), *rg_args, h0_p,
                         n_seq=BATCH, seq=SEQ, row_block0=0)
    yrg, _ = _rglru(main, yrg, *rg_args, state_rglru[:, e],
                    n_seq=DEC_BATCH, seq=DEC_SEQ, row_block0=N_PROMPT // DEC_SEQ)

    cos_t, sm_t, sp_t = _rope_tables_mla()
    w_uq = mla_w_uq[e].reshape(Q_LORA, MLA_HEADS, QK_NOPE + QK_ROPE)
    wq_p = _pad_heads(w_uq.reshape(Q_LORA, -1), QK_NOPE + QK_ROPE).astype(BF16)
    w_ukv = mla_w_ukv[e].reshape(KV_LORA, MLA_HEADS, QK_NOPE + V_HEAD)
    wuk_p = _pad_heads(w_ukv[:, :, :QK_NOPE].reshape(KV_LORA, -1), QK_NOPE).astype(BF16)
    wuv = w_ukv[:, :, QK_NOPE:].reshape(KV_LORA, MLA_HEADS * V_HEAD).astype(BF16)
    q_att, k_att, v_att, ckv_n = _mla_prep(main, krp, cos_t, sm_t, sp_t, mla_q_norm[e].reshape(1, Q_LORA), wq_p,
                                           mla_kv_norm[e].reshape(1, KV_LORA), wuk_p, wuv)
    ctx_ckv = cache_mla_ckv[:, e].reshape(DEC_BATCH * PAST_LEN, KV_LORA)
    ctx_krp = jnp.zeros((DEC_BATCH * PAST_LEN, LANE), F32).at[:, ROPE_LANE0:ROPE_LANE0 + QK_ROPE].set(
        cache_mla_krope[:, e].reshape(DEC_BATCH * PAST_LEN, QK_ROPE))
    kc_att, vc_att = _mla_ctx(ctx_ckv, ctx_krp, wuk_p, wuv)

    o_att = _attention(q_att, k_att, v_att, jnp.zeros((N_TOK, MLA_HEADS * V_HEAD), BF16), None, None,
                       n_seq=BATCH, seq=SEQ, row_block0=0, tq=SEQ, tk=SEQ)
    o_att = _attention(q_att, k_att, v_att, o_att, kc_att, vc_att,
                       n_seq=DEC_BATCH, seq=DEC_SEQ, row_block0=N_PROMPT // DEC_SEQ, tq=TQ, tk=TK)

    w_out = w_out_ab[e].astype(BF16)
    x1, h2, lgt = _out_ab(yrg, o_att, w_out[:RG_W], w_out[RG_W:], x, mods,
                          ln_g[l, 0].reshape(1, D_MODEL), ln_b[l, 0].reshape(1, D_MODEL), wr_hi[l], wr_lo[l])
    x = _moe_and_norm(x1, h2, lgt, mods, ln_g[l, 1].reshape(1, D_MODEL), ln_b[l, 1].reshape(1, D_MODEL),
                      router_bias[l], tri, wg_e[l], wu_e[l], wd_e[l], wsg[l], wsu[l], wsd[l])

    new_ckv = ckv_n[:N_PROMPT].reshape(BATCH, 1, SEQ, KV_LORA)
    new_krope = krp[:N_PROMPT, ROPE_LANE0:ROPE_LANE0 + QK_ROPE].reshape(BATCH, 1, SEQ, QK_ROPE)
    new_rg = rg_fin.reshape(BATCH, 1, 2, RG_W)

    l, o = 1, 0
    mods = mods_all[l]
    qk = RET_HEADS * RET_DK
    w_c = w_in_c[o].astype(BF16)
    cos_r, sin_r = _rope_tables_ret()
    q_r, k_r, v_r, g_r = _proj_c(x, mods, cos_r, sin_r, w_c[:, :qk], w_c[:, qk:2 * qk],
                                 w_c[:, 2 * qk:2 * qk + MIX_C], w_c[:, 2 * qk + MIX_C:])
    gam = jnp.broadcast_to(ret_gamma_logit[o].astype(F32)[:, :, None, None], (2, RET_HEADS, SUBLANE, LANE))
    r0_p = jnp.zeros((BATCH, 2, RET_HEADS, RET_DK, RET_DV), F32)
    o_ret, r_fin = _retention(q_r, k_r, v_r, gam, r0_p, jnp.zeros((N_TOK, MIX_C), F32),
                              n_seq=BATCH, seq=SEQ, row_block0=0, with_state=True)
    (o_ret,) = _retention(q_r, k_r, v_r, gam, state_ret[:, o], o_ret,
                          n_seq=DEC_BATCH, seq=DEC_SEQ, row_block0=N_PROMPT // DEC_SEQ, with_state=False)
    x1, h2, lgt = _out_c(o_ret, g_r, w_out_c[o].astype(BF16), x, mods,
                         ln_g[l, 0].reshape(1, D_MODEL), ln_b[l, 0].reshape(1, D_MODEL), wr_hi[l], wr_lo[l])
    x = _moe_and_norm(x1, h2, lgt, mods, ln_g[l, 1].reshape(1, D_MODEL), ln_b[l, 1].reshape(1, D_MODEL),
                      router_bias[l], tri, wg_e[l], wu_e[l], wd_e[l], wsg[l], wsu[l], wsd[l])

    y_prompt = x[:N_PROMPT].reshape(BATCH, SEQ, D_MODEL)
    y_sample = x[N_PROMPT:].reshape(DEC_BATCH, DEC_SEQ, D_MODEL)
    new_ret = r_fin.reshape(BATCH, 1, 2, RET_HEADS, RET_DK, RET_DV)
    return (y_prompt, y_sample, new_ckv, new_krope, new_rg, new_ret)
```

```python
import functools
import math

import jax
import jax.numpy as jnp
from jax import lax
from jax.experimental import pallas as pl
from jax.experimental.pallas import tpu as pltpu

F32 = jnp.float32
BF16 = jnp.bfloat16
I32 = jnp.int32

D_MODEL = 1024
BATCH, SEQ = 16, 256
DEC_BATCH, DEC_SEQ = 8, 4096
PAST_LEN = 256
DEPTH = 2
GRID_W = 64
RG_W, RG_BLOCKS = 512, 8
RG_BW = RG_W // RG_BLOCKS
RG_C = 8.0
CONV_W, CONV_LEFT = 4, 2
MLA_HEADS, QK_NOPE, QK_ROPE, V_HEAD = 8, 64, 32, 64
Q_LORA, KV_LORA = 768, 256
ROPE_BASE = 10000.0
ATTN_SCALE = (QK_NOPE + QK_ROPE) ** -0.5
RET_HEADS, RET_DK, RET_DV, RET_CHUNK = 4, 256, 512, 128
MIX_C = RET_HEADS * RET_DV
N_EXPERTS, TOP_K, N_GROUPS, TOPK_GROUPS = 64, 8, 8, 4
GROUP_SIZE = N_EXPERTS // N_GROUPS
D_EXPERT = 256
ROUTED_SCALE = 2.5
ALPHA = (2 * DEPTH) ** 0.25
EPS = 1e-6

N_PROMPT = BATCH * SEQ
N_SAMPLE = DEC_BATCH * DEC_SEQ
N_TOK = N_PROMPT + N_SAMPLE
N_COND = 1 + DEC_BATCH
MOD_ROWS = 8

LANE = 128
SUBLANE = 8
TM = 512
HEAD_PAD = 128
ROPE_LANE0 = QK_NOPE
TQ = 256
KC = 256
SCAN_ROWS = 64
GATE_ROWS = 256
TT = 512
TME = 512
N_PAIRS = N_TOK * TOP_K
N_EBLOCKS = N_PAIRS // TME + N_EXPERTS
N_EROWS = N_EBLOCKS * TME
TC_TOK = 128
NEG = -1e30


def _cparams(sem, vmem_mb=48):
    return pltpu.CompilerParams(dimension_semantics=sem, vmem_limit_bytes=vmem_mb * 1024 * 1024)


def _cond_block(i, tm):
    npb = N_PROMPT // tm
    return jnp.where(i < npb, 0, 1 + (i - npb) // (DEC_SEQ // tm))


def _pos_block(i, tm):
    npb = N_PROMPT // tm
    return jnp.where(i < npb, 0, 1 + (i - npb) % (DEC_SEQ // tm))


def _split_hi_lo(a):
    hi = a.astype(BF16)
    lo = (a - hi.astype(F32)).astype(BF16)
    return hi, lo


def _dot(a, b):
    return jnp.dot(a, b, preferred_element_type=F32)


def _dot_nt(a, b):
    return lax.dot_general(a, b, (((1,), (1,)), ((), ())), preferred_element_type=F32)


def _dot_tn(a, b):
    return lax.dot_general(a, b, (((0,), (0,)), ((), ())), preferred_element_type=F32)


def _silu(x):
    return x * jax.nn.sigmoid(x)


def _gelu_tanh(x):
    return 0.5 * x * (1.0 + jnp.tanh(math.sqrt(2.0 / math.pi) * (x + 0.044715 * (x * x * x))))


def _softplus(x):
    return jnp.maximum(x, 0.0) + jnp.log1p(jnp.exp(-jnp.abs(x)))


def _layernorm_rows(z, g, b):
    mu = jnp.mean(z, axis=-1, keepdims=True)
    zc = z - mu
    var = jnp.mean(zc * zc, axis=-1, keepdims=True)
    return (zc * lax.rsqrt(var + EPS)) * g + b


def _ada_kernel(c_ref, w_ref, b_ref, o_ref):
    s_hi, s_lo = _split_hi_lo(_silu(c_ref[...]))
    w_hi, w_lo = _split_hi_lo(w_ref[...])
    o_ref[...] = _dot(s_hi, w_hi) + _dot(s_hi, w_lo) + _dot(s_lo, w_hi) + b_ref[...]


def _ada_modulation(cond, w_ada, b_ada):
    n6 = 6 * D_MODEL
    tn = D_MODEL
    return pl.pallas_call(
        _ada_kernel,
        out_shape=jax.ShapeDtypeStruct((DEPTH, 16, n6), F32),
        grid=(DEPTH, n6 // tn),
        in_specs=[
            pl.BlockSpec((16, D_MODEL), lambda l, j: (0, 0)),
            pl.BlockSpec((None, D_MODEL, tn), lambda l, j: (l, 0, j)),
            pl.BlockSpec((None, 1, tn), lambda l, j: (l, 0, j)),
        ],
        out_specs=pl.BlockSpec((None, 16, tn), lambda l, j: (l, 0, j)),
        compiler_params=_cparams(("arbitrary", "arbitrary")),
        name="ada_modulation",
    )(cond, w_ada, b_ada.reshape(DEPTH, 1, n6))


def _modulated(x, mod_ref, shift_row, scale_row):
    return x * (1.0 + mod_ref[scale_row:scale_row + 1, :]) + mod_ref[shift_row:shift_row + 1, :]


def _proj_ab_kernel(x_ref, mod_ref, w_ref, wkr_ref, main_ref, kr_ref):
    h = _modulated(x_ref[...], mod_ref, 0, 1).astype(BF16)
    n = w_ref.shape[1]
    step = 512
    for j in range(n // step):
        main_ref[:, j * step:(j + 1) * step] = _dot(h, w_ref[:, j * step:(j + 1) * step])
    kr_ref[...] = _dot(h, wkr_ref[...])


def _proj_ab(x, mods, w_main, w_kr):
    n = w_main.shape[1]
    return pl.pallas_call(
        _proj_ab_kernel,
        out_shape=(jax.ShapeDtypeStruct((N_TOK, n), F32), jax.ShapeDtypeStruct((N_TOK, LANE), F32)),
        grid=(N_TOK // TM,),
        in_specs=[
            pl.BlockSpec((TM, D_MODEL), lambda i: (i, 0)),
            pl.BlockSpec((None, MOD_ROWS, D_MODEL), lambda i: (_cond_block(i, TM), 0, 0)),
            pl.BlockSpec((D_MODEL, n), lambda i: (0, 0)),
            pl.BlockSpec((D_MODEL, LANE), lambda i: (0, 0)),
        ],
        out_specs=(pl.BlockSpec((TM, n), lambda i: (i, 0)), pl.BlockSpec((TM, LANE), lambda i: (i, 0))),
        compiler_params=_cparams(("arbitrary",)),
        name="proj_ab",
    )(x, mods, w_main, w_kr)


def _rglru_kernel(xr_ref, gr_ref, cw_ref, cb_ref, wg_ref, bg_ref, lam_ref, h0_ref, yprev_ref,
                  y_ref, hfin_ref, xpad, a_s, b_s, *, seq):
    del yprev_ref
    pad = SUBLANE
    xpad[0:pad, :] = jnp.zeros((pad, LANE), F32)
    xpad[seq + pad:seq + 2 * pad, :] = jnp.zeros((pad, LANE), F32)
    xpad[pad:seq + pad, :] = xr_ref[...]

    sp = _softplus(-lam_ref[...])
    cw = cw_ref[...]
    cb = cb_ref[...]
    wg = wg_ref[...]
    bg = bg_ref[...]

    def gate_step(c, carry):
        t0 = pl.multiple_of(c * GATE_ROWS, GATE_ROWS)
        win = xpad[pl.ds(t0, GATE_ROWS + 2 * pad), :]
        xc = cb
        for j in range(CONV_W):
            off = pad - CONV_LEFT + j
            xc = xc + win[off:off + GATE_ROWS, :] * cw[j:j + 1, :]
        g = _dot(xc.astype(BF16), wg) + bg
        for d in range(2):
            r = jax.nn.sigmoid(g[:, (2 * d) * LANE:(2 * d + 1) * LANE])
            i = jax.nn.sigmoid(g[:, (2 * d + 1) * LANE:(2 * d + 2) * LANE])
            log_a = (-RG_C * r) * sp[d:d + 1, :]
            a = jnp.exp(log_a)
            t = jnp.tanh(log_a)
            bt = jnp.sqrt(2.0 * t / (t - 1.0)) * (i * xc)
            a_s[d, pl.ds(t0, GATE_ROWS), :] = a
            b_s[d, pl.ds(t0, GATE_ROWS), :] = bt
        return carry

    lax.fori_loop(0, seq // GATE_ROWS, gate_step, 0)

    row = lax.broadcasted_iota(I32, (SCAN_ROWS, LANE), 0) % SUBLANE
    n_steps = seq // SCAN_ROWS
    tiles = SCAN_ROWS // SUBLANE

    def local_scan(a, b, reverse):
        for k in (1, 2, 4):
            if reverse:
                ok = row < SUBLANE - k
                shift = SCAN_ROWS - k
            else:
                ok = row >= k
                shift = k
            a_sh = jnp.where(ok, pltpu.roll(a, shift, 0), 1.0)
            b_sh = jnp.where(ok, pltpu.roll(b, shift, 0), 0.0)
            b = a * b_sh + b
            a = a * a_sh
        return a, b

    def fwd_step(c, h):
        t0 = pl.multiple_of(c * SCAN_ROWS, SCAN_ROWS)
        a, b = local_scan(a_s[0, pl.ds(t0, SCAN_ROWS), :], b_s[0, pl.ds(t0, SCAN_ROWS), :], False)
        outs = []
        for j in range(tiles):
            hj = a[j * SUBLANE:(j + 1) * SUBLANE, :] * h + b[j * SUBLANE:(j + 1) * SUBLANE, :]
            outs.append(hj)
            h = hj[SUBLANE - 1:SUBLANE, :]
        y_ref[pl.ds(t0, SCAN_ROWS), :] = jnp.concatenate(outs, axis=0)
        return h

    h_f = lax.fori_loop(0, n_steps, fwd_step, h0_ref[0:1, :])

    def bwd_step(c, h):
        t0 = pl.multiple_of((n_steps - 1 - c) * SCAN_ROWS, SCAN_ROWS)
        a, b = local_scan(a_s[1, pl.ds(t0, SCAN_ROWS), :], b_s[1, pl.ds(t0, SCAN_ROWS), :], True)
        outs = [None] * tiles
        for j in reversed(range(tiles)):
            hj = a[j * SUBLANE:(j + 1) * SUBLANE, :] * h + b[j * SUBLANE:(j + 1) * SUBLANE, :]
            outs[j] = hj
            h = hj[0:1, :]
        hb = jnp.concatenate(outs, axis=0)
        y_ref[pl.ds(t0, SCAN_ROWS), :] = (y_ref[pl.ds(t0, SCAN_ROWS), :] + hb) * _gelu_tanh(gr_ref[pl.ds(t0, SCAN_ROWS), :])
        return h

    h_b = lax.fori_loop(0, n_steps, bwd_step, h0_ref[1:2, :])
    hfin_ref[0:1, :] = h_f
    hfin_ref[1:2, :] = h_b


def _rglru(main, y_prev, cw, cb, wg, bg, lam, h0, *, n_seq, seq, row_block0):
    n_ct = RG_W // LANE
    gr_col0 = RG_W // LANE
    kern = functools.partial(_rglru_kernel, seq=seq)
    return pl.pallas_call(
        kern,
        out_shape=(jax.ShapeDtypeStruct((N_TOK, RG_W), F32), jax.ShapeDtypeStruct((n_seq, 2, RG_W), F32)),
        grid=(n_seq, n_ct),
        in_specs=[
            pl.BlockSpec((seq, LANE), lambda b, c: (row_block0 + b, c)),
            pl.BlockSpec((seq, LANE), lambda b, c: (row_block0 + b, gr_col0 + c)),
            pl.BlockSpec((CONV_W, LANE), lambda b, c: (0, c)),
            pl.BlockSpec((1, LANE), lambda b, c: (0, c)),
            pl.BlockSpec((None, LANE, 4 * LANE), lambda b, c: (c, 0, 0)),
            pl.BlockSpec((None, 1, 4 * LANE), lambda b, c: (c, 0, 0)),
            pl.BlockSpec((2, LANE), lambda b, c: (0, c)),
            pl.BlockSpec((None, 2, LANE), lambda b, c: (b, 0, c)),
            pl.BlockSpec(memory_space=pl.ANY),
        ],
        out_specs=(
            pl.BlockSpec((seq, LANE), lambda b, c: (row_block0 + b, c)),
            pl.BlockSpec((None, 2, LANE), lambda b, c: (b, 0, c)),
        ),
        scratch_shapes=[
            pltpu.VMEM((seq + 2 * SUBLANE, LANE), F32),
            pltpu.VMEM((2, seq, LANE), F32),
            pltpu.VMEM((2, seq, LANE), F32),
        ],
        input_output_aliases={8: 0},
        compiler_params=_cparams(("arbitrary", "arbitrary")),
        name=f"rglru_s{seq}",
    )(main, main, cw, cb, wg, bg, lam, h0, y_prev)


def _rope_lanes(x, cos, sin_m, sin_p):
    n = x.shape[1] // LANE
    half = QK_ROPE // 4
    cos_t = jnp.concatenate([cos] * n, axis=1) if n > 1 else cos
    sm_t = jnp.concatenate([sin_m] * n, axis=1) if n > 1 else sin_m
    sp_t = jnp.concatenate([sin_p] * n, axis=1) if n > 1 else sin_p
    up = pltpu.roll(x, x.shape[1] - half, 1)
    dn = pltpu.roll(x, half, 1)
    return x * cos_t + up * sm_t + dn * sp_t


def _mla_prep_kernel(cq0_ref, cq1_ref, cq2_ref, ckv_ref, kr_ref, cos_ref, sm_ref, sp_ref,
                     qn_ref, wq_ref, kvn_ref, wuk_ref, wuvt_ref,
                     q_ref, k_ref, vt_ref, ckvn_ref):
    cq = [cq0_ref[...], cq1_ref[...], cq2_ref[...]]
    ms = (jnp.sum(cq[0] * cq[0], axis=-1, keepdims=True) + jnp.sum(cq[1] * cq[1], axis=-1, keepdims=True)
          + jnp.sum(cq[2] * cq[2], axis=-1, keepdims=True)) * (1.0 / Q_LORA)
    inv = lax.rsqrt(ms + EPS)
    blk = Q_LORA // 3
    q = None
    for j in range(3):
        cqn = ((cq[j] * inv) * qn_ref[:, j * blk:(j + 1) * blk]).astype(BF16)
        part = _dot(cqn, wq_ref[j * blk:(j + 1) * blk, :])
        q = part if q is None else q + part
    cos, sm, sp = cos_ref[...], sm_ref[...], sp_ref[...]
    q_ref[...] = _rope_lanes(q, cos, sm, sp).astype(BF16)

    ckv = ckv_ref[...]
    inv_kv = lax.rsqrt(jnp.mean(ckv * ckv, axis=-1, keepdims=True) + EPS)
    ckvn = (ckv * inv_kv) * kvn_ref[...]
    ckvn_ref[...] = ckvn
    ckvn_b = ckvn.astype(BF16)
    kr_rot = _rope_lanes(kr_ref[...], cos, sm, sp)
    k_ref[...] = (_dot(ckvn_b, wuk_ref[...]) + jnp.concatenate([kr_rot] * MLA_HEADS, axis=1)).astype(BF16)
    vt = _dot_nt(wuvt_ref[...], ckvn_b).astype(BF16)
    for c in range(vt_ref.shape[0]):
        vt_ref[c] = vt[:, c * KC:(c + 1) * KC]


def _mla_prep(main, krp, cos_t, sm_t, sp_t, q_norm, wq_p, kv_norm, wuk_p, wuvt):
    cq_col0 = 2 * RG_W // 256
    hp = MLA_HEADS * HEAD_PAD
    full = lambda shape: pl.BlockSpec(shape, lambda i: (0,) * len(shape))
    tab = pl.BlockSpec((TM, LANE), lambda i: (_pos_block(i, TM), 0))
    return pl.pallas_call(
        _mla_prep_kernel,
        out_shape=(
            jax.ShapeDtypeStruct((N_TOK, hp), BF16),
            jax.ShapeDtypeStruct((N_TOK, hp), BF16),
            jax.ShapeDtypeStruct((N_TOK // KC, MLA_HEADS * V_HEAD, KC), BF16),
            jax.ShapeDtypeStruct((N_TOK, KV_LORA), F32),
        ),
        grid=(N_TOK // TM,),
        in_specs=[
            pl.BlockSpec((TM, 256), lambda i: (i, cq_col0)),
            pl.BlockSpec((TM, 256), lambda i: (i, cq_col0 + 1)),
            pl.BlockSpec((TM, 256), lambda i: (i, cq_col0 + 2)),
            pl.BlockSpec((TM, 256), lambda i: (i, cq_col0 + 3)),
            pl.BlockSpec((TM, LANE), lambda i: (i, 0)),
            tab, tab, tab,
            full((1, Q_LORA)), full((Q_LORA, hp)), full((1, KV_LORA)), full((KV_LORA, hp)),
            full((MLA_HEADS * V_HEAD, KV_LORA)),
        ],
        out_specs=(
            pl.BlockSpec((TM, hp), lambda i: (i, 0)),
            pl.BlockSpec((TM, hp), lambda i: (i, 0)),
            pl.BlockSpec((TM // KC, MLA_HEADS * V_HEAD, KC), lambda i: (i, 0, 0)),
            pl.BlockSpec((TM, KV_LORA), lambda i: (i, 0)),
        ),
        compiler_params=_cparams(("arbitrary",)),
        name="mla_prep",
    )(main, main, main, main, krp, cos_t, sm_t, sp_t, q_norm, wq_p, kv_norm, wuk_p, wuvt)


def _mla_ctx_kernel(ckv_ref, kr_ref, wuk_ref, wuvt_ref, k_ref, vt_ref):
    ckv_b = ckv_ref[...].astype(BF16)
    k_ref[...] = (_dot(ckv_b, wuk_ref[...]) + jnp.concatenate([kr_ref[...]] * MLA_HEADS, axis=1)).astype(BF16)
    vt_ref[...] = _dot_nt(wuvt_ref[...], ckv_b).astype(BF16)


def _mla_ctx(ctx_ckv, ctx_krp, wuk_p, wuvt):
    n = ctx_ckv.shape[0]
    hp = MLA_HEADS * HEAD_PAD
    tm = KC
    full = lambda shape: pl.BlockSpec(shape, lambda i: (0,) * len(shape))
    return pl.pallas_call(
        _mla_ctx_kernel,
        out_shape=(jax.ShapeDtypeStruct((n, hp), BF16),
                   jax.ShapeDtypeStruct((n // tm, MLA_HEADS * V_HEAD, tm), BF16)),
        grid=(n // tm,),
        in_specs=[
            pl.BlockSpec((tm, KV_LORA), lambda i: (i, 0)),
            pl.BlockSpec((tm, LANE), lambda i: (i, 0)),
            full((KV_LORA, hp)), full((MLA_HEADS * V_HEAD, KV_LORA)),
        ],
        out_specs=(pl.BlockSpec((tm, hp), lambda i: (i, 0)),
                   pl.BlockSpec((None, MLA_HEADS * V_HEAD, tm), lambda i: (i, 0, 0))),
        compiler_params=_cparams(("arbitrary",)),
        name="mla_ctx",
    )(ctx_ckv, ctx_krp, wuk_p, wuvt)


def _attn_kernel(*refs, seq, tq, n_ctx):
    if n_ctx:
        q_ref, k_ref, vt_ref, kc_ref, vtc_ref, oprev_ref, o_ref, s_scr, p_scr = refs
    else:
        q_ref, k_ref, vt_ref, oprev_ref, o_ref, s_scr, p_scr = refs
    del oprev_ref
    has_ctx = 1 if n_ctx else 0
    n_own = seq // KC
    n = n_own + has_ctx
    c_exp = ATTN_SCALE * math.log2(math.e)
    qs = [q_ref[:, h * HEAD_PAD:(h + 1) * HEAD_PAD] for h in range(2)]

    def k_chunk(c, h):
        if has_ctx and isinstance(c, int) and c == n_own:
            return kc_ref[:, h * HEAD_PAD:(h + 1) * HEAD_PAD]
        t0 = c * KC if isinstance(c, int) else pl.multiple_of(c * KC, KC)
        return k_ref[pl.ds(t0, KC), h * HEAD_PAD:(h + 1) * HEAD_PAD]

    def v_chunk(c, h):
        if has_ctx and isinstance(c, int) and c == n_own:
            return vtc_ref[h * V_HEAD:(h + 1) * V_HEAD, :]
        return vt_ref[c, h * V_HEAD:(h + 1) * V_HEAD, :]

    def scores(c, slot):
        for h in range(2):
            s_scr[slot, h] = _dot_nt(k_chunk(c, h), qs[h])

    def softmax_chunk(slot, st):
        out = []
        for h in range(2):
            m, l, _, acc = st[h]
            t = s_scr[slot, h] * c_exp
            m_new = jnp.maximum(m, jnp.max(t, axis=0, keepdims=True))
            alpha = jnp.exp2(m - m_new)
            p = jnp.exp2(t - m_new)
            p_scr[slot, h] = p.astype(BF16)
            out.append((m_new, alpha * l + jnp.sum(p, axis=0, keepdims=True), alpha, acc))
        return out

    def weighted_values(c, slot, st, alphas):
        return [(st[h][0], st[h][1], st[h][2], alphas[h] * st[h][3] + _dot(v_chunk(c, h), p_scr[slot, h]))
                for h in range(2)]

    def step(c, slot, st, with_s, with_v):
        if with_s:
            scores(c + 1, 1 - slot)
        alphas = [st[h][2] for h in range(2)]
        st = softmax_chunk(slot, st)
        if with_v:
            prev = max(c - 1, 0) if isinstance(c, int) else jnp.maximum(c - 1, 0)
            st = weighted_values(prev, 1 - slot, st, alphas)
        return st

    p_scr[1] = jnp.zeros(p_scr.shape[1:], BF16)
    st = [(jnp.full((1, tq), NEG, F32), jnp.zeros((1, tq), F32), jnp.ones((1, tq), F32),
           jnp.zeros((V_HEAD, tq), F32)) for _ in range(2)]
    scores(0, 0)
    n_pairs = max(n_own - 1, 0) // 2

    def pair(j, flat):
        st = [tuple(flat[0:4]), tuple(flat[4:8])]
        st = step(2 * j, 0, st, True, True)
        st = step(2 * j + 1, 1, st, True, True)
        return tuple(st[0]) + tuple(st[1])

    if n_pairs:
        flat = lax.fori_loop(0, n_pairs, pair, tuple(st[0]) + tuple(st[1]))
        st = [tuple(flat[0:4]), tuple(flat[4:8])]
    for c in range(2 * n_pairs, n):
        st = step(c, c % 2, st, c + 1 < n, c > 0)
    st = weighted_values(n - 1, (n - 1) % 2, st, [st[h][2] for h in range(2)])
    for h in range(2):
        o_ref[:, h * V_HEAD:(h + 1) * V_HEAD] = (st[h][3] / st[h][1]).T.astype(o_ref.dtype)


def _attention(q, k, vt, o_prev, kc, vtc, *, n_seq, seq, row_block0, tq):
    n_ctx = 0 if kc is None else PAST_LEN
    n_hp = MLA_HEADS // 2
    nq = seq // tq
    kern = functools.partial(_attn_kernel, seq=seq, tq=tq, n_ctx=n_ctx)
    in_specs = [
        pl.BlockSpec((tq, 2 * HEAD_PAD), lambda b, j, i: ((row_block0 + b) * nq + i, j)),
        pl.BlockSpec((seq, 2 * HEAD_PAD), lambda b, j, i: (row_block0 + b, j)),
        pl.BlockSpec((seq // KC, 2 * V_HEAD, KC), lambda b, j, i: (row_block0 + b, j, 0)),
    ]
    args = [q, k, vt]
    if n_ctx:
        in_specs += [
            pl.BlockSpec((n_ctx, 2 * HEAD_PAD), lambda b, j, i: (b, j)),
            pl.BlockSpec((None, 2 * V_HEAD, KC), lambda b, j, i: (b, j, 0)),
        ]
        args += [kc, vtc]
    in_specs.append(pl.BlockSpec(memory_space=pl.ANY))
    args.append(o_prev)
    return pl.pallas_call(
        kern,
        out_shape=jax.ShapeDtypeStruct((N_TOK, MLA_HEADS * V_HEAD), BF16),
        grid=(n_seq, n_hp, nq),
        in_specs=in_specs,
        out_specs=pl.BlockSpec((tq, 2 * V_HEAD), lambda b, j, i: ((row_block0 + b) * nq + i, j)),
        scratch_shapes=[pltpu.VMEM((2, 2, KC, tq), F32), pltpu.VMEM((2, 2, KC, tq), BF16)],
        input_output_aliases={len(args) - 1: 0},
        compiler_params=_cparams(("arbitrary", "arbitrary", "arbitrary")),
        name=f"mla_attention_s{seq}",
    )(*args)


def _post_mixer(y, x_ref, mod_ref, lng_ref, lnb_ref, wrh_ref, wrl_ref, x1_ref, h2_ref, lgt_ref):
    z = ALPHA * x_ref[...] + mod_ref[2:3, :] * y
    x1 = _layernorm_rows(z, lng_ref[...], lnb_ref[...])
    x1_ref[...] = x1
    h2 = _modulated(x1, mod_ref, 3, 4)
    h2_ref[...] = h2
    h_hi, h_lo = _split_hi_lo(h2)
    w_hi, w_lo = wrh_ref[...], wrl_ref[...]
    lgt_ref[...] = _dot_nt(w_hi, h_hi) + _dot_nt(w_hi, h_lo) + _dot_nt(w_lo, h_hi)


def _out_ab_kernel(yrg_ref, o_ref, wa_ref, wb_ref, x_ref, mod_ref, lng_ref, lnb_ref, wrh_ref, wrl_ref,
                   x1_ref, h2_ref, lgt_ref):
    y = _dot(yrg_ref[...].astype(BF16), wa_ref[...]) + _dot(o_ref[...], wb_ref[...])
    _post_mixer(y, x_ref, mod_ref, lng_ref, lnb_ref, wrh_ref, wrl_ref, x1_ref, h2_ref, lgt_ref)


def _post_specs():
    full = lambda shape: pl.BlockSpec(shape, lambda i: (0,) * len(shape))
    in_specs = [
        pl.BlockSpec((TM, D_MODEL), lambda i: (i, 0)),
        pl.BlockSpec((None, MOD_ROWS, D_MODEL), lambda i: (_cond_block(i, TM), 0, 0)),
        full((1, D_MODEL)), full((1, D_MODEL)),
        full((N_EXPERTS, D_MODEL)), full((N_EXPERTS, D_MODEL)),
    ]
    out_shape = (
        jax.ShapeDtypeStruct((N_TOK, D_MODEL), F32),
        jax.ShapeDtypeStruct((N_TOK, D_MODEL), F32),
        jax.ShapeDtypeStruct((N_EXPERTS, N_TOK), F32),
    )
    out_specs = (
        pl.BlockSpec((TM, D_MODEL), lambda i: (i, 0)),
        pl.BlockSpec((TM, D_MODEL), lambda i: (i, 0)),
        pl.BlockSpec((N_EXPERTS, TM), lambda i: (0, i)),
    )
    return in_specs, out_shape, out_specs


def _out_ab(yrg, o, w_a, w_b, x, mods, lng, lnb, wr_hi, wr_lo):
    full = lambda shape: pl.BlockSpec(shape, lambda i: (0,) * len(shape))
    post_in, out_shape, out_specs = _post_specs()
    return pl.pallas_call(
        _out_ab_kernel,
        out_shape=out_shape,
        grid=(N_TOK // TM,),
        in_specs=[
            pl.BlockSpec((TM, RG_W), lambda i: (i, 0)),
            pl.BlockSpec((TM, MLA_HEADS * V_HEAD), lambda i: (i, 0)),
            full((RG_W, D_MODEL)), full((MLA_HEADS * V_HEAD, D_MODEL)),
        ] + post_in,
        out_specs=out_specs,
        compiler_params=_cparams(("arbitrary",)),
        name="out_ab",
    )(yrg, o, w_a, w_b, x, mods, lng, lnb, wr_hi, wr_lo)


def _proj_c_kernel(x_ref, mod_ref, cos_ref, sin_ref, wq_ref, wk_ref, wv_ref, wg_ref, q_ref, k_ref, v_ref, g_ref):
    h = _modulated(x_ref[...], mod_ref, 0, 1).astype(BF16)
    cos, sin = cos_ref[...], sin_ref[...]
    half = RET_DK // 2
    for hd in range(RET_HEADS):
        for w_ref, o_ref, scale in ((wq_ref, q_ref, 1.0), (wk_ref, k_ref, RET_DK ** -0.5)):
            p = _dot(h, w_ref[:, hd * RET_DK:(hd + 1) * RET_DK])
            x1, x2 = p[:, :half], p[:, half:]
            r1 = x1 * cos - x2 * sin
            r2 = x2 * cos + x1 * sin
            if scale != 1.0:
                r1, r2 = r1 * scale, r2 * scale
            o_ref[:, hd * RET_DK:hd * RET_DK + half] = r1.astype(BF16)
            o_ref[:, hd * RET_DK + half:(hd + 1) * RET_DK] = r2.astype(BF16)
    step = 512
    for j in range(MIX_C // step):
        v_ref[:, j * step:(j + 1) * step] = _dot(h, wv_ref[:, j * step:(j + 1) * step]).astype(BF16)
        g_ref[:, j * step:(j + 1) * step] = _dot(h, wg_ref[:, j * step:(j + 1) * step])


def _proj_c(x, mods, cos_t, sin_t, wq, wk, wv, wg):
    full = lambda shape: pl.BlockSpec(shape, lambda i: (0,) * len(shape))
    qk = RET_HEADS * RET_DK
    tab = pl.BlockSpec((TM, RET_DK // 2), lambda i: (_pos_block(i, TM), 0))
    return pl.pallas_call(
        _proj_c_kernel,
        out_shape=(
            jax.ShapeDtypeStruct((N_TOK, qk), BF16), jax.ShapeDtypeStruct((N_TOK, qk), BF16),
            jax.ShapeDtypeStruct((N_TOK, MIX_C), BF16), jax.ShapeDtypeStruct((N_TOK, MIX_C), F32),
        ),
        grid=(N_TOK // TM,),
        in_specs=[
            pl.BlockSpec((TM, D_MODEL), lambda i: (i, 0)),
            pl.BlockSpec((None, MOD_ROWS, D_MODEL), lambda i: (_cond_block(i, TM), 0, 0)),
            tab, tab,
            full((D_MODEL, qk)), full((D_MODEL, qk)), full((D_MODEL, MIX_C)), full((D_MODEL, MIX_C)),
        ],
        out_specs=(
            pl.BlockSpec((TM, qk), lambda i: (i, 0)), pl.BlockSpec((TM, qk), lambda i: (i, 0)),
            pl.BlockSpec((TM, MIX_C), lambda i: (i, 0)), pl.BlockSpec((TM, MIX_C), lambda i: (i, 0)),
        ),
        compiler_params=_cparams(("arbitrary",), 56),
        name="proj_c",
    )(x, mods, cos_t, sin_t, wq, wk, wv, wg)


def _retention_kernel(*refs, seq, with_state):
    if with_state:
        q_ref, k_ref, v_ref, gam_ref, r0_ref, oprev_ref, o_ref, rfin_ref, r_s = refs
    else:
        q_ref, k_ref, v_ref, gam_ref, r0_ref, oprev_ref, o_ref, r_s = refs
        rfin_ref = None
    del oprev_ref
    c = RET_CHUNK
    n = seq // c
    ii = lax.broadcasted_iota(I32, (c, c), 0).astype(F32)
    jj = lax.broadcasted_iota(I32, (c, c), 1).astype(F32)
    ci = lax.broadcasted_iota(I32, (c, 1), 0).astype(F32)

    for d in range(2):
        gam = gam_ref[d]
        lg_row = -_softplus(-gam[0:1, :])
        lg = jnp.broadcast_to(lg_row, (c, c))
        lg_col = jnp.broadcast_to(lg_row[:, 0:1], (c, 1))
        if d == 0:
            diff = ii - jj
            xi = jnp.exp((ci + 1.0) * lg_col)
            zeta = jnp.exp((c - 1.0 - ci) * lg_col)
        else:
            diff = jj - ii
            xi = jnp.exp((c - ci) * lg_col)
            zeta = jnp.exp(ci * lg_col)
        dmat = jnp.where(diff >= 0, jnp.exp(jnp.maximum(diff, 0.0) * lg), 0.0)
        g_chunk = jnp.exp(float(c) * lg_row[:, 0:1])
        r_s[...] = r0_ref[d]

        def chunk(s, carry, d=d, dmat=dmat, xi=xi, zeta=zeta, g_chunk=g_chunk):
            idx = s if d == 0 else n - 1 - s
            t0 = pl.multiple_of(idx * c, c)
            qb = q_ref[pl.ds(t0, c), :]
            kb = k_ref[pl.ds(t0, c), :]
            vb = v_ref[pl.ds(t0, c), :]
            r = r_s[...]
            inner = _dot_nt(qb, kb) * dmat
            o = _dot(inner.astype(BF16), vb) + _dot((qb.astype(F32) * xi).astype(BF16), r.astype(BF16))
            r_s[...] = r * g_chunk + _dot_tn((kb.astype(F32) * zeta).astype(BF16), vb)
            if d == 0:
                o_ref[pl.ds(t0, c), :] = o
            else:
                o_ref[pl.ds(t0, c), :] = o_ref[pl.ds(t0, c), :] + o
            return carry

        lax.fori_loop(0, n, chunk, 0)
        if with_state:
            rfin_ref[d] = r_s[...]


def _retention(q, k, v, gam, r0, o_prev, *, n_seq, seq, row_block0, with_state):
    kern = functools.partial(_retention_kernel, seq=seq, with_state=with_state)
    out_shape = [jax.ShapeDtypeStruct((N_TOK, MIX_C), F32)]
    out_specs = [pl.BlockSpec((seq, RET_DV), lambda b, h: (row_block0 + b, h))]
    if with_state:
        out_shape.append(jax.ShapeDtypeStruct((n_seq, 2, RET_HEADS, RET_DK, RET_DV), F32))
        out_specs.append(pl.BlockSpec((None, 2, None, RET_DK, RET_DV), lambda b, h: (b, 0, h, 0, 0)))
    return pl.pallas_call(
        kern,
        out_shape=tuple(out_shape),
        grid=(n_seq, RET_HEADS),
        in_specs=[
            pl.BlockSpec((seq, RET_DK), lambda b, h: (row_block0 + b, h)),
            pl.BlockSpec((seq, RET_DK), lambda b, h: (row_block0 + b, h)),
            pl.BlockSpec((seq, RET_DV), lambda b, h: (row_block0 + b, h)),
            pl.BlockSpec((2, None, SUBLANE, LANE), lambda b, h: (0, h, 0, 0)),
            pl.BlockSpec((None, 2, None, RET_DK, RET_DV), lambda b, h: (b, 0, h, 0, 0)),
            pl.BlockSpec(memory_space=pl.ANY),
        ],
        out_specs=tuple(out_specs),
        scratch_shapes=[pltpu.VMEM((RET_DK, RET_DV), F32)],
        input_output_aliases={5: 0},
        compiler_params=_cparams(("arbitrary", "arbitrary"), 56),
        name=f"retention_s{seq}",
    )(q, k, v, gam, r0, o_prev)


def _out_c_kernel(o_ref, g_ref, w_ref, x_ref, mod_ref, lng_ref, lnb_ref, wrh_ref, wrl_ref,
                  x1_ref, h2_ref, lgt_ref):
    y = None
    for hd in range(RET_HEADS):
        o = o_ref[:, hd * RET_DV:(hd + 1) * RET_DV]
        mu = jnp.mean(o, axis=-1, keepdims=True)
        oc = o - mu
        var = jnp.mean(oc * oc, axis=-1, keepdims=True)
        on = oc * lax.rsqrt(var + EPS)
        a = (on * _silu(g_ref[:, hd * RET_DV:(hd + 1) * RET_DV])).astype(BF16)
        part = _dot(a, w_ref[hd * RET_DV:(hd + 1) * RET_DV, :])
        y = part if y is None else y + part
    _post_mixer(y, x_ref, mod_ref, lng_ref, lnb_ref, wrh_ref, wrl_ref, x1_ref, h2_ref, lgt_ref)


def _out_c(o, g, w, x, mods, lng, lnb, wr_hi, wr_lo):
    full = lambda shape: pl.BlockSpec(shape, lambda i: (0,) * len(shape))
    post_in, out_shape, out_specs = _post_specs()
    return pl.pallas_call(
        _out_c_kernel,
        out_shape=out_shape,
        grid=(N_TOK // TM,),
        in_specs=[
            pl.BlockSpec((TM, MIX_C), lambda i: (i, 0)),
            pl.BlockSpec((TM, MIX_C), lambda i: (i, 0)),
            full((MIX_C, D_MODEL)),
        ] + post_in,
        out_specs=out_specs,
        compiler_params=_cparams(("arbitrary",), 56),
        name="out_c",
    )(o, g, w, x, mods, lng, lnb, wr_hi, wr_lo)


def _route_kernel(lgt_ref, bias_ref, tri_ref, ids_ref, w_ref, pos_ref, cnt_ref, carry):
    @pl.when(pl.program_id(0) == 0)
    def _():
        carry[...] = jnp.zeros_like(carry)

    tt = lgt_ref.shape[1]
    scores = jax.nn.sigmoid(lgt_ref[...])
    sel = scores + bias_ref[...]
    srow = lax.broadcasted_iota(I32, (GROUP_SIZE, tt), 0).astype(F32)
    ninf = -jnp.inf

    gs = []
    for g in range(N_GROUPS):
        sg = sel[g * GROUP_SIZE:(g + 1) * GROUP_SIZE, :]
        m1 = jnp.max(sg, axis=0, keepdims=True)
        i1 = jnp.min(jnp.where(sg == m1, srow, float(GROUP_SIZE)), axis=0, keepdims=True)
        m2 = jnp.max(jnp.where(srow == i1, ninf, sg), axis=0, keepdims=True)
        gs.append(m1 + m2)
    gs = jnp.concatenate(gs, axis=0)
    chosen = jnp.zeros((N_GROUPS, tt), F32)
    for _ in range(TOPK_GROUPS):
        mg = jnp.max(gs, axis=0, keepdims=True)
        gi = jnp.min(jnp.where(gs == mg, srow, float(N_GROUPS)), axis=0, keepdims=True)
        hit = srow == gi
        chosen = jnp.where(hit, 1.0, chosen)
        gs = jnp.where(hit, ninf, gs)
    sel = jnp.concatenate(
        [jnp.where(jnp.broadcast_to(chosen[g:g + 1, :], (GROUP_SIZE, tt)) > 0.5,
                   sel[g * GROUP_SIZE:(g + 1) * GROUP_SIZE, :], ninf) for g in range(N_GROUPS)], axis=0)

    erow = lax.broadcasted_iota(I32, (N_EXPERTS, tt), 0).astype(F32)
    ids, ws = [], []
    for _ in range(TOP_K):
        m = jnp.max(sel, axis=0, keepdims=True)
        ei = jnp.min(jnp.where(sel == m, erow, float(N_EXPERTS)), axis=0, keepdims=True)
        hit = erow == ei
        ids.append(ei)
        ws.append(jnp.sum(jnp.where(hit, scores, 0.0), axis=0, keepdims=True))
        sel = jnp.where(hit, ninf, sel)
    wsum = ws[0]
    for k in range(1, TOP_K):
        wsum = wsum + ws[k]
    ids_ref[...] = jnp.concatenate(ids, axis=0).astype(I32)
    w_ref[...] = jnp.concatenate([w / wsum * ROUTED_SCALE for w in ws], axis=0)

    member_f = jnp.zeros((N_EXPERTS, tt), F32)
    for k in range(TOP_K):
        member_f = jnp.where(erow == ids[k], 1.0, member_f)
    rank = _dot(member_f.astype(BF16), tri_ref[...]) + carry[:, 0:1]
    pos_ref[...] = jnp.concatenate(
        [jnp.sum(jnp.where(erow == ids[k], rank, 0.0), axis=0, keepdims=True) for k in range(TOP_K)],
        axis=0).astype(I32)
    total = carry[...] + jnp.sum(member_f, axis=1, keepdims=True)
    carry[...] = total
    cnt_ref[...] = total


def _route(lgt, bias, tri):
    nb = N_TOK // TT
    return pl.pallas_call(
        _route_kernel,
        out_shape=(
            jax.ShapeDtypeStruct((TOP_K, N_TOK), I32), jax.ShapeDtypeStruct((TOP_K, N_TOK), F32),
            jax.ShapeDtypeStruct((TOP_K, N_TOK), I32), jax.ShapeDtypeStruct((N_EXPERTS, LANE), F32),
        ),
        grid=(nb,),
        in_specs=[
            pl.BlockSpec((N_EXPERTS, TT), lambda i: (0, i)),
            pl.BlockSpec((N_EXPERTS, 1), lambda i: (0, 0)),
            pl.BlockSpec((TT, TT), lambda i: (0, 0)),
        ],
        out_specs=(
            pl.BlockSpec((TOP_K, TT), lambda i: (0, i)), pl.BlockSpec((TOP_K, TT), lambda i: (0, i)),
            pl.BlockSpec((TOP_K, TT), lambda i: (0, i)), pl.BlockSpec((N_EXPERTS, LANE), lambda i: (0, 0)),
        ),
        scratch_shapes=[pltpu.VMEM((N_EXPERTS, LANE), F32)],
        compiler_params=_cparams(("arbitrary",)),
        name="moe_route",
    )(lgt, bias, tri)


def _dest_kernel(start_ref, ids_ref, pos_ref, dest_ref):
    ids = ids_ref[...]
    dest = pos_ref[...]
    for e in range(N_EXPERTS):
        dest = dest + jnp.where(ids == e, start_ref[e], 0)
    dest_ref[...] = dest


def _dest_rows(start, ids, pos):
    tt = 2048
    return pl.pallas_call(
        _dest_kernel,
        out_shape=jax.ShapeDtypeStruct((TOP_K, N_TOK), I32),
        grid_spec=pltpu.PrefetchScalarGridSpec(
            num_scalar_prefetch=1,
            grid=(N_TOK // tt,),
            in_specs=[pl.BlockSpec((TOP_K, tt), lambda i, s: (0, i)), pl.BlockSpec((TOP_K, tt), lambda i, s: (0, i))],
            out_specs=pl.BlockSpec((TOP_K, tt), lambda i, s: (0, i)),
        ),
        compiler_params=_cparams(("arbitrary",)),
        name="moe_dest",
    )(start, ids, pos)


def _dispatch_kernel(dest_hbm, h_ref, xg_in, xg_out, dsm, sem_idx, sem_row):
    del xg_in
    i = pl.program_id(0)
    tt = h_ref.shape[0]
    cp = pltpu.make_async_copy(dest_hbm.at[:, pl.ds(pl.multiple_of(i * tt, tt), tt)], dsm, sem_idx)
    cp.start()
    cp.wait()

    def issue(t, carry):
        for k in range(TOP_K):
            pltpu.make_async_copy(h_ref.at[pl.ds(t, 1)], xg_out.at[pl.ds(dsm[k, t], 1)], sem_row).start()
        return carry

    lax.fori_loop(0, tt, issue, 0)

    def drain(t, carry):
        for k in range(TOP_K):
            pltpu.make_async_copy(h_ref.at[pl.ds(0, 1)], xg_out.at[pl.ds(0, 1)], sem_row).wait()
        return carry

    lax.fori_loop(0, tt, drain, 0)


def _dispatch(dest, h2, xg_zero):
    tt = 256
    return pl.pallas_call(
        _dispatch_kernel,
        out_shape=jax.ShapeDtypeStruct((N_EROWS, D_MODEL), F32),
        grid=(N_TOK // tt,),
        in_specs=[
            pl.BlockSpec(memory_space=pl.ANY),
            pl.BlockSpec((tt, D_MODEL), lambda i: (i, 0)),
            pl.BlockSpec(memory_space=pl.ANY),
        ],
        out_specs=pl.BlockSpec(memory_space=pl.ANY),
        scratch_shapes=[pltpu.SMEM((TOP_K, tt), I32), pltpu.SemaphoreType.DMA, pltpu.SemaphoreType.DMA],
        input_output_aliases={2: 0},
        compiler_params=_cparams(("arbitrary",)),
        name="moe_dispatch",
    )(dest, h2, xg_zero)


def _expert_kernel(be_ref, nb_ref, x_ref, wg_ref, wu_ref, wd_ref, y_ref):
    @pl.when(pl.program_id(0) < nb_ref[0])
    def _():
        x = x_ref[...].astype(BF16)
        hb = _silu(_dot(x, wg_ref[...])) * _dot(x, wu_ref[...])
        y_ref[...] = _dot(hb.astype(BF16), wd_ref[...])

    @pl.when(pl.program_id(0) >= nb_ref[0])
    def _():
        y_ref[...] = jnp.zeros_like(y_ref)


def _experts(block_e, n_used, xg, wg, wu, wd):
    def row_map(i, be, nb):
        return (jnp.minimum(i, nb[0] - 1), 0)

    def w_map(i, be, nb):
        return (be[jnp.minimum(i, nb[0] - 1)], 0, 0)

    return pl.pallas_call(
        _expert_kernel,
        out_shape=jax.ShapeDtypeStruct((N_EROWS, D_MODEL), F32),
        grid_spec=pltpu.PrefetchScalarGridSpec(
            num_scalar_prefetch=2,
            grid=(N_EBLOCKS,),
            in_specs=[
                pl.BlockSpec((TME, D_MODEL), row_map),
                pl.BlockSpec((None, D_MODEL, D_EXPERT), w_map),
                pl.BlockSpec((None, D_MODEL, D_EXPERT), w_map),
                pl.BlockSpec((None, D_EXPERT, D_MODEL), w_map),
            ],
            out_specs=pl.BlockSpec((TME, D_MODEL), lambda i, be, nb: (i, 0)),
        ),
        compiler_params=_cparams(("arbitrary",)),
        name="moe_experts",
    )(block_e, n_used, xg, wg, wu, wd)


def _combine_kernel(dest_hbm, yb_hbm, wt_ref, h_ref, x1_ref, mod_ref, lng_ref, lnb_ref,
                    wsg_ref, wsu_ref, wsd_ref, out_ref, dsm, buf, sem_idx, sem_row):
    i = pl.program_id(0)
    tt = h_ref.shape[0]
    cp = pltpu.make_async_copy(dest_hbm.at[:, pl.ds(pl.multiple_of(i * tt, tt), tt)], dsm, sem_idx)
    cp.start()
    cp.wait()

    def issue(t, carry):
        for k in range(TOP_K):
            pltpu.make_async_copy(yb_hbm.at[pl.ds(dsm[k, t], 1)], buf.at[k, pl.ds(t, 1)], sem_row).start()
        return carry

    lax.fori_loop(0, tt, issue, 0)

    hb = h_ref[...].astype(BF16)
    shared = _dot((_silu(_dot(hb, wsg_ref[...])) * _dot(hb, wsu_ref[...])).astype(BF16), wsd_ref[...])

    def drain(t, carry):
        for k in range(TOP_K):
            pltpu.make_async_copy(yb_hbm.at[pl.ds(0, 1)], buf.at[0, pl.ds(0, 1)], sem_row).wait()
        return carry

    lax.fori_loop(0, tt, drain, 0)

    wt = wt_ref[...]
    routed = buf[0] * wt[:, 0:1]
    for k in range(1, TOP_K):
        routed = routed + buf[k] * wt[:, k:k + 1]
    z = ALPHA * x1_ref[...] + mod_ref[5:6, :] * (routed + shared)
    out_ref[...] = _layernorm_rows(z, lng_ref[...], lnb_ref[...])


def _combine(dest, yb, wt, h2, x1, mods, lng, lnb, wsg, wsu, wsd):
    tt = TC_TOK
    full = lambda shape: pl.BlockSpec(shape, lambda i: (0,) * len(shape))
    return pl.pallas_call(
        _combine_kernel,
        out_shape=jax.ShapeDtypeStruct((N_TOK, D_MODEL), F32),
        grid=(N_TOK // tt,),
        in_specs=[
            pl.BlockSpec(memory_space=pl.ANY),
            pl.BlockSpec(memory_space=pl.ANY),
            pl.BlockSpec((tt, TOP_K), lambda i: (i, 0)),
            pl.BlockSpec((tt, D_MODEL), lambda i: (i, 0)),
            pl.BlockSpec((tt, D_MODEL), lambda i: (i, 0)),
            pl.BlockSpec((None, MOD_ROWS, D_MODEL), lambda i: (_cond_block(i, tt), 0, 0)),
            full((1, D_MODEL)), full((1, D_MODEL)),
            full((D_MODEL, D_EXPERT)), full((D_MODEL, D_EXPERT)), full((D_EXPERT, D_MODEL)),
        ],
        out_specs=pl.BlockSpec((tt, D_MODEL), lambda i: (i, 0)),
        scratch_shapes=[
            pltpu.SMEM((TOP_K, tt), I32), pltpu.VMEM((TOP_K, tt, D_MODEL), F32),
            pltpu.SemaphoreType.DMA, pltpu.SemaphoreType.DMA,
        ],
        compiler_params=_cparams(("arbitrary",)),
        name="moe_combine",
    )(dest, yb, wt, h2, x1, mods, lng, lnb, wsg, wsu, wsd)


def _moe_and_norm(x1, h2, lgt, mods, lng, lnb, router_bias, tri, wg, wu, wd, wsg, wsu, wsd):
    ids, wts, pos, cnt = _route(lgt, router_bias.reshape(N_EXPERTS, 1), tri)
    counts = cnt[:, 0].astype(I32)
    padded = (counts + TME - 1) // TME * TME
    pad_end = jnp.cumsum(padded)
    start = pad_end - padded
    dest = _dest_rows(start.astype(I32), ids, pos)
    n_used = (pad_end[-1] // TME).astype(I32).reshape(1)
    block_row0 = jnp.arange(N_EBLOCKS, dtype=I32) * TME
    block_e = jnp.minimum(jnp.sum((pad_end[None, :] <= block_row0[:, None]).astype(I32), axis=1), N_EXPERTS - 1)
    xg = _dispatch(dest, h2, jnp.zeros((N_EROWS, D_MODEL), F32))
    yb = _experts(block_e, n_used, xg, wg, wu, wd)
    return _combine(dest, yb, wts.T, h2, x1, mods, lng, lnb, wsg, wsu, wsd)


def _rope_tables_mla():
    t = jnp.arange(DEC_SEQ)
    row = (t // GRID_W).astype(F32)
    col = (t % GRID_W).astype(F32)
    n = QK_ROPE // 4
    inv = ROPE_BASE ** (-jnp.arange(n, dtype=F32) / n)
    ang_r = row[:, None] * inv
    ang_c = col[:, None] * inv
    cos = jnp.ones((DEC_SEQ, LANE), F32)
    sin_m = jnp.zeros((DEC_SEQ, LANE), F32)
    sin_p = jnp.zeros((DEC_SEQ, LANE), F32)
    l0 = ROPE_LANE0
    for base, ang in ((l0, ang_r), (l0 + 2 * n, ang_c)):
        c, s = jnp.cos(ang), jnp.sin(ang)
        cos = cos.at[:, base:base + n].set(c).at[:, base + n:base + 2 * n].set(c)
        sin_m = sin_m.at[:, base:base + n].set(-s)
        sin_p = sin_p.at[:, base + n:base + 2 * n].set(s)
    ident = (jnp.ones((TM, LANE), F32), jnp.zeros((TM, LANE), F32), jnp.zeros((TM, LANE), F32))
    return tuple(jnp.concatenate([i, tbl], axis=0) for i, tbl in zip(ident, (cos, sin_m, sin_p)))


def _rope_tables_ret():
    half = RET_DK // 2
    theta = ROPE_BASE ** (-jnp.linspace(0.0, 1.0, half, dtype=F32))
    ang = jnp.arange(DEC_SEQ, dtype=F32)[:, None] * theta
    cos = jnp.concatenate([jnp.ones((TM, half), F32), jnp.cos(ang)], axis=0)
    sin = jnp.concatenate([jnp.zeros((TM, half), F32), jnp.sin(ang)], axis=0)
    return cos, sin


def _pad_heads(w, width, lane0=0):
    k = w.shape[0]
    w = w.reshape(k, MLA_HEADS, width)
    out = jnp.zeros((k, MLA_HEADS, HEAD_PAD), w.dtype).at[:, :, lane0:lane0 + width].set(w)
    return out.reshape(k, MLA_HEADS * HEAD_PAD)


def _rg_gate_weights(wa, ba, wx, bx):
    n_ct = RG_W // LANE
    per = LANE // RG_BW
    tiles_w, tiles_b = [], []
    for c in range(n_ct):
        cols_w, cols_b = [], []
        for d in range(2):
            for w, b in ((wa, ba), (wx, bx)):
                m = jnp.zeros((LANE, LANE), F32)
                for p in range(per):
                    m = m.at[p * RG_BW:(p + 1) * RG_BW, p * RG_BW:(p + 1) * RG_BW].set(w[d, c * per + p])
                cols_w.append(m)
                cols_b.append(b[d, c * LANE:(c + 1) * LANE])
        tiles_w.append(jnp.concatenate(cols_w, axis=1))
        tiles_b.append(jnp.concatenate(cols_b, axis=0)[None, :])
    return jnp.stack(tiles_w).astype(BF16), jnp.stack(tiles_b)


def kernel(x_prompt, x_sample, cache_mla_ckv, cache_mla_krope, state_rglru, state_ret, c, c_ctx, w_ada, b_ada,
           ln_g, ln_b, w_in_ab, rg_conv_w, rg_conv_b, rg_wa, rg_ba, rg_wx, rg_bx, rg_lambda, mla_q_norm, mla_w_uq,
           mla_kv_norm, mla_w_ukv, w_out_ab, w_in_c, ret_gamma_logit, w_out_c, w_router, router_bias,
           w_exp_gate, w_exp_up, w_exp_down, w_sh_gate, w_sh_up, w_sh_down):
    x = jnp.concatenate([x_prompt.reshape(N_PROMPT, D_MODEL), x_sample.reshape(N_SAMPLE, D_MODEL)], axis=0)
    cond = jnp.zeros((16, D_MODEL), F32).at[0].set(c_ctx).at[1:1 + DEC_BATCH].set(c)
    mods_all = _ada_modulation(cond, w_ada, b_ada).reshape(DEPTH, 16, 6, D_MODEL)[:, :N_COND]
    mods_all = jnp.pad(mods_all, ((0, 0), (0, 0), (0, MOD_ROWS - 6), (0, 0)))

    tri = (jnp.arange(TT)[:, None] < jnp.arange(TT)[None, :]).astype(BF16)
    wr_t = jnp.swapaxes(w_router, 1, 2)
    wr_hi = wr_t.astype(BF16)
    wr_lo = (wr_t - wr_hi.astype(F32)).astype(BF16)
    wg_e, wu_e, wd_e = w_exp_gate.astype(BF16), w_exp_up.astype(BF16), w_exp_down.astype(BF16)
    wsg, wsu, wsd = w_sh_gate.astype(BF16), w_sh_up.astype(BF16), w_sh_down.astype(BF16)

    l, e = 0, 0
    mods = mods_all[l]
    n_main = 2 * RG_W + Q_LORA + KV_LORA
    w_main = w_in_ab[e][:, :n_main].astype(BF16)
    w_kr = jnp.zeros((D_MODEL, LANE), F32).at[:, ROPE_LANE0:ROPE_LANE0 + QK_ROPE].set(w_in_ab[e][:, n_main:]).astype(BF16)
    main, krp = _proj_ab(x, mods, w_main, w_kr)

    wg_rg, bg_rg = _rg_gate_weights(rg_wa[e], rg_ba[e], rg_wx[e], rg_bx[e])
    h0_p = jnp.zeros((BATCH, 2, RG_W), F32)
    rg_args = (rg_conv_w[e], rg_conv_b[e].reshape(1, RG_W), wg_rg, bg_rg, rg_lambda[e])
    yrg, rg_fin = _rglru(main, jnp.zeros((N_TOK, RG_W), F32), *rg_args, h0_p,
                         n_seq=BATCH, seq=SEQ, row_block0=0)
    yrg, _ = _rglru(main, yrg, *rg_args, state_rglru[:, e],
                    n_seq=DEC_BATCH, seq=DEC_SEQ, row_block0=N_PROMPT // DEC_SEQ)

    cos_t, sm_t, sp_t = _rope_tables_mla()
    w_uq = mla_w_uq[e].reshape(Q_LORA, MLA_HEADS, QK_NOPE + QK_ROPE)
    wq_p = _pad_heads(w_uq.reshape(Q_LORA, -1), QK_NOPE + QK_ROPE).astype(BF16)
    w_ukv = mla_w_ukv[e].reshape(KV_LORA, MLA_HEADS, QK_NOPE + V_HEAD)
    wuk_p = _pad_heads(w_ukv[:, :, :QK_NOPE].reshape(KV_LORA, -1), QK_NOPE).astype(BF16)
    wuvt = w_ukv[:, :, QK_NOPE:].reshape(KV_LORA, MLA_HEADS * V_HEAD).T.astype(BF16)
    q_att, k_att, v_att, ckv_n = _mla_prep(main, krp, cos_t, sm_t, sp_t, mla_q_norm[e].reshape(1, Q_LORA), wq_p,
                                           mla_kv_norm[e].reshape(1, KV_LORA), wuk_p, wuvt)
    ctx_ckv = cache_mla_ckv[:, e].reshape(DEC_BATCH * PAST_LEN, KV_LORA)
    ctx_krp = jnp.zeros((DEC_BATCH * PAST_LEN, LANE), F32).at[:, ROPE_LANE0:ROPE_LANE0 + QK_ROPE].set(
        cache_mla_krope[:, e].reshape(DEC_BATCH * PAST_LEN, QK_ROPE))
    kc_att, vc_att = _mla_ctx(ctx_ckv, ctx_krp, wuk_p, wuvt)

    o_att = _attention(q_att, k_att, v_att, jnp.zeros((N_TOK, MLA_HEADS * V_HEAD), BF16), None, None,
                       n_seq=BATCH, seq=SEQ, row_block0=0, tq=SEQ)
    o_att = _attention(q_att, k_att, v_att, o_att, kc_att, vc_att,
                       n_seq=DEC_BATCH, seq=DEC_SEQ, row_block0=N_PROMPT // DEC_SEQ, tq=TQ)

    w_out = w_out_ab[e].astype(BF16)
    x1, h2, lgt = _out_ab(yrg, o_att, w_out[:RG_W], w_out[RG_W:], x, mods,
                          ln_g[l, 0].reshape(1, D_MODEL), ln_b[l, 0].reshape(1, D_MODEL), wr_hi[l], wr_lo[l])
    x = _moe_and_norm(x1, h2, lgt, mods, ln_g[l, 1].reshape(1, D_MODEL), ln_b[l, 1].reshape(1, D_MODEL),
                      router_bias[l], tri, wg_e[l], wu_e[l], wd_e[l], wsg[l], wsu[l], wsd[l])

    new_ckv = ckv_n[:N_PROMPT].reshape(BATCH, 1, SEQ, KV_LORA)
    new_krope = krp[:N_PROMPT, ROPE_LANE0:ROPE_LANE0 + QK_ROPE].reshape(BATCH, 1, SEQ, QK_ROPE)
    new_rg = rg_fin.reshape(BATCH, 1, 2, RG_W)

    l, o = 1, 0
    mods = mods_all[l]
    qk = RET_HEADS * RET_DK
    w_c = w_in_c[o].astype(BF16)
    cos_r, sin_r = _rope_tables_ret()
    q_r, k_r, v_r, g_r = _proj_c(x, mods, cos_r, sin_r, w_c[:, :qk], w_c[:, qk:2 * qk],
                                 w_c[:, 2 * qk:2 * qk + MIX_C], w_c[:, 2 * qk + MIX_C:])
    gam = jnp.broadcast_to(ret_gamma_logit[o].astype(F32)[:, :, None, None], (2, RET_HEADS, SUBLANE, LANE))
    r0_p = jnp.zeros((BATCH, 2, RET_HEADS, RET_DK, RET_DV), F32)
    o_ret, r_fin = _retention(q_r, k_r, v_r, gam, r0_p, jnp.zeros((N_TOK, MIX_C), F32),
                              n_seq=BATCH, seq=SEQ, row_block0=0, with_state=True)
    (o_ret,) = _retention(q_r, k_r, v_r, gam, state_ret[:, o], o_ret,
                          n_seq=DEC_BATCH, seq=DEC_SEQ, row_block0=N_PROMPT // DEC_SEQ, with_state=False)
    x1, h2, lgt = _out_c(o_ret, g_r, w_out_c[o].astype(BF16), x, mods,
                         ln_g[l, 0].reshape(1, D_MODEL), ln_b[l, 0].reshape(1, D_MODEL), wr_hi[l], wr_lo[l])
    x = _moe_and_norm(x1, h2, lgt, mods, ln_g[l, 1].reshape(1, D_MODEL), ln_b[l, 1].reshape(1, D_MODEL),
                      router_bias[l], tri, wg_e[l], wu_e[l], wd_e[l], wsg[l], wsu[l], wsd[l])

    y_prompt = x[:N_PROMPT].reshape(BATCH, SEQ, D_MODEL)
    y_sample = x[N_PROMPT:].reshape(DEC_BATCH, DEC_SEQ, D_MODEL)
    new_ret = r_fin.reshape(BATCH, 1, 2, RET_HEADS, RET_DK, RET_DV)
    return (y_prompt, y_sample, new_ckv, new_krope, new_rg, new_ret)
```

```python
import functools
import math

import jax
import jax.numpy as jnp
from jax import lax
from jax.experimental import pallas as pl
from jax.experimental.pallas import tpu as pltpu

F32 = jnp.float32
BF16 = jnp.bfloat16
I32 = jnp.int32

D_MODEL = 1024
BATCH, SEQ = 16, 256
DEC_BATCH, DEC_SEQ = 8, 4096
PAST_LEN = 256
DEPTH = 2
GRID_W = 64
RG_W, RG_BLOCKS = 512, 8
RG_BW = RG_W // RG_BLOCKS
RG_C = 8.0
CONV_W, CONV_LEFT = 4, 2
MLA_HEADS, QK_NOPE, QK_ROPE, V_HEAD = 8, 64, 32, 64
Q_LORA, KV_LORA = 768, 256
ROPE_BASE = 10000.0
ATTN_SCALE = (QK_NOPE + QK_ROPE) ** -0.5
RET_HEADS, RET_DK, RET_DV, RET_CHUNK = 4, 256, 512, 128
MIX_C = RET_HEADS * RET_DV
N_EXPERTS, TOP_K, N_GROUPS, TOPK_GROUPS = 64, 8, 8, 4
GROUP_SIZE = N_EXPERTS // N_GROUPS
D_EXPERT = 256
ROUTED_SCALE = 2.5
ALPHA = (2 * DEPTH) ** 0.25
EPS = 1e-6

N_PROMPT = BATCH * SEQ
N_SAMPLE = DEC_BATCH * DEC_SEQ
N_TOK = N_PROMPT + N_SAMPLE
N_COND = 1 + DEC_BATCH
MOD_ROWS = 8

LANE = 128
SUBLANE = 8
TM = 512
HEAD_PAD = 128
ROPE_LANE0 = QK_NOPE
TQ = 256
KC = 256
SCAN_ROWS = 64
GATE_ROWS = 256
WIN = 256
N_WIN = N_TOK // WIN
PIECE = 16
SORT_ROWS = 256
RL = 3072
TME = 512
PIECES = TME // PIECE
N_PAIRS = N_TOK * TOP_K
N_EBLOCKS = (N_PAIRS + N_WIN * N_EXPERTS * (PIECE - 1)) // TME + N_EXPERTS
NEG = -1e30


def _cparams(sem, vmem_mb=48):
    return pltpu.CompilerParams(dimension_semantics=sem, vmem_limit_bytes=vmem_mb * 1024 * 1024)


def _cond_block(i, tm):
    npb = N_PROMPT // tm
    return jnp.where(i < npb, 0, 1 + (i - npb) // (DEC_SEQ // tm))


def _pos_block(i, tm):
    npb = N_PROMPT // tm
    return jnp.where(i < npb, 0, 1 + (i - npb) % (DEC_SEQ // tm))


def _split_hi_lo(a):
    hi = a.astype(BF16)
    lo = (a - hi.astype(F32)).astype(BF16)
    return hi, lo


def _dot(a, b):
    return jnp.dot(a, b, preferred_element_type=F32)


def _dot_nt(a, b):
    return lax.dot_general(a, b, (((1,), (1,)), ((), ())), preferred_element_type=F32)


def _dot_tn(a, b):
    return lax.dot_general(a, b, (((0,), (0,)), ((), ())), preferred_element_type=F32)


def _silu(x):
    return x * jax.nn.sigmoid(x)


def _gelu_tanh(x):
    return 0.5 * x * (1.0 + jnp.tanh(math.sqrt(2.0 / math.pi) * (x + 0.044715 * (x * x * x))))


def _softplus(x):
    return jnp.maximum(x, 0.0) + jnp.log1p(jnp.exp(-jnp.abs(x)))


def _layernorm_rows(z, g, b):
    mu = jnp.mean(z, axis=-1, keepdims=True)
    zc = z - mu
    var = jnp.mean(zc * zc, axis=-1, keepdims=True)
    return (zc * lax.rsqrt(var + EPS)) * g + b


def _ada_kernel(c_ref, w_ref, b_ref, o_ref):
    s_hi, s_lo = _split_hi_lo(_silu(c_ref[...]))
    w_hi, w_lo = _split_hi_lo(w_ref[...])
    o_ref[...] = _dot(s_hi, w_hi) + _dot(s_hi, w_lo) + _dot(s_lo, w_hi) + b_ref[...]


def _ada_modulation(cond, w_ada, b_ada):
    n6 = 6 * D_MODEL
    tn = D_MODEL
    return pl.pallas_call(
        _ada_kernel,
        out_shape=jax.ShapeDtypeStruct((DEPTH, 16, n6), F32),
        grid=(DEPTH, n6 // tn),
        in_specs=[
            pl.BlockSpec((16, D_MODEL), lambda l, j: (0, 0)),
            pl.BlockSpec((None, D_MODEL, tn), lambda l, j: (l, 0, j)),
            pl.BlockSpec((None, 1, tn), lambda l, j: (l, 0, j)),
        ],
        out_specs=pl.BlockSpec((None, 16, tn), lambda l, j: (l, 0, j)),
        compiler_params=_cparams(("arbitrary", "arbitrary")),
        name="ada_modulation",
    )(cond, w_ada, b_ada.reshape(DEPTH, 1, n6))


def _modulated(x, mod_ref, shift_row, scale_row):
    return x * (1.0 + mod_ref[scale_row:scale_row + 1, :]) + mod_ref[shift_row:shift_row + 1, :]


def _proj_ab_kernel(x_ref, mod_ref, w_ref, wkr_ref, main_ref, kr_ref):
    h = _modulated(x_ref[...], mod_ref, 0, 1).astype(BF16)
    n = w_ref.shape[1]
    step = 512
    for j in range(n // step):
        main_ref[:, j * step:(j + 1) * step] = _dot(h, w_ref[:, j * step:(j + 1) * step])
    kr_ref[...] = _dot(h, wkr_ref[...])


def _proj_ab(x, mods, w_main, w_kr):
    n = w_main.shape[1]
    return pl.pallas_call(
        _proj_ab_kernel,
        out_shape=(jax.ShapeDtypeStruct((N_TOK, n), F32), jax.ShapeDtypeStruct((N_TOK, LANE), F32)),
        grid=(N_TOK // TM,),
        in_specs=[
            pl.BlockSpec((TM, D_MODEL), lambda i: (i, 0)),
            pl.BlockSpec((None, MOD_ROWS, D_MODEL), lambda i: (_cond_block(i, TM), 0, 0)),
            pl.BlockSpec((D_MODEL, n), lambda i: (0, 0)),
            pl.BlockSpec((D_MODEL, LANE), lambda i: (0, 0)),
        ],
        out_specs=(pl.BlockSpec((TM, n), lambda i: (i, 0)), pl.BlockSpec((TM, LANE), lambda i: (i, 0))),
        compiler_params=_cparams(("arbitrary",)),
        name="proj_ab",
    )(x, mods, w_main, w_kr)


def _rglru_kernel(xr_ref, gr_ref, cw_ref, cb_ref, wg_ref, bg_ref, lam_ref, h0_ref, yprev_ref,
                  y_ref, hfin_ref, xpad, a_s, b_s, *, seq):
    del yprev_ref
    pad = SUBLANE
    xpad[0:pad, :] = jnp.zeros((pad, LANE), F32)
    xpad[seq + pad:seq + 2 * pad, :] = jnp.zeros((pad, LANE), F32)
    xpad[pad:seq + pad, :] = xr_ref[...]

    sp = _softplus(-lam_ref[...])
    cw = cw_ref[...]
    cb = cb_ref[...]
    wg = wg_ref[...]
    bg = bg_ref[...]

    def gate_step(c, carry):
        t0 = pl.multiple_of(c * GATE_ROWS, GATE_ROWS)
        win = xpad[pl.ds(t0, GATE_ROWS + 2 * pad), :]
        xc = cb
        for j in range(CONV_W):
            off = pad - CONV_LEFT + j
            xc = xc + win[off:off + GATE_ROWS, :] * cw[j:j + 1, :]
        g = _dot(xc.astype(BF16), wg) + bg
        for d in range(2):
            r = jax.nn.sigmoid(g[:, (2 * d) * LANE:(2 * d + 1) * LANE])
            i = jax.nn.sigmoid(g[:, (2 * d + 1) * LANE:(2 * d + 2) * LANE])
            log_a = (-RG_C * r) * sp[d:d + 1, :]
            a = jnp.exp(log_a)
            t = jnp.tanh(log_a)
            bt = jnp.sqrt(2.0 * t / (t - 1.0)) * (i * xc)
            a_s[d, pl.ds(t0, GATE_ROWS), :] = a
            b_s[d, pl.ds(t0, GATE_ROWS), :] = bt
        return carry

    lax.fori_loop(0, seq // GATE_ROWS, gate_step, 0)

    row = lax.broadcasted_iota(I32, (SCAN_ROWS, LANE), 0) % SUBLANE
    n_steps = seq // SCAN_ROWS
    tiles = SCAN_ROWS // SUBLANE

    def local_scan(a, b, reverse):
        for k in (1, 2, 4):
            if reverse:
                ok = row < SUBLANE - k
                shift = SCAN_ROWS - k
            else:
                ok = row >= k
                shift = k
            a_sh = jnp.where(ok, pltpu.roll(a, shift, 0), 1.0)
            b_sh = jnp.where(ok, pltpu.roll(b, shift, 0), 0.0)
            b = a * b_sh + b
            a = a * a_sh
        return a, b

    def fwd_step(c, h):
        t0 = pl.multiple_of(c * SCAN_ROWS, SCAN_ROWS)
        a, b = local_scan(a_s[0, pl.ds(t0, SCAN_ROWS), :], b_s[0, pl.ds(t0, SCAN_ROWS), :], False)
        outs = []
        for j in range(tiles):
            hj = a[j * SUBLANE:(j + 1) * SUBLANE, :] * h + b[j * SUBLANE:(j + 1) * SUBLANE, :]
            outs.append(hj)
            h = hj[SUBLANE - 1:SUBLANE, :]
        y_ref[pl.ds(t0, SCAN_ROWS), :] = jnp.concatenate(outs, axis=0)
        return h

    h_f = lax.fori_loop(0, n_steps, fwd_step, h0_ref[0:1, :])

    def bwd_step(c, h):
        t0 = pl.multiple_of((n_steps - 1 - c) * SCAN_ROWS, SCAN_ROWS)
        a, b = local_scan(a_s[1, pl.ds(t0, SCAN_ROWS), :], b_s[1, pl.ds(t0, SCAN_ROWS), :], True)
        outs = [None] * tiles
        for j in reversed(range(tiles)):
            hj = a[j * SUBLANE:(j + 1) * SUBLANE, :] * h + b[j * SUBLANE:(j + 1) * SUBLANE, :]
            outs[j] = hj
            h = hj[0:1, :]
        hb = jnp.concatenate(outs, axis=0)
        y_ref[pl.ds(t0, SCAN_ROWS), :] = (y_ref[pl.ds(t0, SCAN_ROWS), :] + hb) * _gelu_tanh(gr_ref[pl.ds(t0, SCAN_ROWS), :])
        return h

    h_b = lax.fori_loop(0, n_steps, bwd_step, h0_ref[1:2, :])
    hfin_ref[0:1, :] = h_f
    hfin_ref[1:2, :] = h_b


def _rglru(main, y_prev, cw, cb, wg, bg, lam, h0, *, n_seq, seq, row_block0):
    n_ct = RG_W // LANE
    gr_col0 = RG_W // LANE
    kern = functools.partial(_rglru_kernel, seq=seq)
    return pl.pallas_call(
        kern,
        out_shape=(jax.ShapeDtypeStruct((N_TOK, RG_W), F32), jax.ShapeDtypeStruct((n_seq, 2, RG_W), F32)),
        grid=(n_seq, n_ct),
        in_specs=[
            pl.BlockSpec((seq, LANE), lambda b, c: (row_block0 + b, c)),
            pl.BlockSpec((seq, LANE), lambda b, c: (row_block0 + b, gr_col0 + c)),
            pl.BlockSpec((CONV_W, LANE), lambda b, c: (0, c)),
            pl.BlockSpec((1, LANE), lambda b, c: (0, c)),
            pl.BlockSpec((None, LANE, 4 * LANE), lambda b, c: (c, 0, 0)),
            pl.BlockSpec((None, 1, 4 * LANE), lambda b, c: (c, 0, 0)),
            pl.BlockSpec((2, LANE), lambda b, c: (0, c)),
            pl.BlockSpec((None, 2, LANE), lambda b, c: (b, 0, c)),
            pl.BlockSpec(memory_space=pl.ANY),
        ],
        out_specs=(
            pl.BlockSpec((seq, LANE), lambda b, c: (row_block0 + b, c)),
            pl.BlockSpec((None, 2, LANE), lambda b, c: (b, 0, c)),
        ),
        scratch_shapes=[
            pltpu.VMEM((seq + 2 * SUBLANE, LANE), F32),
            pltpu.VMEM((2, seq, LANE), F32),
            pltpu.VMEM((2, seq, LANE), F32),
        ],
        input_output_aliases={8: 0},
        compiler_params=_cparams(("arbitrary", "arbitrary")),
        name=f"rglru_s{seq}",
    )(main, main, cw, cb, wg, bg, lam, h0, y_prev)


def _rope_lanes(x, cos, sin_m, sin_p):
    n = x.shape[1] // LANE
    half = QK_ROPE // 4
    cos_t = jnp.concatenate([cos] * n, axis=1) if n > 1 else cos
    sm_t = jnp.concatenate([sin_m] * n, axis=1) if n > 1 else sin_m
    sp_t = jnp.concatenate([sin_p] * n, axis=1) if n > 1 else sin_p
    up = pltpu.roll(x, x.shape[1] - half, 1)
    dn = pltpu.roll(x, half, 1)
    return x * cos_t + up * sm_t + dn * sp_t


def _mla_prep_kernel(cq0_ref, cq1_ref, cq2_ref, ckv_ref, kr_ref, cos_ref, sm_ref, sp_ref,
                     qn_ref, wq_ref, kvn_ref, wuk_ref, wuvt_ref,
                     q_ref, k_ref, vt_ref, ckvn_ref):
    cq = [cq0_ref[...], cq1_ref[...], cq2_ref[...]]
    ms = (jnp.sum(cq[0] * cq[0], axis=-1, keepdims=True) + jnp.sum(cq[1] * cq[1], axis=-1, keepdims=True)
          + jnp.sum(cq[2] * cq[2], axis=-1, keepdims=True)) * (1.0 / Q_LORA)
    inv = lax.rsqrt(ms + EPS)
    blk = Q_LORA // 3
    q = None
    for j in range(3):
        cqn = ((cq[j] * inv) * qn_ref[:, j * blk:(j + 1) * blk]).astype(BF16)
        part = _dot(cqn, wq_ref[j * blk:(j + 1) * blk, :])
        q = part if q is None else q + part
    cos, sm, sp = cos_ref[...], sm_ref[...], sp_ref[...]
    q_ref[...] = _rope_lanes(q, cos, sm, sp).astype(BF16)

    ckv = ckv_ref[...]
    inv_kv = lax.rsqrt(jnp.mean(ckv * ckv, axis=-1, keepdims=True) + EPS)
    ckvn = (ckv * inv_kv) * kvn_ref[...]
    ckvn_ref[...] = ckvn
    ckvn_b = ckvn.astype(BF16)
    kr_rot = _rope_lanes(kr_ref[...], cos, sm, sp)
    k_ref[...] = (_dot(ckvn_b, wuk_ref[...]) + jnp.concatenate([kr_rot] * MLA_HEADS, axis=1)).astype(BF16)
    vt = _dot_nt(wuvt_ref[...], ckvn_b).astype(BF16)
    for c in range(vt_ref.shape[0]):
        vt_ref[c] = vt[:, c * KC:(c + 1) * KC]


def _mla_prep(main, krp, cos_t, sm_t, sp_t, q_norm, wq_p, kv_norm, wuk_p, wuvt):
    cq_col0 = 2 * RG_W // 256
    hp = MLA_HEADS * HEAD_PAD
    full = lambda shape: pl.BlockSpec(shape, lambda i: (0,) * len(shape))
    tab = pl.BlockSpec((TM, LANE), lambda i: (_pos_block(i, TM), 0))
    return pl.pallas_call(
        _mla_prep_kernel,
        out_shape=(
            jax.ShapeDtypeStruct((N_TOK, hp), BF16),
            jax.ShapeDtypeStruct((N_TOK, hp), BF16),
            jax.ShapeDtypeStruct((N_TOK // KC, MLA_HEADS * V_HEAD, KC), BF16),
            jax.ShapeDtypeStruct((N_TOK, KV_LORA), F32),
        ),
        grid=(N_TOK // TM,),
        in_specs=[
            pl.BlockSpec((TM, 256), lambda i: (i, cq_col0)),
            pl.BlockSpec((TM, 256), lambda i: (i, cq_col0 + 1)),
            pl.BlockSpec((TM, 256), lambda i: (i, cq_col0 + 2)),
            pl.BlockSpec((TM, 256), lambda i: (i, cq_col0 + 3)),
            pl.BlockSpec((TM, LANE), lambda i: (i, 0)),
            tab, tab, tab,
            full((1, Q_LORA)), full((Q_LORA, hp)), full((1, KV_LORA)), full((KV_LORA, hp)),
            full((MLA_HEADS * V_HEAD, KV_LORA)),
        ],
        out_specs=(
            pl.BlockSpec((TM, hp), lambda i: (i, 0)),
            pl.BlockSpec((TM, hp), lambda i: (i, 0)),
            pl.BlockSpec((TM // KC, MLA_HEADS * V_HEAD, KC), lambda i: (i, 0, 0)),
            pl.BlockSpec((TM, KV_LORA), lambda i: (i, 0)),
        ),
        compiler_params=_cparams(("arbitrary",)),
        name="mla_prep",
    )(main, main, main, main, krp, cos_t, sm_t, sp_t, q_norm, wq_p, kv_norm, wuk_p, wuvt)


def _mla_ctx_kernel(ckv_ref, kr_ref, wuk_ref, wuvt_ref, k_ref, vt_ref):
    ckv_b = ckv_ref[...].astype(BF16)
    k_ref[...] = (_dot(ckv_b, wuk_ref[...]) + jnp.concatenate([kr_ref[...]] * MLA_HEADS, axis=1)).astype(BF16)
    vt_ref[...] = _dot_nt(wuvt_ref[...], ckv_b).astype(BF16)


def _mla_ctx(ctx_ckv, ctx_krp, wuk_p, wuvt):
    n = ctx_ckv.shape[0]
    hp = MLA_HEADS * HEAD_PAD
    tm = KC
    full = lambda shape: pl.BlockSpec(shape, lambda i: (0,) * len(shape))
    return pl.pallas_call(
        _mla_ctx_kernel,
        out_shape=(jax.ShapeDtypeStruct((n, hp), BF16),
                   jax.ShapeDtypeStruct((n // tm, MLA_HEADS * V_HEAD, tm), BF16)),
        grid=(n // tm,),
        in_specs=[
            pl.BlockSpec((tm, KV_LORA), lambda i: (i, 0)),
            pl.BlockSpec((tm, LANE), lambda i: (i, 0)),
            full((KV_LORA, hp)), full((MLA_HEADS * V_HEAD, KV_LORA)),
        ],
        out_specs=(pl.BlockSpec((tm, hp), lambda i: (i, 0)),
                   pl.BlockSpec((None, MLA_HEADS * V_HEAD, tm), lambda i: (i, 0, 0))),
        compiler_params=_cparams(("arbitrary",)),
        name="mla_ctx",
    )(ctx_ckv, ctx_krp, wuk_p, wuvt)


def _attn_kernel(*refs, seq, tq, n_ctx):
    if n_ctx:
        q_ref, k_ref, vt_ref, kc_ref, vtc_ref, oprev_ref, o_ref, s_scr, p_scr = refs
    else:
        q_ref, k_ref, vt_ref, oprev_ref, o_ref, s_scr, p_scr = refs
    del oprev_ref
    has_ctx = 1 if n_ctx else 0
    n_own = seq // KC
    n = n_own + has_ctx
    c_exp = ATTN_SCALE * math.log2(math.e)
    qs = [q_ref[:, h * HEAD_PAD:(h + 1) * HEAD_PAD] for h in range(2)]

    def k_chunk(c, h):
        if has_ctx and isinstance(c, int) and c == n_own:
            return kc_ref[:, h * HEAD_PAD:(h + 1) * HEAD_PAD]
        t0 = c * KC if isinstance(c, int) else pl.multiple_of(c * KC, KC)
        return k_ref[pl.ds(t0, KC), h * HEAD_PAD:(h + 1) * HEAD_PAD]

    def v_chunk(c, h):
        if has_ctx and isinstance(c, int) and c == n_own:
            return vtc_ref[h * V_HEAD:(h + 1) * V_HEAD, :]
        return vt_ref[c, h * V_HEAD:(h + 1) * V_HEAD, :]

    def scores(c, slot):
        for h in range(2):
            s_scr[slot, h] = _dot_nt(k_chunk(c, h), qs[h])

    def softmax_chunk(slot, st):
        out = []
        for h in range(2):
            m, l, _, acc = st[h]
            t = s_scr[slot, h] * c_exp
            m_new = jnp.maximum(m, jnp.max(t, axis=0, keepdims=True))
            alpha = jnp.exp2(m - m_new)
            p = jnp.exp2(t - m_new)
            p_scr[slot, h] = p.astype(BF16)
            out.append((m_new, alpha * l + jnp.sum(p, axis=0, keepdims=True), alpha, acc))
        return out

    def weighted_values(c, slot, st, alphas):
        return [(st[h][0], st[h][1], st[h][2], alphas[h] * st[h][3] + _dot(v_chunk(c, h), p_scr[slot, h]))
                for h in range(2)]

    def step(c, slot, st, with_s, with_v):
        if with_s:
            scores(c + 1, 1 - slot)
        alphas = [st[h][2] for h in range(2)]
        st = softmax_chunk(slot, st)
        if with_v:
            prev = max(c - 1, 0) if isinstance(c, int) else jnp.maximum(c - 1, 0)
            st = weighted_values(prev, 1 - slot, st, alphas)
        return st

    p_scr[1] = jnp.zeros(p_scr.shape[1:], BF16)
    st = [(jnp.full((1, tq), NEG, F32), jnp.zeros((1, tq), F32), jnp.ones((1, tq), F32),
           jnp.zeros((V_HEAD, tq), F32)) for _ in range(2)]
    scores(0, 0)
    n_pairs = max(n_own - 1, 0) // 2

    def pair(j, flat):
        st = [tuple(flat[0:4]), tuple(flat[4:8])]
        st = step(2 * j, 0, st, True, True)
        st = step(2 * j + 1, 1, st, True, True)
        return tuple(st[0]) + tuple(st[1])

    if n_pairs:
        flat = lax.fori_loop(0, n_pairs, pair, tuple(st[0]) + tuple(st[1]))
        st = [tuple(flat[0:4]), tuple(flat[4:8])]
    for c in range(2 * n_pairs, n):
        st = step(c, c % 2, st, c + 1 < n, c > 0)
    st = weighted_values(n - 1, (n - 1) % 2, st, [st[h][2] for h in range(2)])
    for h in range(2):
        o_ref[:, h * V_HEAD:(h + 1) * V_HEAD] = (st[h][3] / st[h][1]).T.astype(o_ref.dtype)


def _attention(q, k, vt, o_prev, kc, vtc, *, n_seq, seq, row_block0, tq):
    n_ctx = 0 if kc is None else PAST_LEN
    n_hp = MLA_HEADS // 2
    nq = seq // tq
    kern = functools.partial(_attn_kernel, seq=seq, tq=tq, n_ctx=n_ctx)
    in_specs = [
        pl.BlockSpec((tq, 2 * HEAD_PAD), lambda b, j, i: ((row_block0 + b) * nq + i, j)),
        pl.BlockSpec((seq, 2 * HEAD_PAD), lambda b, j, i: (row_block0 + b, j)),
        pl.BlockSpec((seq // KC, 2 * V_HEAD, KC), lambda b, j, i: (row_block0 + b, j, 0)),
    ]
    args = [q, k, vt]
    if n_ctx:
        in_specs += [
            pl.BlockSpec((n_ctx, 2 * HEAD_PAD), lambda b, j, i: (b, j)),
            pl.BlockSpec((None, 2 * V_HEAD, KC), lambda b, j, i: (b, j, 0)),
        ]
        args += [kc, vtc]
    in_specs.append(pl.BlockSpec(memory_space=pl.ANY))
    args.append(o_prev)
    return pl.pallas_call(
        kern,
        out_shape=jax.ShapeDtypeStruct((N_TOK, MLA_HEADS * V_HEAD), BF16),
        grid=(n_seq, n_hp, nq),
        in_specs=in_specs,
        out_specs=pl.BlockSpec((tq, 2 * V_HEAD), lambda b, j, i: ((row_block0 + b) * nq + i, j)),
        scratch_shapes=[pltpu.VMEM((2, 2, KC, tq), F32), pltpu.VMEM((2, 2, KC, tq), BF16)],
        input_output_aliases={len(args) - 1: 0},
        compiler_params=_cparams(("arbitrary", "arbitrary", "arbitrary")),
        name=f"mla_attention_s{seq}",
    )(*args)


def _post_mixer(y, x_ref, mod_ref, lng_ref, lnb_ref, wrh_ref, wrl_ref, x1_ref, h2_ref, lgt_ref):
    z = ALPHA * x_ref[...] + mod_ref[2:3, :] * y
    x1 = _layernorm_rows(z, lng_ref[...], lnb_ref[...])
    x1_ref[...] = x1
    h2 = _modulated(x1, mod_ref, 3, 4)
    h_hi, h_lo = _split_hi_lo(h2)
    h2_ref[...] = h_hi
    w_hi, w_lo = wrh_ref[...], wrl_ref[...]
    lgt_ref[...] = _dot_nt(w_hi, h_hi) + _dot_nt(w_hi, h_lo) + _dot_nt(w_lo, h_hi)


def _out_ab_kernel(yrg_ref, o_ref, wa_ref, wb_ref, x_ref, mod_ref, lng_ref, lnb_ref, wrh_ref, wrl_ref,
                   x1_ref, h2_ref, lgt_ref):
    y = _dot(yrg_ref[...].astype(BF16), wa_ref[...]) + _dot(o_ref[...], wb_ref[...])
    _post_mixer(y, x_ref, mod_ref, lng_ref, lnb_ref, wrh_ref, wrl_ref, x1_ref, h2_ref, lgt_ref)


def _post_specs():
    full = lambda shape: pl.BlockSpec(shape, lambda i: (0,) * len(shape))
    in_specs = [
        pl.BlockSpec((TM, D_MODEL), lambda i: (i, 0)),
        pl.BlockSpec((None, MOD_ROWS, D_MODEL), lambda i: (_cond_block(i, TM), 0, 0)),
        full((1, D_MODEL)), full((1, D_MODEL)),
        full((N_EXPERTS, D_MODEL)), full((N_EXPERTS, D_MODEL)),
    ]
    out_shape = (
        jax.ShapeDtypeStruct((N_TOK, D_MODEL), F32),
        jax.ShapeDtypeStruct((N_TOK, D_MODEL), BF16),
        jax.ShapeDtypeStruct((N_EXPERTS, N_TOK), F32),
    )
    out_specs = (
        pl.BlockSpec((TM, D_MODEL), lambda i: (i, 0)),
        pl.BlockSpec((TM, D_MODEL), lambda i: (i, 0)),
        pl.BlockSpec((N_EXPERTS, TM), lambda i: (0, i)),
    )
    return in_specs, out_shape, out_specs


def _out_ab(yrg, o, w_a, w_b, x, mods, lng, lnb, wr_hi, wr_lo):
    full = lambda shape: pl.BlockSpec(shape, lambda i: (0,) * len(shape))
    post_in, out_shape, out_specs = _post_specs()
    return pl.pallas_call(
        _out_ab_kernel,
        out_shape=out_shape,
        grid=(N_TOK // TM,),
        in_specs=[
            pl.BlockSpec((TM, RG_W), lambda i: (i, 0)),
            pl.BlockSpec((TM, MLA_HEADS * V_HEAD), lambda i: (i, 0)),
            full((RG_W, D_MODEL)), full((MLA_HEADS * V_HEAD, D_MODEL)),
        ] + post_in,
        out_specs=out_specs,
        compiler_params=_cparams(("arbitrary",)),
        name="out_ab",
    )(yrg, o, w_a, w_b, x, mods, lng, lnb, wr_hi, wr_lo)


def _proj_c_kernel(x_ref, mod_ref, cos_ref, sin_ref, wq_ref, wk_ref, wv_ref, wg_ref, q_ref, k_ref, v_ref, g_ref):
    h = _modulated(x_ref[...], mod_ref, 0, 1).astype(BF16)
    cos, sin = cos_ref[...], sin_ref[...]
    half = RET_DK // 2
    for hd in range(RET_HEADS):
        for w_ref, o_ref, scale in ((wq_ref, q_ref, 1.0), (wk_ref, k_ref, RET_DK ** -0.5)):
            p = _dot(h, w_ref[:, hd * RET_DK:(hd + 1) * RET_DK])
            x1, x2 = p[:, :half], p[:, half:]
            r1 = x1 * cos - x2 * sin
            r2 = x2 * cos + x1 * sin
            if scale != 1.0:
                r1, r2 = r1 * scale, r2 * scale
            o_ref[:, hd * RET_DK:hd * RET_DK + half] = r1.astype(BF16)
            o_ref[:, hd * RET_DK + half:(hd + 1) * RET_DK] = r2.astype(BF16)
    step = 512
    for j in range(MIX_C // step):
        v_ref[:, j * step:(j + 1) * step] = _dot(h, wv_ref[:, j * step:(j + 1) * step]).astype(BF16)
        g_ref[:, j * step:(j + 1) * step] = _dot(h, wg_ref[:, j * step:(j + 1) * step])


def _proj_c(x, mods, cos_t, sin_t, wq, wk, wv, wg):
    full = lambda shape: pl.BlockSpec(shape, lambda i: (0,) * len(shape))
    qk = RET_HEADS * RET_DK
    tab = pl.BlockSpec((TM, RET_DK // 2), lambda i: (_pos_block(i, TM), 0))
    return pl.pallas_call(
        _proj_c_kernel,
        out_shape=(
            jax.ShapeDtypeStruct((N_TOK, qk), BF16), jax.ShapeDtypeStruct((N_TOK, qk), BF16),
            jax.ShapeDtypeStruct((N_TOK, MIX_C), BF16), jax.ShapeDtypeStruct((N_TOK, MIX_C), F32),
        ),
        grid=(N_TOK // TM,),
        in_specs=[
            pl.BlockSpec((TM, D_MODEL), lambda i: (i, 0)),
            pl.BlockSpec((None, MOD_ROWS, D_MODEL), lambda i: (_cond_block(i, TM), 0, 0)),
            tab, tab,
            full((D_MODEL, qk)), full((D_MODEL, qk)), full((D_MODEL, MIX_C)), full((D_MODEL, MIX_C)),
        ],
        out_specs=(
            pl.BlockSpec((TM, qk), lambda i: (i, 0)), pl.BlockSpec((TM, qk), lambda i: (i, 0)),
            pl.BlockSpec((TM, MIX_C), lambda i: (i, 0)), pl.BlockSpec((TM, MIX_C), lambda i: (i, 0)),
        ),
        compiler_params=_cparams(("arbitrary",), 56),
        name="proj_c",
    )(x, mods, cos_t, sin_t, wq, wk, wv, wg)


def _retention_kernel(*refs, seq, with_state):
    if with_state:
        q_ref, k_ref, v_ref, gam_ref, r0_ref, oprev_ref, o_ref, rfin_ref, r_s = refs
    else:
        q_ref, k_ref, v_ref, gam_ref, r0_ref, oprev_ref, o_ref, r_s = refs
        rfin_ref = None
    del oprev_ref
    c = RET_CHUNK
    n = seq // c
    ii = lax.broadcasted_iota(I32, (c, c), 0).astype(F32)
    jj = lax.broadcasted_iota(I32, (c, c), 1).astype(F32)
    ci = lax.broadcasted_iota(I32, (c, 1), 0).astype(F32)

    for d in range(2):
        gam = gam_ref[d]
        lg_row = -_softplus(-gam[0:1, :])
        lg = jnp.broadcast_to(lg_row, (c, c))
        lg_col = jnp.broadcast_to(lg_row[:, 0:1], (c, 1))
        if d == 0:
            diff = ii - jj
            xi = jnp.exp((ci + 1.0) * lg_col)
            zeta = jnp.exp((c - 1.0 - ci) * lg_col)
        else:
            diff = jj - ii
            xi = jnp.exp((c - ci) * lg_col)
            zeta = jnp.exp(ci * lg_col)
        dmat = jnp.where(diff >= 0, jnp.exp(jnp.maximum(diff, 0.0) * lg), 0.0)
        g_chunk = jnp.exp(float(c) * lg_row[:, 0:1])
        r_s[...] = r0_ref[d]

        def chunk(s, carry, d=d, dmat=dmat, xi=xi, zeta=zeta, g_chunk=g_chunk):
            idx = s if d == 0 else n - 1 - s
            t0 = pl.multiple_of(idx * c, c)
            qb = q_ref[pl.ds(t0, c), :]
            kb = k_ref[pl.ds(t0, c), :]
            vb = v_ref[pl.ds(t0, c), :]
            r = r_s[...]
            inner = _dot_nt(qb, kb) * dmat
            o = _dot(inner.astype(BF16), vb) + _dot((qb.astype(F32) * xi).astype(BF16), r.astype(BF16))
            r_s[...] = r * g_chunk + _dot_tn((kb.astype(F32) * zeta).astype(BF16), vb)
            if d == 0:
                o_ref[pl.ds(t0, c), :] = o
            else:
                o_ref[pl.ds(t0, c), :] = o_ref[pl.ds(t0, c), :] + o
            return carry

        lax.fori_loop(0, n, chunk, 0)
        if with_state:
            rfin_ref[d] = r_s[...]


def _retention(q, k, v, gam, r0, o_prev, *, n_seq, seq, row_block0, with_state):
    kern = functools.partial(_retention_kernel, seq=seq, with_state=with_state)
    out_shape = [jax.ShapeDtypeStruct((N_TOK, MIX_C), F32)]
    out_specs = [pl.BlockSpec((seq, RET_DV), lambda b, h: (row_block0 + b, h))]
    if with_state:
        out_shape.append(jax.ShapeDtypeStruct((n_seq, 2, RET_HEADS, RET_DK, RET_DV), F32))
        out_specs.append(pl.BlockSpec((None, 2, None, RET_DK, RET_DV), lambda b, h: (b, 0, h, 0, 0)))
    return pl.pallas_call(
        kern,
        out_shape=tuple(out_shape),
        grid=(n_seq, RET_HEADS),
        in_specs=[
            pl.BlockSpec((seq, RET_DK), lambda b, h: (row_block0 + b, h)),
            pl.BlockSpec((seq, RET_DK), lambda b, h: (row_block0 + b, h)),
            pl.BlockSpec((seq, RET_DV), lambda b, h: (row_block0 + b, h)),
            pl.BlockSpec((2, None, SUBLANE, LANE), lambda b, h: (0, h, 0, 0)),
            pl.BlockSpec((None, 2, None, RET_DK, RET_DV), lambda b, h: (b, 0, h, 0, 0)),
            pl.BlockSpec(memory_space=pl.ANY),
        ],
        out_specs=tuple(out_specs),
        scratch_shapes=[pltpu.VMEM((RET_DK, RET_DV), F32)],
        input_output_aliases={5: 0},
        compiler_params=_cparams(("arbitrary", "arbitrary"), 56),
        name=f"retention_s{seq}",
    )(q, k, v, gam, r0, o_prev)


def _out_c_kernel(o_ref, g_ref, w_ref, x_ref, mod_ref, lng_ref, lnb_ref, wrh_ref, wrl_ref,
                  x1_ref, h2_ref, lgt_ref):
    y = None
    for hd in range(RET_HEADS):
        o = o_ref[:, hd * RET_DV:(hd + 1) * RET_DV]
        mu = jnp.mean(o, axis=-1, keepdims=True)
        oc = o - mu
        var = jnp.mean(oc * oc, axis=-1, keepdims=True)
        on = oc * lax.rsqrt(var + EPS)
        a = (on * _silu(g_ref[:, hd * RET_DV:(hd + 1) * RET_DV])).astype(BF16)
        part = _dot(a, w_ref[hd * RET_DV:(hd + 1) * RET_DV, :])
        y = part if y is None else y + part
    _post_mixer(y, x_ref, mod_ref, lng_ref, lnb_ref, wrh_ref, wrl_ref, x1_ref, h2_ref, lgt_ref)


def _out_c(o, g, w, x, mods, lng, lnb, wr_hi, wr_lo):
    full = lambda shape: pl.BlockSpec(shape, lambda i: (0,) * len(shape))
    post_in, out_shape, out_specs = _post_specs()
    return pl.pallas_call(
        _out_c_kernel,
        out_shape=out_shape,
        grid=(N_TOK // TM,),
        in_specs=[
            pl.BlockSpec((TM, MIX_C), lambda i: (i, 0)),
            pl.BlockSpec((TM, MIX_C), lambda i: (i, 0)),
            full((MIX_C, D_MODEL)),
        ] + post_in,
        out_specs=out_specs,
        compiler_params=_cparams(("arbitrary",), 56),
        name="out_c",
    )(o, g, w, x, mods, lng, lnb, wr_hi, wr_lo)


def _route_kernel(lgt_ref, bias_ref, tri_ref, w_ref, lpos_ref, p16_ref):
    tt = lgt_ref.shape[1]
    scores = jax.nn.sigmoid(lgt_ref[...])
    sel = scores + bias_ref[...]
    srow = lax.broadcasted_iota(I32, (GROUP_SIZE, tt), 0).astype(F32)
    ninf = -jnp.inf

    gs = []
    for g in range(N_GROUPS):
        sg = sel[g * GROUP_SIZE:(g + 1) * GROUP_SIZE, :]
        m1 = jnp.max(sg, axis=0, keepdims=True)
        i1 = jnp.min(jnp.where(sg == m1, srow, float(GROUP_SIZE)), axis=0, keepdims=True)
        m2 = jnp.max(jnp.where(srow == i1, ninf, sg), axis=0, keepdims=True)
        gs.append(m1 + m2)
    gs = jnp.concatenate(gs, axis=0)
    chosen = jnp.zeros((N_GROUPS, tt), F32)
    for _ in range(TOPK_GROUPS):
        mg = jnp.max(gs, axis=0, keepdims=True)
        gi = jnp.min(jnp.where(gs == mg, srow, float(N_GROUPS)), axis=0, keepdims=True)
        hit = srow == gi
        chosen = jnp.where(hit, 1.0, chosen)
        gs = jnp.where(hit, ninf, gs)
    sel = jnp.concatenate(
        [jnp.where(jnp.broadcast_to(chosen[g:g + 1, :], (GROUP_SIZE, tt)) > 0.5,
                   sel[g * GROUP_SIZE:(g + 1) * GROUP_SIZE, :], ninf) for g in range(N_GROUPS)], axis=0)

    erow = lax.broadcasted_iota(I32, (N_EXPERTS, tt), 0).astype(F32)
    ids, ws = [], []
    for _ in range(TOP_K):
        m = jnp.max(sel, axis=0, keepdims=True)
        ei = jnp.min(jnp.where(sel == m, erow, float(N_EXPERTS)), axis=0, keepdims=True)
        hit = erow == ei
        ids.append(ei)
        ws.append(jnp.sum(jnp.where(hit, scores, 0.0), axis=0, keepdims=True))
        sel = jnp.where(hit, ninf, sel)
    wsum = ws[0]
    for k in range(1, TOP_K):
        wsum = wsum + ws[k]
    w_ref[...] = jnp.concatenate([w / wsum * ROUTED_SCALE for w in ws], axis=0)

    member_f = jnp.zeros((N_EXPERTS, tt), F32)
    for k in range(TOP_K):
        member_f = jnp.where(erow == ids[k], 1.0, member_f)
    member_b = member_f.astype(BF16)
    cnt_row = _dot_nt(jnp.ones((SUBLANE, tt), BF16), member_b)[0:1, :]
    p16_row = jnp.ceil(cnt_row * (1.0 / PIECE)) * PIECE
    lane_e = lax.broadcasted_iota(I32, (N_EXPERTS, N_EXPERTS), 1)
    sub_e = lax.broadcasted_iota(I32, (N_EXPERTS, N_EXPERTS), 0)
    run_start = jnp.sum(jnp.where(lane_e < sub_e, jnp.broadcast_to(p16_row, (N_EXPERTS, N_EXPERTS)), 0.0),
                        axis=1, keepdims=True)
    rank = _dot(member_b, tri_ref[...]) + run_start
    lpos_ref[...] = jnp.concatenate(
        [jnp.sum(jnp.where(erow == ids[k], rank, 0.0), axis=0, keepdims=True) for k in range(TOP_K)],
        axis=0).astype(I32)
    p16_ref[pl.ds(pl.program_id(0), 1), :] = jnp.concatenate(
        [p16_row, jnp.zeros((1, LANE - N_EXPERTS), F32)], axis=1)


def _route(lgt, bias, tri):
    return pl.pallas_call(
        _route_kernel,
        out_shape=(
            jax.ShapeDtypeStruct((TOP_K, N_TOK), F32), jax.ShapeDtypeStruct((TOP_K, N_TOK), I32),
            jax.ShapeDtypeStruct((N_WIN, LANE), F32),
        ),
        grid=(N_WIN,),
        in_specs=[
            pl.BlockSpec((N_EXPERTS, WIN), lambda i: (0, i)),
            pl.BlockSpec((N_EXPERTS, 1), lambda i: (0, 0)),
            pl.BlockSpec((WIN, WIN), lambda i: (0, 0)),
        ],
        out_specs=(
            pl.BlockSpec((TOP_K, WIN), lambda i: (0, i)), pl.BlockSpec((TOP_K, WIN), lambda i: (0, i)),
            pl.BlockSpec((N_WIN, LANE), lambda i: (0, 0)),
        ),
        compiler_params=_cparams(("arbitrary",)),
        name="moe_route",
    )(lgt, bias, tri)


def _sort_kernel(rw_ref, lpos_ref, h_ref, xl_ref):
    n_tiles = (rw_ref[pl.program_id(0)] + SORT_ROWS - 1) // SORT_ROWS
    x = h_ref[...]
    lp = lpos_ref[...]
    riota = lax.broadcasted_iota(I32, (SORT_ROWS, WIN), 0)

    def tile(j, carry):
        r = riota + j * SORT_ROWS
        oh = jnp.zeros((SORT_ROWS, WIN), F32)
        for k in range(TOP_K):
            oh = jnp.where(lp[k:k + 1, :] == r, 1.0, oh)
        xl_ref[pl.ds(pl.multiple_of(j * SORT_ROWS, SORT_ROWS), SORT_ROWS), :] = _dot(oh.astype(BF16), x).astype(BF16)
        return carry

    lax.fori_loop(0, n_tiles, tile, 0)

    def clear(j, carry):
        xl_ref[pl.ds(pl.multiple_of(j * SORT_ROWS, SORT_ROWS), SORT_ROWS), :] = jnp.zeros((SORT_ROWS, D_MODEL), BF16)
        return carry

    lax.fori_loop(n_tiles, RL // SORT_ROWS, clear, 0)


def _sort_rows(rw, lpos, h2):
    last = N_WIN - 1
    return pl.pallas_call(
        _sort_kernel,
        out_shape=jax.ShapeDtypeStruct(((N_WIN + 1) * RL, D_MODEL), BF16),
        grid_spec=pltpu.PrefetchScalarGridSpec(
            num_scalar_prefetch=1,
            grid=(N_WIN + 1,),
            in_specs=[
                pl.BlockSpec((TOP_K, WIN), lambda i, rw: (0, jnp.minimum(i, last))),
                pl.BlockSpec((WIN, D_MODEL), lambda i, rw: (jnp.minimum(i, last), 0)),
            ],
            out_specs=pl.BlockSpec((RL, D_MODEL), lambda i, rw: (i, 0)),
        ),
        compiler_params=_cparams(("arbitrary",)),
        name="moe_sort",
    )(rw, lpos, h2)


def _moe_tables(p16):
    n_w = jnp.arange(N_WIN, dtype=I32)
    run_start = jnp.cumsum(p16, axis=1) - p16
    rw = jnp.sum(p16, axis=1)
    cum_w = jnp.cumsum(p16, axis=0) - p16
    tot = jnp.sum(p16, axis=0)
    nblk = (tot + TME - 1) // TME
    blk_end = jnp.cumsum(nblk)
    blk0 = blk_end - nblk
    n_used = blk_end[-1]
    b = jnp.arange(N_EBLOCKS, dtype=I32)
    block_e = jnp.minimum(jnp.sum((blk_end[None, :] <= b[:, None]).astype(I32), axis=1), N_EXPERTS - 1)
    piece = jnp.arange(PIECES, dtype=I32) * PIECE
    rp = (b - blk0[block_e])[:, None] * TME + piece[None, :]
    valid = (b[:, None] < n_used) & (rp < tot[block_e][:, None])
    cum_e = cum_w.T[block_e]
    len_e = p16.T[block_e]
    start_e = run_start.T[block_e]
    in_win = (cum_e[:, None, :] <= rp[:, :, None]) & (rp[:, :, None] < (cum_e + len_e)[:, None, :])
    row = n_w[None, None, :] * RL + start_e[:, None, :] + rp[:, :, None] - cum_e[:, None, :]
    row = jnp.sum(jnp.where(in_win, row, 0), axis=2)
    zero_src = N_WIN * RL
    trash = N_WIN * RL + (2 + b % 2)[:, None] * TME + piece[None, :]
    gather_row = jnp.where(valid, row, zero_src).reshape(-1).astype(I32)
    scatter_row = jnp.where(valid, row, trash).reshape(-1).astype(I32)
    rw = jnp.concatenate([rw, jnp.zeros((1,), rw.dtype)])
    return rw.astype(I32), block_e.astype(I32), n_used.astype(I32).reshape(1), gather_row, scatter_row


def _expert_kernel(be_ref, nb_ref, grow_ref, srow_ref, xl_hbm, wg_ref, wu_ref, wd_ref, yl_hbm,
                   xbuf, ybuf, gsem, ssem):
    del be_ref
    b = pl.program_id(0)
    nb = nb_ref[0]

    def gather_start(blk, slot):
        for p in range(PIECES):
            src = pl.multiple_of(grow_ref[blk * PIECES + p], PIECE)
            pltpu.make_async_copy(xl_hbm.at[pl.ds(src, PIECE)], xbuf.at[slot, pl.ds(p * PIECE, PIECE)],
                                  gsem.at[slot]).start()

    def scatter_start(blk, slot):
        for p in range(PIECES):
            dst = pl.multiple_of(srow_ref[blk * PIECES + p], PIECE)
            pltpu.make_async_copy(ybuf.at[slot, pl.ds(p * PIECE, PIECE)], yl_hbm.at[pl.ds(dst, PIECE)],
                                  ssem.at[slot]).start()

    def gather_wait(slot):
        pltpu.make_async_copy(xl_hbm.at[pl.ds(0, TME)], xbuf.at[slot], gsem.at[slot]).wait()

    def scatter_wait(slot):
        pltpu.make_async_copy(ybuf.at[slot], yl_hbm.at[pl.ds(0, TME)], ssem.at[slot]).wait()

    @pl.when(b < nb)
    def _():
        slot = b % 2

        @pl.when(b == 0)
        def _():
            gather_start(0, 0)

        @pl.when(b + 1 < nb)
        def _():
            gather_start(b + 1, 1 - slot)

        gather_wait(slot)

        @pl.when(b >= 2)
        def _():
            scatter_wait(slot)

        x = xbuf[slot]
        hb = _silu(_dot(x, wg_ref[...])) * _dot(x, wu_ref[...])
        ybuf[slot] = _dot(hb.astype(BF16), wd_ref[...]).astype(BF16)
        scatter_start(b, slot)

        @pl.when(b == nb - 1)
        def _():
            scatter_wait(slot)

            @pl.when(b >= 1)
            def _():
                scatter_wait(1 - slot)


def _experts(block_e, n_used, gather_row, scatter_row, xl, wg, wu, wd):
    def w_map(i, be, nb, gr, sr):
        return (be[jnp.minimum(i, nb[0] - 1)], 0, 0)

    return pl.pallas_call(
        _expert_kernel,
        out_shape=jax.ShapeDtypeStruct(((N_WIN + 1) * RL, D_MODEL), BF16),
        input_output_aliases={4: 0},
        grid_spec=pltpu.PrefetchScalarGridSpec(
            num_scalar_prefetch=4,
            grid=(N_EBLOCKS,),
            in_specs=[
                pl.BlockSpec(memory_space=pl.ANY),
                pl.BlockSpec((None, D_MODEL, D_EXPERT), w_map),
                pl.BlockSpec((None, D_MODEL, D_EXPERT), w_map),
                pl.BlockSpec((None, D_EXPERT, D_MODEL), w_map),
            ],
            out_specs=pl.BlockSpec(memory_space=pl.ANY),
            scratch_shapes=[
                pltpu.VMEM((2, TME, D_MODEL), BF16), pltpu.VMEM((2, TME, D_MODEL), BF16),
                pltpu.SemaphoreType.DMA((2,)), pltpu.SemaphoreType.DMA((2,)),
            ],
        ),
        compiler_params=_cparams(("arbitrary",)),
        name="moe_experts",
    )(block_e, n_used, gather_row, scatter_row, xl, wg, wu, wd)


def _combine_kernel(rw_ref, yl_ref, lpt_ref, wt_ref, h_ref, x1_ref, mod_ref, lng_ref, lnb_ref,
                    wsg_ref, wsu_ref, wsd_ref, out_ref, acc):
    rw = rw_ref[pl.program_id(0)]
    n_tiles = (rw + SORT_ROWS - 1) // SORT_ROWS
    hb = h_ref[...]
    acc[...] = _dot((_silu(_dot(hb, wsg_ref[...])) * _dot(hb, wsu_ref[...])).astype(BF16), wsd_ref[...])
    lp = lpt_ref[...]
    wt = wt_ref[...]
    ciota = lax.broadcasted_iota(I32, (WIN, SORT_ROWS), 1)

    def tile(j, carry):
        col = ciota + j * SORT_ROWS
        pm = jnp.zeros((WIN, SORT_ROWS), F32)
        for k in range(TOP_K):
            pm = jnp.where(lp[:, k:k + 1] == col, wt[:, k:k + 1], pm)
        p_hi, p_lo = _split_hi_lo(pm)
        y = yl_ref[pl.ds(pl.multiple_of(j * SORT_ROWS, SORT_ROWS), SORT_ROWS), :]
        acc[...] += _dot(p_hi, y) + _dot(p_lo, y)
        return carry

    lax.fori_loop(0, n_tiles, tile, 0)
    z = ALPHA * x1_ref[...] + mod_ref[5:6, :] * acc[...]
    out_ref[...] = _layernorm_rows(z, lng_ref[...], lnb_ref[...])


def _combine(rw, yl, lpos_t, wt, h2, x1, mods, lng, lnb, wsg, wsu, wsd):
    full = lambda shape: pl.BlockSpec(shape, lambda i, rw: (0,) * len(shape))
    return pl.pallas_call(
        _combine_kernel,
        out_shape=jax.ShapeDtypeStruct((N_TOK, D_MODEL), F32),
        grid_spec=pltpu.PrefetchScalarGridSpec(
            num_scalar_prefetch=1,
            grid=(N_WIN,),
            in_specs=[
                pl.BlockSpec((RL, D_MODEL), lambda i, rw: (i, 0)),
                pl.BlockSpec((WIN, TOP_K), lambda i, rw: (i, 0)),
                pl.BlockSpec((WIN, TOP_K), lambda i, rw: (i, 0)),
                pl.BlockSpec((WIN, D_MODEL), lambda i, rw: (i, 0)),
                pl.BlockSpec((WIN, D_MODEL), lambda i, rw: (i, 0)),
                pl.BlockSpec((None, MOD_ROWS, D_MODEL), lambda i, rw: (_cond_block(i, WIN), 0, 0)),
                full((1, D_MODEL)), full((1, D_MODEL)),
                full((D_MODEL, D_EXPERT)), full((D_MODEL, D_EXPERT)), full((D_EXPERT, D_MODEL)),
            ],
            out_specs=pl.BlockSpec((WIN, D_MODEL), lambda i, rw: (i, 0)),
            scratch_shapes=[pltpu.VMEM((WIN, D_MODEL), F32)],
        ),
        compiler_params=_cparams(("arbitrary",)),
        name="moe_combine",
    )(rw, yl, lpos_t, wt, h2, x1, mods, lng, lnb, wsg, wsu, wsd)


def _moe_and_norm(x1, h2, lgt, mods, lng, lnb, router_bias, tri, wg, wu, wd, wsg, wsu, wsd):
    wts, lpos, p16 = _route(lgt, router_bias.reshape(N_EXPERTS, 1), tri)
    rw, block_e, n_used, gather_row, scatter_row = _moe_tables(p16[:, :N_EXPERTS].astype(I32))
    xl = _sort_rows(rw, lpos, h2)
    yl = _experts(block_e, n_used, gather_row, scatter_row, xl, wg, wu, wd)
    return _combine(rw, yl, lpos.T, wts.T, h2, x1, mods, lng, lnb, wsg, wsu, wsd)


def _rope_tables_mla():
    t = jnp.arange(DEC_SEQ)
    row = (t // GRID_W).astype(F32)
    col = (t % GRID_W).astype(F32)
    n = QK_ROPE // 4
    inv = ROPE_BASE ** (-jnp.arange(n, dtype=F32) / n)
    ang_r = row[:, None] * inv
    ang_c = col[:, None] * inv
    cos = jnp.ones((DEC_SEQ, LANE), F32)
    sin_m = jnp.zeros((DEC_SEQ, LANE), F32)
    sin_p = jnp.zeros((DEC_SEQ, LANE), F32)
    l0 = ROPE_LANE0
    for base, ang in ((l0, ang_r), (l0 + 2 * n, ang_c)):
        c, s = jnp.cos(ang), jnp.sin(ang)
        cos = cos.at[:, base:base + n].set(c).at[:, base + n:base + 2 * n].set(c)
        sin_m = sin_m.at[:, base:base + n].set(-s)
        sin_p = sin_p.at[:, base + n:base + 2 * n].set(s)
    ident = (jnp.ones((TM, LANE), F32), jnp.zeros((TM, LANE), F32), jnp.zeros((TM, LANE), F32))
    return tuple(jnp.concatenate([i, tbl], axis=0) for i, tbl in zip(ident, (cos, sin_m, sin_p)))


def _rope_tables_ret():
    half = RET_DK // 2
    theta = ROPE_BASE ** (-jnp.linspace(0.0, 1.0, half, dtype=F32))
    ang = jnp.arange(DEC_SEQ, dtype=F32)[:, None] * theta
    cos = jnp.concatenate([jnp.ones((TM, half), F32), jnp.cos(ang)], axis=0)
    sin = jnp.concatenate([jnp.zeros((TM, half), F32), jnp.sin(ang)], axis=0)
    return cos, sin


def _pad_heads(w, width, lane0=0):
    k = w.shape[0]
    w = w.reshape(k, MLA_HEADS, width)
    out = jnp.zeros((k, MLA_HEADS, HEAD_PAD), w.dtype).at[:, :, lane0:lane0 + width].set(w)
    return out.reshape(k, MLA_HEADS * HEAD_PAD)


def _rg_gate_weights(wa, ba, wx, bx):
    n_ct = RG_W // LANE
    per = LANE // RG_BW
    tiles_w, tiles_b = [], []
    for c in range(n_ct):
        cols_w, cols_b = [], []
        for d in range(2):
            for w, b in ((wa, ba), (wx, bx)):
                m = jnp.zeros((LANE, LANE), F32)
                for p in range(per):
                    m = m.at[p * RG_BW:(p + 1) * RG_BW, p * RG_BW:(p + 1) * RG_BW].set(w[d, c * per + p])
                cols_w.append(m)
                cols_b.append(b[d, c * LANE:(c + 1) * LANE])
        tiles_w.append(jnp.concatenate(cols_w, axis=1))
        tiles_b.append(jnp.concatenate(cols_b, axis=0)[None, :])
    return jnp.stack(tiles_w).astype(BF16), jnp.stack(tiles_b)


def kernel(x_prompt, x_sample, cache_mla_ckv, cache_mla_krope, state_rglru, state_ret, c, c_ctx, w_ada, b_ada,
           ln_g, ln_b, w_in_ab, rg_conv_w, rg_conv_b, rg_wa, rg_ba, rg_wx, rg_bx, rg_lambda, mla_q_norm, mla_w_uq,
           mla_kv_norm, mla_w_ukv, w_out_ab, w_in_c, ret_gamma_logit, w_out_c, w_router, router_bias,
           w_exp_gate, w_exp_up, w_exp_down, w_sh_gate, w_sh_up, w_sh_down):
    x = jnp.concatenate([x_prompt.reshape(N_PROMPT, D_MODEL), x_sample.reshape(N_SAMPLE, D_MODEL)], axis=0)
    cond = jnp.zeros((16, D_MODEL), F32).at[0].set(c_ctx).at[1:1 + DEC_BATCH].set(c)
    mods_all = _ada_modulation(cond, w_ada, b_ada).reshape(DEPTH, 16, 6, D_MODEL)[:, :N_COND]
    mods_all = jnp.pad(mods_all, ((0, 0), (0, 0), (0, MOD_ROWS - 6), (0, 0)))

    tri = (jnp.arange(WIN)[:, None] < jnp.arange(WIN)[None, :]).astype(BF16)
    wr_t = jnp.swapaxes(w_router, 1, 2)
    wr_hi = wr_t.astype(BF16)
    wr_lo = (wr_t - wr_hi.astype(F32)).astype(BF16)
    wg_e, wu_e, wd_e = w_exp_gate.astype(BF16), w_exp_up.astype(BF16), w_exp_down.astype(BF16)
    wsg, wsu, wsd = w_sh_gate.astype(BF16), w_sh_up.astype(BF16), w_sh_down.astype(BF16)

    l, e = 0, 0
    mods = mods_all[l]
    n_main = 2 * RG_W + Q_LORA + KV_LORA
    w_main = w_in_ab[e][:, :n_main].astype(BF16)
    w_kr = jnp.zeros((D_MODEL, LANE), F32).at[:, ROPE_LANE0:ROPE_LANE0 + QK_ROPE].set(w_in_ab[e][:, n_main:]).astype(BF16)
    main, krp = _proj_ab(x, mods, w_main, w_kr)

    wg_rg, bg_rg = _rg_gate_weights(rg_wa[e], rg_ba[e], rg_wx[e], rg_bx[e])
    h0_p = jnp.zeros((BATCH, 2, RG_W), F32)
    rg_args = (rg_conv_w[e], rg_conv_b[e].reshape(1, RG_W), wg_rg, bg_rg, rg_lambda[e])
    yrg, rg_fin = _rglru(main, jnp.zeros((N_TOK, RG_W), F32), *rg_args, h0_p,
                         n_seq=BATCH, seq=SEQ, row_block0=0)
    yrg, _ = _rglru(main, yrg, *rg_args, state_rglru[:, e],
                    n_seq=DEC_BATCH, seq=DEC_SEQ, row_block0=N_PROMPT // DEC_SEQ)

    cos_t, sm_t, sp_t = _rope_tables_mla()
    w_uq = mla_w_uq[e].reshape(Q_LORA, MLA_HEADS, QK_NOPE + QK_ROPE)
    wq_p = _pad_heads(w_uq.reshape(Q_LORA, -1), QK_NOPE + QK_ROPE).astype(BF16)
    w_ukv = mla_w_ukv[e].reshape(KV_LORA, MLA_HEADS, QK_NOPE + V_HEAD)
    wuk_p = _pad_heads(w_ukv[:, :, :QK_NOPE].reshape(KV_LORA, -1), QK_NOPE).astype(BF16)
    wuvt = w_ukv[:, :, QK_NOPE:].reshape(KV_LORA, MLA_HEADS * V_HEAD).T.astype(BF16)
    q_att, k_att, v_att, ckv_n = _mla_prep(main, krp, cos_t, sm_t, sp_t, mla_q_norm[e].reshape(1, Q_LORA), wq_p,
                                           mla_kv_norm[e].reshape(1, KV_LORA), wuk_p, wuvt)
    ctx_ckv = cache_mla_ckv[:, e].reshape(DEC_BATCH * PAST_LEN, KV_LORA)
    ctx_krp = jnp.zeros((DEC_BATCH * PAST_LEN, LANE), F32).at[:, ROPE_LANE0:ROPE_LANE0 + QK_ROPE].set(
        cache_mla_krope[:, e].reshape(DEC_BATCH * PAST_LEN, QK_ROPE))
    kc_att, vc_att = _mla_ctx(ctx_ckv, ctx_krp, wuk_p, wuvt)

    o_att = _attention(q_att, k_att, v_att, jnp.zeros((N_TOK, MLA_HEADS * V_HEAD), BF16), None, None,
                       n_seq=BATCH, seq=SEQ, row_block0=0, tq=SEQ)
    o_att = _attention(q_att, k_att, v_att, o_att, kc_att, vc_att,
                       n_seq=DEC_BATCH, seq=DEC_SEQ, row_block0=N_PROMPT // DEC_SEQ, tq=TQ)

    w_out = w_out_ab[e].astype(BF16)
    x1, h2, lgt = _out_ab(yrg, o_att, w_out[:RG_W], w_out[RG_W:], x, mods,
                          ln_g[l, 0].reshape(1, D_MODEL), ln_b[l, 0].reshape(1, D_MODEL), wr_hi[l], wr_lo[l])
    x = _moe_and_norm(x1, h2, lgt, mods, ln_g[l, 1].reshape(1, D_MODEL), ln_b[l, 1].reshape(1, D_MODEL),
                      router_bias[l], tri, wg_e[l], wu_e[l], wd_e[l], wsg[l], wsu[l], wsd[l])

    new_ckv = ckv_n[:N_PROMPT].reshape(BATCH, 1, SEQ, KV_LORA)
    new_krope = krp[:N_PROMPT, ROPE_LANE0:ROPE_LANE0 + QK_ROPE].reshape(BATCH, 1, SEQ, QK_ROPE)
    new_rg = rg_fin.reshape(BATCH, 1, 2, RG_W)

    l, o = 1, 0
    mods = mods_all[l]
    qk = RET_HEADS * RET_DK
    w_c = w_in_c[o].astype(BF16)
    cos_r, sin_r = _rope_tables_ret()
    q_r, k_r, v_r, g_r = _proj_c(x, mods, cos_r, sin_r, w_c[:, :qk], w_c[:, qk:2 * qk],
                                 w_c[:, 2 * qk:2 * qk + MIX_C], w_c[:, 2 * qk + MIX_C:])
    gam = jnp.broadcast_to(ret_gamma_logit[o].astype(F32)[:, :, None, None], (2, RET_HEADS, SUBLANE, LANE))
    r0_p = jnp.zeros((BATCH, 2, RET_HEADS, RET_DK, RET_DV), F32)
    o_ret, r_fin = _retention(q_r, k_r, v_r, gam, r0_p, jnp.zeros((N_TOK, MIX_C), F32),
                              n_seq=BATCH, seq=SEQ, row_block0=0, with_state=True)
    (o_ret,) = _retention(q_r, k_r, v_r, gam, state_ret[:, o], o_ret,
                          n_seq=DEC_BATCH, seq=DEC_SEQ, row_block0=N_PROMPT // DEC_SEQ, with_state=False)
    x1, h2, lgt = _out_c(o_ret, g_r, w_out_c[o].astype(BF16), x, mods,
                         ln_g[l, 0].reshape(1, D_MODEL), ln_b[l, 0].reshape(1, D_MODEL), wr_hi[l], wr_lo[l])
    x = _moe_and_norm(x1, h2, lgt, mods, ln_g[l, 1].reshape(1, D_MODEL), ln_b[l, 1].reshape(1, D_MODEL),
                      router_bias[l], tri, wg_e[l], wu_e[l], wd_e[l], wsg[l], wsu[l], wsd[l])

    y_prompt = x[:N_PROMPT].reshape(BATCH, SEQ, D_MODEL)
    y_sample = x[N_PROMPT:].reshape(DEC_BATCH, DEC_SEQ, D_MODEL)
    new_ret = r_fin.reshape(BATCH, 1, 2, RET_HEADS, RET_DK, RET_DV)
    return (y_prompt, y_sample, new_ckv, new_krope, new_rg, new_ret)
```

```python
import functools
import math

import jax
import jax.numpy as jnp
from jax import lax
from jax.experimental import pallas as pl
from jax.experimental.pallas import tpu as pltpu

F32 = jnp.float32
BF16 = jnp.bfloat16
I32 = jnp.int32

D_MODEL = 1024
BATCH, SEQ = 16, 256
DEC_BATCH, DEC_SEQ = 8, 4096
PAST_LEN = 256
DEPTH = 2
GRID_W = 64
RG_W, RG_BLOCKS = 512, 8
RG_BW = RG_W // RG_BLOCKS
RG_C = 8.0
CONV_W, CONV_LEFT = 4, 2
MLA_HEADS, QK_NOPE, QK_ROPE, V_HEAD = 8, 64, 32, 64
Q_LORA, KV_LORA = 768, 256
ROPE_BASE = 10000.0
ATTN_SCALE = (QK_NOPE + QK_ROPE) ** -0.5
RET_HEADS, RET_DK, RET_DV, RET_CHUNK = 4, 256, 512, 128
MIX_C = RET_HEADS * RET_DV
N_EXPERTS, TOP_K, N_GROUPS, TOPK_GROUPS = 64, 8, 8, 4
GROUP_SIZE = N_EXPERTS // N_GROUPS
D_EXPERT = 256
ROUTED_SCALE = 2.5
ALPHA = (2 * DEPTH) ** 0.25
EPS = 1e-6

N_PROMPT = BATCH * SEQ
N_SAMPLE = DEC_BATCH * DEC_SEQ
N_TOK = N_PROMPT + N_SAMPLE
N_COND = 1 + DEC_BATCH
MOD_ROWS = 8

LANE = 128
SUBLANE = 8
TM = 512
HEAD_PAD = 128
ROPE_LANE0 = QK_NOPE
TQ = 256
KC = 256
SCAN_ROWS = 64
GATE_ROWS = 256
WIN = 256
N_WIN = N_TOK // WIN
PIECE = 16
SORT_ROWS = 256
RL = 3072
TME = 512
PIECES = TME // PIECE
N_PAIRS = N_TOK * TOP_K
N_EBLOCKS = (N_PAIRS + N_WIN * N_EXPERTS * (PIECE - 1)) // TME + N_EXPERTS
NEG = -1e30


def _cparams(sem, vmem_mb=48):
    return pltpu.CompilerParams(dimension_semantics=sem, vmem_limit_bytes=vmem_mb * 1024 * 1024)


def _cond_block(i, tm):
    npb = N_PROMPT // tm
    return jnp.where(i < npb, 0, 1 + (i - npb) // (DEC_SEQ // tm))


def _pos_block(i, tm):
    npb = N_PROMPT // tm
    return jnp.where(i < npb, 0, 1 + (i - npb) % (DEC_SEQ // tm))


def _split_hi_lo(a):
    hi = a.astype(BF16)
    lo = (a - hi.astype(F32)).astype(BF16)
    return hi, lo


def _dot(a, b):
    return jnp.dot(a, b, preferred_element_type=F32)


def _dot_nt(a, b):
    return lax.dot_general(a, b, (((1,), (1,)), ((), ())), preferred_element_type=F32)


def _dot_tn(a, b):
    return lax.dot_general(a, b, (((0,), (0,)), ((), ())), preferred_element_type=F32)


def _silu(x):
    return x * jax.nn.sigmoid(x)


def _gelu_tanh(x):
    return 0.5 * x * (1.0 + jnp.tanh(math.sqrt(2.0 / math.pi) * (x + 0.044715 * (x * x * x))))


def _softplus(x):
    return jnp.maximum(x, 0.0) + jnp.log1p(jnp.exp(-jnp.abs(x)))


def _layernorm_rows(z, g, b):
    mu = jnp.mean(z, axis=-1, keepdims=True)
    zc = z - mu
    var = jnp.mean(zc * zc, axis=-1, keepdims=True)
    return (zc * lax.rsqrt(var + EPS)) * g + b


def _ada_kernel(c_ref, w_ref, b_ref, o_ref):
    s_hi, s_lo = _split_hi_lo(_silu(c_ref[...]))
    w_hi, w_lo = _split_hi_lo(w_ref[...])
    o_ref[...] = _dot(s_hi, w_hi) + _dot(s_hi, w_lo) + _dot(s_lo, w_hi) + b_ref[...]


def _ada_modulation(cond, w_ada, b_ada):
    n6 = 6 * D_MODEL
    tn = D_MODEL
    return pl.pallas_call(
        _ada_kernel,
        out_shape=jax.ShapeDtypeStruct((DEPTH, 16, n6), F32),
        grid=(DEPTH, n6 // tn),
        in_specs=[
            pl.BlockSpec((16, D_MODEL), lambda l, j: (0, 0)),
            pl.BlockSpec((None, D_MODEL, tn), lambda l, j: (l, 0, j)),
            pl.BlockSpec((None, 1, tn), lambda l, j: (l, 0, j)),
        ],
        out_specs=pl.BlockSpec((None, 16, tn), lambda l, j: (l, 0, j)),
        compiler_params=_cparams(("arbitrary", "arbitrary")),
        name="ada_modulation",
    )(cond, w_ada, b_ada.reshape(DEPTH, 1, n6))


def _modulated(x, mod_ref, shift_row, scale_row):
    return x * (1.0 + mod_ref[scale_row:scale_row + 1, :]) + mod_ref[shift_row:shift_row + 1, :]


def _proj_ab_kernel(x_ref, mod_ref, w_ref, wkr_ref, main_ref, kr_ref):
    h = _modulated(x_ref[...], mod_ref, 0, 1).astype(BF16)
    n = w_ref.shape[1]
    step = 512
    for j in range(n // step):
        main_ref[:, j * step:(j + 1) * step] = _dot(h, w_ref[:, j * step:(j + 1) * step])
    kr_ref[...] = _dot(h, wkr_ref[...])


def _proj_ab(x, mods, w_main, w_kr):
    n = w_main.shape[1]
    return pl.pallas_call(
        _proj_ab_kernel,
        out_shape=(jax.ShapeDtypeStruct((N_TOK, n), F32), jax.ShapeDtypeStruct((N_TOK, LANE), F32)),
        grid=(N_TOK // TM,),
        in_specs=[
            pl.BlockSpec((TM, D_MODEL), lambda i: (i, 0)),
            pl.BlockSpec((None, MOD_ROWS, D_MODEL), lambda i: (_cond_block(i, TM), 0, 0)),
            pl.BlockSpec((D_MODEL, n), lambda i: (0, 0)),
            pl.BlockSpec((D_MODEL, LANE), lambda i: (0, 0)),
        ],
        out_specs=(pl.BlockSpec((TM, n), lambda i: (i, 0)), pl.BlockSpec((TM, LANE), lambda i: (i, 0))),
        compiler_params=_cparams(("arbitrary",)),
        name="proj_ab",
    )(x, mods, w_main, w_kr)


def _rglru_kernel(xr_ref, gr_ref, cw_ref, cb_ref, wg_ref, bg_ref, lam_ref, h0_ref, yprev_ref,
                  y_ref, hfin_ref, xpad, a_s, b_s, *, seq):
    del yprev_ref
    pad = SUBLANE
    xpad[0:pad, :] = jnp.zeros((pad, LANE), F32)
    xpad[seq + pad:seq + 2 * pad, :] = jnp.zeros((pad, LANE), F32)
    xpad[pad:seq + pad, :] = xr_ref[...]

    sp = _softplus(-lam_ref[...])
    cw = cw_ref[...]
    cb = cb_ref[...]
    wg = wg_ref[...]
    bg = bg_ref[...]

    def gate_step(c, carry):
        t0 = pl.multiple_of(c * GATE_ROWS, GATE_ROWS)
        win = xpad[pl.ds(t0, GATE_ROWS + 2 * pad), :]
        xc = cb
        for j in range(CONV_W):
            off = pad - CONV_LEFT + j
            xc = xc + win[off:off + GATE_ROWS, :] * cw[j:j + 1, :]
        g = _dot(xc.astype(BF16), wg) + bg
        for d in range(2):
            r = jax.nn.sigmoid(g[:, (2 * d) * LANE:(2 * d + 1) * LANE])
            i = jax.nn.sigmoid(g[:, (2 * d + 1) * LANE:(2 * d + 2) * LANE])
            log_a = (-RG_C * r) * sp[d:d + 1, :]
            a = jnp.exp(log_a)
            t = jnp.tanh(log_a)
            bt = jnp.sqrt(2.0 * t / (t - 1.0)) * (i * xc)
            a_s[d, pl.ds(t0, GATE_ROWS), :] = a
            b_s[d, pl.ds(t0, GATE_ROWS), :] = bt
        return carry

    lax.fori_loop(0, seq // GATE_ROWS, gate_step, 0)

    row = lax.broadcasted_iota(I32, (SCAN_ROWS, LANE), 0) % SUBLANE
    n_steps = seq // SCAN_ROWS
    tiles = SCAN_ROWS // SUBLANE

    def local_scan(a, b, reverse):
        for k in (1, 2, 4):
            if reverse:
                ok = row < SUBLANE - k
                shift = SCAN_ROWS - k
            else:
                ok = row >= k
                shift = k
            a_sh = jnp.where(ok, pltpu.roll(a, shift, 0), 1.0)
            b_sh = jnp.where(ok, pltpu.roll(b, shift, 0), 0.0)
            b = a * b_sh + b
            a = a * a_sh
        return a, b

    def fwd_step(c, h):
        t0 = pl.multiple_of(c * SCAN_ROWS, SCAN_ROWS)
        a, b = local_scan(a_s[0, pl.ds(t0, SCAN_ROWS), :], b_s[0, pl.ds(t0, SCAN_ROWS), :], False)
        outs = []
        for j in range(tiles):
            hj = a[j * SUBLANE:(j + 1) * SUBLANE, :] * h + b[j * SUBLANE:(j + 1) * SUBLANE, :]
            outs.append(hj)
            h = hj[SUBLANE - 1:SUBLANE, :]
        y_ref[pl.ds(t0, SCAN_ROWS), :] = jnp.concatenate(outs, axis=0)
        return h

    h_f = lax.fori_loop(0, n_steps, fwd_step, h0_ref[0:1, :])

    def bwd_step(c, h):
        t0 = pl.multiple_of((n_steps - 1 - c) * SCAN_ROWS, SCAN_ROWS)
        a, b = local_scan(a_s[1, pl.ds(t0, SCAN_ROWS), :], b_s[1, pl.ds(t0, SCAN_ROWS), :], True)
        outs = [None] * tiles
        for j in reversed(range(tiles)):
            hj = a[j * SUBLANE:(j + 1) * SUBLANE, :] * h + b[j * SUBLANE:(j + 1) * SUBLANE, :]
            outs[j] = hj
            h = hj[0:1, :]
        hb = jnp.concatenate(outs, axis=0)
        y_ref[pl.ds(t0, SCAN_ROWS), :] = (y_ref[pl.ds(t0, SCAN_ROWS), :] + hb) * _gelu_tanh(gr_ref[pl.ds(t0, SCAN_ROWS), :])
        return h

    h_b = lax.fori_loop(0, n_steps, bwd_step, h0_ref[1:2, :])
    hfin_ref[0:1, :] = h_f
    hfin_ref[1:2, :] = h_b


def _rglru(main, y_prev, cw, cb, wg, bg, lam, h0, *, n_seq, seq, row_block0):
    n_ct = RG_W // LANE
    gr_col0 = RG_W // LANE
    kern = functools.partial(_rglru_kernel, seq=seq)
    return pl.pallas_call(
        kern,
        out_shape=(jax.ShapeDtypeStruct((N_TOK, RG_W), F32), jax.ShapeDtypeStruct((n_seq, 2, RG_W), F32)),
        grid=(n_seq, n_ct),
        in_specs=[
            pl.BlockSpec((seq, LANE), lambda b, c: (row_block0 + b, c)),
            pl.BlockSpec((seq, LANE), lambda b, c: (row_block0 + b, gr_col0 + c)),
            pl.BlockSpec((CONV_W, LANE), lambda b, c: (0, c)),
            pl.BlockSpec((1, LANE), lambda b, c: (0, c)),
            pl.BlockSpec((None, LANE, 4 * LANE), lambda b, c: (c, 0, 0)),
            pl.BlockSpec((None, 1, 4 * LANE), lambda b, c: (c, 0, 0)),
            pl.BlockSpec((2, LANE), lambda b, c: (0, c)),
            pl.BlockSpec((None, 2, LANE), lambda b, c: (b, 0, c)),
            pl.BlockSpec(memory_space=pl.ANY),
        ],
        out_specs=(
            pl.BlockSpec((seq, LANE), lambda b, c: (row_block0 + b, c)),
            pl.BlockSpec((None, 2, LANE), lambda b, c: (b, 0, c)),
        ),
        scratch_shapes=[
            pltpu.VMEM((seq + 2 * SUBLANE, LANE), F32),
            pltpu.VMEM((2, seq, LANE), F32),
            pltpu.VMEM((2, seq, LANE), F32),
        ],
        input_output_aliases={8: 0},
        compiler_params=_cparams(("arbitrary", "arbitrary")),
        name=f"rglru_s{seq}",
    )(main, main, cw, cb, wg, bg, lam, h0, y_prev)


def _rope_lanes(x, cos, sin_m, sin_p):
    n = x.shape[1] // LANE
    half = QK_ROPE // 4
    cos_t = jnp.concatenate([cos] * n, axis=1) if n > 1 else cos
    sm_t = jnp.concatenate([sin_m] * n, axis=1) if n > 1 else sin_m
    sp_t = jnp.concatenate([sin_p] * n, axis=1) if n > 1 else sin_p
    up = pltpu.roll(x, x.shape[1] - half, 1)
    dn = pltpu.roll(x, half, 1)
    return x * cos_t + up * sm_t + dn * sp_t


def _mla_prep_kernel(cq0_ref, cq1_ref, cq2_ref, ckv_ref, kr_ref, cos_ref, sm_ref, sp_ref,
                     qn_ref, wq_ref, kvn_ref, wuk_ref, wuvt_ref,
                     q_ref, k_ref, vt_ref, ckvn_ref):
    cq = [cq0_ref[...], cq1_ref[...], cq2_ref[...]]
    ms = (jnp.sum(cq[0] * cq[0], axis=-1, keepdims=True) + jnp.sum(cq[1] * cq[1], axis=-1, keepdims=True)
          + jnp.sum(cq[2] * cq[2], axis=-1, keepdims=True)) * (1.0 / Q_LORA)
    inv = lax.rsqrt(ms + EPS)
    blk = Q_LORA // 3
    q = None
    for j in range(3):
        cqn = ((cq[j] * inv) * qn_ref[:, j * blk:(j + 1) * blk]).astype(BF16)
        part = _dot(cqn, wq_ref[j * blk:(j + 1) * blk, :])
        q = part if q is None else q + part
    cos, sm, sp = cos_ref[...], sm_ref[...], sp_ref[...]
    q_ref[...] = _rope_lanes(q, cos, sm, sp).astype(BF16)

    ckv = ckv_ref[...]
    inv_kv = lax.rsqrt(jnp.mean(ckv * ckv, axis=-1, keepdims=True) + EPS)
    ckvn = (ckv * inv_kv) * kvn_ref[...]
    ckvn_ref[...] = ckvn
    ckvn_b = ckvn.astype(BF16)
    kr_rot = _rope_lanes(kr_ref[...], cos, sm, sp)
    k_ref[...] = (_dot(ckvn_b, wuk_ref[...]) + jnp.concatenate([kr_rot] * MLA_HEADS, axis=1)).astype(BF16)
    vt = _dot_nt(wuvt_ref[...], ckvn_b).astype(BF16)
    for c in range(vt_ref.shape[0]):
        vt_ref[c] = vt[:, c * KC:(c + 1) * KC]


def _mla_prep(main, krp, cos_t, sm_t, sp_t, q_norm, wq_p, kv_norm, wuk_p, wuvt):
    cq_col0 = 2 * RG_W // 256
    hp = MLA_HEADS * HEAD_PAD
    full = lambda shape: pl.BlockSpec(shape, lambda i: (0,) * len(shape))
    tab = pl.BlockSpec((TM, LANE), lambda i: (_pos_block(i, TM), 0))
    return pl.pallas_call(
        _mla_prep_kernel,
        out_shape=(
            jax.ShapeDtypeStruct((N_TOK, hp), BF16),
            jax.ShapeDtypeStruct((N_TOK, hp), BF16),
            jax.ShapeDtypeStruct((N_TOK // KC, MLA_HEADS * V_HEAD, KC), BF16),
            jax.ShapeDtypeStruct((N_TOK, KV_LORA), F32),
        ),
        grid=(N_TOK // TM,),
        in_specs=[
            pl.BlockSpec((TM, 256), lambda i: (i, cq_col0)),
            pl.BlockSpec((TM, 256), lambda i: (i, cq_col0 + 1)),
            pl.BlockSpec((TM, 256), lambda i: (i, cq_col0 + 2)),
            pl.BlockSpec((TM, 256), lambda i: (i, cq_col0 + 3)),
            pl.BlockSpec((TM, LANE), lambda i: (i, 0)),
            tab, tab, tab,
            full((1, Q_LORA)), full((Q_LORA, hp)), full((1, KV_LORA)), full((KV_LORA, hp)),
            full((MLA_HEADS * V_HEAD, KV_LORA)),
        ],
        out_specs=(
            pl.BlockSpec((TM, hp), lambda i: (i, 0)),
            pl.BlockSpec((TM, hp), lambda i: (i, 0)),
            pl.BlockSpec((TM // KC, MLA_HEADS * V_HEAD, KC), lambda i: (i, 0, 0)),
            pl.BlockSpec((TM, KV_LORA), lambda i: (i, 0)),
        ),
        compiler_params=_cparams(("arbitrary",)),
        name="mla_prep",
    )(main, main, main, main, krp, cos_t, sm_t, sp_t, q_norm, wq_p, kv_norm, wuk_p, wuvt)


def _mla_ctx_kernel(ckv_ref, kr_ref, wuk_ref, wuvt_ref, k_ref, vt_ref):
    ckv_b = ckv_ref[...].astype(BF16)
    k_ref[...] = (_dot(ckv_b, wuk_ref[...]) + jnp.concatenate([kr_ref[...]] * MLA_HEADS, axis=1)).astype(BF16)
    vt_ref[...] = _dot_nt(wuvt_ref[...], ckv_b).astype(BF16)


def _mla_ctx(ctx_ckv, ctx_krp, wuk_p, wuvt):
    n = ctx_ckv.shape[0]
    hp = MLA_HEADS * HEAD_PAD
    tm = KC
    full = lambda shape: pl.BlockSpec(shape, lambda i: (0,) * len(shape))
    return pl.pallas_call(
        _mla_ctx_kernel,
        out_shape=(jax.ShapeDtypeStruct((n, hp), BF16),
                   jax.ShapeDtypeStruct((n // tm, MLA_HEADS * V_HEAD, tm), BF16)),
        grid=(n // tm,),
        in_specs=[
            pl.BlockSpec((tm, KV_LORA), lambda i: (i, 0)),
            pl.BlockSpec((tm, LANE), lambda i: (i, 0)),
            full((KV_LORA, hp)), full((MLA_HEADS * V_HEAD, KV_LORA)),
        ],
        out_specs=(pl.BlockSpec((tm, hp), lambda i: (i, 0)),
                   pl.BlockSpec((None, MLA_HEADS * V_HEAD, tm), lambda i: (i, 0, 0))),
        compiler_params=_cparams(("arbitrary",)),
        name="mla_ctx",
    )(ctx_ckv, ctx_krp, wuk_p, wuvt)


def _attn_kernel(*refs, seq, tq, n_ctx):
    if n_ctx:
        q_ref, k_ref, vt_ref, kc_ref, vtc_ref, oprev_ref, o_ref, s_scr, p_scr, k_all, vt_all = refs
    else:
        q_ref, k_ref, vt_ref, oprev_ref, o_ref, s_scr, p_scr, k_all, vt_all = refs
    del oprev_ref
    has_ctx = 1 if n_ctx else 0
    n_own = seq // KC
    n = n_own + has_ctx
    c_exp = ATTN_SCALE * math.log2(math.e)
    qs = [q_ref[:, h * HEAD_PAD:(h + 1) * HEAD_PAD] for h in range(2)]

    @pl.when(pl.program_id(2) == 0)
    def _():
        k_all[0:seq, :] = k_ref[...]
        vt_all[0:n_own] = vt_ref[...]
        if has_ctx:
            k_all[seq:seq + KC, :] = kc_ref[...]
            vt_all[n_own] = vtc_ref[...]

    def k_chunk(c, h):
        t0 = c * KC if isinstance(c, int) else pl.multiple_of(c * KC, KC)
        return k_all[pl.ds(t0, KC), h * HEAD_PAD:(h + 1) * HEAD_PAD]

    def v_chunk(c, h):
        return vt_all[c, h * V_HEAD:(h + 1) * V_HEAD, :]

    def scores(c, slot):
        for h in range(2):
            s_scr[slot, h] = _dot_nt(k_chunk(c, h), qs[h])

    def softmax_chunk(slot, st):
        out = []
        for h in range(2):
            m, l, _, acc = st[h]
            t = s_scr[slot, h] * c_exp
            m_new = jnp.maximum(m, jnp.max(t, axis=0, keepdims=True))
            alpha = jnp.exp2(m - m_new)
            p = jnp.exp2(t - m_new)
            p_scr[slot, h] = p.astype(BF16)
            out.append((m_new, alpha * l + jnp.sum(p, axis=0, keepdims=True), alpha, acc))
        return out

    def weighted_values(c, slot, st, alphas):
        return [(st[h][0], st[h][1], st[h][2], alphas[h] * st[h][3] + _dot(v_chunk(c, h), p_scr[slot, h]))
                for h in range(2)]

    def step(c, slot, st, with_s, with_v):
        if with_s:
            scores(c + 1, 1 - slot)
        alphas = [st[h][2] for h in range(2)]
        st = softmax_chunk(slot, st)
        if with_v:
            prev = max(c - 1, 0) if isinstance(c, int) else jnp.maximum(c - 1, 0)
            st = weighted_values(prev, 1 - slot, st, alphas)
        return st

    p_scr[1] = jnp.zeros(p_scr.shape[1:], BF16)
    st = [(jnp.full((1, tq), NEG, F32), jnp.zeros((1, tq), F32), jnp.ones((1, tq), F32),
           jnp.zeros((V_HEAD, tq), F32)) for _ in range(2)]
    scores(0, 0)
    n_pairs = (n - 1) // 2

    def pair(j, flat):
        st = [tuple(flat[0:4]), tuple(flat[4:8])]
        st = step(2 * j, 0, st, True, True)
        st = step(2 * j + 1, 1, st, True, True)
        return tuple(st[0]) + tuple(st[1])

    if n_pairs:
        flat = lax.fori_loop(0, n_pairs, pair, tuple(st[0]) + tuple(st[1]))
        st = [tuple(flat[0:4]), tuple(flat[4:8])]
    for c in range(2 * n_pairs, n):
        st = step(c, c % 2, st, c + 1 < n, c > 0)
    st = weighted_values(n - 1, (n - 1) % 2, st, [st[h][2] for h in range(2)])
    for h in range(2):
        o_ref[:, h * V_HEAD:(h + 1) * V_HEAD] = (st[h][3] / st[h][1]).T.astype(o_ref.dtype)


def _attention(q, k, vt, o_prev, kc, vtc, *, n_seq, seq, row_block0, tq):
    n_ctx = 0 if kc is None else PAST_LEN
    n_hp = MLA_HEADS // 2
    nq = seq // tq
    kern = functools.partial(_attn_kernel, seq=seq, tq=tq, n_ctx=n_ctx)
    in_specs = [
        pl.BlockSpec((tq, 2 * HEAD_PAD), lambda b, j, i: ((row_block0 + b) * nq + i, j)),
        pl.BlockSpec((seq, 2 * HEAD_PAD), lambda b, j, i: (row_block0 + b, j)),
        pl.BlockSpec((seq // KC, 2 * V_HEAD, KC), lambda b, j, i: (row_block0 + b, j, 0)),
    ]
    args = [q, k, vt]
    if n_ctx:
        in_specs += [
            pl.BlockSpec((n_ctx, 2 * HEAD_PAD), lambda b, j, i: (b, j)),
            pl.BlockSpec((None, 2 * V_HEAD, KC), lambda b, j, i: (b, j, 0)),
        ]
        args += [kc, vtc]
    in_specs.append(pl.BlockSpec(memory_space=pl.ANY))
    args.append(o_prev)
    return pl.pallas_call(
        kern,
        out_shape=jax.ShapeDtypeStruct((N_TOK, MLA_HEADS * V_HEAD), BF16),
        grid=(n_seq, n_hp, nq),
        in_specs=in_specs,
        out_specs=pl.BlockSpec((tq, 2 * V_HEAD), lambda b, j, i: ((row_block0 + b) * nq + i, j)),
        scratch_shapes=[pltpu.VMEM((2, 2, KC, tq), F32), pltpu.VMEM((2, 2, KC, tq), BF16),
                        pltpu.VMEM((seq + n_ctx, 2 * HEAD_PAD), BF16),
                        pltpu.VMEM(((seq + n_ctx) // KC, 2 * V_HEAD, KC), BF16)],
        input_output_aliases={len(args) - 1: 0},
        compiler_params=_cparams(("arbitrary", "arbitrary", "arbitrary")),
        name=f"mla_attention_s{seq}",
    )(*args)


def _post_mixer(y, x_ref, mod_ref, lng_ref, lnb_ref, wrh_ref, wrl_ref, x1_ref, h2_ref, lgt_ref):
    z = ALPHA * x_ref[...] + mod_ref[2:3, :] * y
    x1 = _layernorm_rows(z, lng_ref[...], lnb_ref[...])
    x1_ref[...] = x1
    h2 = _modulated(x1, mod_ref, 3, 4)
    h_hi, h_lo = _split_hi_lo(h2)
    h2_ref[...] = h_hi
    w_hi, w_lo = wrh_ref[...], wrl_ref[...]
    lgt_ref[...] = _dot_nt(w_hi, h_hi) + _dot_nt(w_hi, h_lo) + _dot_nt(w_lo, h_hi)


def _out_ab_kernel(yrg_ref, o_ref, wa_ref, wb_ref, x_ref, mod_ref, lng_ref, lnb_ref, wrh_ref, wrl_ref,
                   x1_ref, h2_ref, lgt_ref):
    y = _dot(yrg_ref[...].astype(BF16), wa_ref[...]) + _dot(o_ref[...], wb_ref[...])
    _post_mixer(y, x_ref, mod_ref, lng_ref, lnb_ref, wrh_ref, wrl_ref, x1_ref, h2_ref, lgt_ref)


def _post_specs():
    full = lambda shape: pl.BlockSpec(shape, lambda i: (0,) * len(shape))
    in_specs = [
        pl.BlockSpec((TM, D_MODEL), lambda i: (i, 0)),
        pl.BlockSpec((None, MOD_ROWS, D_MODEL), lambda i: (_cond_block(i, TM), 0, 0)),
        full((1, D_MODEL)), full((1, D_MODEL)),
        full((N_EXPERTS, D_MODEL)), full((N_EXPERTS, D_MODEL)),
    ]
    out_shape = (
        jax.ShapeDtypeStruct((N_TOK, D_MODEL), F32),
        jax.ShapeDtypeStruct((N_TOK, D_MODEL), BF16),
        jax.ShapeDtypeStruct((N_EXPERTS, N_TOK), F32),
    )
    out_specs = (
        pl.BlockSpec((TM, D_MODEL), lambda i: (i, 0)),
        pl.BlockSpec((TM, D_MODEL), lambda i: (i, 0)),
        pl.BlockSpec((N_EXPERTS, TM), lambda i: (0, i)),
    )
    return in_specs, out_shape, out_specs


def _out_ab(yrg, o, w_a, w_b, x, mods, lng, lnb, wr_hi, wr_lo):
    full = lambda shape: pl.BlockSpec(shape, lambda i: (0,) * len(shape))
    post_in, out_shape, out_specs = _post_specs()
    return pl.pallas_call(
        _out_ab_kernel,
        out_shape=out_shape,
        grid=(N_TOK // TM,),
        in_specs=[
            pl.BlockSpec((TM, RG_W), lambda i: (i, 0)),
            pl.BlockSpec((TM, MLA_HEADS * V_HEAD), lambda i: (i, 0)),
            full((RG_W, D_MODEL)), full((MLA_HEADS * V_HEAD, D_MODEL)),
        ] + post_in,
        out_specs=out_specs,
        compiler_params=_cparams(("arbitrary",)),
        name="out_ab",
    )(yrg, o, w_a, w_b, x, mods, lng, lnb, wr_hi, wr_lo)


def _proj_c_kernel(x_ref, mod_ref, cos_ref, sin_ref, wq_ref, wk_ref, wv_ref, wg_ref, q_ref, k_ref, v_ref, g_ref):
    h = _modulated(x_ref[...], mod_ref, 0, 1).astype(BF16)
    cos, sin = cos_ref[...], sin_ref[...]
    half = RET_DK // 2
    for hd in range(RET_HEADS):
        for w_ref, o_ref, scale in ((wq_ref, q_ref, 1.0), (wk_ref, k_ref, RET_DK ** -0.5)):
            p = _dot(h, w_ref[:, hd * RET_DK:(hd + 1) * RET_DK])
            x1, x2 = p[:, :half], p[:, half:]
            r1 = x1 * cos - x2 * sin
            r2 = x2 * cos + x1 * sin
            if scale != 1.0:
                r1, r2 = r1 * scale, r2 * scale
            o_ref[:, hd * RET_DK:hd * RET_DK + half] = r1.astype(BF16)
            o_ref[:, hd * RET_DK + half:(hd + 1) * RET_DK] = r2.astype(BF16)
    step = 512
    for j in range(MIX_C // step):
        v_ref[:, j * step:(j + 1) * step] = _dot(h, wv_ref[:, j * step:(j + 1) * step]).astype(BF16)
        g_ref[:, j * step:(j + 1) * step] = _dot(h, wg_ref[:, j * step:(j + 1) * step])


def _proj_c(x, mods, cos_t, sin_t, wq, wk, wv, wg):
    full = lambda shape: pl.BlockSpec(shape, lambda i: (0,) * len(shape))
    qk = RET_HEADS * RET_DK
    tab = pl.BlockSpec((TM, RET_DK // 2), lambda i: (_pos_block(i, TM), 0))
    return pl.pallas_call(
        _proj_c_kernel,
        out_shape=(
            jax.ShapeDtypeStruct((N_TOK, qk), BF16), jax.ShapeDtypeStruct((N_TOK, qk), BF16),
            jax.ShapeDtypeStruct((N_TOK, MIX_C), BF16), jax.ShapeDtypeStruct((N_TOK, MIX_C), F32),
        ),
        grid=(N_TOK // TM,),
        in_specs=[
            pl.BlockSpec((TM, D_MODEL), lambda i: (i, 0)),
            pl.BlockSpec((None, MOD_ROWS, D_MODEL), lambda i: (_cond_block(i, TM), 0, 0)),
            tab, tab,
            full((D_MODEL, qk)), full((D_MODEL, qk)), full((D_MODEL, MIX_C)), full((D_MODEL, MIX_C)),
        ],
        out_specs=(
            pl.BlockSpec((TM, qk), lambda i: (i, 0)), pl.BlockSpec((TM, qk), lambda i: (i, 0)),
            pl.BlockSpec((TM, MIX_C), lambda i: (i, 0)), pl.BlockSpec((TM, MIX_C), lambda i: (i, 0)),
        ),
        compiler_params=_cparams(("arbitrary",), 56),
        name="proj_c",
    )(x, mods, cos_t, sin_t, wq, wk, wv, wg)


def _retention_kernel(*refs, seq, with_state):
    if with_state:
        q_ref, k_ref, v_ref, gam_ref, r0_ref, oprev_ref, o_ref, rfin_ref, r_s = refs
    else:
        q_ref, k_ref, v_ref, gam_ref, r0_ref, oprev_ref, o_ref, r_s = refs
        rfin_ref = None
    del oprev_ref
    c = RET_CHUNK
    n = seq // c
    ii = lax.broadcasted_iota(I32, (c, c), 0).astype(F32)
    jj = lax.broadcasted_iota(I32, (c, c), 1).astype(F32)
    ci = lax.broadcasted_iota(I32, (c, 1), 0).astype(F32)

    consts = []
    for d in range(2):
        gam = gam_ref[d]
        lg_row = -_softplus(-gam[0:1, :])
        lg = jnp.broadcast_to(lg_row, (c, c))
        lg_col = jnp.broadcast_to(lg_row[:, 0:1], (c, 1))
        if d == 0:
            diff = ii - jj
            xi = jnp.exp((ci + 1.0) * lg_col)
            zeta = jnp.exp((c - 1.0 - ci) * lg_col)
        else:
            diff = jj - ii
            xi = jnp.exp((c - ci) * lg_col)
            zeta = jnp.exp(ci * lg_col)
        dmat = jnp.where(diff >= 0, jnp.exp(jnp.maximum(diff, 0.0) * lg), 0.0)
        g_chunk = jnp.exp(float(c) * lg_row[:, 0:1])
        consts.append((dmat, xi, zeta, g_chunk))
        r_s[d] = r0_ref[d]

    def chunk(d, idx, accumulate):
        dmat, xi, zeta, g_chunk = consts[d]
        t0 = pl.multiple_of(idx * c, c)
        qb = q_ref[pl.ds(t0, c), :]
        kb = k_ref[pl.ds(t0, c), :]
        vb = v_ref[pl.ds(t0, c), :]
        r = r_s[d]
        inner = _dot_nt(qb, kb) * dmat
        o = _dot(inner.astype(BF16), vb) + _dot((qb.astype(F32) * xi).astype(BF16), r.astype(BF16))
        r_s[d] = r * g_chunk + _dot_tn((kb.astype(F32) * zeta).astype(BF16), vb)
        if accumulate:
            o_ref[pl.ds(t0, c), :] = o_ref[pl.ds(t0, c), :] + o
        else:
            o_ref[pl.ds(t0, c), :] = o

    def first_half(s, carry):
        chunk(0, s, False)
        chunk(1, n - 1 - s, False)
        return carry

    def second_half(s, carry):
        chunk(0, s, True)
        chunk(1, n - 1 - s, True)
        return carry

    lax.fori_loop(0, n // 2, first_half, 0)
    lax.fori_loop(n // 2, n, second_half, 0)
    if with_state:
        for d in range(2):
            rfin_ref[d] = r_s[d]


def _retention(q, k, v, gam, r0, o_prev, *, n_seq, seq, row_block0, with_state):
    kern = functools.partial(_retention_kernel, seq=seq, with_state=with_state)
    out_shape = [jax.ShapeDtypeStruct((N_TOK, MIX_C), F32)]
    out_specs = [pl.BlockSpec((seq, RET_DV), lambda b, h: (row_block0 + b, h))]
    if with_state:
        out_shape.append(jax.ShapeDtypeStruct((n_seq, 2, RET_HEADS, RET_DK, RET_DV), F32))
        out_specs.append(pl.BlockSpec((None, 2, None, RET_DK, RET_DV), lambda b, h: (b, 0, h, 0, 0)))
    return pl.pallas_call(
        kern,
        out_shape=tuple(out_shape),
        grid=(n_seq, RET_HEADS),
        in_specs=[
            pl.BlockSpec((seq, RET_DK), lambda b, h: (row_block0 + b, h)),
            pl.BlockSpec((seq, RET_DK), lambda b, h: (row_block0 + b, h)),
            pl.BlockSpec((seq, RET_DV), lambda b, h: (row_block0 + b, h)),
            pl.BlockSpec((2, None, SUBLANE, LANE), lambda b, h: (0, h, 0, 0)),
            pl.BlockSpec((None, 2, None, RET_DK, RET_DV), lambda b, h: (b, 0, h, 0, 0)),
            pl.BlockSpec(memory_space=pl.ANY),
        ],
        out_specs=tuple(out_specs),
        scratch_shapes=[pltpu.VMEM((2, RET_DK, RET_DV), F32)],
        input_output_aliases={5: 0},
        compiler_params=_cparams(("arbitrary", "arbitrary"), 56),
        name=f"retention_s{seq}",
    )(q, k, v, gam, r0, o_prev)


def _out_c_kernel(o_ref, g_ref, w_ref, x_ref, mod_ref, lng_ref, lnb_ref, wrh_ref, wrl_ref,
                  x1_ref, h2_ref, lgt_ref):
    y = None
    for hd in range(RET_HEADS):
        o = o_ref[:, hd * RET_DV:(hd + 1) * RET_DV]
        mu = jnp.mean(o, axis=-1, keepdims=True)
        oc = o - mu
        var = jnp.mean(oc * oc, axis=-1, keepdims=True)
        on = oc * lax.rsqrt(var + EPS)
        a = (on * _silu(g_ref[:, hd * RET_DV:(hd + 1) * RET_DV])).astype(BF16)
        part = _dot(a, w_ref[hd * RET_DV:(hd + 1) * RET_DV, :])
        y = part if y is None else y + part
    _post_mixer(y, x_ref, mod_ref, lng_ref, lnb_ref, wrh_ref, wrl_ref, x1_ref, h2_ref, lgt_ref)


def _out_c(o, g, w, x, mods, lng, lnb, wr_hi, wr_lo):
    full = lambda shape: pl.BlockSpec(shape, lambda i: (0,) * len(shape))
    post_in, out_shape, out_specs = _post_specs()
    return pl.pallas_call(
        _out_c_kernel,
        out_shape=out_shape,
        grid=(N_TOK // TM,),
        in_specs=[
            pl.BlockSpec((TM, MIX_C), lambda i: (i, 0)),
            pl.BlockSpec((TM, MIX_C), lambda i: (i, 0)),
            full((MIX_C, D_MODEL)),
        ] + post_in,
        out_specs=out_specs,
        compiler_params=_cparams(("arbitrary",), 56),
        name="out_c",
    )(o, g, w, x, mods, lng, lnb, wr_hi, wr_lo)


def _route_kernel(lgt_ref, bias_ref, tri_ref, w_ref, lpos_ref, p16_ref):
    tt = lgt_ref.shape[1]
    scores = jax.nn.sigmoid(lgt_ref[...])
    sel = scores + bias_ref[...]
    srow = lax.broadcasted_iota(I32, (GROUP_SIZE, tt), 0).astype(F32)
    ninf = -jnp.inf

    gs = []
    for g in range(N_GROUPS):
        sg = sel[g * GROUP_SIZE:(g + 1) * GROUP_SIZE, :]
        m1 = jnp.max(sg, axis=0, keepdims=True)
        i1 = jnp.min(jnp.where(sg == m1, srow, float(GROUP_SIZE)), axis=0, keepdims=True)
        m2 = jnp.max(jnp.where(srow == i1, ninf, sg), axis=0, keepdims=True)
        gs.append(m1 + m2)
    gs = jnp.concatenate(gs, axis=0)
    chosen = jnp.zeros((N_GROUPS, tt), F32)
    for _ in range(TOPK_GROUPS):
        mg = jnp.max(gs, axis=0, keepdims=True)
        gi = jnp.min(jnp.where(gs == mg, srow, float(N_GROUPS)), axis=0, keepdims=True)
        hit = srow == gi
        chosen = jnp.where(hit, 1.0, chosen)
        gs = jnp.where(hit, ninf, gs)
    sel = jnp.concatenate(
        [jnp.where(jnp.broadcast_to(chosen[g:g + 1, :], (GROUP_SIZE, tt)) > 0.5,
                   sel[g * GROUP_SIZE:(g + 1) * GROUP_SIZE, :], ninf) for g in range(N_GROUPS)], axis=0)

    erow = lax.broadcasted_iota(I32, (N_EXPERTS, tt), 0).astype(F32)
    ids, ws = [], []
    for _ in range(TOP_K):
        m = jnp.max(sel, axis=0, keepdims=True)
        ei = jnp.min(jnp.where(sel == m, erow, float(N_EXPERTS)), axis=0, keepdims=True)
        hit = erow == ei
        ids.append(ei)
        ws.append(jnp.sum(jnp.where(hit, scores, 0.0), axis=0, keepdims=True))
        sel = jnp.where(hit, ninf, sel)
    wsum = ws[0]
    for k in range(1, TOP_K):
        wsum = wsum + ws[k]
    w_ref[...] = jnp.concatenate([w / wsum * ROUTED_SCALE for w in ws], axis=0)

    member_f = jnp.zeros((N_EXPERTS, tt), F32)
    for k in range(TOP_K):
        member_f = jnp.where(erow == ids[k], 1.0, member_f)
    member_b = member_f.astype(BF16)
    cnt_row = _dot_nt(jnp.ones((SUBLANE, tt), BF16), member_b)[0:1, :]
    p16_row = jnp.ceil(cnt_row * (1.0 / PIECE)) * PIECE
    lane_e = lax.broadcasted_iota(I32, (N_EXPERTS, N_EXPERTS), 1)
    sub_e = lax.broadcasted_iota(I32, (N_EXPERTS, N_EXPERTS), 0)
    run_start = jnp.sum(jnp.where(lane_e < sub_e, jnp.broadcast_to(p16_row, (N_EXPERTS, N_EXPERTS)), 0.0),
                        axis=1, keepdims=True)
    rank = _dot(member_b, tri_ref[...]) + run_start
    lpos_ref[...] = jnp.concatenate(
        [jnp.sum(jnp.where(erow == ids[k], rank, 0.0), axis=0, keepdims=True) for k in range(TOP_K)],
        axis=0).astype(I32)
    p16_ref[pl.ds(pl.program_id(0), 1), :] = jnp.concatenate(
        [p16_row, jnp.zeros((1, LANE - N_EXPERTS), F32)], axis=1)


def _route(lgt, bias, tri):
    return pl.pallas_call(
        _route_kernel,
        out_shape=(
            jax.ShapeDtypeStruct((TOP_K, N_TOK), F32), jax.ShapeDtypeStruct((TOP_K, N_TOK), I32),
            jax.ShapeDtypeStruct((N_WIN, LANE), F32),
        ),
        grid=(N_WIN,),
        in_specs=[
            pl.BlockSpec((N_EXPERTS, WIN), lambda i: (0, i)),
            pl.BlockSpec((N_EXPERTS, 1), lambda i: (0, 0)),
            pl.BlockSpec((WIN, WIN), lambda i: (0, 0)),
        ],
        out_specs=(
            pl.BlockSpec((TOP_K, WIN), lambda i: (0, i)), pl.BlockSpec((TOP_K, WIN), lambda i: (0, i)),
            pl.BlockSpec((N_WIN, LANE), lambda i: (0, 0)),
        ),
        compiler_params=_cparams(("arbitrary",)),
        name="moe_route",
    )(lgt, bias, tri)


def _sort_kernel(rw_ref, lpos_ref, h_ref, xl_ref):
    n_tiles = (rw_ref[pl.program_id(0)] + SORT_ROWS - 1) // SORT_ROWS
    x = h_ref[...]
    lp = lpos_ref[...]
    riota = lax.broadcasted_iota(I32, (SORT_ROWS, WIN), 0)

    def tile(j, carry):
        r = riota + j * SORT_ROWS
        oh = jnp.zeros((SORT_ROWS, WIN), F32)
        for k in range(TOP_K):
            oh = jnp.where(lp[k:k + 1, :] == r, 1.0, oh)
        xl_ref[pl.ds(pl.multiple_of(j * SORT_ROWS, SORT_ROWS), SORT_ROWS), :] = _dot(oh.astype(BF16), x).astype(BF16)
        return carry

    lax.fori_loop(0, n_tiles, tile, 0)

    def clear(j, carry):
        xl_ref[pl.ds(pl.multiple_of(j * SORT_ROWS, SORT_ROWS), SORT_ROWS), :] = jnp.zeros((SORT_ROWS, D_MODEL), BF16)
        return carry

    lax.fori_loop(n_tiles, RL // SORT_ROWS, clear, 0)


def _sort_rows(rw, lpos, h2):
    last = N_WIN - 1
    return pl.pallas_call(
        _sort_kernel,
        out_shape=jax.ShapeDtypeStruct(((N_WIN + 1) * RL, D_MODEL), BF16),
        grid_spec=pltpu.PrefetchScalarGridSpec(
            num_scalar_prefetch=1,
            grid=(N_WIN + 1,),
            in_specs=[
                pl.BlockSpec((TOP_K, WIN), lambda i, rw: (0, jnp.minimum(i, last))),
                pl.BlockSpec((WIN, D_MODEL), lambda i, rw: (jnp.minimum(i, last), 0)),
            ],
            out_specs=pl.BlockSpec((RL, D_MODEL), lambda i, rw: (i, 0)),
        ),
        compiler_params=_cparams(("arbitrary",)),
        name="moe_sort",
    )(rw, lpos, h2)


def _moe_tables(p16):
    n_w = jnp.arange(N_WIN, dtype=I32)
    run_start = jnp.cumsum(p16, axis=1) - p16
    rw = jnp.sum(p16, axis=1)
    cum_w = jnp.cumsum(p16, axis=0) - p16
    tot = jnp.sum(p16, axis=0)
    nblk = (tot + TME - 1) // TME
    blk_end = jnp.cumsum(nblk)
    blk0 = blk_end - nblk
    n_used = blk_end[-1]
    b = jnp.arange(N_EBLOCKS + 1, dtype=I32)
    block_e = jnp.minimum(jnp.sum((blk_end[None, :] <= b[:, None]).astype(I32), axis=1), N_EXPERTS - 1)
    piece = jnp.arange(PIECES, dtype=I32) * PIECE
    rp = (b - blk0[block_e])[:, None] * TME + piece[None, :]
    valid = (b[:, None] < n_used) & (rp < tot[block_e][:, None])
    cum_e = cum_w.T[block_e]
    len_e = p16.T[block_e]
    start_e = run_start.T[block_e]
    in_win = (cum_e[:, None, :] <= rp[:, :, None]) & (rp[:, :, None] < (cum_e + len_e)[:, None, :])
    row = n_w[None, None, :] * RL + start_e[:, None, :] + rp[:, :, None] - cum_e[:, None, :]
    row = jnp.sum(jnp.where(in_win, row, 0), axis=2)
    zero_src = N_WIN * RL
    trash = N_WIN * RL + (2 + b % 2)[:, None] * TME + piece[None, :]
    gather_row = jnp.where(valid, row, zero_src).reshape(-1).astype(I32)
    scatter_row = jnp.where(valid, row, trash).reshape(-1).astype(I32)
    rw = jnp.concatenate([rw, jnp.zeros((1,), rw.dtype)])
    return rw.astype(I32), block_e.astype(I32), n_used.astype(I32).reshape(1), gather_row, scatter_row


def _expert_kernel(be_ref, nb_ref, grow_ref, srow_ref, xl_hbm, wg_ref, wu_ref, wd_ref, yl_hbm,
                   xbuf, ybuf, gsem, ssem):
    del be_ref
    b = pl.program_id(0)
    nb = nb_ref[0]

    def gather_start(blk, slot):
        for p in range(PIECES):
            src = pl.multiple_of(grow_ref[blk * PIECES + p], PIECE)
            pltpu.make_async_copy(xl_hbm.at[pl.ds(src, PIECE)], xbuf.at[slot, pl.ds(p * PIECE, PIECE)],
                                  gsem.at[slot]).start()

    def gather_wait(slot):
        pltpu.make_async_copy(xl_hbm.at[pl.ds(0, TME)], xbuf.at[slot], gsem.at[slot]).wait()

    def scatter_wait(slot):
        pltpu.make_async_copy(ybuf.at[slot], yl_hbm.at[pl.ds(0, TME)], ssem.at[slot]).wait()

    @pl.when(b < nb)
    def _():
        slot = b % 2

        @pl.when(b == 0)
        def _():
            gather_start(0, 0)

        @pl.when(b + 1 < nb)
        def _():
            gather_start(b + 1, 1 - slot)

        gather_wait(slot)

        @pl.when(b >= 2)
        def _():
            scatter_wait(slot)

        x = xbuf[slot]
        hb = _silu(_dot(x, wg_ref[...])) * _dot(x, wu_ref[...])
        ybuf[slot] = _dot(hb.astype(BF16), wd_ref[...]).astype(BF16)
        for p in range(PIECES):
            dst = pl.multiple_of(srow_ref[b * PIECES + p], PIECE)
            pltpu.make_async_copy(ybuf.at[slot, pl.ds(p * PIECE, PIECE)], yl_hbm.at[pl.ds(dst, PIECE)],
                                  ssem.at[slot]).start()

        @pl.when(b == nb - 1)
        def _():
            scatter_wait(slot)

            @pl.when(b >= 1)
            def _():
                scatter_wait(1 - slot)


def _experts(block_e, n_used, gather_row, scatter_row, xl, wg, wu, wd):
    def w_map(i, be, nb, gr, sr):
        return (be[jnp.minimum(i, nb[0] - 1)], 0, 0)

    return pl.pallas_call(
        _expert_kernel,
        out_shape=jax.ShapeDtypeStruct(((N_WIN + 1) * RL, D_MODEL), BF16),
        input_output_aliases={4: 0},
        grid_spec=pltpu.PrefetchScalarGridSpec(
            num_scalar_prefetch=4,
            grid=(N_EBLOCKS,),
            in_specs=[
                pl.BlockSpec(memory_space=pl.ANY),
                pl.BlockSpec((None, D_MODEL, D_EXPERT), w_map),
                pl.BlockSpec((None, D_MODEL, D_EXPERT), w_map),
                pl.BlockSpec((None, D_EXPERT, D_MODEL), w_map),
            ],
            out_specs=pl.BlockSpec(memory_space=pl.ANY),
            scratch_shapes=[
                pltpu.VMEM((2, TME, D_MODEL), BF16), pltpu.VMEM((2, TME, D_MODEL), BF16),
                pltpu.SemaphoreType.DMA((2,)), pltpu.SemaphoreType.DMA((2,)),
            ],
        ),
        compiler_params=_cparams(("arbitrary",)),
        name="moe_experts",
    )(block_e, n_used, gather_row, scatter_row, xl, wg, wu, wd)


def _combine_kernel(rw_ref, yl_ref, lpt_ref, wt_ref, h_ref, x1_ref, mod_ref, lng_ref, lnb_ref,
                    wsg_ref, wsu_ref, wsd_ref, out_ref, p_hi, p_lo, lp_b, wt_b):
    del rw_ref
    hb = h_ref[...]
    shared = _dot((_silu(_dot(hb, wsg_ref[...])) * _dot(hb, wsu_ref[...])).astype(BF16), wsd_ref[...])
    lp = lpt_ref[...]
    wt = wt_ref[...]
    for k in range(TOP_K):
        lp_b[k] = jnp.broadcast_to(lp[:, k:k + 1], (WIN, SORT_ROWS))
        wt_b[k] = jnp.broadcast_to(wt[:, k:k + 1], (WIN, SORT_ROWS))
    ciota = lax.broadcasted_iota(I32, (WIN, SORT_ROWS), 1)
    for j in range(RL // SORT_ROWS):
        col = ciota + j * SORT_ROWS
        pm = jnp.zeros((WIN, SORT_ROWS), F32)
        for k in range(TOP_K):
            pm = jnp.where(lp_b[k] == col, wt_b[k], pm)
        hi, lo = _split_hi_lo(pm)
        p_hi[:, j * SORT_ROWS:(j + 1) * SORT_ROWS] = hi
        p_lo[:, j * SORT_ROWS:(j + 1) * SORT_ROWS] = lo
    y = yl_ref[...]
    routed = _dot(p_hi[...], y) + _dot(p_lo[...], y)
    z = ALPHA * x1_ref[...] + mod_ref[5:6, :] * (routed + shared)
    out_ref[...] = _layernorm_rows(z, lng_ref[...], lnb_ref[...])


def _combine(rw, yl, lpos_t, wt, h2, x1, mods, lng, lnb, wsg, wsu, wsd):
    full = lambda shape: pl.BlockSpec(shape, lambda i, rw: (0,) * len(shape))
    return pl.pallas_call(
        _combine_kernel,
        out_shape=jax.ShapeDtypeStruct((N_TOK, D_MODEL), F32),
        grid_spec=pltpu.PrefetchScalarGridSpec(
            num_scalar_prefetch=1,
            grid=(N_WIN,),
            in_specs=[
                pl.BlockSpec((RL, D_MODEL), lambda i, rw: (i, 0)),
                pl.BlockSpec((WIN, TOP_K), lambda i, rw: (i, 0)),
                pl.BlockSpec((WIN, TOP_K), lambda i, rw: (i, 0)),
                pl.BlockSpec((WIN, D_MODEL), lambda i, rw: (i, 0)),
                pl.BlockSpec((WIN, D_MODEL), lambda i, rw: (i, 0)),
                pl.BlockSpec((None, MOD_ROWS, D_MODEL), lambda i, rw: (_cond_block(i, WIN), 0, 0)),
                full((1, D_MODEL)), full((1, D_MODEL)),
                full((D_MODEL, D_EXPERT)), full((D_MODEL, D_EXPERT)), full((D_EXPERT, D_MODEL)),
            ],
            out_specs=pl.BlockSpec((WIN, D_MODEL), lambda i, rw: (i, 0)),
            scratch_shapes=[pltpu.VMEM((WIN, RL), BF16), pltpu.VMEM((WIN, RL), BF16),
                            pltpu.VMEM((TOP_K, WIN, SORT_ROWS), I32), pltpu.VMEM((TOP_K, WIN, SORT_ROWS), F32)],
        ),
        compiler_params=_cparams(("arbitrary",)),
        name="moe_combine",
    )(rw, yl, lpos_t, wt, h2, x1, mods, lng, lnb, wsg, wsu, wsd)


def _moe_and_norm(x1, h2, lgt, mods, lng, lnb, router_bias, tri, wg, wu, wd, wsg, wsu, wsd):
    wts, lpos, p16 = _route(lgt, router_bias.reshape(N_EXPERTS, 1), tri)
    rw, block_e, n_used, gather_row, scatter_row = _moe_tables(p16[:, :N_EXPERTS].astype(I32))
    xl = _sort_rows(rw, lpos, h2)
    yl = _experts(block_e, n_used, gather_row, scatter_row, xl, wg, wu, wd)
    return _combine(rw, yl, lpos.T, wts.T, h2, x1, mods, lng, lnb, wsg, wsu, wsd)


def _rope_tables_mla():
    t = jnp.arange(DEC_SEQ)
    row = (t // GRID_W).astype(F32)
    col = (t % GRID_W).astype(F32)
    n = QK_ROPE // 4
    inv = ROPE_BASE ** (-jnp.arange(n, dtype=F32) / n)
    ang_r = row[:, None] * inv
    ang_c = col[:, None] * inv
    cos = jnp.ones((DEC_SEQ, LANE), F32)
    sin_m = jnp.zeros((DEC_SEQ, LANE), F32)
    sin_p = jnp.zeros((DEC_SEQ, LANE), F32)
    l0 = ROPE_LANE0
    for base, ang in ((l0, ang_r), (l0 + 2 * n, ang_c)):
        c, s = jnp.cos(ang), jnp.sin(ang)
        cos = cos.at[:, base:base + n].set(c).at[:, base + n:base + 2 * n].set(c)
        sin_m = sin_m.at[:, base:base + n].set(-s)
        sin_p = sin_p.at[:, base + n:base + 2 * n].set(s)
    ident = (jnp.ones((TM, LANE), F32), jnp.zeros((TM, LANE), F32), jnp.zeros((TM, LANE), F32))
    return tuple(jnp.concatenate([i, tbl], axis=0) for i, tbl in zip(ident, (cos, sin_m, sin_p)))


def _rope_tables_ret():
    half = RET_DK // 2
    theta = ROPE_BASE ** (-jnp.linspace(0.0, 1.0, half, dtype=F32))
    ang = jnp.arange(DEC_SEQ, dtype=F32)[:, None] * theta
    cos = jnp.concatenate([jnp.ones((TM, half), F32), jnp.cos(ang)], axis=0)
    sin = jnp.concatenate([jnp.zeros((TM, half), F32), jnp.sin(ang)], axis=0)
    return cos, sin


def _pad_heads(w, width, lane0=0):
    k = w.shape[0]
    w = w.reshape(k, MLA_HEADS, width)
    out = jnp.zeros((k, MLA_HEADS, HEAD_PAD), w.dtype).at[:, :, lane0:lane0 + width].set(w)
    return out.reshape(k, MLA_HEADS * HEAD_PAD)


def _rg_gate_weights(wa, ba, wx, bx):
    n_ct = RG_W // LANE
    per = LANE // RG_BW
    tiles_w, tiles_b = [], []
    for c in range(n_ct):
        cols_w, cols_b = [], []
        for d in range(2):
            for w, b in ((wa, ba), (wx, bx)):
                m = jnp.zeros((LANE, LANE), F32)
                for p in range(per):
                    m = m.at[p * RG_BW:(p + 1) * RG_BW, p * RG_BW:(p + 1) * RG_BW].set(w[d, c * per + p])
                cols_w.append(m)
                cols_b.append(b[d, c * LANE:(c + 1) * LANE])
        tiles_w.append(jnp.concatenate(cols_w, axis=1))
        tiles_b.append(jnp.concatenate(cols_b, axis=0)[None, :])
    return jnp.stack(tiles_w).astype(BF16), jnp.stack(tiles_b)


def kernel(x_prompt, x_sample, cache_mla_ckv, cache_mla_krope, state_rglru, state_ret, c, c_ctx, w_ada, b_ada,
           ln_g, ln_b, w_in_ab, rg_conv_w, rg_conv_b, rg_wa, rg_ba, rg_wx, rg_bx, rg_lambda, mla_q_norm, mla_w_uq,
           mla_kv_norm, mla_w_ukv, w_out_ab, w_in_c, ret_gamma_logit, w_out_c, w_router, router_bias,
           w_exp_gate, w_exp_up, w_exp_down, w_sh_gate, w_sh_up, w_sh_down):
    x = jnp.concatenate([x_prompt.reshape(N_PROMPT, D_MODEL), x_sample.reshape(N_SAMPLE, D_MODEL)], axis=0)
    cond = jnp.zeros((16, D_MODEL), F32).at[0].set(c_ctx).at[1:1 + DEC_BATCH].set(c)
    mods_all = _ada_modulation(cond, w_ada, b_ada).reshape(DEPTH, 16, 6, D_MODEL)[:, :N_COND]
    mods_all = jnp.pad(mods_all, ((0, 0), (0, 0), (0, MOD_ROWS - 6), (0, 0)))

    tri = (jnp.arange(WIN)[:, None] < jnp.arange(WIN)[None, :]).astype(BF16)
    wr_t = jnp.swapaxes(w_router, 1, 2)
    wr_hi = wr_t.astype(BF16)
    wr_lo = (wr_t - wr_hi.astype(F32)).astype(BF16)
    wg_e, wu_e, wd_e = w_exp_gate.astype(BF16), w_exp_up.astype(BF16), w_exp_down.astype(BF16)
    wsg, wsu, wsd = w_sh_gate.astype(BF16), w_sh_up.astype(BF16), w_sh_down.astype(BF16)

    l, e = 0, 0
    mods = mods_all[l]
    n_main = 2 * RG_W + Q_LORA + KV_LORA
    w_main = w_in_ab[e][:, :n_main].astype(BF16)
    w_kr = jnp.zeros((D_MODEL, LANE), F32).at[:, ROPE_LANE0:ROPE_LANE0 + QK_ROPE].set(w_in_ab[e][:, n_main:]).astype(BF16)
    main, krp = _proj_ab(x, mods, w_main, w_kr)

    wg_rg, bg_rg = _rg_gate_weights(rg_wa[e], rg_ba[e], rg_wx[e], rg_bx[e])
    h0_p = jnp.zeros((BATCH, 2, RG_W), F32)
    rg_args = (rg_conv_w[e], rg_conv_b[e].reshape(1, RG_W), wg_rg, bg_rg, rg_lambda[e])
    yrg, rg_fin = _rglru(main, jnp.zeros((N_TOK, RG_W), F32), *rg_args, h0_p,
                         n_seq=BATCH, seq=SEQ, row_block0=0)
    yrg, _ = _rglru(main, yrg, *rg_args, state_rglru[:, e],
                    n_seq=DEC_BATCH, seq=DEC_SEQ, row_block0=N_PROMPT // DEC_SEQ)

    cos_t, sm_t, sp_t = _rope_tables_mla()
    w_uq = mla_w_uq[e].reshape(Q_LORA, MLA_HEADS, QK_NOPE + QK_ROPE)
    wq_p = _pad_heads(w_uq.reshape(Q_LORA, -1), QK_NOPE + QK_ROPE).astype(BF16)
    w_ukv = mla_w_ukv[e].reshape(KV_LORA, MLA_HEADS, QK_NOPE + V_HEAD)
    wuk_p = _pad_heads(w_ukv[:, :, :QK_NOPE].reshape(KV_LORA, -1), QK_NOPE).astype(BF16)
    wuvt = w_ukv[:, :, QK_NOPE:].reshape(KV_LORA, MLA_HEADS * V_HEAD).T.astype(BF16)
    q_att, k_att, v_att, ckv_n = _mla_prep(main, krp, cos_t, sm_t, sp_t, mla_q_norm[e].reshape(1, Q_LORA), wq_p,
                                           mla_kv_norm[e].reshape(1, KV_LORA), wuk_p, wuvt)
    ctx_ckv = cache_mla_ckv[:, e].reshape(DEC_BATCH * PAST_LEN, KV_LORA)
    ctx_krp = jnp.zeros((DEC_BATCH * PAST_LEN, LANE), F32).at[:, ROPE_LANE0:ROPE_LANE0 + QK_ROPE].set(
        cache_mla_krope[:, e].reshape(DEC_BATCH * PAST_LEN, QK_ROPE))
    kc_att, vc_att = _mla_ctx(ctx_ckv, ctx_krp, wuk_p, wuvt)

    o_att = _attention(q_att, k_att, v_att, jnp.zeros((N_TOK, MLA_HEADS * V_HEAD), BF16), None, None,
                       n_seq=BATCH, seq=SEQ, row_block0=0, tq=SEQ)
    o_att = _attention(q_att, k_att, v_att, o_att, kc_att, vc_att,
                       n_seq=DEC_BATCH, seq=DEC_SEQ, row_block0=N_PROMPT // DEC_SEQ, tq=TQ)

    w_out = w_out_ab[e].astype(BF16)
    x1, h2, lgt = _out_ab(yrg, o_att, w_out[:RG_W], w_out[RG_W:], x, mods,
                          ln_g[l, 0].reshape(1, D_MODEL), ln_b[l, 0].reshape(1, D_MODEL), wr_hi[l], wr_lo[l])
    x = _moe_and_norm(x1, h2, lgt, mods, ln_g[l, 1].reshape(1, D_MODEL), ln_b[l, 1].reshape(1, D_MODEL),
                      router_bias[l], tri, wg_e[l], wu_e[l], wd_e[l], wsg[l], wsu[l], wsd[l])

    new_ckv = ckv_n[:N_PROMPT].reshape(BATCH, 1, SEQ, KV_LORA)
    new_krope = krp[:N_PROMPT, ROPE_LANE0:ROPE_LANE0 + QK_ROPE].reshape(BATCH, 1, SEQ, QK_ROPE)
    new_rg = rg_fin.reshape(BATCH, 1, 2, RG_W)

    l, o = 1, 0
    mods = mods_all[l]
    qk = RET_HEADS * RET_DK
    w_c = w_in_c[o].astype(BF16)
    cos_r, sin_r = _rope_tables_ret()
    q_r, k_r, v_r, g_r = _proj_c(x, mods, cos_r, sin_r, w_c[:, :qk], w_c[:, qk:2 * qk],
                                 w_c[:, 2 * qk:2 * qk + MIX_C], w_c[:, 2 * qk + MIX_C:])
    gam = jnp.broadcast_to(ret_gamma_logit[o].astype(F32)[:, :, None, None], (2, RET_HEADS, SUBLANE, LANE))
    r0_p = jnp.zeros((BATCH, 2, RET_HEADS, RET_DK, RET_DV), F32)
    o_ret, r_fin = _retention(q_r, k_r, v_r, gam, r0_p, jnp.zeros((N_TOK, MIX_C), F32),
                              n_seq=BATCH, seq=SEQ, row_block0=0, with_state=True)
    (o_ret,) = _retention(q_r, k_r, v_r, gam, state_ret[:, o], o_ret,
                          n_seq=DEC_BATCH, seq=DEC_SEQ, row_block0=N_PROMPT // DEC_SEQ, with_state=False)
    x1, h2, lgt = _out_c(o_ret, g_r, w_out_c[o].astype(BF16), x, mods,
                         ln_g[l, 0].reshape(1, D_MODEL), ln_b[l, 0].reshape(1, D_MODEL), wr_hi[l], wr_lo[l])
    x = _moe_and_norm(x1, h2, lgt, mods, ln_g[l, 1].reshape(1, D_MODEL), ln_b[l, 1].reshape(1, D_MODEL),
                      router_bias[l], tri, wg_e[l], wu_e[l], wd_e[l], wsg[l], wsu[l], wsd[l])

    y_prompt = x[:N_PROMPT].reshape(BATCH, SEQ, D_MODEL)
    y_sample = x[N_PROMPT:].reshape(DEC_BATCH, DEC_SEQ, D_MODEL)
    new_ret = r_fin.reshape(BATCH, 1, 2, RET_HEADS, RET_DK, RET_DV)
    return (y_prompt, y_sample, new_ckv, new_krope, new_rg, new_ret)
```

```python
import functools
import math

import jax
import jax.numpy as jnp
from jax import lax
from jax.experimental import pallas as pl
from jax.experimental.pallas import tpu as pltpu

F32 = jnp.float32
BF16 = jnp.bfloat16
I32 = jnp.int32

D_MODEL = 1024
BATCH, SEQ = 16, 256
DEC_BATCH, DEC_SEQ = 8, 4096
PAST_LEN = 256
DEPTH = 2
GRID_W = 64
RG_W, RG_BLOCKS = 512, 8
RG_BW = RG_W // RG_BLOCKS
RG_C = 8.0
CONV_W, CONV_LEFT = 4, 2
MLA_HEADS, QK_NOPE, QK_ROPE, V_HEAD = 8, 64, 32, 64
Q_LORA, KV_LORA = 768, 256
ROPE_BASE = 10000.0
ATTN_SCALE = (QK_NOPE + QK_ROPE) ** -0.5
RET_HEADS, RET_DK, RET_DV, RET_CHUNK = 4, 256, 512, 128
MIX_C = RET_HEADS * RET_DV
N_EXPERTS, TOP_K, N_GROUPS, TOPK_GROUPS = 64, 8, 8, 4
GROUP_SIZE = N_EXPERTS // N_GROUPS
D_EXPERT = 256
ROUTED_SCALE = 2.5
ALPHA = (2 * DEPTH) ** 0.25
EPS = 1e-6

N_PROMPT = BATCH * SEQ
N_SAMPLE = DEC_BATCH * DEC_SEQ
N_TOK = N_PROMPT + N_SAMPLE
N_COND = 1 + DEC_BATCH
MOD_ROWS = 8

LANE = 128
SUBLANE = 8
TM = 512
HEAD_PAD = 128
ROPE_LANE0 = QK_NOPE
TQ = 256
KC = 256
SCAN_ROWS = 64
GATE_ROWS = 256
WIN = 256
N_WIN = N_TOK // WIN
PIECE = 16
SORT_ROWS = 256
RL = 3072
TME = 512
PIECES = TME // PIECE
N_PAIRS = N_TOK * TOP_K
N_EBLOCKS = (N_PAIRS + N_WIN * N_EXPERTS * (PIECE - 1)) // TME + N_EXPERTS
NEG = -1e30


def _cparams(sem, vmem_mb=48):
    return pltpu.CompilerParams(dimension_semantics=sem, vmem_limit_bytes=vmem_mb * 1024 * 1024)


def _cond_block(i, tm):
    npb = N_PROMPT // tm
    return jnp.where(i < npb, 0, 1 + (i - npb) // (DEC_SEQ // tm))


def _pos_block(i, tm):
    npb = N_PROMPT // tm
    return jnp.where(i < npb, 0, 1 + (i - npb) % (DEC_SEQ // tm))


def _split_hi_lo(a):
    hi = a.astype(BF16)
    lo = (a - hi.astype(F32)).astype(BF16)
    return hi, lo


def _dot(a, b):
    return jnp.dot(a, b, preferred_element_type=F32)


def _dot_nt(a, b):
    return lax.dot_general(a, b, (((1,), (1,)), ((), ())), preferred_element_type=F32)


def _dot_tn(a, b):
    return lax.dot_general(a, b, (((0,), (0,)), ((), ())), preferred_element_type=F32)


def _silu(x):
    return x * jax.nn.sigmoid(x)


def _gelu_tanh(x):
    return 0.5 * x * (1.0 + jnp.tanh(math.sqrt(2.0 / math.pi) * (x + 0.044715 * (x * x * x))))


def _softplus(x):
    return jnp.maximum(x, 0.0) + jnp.log1p(jnp.exp(-jnp.abs(x)))


def _layernorm_rows(z, g, b):
    mu = jnp.mean(z, axis=-1, keepdims=True)
    zc = z - mu
    var = jnp.mean(zc * zc, axis=-1, keepdims=True)
    return (zc * lax.rsqrt(var + EPS)) * g + b


def _ada_kernel(c_ref, w_ref, b_ref, o_ref):
    s_hi, s_lo = _split_hi_lo(_silu(c_ref[...]))
    w_hi, w_lo = _split_hi_lo(w_ref[...])
    o_ref[...] = _dot(s_hi, w_hi) + _dot(s_hi, w_lo) + _dot(s_lo, w_hi) + b_ref[...]


def _ada_modulation(cond, w_ada, b_ada):
    n6 = 6 * D_MODEL
    tn = D_MODEL
    return pl.pallas_call(
        _ada_kernel,
        out_shape=jax.ShapeDtypeStruct((DEPTH, 16, n6), F32),
        grid=(DEPTH, n6 // tn),
        in_specs=[
            pl.BlockSpec((16, D_MODEL), lambda l, j: (0, 0)),
            pl.BlockSpec((None, D_MODEL, tn), lambda l, j: (l, 0, j)),
            pl.BlockSpec((None, 1, tn), lambda l, j: (l, 0, j)),
        ],
        out_specs=pl.BlockSpec((None, 16, tn), lambda l, j: (l, 0, j)),
        compiler_params=_cparams(("arbitrary", "arbitrary")),
        name="ada_modulation",
    )(cond, w_ada, b_ada.reshape(DEPTH, 1, n6))


def _modulated(x, mod_ref, shift_row, scale_row):
    return x * (1.0 + mod_ref[scale_row:scale_row + 1, :]) + mod_ref[shift_row:shift_row + 1, :]


def _row_pair_specs(tm, width, col=0):
    npb = N_PROMPT // tm
    return [pl.BlockSpec((tm, width), lambda i, *_: (jnp.minimum(i, npb - 1), col)),
            pl.BlockSpec((tm, width), lambda i, *_: (jnp.maximum(i - npb, 0), col))]


def _pick_rows(p_ref, s_ref, tm):
    return jnp.where(pl.program_id(0) < N_PROMPT // tm, p_ref[...], s_ref[...])


def _proj_ab_kernel(xp_ref, xs_ref, mod_ref, w_ref, wkr_ref, main_ref, kr_ref):
    h = _modulated(_pick_rows(xp_ref, xs_ref, TM), mod_ref, 0, 1).astype(BF16)
    n = w_ref.shape[1]
    step = 512
    for j in range(n // step):
        main_ref[:, j * step:(j + 1) * step] = _dot(h, w_ref[:, j * step:(j + 1) * step])
    kr_ref[...] = _dot(h, wkr_ref[...])


def _proj_ab(xp, xs, mods, w_main, w_kr):
    n = w_main.shape[1]
    return pl.pallas_call(
        _proj_ab_kernel,
        out_shape=(jax.ShapeDtypeStruct((N_TOK, n), F32), jax.ShapeDtypeStruct((N_TOK, LANE), F32)),
        grid=(N_TOK // TM,),
        in_specs=_row_pair_specs(TM, D_MODEL) + [
            pl.BlockSpec((None, MOD_ROWS, D_MODEL), lambda i: (_cond_block(i, TM), 0, 0)),
            pl.BlockSpec((D_MODEL, n), lambda i: (0, 0)),
            pl.BlockSpec((D_MODEL, LANE), lambda i: (0, 0)),
        ],
        out_specs=(pl.BlockSpec((TM, n), lambda i: (i, 0)), pl.BlockSpec((TM, LANE), lambda i: (i, 0))),
        compiler_params=_cparams(("arbitrary",)),
        name="proj_ab",
    )(xp, xs, mods, w_main, w_kr)


def _rglru_kernel(xr_ref, gr_ref, cw_ref, cb_ref, wg_ref, bg_ref, lam_ref, h0_ref,
                  y_ref, hfin_ref, xpad, a_s, b_s, *, seq):
    pad = SUBLANE
    xpad[0:pad, :] = jnp.zeros((pad, LANE), F32)
    xpad[seq + pad:seq + 2 * pad, :] = jnp.zeros((pad, LANE), F32)
    xpad[pad:seq + pad, :] = xr_ref[...]

    sp = _softplus(-lam_ref[...])
    cw = cw_ref[...]
    cb = cb_ref[...]
    wg = wg_ref[...]
    bg = bg_ref[...]

    def gate_step(c, carry):
        t0 = pl.multiple_of(c * GATE_ROWS, GATE_ROWS)
        win = xpad[pl.ds(t0, GATE_ROWS + 2 * pad), :]
        xc = cb
        for j in range(CONV_W):
            off = pad - CONV_LEFT + j
            xc = xc + win[off:off + GATE_ROWS, :] * cw[j:j + 1, :]
        g = _dot(xc.astype(BF16), wg) + bg
        for d in range(2):
            r = jax.nn.sigmoid(g[:, (2 * d) * LANE:(2 * d + 1) * LANE])
            i = jax.nn.sigmoid(g[:, (2 * d + 1) * LANE:(2 * d + 2) * LANE])
            log_a = (-RG_C * r) * sp[d:d + 1, :]
            a = jnp.exp(log_a)
            t = jnp.tanh(log_a)
            bt = jnp.sqrt(2.0 * t / (t - 1.0)) * (i * xc)
            a_s[d, pl.ds(t0, GATE_ROWS), :] = a
            b_s[d, pl.ds(t0, GATE_ROWS), :] = bt
        return carry

    lax.fori_loop(0, seq // GATE_ROWS, gate_step, 0)

    row = lax.broadcasted_iota(I32, (SCAN_ROWS, LANE), 0) % SUBLANE
    n_steps = seq // SCAN_ROWS
    tiles = SCAN_ROWS // SUBLANE

    def local_scan(a, b, reverse):
        for k in (1, 2, 4):
            if reverse:
                ok = row < SUBLANE - k
                shift = SCAN_ROWS - k
            else:
                ok = row >= k
                shift = k
            a_sh = jnp.where(ok, pltpu.roll(a, shift, 0), 1.0)
            b_sh = jnp.where(ok, pltpu.roll(b, shift, 0), 0.0)
            b = a * b_sh + b
            a = a * a_sh
        return a, b

    def fwd_step(c, h):
        t0 = pl.multiple_of(c * SCAN_ROWS, SCAN_ROWS)
        a, b = local_scan(a_s[0, pl.ds(t0, SCAN_ROWS), :], b_s[0, pl.ds(t0, SCAN_ROWS), :], False)
        outs = []
        for j in range(tiles):
            hj = a[j * SUBLANE:(j + 1) * SUBLANE, :] * h + b[j * SUBLANE:(j + 1) * SUBLANE, :]
            outs.append(hj)
            h = hj[SUBLANE - 1:SUBLANE, :]
        y_ref[pl.ds(t0, SCAN_ROWS), :] = jnp.concatenate(outs, axis=0)
        return h

    h_f = lax.fori_loop(0, n_steps, fwd_step, h0_ref[0:1, :])

    def bwd_step(c, h):
        t0 = pl.multiple_of((n_steps - 1 - c) * SCAN_ROWS, SCAN_ROWS)
        a, b = local_scan(a_s[1, pl.ds(t0, SCAN_ROWS), :], b_s[1, pl.ds(t0, SCAN_ROWS), :], True)
        outs = [None] * tiles
        for j in reversed(range(tiles)):
            hj = a[j * SUBLANE:(j + 1) * SUBLANE, :] * h + b[j * SUBLANE:(j + 1) * SUBLANE, :]
            outs[j] = hj
            h = hj[0:1, :]
        hb = jnp.concatenate(outs, axis=0)
        y_ref[pl.ds(t0, SCAN_ROWS), :] = (y_ref[pl.ds(t0, SCAN_ROWS), :] + hb) * _gelu_tanh(gr_ref[pl.ds(t0, SCAN_ROWS), :])
        return h

    h_b = lax.fori_loop(0, n_steps, bwd_step, h0_ref[1:2, :])
    hfin_ref[0:1, :] = h_f
    hfin_ref[1:2, :] = h_b


def _rglru(main, cw, cb, wg, bg, lam, h0, *, n_seq, seq, row_block0):
    n_ct = RG_W // LANE
    gr_col0 = RG_W // LANE
    kern = functools.partial(_rglru_kernel, seq=seq)
    return pl.pallas_call(
        kern,
        out_shape=(jax.ShapeDtypeStruct((n_seq * seq, RG_W), F32), jax.ShapeDtypeStruct((n_seq, 2, RG_W), F32)),
        grid=(n_seq, n_ct),
        in_specs=[
            pl.BlockSpec((seq, LANE), lambda b, c: (row_block0 + b, c)),
            pl.BlockSpec((seq, LANE), lambda b, c: (row_block0 + b, gr_col0 + c)),
            pl.BlockSpec((CONV_W, LANE), lambda b, c: (0, c)),
            pl.BlockSpec((1, LANE), lambda b, c: (0, c)),
            pl.BlockSpec((None, LANE, 4 * LANE), lambda b, c: (c, 0, 0)),
            pl.BlockSpec((None, 1, 4 * LANE), lambda b, c: (c, 0, 0)),
            pl.BlockSpec((2, LANE), lambda b, c: (0, c)),
            pl.BlockSpec((None, 2, LANE), lambda b, c: (b, 0, c)),
        ],
        out_specs=(
            pl.BlockSpec((seq, LANE), lambda b, c: (b, c)),
            pl.BlockSpec((None, 2, LANE), lambda b, c: (b, 0, c)),
        ),
        scratch_shapes=[
            pltpu.VMEM((seq + 2 * SUBLANE, LANE), F32),
            pltpu.VMEM((2, seq, LANE), F32),
            pltpu.VMEM((2, seq, LANE), F32),
        ],
        compiler_params=_cparams(("arbitrary", "arbitrary")),
        name=f"rglru_s{seq}",
    )(main, main, cw, cb, wg, bg, lam, h0)


def _rope_lanes(x, cos, sin_m, sin_p):
    n = x.shape[1] // LANE
    half = QK_ROPE // 4
    cos_t = jnp.concatenate([cos] * n, axis=1) if n > 1 else cos
    sm_t = jnp.concatenate([sin_m] * n, axis=1) if n > 1 else sin_m
    sp_t = jnp.concatenate([sin_p] * n, axis=1) if n > 1 else sin_p
    up = pltpu.roll(x, x.shape[1] - half, 1)
    dn = pltpu.roll(x, half, 1)
    return x * cos_t + up * sm_t + dn * sp_t


def _mla_prep_kernel(cq0_ref, cq1_ref, cq2_ref, ckv_ref, kr_ref, cos_ref, sm_ref, sp_ref,
                     qn_ref, wq_ref, kvn_ref, wuk_ref, wuvt_ref,
                     q_ref, k_ref, vt_ref, ckvn_ref):
    cq = [cq0_ref[...], cq1_ref[...], cq2_ref[...]]
    ms = (jnp.sum(cq[0] * cq[0], axis=-1, keepdims=True) + jnp.sum(cq[1] * cq[1], axis=-1, keepdims=True)
          + jnp.sum(cq[2] * cq[2], axis=-1, keepdims=True)) * (1.0 / Q_LORA)
    inv = lax.rsqrt(ms + EPS)
    blk = Q_LORA // 3
    q = None
    for j in range(3):
        cqn = ((cq[j] * inv) * qn_ref[:, j * blk:(j + 1) * blk]).astype(BF16)
        part = _dot(cqn, wq_ref[j * blk:(j + 1) * blk, :])
        q = part if q is None else q + part
    cos, sm, sp = cos_ref[...], sm_ref[...], sp_ref[...]
    q_ref[...] = _rope_lanes(q, cos, sm, sp).astype(BF16)

    ckv = ckv_ref[...]
    inv_kv = lax.rsqrt(jnp.mean(ckv * ckv, axis=-1, keepdims=True) + EPS)
    ckvn = (ckv * inv_kv) * kvn_ref[...]

    @pl.when(pl.program_id(0) < N_PROMPT // TM)
    def _():
        ckvn_ref[...] = ckvn

    ckvn_b = ckvn.astype(BF16)
    kr_rot = _rope_lanes(kr_ref[...], cos, sm, sp)
    k_ref[...] = (_dot(ckvn_b, wuk_ref[...]) + jnp.concatenate([kr_rot] * MLA_HEADS, axis=1)).astype(BF16)
    vt = _dot_nt(wuvt_ref[...], ckvn_b).astype(BF16)
    for c in range(vt_ref.shape[0]):
        vt_ref[c] = vt[:, c * KC:(c + 1) * KC]


def _mla_prep(main, krp, cos_t, sm_t, sp_t, q_norm, wq_p, kv_norm, wuk_p, wuvt):
    cq_col0 = 2 * RG_W // 256
    hp = MLA_HEADS * HEAD_PAD
    full = lambda shape: pl.BlockSpec(shape, lambda i: (0,) * len(shape))
    tab = pl.BlockSpec((TM, LANE), lambda i: (_pos_block(i, TM), 0))
    return pl.pallas_call(
        _mla_prep_kernel,
        out_shape=(
            jax.ShapeDtypeStruct((N_TOK, hp), BF16),
            jax.ShapeDtypeStruct((N_TOK, hp), BF16),
            jax.ShapeDtypeStruct((N_TOK // KC, MLA_HEADS * V_HEAD, KC), BF16),
            jax.ShapeDtypeStruct((N_PROMPT, KV_LORA), F32),
        ),
        grid=(N_TOK // TM,),
        in_specs=[
            pl.BlockSpec((TM, 256), lambda i: (i, cq_col0)),
            pl.BlockSpec((TM, 256), lambda i: (i, cq_col0 + 1)),
            pl.BlockSpec((TM, 256), lambda i: (i, cq_col0 + 2)),
            pl.BlockSpec((TM, 256), lambda i: (i, cq_col0 + 3)),
            pl.BlockSpec((TM, LANE), lambda i: (i, 0)),
            tab, tab, tab,
            full((1, Q_LORA)), full((Q_LORA, hp)), full((1, KV_LORA)), full((KV_LORA, hp)),
            full((MLA_HEADS * V_HEAD, KV_LORA)),
        ],
        out_specs=(
            pl.BlockSpec((TM, hp), lambda i: (i, 0)),
            pl.BlockSpec((TM, hp), lambda i: (i, 0)),
            pl.BlockSpec((TM // KC, MLA_HEADS * V_HEAD, KC), lambda i: (i, 0, 0)),
            pl.BlockSpec((TM, KV_LORA), lambda i: (jnp.minimum(i, N_PROMPT // TM - 1), 0)),
        ),
        compiler_params=_cparams(("arbitrary",)),
        name="mla_prep",
    )(main, main, main, main, krp, cos_t, sm_t, sp_t, q_norm, wq_p, kv_norm, wuk_p, wuvt)


def _mla_ctx_kernel(ckv_ref, kr_ref, wuk_ref, wuvt_ref, k_ref, vt_ref):
    ckv_b = ckv_ref[...].astype(BF16)
    k_ref[...] = (_dot(ckv_b, wuk_ref[...]) + jnp.concatenate([kr_ref[...]] * MLA_HEADS, axis=1)).astype(BF16)
    vt_ref[...] = _dot_nt(wuvt_ref[...], ckv_b).astype(BF16)


def _mla_ctx(ctx_ckv, ctx_krp, wuk_p, wuvt):
    n = ctx_ckv.shape[0]
    hp = MLA_HEADS * HEAD_PAD
    tm = KC
    full = lambda shape: pl.BlockSpec(shape, lambda i: (0,) * len(shape))
    return pl.pallas_call(
        _mla_ctx_kernel,
        out_shape=(jax.ShapeDtypeStruct((n, hp), BF16),
                   jax.ShapeDtypeStruct((n // tm, MLA_HEADS * V_HEAD, tm), BF16)),
        grid=(n // tm,),
        in_specs=[
            pl.BlockSpec((tm, KV_LORA), lambda i: (i, 0)),
            pl.BlockSpec((tm, LANE), lambda i: (i, 0)),
            full((KV_LORA, hp)), full((MLA_HEADS * V_HEAD, KV_LORA)),
        ],
        out_specs=(pl.BlockSpec((tm, hp), lambda i: (i, 0)),
                   pl.BlockSpec((None, MLA_HEADS * V_HEAD, tm), lambda i: (i, 0, 0))),
        compiler_params=_cparams(("arbitrary",)),
        name="mla_ctx",
    )(ctx_ckv, ctx_krp, wuk_p, wuvt)


def _attn_kernel(*refs, seq, tq, n_ctx):
    if n_ctx:
        q_ref, k_ref, vt_ref, kc_ref, vtc_ref, o_ref, s_scr, p_scr, k_all, vt_all = refs
    else:
        q_ref, k_ref, vt_ref, o_ref, s_scr, p_scr, k_all, vt_all = refs
    has_ctx = 1 if n_ctx else 0
    n_own = seq // KC
    n = n_own + has_ctx
    c_exp = ATTN_SCALE * math.log2(math.e)
    qs = [q_ref[:, h * HEAD_PAD:(h + 1) * HEAD_PAD] for h in range(2)]

    @pl.when(pl.program_id(2) == 0)
    def _():
        k_all[0:seq, :] = k_ref[...]
        vt_all[0:n_own] = vt_ref[...]
        if has_ctx:
            k_all[seq:seq + KC, :] = kc_ref[...]
            vt_all[n_own] = vtc_ref[...]

    def k_chunk(c, h):
        t0 = c * KC if isinstance(c, int) else pl.multiple_of(c * KC, KC)
        return k_all[pl.ds(t0, KC), h * HEAD_PAD:(h + 1) * HEAD_PAD]

    def v_chunk(c, h):
        return vt_all[c, h * V_HEAD:(h + 1) * V_HEAD, :]

    def scores(c, slot):
        for h in range(2):
            s_scr[slot, h] = _dot_nt(k_chunk(c, h), qs[h])

    def softmax_chunk(slot, st):
        out = []
        for h in range(2):
            m, l, _, acc = st[h]
            t = s_scr[slot, h] * c_exp
            m_new = jnp.maximum(m, jnp.max(t, axis=0, keepdims=True))
            alpha = jnp.exp2(m - m_new)
            p = jnp.exp2(t - m_new)
            p_scr[slot, h] = p.astype(BF16)
            out.append((m_new, alpha * l + jnp.sum(p, axis=0, keepdims=True), alpha, acc))
        return out

    def weighted_values(c, slot, st, alphas):
        return [(st[h][0], st[h][1], st[h][2], alphas[h] * st[h][3] + _dot(v_chunk(c, h), p_scr[slot, h]))
                for h in range(2)]

    def step(c, slot, st, with_s, with_v):
        if with_s:
            scores(c + 1, 1 - slot)
        alphas = [st[h][2] for h in range(2)]
        st = softmax_chunk(slot, st)
        if with_v:
            prev = max(c - 1, 0) if isinstance(c, int) else jnp.maximum(c - 1, 0)
            st = weighted_values(prev, 1 - slot, st, alphas)
        return st

    p_scr[1] = jnp.zeros(p_scr.shape[1:], BF16)
    st = [(jnp.full((1, tq), NEG, F32), jnp.zeros((1, tq), F32), jnp.ones((1, tq), F32),
           jnp.zeros((V_HEAD, tq), F32)) for _ in range(2)]
    scores(0, 0)
    n_pairs = (n - 1) // 2

    def pair(j, flat):
        st = [tuple(flat[0:4]), tuple(flat[4:8])]
        st = step(2 * j, 0, st, True, True)
        st = step(2 * j + 1, 1, st, True, True)
        return tuple(st[0]) + tuple(st[1])

    if n_pairs:
        flat = lax.fori_loop(0, n_pairs, pair, tuple(st[0]) + tuple(st[1]))
        st = [tuple(flat[0:4]), tuple(flat[4:8])]
    for c in range(2 * n_pairs, n):
        st = step(c, c % 2, st, c + 1 < n, c > 0)
    st = weighted_values(n - 1, (n - 1) % 2, st, [st[h][2] for h in range(2)])
    for h in range(2):
        o_ref[:, h * V_HEAD:(h + 1) * V_HEAD] = (st[h][3] / st[h][1]).T.astype(o_ref.dtype)


def _attention(q, k, vt, kc, vtc, *, n_seq, seq, row_block0, tq):
    n_ctx = 0 if kc is None else PAST_LEN
    n_hp = MLA_HEADS // 2
    nq = seq // tq
    kern = functools.partial(_attn_kernel, seq=seq, tq=tq, n_ctx=n_ctx)
    in_specs = [
        pl.BlockSpec((tq, 2 * HEAD_PAD), lambda b, j, i: ((row_block0 + b) * nq + i, j)),
        pl.BlockSpec((seq, 2 * HEAD_PAD), lambda b, j, i: (row_block0 + b, j)),
        pl.BlockSpec((seq // KC, 2 * V_HEAD, KC), lambda b, j, i: (row_block0 + b, j, 0)),
    ]
    args = [q, k, vt]
    if n_ctx:
        in_specs += [
            pl.BlockSpec((n_ctx, 2 * HEAD_PAD), lambda b, j, i: (b, j)),
            pl.BlockSpec((None, 2 * V_HEAD, KC), lambda b, j, i: (b, j, 0)),
        ]
        args += [kc, vtc]
    return pl.pallas_call(
        kern,
        out_shape=jax.ShapeDtypeStruct((n_seq * seq, MLA_HEADS * V_HEAD), BF16),
        grid=(n_seq, n_hp, nq),
        in_specs=in_specs,
        out_specs=pl.BlockSpec((tq, 2 * V_HEAD), lambda b, j, i: (b * nq + i, j)),
        scratch_shapes=[pltpu.VMEM((2, 2, KC, tq), F32), pltpu.VMEM((2, 2, KC, tq), BF16),
                        pltpu.VMEM((seq + n_ctx, 2 * HEAD_PAD), BF16),
                        pltpu.VMEM(((seq + n_ctx) // KC, 2 * V_HEAD, KC), BF16)],
        compiler_params=_cparams(("arbitrary", "arbitrary", "arbitrary")),
        name=f"mla_attention_s{seq}",
    )(*args)


def _post_mixer(y, x, mod_ref, lng_ref, lnb_ref, wrh_ref, wrl_ref, x1_ref, h2_ref, lgt_ref):
    z = ALPHA * x + mod_ref[2:3, :] * y
    x1 = _layernorm_rows(z, lng_ref[...], lnb_ref[...])
    x1_ref[...] = x1
    h2 = _modulated(x1, mod_ref, 3, 4)
    h_hi, h_lo = _split_hi_lo(h2)
    h2_ref[...] = h_hi
    w_hi, w_lo = wrh_ref[...], wrl_ref[...]
    lgt_ref[...] = _dot_nt(w_hi, h_hi) + _dot_nt(w_hi, h_lo) + _dot_nt(w_lo, h_hi)


def _out_ab_kernel(yrgp_ref, yrgs_ref, op_ref, os_ref, wa_ref, wb_ref, xp_ref, xs_ref,
                   mod_ref, lng_ref, lnb_ref, wrh_ref, wrl_ref, x1_ref, h2_ref, lgt_ref):
    y = (_dot(_pick_rows(yrgp_ref, yrgs_ref, TM).astype(BF16), wa_ref[...])
         + _dot(_pick_rows(op_ref, os_ref, TM), wb_ref[...]))
    _post_mixer(y, _pick_rows(xp_ref, xs_ref, TM), mod_ref, lng_ref, lnb_ref, wrh_ref, wrl_ref,
                x1_ref, h2_ref, lgt_ref)


def _post_specs():
    full = lambda shape: pl.BlockSpec(shape, lambda i: (0,) * len(shape))
    in_specs = [
        pl.BlockSpec((None, MOD_ROWS, D_MODEL), lambda i: (_cond_block(i, TM), 0, 0)),
        full((1, D_MODEL)), full((1, D_MODEL)),
        full((N_EXPERTS, D_MODEL)), full((N_EXPERTS, D_MODEL)),
    ]
    out_shape = (
        jax.ShapeDtypeStruct((N_TOK, D_MODEL), F32),
        jax.ShapeDtypeStruct((N_TOK, D_MODEL), BF16),
        jax.ShapeDtypeStruct((N_EXPERTS, N_TOK), F32),
    )
    out_specs = (
        pl.BlockSpec((TM, D_MODEL), lambda i: (i, 0)),
        pl.BlockSpec((TM, D_MODEL), lambda i: (i, 0)),
        pl.BlockSpec((N_EXPERTS, TM), lambda i: (0, i)),
    )
    return in_specs, out_shape, out_specs


def _out_ab(yrg_p, yrg_s, o_p, o_s, w_a, w_b, xp, xs, mods, lng, lnb, wr_hi, wr_lo):
    full = lambda shape: pl.BlockSpec(shape, lambda i: (0,) * len(shape))
    post_in, out_shape, out_specs = _post_specs()
    return pl.pallas_call(
        _out_ab_kernel,
        out_shape=out_shape,
        grid=(N_TOK // TM,),
        in_specs=(_row_pair_specs(TM, RG_W) + _row_pair_specs(TM, MLA_HEADS * V_HEAD)
                  + [full((RG_W, D_MODEL)), full((MLA_HEADS * V_HEAD, D_MODEL))]
                  + _row_pair_specs(TM, D_MODEL) + post_in),
        out_specs=out_specs,
        compiler_params=_cparams(("arbitrary",)),
        name="out_ab",
    )(yrg_p, yrg_s, o_p, o_s, w_a, w_b, xp, xs, mods, lng, lnb, wr_hi, wr_lo)


def _proj_c_kernel(x_ref, mod_ref, cos_ref, sin_ref, wq_ref, wk_ref, wv_ref, wg_ref, q_ref, k_ref, v_ref, g_ref):
    h = _modulated(x_ref[...], mod_ref, 0, 1).astype(BF16)
    cos, sin = cos_ref[...], sin_ref[...]
    half = RET_DK // 2
    for hd in range(RET_HEADS):
        for w_ref, o_ref, scale in ((wq_ref, q_ref, 1.0), (wk_ref, k_ref, RET_DK ** -0.5)):
            p = _dot(h, w_ref[:, hd * RET_DK:(hd + 1) * RET_DK])
            x1, x2 = p[:, :half], p[:, half:]
            r1 = x1 * cos - x2 * sin
            r2 = x2 * cos + x1 * sin
            if scale != 1.0:
                r1, r2 = r1 * scale, r2 * scale
            o_ref[:, hd * RET_DK:hd * RET_DK + half] = r1.astype(BF16)
            o_ref[:, hd * RET_DK + half:(hd + 1) * RET_DK] = r2.astype(BF16)
    step = 512
    for j in range(MIX_C // step):
        v_ref[:, j * step:(j + 1) * step] = _dot(h, wv_ref[:, j * step:(j + 1) * step]).astype(BF16)
        g_ref[:, j * step:(j + 1) * step] = _dot(h, wg_ref[:, j * step:(j + 1) * step])


def _proj_c(x, mods, cos_t, sin_t, wq, wk, wv, wg):
    full = lambda shape: pl.BlockSpec(shape, lambda i: (0,) * len(shape))
    qk = RET_HEADS * RET_DK
    tab = pl.BlockSpec((TM, RET_DK // 2), lambda i: (_pos_block(i, TM), 0))
    return pl.pallas_call(
        _proj_c_kernel,
        out_shape=(
            jax.ShapeDtypeStruct((N_TOK, qk), BF16), jax.ShapeDtypeStruct((N_TOK, qk), BF16),
            jax.ShapeDtypeStruct((N_TOK, MIX_C), BF16), jax.ShapeDtypeStruct((N_TOK, MIX_C), F32),
        ),
        grid=(N_TOK // TM,),
        in_specs=[
            pl.BlockSpec((TM, D_MODEL), lambda i: (i, 0)),
            pl.BlockSpec((None, MOD_ROWS, D_MODEL), lambda i: (_cond_block(i, TM), 0, 0)),
            tab, tab,
            full((D_MODEL, qk)), full((D_MODEL, qk)), full((D_MODEL, MIX_C)), full((D_MODEL, MIX_C)),
        ],
        out_specs=(
            pl.BlockSpec((TM, qk), lambda i: (i, 0)), pl.BlockSpec((TM, qk), lambda i: (i, 0)),
            pl.BlockSpec((TM, MIX_C), lambda i: (i, 0)), pl.BlockSpec((TM, MIX_C), lambda i: (i, 0)),
        ),
        compiler_params=_cparams(("arbitrary",), 56),
        name="proj_c",
    )(x, mods, cos_t, sin_t, wq, wk, wv, wg)


def _retention_kernel(*refs, seq, with_state):
    if with_state:
        q_ref, k_ref, v_ref, gam_ref, r0_ref, o_ref, rfin_ref, r_s = refs
    else:
        q_ref, k_ref, v_ref, gam_ref, r0_ref, o_ref, r_s = refs
        rfin_ref = None
    c = RET_CHUNK
    n = seq // c
    ii = lax.broadcasted_iota(I32, (c, c), 0).astype(F32)
    jj = lax.broadcasted_iota(I32, (c, c), 1).astype(F32)
    ci = lax.broadcasted_iota(I32, (c, 1), 0).astype(F32)

    consts = []
    for d in range(2):
        gam = gam_ref[d]
        lg_row = -_softplus(-gam[0:1, :])
        lg = jnp.broadcast_to(lg_row, (c, c))
        lg_col = jnp.broadcast_to(lg_row[:, 0:1], (c, 1))
        if d == 0:
            diff = ii - jj
            xi = jnp.exp((ci + 1.0) * lg_col)
            zeta = jnp.exp((c - 1.0 - ci) * lg_col)
        else:
            diff = jj - ii
            xi = jnp.exp((c - ci) * lg_col)
            zeta = jnp.exp(ci * lg_col)
        dmat = jnp.where(diff >= 0, jnp.exp(jnp.maximum(diff, 0.0) * lg), 0.0)
        g_chunk = jnp.exp(float(c) * lg_row[:, 0:1])
        consts.append((dmat, xi, zeta, g_chunk))
        r_s[d] = r0_ref[d]

    def chunk(d, idx, accumulate):
        dmat, xi, zeta, g_chunk = consts[d]
        t0 = pl.multiple_of(idx * c, c)
        qb = q_ref[pl.ds(t0, c), :]
        kb = k_ref[pl.ds(t0, c), :]
        vb = v_ref[pl.ds(t0, c), :]
        r = r_s[d]
        inner = _dot_nt(qb, kb) * dmat
        o = _dot(inner.astype(BF16), vb) + _dot((qb.astype(F32) * xi).astype(BF16), r.astype(BF16))
        r_s[d] = r * g_chunk + _dot_tn((kb.astype(F32) * zeta).astype(BF16), vb)
        if accumulate:
            o_ref[pl.ds(t0, c), :] = o_ref[pl.ds(t0, c), :] + o
        else:
            o_ref[pl.ds(t0, c), :] = o

    def first_half(s, carry):
        chunk(0, s, False)
        chunk(1, n - 1 - s, False)
        return carry

    def second_half(s, carry):
        chunk(0, s, True)
        chunk(1, n - 1 - s, True)
        return carry

    lax.fori_loop(0, n // 2, first_half, 0)
    lax.fori_loop(n // 2, n, second_half, 0)
    if with_state:
        for d in range(2):
            rfin_ref[d] = r_s[d]


def _retention(q, k, v, gam, r0, *, n_seq, seq, row_block0, with_state):
    kern = functools.partial(_retention_kernel, seq=seq, with_state=with_state)
    out_shape = [jax.ShapeDtypeStruct((n_seq * seq, MIX_C), F32)]
    out_specs = [pl.BlockSpec((seq, RET_DV), lambda b, h: (b, h))]
    if with_state:
        out_shape.append(jax.ShapeDtypeStruct((n_seq, 2, RET_HEADS, RET_DK, RET_DV), F32))
        out_specs.append(pl.BlockSpec((None, 2, None, RET_DK, RET_DV), lambda b, h: (b, 0, h, 0, 0)))
    return pl.pallas_call(
        kern,
        out_shape=tuple(out_shape),
        grid=(n_seq, RET_HEADS),
        in_specs=[
            pl.BlockSpec((seq, RET_DK), lambda b, h: (row_block0 + b, h)),
            pl.BlockSpec((seq, RET_DK), lambda b, h: (row_block0 + b, h)),
            pl.BlockSpec((seq, RET_DV), lambda b, h: (row_block0 + b, h)),
            pl.BlockSpec((2, None, SUBLANE, LANE), lambda b, h: (0, h, 0, 0)),
            pl.BlockSpec((None, 2, None, RET_DK, RET_DV), lambda b, h: (b, 0, h, 0, 0)),
        ],
        out_specs=tuple(out_specs),
        scratch_shapes=[pltpu.VMEM((2, RET_DK, RET_DV), F32)],
        compiler_params=_cparams(("arbitrary", "arbitrary"), 56),
        name=f"retention_s{seq}",
    )(q, k, v, gam, r0)


def _out_c_kernel(op_ref, os_ref, g_ref, w_ref, x_ref, mod_ref, lng_ref, lnb_ref, wrh_ref, wrl_ref,
                  x1_ref, h2_ref, lgt_ref):
    y = None
    o_all = _pick_rows(op_ref, os_ref, TM)
    for hd in range(RET_HEADS):
        o = o_all[:, hd * RET_DV:(hd + 1) * RET_DV]
        mu = jnp.mean(o, axis=-1, keepdims=True)
        oc = o - mu
        var = jnp.mean(oc * oc, axis=-1, keepdims=True)
        on = oc * lax.rsqrt(var + EPS)
        a = (on * _silu(g_ref[:, hd * RET_DV:(hd + 1) * RET_DV])).astype(BF16)
        part = _dot(a, w_ref[hd * RET_DV:(hd + 1) * RET_DV, :])
        y = part if y is None else y + part
    _post_mixer(y, x_ref[...], mod_ref, lng_ref, lnb_ref, wrh_ref, wrl_ref, x1_ref, h2_ref, lgt_ref)


def _out_c(o_p, o_s, g, w, x, mods, lng, lnb, wr_hi, wr_lo):
    full = lambda shape: pl.BlockSpec(shape, lambda i: (0,) * len(shape))
    post_in, out_shape, out_specs = _post_specs()
    return pl.pallas_call(
        _out_c_kernel,
        out_shape=out_shape,
        grid=(N_TOK // TM,),
        in_specs=_row_pair_specs(TM, MIX_C) + [
            pl.BlockSpec((TM, MIX_C), lambda i: (i, 0)),
            full((MIX_C, D_MODEL)),
            pl.BlockSpec((TM, D_MODEL), lambda i: (i, 0)),
        ] + post_in,
        out_specs=out_specs,
        compiler_params=_cparams(("arbitrary",), 56),
        name="out_c",
    )(o_p, o_s, g, w, x, mods, lng, lnb, wr_hi, wr_lo)


def _route_kernel(lgt_ref, bias_ref, tri_ref, w_ref, lpos_ref, p16_ref):
    tt = lgt_ref.shape[1]
    scores = jax.nn.sigmoid(lgt_ref[...])
    sel = scores + bias_ref[...]
    srow = lax.broadcasted_iota(I32, (GROUP_SIZE, tt), 0).astype(F32)
    ninf = -jnp.inf

    gs = []
    for g in range(N_GROUPS):
        sg = sel[g * GROUP_SIZE:(g + 1) * GROUP_SIZE, :]
        m1 = jnp.max(sg, axis=0, keepdims=True)
        i1 = jnp.min(jnp.where(sg == m1, srow, float(GROUP_SIZE)), axis=0, keepdims=True)
        m2 = jnp.max(jnp.where(srow == i1, ninf, sg), axis=0, keepdims=True)
        gs.append(m1 + m2)
    gs = jnp.concatenate(gs, axis=0)
    chosen = jnp.zeros((N_GROUPS, tt), F32)
    for _ in range(TOPK_GROUPS):
        mg = jnp.max(gs, axis=0, keepdims=True)
        gi = jnp.min(jnp.where(gs == mg, srow, float(N_GROUPS)), axis=0, keepdims=True)
        hit = srow == gi
        chosen = jnp.where(hit, 1.0, chosen)
        gs = jnp.where(hit, ninf, gs)
    sel = jnp.concatenate(
        [jnp.where(jnp.broadcast_to(chosen[g:g + 1, :], (GROUP_SIZE, tt)) > 0.5,
                   sel[g * GROUP_SIZE:(g + 1) * GROUP_SIZE, :], ninf) for g in range(N_GROUPS)], axis=0)

    erow = lax.broadcasted_iota(I32, (N_EXPERTS, tt), 0).astype(F32)
    ids, ws = [], []
    for _ in range(TOP_K):
        m = jnp.max(sel, axis=0, keepdims=True)
        ei = jnp.min(jnp.where(sel == m, erow, float(N_EXPERTS)), axis=0, keepdims=True)
        hit = erow == ei
        ids.append(ei)
        ws.append(jnp.sum(jnp.where(hit, scores, 0.0), axis=0, keepdims=True))
        sel = jnp.where(hit, ninf, sel)
    wsum = ws[0]
    for k in range(1, TOP_K):
        wsum = wsum + ws[k]
    w_ref[...] = jnp.concatenate([w / wsum * ROUTED_SCALE for w in ws], axis=0)

    member_f = jnp.zeros((N_EXPERTS, tt), F32)
    for k in range(TOP_K):
        member_f = jnp.where(erow == ids[k], 1.0, member_f)
    member_b = member_f.astype(BF16)
    cnt_row = _dot_nt(jnp.ones((SUBLANE, tt), BF16), member_b)[0:1, :]
    p16_row = jnp.ceil(cnt_row * (1.0 / PIECE)) * PIECE
    lane_e = lax.broadcasted_iota(I32, (N_EXPERTS, N_EXPERTS), 1)
    sub_e = lax.broadcasted_iota(I32, (N_EXPERTS, N_EXPERTS), 0)
    run_start = jnp.sum(jnp.where(lane_e < sub_e, jnp.broadcast_to(p16_row, (N_EXPERTS, N_EXPERTS)), 0.0),
                        axis=1, keepdims=True)
    rank = _dot(member_b, tri_ref[...]) + run_start
    lpos_ref[...] = jnp.concatenate(
        [jnp.sum(jnp.where(erow == ids[k], rank, 0.0), axis=0, keepdims=True) for k in range(TOP_K)],
        axis=0).astype(I32)
    p16_ref[pl.ds(pl.program_id(0), 1), :] = jnp.concatenate(
        [p16_row, jnp.zeros((1, LANE - N_EXPERTS), F32)], axis=1)


def _route(lgt, bias, tri):
    return pl.pallas_call(
        _route_kernel,
        out_shape=(
            jax.ShapeDtypeStruct((TOP_K, N_TOK), F32), jax.ShapeDtypeStruct((TOP_K, N_TOK), I32),
            jax.ShapeDtypeStruct((N_WIN, LANE), F32),
        ),
        grid=(N_WIN,),
        in_specs=[
            pl.BlockSpec((N_EXPERTS, WIN), lambda i: (0, i)),
            pl.BlockSpec((N_EXPERTS, 1), lambda i: (0, 0)),
            pl.BlockSpec((WIN, WIN), lambda i: (0, 0)),
        ],
        out_specs=(
            pl.BlockSpec((TOP_K, WIN), lambda i: (0, i)), pl.BlockSpec((TOP_K, WIN), lambda i: (0, i)),
            pl.BlockSpec((N_WIN, LANE), lambda i: (0, 0)),
        ),
        compiler_params=_cparams(("arbitrary",)),
        name="moe_route",
    )(lgt, bias, tri)


def _sort_kernel(rw_ref, lpos_ref, h_ref, xl_ref):
    n_tiles = (rw_ref[pl.program_id(0)] + SORT_ROWS - 1) // SORT_ROWS
    x = h_ref[...]
    lp = lpos_ref[...]
    riota = lax.broadcasted_iota(I32, (SORT_ROWS, WIN), 0).astype(jnp.int16)
    one = jnp.ones((SORT_ROWS, WIN), BF16)

    def tile(j, carry):
        rel = (lp - j * SORT_ROWS).astype(jnp.int16)
        oh = jnp.zeros((SORT_ROWS, WIN), BF16)
        for k in range(TOP_K):
            oh = jnp.where(rel[k:k + 1, :] == riota, one, oh)
        xl_ref[pl.ds(pl.multiple_of(j * SORT_ROWS, SORT_ROWS), SORT_ROWS), :] = _dot(oh, x).astype(BF16)
        return carry

    lax.fori_loop(0, n_tiles, tile, 0)

    def clear(j, carry):
        xl_ref[pl.ds(pl.multiple_of(j * SORT_ROWS, SORT_ROWS), SORT_ROWS), :] = jnp.zeros((SORT_ROWS, D_MODEL), BF16)
        return carry

    lax.fori_loop(n_tiles, RL // SORT_ROWS, clear, 0)


def _sort_rows(rw, lpos, h2):
    last = N_WIN - 1
    return pl.pallas_call(
        _sort_kernel,
        out_shape=jax.ShapeDtypeStruct(((N_WIN + 1) * RL, D_MODEL), BF16),
        grid_spec=pltpu.PrefetchScalarGridSpec(
            num_scalar_prefetch=1,
            grid=(N_WIN + 1,),
            in_specs=[
                pl.BlockSpec((TOP_K, WIN), lambda i, rw: (0, jnp.minimum(i, last))),
                pl.BlockSpec((WIN, D_MODEL), lambda i, rw: (jnp.minimum(i, last), 0)),
            ],
            out_specs=pl.BlockSpec((RL, D_MODEL), lambda i, rw: (i, 0)),
        ),
        compiler_params=_cparams(("arbitrary",)),
        name="moe_sort",
    )(rw, lpos, h2)


def _moe_tables(p16):
    n_w = jnp.arange(N_WIN, dtype=I32)
    run_start = jnp.cumsum(p16, axis=1) - p16
    rw = jnp.sum(p16, axis=1)
    cum_w = jnp.cumsum(p16, axis=0) - p16
    tot = jnp.sum(p16, axis=0)
    nblk = (tot + TME - 1) // TME
    blk_end = jnp.cumsum(nblk)
    blk0 = blk_end - nblk
    n_used = blk_end[-1]
    b = jnp.arange(N_EBLOCKS + 1, dtype=I32)
    block_e = jnp.minimum(jnp.sum((blk_end[None, :] <= b[:, None]).astype(I32), axis=1), N_EXPERTS - 1)
    piece = jnp.arange(PIECES, dtype=I32) * PIECE
    rp = (b - blk0[block_e])[:, None] * TME + piece[None, :]
    valid = (b[:, None] < n_used) & (rp < tot[block_e][:, None])
    cum_e = cum_w.T[block_e]
    len_e = p16.T[block_e]
    start_e = run_start.T[block_e]
    in_win = (cum_e[:, None, :] <= rp[:, :, None]) & (rp[:, :, None] < (cum_e + len_e)[:, None, :])
    row = n_w[None, None, :] * RL + start_e[:, None, :] + rp[:, :, None] - cum_e[:, None, :]
    row = jnp.sum(jnp.where(in_win, row, 0), axis=2)
    zero_src = N_WIN * RL
    trash = N_WIN * RL + (2 + b % 2)[:, None] * TME + piece[None, :]
    gather_row = jnp.where(valid, row, zero_src).reshape(-1).astype(I32)
    scatter_row = jnp.where(valid, row, trash).reshape(-1).astype(I32)
    rw = jnp.concatenate([rw, jnp.zeros((1,), rw.dtype)])
    return rw.astype(I32), block_e.astype(I32), n_used.astype(I32).reshape(1), gather_row, scatter_row


def _expert_kernel(be_ref, nb_ref, grow_ref, srow_ref, xl_hbm, wg_ref, wu_ref, wd_ref, yl_hbm,
                   xbuf, ybuf, wg_b, wu_b, wd_b, gsem, ssem):
    b = pl.program_id(0)
    nb = nb_ref[0]

    @pl.when(jnp.logical_and(b < nb, jnp.logical_or(b == 0, be_ref[b] != be_ref[jnp.maximum(b - 1, 0)])))
    def _():
        wg_b[...] = wg_ref[...].astype(BF16)
        wu_b[...] = wu_ref[...].astype(BF16)
        wd_b[...] = wd_ref[...].astype(BF16)

    def gather_start(blk, slot):
        for p in range(PIECES):
            src = pl.multiple_of(grow_ref[blk * PIECES + p], PIECE)
            pltpu.make_async_copy(xl_hbm.at[pl.ds(src, PIECE)], xbuf.at[slot, pl.ds(p * PIECE, PIECE)],
                                  gsem.at[slot]).start()

    def gather_wait(slot):
        pltpu.make_async_copy(xl_hbm.at[pl.ds(0, TME)], xbuf.at[slot], gsem.at[slot]).wait()

    def scatter_wait(slot):
        pltpu.make_async_copy(ybuf.at[slot], yl_hbm.at[pl.ds(0, TME)], ssem.at[slot]).wait()

    @pl.when(b < nb)
    def _():
        slot = b % 2

        @pl.when(b == 0)
        def _():
            gather_start(0, 0)

        @pl.when(b + 1 < nb)
        def _():
            gather_start(b + 1, 1 - slot)

        gather_wait(slot)

        @pl.when(b >= 2)
        def _():
            scatter_wait(slot)

        x = xbuf[slot]
        hb = _silu(_dot(x, wg_b[...])) * _dot(x, wu_b[...])
        ybuf[slot] = _dot(hb.astype(BF16), wd_b[...]).astype(BF16)
        for p in range(PIECES):
            dst = pl.multiple_of(srow_ref[b * PIECES + p], PIECE)
            pltpu.make_async_copy(ybuf.at[slot, pl.ds(p * PIECE, PIECE)], yl_hbm.at[pl.ds(dst, PIECE)],
                                  ssem.at[slot]).start()

        @pl.when(b == nb - 1)
        def _():
            scatter_wait(slot)

            @pl.when(b >= 1)
            def _():
                scatter_wait(1 - slot)


def _experts(block_e, n_used, gather_row, scatter_row, xl, wg, wu, wd):
    def w_map(i, be, nb, gr, sr):
        return (be[jnp.minimum(i, nb[0] - 1)], 0, 0)

    return pl.pallas_call(
        _expert_kernel,
        out_shape=jax.ShapeDtypeStruct(((N_WIN + 1) * RL, D_MODEL), BF16),
        input_output_aliases={4: 0},
        grid_spec=pltpu.PrefetchScalarGridSpec(
            num_scalar_prefetch=4,
            grid=(N_EBLOCKS,),
            in_specs=[
                pl.BlockSpec(memory_space=pl.ANY),
                pl.BlockSpec((None, D_MODEL, D_EXPERT), w_map),
                pl.BlockSpec((None, D_MODEL, D_EXPERT), w_map),
                pl.BlockSpec((None, D_EXPERT, D_MODEL), w_map),
            ],
            out_specs=pl.BlockSpec(memory_space=pl.ANY),
            scratch_shapes=[
                pltpu.VMEM((2, TME, D_MODEL), BF16), pltpu.VMEM((2, TME, D_MODEL), BF16),
                pltpu.VMEM((D_MODEL, D_EXPERT), BF16), pltpu.VMEM((D_MODEL, D_EXPERT), BF16),
                pltpu.VMEM((D_EXPERT, D_MODEL), BF16),
                pltpu.SemaphoreType.DMA((2,)), pltpu.SemaphoreType.DMA((2,)),
            ],
        ),
        compiler_params=_cparams(("arbitrary",)),
        name="moe_experts",
    )(block_e, n_used, gather_row, scatter_row, xl, wg, wu, wd)


def _combine_kernel(rw_ref, yl_ref, lpt_ref, wt_ref, h_ref, x1_ref, mod_ref, lng_ref, lnb_ref,
                    wsg_ref, wsu_ref, wsd_ref, *rest, split):
    del rw_ref
    if split:
        outp_ref, outs_ref, p_w, lp_b, wt_b = rest
    else:
        out_ref, p_w, lp_b, wt_b = rest
    hb = h_ref[...]
    shared = _dot((_silu(_dot(hb, wsg_ref[...])) * _dot(hb, wsu_ref[...])).astype(BF16), wsd_ref[...])
    lp = lpt_ref[...]
    wt = wt_ref[...]
    for k in range(TOP_K):
        lp_b[k] = jnp.broadcast_to(lp[:, k:k + 1], (WIN, SORT_ROWS))
        wt_b[k] = jnp.broadcast_to(wt[:, k:k + 1], (WIN, SORT_ROWS))
    ciota = lax.broadcasted_iota(I32, (WIN, SORT_ROWS), 1)
    for j in range(RL // SORT_ROWS):
        col = ciota + j * SORT_ROWS
        pm = jnp.zeros((WIN, SORT_ROWS), F32)
        for k in range(TOP_K):
            pm = jnp.where(lp_b[k] == col, wt_b[k], pm)
        p_w[:, j * SORT_ROWS:(j + 1) * SORT_ROWS] = pm.astype(BF16)
    routed = _dot(p_w[...], yl_ref[...])
    z = ALPHA * x1_ref[...] + mod_ref[5:6, :] * (routed + shared)
    out = _layernorm_rows(z, lng_ref[...], lnb_ref[...])
    if split:
        @pl.when(pl.program_id(0) < N_PROMPT // WIN)
        def _():
            outp_ref[...] = out

        @pl.when(pl.program_id(0) >= N_PROMPT // WIN)
        def _():
            outs_ref[...] = out
    else:
        out_ref[...] = out


def _combine(rw, yl, lpos_t, wt, h2, x1, mods, lng, lnb, wsg, wsu, wsd, *, split):
    full = lambda shape: pl.BlockSpec(shape, lambda i, rw: (0,) * len(shape))
    if split:
        out_shape = (jax.ShapeDtypeStruct((N_PROMPT, D_MODEL), F32), jax.ShapeDtypeStruct((N_SAMPLE, D_MODEL), F32))
        out_specs = tuple(_row_pair_specs(WIN, D_MODEL))
    else:
        out_shape = jax.ShapeDtypeStruct((N_TOK, D_MODEL), F32)
        out_specs = pl.BlockSpec((WIN, D_MODEL), lambda i, rw: (i, 0))
    return pl.pallas_call(
        functools.partial(_combine_kernel, split=split),
        out_shape=out_shape,
        grid_spec=pltpu.PrefetchScalarGridSpec(
            num_scalar_prefetch=1,
            grid=(N_WIN,),
            in_specs=[
                pl.BlockSpec((RL, D_MODEL), lambda i, rw: (i, 0)),
                pl.BlockSpec((WIN, TOP_K), lambda i, rw: (i, 0)),
                pl.BlockSpec((WIN, TOP_K), lambda i, rw: (i, 0)),
                pl.BlockSpec((WIN, D_MODEL), lambda i, rw: (i, 0)),
                pl.BlockSpec((WIN, D_MODEL), lambda i, rw: (i, 0)),
                pl.BlockSpec((None, MOD_ROWS, D_MODEL), lambda i, rw: (_cond_block(i, WIN), 0, 0)),
                full((1, D_MODEL)), full((1, D_MODEL)),
                full((D_MODEL, D_EXPERT)), full((D_MODEL, D_EXPERT)), full((D_EXPERT, D_MODEL)),
            ],
            out_specs=out_specs,
            scratch_shapes=[pltpu.VMEM((WIN, RL), BF16),
                            pltpu.VMEM((TOP_K, WIN, SORT_ROWS), I32), pltpu.VMEM((TOP_K, WIN, SORT_ROWS), F32)],
        ),
        compiler_params=_cparams(("arbitrary",)),
        name="moe_combine",
    )(rw, yl, lpos_t, wt, h2, x1, mods, lng, lnb, wsg, wsu, wsd)


def _moe_and_norm(x1, h2, lgt, mods, lng, lnb, router_bias, tri, wg, wu, wd, wsg, wsu, wsd, *, split):
    wts, lpos, p16 = _route(lgt, router_bias.reshape(N_EXPERTS, 1), tri)
    rw, block_e, n_used, gather_row, scatter_row = _moe_tables(p16[:, :N_EXPERTS].astype(I32))
    xl = _sort_rows(rw, lpos, h2)
    yl = _experts(block_e, n_used, gather_row, scatter_row, xl, wg, wu, wd)
    return _combine(rw, yl, lpos.T, wts.T, h2, x1, mods, lng, lnb, wsg, wsu, wsd, split=split)


def _rope_tables_mla():
    t = jnp.arange(DEC_SEQ)
    row = (t // GRID_W).astype(F32)
    col = (t % GRID_W).astype(F32)
    n = QK_ROPE // 4
    inv = ROPE_BASE ** (-jnp.arange(n, dtype=F32) / n)
    ang_r = row[:, None] * inv
    ang_c = col[:, None] * inv
    cos = jnp.ones((DEC_SEQ, LANE), F32)
    sin_m = jnp.zeros((DEC_SEQ, LANE), F32)
    sin_p = jnp.zeros((DEC_SEQ, LANE), F32)
    l0 = ROPE_LANE0
    for base, ang in ((l0, ang_r), (l0 + 2 * n, ang_c)):
        c, s = jnp.cos(ang), jnp.sin(ang)
        cos = cos.at[:, base:base + n].set(c).at[:, base + n:base + 2 * n].set(c)
        sin_m = sin_m.at[:, base:base + n].set(-s)
        sin_p = sin_p.at[:, base + n:base + 2 * n].set(s)
    ident = (jnp.ones((TM, LANE), F32), jnp.zeros((TM, LANE), F32), jnp.zeros((TM, LANE), F32))
    return tuple(jnp.concatenate([i, tbl], axis=0) for i, tbl in zip(ident, (cos, sin_m, sin_p)))


def _rope_tables_ret():
    half = RET_DK // 2
    theta = ROPE_BASE ** (-jnp.linspace(0.0, 1.0, half, dtype=F32))
    ang = jnp.arange(DEC_SEQ, dtype=F32)[:, None] * theta
    cos = jnp.concatenate([jnp.ones((TM, half), F32), jnp.cos(ang)], axis=0)
    sin = jnp.concatenate([jnp.zeros((TM, half), F32), jnp.sin(ang)], axis=0)
    return cos, sin


def _pad_heads(w, width, lane0=0):
    k = w.shape[0]
    w = w.reshape(k, MLA_HEADS, width)
    out = jnp.zeros((k, MLA_HEADS, HEAD_PAD), w.dtype).at[:, :, lane0:lane0 + width].set(w)
    return out.reshape(k, MLA_HEADS * HEAD_PAD)


def _rg_gate_weights(wa, ba, wx, bx):
    n_ct = RG_W // LANE
    per = LANE // RG_BW
    tiles_w, tiles_b = [], []
    for c in range(n_ct):
        cols_w, cols_b = [], []
        for d in range(2):
            for w, b in ((wa, ba), (wx, bx)):
                m = jnp.zeros((LANE, LANE), F32)
                for p in range(per):
                    m = m.at[p * RG_BW:(p + 1) * RG_BW, p * RG_BW:(p + 1) * RG_BW].set(w[d, c * per + p])
                cols_w.append(m)
                cols_b.append(b[d, c * LANE:(c + 1) * LANE])
        tiles_w.append(jnp.concatenate(cols_w, axis=1))
        tiles_b.append(jnp.concatenate(cols_b, axis=0)[None, :])
    return jnp.stack(tiles_w).astype(BF16), jnp.stack(tiles_b)


def kernel(x_prompt, x_sample, cache_mla_ckv, cache_mla_krope, state_rglru, state_ret, c, c_ctx, w_ada, b_ada,
           ln_g, ln_b, w_in_ab, rg_conv_w, rg_conv_b, rg_wa, rg_ba, rg_wx, rg_bx, rg_lambda, mla_q_norm, mla_w_uq,
           mla_kv_norm, mla_w_ukv, w_out_ab, w_in_c, ret_gamma_logit, w_out_c, w_router, router_bias,
           w_exp_gate, w_exp_up, w_exp_down, w_sh_gate, w_sh_up, w_sh_down):
    xp = x_prompt.reshape(N_PROMPT, D_MODEL)
    xs = x_sample.reshape(N_SAMPLE, D_MODEL)
    cond = jnp.zeros((16, D_MODEL), F32).at[0].set(c_ctx).at[1:1 + DEC_BATCH].set(c)
    mods_all = _ada_modulation(cond, w_ada, b_ada).reshape(DEPTH, 16, 6, D_MODEL)[:, :N_COND]
    mods_all = jnp.pad(mods_all, ((0, 0), (0, 0), (0, MOD_ROWS - 6), (0, 0)))

    tri = (jnp.arange(WIN)[:, None] < jnp.arange(WIN)[None, :]).astype(BF16)
    wr_t = jnp.swapaxes(w_router, 1, 2)
    wr_hi = wr_t.astype(BF16)
    wr_lo = (wr_t - wr_hi.astype(F32)).astype(BF16)
    wg_e, wu_e, wd_e = w_exp_gate, w_exp_up, w_exp_down
    wsg, wsu, wsd = w_sh_gate.astype(BF16), w_sh_up.astype(BF16), w_sh_down.astype(BF16)

    l, e = 0, 0
    mods = mods_all[l]
    n_main = 2 * RG_W + Q_LORA + KV_LORA
    w_main = w_in_ab[e][:, :n_main].astype(BF16)
    w_kr = jnp.zeros((D_MODEL, LANE), F32).at[:, ROPE_LANE0:ROPE_LANE0 + QK_ROPE].set(w_in_ab[e][:, n_main:]).astype(BF16)
    main, krp = _proj_ab(xp, xs, mods, w_main, w_kr)

    wg_rg, bg_rg = _rg_gate_weights(rg_wa[e], rg_ba[e], rg_wx[e], rg_bx[e])
    h0_p = jnp.zeros((BATCH, 2, RG_W), F32)
    rg_args = (rg_conv_w[e], rg_conv_b[e].reshape(1, RG_W), wg_rg, bg_rg, rg_lambda[e])
    yrg_p, rg_fin = _rglru(main, *rg_args, h0_p, n_seq=BATCH, seq=SEQ, row_block0=0)
    yrg_s, _ = _rglru(main, *rg_args, state_rglru[:, e],
                      n_seq=DEC_BATCH, seq=DEC_SEQ, row_block0=N_PROMPT // DEC_SEQ)

    cos_t, sm_t, sp_t = _rope_tables_mla()
    w_uq = mla_w_uq[e].reshape(Q_LORA, MLA_HEADS, QK_NOPE + QK_ROPE)
    wq_p = _pad_heads(w_uq.reshape(Q_LORA, -1), QK_NOPE + QK_ROPE).astype(BF16)
    w_ukv = mla_w_ukv[e].reshape(KV_LORA, MLA_HEADS, QK_NOPE + V_HEAD)
    wuk_p = _pad_heads(w_ukv[:, :, :QK_NOPE].reshape(KV_LORA, -1), QK_NOPE).astype(BF16)
    wuvt = w_ukv[:, :, QK_NOPE:].reshape(KV_LORA, MLA_HEADS * V_HEAD).T.astype(BF16)
    q_att, k_att, v_att, ckv_n = _mla_prep(main, krp, cos_t, sm_t, sp_t, mla_q_norm[e].reshape(1, Q_LORA), wq_p,
                                           mla_kv_norm[e].reshape(1, KV_LORA), wuk_p, wuvt)
    ctx_ckv = cache_mla_ckv[:, e].reshape(DEC_BATCH * PAST_LEN, KV_LORA)
    ctx_krp = jnp.zeros((DEC_BATCH * PAST_LEN, LANE), F32).at[:, ROPE_LANE0:ROPE_LANE0 + QK_ROPE].set(
        cache_mla_krope[:, e].reshape(DEC_BATCH * PAST_LEN, QK_ROPE))
    kc_att, vc_att = _mla_ctx(ctx_ckv, ctx_krp, wuk_p, wuvt)

    o_att_p = _attention(q_att, k_att, v_att, None, None, n_seq=BATCH, seq=SEQ, row_block0=0, tq=SEQ)
    o_att_s = _attention(q_att, k_att, v_att, kc_att, vc_att,
                         n_seq=DEC_BATCH, seq=DEC_SEQ, row_block0=N_PROMPT // DEC_SEQ, tq=TQ)

    w_out = w_out_ab[e].astype(BF16)
    x1, h2, lgt = _out_ab(yrg_p, yrg_s, o_att_p, o_att_s, w_out[:RG_W], w_out[RG_W:], xp, xs, mods,
                          ln_g[l, 0].reshape(1, D_MODEL), ln_b[l, 0].reshape(1, D_MODEL), wr_hi[l], wr_lo[l])
    x = _moe_and_norm(x1, h2, lgt, mods, ln_g[l, 1].reshape(1, D_MODEL), ln_b[l, 1].reshape(1, D_MODEL),
                      router_bias[l], tri, wg_e[l], wu_e[l], wd_e[l], wsg[l], wsu[l], wsd[l], split=False)

    new_ckv = ckv_n.reshape(BATCH, 1, SEQ, KV_LORA)
    new_krope = krp[:N_PROMPT, ROPE_LANE0:ROPE_LANE0 + QK_ROPE].reshape(BATCH, 1, SEQ, QK_ROPE)
    new_rg = rg_fin.reshape(BATCH, 1, 2, RG_W)

    l, o = 1, 0
    mods = mods_all[l]
    qk = RET_HEADS * RET_DK
    w_c = w_in_c[o].astype(BF16)
    cos_r, sin_r = _rope_tables_ret()
    q_r, k_r, v_r, g_r = _proj_c(x, mods, cos_r, sin_r, w_c[:, :qk], w_c[:, qk:2 * qk],
                                 w_c[:, 2 * qk:2 * qk + MIX_C], w_c[:, 2 * qk + MIX_C:])
    gam = jnp.broadcast_to(ret_gamma_logit[o].astype(F32)[:, :, None, None], (2, RET_HEADS, SUBLANE, LANE))
    r0_p = jnp.zeros((BATCH, 2, RET_HEADS, RET_DK, RET_DV), F32)
    o_ret_p, r_fin = _retention(q_r, k_r, v_r, gam, r0_p, n_seq=BATCH, seq=SEQ, row_block0=0, with_state=True)
    (o_ret_s,) = _retention(q_r, k_r, v_r, gam, state_ret[:, o],
                            n_seq=DEC_BATCH, seq=DEC_SEQ, row_block0=N_PROMPT // DEC_SEQ, with_state=False)
    x1, h2, lgt = _out_c(o_ret_p, o_ret_s, g_r, w_out_c[o].astype(BF16), x, mods,
                         ln_g[l, 0].reshape(1, D_MODEL), ln_b[l, 0].reshape(1, D_MODEL), wr_hi[l], wr_lo[l])
    y_p, y_s = _moe_and_norm(x1, h2, lgt, mods, ln_g[l, 1].reshape(1, D_MODEL), ln_b[l, 1].reshape(1, D_MODEL),
                             router_bias[l], tri, wg_e[l], wu_e[l], wd_e[l], wsg[l], wsu[l], wsd[l], split=True)

    y_prompt = y_p.reshape(BATCH, SEQ, D_MODEL)
    y_sample = y_s.reshape(DEC_BATCH, DEC_SEQ, D_MODEL)
    new_ret = r_fin.reshape(BATCH, 1, 2, RET_HEADS, RET_DK, RET_DV)
    return (y_prompt, y_sample, new_ckv, new_krope, new_rg, new_ret)
```

```python
import functools
import math

import jax
import jax.numpy as jnp
from jax import lax
from jax.experimental import pallas as pl
from jax.experimental.pallas import tpu as pltpu

F32 = jnp.float32
BF16 = jnp.bfloat16
I32 = jnp.int32

D_MODEL = 1024
BATCH, SEQ = 16, 256
DEC_BATCH, DEC_SEQ = 8, 4096
PAST_LEN = 256
DEPTH = 2
GRID_W = 64
RG_W, RG_BLOCKS = 512, 8
RG_BW = RG_W // RG_BLOCKS
RG_C = 8.0
CONV_W, CONV_LEFT = 4, 2
MLA_HEADS, QK_NOPE, QK_ROPE, V_HEAD = 8, 64, 32, 64
Q_LORA, KV_LORA = 768, 256
ROPE_BASE = 10000.0
ATTN_SCALE = (QK_NOPE + QK_ROPE) ** -0.5
RET_HEADS, RET_DK, RET_DV, RET_CHUNK = 4, 256, 512, 128
MIX_C = RET_HEADS * RET_DV
N_EXPERTS, TOP_K, N_GROUPS, TOPK_GROUPS = 64, 8, 8, 4
GROUP_SIZE = N_EXPERTS // N_GROUPS
D_EXPERT = 256
ROUTED_SCALE = 2.5
ALPHA = (2 * DEPTH) ** 0.25
EPS = 1e-6

N_PROMPT = BATCH * SEQ
N_SAMPLE = DEC_BATCH * DEC_SEQ
N_TOK = N_PROMPT + N_SAMPLE
N_COND = 1 + DEC_BATCH
MOD_ROWS = 8

LANE = 128
SUBLANE = 8
TM = 512
HEAD_PAD = 128
ROPE_LANE0 = QK_NOPE
TQ = 256
KC = 256
SCAN_ROWS = 64
GATE_ROWS = 256
WIN = 256
N_WIN = N_TOK // WIN
PIECE = 16
SORT_ROWS = 256
RL = 3072
TME = 512
PIECES = TME // PIECE
N_PAIRS = N_TOK * TOP_K
N_EBLOCKS = (N_PAIRS + N_WIN * N_EXPERTS * (PIECE - 1)) // TME + N_EXPERTS
NEG = -1e30


def _cparams(sem, vmem_mb=48):
    return pltpu.CompilerParams(dimension_semantics=sem, vmem_limit_bytes=vmem_mb * 1024 * 1024)


def _cond_block(i, tm):
    npb = N_PROMPT // tm
    return jnp.where(i < npb, 0, 1 + (i - npb) // (DEC_SEQ // tm))


def _pos_block(i, tm):
    npb = N_PROMPT // tm
    return jnp.where(i < npb, 0, 1 + (i - npb) % (DEC_SEQ // tm))


def _split_hi_lo(a):
    hi = a.astype(BF16)
    lo = (a - hi.astype(F32)).astype(BF16)
    return hi, lo


def _dot(a, b):
    return jnp.dot(a, b, preferred_element_type=F32)


def _dot_nt(a, b):
    return lax.dot_general(a, b, (((1,), (1,)), ((), ())), preferred_element_type=F32)


def _dot_tn(a, b):
    return lax.dot_general(a, b, (((0,), (0,)), ((), ())), preferred_element_type=F32)


def _silu(x):
    return x * jax.nn.sigmoid(x)


def _gelu_tanh(x):
    return 0.5 * x * (1.0 + jnp.tanh(math.sqrt(2.0 / math.pi) * (x + 0.044715 * (x * x * x))))


def _softplus(x):
    return jnp.maximum(x, 0.0) + jnp.log1p(jnp.exp(-jnp.abs(x)))


def _layernorm_rows(z, g, b):
    mu = jnp.mean(z, axis=-1, keepdims=True)
    zc = z - mu
    var = jnp.mean(zc * zc, axis=-1, keepdims=True)
    return (zc * lax.rsqrt(var + EPS)) * g + b


def _ada_kernel(c_ref, w_ref, b_ref, o_ref):
    s_hi, s_lo = _split_hi_lo(_silu(c_ref[...]))
    w_hi, w_lo = _split_hi_lo(w_ref[...])
    o_ref[...] = _dot(s_hi, w_hi) + _dot(s_hi, w_lo) + _dot(s_lo, w_hi) + b_ref[...]


def _ada_modulation(cond, w_ada, b_ada):
    n6 = 6 * D_MODEL
    tn = D_MODEL
    return pl.pallas_call(
        _ada_kernel,
        out_shape=jax.ShapeDtypeStruct((DEPTH, 16, n6), F32),
        grid=(DEPTH, n6 // tn),
        in_specs=[
            pl.BlockSpec((16, D_MODEL), lambda l, j: (0, 0)),
            pl.BlockSpec((None, D_MODEL, tn), lambda l, j: (l, 0, j)),
            pl.BlockSpec((None, 1, tn), lambda l, j: (l, 0, j)),
        ],
        out_specs=pl.BlockSpec((None, 16, tn), lambda l, j: (l, 0, j)),
        compiler_params=_cparams(("arbitrary", "arbitrary")),
        name="ada_modulation",
    )(cond, w_ada, b_ada.reshape(DEPTH, 1, n6))


def _modulated(x, mod_ref, shift_row, scale_row):
    return x * (1.0 + mod_ref[scale_row:scale_row + 1, :]) + mod_ref[shift_row:shift_row + 1, :]


def _row_pair_specs(tm, width, col=0):
    npb = N_PROMPT // tm
    return [pl.BlockSpec((tm, width), lambda i, *_: (jnp.minimum(i, npb - 1), col)),
            pl.BlockSpec((tm, width), lambda i, *_: (jnp.maximum(i - npb, 0), col))]


def _pick_rows(p_ref, s_ref, tm):
    return jnp.where(pl.program_id(0) < N_PROMPT // tm, p_ref[...], s_ref[...])


def _proj_ab_kernel(xp_ref, xs_ref, mod_ref, w_ref, wkr_ref, main_ref, kr_ref):
    h = _modulated(_pick_rows(xp_ref, xs_ref, TM), mod_ref, 0, 1).astype(BF16)
    n = w_ref.shape[1]
    step = 512
    for j in range(n // step):
        main_ref[:, j * step:(j + 1) * step] = _dot(h, w_ref[:, j * step:(j + 1) * step])
    kr_ref[...] = _dot(h, wkr_ref[...])


def _proj_ab(xp, xs, mods, w_main, w_kr):
    n = w_main.shape[1]
    return pl.pallas_call(
        _proj_ab_kernel,
        out_shape=(jax.ShapeDtypeStruct((N_TOK, n), F32), jax.ShapeDtypeStruct((N_TOK, LANE), F32)),
        grid=(N_TOK // TM,),
        in_specs=_row_pair_specs(TM, D_MODEL) + [
            pl.BlockSpec((None, MOD_ROWS, D_MODEL), lambda i: (_cond_block(i, TM), 0, 0)),
            pl.BlockSpec((D_MODEL, n), lambda i: (0, 0)),
            pl.BlockSpec((D_MODEL, LANE), lambda i: (0, 0)),
        ],
        out_specs=(pl.BlockSpec((TM, n), lambda i: (i, 0)), pl.BlockSpec((TM, LANE), lambda i: (i, 0))),
        compiler_params=_cparams(("arbitrary",)),
        name="proj_ab",
    )(xp, xs, mods, w_main, w_kr)


def _rglru_kernel(xr_ref, gr_ref, cw_ref, cb_ref, wg_ref, bg_ref, lam_ref, h0_ref,
                  y_ref, hfin_ref, xpad, a_s, b_s, *, seq):
    pad = SUBLANE
    xpad[0:pad, :] = jnp.zeros((pad, LANE), F32)
    xpad[seq + pad:seq + 2 * pad, :] = jnp.zeros((pad, LANE), F32)
    xpad[pad:seq + pad, :] = xr_ref[...]

    sp = _softplus(-lam_ref[...])
    cw = cw_ref[...]
    cb = cb_ref[...]
    wg = wg_ref[...]
    bg = bg_ref[...]

    def gate_step(c, carry):
        t0 = pl.multiple_of(c * GATE_ROWS, GATE_ROWS)
        win = xpad[pl.ds(t0, GATE_ROWS + 2 * pad), :]
        xc = cb
        for j in range(CONV_W):
            off = pad - CONV_LEFT + j
            xc = xc + win[off:off + GATE_ROWS, :] * cw[j:j + 1, :]
        g = _dot(xc.astype(BF16), wg) + bg
        for d in range(2):
            r = jax.nn.sigmoid(g[:, (2 * d) * LANE:(2 * d + 1) * LANE])
            i = jax.nn.sigmoid(g[:, (2 * d + 1) * LANE:(2 * d + 2) * LANE])
            log_a = (-RG_C * r) * sp[d:d + 1, :]
            a = jnp.exp(log_a)
            t = jnp.tanh(log_a)
            bt = jnp.sqrt(2.0 * t / (t - 1.0)) * (i * xc)
            a_s[d, pl.ds(t0, GATE_ROWS), :] = a
            b_s[d, pl.ds(t0, GATE_ROWS), :] = bt
        return carry

    lax.fori_loop(0, seq // GATE_ROWS, gate_step, 0)

    row = lax.broadcasted_iota(I32, (SCAN_ROWS, LANE), 0) % SUBLANE
    n_steps = seq // SCAN_ROWS
    tiles = SCAN_ROWS // SUBLANE

    def local_scan(a, b, reverse):
        for k in (1, 2, 4):
            if reverse:
                ok = row < SUBLANE - k
                shift = SCAN_ROWS - k
            else:
                ok = row >= k
                shift = k
            a_sh = jnp.where(ok, pltpu.roll(a, shift, 0), 1.0)
            b_sh = jnp.where(ok, pltpu.roll(b, shift, 0), 0.0)
            b = a * b_sh + b
            a = a * a_sh
        return a, b

    def fwd_step(c, h):
        t0 = pl.multiple_of(c * SCAN_ROWS, SCAN_ROWS)
        a, b = local_scan(a_s[0, pl.ds(t0, SCAN_ROWS), :], b_s[0, pl.ds(t0, SCAN_ROWS), :], False)
        outs = []
        for j in range(tiles):
            hj = a[j * SUBLANE:(j + 1) * SUBLANE, :] * h + b[j * SUBLANE:(j + 1) * SUBLANE, :]
            outs.append(hj)
            h = hj[SUBLANE - 1:SUBLANE, :]
        y_ref[pl.ds(t0, SCAN_ROWS), :] = jnp.concatenate(outs, axis=0)
        return h

    h_f = lax.fori_loop(0, n_steps, fwd_step, h0_ref[0:1, :])

    def bwd_step(c, h):
        t0 = pl.multiple_of((n_steps - 1 - c) * SCAN_ROWS, SCAN_ROWS)
        a, b = local_scan(a_s[1, pl.ds(t0, SCAN_ROWS), :], b_s[1, pl.ds(t0, SCAN_ROWS), :], True)
        outs = [None] * tiles
        for j in reversed(range(tiles)):
            hj = a[j * SUBLANE:(j + 1) * SUBLANE, :] * h + b[j * SUBLANE:(j + 1) * SUBLANE, :]
            outs[j] = hj
            h = hj[0:1, :]
        hb = jnp.concatenate(outs, axis=0)
        y_ref[pl.ds(t0, SCAN_ROWS), :] = (y_ref[pl.ds(t0, SCAN_ROWS), :] + hb) * _gelu_tanh(gr_ref[pl.ds(t0, SCAN_ROWS), :])
        return h

    h_b = lax.fori_loop(0, n_steps, bwd_step, h0_ref[1:2, :])
    hfin_ref[0:1, :] = h_f
    hfin_ref[1:2, :] = h_b


def _rglru(main, cw, cb, wg, bg, lam, h0, *, n_seq, seq, row_block0):
    n_ct = RG_W // LANE
    gr_col0 = RG_W // LANE
    kern = functools.partial(_rglru_kernel, seq=seq)
    return pl.pallas_call(
        kern,
        out_shape=(jax.ShapeDtypeStruct((n_seq * seq, RG_W), F32), jax.ShapeDtypeStruct((n_seq, 2, RG_W), F32)),
        grid=(n_seq, n_ct),
        in_specs=[
            pl.BlockSpec((seq, LANE), lambda b, c: (row_block0 + b, c)),
            pl.BlockSpec((seq, LANE), lambda b, c: (row_block0 + b, gr_col0 + c)),
            pl.BlockSpec((CONV_W, LANE), lambda b, c: (0, c)),
            pl.BlockSpec((1, LANE), lambda b, c: (0, c)),
            pl.BlockSpec((None, LANE, 4 * LANE), lambda b, c: (c, 0, 0)),
            pl.BlockSpec((None, 1, 4 * LANE), lambda b, c: (c, 0, 0)),
            pl.BlockSpec((2, LANE), lambda b, c: (0, c)),
            pl.BlockSpec((None, 2, LANE), lambda b, c: (b, 0, c)),
        ],
        out_specs=(
            pl.BlockSpec((seq, LANE), lambda b, c: (b, c)),
            pl.BlockSpec((None, 2, LANE), lambda b, c: (b, 0, c)),
        ),
        scratch_shapes=[
            pltpu.VMEM((seq + 2 * SUBLANE, LANE), F32),
            pltpu.VMEM((2, seq, LANE), F32),
            pltpu.VMEM((2, seq, LANE), F32),
        ],
        compiler_params=_cparams(("arbitrary", "arbitrary")),
        name=f"rglru_s{seq}",
    )(main, main, cw, cb, wg, bg, lam, h0)


def _rope_lanes(x, cos, sin_m, sin_p):
    n = x.shape[1] // LANE
    half = QK_ROPE // 4
    cos_t = jnp.concatenate([cos] * n, axis=1) if n > 1 else cos
    sm_t = jnp.concatenate([sin_m] * n, axis=1) if n > 1 else sin_m
    sp_t = jnp.concatenate([sin_p] * n, axis=1) if n > 1 else sin_p
    up = pltpu.roll(x, x.shape[1] - half, 1)
    dn = pltpu.roll(x, half, 1)
    return x * cos_t + up * sm_t + dn * sp_t


def _mla_prep_kernel(cq0_ref, cq1_ref, cq2_ref, ckv_ref, kr_ref, cos_ref, sm_ref, sp_ref,
                     qn_ref, wq_ref, kvn_ref, wuk_ref, wuvt_ref,
                     q_ref, k_ref, vt_ref, ckvn_ref):
    cq = [cq0_ref[...], cq1_ref[...], cq2_ref[...]]
    ms = (jnp.sum(cq[0] * cq[0], axis=-1, keepdims=True) + jnp.sum(cq[1] * cq[1], axis=-1, keepdims=True)
          + jnp.sum(cq[2] * cq[2], axis=-1, keepdims=True)) * (1.0 / Q_LORA)
    inv = lax.rsqrt(ms + EPS)
    blk = Q_LORA // 3
    q = None
    for j in range(3):
        cqn = ((cq[j] * inv) * qn_ref[:, j * blk:(j + 1) * blk]).astype(BF16)
        part = _dot(cqn, wq_ref[j * blk:(j + 1) * blk, :])
        q = part if q is None else q + part
    cos, sm, sp = cos_ref[...], sm_ref[...], sp_ref[...]
    q_ref[...] = _rope_lanes(q, cos, sm, sp).T.astype(BF16)

    ckv = ckv_ref[...]
    inv_kv = lax.rsqrt(jnp.mean(ckv * ckv, axis=-1, keepdims=True) + EPS)
    ckvn = (ckv * inv_kv) * kvn_ref[...]

    @pl.when(pl.program_id(0) < N_PROMPT // TM)
    def _():
        ckvn_ref[...] = ckvn

    ckvn_b = ckvn.astype(BF16)
    kr_rot = _rope_lanes(kr_ref[...], cos, sm, sp)
    k_ref[...] = (_dot(ckvn_b, wuk_ref[...]) + jnp.concatenate([kr_rot] * MLA_HEADS, axis=1)).astype(BF16)
    vt = _dot_nt(wuvt_ref[...], ckvn_b).astype(BF16)
    for c in range(vt_ref.shape[0]):
        vt_ref[c] = vt[:, c * KC:(c + 1) * KC]


def _mla_prep(main, krp, cos_t, sm_t, sp_t, q_norm, wq_p, kv_norm, wuk_p, wuvt):
    cq_col0 = 2 * RG_W // 256
    hp = MLA_HEADS * HEAD_PAD
    full = lambda shape: pl.BlockSpec(shape, lambda i: (0,) * len(shape))
    tab = pl.BlockSpec((TM, LANE), lambda i: (_pos_block(i, TM), 0))
    return pl.pallas_call(
        _mla_prep_kernel,
        out_shape=(
            jax.ShapeDtypeStruct((hp, N_TOK), BF16),
            jax.ShapeDtypeStruct((N_TOK, hp), BF16),
            jax.ShapeDtypeStruct((N_TOK // KC, MLA_HEADS * V_HEAD, KC), BF16),
            jax.ShapeDtypeStruct((N_PROMPT, KV_LORA), F32),
        ),
        grid=(N_TOK // TM,),
        in_specs=[
            pl.BlockSpec((TM, 256), lambda i: (i, cq_col0)),
            pl.BlockSpec((TM, 256), lambda i: (i, cq_col0 + 1)),
            pl.BlockSpec((TM, 256), lambda i: (i, cq_col0 + 2)),
            pl.BlockSpec((TM, 256), lambda i: (i, cq_col0 + 3)),
            pl.BlockSpec((TM, LANE), lambda i: (i, 0)),
            tab, tab, tab,
            full((1, Q_LORA)), full((Q_LORA, hp)), full((1, KV_LORA)), full((KV_LORA, hp)),
            full((MLA_HEADS * V_HEAD, KV_LORA)),
        ],
        out_specs=(
            pl.BlockSpec((hp, TM), lambda i: (0, i)),
            pl.BlockSpec((TM, hp), lambda i: (i, 0)),
            pl.BlockSpec((TM // KC, MLA_HEADS * V_HEAD, KC), lambda i: (i, 0, 0)),
            pl.BlockSpec((TM, KV_LORA), lambda i: (jnp.minimum(i, N_PROMPT // TM - 1), 0)),
        ),
        compiler_params=_cparams(("arbitrary",)),
        name="mla_prep",
    )(main, main, main, main, krp, cos_t, sm_t, sp_t, q_norm, wq_p, kv_norm, wuk_p, wuvt)


def _mla_ctx_kernel(ckv_ref, kr_ref, wuk_ref, wuvt_ref, k_ref, vt_ref):
    ckv_b = ckv_ref[...].astype(BF16)
    k_ref[...] = (_dot(ckv_b, wuk_ref[...]) + jnp.concatenate([kr_ref[...]] * MLA_HEADS, axis=1)).astype(BF16)
    vt_ref[...] = _dot_nt(wuvt_ref[...], ckv_b).astype(BF16)


def _mla_ctx(ctx_ckv, ctx_krp, wuk_p, wuvt):
    n = ctx_ckv.shape[0]
    hp = MLA_HEADS * HEAD_PAD
    tm = KC
    full = lambda shape: pl.BlockSpec(shape, lambda i: (0,) * len(shape))
    return pl.pallas_call(
        _mla_ctx_kernel,
        out_shape=(jax.ShapeDtypeStruct((n, hp), BF16),
                   jax.ShapeDtypeStruct((n // tm, MLA_HEADS * V_HEAD, tm), BF16)),
        grid=(n // tm,),
        in_specs=[
            pl.BlockSpec((tm, KV_LORA), lambda i: (i, 0)),
            pl.BlockSpec((tm, LANE), lambda i: (i, 0)),
            full((KV_LORA, hp)), full((MLA_HEADS * V_HEAD, KV_LORA)),
        ],
        out_specs=(pl.BlockSpec((tm, hp), lambda i: (i, 0)),
                   pl.BlockSpec((None, MLA_HEADS * V_HEAD, tm), lambda i: (i, 0, 0))),
        compiler_params=_cparams(("arbitrary",)),
        name="mla_ctx",
    )(ctx_ckv, ctx_krp, wuk_p, wuvt)


def _attn_kernel(*refs, seq, tq, n_ctx):
    if n_ctx:
        q_ref, k_ref, vt_ref, kc_ref, vtc_ref, o_ref, s_scr, p_scr, k_all, vt_all = refs
    else:
        q_ref, k_ref, vt_ref, o_ref, s_scr, p_scr, k_all, vt_all = refs
    has_ctx = 1 if n_ctx else 0
    n_own = seq // KC
    n = n_own + has_ctx
    c_exp = ATTN_SCALE * math.log2(math.e)
    qs = [q_ref[h * HEAD_PAD:(h + 1) * HEAD_PAD, :] for h in range(2)]

    @pl.when(pl.program_id(2) == 0)
    def _():
        k_all[0:seq, :] = k_ref[...]
        vt_all[0:n_own] = vt_ref[...]
        if has_ctx:
            k_all[seq:seq + KC, :] = kc_ref[...]
            vt_all[n_own] = vtc_ref[...]

    def k_chunk(c, h):
        t0 = c * KC if isinstance(c, int) else pl.multiple_of(c * KC, KC)
        return k_all[pl.ds(t0, KC), h * HEAD_PAD:(h + 1) * HEAD_PAD]

    def v_chunk(c, h):
        return vt_all[c, h * V_HEAD:(h + 1) * V_HEAD, :]

    def scores(c, slot):
        for h in range(2):
            s_scr[slot, h] = _dot(k_chunk(c, h), qs[h])

    def softmax_chunk(slot, st):
        out = []
        for h in range(2):
            m, l, _, acc = st[h]
            t = s_scr[slot, h] * c_exp
            m_new = jnp.maximum(m, jnp.max(t, axis=0, keepdims=True))
            alpha = jnp.exp2(m - m_new)
            p = jnp.exp2(t - m_new)
            p_scr[slot, h] = p.astype(BF16)
            out.append((m_new, alpha * l + jnp.sum(p, axis=0, keepdims=True), alpha, acc))
        return out

    def weighted_values(c, slot, st, alphas):
        return [(st[h][0], st[h][1], st[h][2], alphas[h] * st[h][3] + _dot(v_chunk(c, h), p_scr[slot, h]))
                for h in range(2)]

    def step(c, slot, st, with_s, with_v):
        if with_s:
            scores(c + 1, 1 - slot)
        alphas = [st[h][2] for h in range(2)]
        st = softmax_chunk(slot, st)
        if with_v:
            prev = max(c - 1, 0) if isinstance(c, int) else jnp.maximum(c - 1, 0)
            st = weighted_values(prev, 1 - slot, st, alphas)
        return st

    p_scr[1] = jnp.zeros(p_scr.shape[1:], BF16)
    st = [(jnp.full((1, tq), NEG, F32), jnp.zeros((1, tq), F32), jnp.ones((1, tq), F32),
           jnp.zeros((V_HEAD, tq), F32)) for _ in range(2)]
    scores(0, 0)
    n_pairs = (n - 1) // 2

    def pair(j, flat):
        st = [tuple(flat[0:4]), tuple(flat[4:8])]
        st = step(2 * j, 0, st, True, True)
        st = step(2 * j + 1, 1, st, True, True)
        return tuple(st[0]) + tuple(st[1])

    if n_pairs:
        flat = lax.fori_loop(0, n_pairs, pair, tuple(st[0]) + tuple(st[1]))
        st = [tuple(flat[0:4]), tuple(flat[4:8])]
    for c in range(2 * n_pairs, n):
        st = step(c, c % 2, st, c + 1 < n, c > 0)
    st = weighted_values(n - 1, (n - 1) % 2, st, [st[h][2] for h in range(2)])
    for h in range(2):
        o_ref[:, h * V_HEAD:(h + 1) * V_HEAD] = (st[h][3] / st[h][1]).T.astype(o_ref.dtype)


def _attention(q, k, vt, kc, vtc, *, n_seq, seq, row_block0, tq):
    n_ctx = 0 if kc is None else PAST_LEN
    n_hp = MLA_HEADS // 2
    nq = seq // tq
    kern = functools.partial(_attn_kernel, seq=seq, tq=tq, n_ctx=n_ctx)
    in_specs = [
        pl.BlockSpec((2 * HEAD_PAD, tq), lambda b, j, i: (j, (row_block0 + b) * nq + i)),
        pl.BlockSpec((seq, 2 * HEAD_PAD), lambda b, j, i: (row_block0 + b, j)),
        pl.BlockSpec((seq // KC, 2 * V_HEAD, KC), lambda b, j, i: (row_block0 + b, j, 0)),
    ]
    args = [q, k, vt]
    if n_ctx:
        in_specs += [
            pl.BlockSpec((n_ctx, 2 * HEAD_PAD), lambda b, j, i: (b, j)),
            pl.BlockSpec((None, 2 * V_HEAD, KC), lambda b, j, i: (b, j, 0)),
        ]
        args += [kc, vtc]
    return pl.pallas_call(
        kern,
        out_shape=jax.ShapeDtypeStruct((n_seq * seq, MLA_HEADS * V_HEAD), BF16),
        grid=(n_seq, n_hp, nq),
        in_specs=in_specs,
        out_specs=pl.BlockSpec((tq, 2 * V_HEAD), lambda b, j, i: (b * nq + i, j)),
        scratch_shapes=[pltpu.VMEM((2, 2, KC, tq), F32), pltpu.VMEM((2, 2, KC, tq), BF16),
                        pltpu.VMEM((seq + n_ctx, 2 * HEAD_PAD), BF16),
                        pltpu.VMEM(((seq + n_ctx) // KC, 2 * V_HEAD, KC), BF16)],
        compiler_params=_cparams(("arbitrary", "arbitrary", "arbitrary")),
        name=f"mla_attention_s{seq}",
    )(*args)


def _post_mixer(y, x, mod_ref, lng_ref, lnb_ref, wrh_ref, wrl_ref, x1_ref, h2_ref, lgt_ref):
    z = ALPHA * x + mod_ref[2:3, :] * y
    x1 = _layernorm_rows(z, lng_ref[...], lnb_ref[...])
    x1_ref[...] = x1
    h2 = _modulated(x1, mod_ref, 3, 4)
    h_hi, h_lo = _split_hi_lo(h2)
    h2_ref[...] = h_hi
    w_hi, w_lo = wrh_ref[...], wrl_ref[...]
    lgt_ref[...] = _dot_nt(w_hi, h_hi) + _dot_nt(w_hi, h_lo) + _dot_nt(w_lo, h_hi)


def _out_ab_kernel(yrgp_ref, yrgs_ref, op_ref, os_ref, wa_ref, wb_ref, xp_ref, xs_ref,
                   mod_ref, lng_ref, lnb_ref, wrh_ref, wrl_ref, x1_ref, h2_ref, lgt_ref):
    y = (_dot(_pick_rows(yrgp_ref, yrgs_ref, TM).astype(BF16), wa_ref[...])
         + _dot(_pick_rows(op_ref, os_ref, TM), wb_ref[...]))
    _post_mixer(y, _pick_rows(xp_ref, xs_ref, TM), mod_ref, lng_ref, lnb_ref, wrh_ref, wrl_ref,
                x1_ref, h2_ref, lgt_ref)


def _post_specs():
    full = lambda shape: pl.BlockSpec(shape, lambda i: (0,) * len(shape))
    in_specs = [
        pl.BlockSpec((None, MOD_ROWS, D_MODEL), lambda i: (_cond_block(i, TM), 0, 0)),
        full((1, D_MODEL)), full((1, D_MODEL)),
        full((N_EXPERTS, D_MODEL)), full((N_EXPERTS, D_MODEL)),
    ]
    out_shape = (
        jax.ShapeDtypeStruct((N_TOK, D_MODEL), F32),
        jax.ShapeDtypeStruct((N_TOK, D_MODEL), BF16),
        jax.ShapeDtypeStruct((N_EXPERTS, N_TOK), F32),
    )
    out_specs = (
        pl.BlockSpec((TM, D_MODEL), lambda i: (i, 0)),
        pl.BlockSpec((TM, D_MODEL), lambda i: (i, 0)),
        pl.BlockSpec((N_EXPERTS, TM), lambda i: (0, i)),
    )
    return in_specs, out_shape, out_specs


def _out_ab(yrg_p, yrg_s, o_p, o_s, w_a, w_b, xp, xs, mods, lng, lnb, wr_hi, wr_lo):
    full = lambda shape: pl.BlockSpec(shape, lambda i: (0,) * len(shape))
    post_in, out_shape, out_specs = _post_specs()
    return pl.pallas_call(
        _out_ab_kernel,
        out_shape=out_shape,
        grid=(N_TOK // TM,),
        in_specs=(_row_pair_specs(TM, RG_W) + _row_pair_specs(TM, MLA_HEADS * V_HEAD)
                  + [full((RG_W, D_MODEL)), full((MLA_HEADS * V_HEAD, D_MODEL))]
                  + _row_pair_specs(TM, D_MODEL) + post_in),
        out_specs=out_specs,
        compiler_params=_cparams(("arbitrary",)),
        name="out_ab",
    )(yrg_p, yrg_s, o_p, o_s, w_a, w_b, xp, xs, mods, lng, lnb, wr_hi, wr_lo)


def _proj_c_kernel(x_ref, mod_ref, cos_ref, sin_ref, wq_ref, wk_ref, wv_ref, wg_ref, q_ref, k_ref, v_ref, g_ref):
    h = _modulated(x_ref[...], mod_ref, 0, 1).astype(BF16)
    cos, sin = cos_ref[...], sin_ref[...]
    half = RET_DK // 2
    for hd in range(RET_HEADS):
        for w_ref, is_k in ((wq_ref, False), (wk_ref, True)):
            p = _dot(h, w_ref[:, hd * RET_DK:(hd + 1) * RET_DK])
            x1, x2 = p[:, :half], p[:, half:]
            r1 = x1 * cos - x2 * sin
            r2 = x2 * cos + x1 * sin
            if not is_k:
                q_ref[:, hd * RET_DK:hd * RET_DK + half] = r1.astype(BF16)
                q_ref[:, hd * RET_DK + half:(hd + 1) * RET_DK] = r2.astype(BF16)
            else:
                t1 = (r1 * RET_DK ** -0.5).T.astype(BF16)
                t2 = (r2 * RET_DK ** -0.5).T.astype(BF16)
                for c in range(k_ref.shape[0]):
                    k_ref[c, hd * RET_DK:hd * RET_DK + half, :] = t1[:, c * RET_CHUNK:(c + 1) * RET_CHUNK]
                    k_ref[c, hd * RET_DK + half:(hd + 1) * RET_DK, :] = t2[:, c * RET_CHUNK:(c + 1) * RET_CHUNK]
    step = 512
    for j in range(MIX_C // step):
        v_ref[:, j * step:(j + 1) * step] = _dot(h, wv_ref[:, j * step:(j + 1) * step]).astype(BF16)
        g_ref[:, j * step:(j + 1) * step] = _dot(h, wg_ref[:, j * step:(j + 1) * step])


def _proj_c(x, mods, cos_t, sin_t, wq, wk, wv, wg):
    full = lambda shape: pl.BlockSpec(shape, lambda i: (0,) * len(shape))
    qk = RET_HEADS * RET_DK
    tab = pl.BlockSpec((TM, RET_DK // 2), lambda i: (_pos_block(i, TM), 0))
    return pl.pallas_call(
        _proj_c_kernel,
        out_shape=(
            jax.ShapeDtypeStruct((N_TOK, qk), BF16), jax.ShapeDtypeStruct((N_TOK // RET_CHUNK, qk, RET_CHUNK), BF16),
            jax.ShapeDtypeStruct((N_TOK, MIX_C), BF16), jax.ShapeDtypeStruct((N_TOK, MIX_C), F32),
        ),
        grid=(N_TOK // TM,),
        in_specs=[
            pl.BlockSpec((TM, D_MODEL), lambda i: (i, 0)),
            pl.BlockSpec((None, MOD_ROWS, D_MODEL), lambda i: (_cond_block(i, TM), 0, 0)),
            tab, tab,
            full((D_MODEL, qk)), full((D_MODEL, qk)), full((D_MODEL, MIX_C)), full((D_MODEL, MIX_C)),
        ],
        out_specs=(
            pl.BlockSpec((TM, qk), lambda i: (i, 0)),
            pl.BlockSpec((TM // RET_CHUNK, qk, RET_CHUNK), lambda i: (i, 0, 0)),
            pl.BlockSpec((TM, MIX_C), lambda i: (i, 0)), pl.BlockSpec((TM, MIX_C), lambda i: (i, 0)),
        ),
        compiler_params=_cparams(("arbitrary",), 56),
        name="proj_c",
    )(x, mods, cos_t, sin_t, wq, wk, wv, wg)


def _retention_kernel(*refs, seq, with_state):
    if with_state:
        q_ref, kt_ref, v_ref, gam_ref, r0_ref, o_ref, rfin_ref, r_s = refs
    else:
        q_ref, kt_ref, v_ref, gam_ref, r0_ref, o_ref, r_s = refs
        rfin_ref = None
    c = RET_CHUNK
    n = seq // c
    ii = lax.broadcasted_iota(I32, (c, c), 0).astype(F32)
    jj = lax.broadcasted_iota(I32, (c, c), 1).astype(F32)
    ci = lax.broadcasted_iota(I32, (c, 1), 0).astype(F32)
    li = lax.broadcasted_iota(I32, (1, c), 1).astype(F32)

    consts = []
    for d in range(2):
        gam = gam_ref[d]
        lg_row = -_softplus(-gam[0:1, :])
        lg = jnp.broadcast_to(lg_row, (c, c))
        lg_col = jnp.broadcast_to(lg_row[:, 0:1], (c, 1))
        if d == 0:
            diff = ii - jj
            xi = jnp.exp((ci + 1.0) * lg_col)
            zeta = jnp.exp((c - 1.0 - li) * lg_row)
        else:
            diff = jj - ii
            xi = jnp.exp((c - ci) * lg_col)
            zeta = jnp.exp(li * lg_row)
        dmat = jnp.where(diff >= 0, jnp.exp(jnp.maximum(diff, 0.0) * lg), 0.0)
        g_chunk = jnp.exp(float(c) * lg_row[:, 0:1])
        consts.append((dmat, xi, zeta, g_chunk))
        r_s[d] = r0_ref[d]

    def chunk(d, idx, accumulate):
        dmat, xi, zeta, g_chunk = consts[d]
        t0 = pl.multiple_of(idx * c, c)
        qb = q_ref[pl.ds(t0, c), :]
        kt = kt_ref[idx]
        vb = v_ref[pl.ds(t0, c), :]
        r = r_s[d]
        inner = _dot(qb, kt) * dmat
        o = _dot(inner.astype(BF16), vb) + _dot((qb.astype(F32) * xi).astype(BF16), r.astype(BF16))
        r_s[d] = r * g_chunk + _dot((kt.astype(F32) * zeta).astype(BF16), vb)
        if accumulate:
            o_ref[pl.ds(t0, c), :] = o_ref[pl.ds(t0, c), :] + o
        else:
            o_ref[pl.ds(t0, c), :] = o

    def first_half(s, carry):
        chunk(0, s, False)
        chunk(1, n - 1 - s, False)
        return carry

    def second_half(s, carry):
        chunk(0, s, True)
        chunk(1, n - 1 - s, True)
        return carry

    lax.fori_loop(0, n // 2, first_half, 0)
    lax.fori_loop(n // 2, n, second_half, 0)
    if with_state:
        for d in range(2):
            rfin_ref[d] = r_s[d]


def _retention(q, k, v, gam, r0, *, n_seq, seq, row_block0, with_state):
    kern = functools.partial(_retention_kernel, seq=seq, with_state=with_state)
    out_shape = [jax.ShapeDtypeStruct((n_seq * seq, MIX_C), F32)]
    out_specs = [pl.BlockSpec((seq, RET_DV), lambda b, h: (b, h))]
    if with_state:
        out_shape.append(jax.ShapeDtypeStruct((n_seq, 2, RET_HEADS, RET_DK, RET_DV), F32))
        out_specs.append(pl.BlockSpec((None, 2, None, RET_DK, RET_DV), lambda b, h: (b, 0, h, 0, 0)))
    return pl.pallas_call(
        kern,
        out_shape=tuple(out_shape),
        grid=(n_seq, RET_HEADS),
        in_specs=[
            pl.BlockSpec((seq, RET_DK), lambda b, h: (row_block0 + b, h)),
            pl.BlockSpec((seq // RET_CHUNK, RET_DK, RET_CHUNK), lambda b, h: (row_block0 + b, h, 0)),
            pl.BlockSpec((seq, RET_DV), lambda b, h: (row_block0 + b, h)),
            pl.BlockSpec((2, None, SUBLANE, LANE), lambda b, h: (0, h, 0, 0)),
            pl.BlockSpec((None, 2, None, RET_DK, RET_DV), lambda b, h: (b, 0, h, 0, 0)),
        ],
        out_specs=tuple(out_specs),
        scratch_shapes=[pltpu.VMEM((2, RET_DK, RET_DV), F32)],
        compiler_params=_cparams(("arbitrary", "arbitrary"), 56),
        name=f"retention_s{seq}",
    )(q, k, v, gam, r0)


def _out_c_kernel(op_ref, os_ref, g_ref, w_ref, x_ref, mod_ref, lng_ref, lnb_ref, wrh_ref, wrl_ref,
                  x1_ref, h2_ref, lgt_ref):
    y = None
    o_all = _pick_rows(op_ref, os_ref, TM)
    for hd in range(RET_HEADS):
        o = o_all[:, hd * RET_DV:(hd + 1) * RET_DV]
        mu = jnp.mean(o, axis=-1, keepdims=True)
        oc = o - mu
        var = jnp.mean(oc * oc, axis=-1, keepdims=True)
        on = oc * lax.rsqrt(var + EPS)
        a = (on * _silu(g_ref[:, hd * RET_DV:(hd + 1) * RET_DV])).astype(BF16)
        part = _dot(a, w_ref[hd * RET_DV:(hd + 1) * RET_DV, :])
        y = part if y is None else y + part
    _post_mixer(y, x_ref[...], mod_ref, lng_ref, lnb_ref, wrh_ref, wrl_ref, x1_ref, h2_ref, lgt_ref)


def _out_c(o_p, o_s, g, w, x, mods, lng, lnb, wr_hi, wr_lo):
    full = lambda shape: pl.BlockSpec(shape, lambda i: (0,) * len(shape))
    post_in, out_shape, out_specs = _post_specs()
    return pl.pallas_call(
        _out_c_kernel,
        out_shape=out_shape,
        grid=(N_TOK // TM,),
        in_specs=_row_pair_specs(TM, MIX_C) + [
            pl.BlockSpec((TM, MIX_C), lambda i: (i, 0)),
            full((MIX_C, D_MODEL)),
            pl.BlockSpec((TM, D_MODEL), lambda i: (i, 0)),
        ] + post_in,
        out_specs=out_specs,
        compiler_params=_cparams(("arbitrary",), 56),
        name="out_c",
    )(o_p, o_s, g, w, x, mods, lng, lnb, wr_hi, wr_lo)


def _route_kernel(lgt_ref, bias_ref, tri_ref, w_ref, lpos_ref, p16_ref):
    tt = lgt_ref.shape[1]
    scores = jax.nn.sigmoid(lgt_ref[...])
    sel = scores + bias_ref[...]
    srow = lax.broadcasted_iota(I32, (GROUP_SIZE, tt), 0).astype(F32)
    ninf = -jnp.inf

    gs = []
    for g in range(N_GROUPS):
        sg = sel[g * GROUP_SIZE:(g + 1) * GROUP_SIZE, :]
        m1 = jnp.max(sg, axis=0, keepdims=True)
        i1 = jnp.min(jnp.where(sg == m1, srow, float(GROUP_SIZE)), axis=0, keepdims=True)
        m2 = jnp.max(jnp.where(srow == i1, ninf, sg), axis=0, keepdims=True)
        gs.append(m1 + m2)
    gs = jnp.concatenate(gs, axis=0)
    chosen = jnp.zeros((N_GROUPS, tt), F32)
    for _ in range(TOPK_GROUPS):
        mg = jnp.max(gs, axis=0, keepdims=True)
        gi = jnp.min(jnp.where(gs == mg, srow, float(N_GROUPS)), axis=0, keepdims=True)
        hit = srow == gi
        chosen = jnp.where(hit, 1.0, chosen)
        gs = jnp.where(hit, ninf, gs)
    sel = jnp.concatenate(
        [jnp.where(jnp.broadcast_to(chosen[g:g + 1, :], (GROUP_SIZE, tt)) > 0.5,
                   sel[g * GROUP_SIZE:(g + 1) * GROUP_SIZE, :], ninf) for g in range(N_GROUPS)], axis=0)

    erow = lax.broadcasted_iota(I32, (N_EXPERTS, tt), 0).astype(F32)
    ids, ws = [], []
    for _ in range(TOP_K):
        m = jnp.max(sel, axis=0, keepdims=True)
        ei = jnp.min(jnp.where(sel == m, erow, float(N_EXPERTS)), axis=0, keepdims=True)
        hit = erow == ei
        ids.append(ei)
        ws.append(jnp.sum(jnp.where(hit, scores, 0.0), axis=0, keepdims=True))
        sel = jnp.where(hit, ninf, sel)
    wsum = ws[0]
    for k in range(1, TOP_K):
        wsum = wsum + ws[k]
    w_ref[...] = jnp.concatenate([w / wsum * ROUTED_SCALE for w in ws], axis=0)

    member_f = jnp.zeros((N_EXPERTS, tt), F32)
    for k in range(TOP_K):
        member_f = jnp.where(erow == ids[k], 1.0, member_f)
    member_b = member_f.astype(BF16)
    cnt_row = _dot_nt(jnp.ones((SUBLANE, tt), BF16), member_b)[0:1, :]
    p16_row = jnp.ceil(cnt_row * (1.0 / PIECE)) * PIECE
    lane_e = lax.broadcasted_iota(I32, (N_EXPERTS, N_EXPERTS), 1)
    sub_e = lax.broadcasted_iota(I32, (N_EXPERTS, N_EXPERTS), 0)
    run_start = jnp.sum(jnp.where(lane_e < sub_e, jnp.broadcast_to(p16_row, (N_EXPERTS, N_EXPERTS)), 0.0),
                        axis=1, keepdims=True)
    rank = _dot(member_b, tri_ref[...]) + run_start
    lpos_ref[...] = jnp.concatenate(
        [jnp.sum(jnp.where(erow == ids[k], rank, 0.0), axis=0, keepdims=True) for k in range(TOP_K)],
        axis=0).astype(I32)
    p16_ref[pl.ds(pl.program_id(0), 1), :] = jnp.concatenate(
        [p16_row, jnp.zeros((1, LANE - N_EXPERTS), F32)], axis=1)


def _route(lgt, bias, tri):
    return pl.pallas_call(
        _route_kernel,
        out_shape=(
            jax.ShapeDtypeStruct((TOP_K, N_TOK), F32), jax.ShapeDtypeStruct((TOP_K, N_TOK), I32),
            jax.ShapeDtypeStruct((N_WIN, LANE), F32),
        ),
        grid=(N_WIN,),
        in_specs=[
            pl.BlockSpec((N_EXPERTS, WIN), lambda i: (0, i)),
            pl.BlockSpec((N_EXPERTS, 1), lambda i: (0, 0)),
            pl.BlockSpec((WIN, WIN), lambda i: (0, 0)),
        ],
        out_specs=(
            pl.BlockSpec((TOP_K, WIN), lambda i: (0, i)), pl.BlockSpec((TOP_K, WIN), lambda i: (0, i)),
            pl.BlockSpec((N_WIN, LANE), lambda i: (0, 0)),
        ),
        compiler_params=_cparams(("arbitrary",)),
        name="moe_route",
    )(lgt, bias, tri)


def _sort_kernel(rw_ref, lpos_ref, h_ref, xl_ref):
    n_tiles = (rw_ref[pl.program_id(0)] + SORT_ROWS - 1) // SORT_ROWS
    x = h_ref[...]
    lp = lpos_ref[...]
    riota = lax.broadcasted_iota(I32, (SORT_ROWS, WIN), 0).astype(jnp.int16)
    one = jnp.ones((SORT_ROWS, WIN), BF16)

    def tile(j, carry):
        rel = (lp - j * SORT_ROWS).astype(jnp.int16)
        oh = jnp.zeros((SORT_ROWS, WIN), BF16)
        for k in range(TOP_K):
            oh = jnp.where(rel[k:k + 1, :] == riota, one, oh)
        xl_ref[pl.ds(pl.multiple_of(j * SORT_ROWS, SORT_ROWS), SORT_ROWS), :] = _dot(oh, x).astype(BF16)
        return carry

    lax.fori_loop(0, n_tiles, tile, 0)

    def clear(j, carry):
        xl_ref[pl.ds(pl.multiple_of(j * SORT_ROWS, SORT_ROWS), SORT_ROWS), :] = jnp.zeros((SORT_ROWS, D_MODEL), BF16)
        return carry

    lax.fori_loop(n_tiles, RL // SORT_ROWS, clear, 0)


def _sort_rows(rw, lpos, h2):
    last = N_WIN - 1
    return pl.pallas_call(
        _sort_kernel,
        out_shape=jax.ShapeDtypeStruct(((N_WIN + 1) * RL, D_MODEL), BF16),
        grid_spec=pltpu.PrefetchScalarGridSpec(
            num_scalar_prefetch=1,
            grid=(N_WIN + 1,),
            in_specs=[
                pl.BlockSpec((TOP_K, WIN), lambda i, rw: (0, jnp.minimum(i, last))),
                pl.BlockSpec((WIN, D_MODEL), lambda i, rw: (jnp.minimum(i, last), 0)),
            ],
            out_specs=pl.BlockSpec((RL, D_MODEL), lambda i, rw: (i, 0)),
        ),
        compiler_params=_cparams(("arbitrary",)),
        name="moe_sort",
    )(rw, lpos, h2)


def _moe_tables(p16):
    n_w = jnp.arange(N_WIN, dtype=I32)
    run_start = jnp.cumsum(p16, axis=1) - p16
    rw = jnp.sum(p16, axis=1)
    cum_w = jnp.cumsum(p16, axis=0) - p16
    tot = jnp.sum(p16, axis=0)
    nblk = (tot + TME - 1) // TME
    blk_end = jnp.cumsum(nblk)
    blk0 = blk_end - nblk
    n_used = blk_end[-1]
    b = jnp.arange(N_EBLOCKS + 1, dtype=I32)
    block_e = jnp.minimum(jnp.sum((blk_end[None, :] <= b[:, None]).astype(I32), axis=1), N_EXPERTS - 1)
    piece = jnp.arange(PIECES, dtype=I32) * PIECE
    rp = (b - blk0[block_e])[:, None] * TME + piece[None, :]
    valid = (b[:, None] < n_used) & (rp < tot[block_e][:, None])
    cum_e = cum_w.T[block_e]
    len_e = p16.T[block_e]
    start_e = run_start.T[block_e]
    in_win = (cum_e[:, None, :] <= rp[:, :, None]) & (rp[:, :, None] < (cum_e + len_e)[:, None, :])
    row = n_w[None, None, :] * RL + start_e[:, None, :] + rp[:, :, None] - cum_e[:, None, :]
    row = jnp.sum(jnp.where(in_win, row, 0), axis=2)
    zero_src = N_WIN * RL
    trash = N_WIN * RL + (2 + b % 2)[:, None] * TME + piece[None, :]
    gather_row = jnp.where(valid, row, zero_src).reshape(-1).astype(I32)
    scatter_row = jnp.where(valid, row, trash).reshape(-1).astype(I32)
    rw = jnp.concatenate([rw, jnp.zeros((1,), rw.dtype)])
    return rw.astype(I32), block_e.astype(I32), n_used.astype(I32).reshape(1), gather_row, scatter_row


def _expert_kernel(be_ref, nb_ref, grow_ref, srow_ref, xl_hbm, wg_ref, wu_ref, wd_ref, yl_hbm,
                   xbuf, ybuf, wg_b, wu_b, wd_b, gsem, ssem):
    b = pl.program_id(0)
    nb = nb_ref[0]

    @pl.when(jnp.logical_and(b < nb, jnp.logical_or(b == 0, be_ref[b] != be_ref[jnp.maximum(b - 1, 0)])))
    def _():
        wg_b[...] = wg_ref[...].astype(BF16)
        wu_b[...] = wu_ref[...].astype(BF16)
        wd_b[...] = wd_ref[...].astype(BF16)

    def gather_start(blk, slot):
        for p in range(PIECES):
            src = pl.multiple_of(grow_ref[blk * PIECES + p], PIECE)
            pltpu.make_async_copy(xl_hbm.at[pl.ds(src, PIECE)], xbuf.at[slot, pl.ds(p * PIECE, PIECE)],
                                  gsem.at[slot]).start()

    def gather_wait(slot):
        pltpu.make_async_copy(xl_hbm.at[pl.ds(0, TME)], xbuf.at[slot], gsem.at[slot]).wait()

    def scatter_wait(slot):
        pltpu.make_async_copy(ybuf.at[slot], yl_hbm.at[pl.ds(0, TME)], ssem.at[slot]).wait()

    @pl.when(b < nb)
    def _():
        slot = b % 2

        @pl.when(b == 0)
        def _():
            gather_start(0, 0)

        @pl.when(b + 1 < nb)
        def _():
            gather_start(b + 1, 1 - slot)

        gather_wait(slot)

        @pl.when(b >= 2)
        def _():
            scatter_wait(slot)

        x = xbuf[slot]
        hb = _silu(_dot(x, wg_b[...])) * _dot(x, wu_b[...])
        ybuf[slot] = _dot(hb.astype(BF16), wd_b[...]).astype(BF16)
        for p in range(PIECES):
            dst = pl.multiple_of(srow_ref[b * PIECES + p], PIECE)
            pltpu.make_async_copy(ybuf.at[slot, pl.ds(p * PIECE, PIECE)], yl_hbm.at[pl.ds(dst, PIECE)],
                                  ssem.at[slot]).start()

        @pl.when(b == nb - 1)
        def _():
            scatter_wait(slot)

            @pl.when(b >= 1)
            def _():
                scatter_wait(1 - slot)


def _experts(block_e, n_used, gather_row, scatter_row, xl, wg, wu, wd, layer):
    def w_map(i, be, nb, gr, sr):
        return (layer, be[jnp.minimum(i, nb[0] - 1)], 0, 0)

    return pl.pallas_call(
        _expert_kernel,
        out_shape=jax.ShapeDtypeStruct(((N_WIN + 1) * RL, D_MODEL), BF16),
        input_output_aliases={4: 0},
        grid_spec=pltpu.PrefetchScalarGridSpec(
            num_scalar_prefetch=4,
            grid=(N_EBLOCKS,),
            in_specs=[
                pl.BlockSpec(memory_space=pl.ANY),
                pl.BlockSpec((None, None, D_MODEL, D_EXPERT), w_map),
                pl.BlockSpec((None, None, D_MODEL, D_EXPERT), w_map),
                pl.BlockSpec((None, None, D_EXPERT, D_MODEL), w_map),
            ],
            out_specs=pl.BlockSpec(memory_space=pl.ANY),
            scratch_shapes=[
                pltpu.VMEM((2, TME, D_MODEL), BF16), pltpu.VMEM((2, TME, D_MODEL), BF16),
                pltpu.VMEM((D_MODEL, D_EXPERT), BF16), pltpu.VMEM((D_MODEL, D_EXPERT), BF16),
                pltpu.VMEM((D_EXPERT, D_MODEL), BF16),
                pltpu.SemaphoreType.DMA((2,)), pltpu.SemaphoreType.DMA((2,)),
            ],
        ),
        compiler_params=_cparams(("arbitrary",)),
        name="moe_experts",
    )(block_e, n_used, gather_row, scatter_row, xl, wg, wu, wd)


def _combine_kernel(rw_ref, yl_ref, lpt_ref, wt_ref, h_ref, x1_ref, mod_ref, lng_ref, lnb_ref,
                    wsg_ref, wsu_ref, wsd_ref, *rest, split):
    del rw_ref
    if split:
        outp_ref, outs_ref, p_w, lp_b, wt_b = rest
    else:
        out_ref, p_w, lp_b, wt_b = rest
    hb = h_ref[...]
    shared = _dot((_silu(_dot(hb, wsg_ref[...])) * _dot(hb, wsu_ref[...])).astype(BF16), wsd_ref[...])
    lp = lpt_ref[...]
    wt = wt_ref[...]
    for k in range(TOP_K):
        lp_b[k] = jnp.broadcast_to(lp[:, k:k + 1], (WIN, SORT_ROWS)).astype(jnp.int16)
        wt_b[k] = jnp.broadcast_to(wt[:, k:k + 1], (WIN, SORT_ROWS)).astype(BF16)
    ciota = lax.broadcasted_iota(I32, (WIN, SORT_ROWS), 1)
    for j in range(RL // SORT_ROWS):
        col = (ciota + j * SORT_ROWS).astype(jnp.int16)
        pm = jnp.zeros((WIN, SORT_ROWS), BF16)
        for k in range(TOP_K):
            pm = jnp.where(lp_b[k] == col, wt_b[k], pm)
        p_w[:, j * SORT_ROWS:(j + 1) * SORT_ROWS] = pm
    routed = _dot(p_w[...], yl_ref[...])
    z = ALPHA * x1_ref[...] + mod_ref[5:6, :] * (routed + shared)
    out = _layernorm_rows(z, lng_ref[...], lnb_ref[...])
    if split:
        @pl.when(pl.program_id(0) < N_PROMPT // WIN)
        def _():
            outp_ref[...] = out

        @pl.when(pl.program_id(0) >= N_PROMPT // WIN)
        def _():
            outs_ref[...] = out
    else:
        out_ref[...] = out


def _combine(rw, yl, lpos_t, wt, h2, x1, mods, lng, lnb, wsg, wsu, wsd, *, split):
    full = lambda shape: pl.BlockSpec(shape, lambda i, rw: (0,) * len(shape))
    if split:
        out_shape = (jax.ShapeDtypeStruct((N_PROMPT, D_MODEL), F32), jax.ShapeDtypeStruct((N_SAMPLE, D_MODEL), F32))
        out_specs = tuple(_row_pair_specs(WIN, D_MODEL))
    else:
        out_shape = jax.ShapeDtypeStruct((N_TOK, D_MODEL), F32)
        out_specs = pl.BlockSpec((WIN, D_MODEL), lambda i, rw: (i, 0))
    return pl.pallas_call(
        functools.partial(_combine_kernel, split=split),
        out_shape=out_shape,
        grid_spec=pltpu.PrefetchScalarGridSpec(
            num_scalar_prefetch=1,
            grid=(N_WIN,),
            in_specs=[
                pl.BlockSpec((RL, D_MODEL), lambda i, rw: (i, 0)),
                pl.BlockSpec((WIN, TOP_K), lambda i, rw: (i, 0)),
                pl.BlockSpec((WIN, TOP_K), lambda i, rw: (i, 0)),
                pl.BlockSpec((WIN, D_MODEL), lambda i, rw: (i, 0)),
                pl.BlockSpec((WIN, D_MODEL), lambda i, rw: (i, 0)),
                pl.BlockSpec((None, MOD_ROWS, D_MODEL), lambda i, rw: (_cond_block(i, WIN), 0, 0)),
                full((1, D_MODEL)), full((1, D_MODEL)),
                full((D_MODEL, D_EXPERT)), full((D_MODEL, D_EXPERT)), full((D_EXPERT, D_MODEL)),
            ],
            out_specs=out_specs,
            scratch_shapes=[pltpu.VMEM((WIN, RL), BF16),
                            pltpu.VMEM((TOP_K, WIN, SORT_ROWS), jnp.int16),
                            pltpu.VMEM((TOP_K, WIN, SORT_ROWS), BF16)],
        ),
        compiler_params=_cparams(("arbitrary",)),
        name="moe_combine",
    )(rw, yl, lpos_t, wt, h2, x1, mods, lng, lnb, wsg, wsu, wsd)


def _moe_and_norm(x1, h2, lgt, mods, lng, lnb, router_bias, tri, wg, wu, wd, wsg, wsu, wsd, *, layer, split):
    wts, lpos, p16 = _route(lgt, router_bias.reshape(N_EXPERTS, 1), tri)
    rw, block_e, n_used, gather_row, scatter_row = _moe_tables(p16[:, :N_EXPERTS].astype(I32))
    xl = _sort_rows(rw, lpos, h2)
    yl = _experts(block_e, n_used, gather_row, scatter_row, xl, wg, wu, wd, layer)
    return _combine(rw, yl, lpos.T, wts.T, h2, x1, mods, lng, lnb, wsg, wsu, wsd, split=split)


def _rope_tables_mla():
    t = jnp.arange(DEC_SEQ)
    row = (t // GRID_W).astype(F32)
    col = (t % GRID_W).astype(F32)
    n = QK_ROPE // 4
    inv = ROPE_BASE ** (-jnp.arange(n, dtype=F32) / n)
    ang_r = row[:, None] * inv
    ang_c = col[:, None] * inv
    cos = jnp.ones((DEC_SEQ, LANE), F32)
    sin_m = jnp.zeros((DEC_SEQ, LANE), F32)
    sin_p = jnp.zeros((DEC_SEQ, LANE), F32)
    l0 = ROPE_LANE0
    for base, ang in ((l0, ang_r), (l0 + 2 * n, ang_c)):
        c, s = jnp.cos(ang), jnp.sin(ang)
        cos = cos.at[:, base:base + n].set(c).at[:, base + n:base + 2 * n].set(c)
        sin_m = sin_m.at[:, base:base + n].set(-s)
        sin_p = sin_p.at[:, base + n:base + 2 * n].set(s)
    ident = (jnp.ones((TM, LANE), F32), jnp.zeros((TM, LANE), F32), jnp.zeros((TM, LANE), F32))
    return tuple(jnp.concatenate([i, tbl], axis=0) for i, tbl in zip(ident, (cos, sin_m, sin_p)))


def _rope_tables_ret():
    half = RET_DK // 2
    theta = ROPE_BASE ** (-jnp.linspace(0.0, 1.0, half, dtype=F32))
    ang = jnp.arange(DEC_SEQ, dtype=F32)[:, None] * theta
    cos = jnp.concatenate([jnp.ones((TM, half), F32), jnp.cos(ang)], axis=0)
    sin = jnp.concatenate([jnp.zeros((TM, half), F32), jnp.sin(ang)], axis=0)
    return cos, sin


def _pad_heads(w, width, lane0=0):
    k = w.shape[0]
    w = w.reshape(k, MLA_HEADS, width)
    out = jnp.zeros((k, MLA_HEADS, HEAD_PAD), w.dtype).at[:, :, lane0:lane0 + width].set(w)
    return out.reshape(k, MLA_HEADS * HEAD_PAD)


def _rg_gate_weights(wa, ba, wx, bx):
    n_ct = RG_W // LANE
    per = LANE // RG_BW
    tiles_w, tiles_b = [], []
    for c in range(n_ct):
        cols_w, cols_b = [], []
        for d in range(2):
            for w, b in ((wa, ba), (wx, bx)):
                m = jnp.zeros((LANE, LANE), F32)
                for p in range(per):
                    m = m.at[p * RG_BW:(p + 1) * RG_BW, p * RG_BW:(p + 1) * RG_BW].set(w[d, c * per + p])
                cols_w.append(m)
                cols_b.append(b[d, c * LANE:(c + 1) * LANE])
        tiles_w.append(jnp.concatenate(cols_w, axis=1))
        tiles_b.append(jnp.concatenate(cols_b, axis=0)[None, :])
    return jnp.stack(tiles_w).astype(BF16), jnp.stack(tiles_b)


def kernel(x_prompt, x_sample, cache_mla_ckv, cache_mla_krope, state_rglru, state_ret, c, c_ctx, w_ada, b_ada,
           ln_g, ln_b, w_in_ab, rg_conv_w, rg_conv_b, rg_wa, rg_ba, rg_wx, rg_bx, rg_lambda, mla_q_norm, mla_w_uq,
           mla_kv_norm, mla_w_ukv, w_out_ab, w_in_c, ret_gamma_logit, w_out_c, w_router, router_bias,
           w_exp_gate, w_exp_up, w_exp_down, w_sh_gate, w_sh_up, w_sh_down):
    xp = x_prompt.reshape(N_PROMPT, D_MODEL)
    xs = x_sample.reshape(N_SAMPLE, D_MODEL)
    cond = jnp.zeros((16, D_MODEL), F32).at[0].set(c_ctx).at[1:1 + DEC_BATCH].set(c)
    mods_all = _ada_modulation(cond, w_ada, b_ada).reshape(DEPTH, 16, 6, D_MODEL)[:, :N_COND]
    mods_all = jnp.pad(mods_all, ((0, 0), (0, 0), (0, MOD_ROWS - 6), (0, 0)))

    tri = (jnp.arange(WIN)[:, None] < jnp.arange(WIN)[None, :]).astype(BF16)
    wr_t = jnp.swapaxes(w_router, 1, 2)
    wr_hi = wr_t.astype(BF16)
    wr_lo = (wr_t - wr_hi.astype(F32)).astype(BF16)
    wg_e, wu_e, wd_e = w_exp_gate, w_exp_up, w_exp_down
    wsg, wsu, wsd = w_sh_gate.astype(BF16), w_sh_up.astype(BF16), w_sh_down.astype(BF16)

    l, e = 0, 0
    mods = mods_all[l]
    n_main = 2 * RG_W + Q_LORA + KV_LORA
    w_main = w_in_ab[e][:, :n_main].astype(BF16)
    w_kr = jnp.zeros((D_MODEL, LANE), F32).at[:, ROPE_LANE0:ROPE_LANE0 + QK_ROPE].set(w_in_ab[e][:, n_main:]).astype(BF16)
    main, krp = _proj_ab(xp, xs, mods, w_main, w_kr)

    wg_rg, bg_rg = _rg_gate_weights(rg_wa[e], rg_ba[e], rg_wx[e], rg_bx[e])
    h0_p = jnp.zeros((BATCH, 2, RG_W), F32)
    rg_args = (rg_conv_w[e], rg_conv_b[e].reshape(1, RG_W), wg_rg, bg_rg, rg_lambda[e])
    yrg_p, rg_fin = _rglru(main, *rg_args, h0_p, n_seq=BATCH, seq=SEQ, row_block0=0)
    yrg_s, _ = _rglru(main, *rg_args, state_rglru[:, e],
                      n_seq=DEC_BATCH, seq=DEC_SEQ, row_block0=N_PROMPT // DEC_SEQ)

    cos_t, sm_t, sp_t = _rope_tables_mla()
    w_uq = mla_w_uq[e].reshape(Q_LORA, MLA_HEADS, QK_NOPE + QK_ROPE)
    wq_p = _pad_heads(w_uq.reshape(Q_LORA, -1), QK_NOPE + QK_ROPE).astype(BF16)
    w_ukv = mla_w_ukv[e].reshape(KV_LORA, MLA_HEADS, QK_NOPE + V_HEAD)
    wuk_p = _pad_heads(w_ukv[:, :, :QK_NOPE].reshape(KV_LORA, -1), QK_NOPE).astype(BF16)
    wuvt = w_ukv[:, :, QK_NOPE:].reshape(KV_LORA, MLA_HEADS * V_HEAD).T.astype(BF16)
    q_att, k_att, v_att, ckv_n = _mla_prep(main, krp, cos_t, sm_t, sp_t, mla_q_norm[e].reshape(1, Q_LORA), wq_p,
                                           mla_kv_norm[e].reshape(1, KV_LORA), wuk_p, wuvt)
    ctx_ckv = cache_mla_ckv[:, e].reshape(DEC_BATCH * PAST_LEN, KV_LORA)
    ctx_krp = jnp.zeros((DEC_BATCH * PAST_LEN, LANE), F32).at[:, ROPE_LANE0:ROPE_LANE0 + QK_ROPE].set(
        cache_mla_krope[:, e].reshape(DEC_BATCH * PAST_LEN, QK_ROPE))
    kc_att, vc_att = _mla_ctx(ctx_ckv, ctx_krp, wuk_p, wuvt)

    o_att_p = _attention(q_att, k_att, v_att, None, None, n_seq=BATCH, seq=SEQ, row_block0=0, tq=SEQ)
    o_att_s = _attention(q_att, k_att, v_att, kc_att, vc_att,
                         n_seq=DEC_BATCH, seq=DEC_SEQ, row_block0=N_PROMPT // DEC_SEQ, tq=TQ)

    w_out = w_out_ab[e].astype(BF16)
    x1, h2, lgt = _out_ab(yrg_p, yrg_s, o_att_p, o_att_s, w_out[:RG_W], w_out[RG_W:], xp, xs, mods,
                          ln_g[l, 0].reshape(1, D_MODEL), ln_b[l, 0].reshape(1, D_MODEL), wr_hi[l], wr_lo[l])
    x = _moe_and_norm(x1, h2, lgt, mods, ln_g[l, 1].reshape(1, D_MODEL), ln_b[l, 1].reshape(1, D_MODEL),
                      router_bias[l], tri, wg_e, wu_e, wd_e, wsg[l], wsu[l], wsd[l], layer=l, split=False)

    new_ckv = ckv_n.reshape(BATCH, 1, SEQ, KV_LORA)
    new_krope = krp[:N_PROMPT, ROPE_LANE0:ROPE_LANE0 + QK_ROPE].reshape(BATCH, 1, SEQ, QK_ROPE)
    new_rg = rg_fin.reshape(BATCH, 1, 2, RG_W)

    l, o = 1, 0
    mods = mods_all[l]
    qk = RET_HEADS * RET_DK
    w_c = w_in_c[o].astype(BF16)
    cos_r, sin_r = _rope_tables_ret()
    q_r, k_r, v_r, g_r = _proj_c(x, mods, cos_r, sin_r, w_c[:, :qk], w_c[:, qk:2 * qk],
                                 w_c[:, 2 * qk:2 * qk + MIX_C], w_c[:, 2 * qk + MIX_C:])
    gam = jnp.broadcast_to(ret_gamma_logit[o].astype(F32)[:, :, None, None], (2, RET_HEADS, SUBLANE, LANE))
    r0_p = jnp.zeros((BATCH, 2, RET_HEADS, RET_DK, RET_DV), F32)
    o_ret_p, r_fin = _retention(q_r, k_r, v_r, gam, r0_p, n_seq=BATCH, seq=SEQ, row_block0=0, with_state=True)
    (o_ret_s,) = _retention(q_r, k_r, v_r, gam, state_ret[:, o],
                            n_seq=DEC_BATCH, seq=DEC_SEQ, row_block0=N_PROMPT // DEC_SEQ, with_state=False)
    x1, h2, lgt = _out_c(o_ret_p, o_ret_s, g_r, w_out_c[o].astype(BF16), x, mods,
                         ln_g[l, 0].reshape(1, D_MODEL), ln_b[l, 0].reshape(1, D_MODEL), wr_hi[l], wr_lo[l])
    y_p, y_s = _moe_and_norm(x1, h2, lgt, mods, ln_g[l, 1].reshape(1, D_MODEL), ln_b[l, 1].reshape(1, D_MODEL),
                             router_bias[l], tri, wg_e, wu_e, wd_e, wsg[l], wsu[l], wsd[l], layer=l, split=True)

    y_prompt = y_p.reshape(BATCH, SEQ, D_MODEL)
    y_sample = y_s.reshape(DEC_BATCH, DEC_SEQ, D_MODEL)
    new_ret = r_fin.reshape(BATCH, 1, 2, RET_HEADS, RET_DK, RET_DV)
    return (y_prompt, y_sample, new_ckv, new_krope, new_rg, new_ret)
```

```python
import functools
import math

import jax
import jax.numpy as jnp
from jax import lax
from jax.experimental import pallas as pl
from jax.experimental.pallas import tpu as pltpu

F32 = jnp.float32
BF16 = jnp.bfloat16
I32 = jnp.int32

D_MODEL = 1024
BATCH, SEQ = 16, 256
DEC_BATCH, DEC_SEQ = 8, 4096
PAST_LEN = 256
DEPTH = 2
GRID_W = 64
RG_W, RG_BLOCKS = 512, 8
RG_BW = RG_W // RG_BLOCKS
RG_C = 8.0
CONV_W, CONV_LEFT = 4, 2
MLA_HEADS, QK_NOPE, QK_ROPE, V_HEAD = 8, 64, 32, 64
Q_LORA, KV_LORA = 768, 256
ROPE_BASE = 10000.0
ATTN_SCALE = (QK_NOPE + QK_ROPE) ** -0.5
RET_HEADS, RET_DK, RET_DV, RET_CHUNK = 4, 256, 512, 128
MIX_C = RET_HEADS * RET_DV
N_EXPERTS, TOP_K, N_GROUPS, TOPK_GROUPS = 64, 8, 8, 4
GROUP_SIZE = N_EXPERTS // N_GROUPS
D_EXPERT = 256
ROUTED_SCALE = 2.5
ALPHA = (2 * DEPTH) ** 0.25
EPS = 1e-6

N_PROMPT = BATCH * SEQ
N_SAMPLE = DEC_BATCH * DEC_SEQ
N_TOK = N_PROMPT + N_SAMPLE
N_COND = 1 + DEC_BATCH
MOD_ROWS = 8

LANE = 128
SUBLANE = 8
TM = 512
HEAD_PAD = 128
ROPE_LANE0 = QK_NOPE
TQ = 256
KC = 256
V_ONES = 16
Q_PRESCALE = ATTN_SCALE * math.log2(math.e)
SCAN_ROWS = 64
GATE_ROWS = 256
WIN = 256
N_WIN = N_TOK // WIN
PIECE = 16
SORT_ROWS = 256
RL = 3072
TME = 1024
PIECES = TME // PIECE
N_PAIRS = N_TOK * TOP_K
N_EBLOCKS = (N_PAIRS + N_WIN * N_EXPERTS * (PIECE - 1)) // TME + N_EXPERTS
NEG = -1e30


def _cparams(sem, vmem_mb=48):
    return pltpu.CompilerParams(dimension_semantics=sem, vmem_limit_bytes=vmem_mb * 1024 * 1024)


def _cond_block(i, tm):
    npb = N_PROMPT // tm
    return jnp.where(i < npb, 0, 1 + (i - npb) // (DEC_SEQ // tm))


def _pos_block(i, tm):
    npb = N_PROMPT // tm
    return jnp.where(i < npb, 0, 1 + (i - npb) % (DEC_SEQ // tm))


def _split_hi_lo(a):
    hi = a.astype(BF16)
    lo = (a - hi.astype(F32)).astype(BF16)
    return hi, lo


def _dot(a, b):
    return jnp.dot(a, b, preferred_element_type=F32)


def _dot_nt(a, b):
    return lax.dot_general(a, b, (((1,), (1,)), ((), ())), preferred_element_type=F32)


def _silu(x):
    return x * jax.nn.sigmoid(x)


def _gelu_tanh(x):
    return 0.5 * x * (1.0 + jnp.tanh(math.sqrt(2.0 / math.pi) * (x + 0.044715 * (x * x * x))))


def _softplus(x):
    return jnp.maximum(x, 0.0) + jnp.log1p(jnp.exp(-jnp.abs(x)))


def _layernorm_rows(z, g, b):
    mu = jnp.mean(z, axis=-1, keepdims=True)
    zc = z - mu
    var = jnp.mean(zc * zc, axis=-1, keepdims=True)
    return (zc * lax.rsqrt(var + EPS)) * g + b


def _ada_kernel(c_ref, w_ref, b_ref, o_ref):
    s_hi, s_lo = _split_hi_lo(_silu(c_ref[...]))
    w_hi, w_lo = _split_hi_lo(w_ref[...])
    o_ref[...] = _dot(s_hi, w_hi) + _dot(s_hi, w_lo) + _dot(s_lo, w_hi) + b_ref[...]


def _ada_modulation(cond, w_ada, b_ada):
    n6 = 6 * D_MODEL
    tn = D_MODEL
    return pl.pallas_call(
        _ada_kernel,
        out_shape=jax.ShapeDtypeStruct((DEPTH, 16, n6), F32),
        grid=(DEPTH, n6 // tn),
        in_specs=[
            pl.BlockSpec((16, D_MODEL), lambda l, j: (0, 0)),
            pl.BlockSpec((None, D_MODEL, tn), lambda l, j: (l, 0, j)),
            pl.BlockSpec((None, 1, tn), lambda l, j: (l, 0, j)),
        ],
        out_specs=pl.BlockSpec((None, 16, tn), lambda l, j: (l, 0, j)),
        compiler_params=_cparams(("arbitrary", "arbitrary")),
        name="ada_modulation",
    )(cond, w_ada, b_ada.reshape(DEPTH, 1, n6))


def _modulated(x, mod_ref, shift_row, scale_row):
    return x * (1.0 + mod_ref[scale_row:scale_row + 1, :]) + mod_ref[shift_row:shift_row + 1, :]


def _row_pair_specs(tm, width, col=0):
    npb = N_PROMPT // tm
    return [pl.BlockSpec((tm, width), lambda i, *_: (jnp.minimum(i, npb - 1), col)),
            pl.BlockSpec((tm, width), lambda i, *_: (jnp.maximum(i - npb, 0), col))]


def _pick_rows(p_ref, s_ref, tm):
    return jnp.where(pl.program_id(0) < N_PROMPT // tm, p_ref[...], s_ref[...])


def _proj_ab_kernel(xp_ref, xs_ref, mod_ref, w_ref, wkr_ref, main_ref, kr_ref):
    h = _modulated(_pick_rows(xp_ref, xs_ref, TM), mod_ref, 0, 1).astype(BF16)
    n = w_ref.shape[1]
    step = 512
    for j in range(n // step):
        main_ref[:, j * step:(j + 1) * step] = _dot(h, w_ref[:, j * step:(j + 1) * step])
    kr_ref[...] = _dot(h, wkr_ref[...])


def _proj_ab(xp, xs, mods, w_main, w_kr):
    n = w_main.shape[1]
    return pl.pallas_call(
        _proj_ab_kernel,
        out_shape=(jax.ShapeDtypeStruct((N_TOK, n), F32), jax.ShapeDtypeStruct((N_TOK, LANE), F32)),
        grid=(N_TOK // TM,),
        in_specs=_row_pair_specs(TM, D_MODEL) + [
            pl.BlockSpec((None, MOD_ROWS, D_MODEL), lambda i: (_cond_block(i, TM), 0, 0)),
            pl.BlockSpec((D_MODEL, n), lambda i: (0, 0)),
            pl.BlockSpec((D_MODEL, LANE), lambda i: (0, 0)),
        ],
        out_specs=(pl.BlockSpec((TM, n), lambda i: (i, 0)), pl.BlockSpec((TM, LANE), lambda i: (i, 0))),
        compiler_params=_cparams(("arbitrary",)),
        name="proj_ab",
    )(xp, xs, mods, w_main, w_kr)


def _rglru_kernel(xr_ref, gr_ref, cw_ref, cb_ref, wg_ref, bg_ref, lam_ref, h0_ref,
                  y_ref, hfin_ref, xpad, a_s, b_s, *, seq):
    pad = SUBLANE
    xpad[0:pad, :] = jnp.zeros((pad, LANE), F32)
    xpad[seq + pad:seq + 2 * pad, :] = jnp.zeros((pad, LANE), F32)
    xpad[pad:seq + pad, :] = xr_ref[...]

    sp = _softplus(-lam_ref[...])
    cw = cw_ref[...]
    cb = cb_ref[...]
    wg = wg_ref[...]
    bg = bg_ref[...]

    def gate_step(c, carry):
        t0 = pl.multiple_of(c * GATE_ROWS, GATE_ROWS)
        win = xpad[pl.ds(t0, GATE_ROWS + 2 * pad), :]
        xc = cb
        for j in range(CONV_W):
            off = pad - CONV_LEFT + j
            xc = xc + win[off:off + GATE_ROWS, :] * cw[j:j + 1, :]
        g = _dot(xc.astype(BF16), wg) + bg
        for d in range(2):
            r = jax.nn.sigmoid(g[:, (2 * d) * LANE:(2 * d + 1) * LANE])
            i = jax.nn.sigmoid(g[:, (2 * d + 1) * LANE:(2 * d + 2) * LANE])
            log_a = (-RG_C * r) * sp[d:d + 1, :]
            a = jnp.exp(log_a)
            t = jnp.tanh(log_a)
            bt = jnp.sqrt(2.0 * t / (t - 1.0)) * (i * xc)
            a_s[d, pl.ds(t0, GATE_ROWS), :] = a
            b_s[d, pl.ds(t0, GATE_ROWS), :] = bt
        return carry

    lax.fori_loop(0, seq // GATE_ROWS, gate_step, 0)

    row = lax.broadcasted_iota(I32, (SCAN_ROWS, LANE), 0) % SUBLANE
    n_steps = seq // SCAN_ROWS
    tiles = SCAN_ROWS // SUBLANE

    def local_scan(a, b, reverse):
        for k in (1, 2, 4):
            if reverse:
                ok = row < SUBLANE - k
                shift = SCAN_ROWS - k
            else:
                ok = row >= k
                shift = k
            a_sh = jnp.where(ok, pltpu.roll(a, shift, 0), 1.0)
            b_sh = jnp.where(ok, pltpu.roll(b, shift, 0), 0.0)
            b = a * b_sh + b
            a = a * a_sh
        return a, b

    def fwd_step(c, h):
        t0 = pl.multiple_of(c * SCAN_ROWS, SCAN_ROWS)
        a, b = local_scan(a_s[0, pl.ds(t0, SCAN_ROWS), :], b_s[0, pl.ds(t0, SCAN_ROWS), :], False)
        outs = []
        for j in range(tiles):
            hj = a[j * SUBLANE:(j + 1) * SUBLANE, :] * h + b[j * SUBLANE:(j + 1) * SUBLANE, :]
            outs.append(hj)
            h = hj[SUBLANE - 1:SUBLANE, :]
        y_ref[pl.ds(t0, SCAN_ROWS), :] = jnp.concatenate(outs, axis=0)
        return h

    h_f = lax.fori_loop(0, n_steps, fwd_step, h0_ref[0:1, :])

    def bwd_step(c, h):
        t0 = pl.multiple_of((n_steps - 1 - c) * SCAN_ROWS, SCAN_ROWS)
        a, b = local_scan(a_s[1, pl.ds(t0, SCAN_ROWS), :], b_s[1, pl.ds(t0, SCAN_ROWS), :], True)
        outs = [None] * tiles
        for j in reversed(range(tiles)):
            hj = a[j * SUBLANE:(j + 1) * SUBLANE, :] * h + b[j * SUBLANE:(j + 1) * SUBLANE, :]
            outs[j] = hj
            h = hj[0:1, :]
        hb = jnp.concatenate(outs, axis=0)
        y_ref[pl.ds(t0, SCAN_ROWS), :] = (y_ref[pl.ds(t0, SCAN_ROWS), :] + hb) * _gelu_tanh(gr_ref[pl.ds(t0, SCAN_ROWS), :])
        return h

    h_b = lax.fori_loop(0, n_steps, bwd_step, h0_ref[1:2, :])
    hfin_ref[0:1, :] = h_f
    hfin_ref[1:2, :] = h_b


def _rglru(main, cw, cb, wg, bg, lam, h0, *, n_seq, seq, row_block0):
    n_ct = RG_W // LANE
    gr_col0 = RG_W // LANE
    kern = functools.partial(_rglru_kernel, seq=seq)
    return pl.pallas_call(
        kern,
        out_shape=(jax.ShapeDtypeStruct((n_seq * seq, RG_W), F32), jax.ShapeDtypeStruct((n_seq, 2, RG_W), F32)),
        grid=(n_seq, n_ct),
        in_specs=[
            pl.BlockSpec((seq, LANE), lambda b, c: (row_block0 + b, c)),
            pl.BlockSpec((seq, LANE), lambda b, c: (row_block0 + b, gr_col0 + c)),
            pl.BlockSpec((CONV_W, LANE), lambda b, c: (0, c)),
            pl.BlockSpec((1, LANE), lambda b, c: (0, c)),
            pl.BlockSpec((None, LANE, 4 * LANE), lambda b, c: (c, 0, 0)),
            pl.BlockSpec((None, 1, 4 * LANE), lambda b, c: (c, 0, 0)),
            pl.BlockSpec((2, LANE), lambda b, c: (0, c)),
            pl.BlockSpec((None, 2, LANE), lambda b, c: (b, 0, c)),
        ],
        out_specs=(
            pl.BlockSpec((seq, LANE), lambda b, c: (b, c)),
            pl.BlockSpec((None, 2, LANE), lambda b, c: (b, 0, c)),
        ),
        scratch_shapes=[
            pltpu.VMEM((seq + 2 * SUBLANE, LANE), F32),
            pltpu.VMEM((2, seq, LANE), F32),
            pltpu.VMEM((2, seq, LANE), F32),
        ],
        compiler_params=_cparams(("arbitrary", "arbitrary")),
        name=f"rglru_s{seq}",
    )(main, main, cw, cb, wg, bg, lam, h0)


def _rope_lanes(x, cos, sin_m, sin_p):
    n = x.shape[1] // LANE
    half = QK_ROPE // 4
    cos_t = jnp.concatenate([cos] * n, axis=1) if n > 1 else cos
    sm_t = jnp.concatenate([sin_m] * n, axis=1) if n > 1 else sin_m
    sp_t = jnp.concatenate([sin_p] * n, axis=1) if n > 1 else sin_p
    up = pltpu.roll(x, x.shape[1] - half, 1)
    dn = pltpu.roll(x, half, 1)
    return x * cos_t + up * sm_t + dn * sp_t


def _mla_prep_kernel(cq0_ref, cq1_ref, cq2_ref, ckv_ref, kr_ref, cos_ref, sm_ref, sp_ref,
                     qn_ref, wq_ref, kvn_ref, wuk_ref, wuvt_ref,
                     q_ref, k_ref, vt_ref, ckvn_ref):
    cq = [cq0_ref[...], cq1_ref[...], cq2_ref[...]]
    ms = (jnp.sum(cq[0] * cq[0], axis=-1, keepdims=True) + jnp.sum(cq[1] * cq[1], axis=-1, keepdims=True)
          + jnp.sum(cq[2] * cq[2], axis=-1, keepdims=True)) * (1.0 / Q_LORA)
    inv = lax.rsqrt(ms + EPS)
    blk = Q_LORA // 3
    q = None
    for j in range(3):
        cqn = ((cq[j] * inv) * qn_ref[:, j * blk:(j + 1) * blk]).astype(BF16)
        part = _dot(cqn, wq_ref[j * blk:(j + 1) * blk, :])
        q = part if q is None else q + part
    cos, sm, sp = cos_ref[...], sm_ref[...], sp_ref[...]
    q_ref[...] = (_rope_lanes(q, cos, sm, sp) * Q_PRESCALE).T.astype(BF16)

    ckv = ckv_ref[...]
    inv_kv = lax.rsqrt(jnp.mean(ckv * ckv, axis=-1, keepdims=True) + EPS)
    ckvn = (ckv * inv_kv) * kvn_ref[...]

    @pl.when(pl.program_id(0) < N_PROMPT // TM)
    def _():
        ckvn_ref[...] = ckvn

    ckvn_b = ckvn.astype(BF16)
    kr_rot = _rope_lanes(kr_ref[...], cos, sm, sp)
    k_ref[...] = (_dot(ckvn_b, wuk_ref[...]) + jnp.concatenate([kr_rot] * MLA_HEADS, axis=1)).astype(BF16)
    vt = _dot_nt(wuvt_ref[...], ckvn_b).astype(BF16)
    for c in range(vt_ref.shape[0]):
        vt_ref[c] = vt[:, c * KC:(c + 1) * KC]


def _mla_prep(main, krp, cos_t, sm_t, sp_t, q_norm, wq_p, kv_norm, wuk_p, wuvt):
    cq_col0 = 2 * RG_W // 256
    hp = MLA_HEADS * HEAD_PAD
    full = lambda shape: pl.BlockSpec(shape, lambda i: (0,) * len(shape))
    tab = pl.BlockSpec((TM, LANE), lambda i: (_pos_block(i, TM), 0))
    return pl.pallas_call(
        _mla_prep_kernel,
        out_shape=(
            jax.ShapeDtypeStruct((hp, N_TOK), BF16),
            jax.ShapeDtypeStruct((N_TOK, hp), BF16),
            jax.ShapeDtypeStruct((N_TOK // KC, MLA_HEADS * V_HEAD, KC), BF16),
            jax.ShapeDtypeStruct((N_PROMPT, KV_LORA), F32),
        ),
        grid=(N_TOK // TM,),
        in_specs=[
            pl.BlockSpec((TM, 256), lambda i: (i, cq_col0)),
            pl.BlockSpec((TM, 256), lambda i: (i, cq_col0 + 1)),
            pl.BlockSpec((TM, 256), lambda i: (i, cq_col0 + 2)),
            pl.BlockSpec((TM, 256), lambda i: (i, cq_col0 + 3)),
            pl.BlockSpec((TM, LANE), lambda i: (i, 0)),
            tab, tab, tab,
            full((1, Q_LORA)), full((Q_LORA, hp)), full((1, KV_LORA)), full((KV_LORA, hp)),
            full((MLA_HEADS * V_HEAD, KV_LORA)),
        ],
        out_specs=(
            pl.BlockSpec((hp, TM), lambda i: (0, i)),
            pl.BlockSpec((TM, hp), lambda i: (i, 0)),
            pl.BlockSpec((TM // KC, MLA_HEADS * V_HEAD, KC), lambda i: (i, 0, 0)),
            pl.BlockSpec((TM, KV_LORA), lambda i: (jnp.minimum(i, N_PROMPT // TM - 1), 0)),
        ),
        compiler_params=_cparams(("arbitrary",)),
        name="mla_prep",
    )(main, main, main, main, krp, cos_t, sm_t, sp_t, q_norm, wq_p, kv_norm, wuk_p, wuvt)


def _mla_ctx_kernel(ckv_ref, kr_ref, wuk_ref, wuvt_ref, k_ref, vt_ref):
    ckv_b = ckv_ref[...].astype(BF16)
    k_ref[...] = (_dot(ckv_b, wuk_ref[...]) + jnp.concatenate([kr_ref[...]] * MLA_HEADS, axis=1)).astype(BF16)
    vt_ref[...] = _dot_nt(wuvt_ref[...], ckv_b).astype(BF16)


def _mla_ctx(ctx_ckv, ctx_krp, wuk_p, wuvt):
    n = ctx_ckv.shape[0]
    hp = MLA_HEADS * HEAD_PAD
    tm = KC
    full = lambda shape: pl.BlockSpec(shape, lambda i: (0,) * len(shape))
    return pl.pallas_call(
        _mla_ctx_kernel,
        out_shape=(jax.ShapeDtypeStruct((n, hp), BF16),
                   jax.ShapeDtypeStruct((n // tm, MLA_HEADS * V_HEAD, tm), BF16)),
        grid=(n // tm,),
        in_specs=[
            pl.BlockSpec((tm, KV_LORA), lambda i: (i, 0)),
            pl.BlockSpec((tm, LANE), lambda i: (i, 0)),
            full((KV_LORA, hp)), full((MLA_HEADS * V_HEAD, KV_LORA)),
        ],
        out_specs=(pl.BlockSpec((tm, hp), lambda i: (i, 0)),
                   pl.BlockSpec((None, MLA_HEADS * V_HEAD, tm), lambda i: (i, 0, 0))),
        compiler_params=_cparams(("arbitrary",)),
        name="mla_ctx",
    )(ctx_ckv, ctx_krp, wuk_p, wuvt)


def _attn_kernel(*refs, seq, tq, n_ctx):
    if n_ctx:
        q_ref, k_ref, vt_ref, kc_ref, vtc_ref, o_ref, s_scr, p_scr, k_all, vt_all = refs
    else:
        q_ref, k_ref, vt_ref, o_ref, s_scr, p_scr, k_all, vt_all = refs
    has_ctx = 1 if n_ctx else 0
    n_own = seq // KC
    n = n_own + has_ctx
    qs = [q_ref[h * HEAD_PAD:(h + 1) * HEAD_PAD, :] for h in range(2)]

    @pl.when(pl.program_id(2) == 0)
    def _():
        k_all[0:seq, :] = k_ref[...]
        if has_ctx:
            k_all[seq:seq + KC, :] = kc_ref[...]
        for h in range(2):
            vt_all[0:n_own, h, 0:V_HEAD, :] = vt_ref[:, h * V_HEAD:(h + 1) * V_HEAD, :]
            if has_ctx:
                vt_all[n_own, h, 0:V_HEAD, :] = vtc_ref[h * V_HEAD:(h + 1) * V_HEAD, :]
            vt_all[:, h, V_HEAD:V_HEAD + V_ONES, :] = jnp.ones((n, V_ONES, KC), BF16)

    def k_chunk(c, h):
        t0 = c * KC if isinstance(c, int) else pl.multiple_of(c * KC, KC)
        return k_all[pl.ds(t0, KC), h * HEAD_PAD:(h + 1) * HEAD_PAD]

    def scores(c, slot):
        for h in range(2):
            s_scr[slot, h] = _dot(k_chunk(c, h), qs[h])

    def softmax_chunk(slot, st):
        out = []
        for h in range(2):
            m, _, acc = st[h]
            t = s_scr[slot, h]
            m_new = jnp.maximum(m, jnp.max(t, axis=0, keepdims=True))
            p_scr[slot, h] = jnp.exp2(t - m_new).astype(BF16)
            out.append((m_new, jnp.exp2(m - m_new), acc))
        return out

    def weighted_values(c, slot, st, alphas):
        return [(st[h][0], st[h][1], alphas[h] * st[h][2] + _dot(vt_all[c, h], p_scr[slot, h])) for h in range(2)]

    def step(c, slot, st, with_s, with_v):
        if with_s:
            scores(c + 1, 1 - slot)
        alphas = [st[h][1] for h in range(2)]
        st = softmax_chunk(slot, st)
        if with_v:
            prev = max(c - 1, 0) if isinstance(c, int) else jnp.maximum(c - 1, 0)
            st = weighted_values(prev, 1 - slot, st, alphas)
        return st

    p_scr[1] = jnp.zeros(p_scr.shape[1:], BF16)
    st = [(jnp.full((1, tq), NEG, F32), jnp.ones((1, tq), F32), jnp.zeros((V_HEAD + V_ONES, tq), F32))
          for _ in range(2)]
    scores(0, 0)
    n_pairs = (n - 1) // 2

    def pair(j, flat):
        st = [tuple(flat[0:3]), tuple(flat[3:6])]
        st = step(2 * j, 0, st, True, True)
        st = step(2 * j + 1, 1, st, True, True)
        return tuple(st[0]) + tuple(st[1])

    if n_pairs:
        flat = lax.fori_loop(0, n_pairs, pair, tuple(st[0]) + tuple(st[1]))
        st = [tuple(flat[0:3]), tuple(flat[3:6])]
    for c in range(2 * n_pairs, n):
        st = step(c, c % 2, st, c + 1 < n, c > 0)
    st = weighted_values(n - 1, (n - 1) % 2, st, [st[h][1] for h in range(2)])
    for h in range(2):
        acc = st[h][2]
        o_ref[:, h * V_HEAD:(h + 1) * V_HEAD] = (acc[0:V_HEAD] / acc[V_HEAD:V_HEAD + 1]).T.astype(o_ref.dtype)


def _attention(q, k, vt, kc, vtc, *, n_seq, seq, row_block0, tq):
    n_ctx = 0 if kc is None else PAST_LEN
    n_hp = MLA_HEADS // 2
    nq = seq // tq
    kern = functools.partial(_attn_kernel, seq=seq, tq=tq, n_ctx=n_ctx)
    in_specs = [
        pl.BlockSpec((2 * HEAD_PAD, tq), lambda b, j, i: (j, (row_block0 + b) * nq + i)),
        pl.BlockSpec((seq, 2 * HEAD_PAD), lambda b, j, i: (row_block0 + b, j)),
        pl.BlockSpec((seq // KC, 2 * V_HEAD, KC), lambda b, j, i: (row_block0 + b, j, 0)),
    ]
    args = [q, k, vt]
    if n_ctx:
        in_specs += [
            pl.BlockSpec((n_ctx, 2 * HEAD_PAD), lambda b, j, i: (b, j)),
            pl.BlockSpec((None, 2 * V_HEAD, KC), lambda b, j, i: (b, j, 0)),
        ]
        args += [kc, vtc]
    return pl.pallas_call(
        kern,
        out_shape=jax.ShapeDtypeStruct((n_seq * seq, MLA_HEADS * V_HEAD), BF16),
        grid=(n_seq, n_hp, nq),
        in_specs=in_specs,
        out_specs=pl.BlockSpec((tq, 2 * V_HEAD), lambda b, j, i: (b * nq + i, j)),
        scratch_shapes=[pltpu.VMEM((2, 2, KC, tq), F32), pltpu.VMEM((2, 2, KC, tq), BF16),
                        pltpu.VMEM((seq + n_ctx, 2 * HEAD_PAD), BF16),
                        pltpu.VMEM(((seq + n_ctx) // KC, 2, V_HEAD + V_ONES, KC), BF16)],
        compiler_params=_cparams(("arbitrary", "arbitrary", "arbitrary")),
        name=f"mla_attention_s{seq}",
    )(*args)


def _post_mixer(y, x, mod_ref, lng_ref, lnb_ref, wrh_ref, wrl_ref, x1_ref, h2_ref, lgt_ref):
    z = ALPHA * x + mod_ref[2:3, :] * y
    x1 = _layernorm_rows(z, lng_ref[...], lnb_ref[...])
    x1_ref[...] = x1
    h2 = _modulated(x1, mod_ref, 3, 4)
    h_hi, h_lo = _split_hi_lo(h2)
    h2_ref[...] = h_hi
    w_hi, w_lo = wrh_ref[...], wrl_ref[...]
    lgt_ref[...] = _dot_nt(w_hi, h_hi) + _dot_nt(w_hi, h_lo) + _dot_nt(w_lo, h_hi)


def _out_ab_kernel(yrgp_ref, yrgs_ref, op_ref, os_ref, wa_ref, wb_ref, xp_ref, xs_ref,
                   mod_ref, lng_ref, lnb_ref, wrh_ref, wrl_ref, x1_ref, h2_ref, lgt_ref):
    y = (_dot(_pick_rows(yrgp_ref, yrgs_ref, TM).astype(BF16), wa_ref[...])
         + _dot(_pick_rows(op_ref, os_ref, TM), wb_ref[...]))
    _post_mixer(y, _pick_rows(xp_ref, xs_ref, TM), mod_ref, lng_ref, lnb_ref, wrh_ref, wrl_ref,
                x1_ref, h2_ref, lgt_ref)


def _post_specs():
    full = lambda shape: pl.BlockSpec(shape, lambda i: (0,) * len(shape))
    in_specs = [
        pl.BlockSpec((None, MOD_ROWS, D_MODEL), lambda i: (_cond_block(i, TM), 0, 0)),
        full((1, D_MODEL)), full((1, D_MODEL)),
        full((N_EXPERTS, D_MODEL)), full((N_EXPERTS, D_MODEL)),
    ]
    out_shape = (
        jax.ShapeDtypeStruct((N_TOK, D_MODEL), F32),
        jax.ShapeDtypeStruct((N_TOK, D_MODEL), BF16),
        jax.ShapeDtypeStruct((N_EXPERTS, N_TOK), F32),
    )
    out_specs = (
        pl.BlockSpec((TM, D_MODEL), lambda i: (i, 0)),
        pl.BlockSpec((TM, D_MODEL), lambda i: (i, 0)),
        pl.BlockSpec((N_EXPERTS, TM), lambda i: (0, i)),
    )
    return in_specs, out_shape, out_specs


def _out_ab(yrg_p, yrg_s, o_p, o_s, w_a, w_b, xp, xs, mods, lng, lnb, wr_hi, wr_lo):
    full = lambda shape: pl.BlockSpec(shape, lambda i: (0,) * len(shape))
    post_in, out_shape, out_specs = _post_specs()
    return pl.pallas_call(
        _out_ab_kernel,
        out_shape=out_shape,
        grid=(N_TOK // TM,),
        in_specs=(_row_pair_specs(TM, RG_W) + _row_pair_specs(TM, MLA_HEADS * V_HEAD)
                  + [full((RG_W, D_MODEL)), full((MLA_HEADS * V_HEAD, D_MODEL))]
                  + _row_pair_specs(TM, D_MODEL) + post_in),
        out_specs=out_specs,
        compiler_params=_cparams(("arbitrary",)),
        name="out_ab",
    )(yrg_p, yrg_s, o_p, o_s, w_a, w_b, xp, xs, mods, lng, lnb, wr_hi, wr_lo)


def _proj_c_kernel(x_ref, mod_ref, cos_ref, sin_ref, wq_ref, wk_ref, wv_ref, wg_ref, q_ref, k_ref, v_ref, g_ref):
    h = _modulated(x_ref[...], mod_ref, 0, 1).astype(BF16)
    cos, sin = cos_ref[...], sin_ref[...]
    half = RET_DK // 2
    for hd in range(RET_HEADS):
        for w_ref, is_k in ((wq_ref, False), (wk_ref, True)):
            p = _dot(h, w_ref[:, hd * RET_DK:(hd + 1) * RET_DK])
            x1, x2 = p[:, :half], p[:, half:]
            r1 = x1 * cos - x2 * sin
            r2 = x2 * cos + x1 * sin
            if not is_k:
                q_ref[:, hd * RET_DK:hd * RET_DK + half] = r1.astype(BF16)
                q_ref[:, hd * RET_DK + half:(hd + 1) * RET_DK] = r2.astype(BF16)
            else:
                t1 = (r1 * RET_DK ** -0.5).T.astype(BF16)
                t2 = (r2 * RET_DK ** -0.5).T.astype(BF16)
                for c in range(k_ref.shape[0]):
                    k_ref[c, hd * RET_DK:hd * RET_DK + half, :] = t1[:, c * RET_CHUNK:(c + 1) * RET_CHUNK]
                    k_ref[c, hd * RET_DK + half:(hd + 1) * RET_DK, :] = t2[:, c * RET_CHUNK:(c + 1) * RET_CHUNK]
    step = 512
    for j in range(MIX_C // step):
        v_ref[:, j * step:(j + 1) * step] = _dot(h, wv_ref[:, j * step:(j + 1) * step]).astype(BF16)
        g_ref[:, j * step:(j + 1) * step] = _dot(h, wg_ref[:, j * step:(j + 1) * step])


def _proj_c(x, mods, cos_t, sin_t, wq, wk, wv, wg):
    full = lambda shape: pl.BlockSpec(shape, lambda i: (0,) * len(shape))
    qk = RET_HEADS * RET_DK
    tab = pl.BlockSpec((TM, RET_DK // 2), lambda i: (_pos_block(i, TM), 0))
    return pl.pallas_call(
        _proj_c_kernel,
        out_shape=(
            jax.ShapeDtypeStruct((N_TOK, qk), BF16), jax.ShapeDtypeStruct((N_TOK // RET_CHUNK, qk, RET_CHUNK), BF16),
            jax.ShapeDtypeStruct((N_TOK, MIX_C), BF16), jax.ShapeDtypeStruct((N_TOK, MIX_C), F32),
        ),
        grid=(N_TOK // TM,),
        in_specs=[
            pl.BlockSpec((TM, D_MODEL), lambda i: (i, 0)),
            pl.BlockSpec((None, MOD_ROWS, D_MODEL), lambda i: (_cond_block(i, TM), 0, 0)),
            tab, tab,
            full((D_MODEL, qk)), full((D_MODEL, qk)), full((D_MODEL, MIX_C)), full((D_MODEL, MIX_C)),
        ],
        out_specs=(
            pl.BlockSpec((TM, qk), lambda i: (i, 0)),
            pl.BlockSpec((TM // RET_CHUNK, qk, RET_CHUNK), lambda i: (i, 0, 0)),
            pl.BlockSpec((TM, MIX_C), lambda i: (i, 0)), pl.BlockSpec((TM, MIX_C), lambda i: (i, 0)),
        ),
        compiler_params=_cparams(("arbitrary",), 56),
        name="proj_c",
    )(x, mods, cos_t, sin_t, wq, wk, wv, wg)


def _retention_kernel(*refs, seq, with_state):
    if with_state:
        q_ref, kt_ref, v_ref, gam_ref, r0_ref, o_ref, rfin_ref, r_s = refs
    else:
        q_ref, kt_ref, v_ref, gam_ref, r0_ref, o_ref, r_s = refs
        rfin_ref = None
    c = RET_CHUNK
    n = seq // c
    ii = lax.broadcasted_iota(I32, (c, c), 0).astype(F32)
    jj = lax.broadcasted_iota(I32, (c, c), 1).astype(F32)
    ci = lax.broadcasted_iota(I32, (c, 1), 0).astype(F32)
    li = lax.broadcasted_iota(I32, (1, c), 1).astype(F32)

    consts = []
    for d in range(2):
        gam = gam_ref[d]
        lg_row = -_softplus(-gam[0:1, :])
        lg = jnp.broadcast_to(lg_row, (c, c))
        lg_col = jnp.broadcast_to(lg_row[:, 0:1], (c, 1))
        if d == 0:
            diff = ii - jj
            xi = jnp.exp((ci + 1.0) * lg_col)
            zeta = jnp.exp((c - 1.0 - li) * lg_row)
        else:
            diff = jj - ii
            xi = jnp.exp((c - ci) * lg_col)
            zeta = jnp.exp(li * lg_row)
        dmat = jnp.where(diff >= 0, jnp.exp(jnp.maximum(diff, 0.0) * lg), 0.0)
        g_chunk = jnp.exp(float(c) * lg_row[:, 0:1])
        consts.append((dmat, xi, zeta, g_chunk))
        r_s[d] = r0_ref[d]

    def chunk(d, idx, accumulate):
        dmat, xi, zeta, g_chunk = consts[d]
        t0 = pl.multiple_of(idx * c, c)
        qb = q_ref[pl.ds(t0, c), :]
        kt = kt_ref[idx]
        vb = v_ref[pl.ds(t0, c), :]
        r = r_s[d]
        inner = _dot(qb, kt) * dmat
        o = _dot(inner.astype(BF16), vb) + _dot((qb.astype(F32) * xi).astype(BF16), r.astype(BF16))
        r_s[d] = r * g_chunk + _dot((kt.astype(F32) * zeta).astype(BF16), vb)
        if accumulate:
            o_ref[pl.ds(t0, c), :] = o_ref[pl.ds(t0, c), :] + o
        else:
            o_ref[pl.ds(t0, c), :] = o

    def first_half(s, carry):
        chunk(0, s, False)
        chunk(1, n - 1 - s, False)
        return carry

    def second_half(s, carry):
        chunk(0, s, True)
        chunk(1, n - 1 - s, True)
        return carry

    lax.fori_loop(0, n // 2, first_half, 0)
    lax.fori_loop(n // 2, n, second_half, 0)
    if with_state:
        for d in range(2):
            rfin_ref[d] = r_s[d]


def _retention(q, k, v, gam, r0, *, n_seq, seq, row_block0, with_state):
    kern = functools.partial(_retention_kernel, seq=seq, with_state=with_state)
    out_shape = [jax.ShapeDtypeStruct((n_seq * seq, MIX_C), F32)]
    out_specs = [pl.BlockSpec((seq, RET_DV), lambda b, h: (b, h))]
    if with_state:
        out_shape.append(jax.ShapeDtypeStruct((n_seq, 2, RET_HEADS, RET_DK, RET_DV), F32))
        out_specs.append(pl.BlockSpec((None, 2, None, RET_DK, RET_DV), lambda b, h: (b, 0, h, 0, 0)))
    return pl.pallas_call(
        kern,
        out_shape=tuple(out_shape),
        grid=(n_seq, RET_HEADS),
        in_specs=[
            pl.BlockSpec((seq, RET_DK), lambda b, h: (row_block0 + b, h)),
            pl.BlockSpec((seq // RET_CHUNK, RET_DK, RET_CHUNK), lambda b, h: (row_block0 + b, h, 0)),
            pl.BlockSpec((seq, RET_DV), lambda b, h: (row_block0 + b, h)),
            pl.BlockSpec((2, None, SUBLANE, LANE), lambda b, h: (0, h, 0, 0)),
            pl.BlockSpec((None, 2, None, RET_DK, RET_DV), lambda b, h: (b, 0, h, 0, 0)),
        ],
        out_specs=tuple(out_specs),
        scratch_shapes=[pltpu.VMEM((2, RET_DK, RET_DV), F32)],
        compiler_params=_cparams(("arbitrary", "arbitrary"), 56),
        name=f"retention_s{seq}",
    )(q, k, v, gam, r0)


def _out_c_kernel(op_ref, os_ref, g_ref, w_ref, x_ref, mod_ref, lng_ref, lnb_ref, wrh_ref, wrl_ref,
                  x1_ref, h2_ref, lgt_ref):
    y = None
    o_all = _pick_rows(op_ref, os_ref, TM)
    for hd in range(RET_HEADS):
        o = o_all[:, hd * RET_DV:(hd + 1) * RET_DV]
        mu = jnp.mean(o, axis=-1, keepdims=True)
        oc = o - mu
        var = jnp.mean(oc * oc, axis=-1, keepdims=True)
        on = oc * lax.rsqrt(var + EPS)
        a = (on * _silu(g_ref[:, hd * RET_DV:(hd + 1) * RET_DV])).astype(BF16)
        part = _dot(a, w_ref[hd * RET_DV:(hd + 1) * RET_DV, :])
        y = part if y is None else y + part
    _post_mixer(y, x_ref[...], mod_ref, lng_ref, lnb_ref, wrh_ref, wrl_ref, x1_ref, h2_ref, lgt_ref)


def _out_c(o_p, o_s, g, w, x, mods, lng, lnb, wr_hi, wr_lo):
    full = lambda shape: pl.BlockSpec(shape, lambda i: (0,) * len(shape))
    post_in, out_shape, out_specs = _post_specs()
    return pl.pallas_call(
        _out_c_kernel,
        out_shape=out_shape,
        grid=(N_TOK // TM,),
        in_specs=_row_pair_specs(TM, MIX_C) + [
            pl.BlockSpec((TM, MIX_C), lambda i: (i, 0)),
            full((MIX_C, D_MODEL)),
            pl.BlockSpec((TM, D_MODEL), lambda i: (i, 0)),
        ] + post_in,
        out_specs=out_specs,
        compiler_params=_cparams(("arbitrary",), 56),
        name="out_c",
    )(o_p, o_s, g, w, x, mods, lng, lnb, wr_hi, wr_lo)


def _route_kernel(lgt_ref, bias_ref, tri_ref, w_ref, lpos_ref, p16_ref):
    tt = lgt_ref.shape[1]
    scores = jax.nn.sigmoid(lgt_ref[...])
    sel = scores + bias_ref[...]
    srow = lax.broadcasted_iota(I32, (GROUP_SIZE, tt), 0).astype(F32)
    ninf = -jnp.inf

    gs = []
    for g in range(N_GROUPS):
        sg = sel[g * GROUP_SIZE:(g + 1) * GROUP_SIZE, :]
        m1 = jnp.max(sg, axis=0, keepdims=True)
        i1 = jnp.min(jnp.where(sg == m1, srow, float(GROUP_SIZE)), axis=0, keepdims=True)
        m2 = jnp.max(jnp.where(srow == i1, ninf, sg), axis=0, keepdims=True)
        gs.append(m1 + m2)
    gs = jnp.concatenate(gs, axis=0)
    chosen = jnp.zeros((N_GROUPS, tt), F32)
    for _ in range(TOPK_GROUPS):
        mg = jnp.max(gs, axis=0, keepdims=True)
        gi = jnp.min(jnp.where(gs == mg, srow, float(N_GROUPS)), axis=0, keepdims=True)
        hit = srow == gi
        chosen = jnp.where(hit, 1.0, chosen)
        gs = jnp.where(hit, ninf, gs)
    sel = jnp.concatenate(
        [jnp.where(jnp.broadcast_to(chosen[g:g + 1, :], (GROUP_SIZE, tt)) > 0.5,
                   sel[g * GROUP_SIZE:(g + 1) * GROUP_SIZE, :], ninf) for g in range(N_GROUPS)], axis=0)

    erow = lax.broadcasted_iota(I32, (N_EXPERTS, tt), 0).astype(F32)
    ids, ws = [], []
    for _ in range(TOP_K):
        m = jnp.max(sel, axis=0, keepdims=True)
        ei = jnp.min(jnp.where(sel == m, erow, float(N_EXPERTS)), axis=0, keepdims=True)
        hit = erow == ei
        ids.append(ei)
        ws.append(jnp.sum(jnp.where(hit, scores, 0.0), axis=0, keepdims=True))
        sel = jnp.where(hit, ninf, sel)
    wsum = ws[0]
    for k in range(1, TOP_K):
        wsum = wsum + ws[k]
    w_ref[...] = jnp.concatenate([w / wsum * ROUTED_SCALE for w in ws], axis=0)

    member_f = jnp.zeros((N_EXPERTS, tt), F32)
    for k in range(TOP_K):
        member_f = jnp.where(erow == ids[k], 1.0, member_f)
    member_b = member_f.astype(BF16)
    cnt_row = _dot_nt(jnp.ones((SUBLANE, tt), BF16), member_b)[0:1, :]
    p16_row = jnp.ceil(cnt_row * (1.0 / PIECE)) * PIECE
    lane_e = lax.broadcasted_iota(I32, (N_EXPERTS, N_EXPERTS), 1)
    sub_e = lax.broadcasted_iota(I32, (N_EXPERTS, N_EXPERTS), 0)
    run_start = jnp.sum(jnp.where(lane_e < sub_e, jnp.broadcast_to(p16_row, (N_EXPERTS, N_EXPERTS)), 0.0),
                        axis=1, keepdims=True)
    rank = _dot(member_b, tri_ref[...]) + run_start
    lpos_ref[...] = jnp.concatenate(
        [jnp.sum(jnp.where(erow == ids[k], rank, 0.0), axis=0, keepdims=True) for k in range(TOP_K)],
        axis=0).astype(I32)
    p16_ref[pl.ds(pl.program_id(0), 1), :] = jnp.concatenate(
        [p16_row, jnp.zeros((1, LANE - N_EXPERTS), F32)], axis=1)


def _route(lgt, bias, tri):
    return pl.pallas_call(
        _route_kernel,
        out_shape=(
            jax.ShapeDtypeStruct((TOP_K, N_TOK), F32), jax.ShapeDtypeStruct((TOP_K, N_TOK), I32),
            jax.ShapeDtypeStruct((N_WIN, LANE), F32),
        ),
        grid=(N_WIN,),
        in_specs=[
            pl.BlockSpec((N_EXPERTS, WIN), lambda i: (0, i)),
            pl.BlockSpec((N_EXPERTS, 1), lambda i: (0, 0)),
            pl.BlockSpec((WIN, WIN), lambda i: (0, 0)),
        ],
        out_specs=(
            pl.BlockSpec((TOP_K, WIN), lambda i: (0, i)), pl.BlockSpec((TOP_K, WIN), lambda i: (0, i)),
            pl.BlockSpec((N_WIN, LANE), lambda i: (0, 0)),
        ),
        compiler_params=_cparams(("arbitrary",)),
        name="moe_route",
    )(lgt, bias, tri)


def _sort_kernel(lpos_ref, h_ref, xl_ref, oh_all):
    lp = jnp.where(pl.program_id(0) < N_WIN, lpos_ref[...], -1)
    riota = lax.broadcasted_iota(I32, (SORT_ROWS, WIN), 0).astype(jnp.int16)
    one = jnp.ones((SORT_ROWS, WIN), BF16)
    for j in range(RL // SORT_ROWS):
        rel = (lp - j * SORT_ROWS).astype(jnp.int16)
        oh = jnp.zeros((SORT_ROWS, WIN), BF16)
        for k in range(TOP_K):
            oh = jnp.where(rel[k:k + 1, :] == riota, one, oh)
        oh_all[j * SORT_ROWS:(j + 1) * SORT_ROWS, :] = oh
    x = h_ref[...]
    for nt in range(D_MODEL // SORT_ROWS):
        cols = slice(nt * SORT_ROWS, (nt + 1) * SORT_ROWS)
        xl_ref[:, cols] = _dot(oh_all[...], x[:, cols]).astype(BF16)


def _sort_rows(lpos, h2):
    last = N_WIN - 1
    return pl.pallas_call(
        _sort_kernel,
        out_shape=jax.ShapeDtypeStruct(((N_WIN + 1) * RL, D_MODEL), BF16),
        grid=(N_WIN + 1,),
        in_specs=[
            pl.BlockSpec((TOP_K, WIN), lambda i: (0, jnp.minimum(i, last))),
            pl.BlockSpec((WIN, D_MODEL), lambda i: (jnp.minimum(i, last), 0)),
        ],
        out_specs=pl.BlockSpec((RL, D_MODEL), lambda i: (i, 0)),
        scratch_shapes=[pltpu.VMEM((RL, WIN), BF16)],
        compiler_params=_cparams(("arbitrary",)),
        name="moe_sort",
    )(lpos, h2)


def _moe_tables(p16):
    n_w = jnp.arange(N_WIN, dtype=I32)
    run_start = jnp.cumsum(p16, axis=1) - p16
    rw = jnp.sum(p16, axis=1)
    cum_w = jnp.cumsum(p16, axis=0) - p16
    tot = jnp.sum(p16, axis=0)
    nblk = (tot + TME - 1) // TME
    blk_end = jnp.cumsum(nblk)
    blk0 = blk_end - nblk
    n_used = blk_end[-1]
    b = jnp.arange(N_EBLOCKS + 1, dtype=I32)
    block_e = jnp.minimum(jnp.sum((blk_end[None, :] <= b[:, None]).astype(I32), axis=1), N_EXPERTS - 1)
    piece = jnp.arange(PIECES, dtype=I32) * PIECE
    rp = (b - blk0[block_e])[:, None] * TME + piece[None, :]
    valid = (b[:, None] < n_used) & (rp < tot[block_e][:, None])
    cum_e = cum_w.T[block_e]
    len_e = p16.T[block_e]
    start_e = run_start.T[block_e]
    in_win = (cum_e[:, None, :] <= rp[:, :, None]) & (rp[:, :, None] < (cum_e + len_e)[:, None, :])
    row = n_w[None, None, :] * RL + start_e[:, None, :] + rp[:, :, None] - cum_e[:, None, :]
    row = jnp.sum(jnp.where(in_win, row, 0), axis=2)
    zero_src = N_WIN * RL
    assert 3 * TME <= RL
    trash = N_WIN * RL + (1 + b % 2)[:, None] * TME + piece[None, :]
    gather_row = jnp.where(valid, row, zero_src).reshape(-1).astype(I32)
    scatter_row = jnp.where(valid, row, trash).reshape(-1).astype(I32)
    rw = jnp.concatenate([rw, jnp.zeros((1,), rw.dtype)])
    return rw.astype(I32), block_e.astype(I32), n_used.astype(I32).reshape(1), gather_row, scatter_row


def _expert_kernel(be_ref, nb_ref, grow_ref, srow_ref, xl_hbm, wg_ref, wu_ref, wd_ref, yl_hbm,
                   xbuf, ybuf, wg_b, wu_b, wd_b, gsem, ssem):
    b = pl.program_id(0)
    nb = nb_ref[0]

    @pl.when(jnp.logical_and(b < nb, jnp.logical_or(b == 0, be_ref[b] != be_ref[jnp.maximum(b - 1, 0)])))
    def _():
        wg_b[...] = wg_ref[...].astype(BF16)
        wu_b[...] = wu_ref[...].astype(BF16)
        wd_b[...] = wd_ref[...].astype(BF16)

    def gather_start(blk, slot):
        for p in range(PIECES):
            src = pl.multiple_of(grow_ref[blk * PIECES + p], PIECE)
            pltpu.make_async_copy(xl_hbm.at[pl.ds(src, PIECE)], xbuf.at[slot, pl.ds(p * PIECE, PIECE)],
                                  gsem.at[slot]).start()

    def gather_wait(slot):
        pltpu.make_async_copy(xl_hbm.at[pl.ds(0, TME)], xbuf.at[slot], gsem.at[slot]).wait()

    def scatter_wait(slot):
        pltpu.make_async_copy(ybuf.at[slot], yl_hbm.at[pl.ds(0, TME)], ssem.at[slot]).wait()

    @pl.when(b < nb)
    def _():
        slot = b % 2

        @pl.when(b == 0)
        def _():
            gather_start(0, 0)

        @pl.when(b + 1 < nb)
        def _():
            gather_start(b + 1, 1 - slot)

        gather_wait(slot)

        @pl.when(b >= 2)
        def _():
            scatter_wait(slot)

        x = xbuf[slot]
        hb = _silu(_dot(x, wg_b[...])) * _dot(x, wu_b[...])
        ybuf[slot] = _dot(hb.astype(BF16), wd_b[...]).astype(BF16)
        for p in range(PIECES):
            dst = pl.multiple_of(srow_ref[b * PIECES + p], PIECE)
            pltpu.make_async_copy(ybuf.at[slot, pl.ds(p * PIECE, PIECE)], yl_hbm.at[pl.ds(dst, PIECE)],
                                  ssem.at[slot]).start()

        @pl.when(b == nb - 1)
        def _():
            scatter_wait(slot)

            @pl.when(b >= 1)
            def _():
                scatter_wait(1 - slot)


def _experts(block_e, n_used, gather_row, scatter_row, xl, wg, wu, wd, layer):
    def w_map(i, be, nb, gr, sr):
        return (layer, be[jnp.minimum(i, nb[0] - 1)], 0, 0)

    return pl.pallas_call(
        _expert_kernel,
        out_shape=jax.ShapeDtypeStruct(((N_WIN + 1) * RL, D_MODEL), BF16),
        input_output_aliases={4: 0},
        grid_spec=pltpu.PrefetchScalarGridSpec(
            num_scalar_prefetch=4,
            grid=(N_EBLOCKS,),
            in_specs=[
                pl.BlockSpec(memory_space=pl.ANY),
                pl.BlockSpec((None, None, D_MODEL, D_EXPERT), w_map),
                pl.BlockSpec((None, None, D_MODEL, D_EXPERT), w_map),
                pl.BlockSpec((None, None, D_EXPERT, D_MODEL), w_map),
            ],
            out_specs=pl.BlockSpec(memory_space=pl.ANY),
            scratch_shapes=[
                pltpu.VMEM((2, TME, D_MODEL), BF16), pltpu.VMEM((2, TME, D_MODEL), BF16),
                pltpu.VMEM((D_MODEL, D_EXPERT), BF16), pltpu.VMEM((D_MODEL, D_EXPERT), BF16),
                pltpu.VMEM((D_EXPERT, D_MODEL), BF16),
                pltpu.SemaphoreType.DMA((2,)), pltpu.SemaphoreType.DMA((2,)),
            ],
        ),
        compiler_params=_cparams(("arbitrary",)),
        name="moe_experts",
    )(block_e, n_used, gather_row, scatter_row, xl, wg, wu, wd)


def _combine_kernel(rw_ref, yl_ref, lpt_ref, wt_ref, h_ref, x1_ref, mod_ref, lng_ref, lnb_ref,
                    wsg_ref, wsu_ref, wsd_ref, *rest, split):
    del rw_ref
    if split:
        outp_ref, outs_ref, p_w, lp_b, wt_b = rest
    else:
        out_ref, p_w, lp_b, wt_b = rest
    hb = h_ref[...]
    shared = _dot((_silu(_dot(hb, wsg_ref[...])) * _dot(hb, wsu_ref[...])).astype(BF16), wsd_ref[...])
    lp = lpt_ref[...]
    wt = wt_ref[...]
    for k in range(TOP_K):
        lp_b[k] = jnp.broadcast_to(lp[:, k:k + 1], (WIN, SORT_ROWS)).astype(jnp.int16)
        wt_b[k] = jnp.broadcast_to(wt[:, k:k + 1], (WIN, SORT_ROWS)).astype(BF16)
    ciota = lax.broadcasted_iota(I32, (WIN, SORT_ROWS), 1)
    for j in range(RL // SORT_ROWS):
        col = (ciota + j * SORT_ROWS).astype(jnp.int16)
        pm = jnp.zeros((WIN, SORT_ROWS), BF16)
        for k in range(TOP_K):
            pm = jnp.where(lp_b[k] == col, wt_b[k], pm)
        p_w[:, j * SORT_ROWS:(j + 1) * SORT_ROWS] = pm
    routed = _dot(p_w[...], yl_ref[...])
    z = ALPHA * x1_ref[...] + mod_ref[5:6, :] * (routed + shared)
    out = _layernorm_rows(z, lng_ref[...], lnb_ref[...])
    if split:
        @pl.when(pl.program_id(0) < N_PROMPT // WIN)
        def _():
            outp_ref[...] = out

        @pl.when(pl.program_id(0) >= N_PROMPT // WIN)
        def _():
            outs_ref[...] = out
    else:
        out_ref[...] = out


def _combine(rw, yl, lpos_t, wt, h2, x1, mods, lng, lnb, wsg, wsu, wsd, *, split):
    full = lambda shape: pl.BlockSpec(shape, lambda i, rw: (0,) * len(shape))
    if split:
        out_shape = (jax.ShapeDtypeStruct((N_PROMPT, D_MODEL), F32), jax.ShapeDtypeStruct((N_SAMPLE, D_MODEL), F32))
        out_specs = tuple(_row_pair_specs(WIN, D_MODEL))
    else:
        out_shape = jax.ShapeDtypeStruct((N_TOK, D_MODEL), F32)
        out_specs = pl.BlockSpec((WIN, D_MODEL), lambda i, rw: (i, 0))
    return pl.pallas_call(
        functools.partial(_combine_kernel, split=split),
        out_shape=out_shape,
        grid_spec=pltpu.PrefetchScalarGridSpec(
            num_scalar_prefetch=1,
            grid=(N_WIN,),
            in_specs=[
                pl.BlockSpec((RL, D_MODEL), lambda i, rw: (i, 0)),
                pl.BlockSpec((WIN, TOP_K), lambda i, rw: (i, 0)),
                pl.BlockSpec((WIN, TOP_K), lambda i, rw: (i, 0)),
                pl.BlockSpec((WIN, D_MODEL), lambda i, rw: (i, 0)),
                pl.BlockSpec((WIN, D_MODEL), lambda i, rw: (i, 0)),
                pl.BlockSpec((None, MOD_ROWS, D_MODEL), lambda i, rw: (_cond_block(i, WIN), 0, 0)),
                full((1, D_MODEL)), full((1, D_MODEL)),
                full((D_MODEL, D_EXPERT)), full((D_MODEL, D_EXPERT)), full((D_EXPERT, D_MODEL)),
            ],
            out_specs=out_specs,
            scratch_shapes=[pltpu.VMEM((WIN, RL), BF16),
                            pltpu.VMEM((TOP_K, WIN, SORT_ROWS), jnp.int16),
                            pltpu.VMEM((TOP_K, WIN, SORT_ROWS), BF16)],
        ),
        compiler_params=_cparams(("arbitrary",)),
        name="moe_combine",
    )(rw, yl, lpos_t, wt, h2, x1, mods, lng, lnb, wsg, wsu, wsd)


def _moe_and_norm(x1, h2, lgt, mods, lng, lnb, router_bias, tri, wg, wu, wd, wsg, wsu, wsd, *, layer, split):
    wts, lpos, p16 = _route(lgt, router_bias.reshape(N_EXPERTS, 1), tri)
    rw, block_e, n_used, gather_row, scatter_row = _moe_tables(p16[:, :N_EXPERTS].astype(I32))
    xl = _sort_rows(lpos, h2)
    yl = _experts(block_e, n_used, gather_row, scatter_row, xl, wg, wu, wd, layer)
    return _combine(rw, yl, lpos.T, wts.T, h2, x1, mods, lng, lnb, wsg, wsu, wsd, split=split)


def _rope_tables_mla():
    t = jnp.arange(DEC_SEQ)
    row = (t // GRID_W).astype(F32)
    col = (t % GRID_W).astype(F32)
    n = QK_ROPE // 4
    inv = ROPE_BASE ** (-jnp.arange(n, dtype=F32) / n)
    ang_r = row[:, None] * inv
    ang_c = col[:, None] * inv
    cos = jnp.ones((DEC_SEQ, LANE), F32)
    sin_m = jnp.zeros((DEC_SEQ, LANE), F32)
    sin_p = jnp.zeros((DEC_SEQ, LANE), F32)
    l0 = ROPE_LANE0
    for base, ang in ((l0, ang_r), (l0 + 2 * n, ang_c)):
        c, s = jnp.cos(ang), jnp.sin(ang)
        cos = cos.at[:, base:base + n].set(c).at[:, base + n:base + 2 * n].set(c)
        sin_m = sin_m.at[:, base:base + n].set(-s)
        sin_p = sin_p.at[:, base + n:base + 2 * n].set(s)
    ident = (jnp.ones((TM, LANE), F32), jnp.zeros((TM, LANE), F32), jnp.zeros((TM, LANE), F32))
    return tuple(jnp.concatenate([i, tbl], axis=0) for i, tbl in zip(ident, (cos, sin_m, sin_p)))


def _rope_tables_ret():
    half = RET_DK // 2
    theta = ROPE_BASE ** (-jnp.linspace(0.0, 1.0, half, dtype=F32))
    ang = jnp.arange(DEC_SEQ, dtype=F32)[:, None] * theta
    cos = jnp.concatenate([jnp.ones((TM, half), F32), jnp.cos(ang)], axis=0)
    sin = jnp.concatenate([jnp.zeros((TM, half), F32), jnp.sin(ang)], axis=0)
    return cos, sin


def _pad_heads(w, width, lane0=0):
    k = w.shape[0]
    w = w.reshape(k, MLA_HEADS, width)
    out = jnp.zeros((k, MLA_HEADS, HEAD_PAD), w.dtype).at[:, :, lane0:lane0 + width].set(w)
    return out.reshape(k, MLA_HEADS * HEAD_PAD)


def _rg_gate_weights(wa, ba, wx, bx):
    n_ct = RG_W // LANE
    per = LANE // RG_BW
    tiles_w, tiles_b = [], []
    for c in range(n_ct):
        cols_w, cols_b = [], []
        for d in range(2):
            for w, b in ((wa, ba), (wx, bx)):
                m = jnp.zeros((LANE, LANE), F32)
                for p in range(per):
                    m = m.at[p * RG_BW:(p + 1) * RG_BW, p * RG_BW:(p + 1) * RG_BW].set(w[d, c * per + p])
                cols_w.append(m)
                cols_b.append(b[d, c * LANE:(c + 1) * LANE])
        tiles_w.append(jnp.concatenate(cols_w, axis=1))
        tiles_b.append(jnp.concatenate(cols_b, axis=0)[None, :])
    return jnp.stack(tiles_w).astype(BF16), jnp.stack(tiles_b)


def kernel(x_prompt, x_sample, cache_mla_ckv, cache_mla_krope, state_rglru, state_ret, c, c_ctx, w_ada, b_ada,
           ln_g, ln_b, w_in_ab, rg_conv_w, rg_conv_b, rg_wa, rg_ba, rg_wx, rg_bx, rg_lambda, mla_q_norm, mla_w_uq,
           mla_kv_norm, mla_w_ukv, w_out_ab, w_in_c, ret_gamma_logit, w_out_c, w_router, router_bias,
           w_exp_gate, w_exp_up, w_exp_down, w_sh_gate, w_sh_up, w_sh_down):
    xp = x_prompt.reshape(N_PROMPT, D_MODEL)
    xs = x_sample.reshape(N_SAMPLE, D_MODEL)
    cond = jnp.zeros((16, D_MODEL), F32).at[0].set(c_ctx).at[1:1 + DEC_BATCH].set(c)
    mods_all = _ada_modulation(cond, w_ada, b_ada).reshape(DEPTH, 16, 6, D_MODEL)[:, :N_COND]
    mods_all = jnp.pad(mods_all, ((0, 0), (0, 0), (0, MOD_ROWS - 6), (0, 0)))

    tri = (jnp.arange(WIN)[:, None] < jnp.arange(WIN)[None, :]).astype(BF16)
    wr_t = jnp.swapaxes(w_router, 1, 2)
    wr_hi = wr_t.astype(BF16)
    wr_lo = (wr_t - wr_hi.astype(F32)).astype(BF16)
    wg_e, wu_e, wd_e = w_exp_gate, w_exp_up, w_exp_down
    wsg, wsu, wsd = w_sh_gate.astype(BF16), w_sh_up.astype(BF16), w_sh_down.astype(BF16)

    l, e = 0, 0
    mods = mods_all[l]
    n_main = 2 * RG_W + Q_LORA + KV_LORA
    w_main = w_in_ab[e][:, :n_main].astype(BF16)
    w_kr = jnp.zeros((D_MODEL, LANE), F32).at[:, ROPE_LANE0:ROPE_LANE0 + QK_ROPE].set(w_in_ab[e][:, n_main:]).astype(BF16)
    main, krp = _proj_ab(xp, xs, mods, w_main, w_kr)

    wg_rg, bg_rg = _rg_gate_weights(rg_wa[e], rg_ba[e], rg_wx[e], rg_bx[e])
    h0_p = jnp.zeros((BATCH, 2, RG_W), F32)
    rg_args = (rg_conv_w[e], rg_conv_b[e].reshape(1, RG_W), wg_rg, bg_rg, rg_lambda[e])
    yrg_p, rg_fin = _rglru(main, *rg_args, h0_p, n_seq=BATCH, seq=SEQ, row_block0=0)
    yrg_s, _ = _rglru(main, *rg_args, state_rglru[:, e],
                      n_seq=DEC_BATCH, seq=DEC_SEQ, row_block0=N_PROMPT // DEC_SEQ)

    cos_t, sm_t, sp_t = _rope_tables_mla()
    w_uq = mla_w_uq[e].reshape(Q_LORA, MLA_HEADS, QK_NOPE + QK_ROPE)
    wq_p = _pad_heads(w_uq.reshape(Q_LORA, -1), QK_NOPE + QK_ROPE).astype(BF16)
    w_ukv = mla_w_ukv[e].reshape(KV_LORA, MLA_HEADS, QK_NOPE + V_HEAD)
    wuk_p = _pad_heads(w_ukv[:, :, :QK_NOPE].reshape(KV_LORA, -1), QK_NOPE).astype(BF16)
    wuvt = w_ukv[:, :, QK_NOPE:].reshape(KV_LORA, MLA_HEADS * V_HEAD).T.astype(BF16)
    q_att, k_att, v_att, ckv_n = _mla_prep(main, krp, cos_t, sm_t, sp_t, mla_q_norm[e].reshape(1, Q_LORA), wq_p,
                                           mla_kv_norm[e].reshape(1, KV_LORA), wuk_p, wuvt)
    ctx_ckv = cache_mla_ckv[:, e].reshape(DEC_BATCH * PAST_LEN, KV_LORA)
    ctx_krp = jnp.zeros((DEC_BATCH * PAST_LEN, LANE), F32).at[:, ROPE_LANE0:ROPE_LANE0 + QK_ROPE].set(
        cache_mla_krope[:, e].reshape(DEC_BATCH * PAST_LEN, QK_ROPE))
    kc_att, vc_att = _mla_ctx(ctx_ckv, ctx_krp, wuk_p, wuvt)

    o_att_p = _attention(q_att, k_att, v_att, None, None, n_seq=BATCH, seq=SEQ, row_block0=0, tq=SEQ)
    o_att_s = _attention(q_att, k_att, v_att, kc_att, vc_att,
                         n_seq=DEC_BATCH, seq=DEC_SEQ, row_block0=N_PROMPT // DEC_SEQ, tq=TQ)

    w_out = w_out_ab[e].astype(BF16)
    x1, h2, lgt = _out_ab(yrg_p, yrg_s, o_att_p, o_att_s, w_out[:RG_W], w_out[RG_W:], xp, xs, mods,
                          ln_g[l, 0].reshape(1, D_MODEL), ln_b[l, 0].reshape(1, D_MODEL), wr_hi[l], wr_lo[l])
    x = _moe_and_norm(x1, h2, lgt, mods, ln_g[l, 1].reshape(1, D_MODEL), ln_b[l, 1].reshape(1, D_MODEL),
                      router_bias[l], tri, wg_e, wu_e, wd_e, wsg[l], wsu[l], wsd[l], layer=l, split=False)

    new_ckv = ckv_n.reshape(BATCH, 1, SEQ, KV_LORA)
    new_krope = krp[:N_PROMPT, ROPE_LANE0:ROPE_LANE0 + QK_ROPE].reshape(BATCH, 1, SEQ, QK_ROPE)
    new_rg = rg_fin.reshape(BATCH, 1, 2, RG_W)

    l, o = 1, 0
    mods = mods_all[l]
    qk = RET_HEADS * RET_DK
    w_c = w_in_c[o].astype(BF16)
    cos_r, sin_r = _rope_tables_ret()
    q_r, k_r, v_r, g_r = _proj_c(x, mods, cos_r, sin_r, w_c[:, :qk], w_c[:, qk:2 * qk],
                                 w_c[:, 2 * qk:2 * qk + MIX_C], w_c[:, 2 * qk + MIX_C:])
    gam = jnp.broadcast_to(ret_gamma_logit[o].astype(F32)[:, :, None, None], (2, RET_HEADS, SUBLANE, LANE))
    r0_p = jnp.zeros((BATCH, 2, RET_HEADS, RET_DK, RET_DV), F32)
    o_ret_p, r_fin = _retention(q_r, k_r, v_r, gam, r0_p, n_seq=BATCH, seq=SEQ, row_block0=0, with_state=True)
    (o_ret_s,) = _retention(q_r, k_r, v_r, gam, state_ret[:, o],
                            n_seq=DEC_BATCH, seq=DEC_SEQ, row_block0=N_PROMPT // DEC_SEQ, with_state=False)
    x1, h2, lgt = _out_c(o_ret_p, o_ret_s, g_r, w_out_c[o].astype(BF16), x, mods,
                         ln_g[l, 0].reshape(1, D_MODEL), ln_b[l, 0].reshape(1, D_MODEL), wr_hi[l], wr_lo[l])
    y_p, y_s = _moe_and_norm(x1, h2, lgt, mods, ln_g[l, 1].reshape(1, D_MODEL), ln_b[l, 1].reshape(1, D_MODEL),
                             router_bias[l], tri, wg_e, wu_e, wd_e, wsg[l], wsu[l], wsd[l], layer=l, split=True)

    y_prompt = y_p.reshape(BATCH, SEQ, D_MODEL)
    y_sample = y_s.reshape(DEC_BATCH, DEC_SEQ, D_MODEL)
    new_ret = r_fin.reshape(BATCH, 1, 2, RET_HEADS, RET_DK, RET_DV)
    return (y_prompt, y_sample, new_ckv, new_krope, new_rg, new_ret)
```

```python
import functools
import math

import jax
import jax.numpy as jnp
from jax import lax
from jax.experimental import pallas as pl
from jax.experimental.pallas import tpu as pltpu

F32 = jnp.float32
BF16 = jnp.bfloat16
I32 = jnp.int32

D_MODEL = 1024
BATCH, SEQ = 16, 256
DEC_BATCH, DEC_SEQ = 8, 4096
PAST_LEN = 256
DEPTH = 2
GRID_W = 64
RG_W, RG_BLOCKS = 512, 8
RG_BW = RG_W // RG_BLOCKS
RG_C = 8.0
CONV_W, CONV_LEFT = 4, 2
MLA_HEADS, QK_NOPE, QK_ROPE, V_HEAD = 8, 64, 32, 64
Q_LORA, KV_LORA = 768, 256
ROPE_BASE = 10000.0
ATTN_SCALE = (QK_NOPE + QK_ROPE) ** -0.5
RET_HEADS, RET_DK, RET_DV, RET_CHUNK = 4, 256, 512, 128
MIX_C = RET_HEADS * RET_DV
N_EXPERTS, TOP_K, N_GROUPS, TOPK_GROUPS = 64, 8, 8, 4
GROUP_SIZE = N_EXPERTS // N_GROUPS
D_EXPERT = 256
ROUTED_SCALE = 2.5
ALPHA = (2 * DEPTH) ** 0.25
EPS = 1e-6

N_PROMPT = BATCH * SEQ
N_SAMPLE = DEC_BATCH * DEC_SEQ
N_TOK = N_PROMPT + N_SAMPLE
N_COND = 1 + DEC_BATCH
MOD_ROWS = 8

LANE = 128
SUBLANE = 8
TM = 512
HEAD_PAD = 128
ROPE_LANE0 = QK_NOPE
TQ = 256
KC = 256
AH = 2
ATTN_UNROLL = True
V_ONES = 16
Q_PRESCALE = ATTN_SCALE * math.log2(math.e)
SCAN_ROWS = 64
GATE_ROWS = 256
WIN = 256
N_WIN = N_TOK // WIN
PIECE = 16
SORT_ROWS = 256
RL = 3072
TME = 1024
PIECES = TME // PIECE
N_PAIRS = N_TOK * TOP_K
N_EBLOCKS = (N_PAIRS + N_WIN * N_EXPERTS * (PIECE - 1)) // TME + N_EXPERTS
NEG = -1e30


def _cparams(sem, vmem_mb=48):
    return pltpu.CompilerParams(dimension_semantics=sem, vmem_limit_bytes=vmem_mb * 1024 * 1024)


def _cond_block(i, tm):
    npb = N_PROMPT // tm
    return jnp.where(i < npb, 0, 1 + (i - npb) // (DEC_SEQ // tm))


def _pos_block(i, tm):
    npb = N_PROMPT // tm
    return jnp.where(i < npb, 0, 1 + (i - npb) % (DEC_SEQ // tm))


def _split_hi_lo(a):
    hi = a.astype(BF16)
    lo = (a - hi.astype(F32)).astype(BF16)
    return hi, lo


def _dot(a, b):
    return jnp.dot(a, b, preferred_element_type=F32)


def _dot_nt(a, b):
    return lax.dot_general(a, b, (((1,), (1,)), ((), ())), preferred_element_type=F32)


def _silu(x):
    return x * jax.nn.sigmoid(x)


def _gelu_tanh(x):
    return 0.5 * x * (1.0 + jnp.tanh(math.sqrt(2.0 / math.pi) * (x + 0.044715 * (x * x * x))))


def _softplus(x):
    return jnp.maximum(x, 0.0) + jnp.log1p(jnp.exp(-jnp.abs(x)))


def _layernorm_rows(z, g, b):
    mu = jnp.mean(z, axis=-1, keepdims=True)
    zc = z - mu
    var = jnp.mean(zc * zc, axis=-1, keepdims=True)
    return (zc * lax.rsqrt(var + EPS)) * g + b


def _ada_kernel(c_ref, w_ref, b_ref, o_ref):
    s_hi, s_lo = _split_hi_lo(_silu(c_ref[...]))
    w_hi, w_lo = _split_hi_lo(w_ref[...])
    o_ref[...] = _dot(s_hi, w_hi) + _dot(s_hi, w_lo) + _dot(s_lo, w_hi) + b_ref[...]


def _ada_modulation(cond, w_ada, b_ada):
    n6 = 6 * D_MODEL
    tn = D_MODEL
    return pl.pallas_call(
        _ada_kernel,
        out_shape=jax.ShapeDtypeStruct((DEPTH, 16, n6), F32),
        grid=(DEPTH, n6 // tn),
        in_specs=[
            pl.BlockSpec((16, D_MODEL), lambda l, j: (0, 0)),
            pl.BlockSpec((None, D_MODEL, tn), lambda l, j: (l, 0, j)),
            pl.BlockSpec((None, 1, tn), lambda l, j: (l, 0, j)),
        ],
        out_specs=pl.BlockSpec((None, 16, tn), lambda l, j: (l, 0, j)),
        compiler_params=_cparams(("arbitrary", "arbitrary")),
        name="ada_modulation",
    )(cond, w_ada, b_ada.reshape(DEPTH, 1, n6))


def _modulated(x, mod_ref, shift_row, scale_row):
    return x * (1.0 + mod_ref[scale_row:scale_row + 1, :]) + mod_ref[shift_row:shift_row + 1, :]


def _row_pair_specs(tm, width, col=0):
    npb = N_PROMPT // tm
    return [pl.BlockSpec((tm, width), lambda i, *_: (jnp.minimum(i, npb - 1), col)),
            pl.BlockSpec((tm, width), lambda i, *_: (jnp.maximum(i - npb, 0), col))]


def _pick_rows(p_ref, s_ref, tm):
    return jnp.where(pl.program_id(0) < N_PROMPT // tm, p_ref[...], s_ref[...])


def _proj_ab_kernel(xp_ref, xs_ref, mod_ref, w_ref, wkr_ref, main_ref, kr_ref):
    h = _modulated(_pick_rows(xp_ref, xs_ref, TM), mod_ref, 0, 1).astype(BF16)
    n = w_ref.shape[1]
    step = 512
    for j in range(n // step):
        main_ref[:, j * step:(j + 1) * step] = _dot(h, w_ref[:, j * step:(j + 1) * step])
    kr_ref[...] = _dot(h, wkr_ref[...])


def _proj_ab(xp, xs, mods, w_main, w_kr):
    n = w_main.shape[1]
    return pl.pallas_call(
        _proj_ab_kernel,
        out_shape=(jax.ShapeDtypeStruct((N_TOK, n), F32), jax.ShapeDtypeStruct((N_TOK, LANE), F32)),
        grid=(N_TOK // TM,),
        in_specs=_row_pair_specs(TM, D_MODEL) + [
            pl.BlockSpec((None, MOD_ROWS, D_MODEL), lambda i: (_cond_block(i, TM), 0, 0)),
            pl.BlockSpec((D_MODEL, n), lambda i: (0, 0)),
            pl.BlockSpec((D_MODEL, LANE), lambda i: (0, 0)),
        ],
        out_specs=(pl.BlockSpec((TM, n), lambda i: (i, 0)), pl.BlockSpec((TM, LANE), lambda i: (i, 0))),
        compiler_params=_cparams(("arbitrary",)),
        name="proj_ab",
    )(xp, xs, mods, w_main, w_kr)


def _rglru_kernel(xr_ref, gr_ref, cw_ref, cb_ref, wg_ref, bg_ref, lam_ref, h0_ref,
                  y_ref, hfin_ref, xpad, a_s, b_s, *, seq):
    pad = SUBLANE
    xpad[0:pad, :] = jnp.zeros((pad, LANE), F32)
    xpad[seq + pad:seq + 2 * pad, :] = jnp.zeros((pad, LANE), F32)
    xpad[pad:seq + pad, :] = xr_ref[...]

    sp = _softplus(-lam_ref[...])
    cw = cw_ref[...]
    cb = cb_ref[...]
    wg = wg_ref[...]
    bg = bg_ref[...]

    def gate_step(c, carry):
        t0 = pl.multiple_of(c * GATE_ROWS, GATE_ROWS)
        win = xpad[pl.ds(t0, GATE_ROWS + 2 * pad), :]
        xc = cb
        for j in range(CONV_W):
            off = pad - CONV_LEFT + j
            xc = xc + win[off:off + GATE_ROWS, :] * cw[j:j + 1, :]
        g = _dot(xc.astype(BF16), wg) + bg
        for d in range(2):
            r = jax.nn.sigmoid(g[:, (2 * d) * LANE:(2 * d + 1) * LANE])
            i = jax.nn.sigmoid(g[:, (2 * d + 1) * LANE:(2 * d + 2) * LANE])
            log_a = (-RG_C * r) * sp[d:d + 1, :]
            a = jnp.exp(log_a)
            t = jnp.tanh(log_a)
            bt = jnp.sqrt(2.0 * t / (t - 1.0)) * (i * xc)
            a_s[d, pl.ds(t0, GATE_ROWS), :] = a
            b_s[d, pl.ds(t0, GATE_ROWS), :] = bt
        return carry

    lax.fori_loop(0, seq // GATE_ROWS, gate_step, 0)

    row = lax.broadcasted_iota(I32, (SCAN_ROWS, LANE), 0) % SUBLANE
    n_steps = seq // SCAN_ROWS
    tiles = SCAN_ROWS // SUBLANE

    def local_scan(a, b, reverse):
        for k in (1, 2, 4):
            if reverse:
                ok = row < SUBLANE - k
                shift = SCAN_ROWS - k
            else:
                ok = row >= k
                shift = k
            a_sh = jnp.where(ok, pltpu.roll(a, shift, 0), 1.0)
            b_sh = jnp.where(ok, pltpu.roll(b, shift, 0), 0.0)
            b = a * b_sh + b
            a = a * a_sh
        return a, b

    def fwd_step(c, h):
        t0 = pl.multiple_of(c * SCAN_ROWS, SCAN_ROWS)
        a, b = local_scan(a_s[0, pl.ds(t0, SCAN_ROWS), :], b_s[0, pl.ds(t0, SCAN_ROWS), :], False)
        outs = []
        for j in range(tiles):
            hj = a[j * SUBLANE:(j + 1) * SUBLANE, :] * h + b[j * SUBLANE:(j + 1) * SUBLANE, :]
            outs.append(hj)
            h = hj[SUBLANE - 1:SUBLANE, :]
        y_ref[pl.ds(t0, SCAN_ROWS), :] = jnp.concatenate(outs, axis=0)
        return h

    h_f = lax.fori_loop(0, n_steps, fwd_step, h0_ref[0:1, :])

    def bwd_step(c, h):
        t0 = pl.multiple_of((n_steps - 1 - c) * SCAN_ROWS, SCAN_ROWS)
        a, b = local_scan(a_s[1, pl.ds(t0, SCAN_ROWS), :], b_s[1, pl.ds(t0, SCAN_ROWS), :], True)
        outs = [None] * tiles
        for j in reversed(range(tiles)):
            hj = a[j * SUBLANE:(j + 1) * SUBLANE, :] * h + b[j * SUBLANE:(j + 1) * SUBLANE, :]
            outs[j] = hj
            h = hj[0:1, :]
        hb = jnp.concatenate(outs, axis=0)
        y_ref[pl.ds(t0, SCAN_ROWS), :] = (y_ref[pl.ds(t0, SCAN_ROWS), :] + hb) * _gelu_tanh(gr_ref[pl.ds(t0, SCAN_ROWS), :])
        return h

    h_b = lax.fori_loop(0, n_steps, bwd_step, h0_ref[1:2, :])
    hfin_ref[0:1, :] = h_f
    hfin_ref[1:2, :] = h_b


def _rglru(main, cw, cb, wg, bg, lam, h0, *, n_seq, seq, row_block0):
    n_ct = RG_W // LANE
    gr_col0 = RG_W // LANE
    kern = functools.partial(_rglru_kernel, seq=seq)
    return pl.pallas_call(
        kern,
        out_shape=(jax.ShapeDtypeStruct((n_seq * seq, RG_W), F32), jax.ShapeDtypeStruct((n_seq, 2, RG_W), F32)),
        grid=(n_seq, n_ct),
        in_specs=[
            pl.BlockSpec((seq, LANE), lambda b, c: (row_block0 + b, c)),
            pl.BlockSpec((seq, LANE), lambda b, c: (row_block0 + b, gr_col0 + c)),
            pl.BlockSpec((CONV_W, LANE), lambda b, c: (0, c)),
            pl.BlockSpec((1, LANE), lambda b, c: (0, c)),
            pl.BlockSpec((None, LANE, 4 * LANE), lambda b, c: (c, 0, 0)),
            pl.BlockSpec((None, 1, 4 * LANE), lambda b, c: (c, 0, 0)),
            pl.BlockSpec((2, LANE), lambda b, c: (0, c)),
            pl.BlockSpec((None, 2, LANE), lambda b, c: (b, 0, c)),
        ],
        out_specs=(
            pl.BlockSpec((seq, LANE), lambda b, c: (b, c)),
            pl.BlockSpec((None, 2, LANE), lambda b, c: (b, 0, c)),
        ),
        scratch_shapes=[
            pltpu.VMEM((seq + 2 * SUBLANE, LANE), F32),
            pltpu.VMEM((2, seq, LANE), F32),
            pltpu.VMEM((2, seq, LANE), F32),
        ],
        compiler_params=_cparams(("arbitrary", "arbitrary")),
        name=f"rglru_s{seq}",
    )(main, main, cw, cb, wg, bg, lam, h0)


def _rope_lanes(x, cos, sin_m, sin_p):
    n = x.shape[1] // LANE
    half = QK_ROPE // 4
    cos_t = jnp.concatenate([cos] * n, axis=1) if n > 1 else cos
    sm_t = jnp.concatenate([sin_m] * n, axis=1) if n > 1 else sin_m
    sp_t = jnp.concatenate([sin_p] * n, axis=1) if n > 1 else sin_p
    up = pltpu.roll(x, x.shape[1] - half, 1)
    dn = pltpu.roll(x, half, 1)
    return x * cos_t + up * sm_t + dn * sp_t


def _mla_prep_kernel(cq0_ref, cq1_ref, cq2_ref, ckv_ref, kr_ref, cos_ref, sm_ref, sp_ref,
                     qn_ref, wq_ref, kvn_ref, wuk_ref, wuvt_ref,
                     q_ref, k_ref, vt_ref, ckvn_ref):
    cq = [cq0_ref[...], cq1_ref[...], cq2_ref[...]]
    ms = (jnp.sum(cq[0] * cq[0], axis=-1, keepdims=True) + jnp.sum(cq[1] * cq[1], axis=-1, keepdims=True)
          + jnp.sum(cq[2] * cq[2], axis=-1, keepdims=True)) * (1.0 / Q_LORA)
    inv = lax.rsqrt(ms + EPS)
    blk = Q_LORA // 3
    q = None
    for j in range(3):
        cqn = ((cq[j] * inv) * qn_ref[:, j * blk:(j + 1) * blk]).astype(BF16)
        part = _dot(cqn, wq_ref[j * blk:(j + 1) * blk, :])
        q = part if q is None else q + part
    cos, sm, sp = cos_ref[...], sm_ref[...], sp_ref[...]
    q_ref[...] = (_rope_lanes(q, cos, sm, sp) * Q_PRESCALE).T.astype(BF16)

    ckv = ckv_ref[...]
    inv_kv = lax.rsqrt(jnp.mean(ckv * ckv, axis=-1, keepdims=True) + EPS)
    ckvn = (ckv * inv_kv) * kvn_ref[...]

    @pl.when(pl.program_id(0) < N_PROMPT // TM)
    def _():
        ckvn_ref[...] = ckvn

    ckvn_b = ckvn.astype(BF16)
    kr_rot = _rope_lanes(kr_ref[...], cos, sm, sp)
    k_ref[...] = (_dot(ckvn_b, wuk_ref[...]) + jnp.concatenate([kr_rot] * MLA_HEADS, axis=1)).astype(BF16)
    vt = _dot_nt(wuvt_ref[...], ckvn_b).astype(BF16)
    for c in range(vt_ref.shape[0]):
        vt_ref[c] = vt[:, c * KC:(c + 1) * KC]


def _mla_prep(main, krp, cos_t, sm_t, sp_t, q_norm, wq_p, kv_norm, wuk_p, wuvt):
    cq_col0 = 2 * RG_W // 256
    hp = MLA_HEADS * HEAD_PAD
    full = lambda shape: pl.BlockSpec(shape, lambda i: (0,) * len(shape))
    tab = pl.BlockSpec((TM, LANE), lambda i: (_pos_block(i, TM), 0))
    return pl.pallas_call(
        _mla_prep_kernel,
        out_shape=(
            jax.ShapeDtypeStruct((hp, N_TOK), BF16),
            jax.ShapeDtypeStruct((N_TOK, hp), BF16),
            jax.ShapeDtypeStruct((N_TOK // KC, MLA_HEADS * V_HEAD, KC), BF16),
            jax.ShapeDtypeStruct((N_PROMPT, KV_LORA), F32),
        ),
        grid=(N_TOK // TM,),
        in_specs=[
            pl.BlockSpec((TM, 256), lambda i: (i, cq_col0)),
            pl.BlockSpec((TM, 256), lambda i: (i, cq_col0 + 1)),
            pl.BlockSpec((TM, 256), lambda i: (i, cq_col0 + 2)),
            pl.BlockSpec((TM, 256), lambda i: (i, cq_col0 + 3)),
            pl.BlockSpec((TM, LANE), lambda i: (i, 0)),
            tab, tab, tab,
            full((1, Q_LORA)), full((Q_LORA, hp)), full((1, KV_LORA)), full((KV_LORA, hp)),
            full((MLA_HEADS * V_HEAD, KV_LORA)),
        ],
        out_specs=(
            pl.BlockSpec((hp, TM), lambda i: (0, i)),
            pl.BlockSpec((TM, hp), lambda i: (i, 0)),
            pl.BlockSpec((TM // KC, MLA_HEADS * V_HEAD, KC), lambda i: (i, 0, 0)),
            pl.BlockSpec((TM, KV_LORA), lambda i: (jnp.minimum(i, N_PROMPT // TM - 1), 0)),
        ),
        compiler_params=_cparams(("arbitrary",)),
        name="mla_prep",
    )(main, main, main, main, krp, cos_t, sm_t, sp_t, q_norm, wq_p, kv_norm, wuk_p, wuvt)


def _mla_ctx_kernel(ckv_ref, kr_ref, wuk_ref, wuvt_ref, k_ref, vt_ref):
    ckv_b = ckv_ref[...].astype(BF16)
    k_ref[...] = (_dot(ckv_b, wuk_ref[...]) + jnp.concatenate([kr_ref[...]] * MLA_HEADS, axis=1)).astype(BF16)
    vt_ref[...] = _dot_nt(wuvt_ref[...], ckv_b).astype(BF16)


def _mla_ctx(ctx_ckv, ctx_krp, wuk_p, wuvt):
    n = ctx_ckv.shape[0]
    hp = MLA_HEADS * HEAD_PAD
    tm = KC
    full = lambda shape: pl.BlockSpec(shape, lambda i: (0,) * len(shape))
    return pl.pallas_call(
        _mla_ctx_kernel,
        out_shape=(jax.ShapeDtypeStruct((n, hp), BF16),
                   jax.ShapeDtypeStruct((n // tm, MLA_HEADS * V_HEAD, tm), BF16)),
        grid=(n // tm,),
        in_specs=[
            pl.BlockSpec((tm, KV_LORA), lambda i: (i, 0)),
            pl.BlockSpec((tm, LANE), lambda i: (i, 0)),
            full((KV_LORA, hp)), full((MLA_HEADS * V_HEAD, KV_LORA)),
        ],
        out_specs=(pl.BlockSpec((tm, hp), lambda i: (i, 0)),
                   pl.BlockSpec((None, MLA_HEADS * V_HEAD, tm), lambda i: (i, 0, 0))),
        compiler_params=_cparams(("arbitrary",)),
        name="mla_ctx",
    )(ctx_ckv, ctx_krp, wuk_p, wuvt)


def _attn_kernel(*refs, seq, tq, n_ctx):
    if n_ctx:
        q_ref, k_ref, vt_ref, kc_ref, vtc_ref, o_ref, s_scr, p_scr, k_all, vt_all = refs
    else:
        q_ref, k_ref, vt_ref, o_ref, s_scr, p_scr, k_all, vt_all = refs
    has_ctx = 1 if n_ctx else 0
    n_own = seq // KC
    n = n_own + has_ctx
    qs = [q_ref[h * HEAD_PAD:(h + 1) * HEAD_PAD, :] for h in range(AH)]

    @pl.when(pl.program_id(2) == 0)
    def _():
        k_all[0:seq, :] = k_ref[...]
        if has_ctx:
            k_all[seq:seq + KC, :] = kc_ref[...]
        for h in range(AH):
            vt_all[0:n_own, h, 0:V_HEAD, :] = vt_ref[:, h * V_HEAD:(h + 1) * V_HEAD, :]
            if has_ctx:
                vt_all[n_own, h, 0:V_HEAD, :] = vtc_ref[h * V_HEAD:(h + 1) * V_HEAD, :]
            vt_all[:, h, V_HEAD:V_HEAD + V_ONES, :] = jnp.ones((n, V_ONES, KC), BF16)

    def k_chunk(c, h):
        t0 = c * KC if isinstance(c, int) else pl.multiple_of(c * KC, KC)
        return k_all[pl.ds(t0, KC), h * HEAD_PAD:(h + 1) * HEAD_PAD]

    def scores(c, slot):
        for h in range(AH):
            s_scr[slot, h] = _dot(k_chunk(c, h), qs[h])

    def softmax_chunk(slot, st):
        out = []
        for h in range(AH):
            m, a1, _, acc = st[h]
            t = s_scr[slot, h]
            m_new = jnp.maximum(m, jnp.max(t, axis=0, keepdims=True))
            p_scr[slot, h] = jnp.exp2(t - m_new).astype(BF16)
            out.append((m_new, jnp.exp2(m - m_new), a1, acc))
        return out

    def weighted_values(c, slot, st, alphas):
        return [st[h][:3] + (alphas[h] * st[h][3] + _dot(vt_all[c, h], p_scr[slot, h]),) for h in range(AH)]

    def step(c, r, st, with_s, with_v):
        if with_s:
            scores(c + 2, (r + 2) % 3)
        alphas = [st[h][2] for h in range(AH)]
        st = softmax_chunk(r, st)
        if with_v:
            prev = max(c - 2, 0) if isinstance(c, int) else jnp.maximum(c - 2, 0)
            st = weighted_values(prev, (r + 1) % 3, st, alphas)
        return st

    p_scr[1] = jnp.zeros(p_scr.shape[1:], BF16)
    p_scr[2] = jnp.zeros(p_scr.shape[1:], BF16)
    one = jnp.ones((1, tq), F32)
    st = [(jnp.full((1, tq), NEG, F32), one, one, jnp.zeros((V_HEAD + V_ONES, tq), F32)) for _ in range(AH)]
    scores(0, 0)
    if n > 1:
        scores(1, 1)
    n_trip = 0 if ATTN_UNROLL else max(n - 2, 0) // 3

    def trip(j, flat):
        st = [tuple(flat[4 * h:4 * h + 4]) for h in range(AH)]
        for r in range(3):
            st = step(3 * j + r, r, st, True, True)
        return sum((tuple(s) for s in st), ())

    if n_trip:
        flat = lax.fori_loop(0, n_trip, trip, sum((tuple(s) for s in st), ()))
        st = [tuple(flat[4 * h:4 * h + 4]) for h in range(AH)]
    for c in range(3 * n_trip, n):
        st = step(c, c % 3, st, c + 2 < n, c >= 2)
    if n > 1:
        st = weighted_values(n - 2, (n - 2) % 3, st, [st[h][2] for h in range(AH)])
    st = weighted_values(n - 1, (n - 1) % 3, st, [st[h][1] for h in range(AH)])
    for h in range(AH):
        acc = st[h][3]
        o_ref[:, h * V_HEAD:(h + 1) * V_HEAD] = (acc[0:V_HEAD] / acc[V_HEAD:V_HEAD + 1]).T.astype(o_ref.dtype)


def _attention(q, k, vt, kc, vtc, *, n_seq, seq, row_block0, tq):
    n_ctx = 0 if kc is None else PAST_LEN
    n_hp = MLA_HEADS // AH
    nq = seq // tq
    kern = functools.partial(_attn_kernel, seq=seq, tq=tq, n_ctx=n_ctx)
    in_specs = [
        pl.BlockSpec((AH * HEAD_PAD, tq), lambda b, j, i: (j, (row_block0 + b) * nq + i)),
        pl.BlockSpec((seq, AH * HEAD_PAD), lambda b, j, i: (row_block0 + b, j)),
        pl.BlockSpec((seq // KC, AH * V_HEAD, KC), lambda b, j, i: (row_block0 + b, j, 0)),
    ]
    args = [q, k, vt]
    if n_ctx:
        in_specs += [
            pl.BlockSpec((n_ctx, AH * HEAD_PAD), lambda b, j, i: (b, j)),
            pl.BlockSpec((None, AH * V_HEAD, KC), lambda b, j, i: (b, j, 0)),
        ]
        args += [kc, vtc]
    return pl.pallas_call(
        kern,
        out_shape=jax.ShapeDtypeStruct((n_seq * seq, MLA_HEADS * V_HEAD), BF16),
        grid=(n_seq, n_hp, nq),
        in_specs=in_specs,
        out_specs=pl.BlockSpec((tq, AH * V_HEAD), lambda b, j, i: (b * nq + i, j)),
        scratch_shapes=[pltpu.VMEM((3, AH, KC, tq), F32), pltpu.VMEM((3, AH, KC, tq), BF16),
                        pltpu.VMEM((seq + n_ctx, AH * HEAD_PAD), BF16),
                        pltpu.VMEM(((seq + n_ctx) // KC, AH, V_HEAD + V_ONES, KC), BF16)],
        compiler_params=_cparams(("arbitrary", "arbitrary", "arbitrary")),
        name=f"mla_attention_s{seq}",
    )(*args)


def _post_mixer(y, x, mod_ref, lng_ref, lnb_ref, wrh_ref, wrl_ref, x1_ref, h2_ref, lgt_ref):
    z = ALPHA * x + mod_ref[2:3, :] * y
    x1 = _layernorm_rows(z, lng_ref[...], lnb_ref[...])
    x1_ref[...] = x1
    h2 = _modulated(x1, mod_ref, 3, 4)
    h_hi, h_lo = _split_hi_lo(h2)
    h2_ref[...] = h_hi
    w_hi, w_lo = wrh_ref[...], wrl_ref[...]
    lgt_ref[...] = _dot_nt(w_hi, h_hi) + _dot_nt(w_hi, h_lo) + _dot_nt(w_lo, h_hi)


def _out_ab_kernel(yrgp_ref, yrgs_ref, op_ref, os_ref, wa_ref, wb_ref, xp_ref, xs_ref,
                   mod_ref, lng_ref, lnb_ref, wrh_ref, wrl_ref, x1_ref, h2_ref, lgt_ref):
    y = (_dot(_pick_rows(yrgp_ref, yrgs_ref, TM).astype(BF16), wa_ref[...])
         + _dot(_pick_rows(op_ref, os_ref, TM), wb_ref[...]))
    _post_mixer(y, _pick_rows(xp_ref, xs_ref, TM), mod_ref, lng_ref, lnb_ref, wrh_ref, wrl_ref,
                x1_ref, h2_ref, lgt_ref)


def _post_specs():
    full = lambda shape: pl.BlockSpec(shape, lambda i: (0,) * len(shape))
    in_specs = [
        pl.BlockSpec((None, MOD_ROWS, D_MODEL), lambda i: (_cond_block(i, TM), 0, 0)),
        full((1, D_MODEL)), full((1, D_MODEL)),
        full((N_EXPERTS, D_MODEL)), full((N_EXPERTS, D_MODEL)),
    ]
    out_shape = (
        jax.ShapeDtypeStruct((N_TOK, D_MODEL), F32),
        jax.ShapeDtypeStruct((N_TOK, D_MODEL), BF16),
        jax.ShapeDtypeStruct((N_EXPERTS, N_TOK), F32),
    )
    out_specs = (
        pl.BlockSpec((TM, D_MODEL), lambda i: (i, 0)),
        pl.BlockSpec((TM, D_MODEL), lambda i: (i, 0)),
        pl.BlockSpec((N_EXPERTS, TM), lambda i: (0, i)),
    )
    return in_specs, out_shape, out_specs


def _out_ab(yrg_p, yrg_s, o_p, o_s, w_a, w_b, xp, xs, mods, lng, lnb, wr_hi, wr_lo):
    full = lambda shape: pl.BlockSpec(shape, lambda i: (0,) * len(shape))
    post_in, out_shape, out_specs = _post_specs()
    return pl.pallas_call(
        _out_ab_kernel,
        out_shape=out_shape,
        grid=(N_TOK // TM,),
        in_specs=(_row_pair_specs(TM, RG_W) + _row_pair_specs(TM, MLA_HEADS * V_HEAD)
                  + [full((RG_W, D_MODEL)), full((MLA_HEADS * V_HEAD, D_MODEL))]
                  + _row_pair_specs(TM, D_MODEL) + post_in),
        out_specs=out_specs,
        compiler_params=_cparams(("arbitrary",)),
        name="out_ab",
    )(yrg_p, yrg_s, o_p, o_s, w_a, w_b, xp, xs, mods, lng, lnb, wr_hi, wr_lo)


def _proj_c_kernel(x_ref, mod_ref, cos_ref, sin_ref, wq_ref, wk_ref, wv_ref, wg_ref, q_ref, k_ref, v_ref, g_ref):
    h = _modulated(x_ref[...], mod_ref, 0, 1).astype(BF16)
    cos, sin = cos_ref[...], sin_ref[...]
    half = RET_DK // 2
    for hd in range(RET_HEADS):
        for w_ref, is_k in ((wq_ref, False), (wk_ref, True)):
            p = _dot(h, w_ref[:, hd * RET_DK:(hd + 1) * RET_DK])
            x1, x2 = p[:, :half], p[:, half:]
            r1 = x1 * cos - x2 * sin
            r2 = x2 * cos + x1 * sin
            if not is_k:
                q_ref[:, hd * RET_DK:hd * RET_DK + half] = r1.astype(BF16)
                q_ref[:, hd * RET_DK + half:(hd + 1) * RET_DK] = r2.astype(BF16)
            else:
                t1 = (r1 * RET_DK ** -0.5).T.astype(BF16)
                t2 = (r2 * RET_DK ** -0.5).T.astype(BF16)
                for c in range(k_ref.shape[0]):
                    k_ref[c, hd * RET_DK:hd * RET_DK + half, :] = t1[:, c * RET_CHUNK:(c + 1) * RET_CHUNK]
                    k_ref[c, hd * RET_DK + half:(hd + 1) * RET_DK, :] = t2[:, c * RET_CHUNK:(c + 1) * RET_CHUNK]
    step = 512
    for j in range(MIX_C // step):
        v_ref[:, j * step:(j + 1) * step] = _dot(h, wv_ref[:, j * step:(j + 1) * step]).astype(BF16)
        g_ref[:, j * step:(j + 1) * step] = _dot(h, wg_ref[:, j * step:(j + 1) * step])


def _proj_c(x, mods, cos_t, sin_t, wq, wk, wv, wg):
    full = lambda shape: pl.BlockSpec(shape, lambda i: (0,) * len(shape))
    qk = RET_HEADS * RET_DK
    tab = pl.BlockSpec((TM, RET_DK // 2), lambda i: (_pos_block(i, TM), 0))
    return pl.pallas_call(
        _proj_c_kernel,
        out_shape=(
            jax.ShapeDtypeStruct((N_TOK, qk), BF16), jax.ShapeDtypeStruct((N_TOK // RET_CHUNK, qk, RET_CHUNK), BF16),
            jax.ShapeDtypeStruct((N_TOK, MIX_C), BF16), jax.ShapeDtypeStruct((N_TOK, MIX_C), F32),
        ),
        grid=(N_TOK // TM,),
        in_specs=[
            pl.BlockSpec((TM, D_MODEL), lambda i: (i, 0)),
            pl.BlockSpec((None, MOD_ROWS, D_MODEL), lambda i: (_cond_block(i, TM), 0, 0)),
            tab, tab,
            full((D_MODEL, qk)), full((D_MODEL, qk)), full((D_MODEL, MIX_C)), full((D_MODEL, MIX_C)),
        ],
        out_specs=(
            pl.BlockSpec((TM, qk), lambda i: (i, 0)),
            pl.BlockSpec((TM // RET_CHUNK, qk, RET_CHUNK), lambda i: (i, 0, 0)),
            pl.BlockSpec((TM, MIX_C), lambda i: (i, 0)), pl.BlockSpec((TM, MIX_C), lambda i: (i, 0)),
        ),
        compiler_params=_cparams(("arbitrary",), 56),
        name="proj_c",
    )(x, mods, cos_t, sin_t, wq, wk, wv, wg)


def _retention_kernel(*refs, seq, with_state):
    if with_state:
        q_ref, kt_ref, v_ref, gam_ref, r0_ref, o_ref, rfin_ref, r_s = refs
    else:
        q_ref, kt_ref, v_ref, gam_ref, r0_ref, o_ref, r_s = refs
        rfin_ref = None
    c = RET_CHUNK
    n = seq // c
    ii = lax.broadcasted_iota(I32, (c, c), 0).astype(F32)
    jj = lax.broadcasted_iota(I32, (c, c), 1).astype(F32)
    ci = lax.broadcasted_iota(I32, (c, 1), 0).astype(F32)
    li = lax.broadcasted_iota(I32, (1, c), 1).astype(F32)

    consts = []
    for d in range(2):
        gam = gam_ref[d]
        lg_row = -_softplus(-gam[0:1, :])
        lg = jnp.broadcast_to(lg_row, (c, c))
        lg_col = jnp.broadcast_to(lg_row[:, 0:1], (c, 1))
        if d == 0:
            diff = ii - jj
            xi = jnp.exp((ci + 1.0) * lg_col)
            zeta = jnp.exp((c - 1.0 - li) * lg_row)
        else:
            diff = jj - ii
            xi = jnp.exp((c - ci) * lg_col)
            zeta = jnp.exp(li * lg_row)
        dmat = jnp.where(diff >= 0, jnp.exp(jnp.maximum(diff, 0.0) * lg), 0.0)
        g_chunk = jnp.exp(float(c) * lg_row[:, 0:1])
        consts.append((dmat, xi, zeta, g_chunk))
        r_s[d] = r0_ref[d]

    def chunk(d, idx, accumulate):
        dmat, xi, zeta, g_chunk = consts[d]
        t0 = pl.multiple_of(idx * c, c)
        qb = q_ref[pl.ds(t0, c), :]
        kt = kt_ref[idx]
        vb = v_ref[pl.ds(t0, c), :]
        r = r_s[d]
        inner = _dot(qb, kt) * dmat
        o = _dot(inner.astype(BF16), vb) + _dot((qb.astype(F32) * xi).astype(BF16), r.astype(BF16))
        r_s[d] = r * g_chunk + _dot((kt.astype(F32) * zeta).astype(BF16), vb)
        if accumulate:
            o_ref[pl.ds(t0, c), :] = o_ref[pl.ds(t0, c), :] + o
        else:
            o_ref[pl.ds(t0, c), :] = o

    def first_half(s, carry):
        chunk(0, s, False)
        chunk(1, n - 1 - s, False)
        return carry

    def second_half(s, carry):
        chunk(0, s, True)
        chunk(1, n - 1 - s, True)
        return carry

    lax.fori_loop(0, n // 2, first_half, 0)
    lax.fori_loop(n // 2, n, second_half, 0)
    if with_state:
        for d in range(2):
            rfin_ref[d] = r_s[d]


def _retention(q, k, v, gam, r0, *, n_seq, seq, row_block0, with_state):
    kern = functools.partial(_retention_kernel, seq=seq, with_state=with_state)
    out_shape = [jax.ShapeDtypeStruct((n_seq * seq, MIX_C), F32)]
    out_specs = [pl.BlockSpec((seq, RET_DV), lambda b, h: (b, h))]
    if with_state:
        out_shape.append(jax.ShapeDtypeStruct((n_seq, 2, RET_HEADS, RET_DK, RET_DV), F32))
        out_specs.append(pl.BlockSpec((None, 2, None, RET_DK, RET_DV), lambda b, h: (b, 0, h, 0, 0)))
    return pl.pallas_call(
        kern,
        out_shape=tuple(out_shape),
        grid=(n_seq, RET_HEADS),
        in_specs=[
            pl.BlockSpec((seq, RET_DK), lambda b, h: (row_block0 + b, h)),
            pl.BlockSpec((seq // RET_CHUNK, RET_DK, RET_CHUNK), lambda b, h: (row_block0 + b, h, 0)),
            pl.BlockSpec((seq, RET_DV), lambda b, h: (row_block0 + b, h)),
            pl.BlockSpec((2, None, SUBLANE, LANE), lambda b, h: (0, h, 0, 0)),
            pl.BlockSpec((None, 2, None, RET_DK, RET_DV), lambda b, h: (b, 0, h, 0, 0)),
        ],
        out_specs=tuple(out_specs),
        scratch_shapes=[pltpu.VMEM((2, RET_DK, RET_DV), F32)],
        compiler_params=_cparams(("arbitrary", "arbitrary"), 56),
        name=f"retention_s{seq}",
    )(q, k, v, gam, r0)


def _out_c_kernel(op_ref, os_ref, g_ref, w_ref, x_ref, mod_ref, lng_ref, lnb_ref, wrh_ref, wrl_ref,
                  x1_ref, h2_ref, lgt_ref):
    y = None
    o_all = _pick_rows(op_ref, os_ref, TM)
    for hd in range(RET_HEADS):
        o = o_all[:, hd * RET_DV:(hd + 1) * RET_DV]
        mu = jnp.mean(o, axis=-1, keepdims=True)
        oc = o - mu
        var = jnp.mean(oc * oc, axis=-1, keepdims=True)
        on = oc * lax.rsqrt(var + EPS)
        a = (on * _silu(g_ref[:, hd * RET_DV:(hd + 1) * RET_DV])).astype(BF16)
        part = _dot(a, w_ref[hd * RET_DV:(hd + 1) * RET_DV, :])
        y = part if y is None else y + part
    _post_mixer(y, x_ref[...], mod_ref, lng_ref, lnb_ref, wrh_ref, wrl_ref, x1_ref, h2_ref, lgt_ref)


def _out_c(o_p, o_s, g, w, x, mods, lng, lnb, wr_hi, wr_lo):
    full = lambda shape: pl.BlockSpec(shape, lambda i: (0,) * len(shape))
    post_in, out_shape, out_specs = _post_specs()
    return pl.pallas_call(
        _out_c_kernel,
        out_shape=out_shape,
        grid=(N_TOK // TM,),
        in_specs=_row_pair_specs(TM, MIX_C) + [
            pl.BlockSpec((TM, MIX_C), lambda i: (i, 0)),
            full((MIX_C, D_MODEL)),
            pl.BlockSpec((TM, D_MODEL), lambda i: (i, 0)),
        ] + post_in,
        out_specs=out_specs,
        compiler_params=_cparams(("arbitrary",), 56),
        name="out_c",
    )(o_p, o_s, g, w, x, mods, lng, lnb, wr_hi, wr_lo)


def _route_kernel(lgt_ref, bias_ref, tri_ref, w_ref, lpos_ref, p16_ref):
    tt = lgt_ref.shape[1]
    scores = jax.nn.sigmoid(lgt_ref[...])
    sel = scores + bias_ref[...]
    srow = lax.broadcasted_iota(I32, (GROUP_SIZE, tt), 0).astype(F32)
    ninf = -jnp.inf

    gs = []
    for g in range(N_GROUPS):
        sg = sel[g * GROUP_SIZE:(g + 1) * GROUP_SIZE, :]
        m1 = jnp.max(sg, axis=0, keepdims=True)
        i1 = jnp.min(jnp.where(sg == m1, srow, float(GROUP_SIZE)), axis=0, keepdims=True)
        m2 = jnp.max(jnp.where(srow == i1, ninf, sg), axis=0, keepdims=True)
        gs.append(m1 + m2)
    gs = jnp.concatenate(gs, axis=0)
    chosen = jnp.zeros((N_GROUPS, tt), F32)
    for _ in range(TOPK_GROUPS):
        mg = jnp.max(gs, axis=0, keepdims=True)
        gi = jnp.min(jnp.where(gs == mg, srow, float(N_GROUPS)), axis=0, keepdims=True)
        hit = srow == gi
        chosen = jnp.where(hit, 1.0, chosen)
        gs = jnp.where(hit, ninf, gs)
    sel = jnp.concatenate(
        [jnp.where(jnp.broadcast_to(chosen[g:g + 1, :], (GROUP_SIZE, tt)) > 0.5,
                   sel[g * GROUP_SIZE:(g + 1) * GROUP_SIZE, :], ninf) for g in range(N_GROUPS)], axis=0)

    erow = lax.broadcasted_iota(I32, (N_EXPERTS, tt), 0).astype(F32)
    ids, ws = [], []
    for _ in range(TOP_K):
        m = jnp.max(sel, axis=0, keepdims=True)
        ei = jnp.min(jnp.where(sel == m, erow, float(N_EXPERTS)), axis=0, keepdims=True)
        hit = erow == ei
        ids.append(ei)
        ws.append(jnp.sum(jnp.where(hit, scores, 0.0), axis=0, keepdims=True))
        sel = jnp.where(hit, ninf, sel)
    wsum = ws[0]
    for k in range(1, TOP_K):
        wsum = wsum + ws[k]
    w_ref[...] = jnp.concatenate([w / wsum * ROUTED_SCALE for w in ws], axis=0)

    member_f = jnp.zeros((N_EXPERTS, tt), F32)
    for k in range(TOP_K):
        member_f = jnp.where(erow == ids[k], 1.0, member_f)
    member_b = member_f.astype(BF16)
    cnt_row = _dot_nt(jnp.ones((SUBLANE, tt), BF16), member_b)[0:1, :]
    p16_row = jnp.ceil(cnt_row * (1.0 / PIECE)) * PIECE
    lane_e = lax.broadcasted_iota(I32, (N_EXPERTS, N_EXPERTS), 1)
    sub_e = lax.broadcasted_iota(I32, (N_EXPERTS, N_EXPERTS), 0)
    run_start = jnp.sum(jnp.where(lane_e < sub_e, jnp.broadcast_to(p16_row, (N_EXPERTS, N_EXPERTS)), 0.0),
                        axis=1, keepdims=True)
    rank = _dot(member_b, tri_ref[...]) + run_start
    lpos_ref[...] = jnp.concatenate(
        [jnp.sum(jnp.where(erow == ids[k], rank, 0.0), axis=0, keepdims=True) for k in range(TOP_K)],
        axis=0).astype(I32)
    p16_ref[pl.ds(pl.program_id(0), 1), :] = jnp.concatenate(
        [p16_row, jnp.zeros((1, LANE - N_EXPERTS), F32)], axis=1)


def _route(lgt, bias, tri):
    return pl.pallas_call(
        _route_kernel,
        out_shape=(
            jax.ShapeDtypeStruct((TOP_K, N_TOK), F32), jax.ShapeDtypeStruct((TOP_K, N_TOK), I32),
            jax.ShapeDtypeStruct((N_WIN, LANE), F32),
        ),
        grid=(N_WIN,),
        in_specs=[
            pl.BlockSpec((N_EXPERTS, WIN), lambda i: (0, i)),
            pl.BlockSpec((N_EXPERTS, 1), lambda i: (0, 0)),
            pl.BlockSpec((WIN, WIN), lambda i: (0, 0)),
        ],
        out_specs=(
            pl.BlockSpec((TOP_K, WIN), lambda i: (0, i)), pl.BlockSpec((TOP_K, WIN), lambda i: (0, i)),
            pl.BlockSpec((N_WIN, LANE), lambda i: (0, 0)),
        ),
        compiler_params=_cparams(("arbitrary",)),
        name="moe_route",
    )(lgt, bias, tri)


def _sort_kernel(lpos_ref, h_ref, xl_ref, oh_all):
    lp = jnp.where(pl.program_id(0) < N_WIN, lpos_ref[...], -1)
    riota = lax.broadcasted_iota(I32, (SORT_ROWS, WIN), 0).astype(jnp.int16)
    one = jnp.ones((SORT_ROWS, WIN), BF16)
    for j in range(RL // SORT_ROWS):
        rel = (lp - j * SORT_ROWS).astype(jnp.int16)
        oh = jnp.zeros((SORT_ROWS, WIN), BF16)
        for k in range(TOP_K):
            oh = jnp.where(rel[k:k + 1, :] == riota, one, oh)
        oh_all[j * SORT_ROWS:(j + 1) * SORT_ROWS, :] = oh
    x = h_ref[...]
    for nt in range(D_MODEL // SORT_ROWS):
        cols = slice(nt * SORT_ROWS, (nt + 1) * SORT_ROWS)
        xl_ref[:, cols] = _dot(oh_all[...], x[:, cols]).astype(BF16)


def _sort_rows(lpos, h2):
    last = N_WIN - 1
    return pl.pallas_call(
        _sort_kernel,
        out_shape=jax.ShapeDtypeStruct(((N_WIN + 1) * RL, D_MODEL), BF16),
        grid=(N_WIN + 1,),
        in_specs=[
            pl.BlockSpec((TOP_K, WIN), lambda i: (0, jnp.minimum(i, last))),
            pl.BlockSpec((WIN, D_MODEL), lambda i: (jnp.minimum(i, last), 0)),
        ],
        out_specs=pl.BlockSpec((RL, D_MODEL), lambda i: (i, 0)),
        scratch_shapes=[pltpu.VMEM((RL, WIN), BF16)],
        compiler_params=_cparams(("arbitrary",)),
        name="moe_sort",
    )(lpos, h2)


def _moe_tables(p16):
    n_w = jnp.arange(N_WIN, dtype=I32)
    run_start = jnp.cumsum(p16, axis=1) - p16
    rw = jnp.sum(p16, axis=1)
    cum_w = jnp.cumsum(p16, axis=0) - p16
    tot = jnp.sum(p16, axis=0)
    nblk = (tot + TME - 1) // TME
    blk_end = jnp.cumsum(nblk)
    blk0 = blk_end - nblk
    n_used = blk_end[-1]
    b = jnp.arange(N_EBLOCKS + 1, dtype=I32)
    block_e = jnp.minimum(jnp.sum((blk_end[None, :] <= b[:, None]).astype(I32), axis=1), N_EXPERTS - 1)
    piece = jnp.arange(PIECES, dtype=I32) * PIECE
    rp = (b - blk0[block_e])[:, None] * TME + piece[None, :]
    valid = (b[:, None] < n_used) & (rp < tot[block_e][:, None])
    cum_e = cum_w.T[block_e]
    len_e = p16.T[block_e]
    start_e = run_start.T[block_e]
    in_win = (cum_e[:, None, :] <= rp[:, :, None]) & (rp[:, :, None] < (cum_e + len_e)[:, None, :])
    row = n_w[None, None, :] * RL + start_e[:, None, :] + rp[:, :, None] - cum_e[:, None, :]
    row = jnp.sum(jnp.where(in_win, row, 0), axis=2)
    zero_src = N_WIN * RL
    assert 3 * TME <= RL
    trash = N_WIN * RL + (1 + b % 2)[:, None] * TME + piece[None, :]
    gather_row = (jnp.where(valid, row, zero_src) // PIECE).reshape(-1).astype(I32)
    scatter_row = (jnp.where(valid, row, trash) // PIECE).reshape(-1).astype(I32)
    rw = jnp.concatenate([rw, jnp.zeros((1,), rw.dtype)])
    return rw.astype(I32), block_e.astype(I32), n_used.astype(I32).reshape(1), gather_row, scatter_row


def _expert_kernel(be_ref, nb_ref, grow_ref, srow_ref, xl_hbm, wg_ref, wu_ref, wd_ref, yl_hbm,
                   xbuf, ybuf, wg_b, wu_b, wd_b, gsem, ssem):
    b = pl.program_id(0)
    nb = nb_ref[0]

    @pl.when(jnp.logical_and(b < nb, jnp.logical_or(b == 0, be_ref[b] != be_ref[jnp.maximum(b - 1, 0)])))
    def _():
        wg_b[...] = wg_ref[...].astype(BF16)
        wu_b[...] = wu_ref[...].astype(BF16)
        wd_b[...] = wd_ref[...].astype(BF16)

    def gather_start(blk, slot):
        for p in range(PIECES):
            pltpu.make_async_copy(xl_hbm.at[grow_ref[blk * PIECES + p]], xbuf.at[slot, p], gsem.at[slot]).start()

    def gather_wait(slot):
        pltpu.make_async_copy(xl_hbm.at[pl.ds(0, PIECES)], xbuf.at[slot], gsem.at[slot]).wait()

    def scatter_wait(slot):
        pltpu.make_async_copy(ybuf.at[slot], yl_hbm.at[pl.ds(0, PIECES)], ssem.at[slot]).wait()

    @pl.when(b < nb)
    def _():
        slot = b % 2

        @pl.when(b == 0)
        def _():
            gather_start(0, 0)

        @pl.when(b + 1 < nb)
        def _():
            gather_start(b + 1, 1 - slot)

        gather_wait(slot)

        @pl.when(b >= 2)
        def _():
            scatter_wait(slot)

        x = xbuf[slot].reshape(TME, D_MODEL)
        hb = _silu(_dot(x, wg_b[...])) * _dot(x, wu_b[...])
        ybuf[slot] = _dot(hb.astype(BF16), wd_b[...]).astype(BF16).reshape(PIECES, PIECE, D_MODEL)
        for p in range(PIECES):
            pltpu.make_async_copy(ybuf.at[slot, p], yl_hbm.at[srow_ref[b * PIECES + p]], ssem.at[slot]).start()

        @pl.when(b == nb - 1)
        def _():
            scatter_wait(slot)

            @pl.when(b >= 1)
            def _():
                scatter_wait(1 - slot)


def _experts(block_e, n_used, gather_row, scatter_row, xl, wg, wu, wd, layer):
    def w_map(i, be, nb, gr, sr):
        return (layer, be[jnp.minimum(i, nb[0] - 1)], 0, 0)

    n_pieces = (N_WIN + 1) * RL // PIECE
    yl = pl.pallas_call(
        _expert_kernel,
        out_shape=jax.ShapeDtypeStruct((n_pieces, PIECE, D_MODEL), BF16),
        input_output_aliases={4: 0},
        grid_spec=pltpu.PrefetchScalarGridSpec(
            num_scalar_prefetch=4,
            grid=(N_EBLOCKS,),
            in_specs=[
                pl.BlockSpec(memory_space=pl.ANY),
                pl.BlockSpec((None, None, D_MODEL, D_EXPERT), w_map),
                pl.BlockSpec((None, None, D_MODEL, D_EXPERT), w_map),
                pl.BlockSpec((None, None, D_EXPERT, D_MODEL), w_map),
            ],
            out_specs=pl.BlockSpec(memory_space=pl.ANY),
            scratch_shapes=[
                pltpu.VMEM((2, PIECES, PIECE, D_MODEL), BF16), pltpu.VMEM((2, PIECES, PIECE, D_MODEL), BF16),
                pltpu.VMEM((D_MODEL, D_EXPERT), BF16), pltpu.VMEM((D_MODEL, D_EXPERT), BF16),
                pltpu.VMEM((D_EXPERT, D_MODEL), BF16),
                pltpu.SemaphoreType.DMA((2,)), pltpu.SemaphoreType.DMA((2,)),
            ],
        ),
        compiler_params=_cparams(("arbitrary",)),
        name="moe_experts",
    )(block_e, n_used, gather_row, scatter_row, xl.reshape(n_pieces, PIECE, D_MODEL), wg, wu, wd)
    return yl.reshape(n_pieces * PIECE, D_MODEL)


def _combine_kernel(rw_ref, yl_ref, lpt_ref, wt_ref, h_ref, x1_ref, mod_ref, lng_ref, lnb_ref,
                    wsg_ref, wsu_ref, wsd_ref, *rest, split):
    del rw_ref
    if split:
        outp_ref, outs_ref, p_w, lp_b, wt_b = rest
    else:
        out_ref, p_w, lp_b, wt_b = rest
    hb = h_ref[...]
    shared = _dot((_silu(_dot(hb, wsg_ref[...])) * _dot(hb, wsu_ref[...])).astype(BF16), wsd_ref[...])
    lp = lpt_ref[...]
    wt = wt_ref[...]
    for k in range(TOP_K):
        lp_b[k] = jnp.broadcast_to(lp[:, k:k + 1], (WIN, SORT_ROWS)).astype(jnp.int16)
        wt_b[k] = jnp.broadcast_to(wt[:, k:k + 1], (WIN, SORT_ROWS)).astype(BF16)
    ciota = lax.broadcasted_iota(I32, (WIN, SORT_ROWS), 1)
    for j in range(RL // SORT_ROWS):
        col = (ciota + j * SORT_ROWS).astype(jnp.int16)
        pm = jnp.zeros((WIN, SORT_ROWS), BF16)
        for k in range(TOP_K):
            pm = jnp.where(lp_b[k] == col, wt_b[k], pm)
        p_w[:, j * SORT_ROWS:(j + 1) * SORT_ROWS] = pm
    routed = _dot(p_w[...], yl_ref[...])
    z = ALPHA * x1_ref[...] + mod_ref[5:6, :] * (routed + shared)
    out = _layernorm_rows(z, lng_ref[...], lnb_ref[...])
    if split:
        @pl.when(pl.program_id(0) < N_PROMPT // WIN)
        def _():
            outp_ref[...] = out

        @pl.when(pl.program_id(0) >= N_PROMPT // WIN)
        def _():
            outs_ref[...] = out
    else:
        out_ref[...] = out


def _combine(rw, yl, lpos_t, wt, h2, x1, mods, lng, lnb, wsg, wsu, wsd, *, split):
    full = lambda shape: pl.BlockSpec(shape, lambda i, rw: (0,) * len(shape))
    if split:
        out_shape = (jax.ShapeDtypeStruct((N_PROMPT, D_MODEL), F32), jax.ShapeDtypeStruct((N_SAMPLE, D_MODEL), F32))
        out_specs = tuple(_row_pair_specs(WIN, D_MODEL))
    else:
        out_shape = jax.ShapeDtypeStruct((N_TOK, D_MODEL), F32)
        out_specs = pl.BlockSpec((WIN, D_MODEL), lambda i, rw: (i, 0))
    return pl.pallas_call(
        functools.partial(_combine_kernel, split=split),
        out_shape=out_shape,
        grid_spec=pltpu.PrefetchScalarGridSpec(
            num_scalar_prefetch=1,
            grid=(N_WIN,),
            in_specs=[
                pl.BlockSpec((RL, D_MODEL), lambda i, rw: (i, 0)),
                pl.BlockSpec((WIN, TOP_K), lambda i, rw: (i, 0)),
                pl.BlockSpec((WIN, TOP_K), lambda i, rw: (i, 0)),
                pl.BlockSpec((WIN, D_MODEL), lambda i, rw: (i, 0)),
                pl.BlockSpec((WIN, D_MODEL), lambda i, rw: (i, 0)),
                pl.BlockSpec((None, MOD_ROWS, D_MODEL), lambda i, rw: (_cond_block(i, WIN), 0, 0)),
                full((1, D_MODEL)), full((1, D_MODEL)),
                full((D_MODEL, D_EXPERT)), full((D_MODEL, D_EXPERT)), full((D_EXPERT, D_MODEL)),
            ],
            out_specs=out_specs,
            scratch_shapes=[pltpu.VMEM((WIN, RL), BF16),
                            pltpu.VMEM((TOP_K, WIN, SORT_ROWS), jnp.int16),
                            pltpu.VMEM((TOP_K, WIN, SORT_ROWS), BF16)],
        ),
        compiler_params=_cparams(("arbitrary",)),
        name="moe_combine",
    )(rw, yl, lpos_t, wt, h2, x1, mods, lng, lnb, wsg, wsu, wsd)


def _moe_and_norm(x1, h2, lgt, mods, lng, lnb, router_bias, tri, wg, wu, wd, wsg, wsu, wsd, *, layer, split):
    wts, lpos, p16 = _route(lgt, router_bias.reshape(N_EXPERTS, 1), tri)
    rw, block_e, n_used, gather_row, scatter_row = _moe_tables(p16[:, :N_EXPERTS].astype(I32))
    xl = _sort_rows(lpos, h2)
    yl = _experts(block_e, n_used, gather_row, scatter_row, xl, wg, wu, wd, layer)
    return _combine(rw, yl, lpos.T, wts.T, h2, x1, mods, lng, lnb, wsg, wsu, wsd, split=split)


def _rope_tables_mla():
    t = jnp.arange(DEC_SEQ)
    row = (t // GRID_W).astype(F32)
    col = (t % GRID_W).astype(F32)
    n = QK_ROPE // 4
    inv = ROPE_BASE ** (-jnp.arange(n, dtype=F32) / n)
    ang_r = row[:, None] * inv
    ang_c = col[:, None] * inv
    cos = jnp.ones((DEC_SEQ, LANE), F32)
    sin_m = jnp.zeros((DEC_SEQ, LANE), F32)
    sin_p = jnp.zeros((DEC_SEQ, LANE), F32)
    l0 = ROPE_LANE0
    for base, ang in ((l0, ang_r), (l0 + 2 * n, ang_c)):
        c, s = jnp.cos(ang), jnp.sin(ang)
        cos = cos.at[:, base:base + n].set(c).at[:, base + n:base + 2 * n].set(c)
        sin_m = sin_m.at[:, base:base + n].set(-s)
        sin_p = sin_p.at[:, base + n:base + 2 * n].set(s)
    ident = (jnp.ones((TM, LANE), F32), jnp.zeros((TM, LANE), F32), jnp.zeros((TM, LANE), F32))
    return tuple(jnp.concatenate([i, tbl], axis=0) for i, tbl in zip(ident, (cos, sin_m, sin_p)))


def _rope_tables_ret():
    half = RET_DK // 2
    theta = ROPE_BASE ** (-jnp.linspace(0.0, 1.0, half, dtype=F32))
    ang = jnp.arange(DEC_SEQ, dtype=F32)[:, None] * theta
    cos = jnp.concatenate([jnp.ones((TM, half), F32), jnp.cos(ang)], axis=0)
    sin = jnp.concatenate([jnp.zeros((TM, half), F32), jnp.sin(ang)], axis=0)
    return cos, sin


def _pad_heads(w, width, lane0=0):
    k = w.shape[0]
    w = w.reshape(k, MLA_HEADS, width)
    out = jnp.zeros((k, MLA_HEADS, HEAD_PAD), w.dtype).at[:, :, lane0:lane0 + width].set(w)
    return out.reshape(k, MLA_HEADS * HEAD_PAD)


def _rg_gate_weights(wa, ba, wx, bx):
    n_ct = RG_W // LANE
    per = LANE // RG_BW
    tiles_w, tiles_b = [], []
    for c in range(n_ct):
        cols_w, cols_b = [], []
        for d in range(2):
            for w, b in ((wa, ba), (wx, bx)):
                m = jnp.zeros((LANE, LANE), F32)
                for p in range(per):
                    m = m.at[p * RG_BW:(p + 1) * RG_BW, p * RG_BW:(p + 1) * RG_BW].set(w[d, c * per + p])
                cols_w.append(m)
                cols_b.append(b[d, c * LANE:(c + 1) * LANE])
        tiles_w.append(jnp.concatenate(cols_w, axis=1))
        tiles_b.append(jnp.concatenate(cols_b, axis=0)[None, :])
    return jnp.stack(tiles_w).astype(BF16), jnp.stack(tiles_b)


def kernel(x_prompt, x_sample, cache_mla_ckv, cache_mla_krope, state_rglru, state_ret, c, c_ctx, w_ada, b_ada,
           ln_g, ln_b, w_in_ab, rg_conv_w, rg_conv_b, rg_wa, rg_ba, rg_wx, rg_bx, rg_lambda, mla_q_norm, mla_w_uq,
           mla_kv_norm, mla_w_ukv, w_out_ab, w_in_c, ret_gamma_logit, w_out_c, w_router, router_bias,
           w_exp_gate, w_exp_up, w_exp_down, w_sh_gate, w_sh_up, w_sh_down):
    xp = x_prompt.reshape(N_PROMPT, D_MODEL)
    xs = x_sample.reshape(N_SAMPLE, D_MODEL)
    cond = jnp.zeros((16, D_MODEL), F32).at[0].set(c_ctx).at[1:1 + DEC_BATCH].set(c)
    mods_all = _ada_modulation(cond, w_ada, b_ada).reshape(DEPTH, 16, 6, D_MODEL)[:, :N_COND]
    mods_all = jnp.pad(mods_all, ((0, 0), (0, 0), (0, MOD_ROWS - 6), (0, 0)))

    tri = (jnp.arange(WIN)[:, None] < jnp.arange(WIN)[None, :]).astype(BF16)
    wr_t = jnp.swapaxes(w_router, 1, 2)
    wr_hi = wr_t.astype(BF16)
    wr_lo = (wr_t - wr_hi.astype(F32)).astype(BF16)
    wg_e, wu_e, wd_e = w_exp_gate, w_exp_up, w_exp_down
    wsg, wsu, wsd = w_sh_gate.astype(BF16), w_sh_up.astype(BF16), w_sh_down.astype(BF16)

    l, e = 0, 0
    mods = mods_all[l]
    n_main = 2 * RG_W + Q_LORA + KV_LORA
    w_main = w_in_ab[e][:, :n_main].astype(BF16)
    w_kr = jnp.zeros((D_MODEL, LANE), F32).at[:, ROPE_LANE0:ROPE_LANE0 + QK_ROPE].set(w_in_ab[e][:, n_main:]).astype(BF16)
    main, krp = _proj_ab(xp, xs, mods, w_main, w_kr)

    wg_rg, bg_rg = _rg_gate_weights(rg_wa[e], rg_ba[e], rg_wx[e], rg_bx[e])
    h0_p = jnp.zeros((BATCH, 2, RG_W), F32)
    rg_args = (rg_conv_w[e], rg_conv_b[e].reshape(1, RG_W), wg_rg, bg_rg, rg_lambda[e])
    yrg_p, rg_fin = _rglru(main, *rg_args, h0_p, n_seq=BATCH, seq=SEQ, row_block0=0)
    yrg_s, _ = _rglru(main, *rg_args, state_rglru[:, e],
                      n_seq=DEC_BATCH, seq=DEC_SEQ, row_block0=N_PROMPT // DEC_SEQ)

    cos_t, sm_t, sp_t = _rope_tables_mla()
    w_uq = mla_w_uq[e].reshape(Q_LORA, MLA_HEADS, QK_NOPE + QK_ROPE)
    wq_p = _pad_heads(w_uq.reshape(Q_LORA, -1), QK_NOPE + QK_ROPE).astype(BF16)
    w_ukv = mla_w_ukv[e].reshape(KV_LORA, MLA_HEADS, QK_NOPE + V_HEAD)
    wuk_p = _pad_heads(w_ukv[:, :, :QK_NOPE].reshape(KV_LORA, -1), QK_NOPE).astype(BF16)
    wuvt = w_ukv[:, :, QK_NOPE:].reshape(KV_LORA, MLA_HEADS * V_HEAD).T.astype(BF16)
    q_att, k_att, v_att, ckv_n = _mla_prep(main, krp, cos_t, sm_t, sp_t, mla_q_norm[e].reshape(1, Q_LORA), wq_p,
                                           mla_kv_norm[e].reshape(1, KV_LORA), wuk_p, wuvt)
    ctx_ckv = cache_mla_ckv[:, e].reshape(DEC_BATCH * PAST_LEN, KV_LORA)
    ctx_krp = jnp.zeros((DEC_BATCH * PAST_LEN, LANE), F32).at[:, ROPE_LANE0:ROPE_LANE0 + QK_ROPE].set(
        cache_mla_krope[:, e].reshape(DEC_BATCH * PAST_LEN, QK_ROPE))
    kc_att, vc_att = _mla_ctx(ctx_ckv, ctx_krp, wuk_p, wuvt)

    o_att_p = _attention(q_att, k_att, v_att, None, None, n_seq=BATCH, seq=SEQ, row_block0=0, tq=SEQ)
    o_att_s = _attention(q_att, k_att, v_att, kc_att, vc_att,
                         n_seq=DEC_BATCH, seq=DEC_SEQ, row_block0=N_PROMPT // DEC_SEQ, tq=TQ)

    w_out = w_out_ab[e].astype(BF16)
    x1, h2, lgt = _out_ab(yrg_p, yrg_s, o_att_p, o_att_s, w_out[:RG_W], w_out[RG_W:], xp, xs, mods,
                          ln_g[l, 0].reshape(1, D_MODEL), ln_b[l, 0].reshape(1, D_MODEL), wr_hi[l], wr_lo[l])
    x = _moe_and_norm(x1, h2, lgt, mods, ln_g[l, 1].reshape(1, D_MODEL), ln_b[l, 1].reshape(1, D_MODEL),
                      router_bias[l], tri, wg_e, wu_e, wd_e, wsg[l], wsu[l], wsd[l], layer=l, split=False)

    new_ckv = ckv_n.reshape(BATCH, 1, SEQ, KV_LORA)
    new_krope = krp[:N_PROMPT, ROPE_LANE0:ROPE_LANE0 + QK_ROPE].reshape(BATCH, 1, SEQ, QK_ROPE)
    new_rg = rg_fin.reshape(BATCH, 1, 2, RG_W)

    l, o = 1, 0
    mods = mods_all[l]
    qk = RET_HEADS * RET_DK
    w_c = w_in_c[o].astype(BF16)
    cos_r, sin_r = _rope_tables_ret()
    q_r, k_r, v_r, g_r = _proj_c(x, mods, cos_r, sin_r, w_c[:, :qk], w_c[:, qk:2 * qk],
                                 w_c[:, 2 * qk:2 * qk + MIX_C], w_c[:, 2 * qk + MIX_C:])
    gam = jnp.broadcast_to(ret_gamma_logit[o].astype(F32)[:, :, None, None], (2, RET_HEADS, SUBLANE, LANE))
    r0_p = jnp.zeros((BATCH, 2, RET_HEADS, RET_DK, RET_DV), F32)
    o_ret_p, r_fin = _retention(q_r, k_r, v_r, gam, r0_p, n_seq=BATCH, seq=SEQ, row_block0=0, with_state=True)
    (o_ret_s,) = _retention(q_r, k_r, v_r, gam, state_ret[:, o],
                            n_seq=DEC_BATCH, seq=DEC_SEQ, row_block0=N_PROMPT // DEC_SEQ, with_state=False)
    x1, h2, lgt = _out_c(o_ret_p, o_ret_s, g_r, w_out_c[o].astype(BF16), x, mods,
                         ln_g[l, 0].reshape(1, D_MODEL), ln_b[l, 0].reshape(1, D_MODEL), wr_hi[l], wr_lo[l])
    y_p, y_s = _moe_and_norm(x1, h2, lgt, mods, ln_g[l, 1].reshape(1, D_MODEL), ln_b[l, 1].reshape(1, D_MODEL),
                             router_bias[l], tri, wg_e, wu_e, wd_e, wsg[l], wsu[l], wsd[l], layer=l, split=True)

    y_prompt = y_p.reshape(BATCH, SEQ, D_MODEL)
    y_sample = y_s.reshape(DEC_BATCH, DEC_SEQ, D_MODEL)
    new_ret = r_fin.reshape(BATCH, 1, 2, RET_HEADS, RET_DK, RET_DV)
    return (y_prompt, y_sample, new_ckv, new_krope, new_rg, new_ret)
```

```python
import functools
import math

import jax
import jax.numpy as jnp
from jax import lax
from jax.experimental import pallas as pl
from jax.experimental.pallas import tpu as pltpu

F32 = jnp.float32
BF16 = jnp.bfloat16
I32 = jnp.int32

D_MODEL = 1024
BATCH, SEQ = 16, 256
DEC_BATCH, DEC_SEQ = 8, 4096
PAST_LEN = 256
DEPTH = 2
GRID_W = 64
RG_W, RG_BLOCKS = 512, 8
RG_BW = RG_W // RG_BLOCKS
RG_C = 8.0
CONV_W, CONV_LEFT = 4, 2
MLA_HEADS, QK_NOPE, QK_ROPE, V_HEAD = 8, 64, 32, 64
Q_LORA, KV_LORA = 768, 256
ROPE_BASE = 10000.0
ATTN_SCALE = (QK_NOPE + QK_ROPE) ** -0.5
RET_HEADS, RET_DK, RET_DV, RET_CHUNK = 4, 256, 512, 128
MIX_C = RET_HEADS * RET_DV
N_EXPERTS, TOP_K, N_GROUPS, TOPK_GROUPS = 64, 8, 8, 4
GROUP_SIZE = N_EXPERTS // N_GROUPS
D_EXPERT = 256
ROUTED_SCALE = 2.5
ALPHA = (2 * DEPTH) ** 0.25
EPS = 1e-6

N_PROMPT = BATCH * SEQ
N_SAMPLE = DEC_BATCH * DEC_SEQ
N_TOK = N_PROMPT + N_SAMPLE
N_COND = 1 + DEC_BATCH
MOD_ROWS = 8

LANE = 128
SUBLANE = 8
TM = 512
HEAD_PAD = 128
ROPE_LANE0 = QK_NOPE
TQ = 256
KC = 256
AH = 2
ATTN_UNROLL = True
RET_UNROLL = 4
V_ONES = 16
Q_PRESCALE = ATTN_SCALE * math.log2(math.e)
SCAN_ROWS = 64
GATE_ROWS = 256
WIN = 256
N_WIN = N_TOK // WIN
PIECE = 16
SORT_ROWS = 256
RL = 3072
TME = 1024
PIECES = TME // PIECE
N_PAIRS = N_TOK * TOP_K
N_EBLOCKS = (N_PAIRS + N_WIN * N_EXPERTS * (PIECE - 1)) // TME + N_EXPERTS
NEG = -1e30


def _cparams(sem, vmem_mb=48):
    return pltpu.CompilerParams(dimension_semantics=sem, vmem_limit_bytes=vmem_mb * 1024 * 1024)


def _cond_block(i, tm):
    npb = N_PROMPT // tm
    return jnp.where(i < npb, 0, 1 + (i - npb) // (DEC_SEQ // tm))


def _pos_block(i, tm):
    npb = N_PROMPT // tm
    return jnp.where(i < npb, 0, 1 + (i - npb) % (DEC_SEQ // tm))


def _split_hi_lo(a):
    hi = a.astype(BF16)
    lo = (a - hi.astype(F32)).astype(BF16)
    return hi, lo


def _dot(a, b):
    return jnp.dot(a, b, preferred_element_type=F32)


def _dot_nt(a, b):
    return lax.dot_general(a, b, (((1,), (1,)), ((), ())), preferred_element_type=F32)


def _silu(x):
    return x * jax.nn.sigmoid(x)


def _gelu_tanh(x):
    return 0.5 * x * (1.0 + jnp.tanh(math.sqrt(2.0 / math.pi) * (x + 0.044715 * (x * x * x))))


def _softplus(x):
    return jnp.maximum(x, 0.0) + jnp.log1p(jnp.exp(-jnp.abs(x)))


def _layernorm_rows(z, g, b):
    mu = jnp.mean(z, axis=-1, keepdims=True)
    zc = z - mu
    var = jnp.mean(zc * zc, axis=-1, keepdims=True)
    return (zc * lax.rsqrt(var + EPS)) * g + b


def _ada_kernel(c_ref, w_ref, b_ref, o_ref):
    s_hi, s_lo = _split_hi_lo(_silu(c_ref[...]))
    w_hi, w_lo = _split_hi_lo(w_ref[...])
    o_ref[...] = _dot(s_hi, w_hi) + _dot(s_hi, w_lo) + _dot(s_lo, w_hi) + b_ref[...]


def _ada_modulation(cond, w_ada, b_ada):
    n6 = 6 * D_MODEL
    tn = D_MODEL
    return pl.pallas_call(
        _ada_kernel,
        out_shape=jax.ShapeDtypeStruct((DEPTH, 16, n6), F32),
        grid=(DEPTH, n6 // tn),
        in_specs=[
            pl.BlockSpec((16, D_MODEL), lambda l, j: (0, 0)),
            pl.BlockSpec((None, D_MODEL, tn), lambda l, j: (l, 0, j)),
            pl.BlockSpec((None, 1, tn), lambda l, j: (l, 0, j)),
        ],
        out_specs=pl.BlockSpec((None, 16, tn), lambda l, j: (l, 0, j)),
        compiler_params=_cparams(("arbitrary", "arbitrary")),
        name="ada_modulation",
    )(cond, w_ada, b_ada.reshape(DEPTH, 1, n6))


def _modulated(x, mod_ref, shift_row, scale_row):
    return x * (1.0 + mod_ref[scale_row:scale_row + 1, :]) + mod_ref[shift_row:shift_row + 1, :]


def _row_pair_specs(tm, width, col=0):
    npb = N_PROMPT // tm
    return [pl.BlockSpec((tm, width), lambda i, *_: (jnp.minimum(i, npb - 1), col)),
            pl.BlockSpec((tm, width), lambda i, *_: (jnp.maximum(i - npb, 0), col))]


def _pick_rows(p_ref, s_ref, tm):
    return jnp.where(pl.program_id(0) < N_PROMPT // tm, p_ref[...], s_ref[...])


def _proj_ab_kernel(xp_ref, xs_ref, mod_ref, w_ref, wkr_ref, main_ref, kr_ref):
    h = _modulated(_pick_rows(xp_ref, xs_ref, TM), mod_ref, 0, 1).astype(BF16)
    n = w_ref.shape[1]
    step = 512
    for j in range(n // step):
        main_ref[:, j * step:(j + 1) * step] = _dot(h, w_ref[:, j * step:(j + 1) * step])
    kr_ref[...] = _dot(h, wkr_ref[...])


def _proj_ab(xp, xs, mods, w_main, w_kr):
    n = w_main.shape[1]
    return pl.pallas_call(
        _proj_ab_kernel,
        out_shape=(jax.ShapeDtypeStruct((N_TOK, n), F32), jax.ShapeDtypeStruct((N_TOK, LANE), F32)),
        grid=(N_TOK // TM,),
        in_specs=_row_pair_specs(TM, D_MODEL) + [
            pl.BlockSpec((None, MOD_ROWS, D_MODEL), lambda i: (_cond_block(i, TM), 0, 0)),
            pl.BlockSpec((D_MODEL, n), lambda i: (0, 0)),
            pl.BlockSpec((D_MODEL, LANE), lambda i: (0, 0)),
        ],
        out_specs=(pl.BlockSpec((TM, n), lambda i: (i, 0)), pl.BlockSpec((TM, LANE), lambda i: (i, 0))),
        compiler_params=_cparams(("arbitrary",)),
        name="proj_ab",
    )(xp, xs, mods, w_main, w_kr)


def _rglru_kernel(xr_ref, gr_ref, cw_ref, cb_ref, wg_ref, bg_ref, lam_ref, h0_ref,
                  y_ref, hfin_ref, xpad, a_s, b_s, *, seq):
    pad = SUBLANE
    xpad[0:pad, :] = jnp.zeros((pad, LANE), F32)
    xpad[seq + pad:seq + 2 * pad, :] = jnp.zeros((pad, LANE), F32)
    xpad[pad:seq + pad, :] = xr_ref[...]

    sp = _softplus(-lam_ref[...])
    cw = cw_ref[...]
    cb = cb_ref[...]
    wg = wg_ref[...]
    bg = bg_ref[...]

    def gate_step(c, carry):
        t0 = pl.multiple_of(c * GATE_ROWS, GATE_ROWS)
        win = xpad[pl.ds(t0, GATE_ROWS + 2 * pad), :]
        xc = cb
        for j in range(CONV_W):
            off = pad - CONV_LEFT + j
            xc = xc + win[off:off + GATE_ROWS, :] * cw[j:j + 1, :]
        g = _dot(xc.astype(BF16), wg) + bg
        for d in range(2):
            r = jax.nn.sigmoid(g[:, (2 * d) * LANE:(2 * d + 1) * LANE])
            i = jax.nn.sigmoid(g[:, (2 * d + 1) * LANE:(2 * d + 2) * LANE])
            log_a = (-RG_C * r) * sp[d:d + 1, :]
            a = jnp.exp(log_a)
            t = jnp.tanh(log_a)
            bt = jnp.sqrt(2.0 * t / (t - 1.0)) * (i * xc)
            a_s[d, pl.ds(t0, GATE_ROWS), :] = a
            b_s[d, pl.ds(t0, GATE_ROWS), :] = bt
        return carry

    lax.fori_loop(0, seq // GATE_ROWS, gate_step, 0, unroll=min(2, seq // GATE_ROWS))

    row = lax.broadcasted_iota(I32, (SCAN_ROWS, LANE), 0) % SUBLANE
    n_steps = seq // SCAN_ROWS
    tiles = SCAN_ROWS // SUBLANE

    def local_scan(a, b, reverse):
        for k in (1, 2, 4):
            if reverse:
                ok = row < SUBLANE - k
                shift = SCAN_ROWS - k
            else:
                ok = row >= k
                shift = k
            a_sh = jnp.where(ok, pltpu.roll(a, shift, 0), 1.0)
            b_sh = jnp.where(ok, pltpu.roll(b, shift, 0), 0.0)
            b = a * b_sh + b
            a = a * a_sh
        return a, b

    def fwd_step(c, h):
        t0 = pl.multiple_of(c * SCAN_ROWS, SCAN_ROWS)
        a, b = local_scan(a_s[0, pl.ds(t0, SCAN_ROWS), :], b_s[0, pl.ds(t0, SCAN_ROWS), :], False)
        outs = []
        for j in range(tiles):
            hj = a[j * SUBLANE:(j + 1) * SUBLANE, :] * h + b[j * SUBLANE:(j + 1) * SUBLANE, :]
            outs.append(hj)
            h = hj[SUBLANE - 1:SUBLANE, :]
        y_ref[pl.ds(t0, SCAN_ROWS), :] = jnp.concatenate(outs, axis=0)
        return h

    h_f = lax.fori_loop(0, n_steps, fwd_step, h0_ref[0:1, :], unroll=4)

    def bwd_step(c, h):
        t0 = pl.multiple_of((n_steps - 1 - c) * SCAN_ROWS, SCAN_ROWS)
        a, b = local_scan(a_s[1, pl.ds(t0, SCAN_ROWS), :], b_s[1, pl.ds(t0, SCAN_ROWS), :], True)
        outs = [None] * tiles
        for j in reversed(range(tiles)):
            hj = a[j * SUBLANE:(j + 1) * SUBLANE, :] * h + b[j * SUBLANE:(j + 1) * SUBLANE, :]
            outs[j] = hj
            h = hj[0:1, :]
        hb = jnp.concatenate(outs, axis=0)
        y_ref[pl.ds(t0, SCAN_ROWS), :] = (y_ref[pl.ds(t0, SCAN_ROWS), :] + hb) * _gelu_tanh(gr_ref[pl.ds(t0, SCAN_ROWS), :])
        return h

    h_b = lax.fori_loop(0, n_steps, bwd_step, h0_ref[1:2, :], unroll=4)
    hfin_ref[0:1, :] = h_f
    hfin_ref[1:2, :] = h_b


def _rglru(main, cw, cb, wg, bg, lam, h0, *, n_seq, seq, row_block0):
    n_ct = RG_W // LANE
    gr_col0 = RG_W // LANE
    kern = functools.partial(_rglru_kernel, seq=seq)
    return pl.pallas_call(
        kern,
        out_shape=(jax.ShapeDtypeStruct((n_seq * seq, RG_W), F32), jax.ShapeDtypeStruct((n_seq, 2, RG_W), F32)),
        grid=(n_seq, n_ct),
        in_specs=[
            pl.BlockSpec((seq, LANE), lambda b, c: (row_block0 + b, c)),
            pl.BlockSpec((seq, LANE), lambda b, c: (row_block0 + b, gr_col0 + c)),
            pl.BlockSpec((CONV_W, LANE), lambda b, c: (0, c)),
            pl.BlockSpec((1, LANE), lambda b, c: (0, c)),
            pl.BlockSpec((None, LANE, 4 * LANE), lambda b, c: (c, 0, 0)),
            pl.BlockSpec((None, 1, 4 * LANE), lambda b, c: (c, 0, 0)),
            pl.BlockSpec((2, LANE), lambda b, c: (0, c)),
            pl.BlockSpec((None, 2, LANE), lambda b, c: (b, 0, c)),
        ],
        out_specs=(
            pl.BlockSpec((seq, LANE), lambda b, c: (b, c)),
            pl.BlockSpec((None, 2, LANE), lambda b, c: (b, 0, c)),
        ),
        scratch_shapes=[
            pltpu.VMEM((seq + 2 * SUBLANE, LANE), F32),
            pltpu.VMEM((2, seq, LANE), F32),
            pltpu.VMEM((2, seq, LANE), F32),
        ],
        compiler_params=_cparams(("arbitrary", "arbitrary")),
        name=f"rglru_s{seq}",
    )(main, main, cw, cb, wg, bg, lam, h0)


def _rope_lanes(x, cos, sin_m, sin_p):
    n = x.shape[1] // LANE
    half = QK_ROPE // 4
    cos_t = jnp.concatenate([cos] * n, axis=1) if n > 1 else cos
    sm_t = jnp.concatenate([sin_m] * n, axis=1) if n > 1 else sin_m
    sp_t = jnp.concatenate([sin_p] * n, axis=1) if n > 1 else sin_p
    up = pltpu.roll(x, x.shape[1] - half, 1)
    dn = pltpu.roll(x, half, 1)
    return x * cos_t + up * sm_t + dn * sp_t


def _mla_prep_kernel(cq0_ref, cq1_ref, cq2_ref, ckv_ref, kr_ref, cos_ref, sm_ref, sp_ref,
                     qn_ref, wq_ref, kvn_ref, wuk_ref, wuvt_ref,
                     q_ref, k_ref, vt_ref, ckvn_ref):
    cq = [cq0_ref[...], cq1_ref[...], cq2_ref[...]]
    ms = (jnp.sum(cq[0] * cq[0], axis=-1, keepdims=True) + jnp.sum(cq[1] * cq[1], axis=-1, keepdims=True)
          + jnp.sum(cq[2] * cq[2], axis=-1, keepdims=True)) * (1.0 / Q_LORA)
    inv = lax.rsqrt(ms + EPS)
    blk = Q_LORA // 3
    q = None
    for j in range(3):
        cqn = ((cq[j] * inv) * qn_ref[:, j * blk:(j + 1) * blk]).astype(BF16)
        part = _dot(cqn, wq_ref[j * blk:(j + 1) * blk, :])
        q = part if q is None else q + part
    cos, sm, sp = cos_ref[...], sm_ref[...], sp_ref[...]
    q_ref[...] = (_rope_lanes(q, cos, sm, sp) * Q_PRESCALE).T.astype(BF16)

    ckv = ckv_ref[...]
    inv_kv = lax.rsqrt(jnp.mean(ckv * ckv, axis=-1, keepdims=True) + EPS)
    ckvn = (ckv * inv_kv) * kvn_ref[...]

    @pl.when(pl.program_id(0) < N_PROMPT // TM)
    def _():
        ckvn_ref[...] = ckvn

    ckvn_b = ckvn.astype(BF16)
    kr_rot = _rope_lanes(kr_ref[...], cos, sm, sp)
    k_ref[...] = (_dot(ckvn_b, wuk_ref[...]) + jnp.concatenate([kr_rot] * MLA_HEADS, axis=1)).astype(BF16)
    vt = _dot_nt(wuvt_ref[...], ckvn_b).astype(BF16)
    for c in range(vt_ref.shape[0]):
        vt_ref[c] = vt[:, c * KC:(c + 1) * KC]


def _mla_prep(main, krp, cos_t, sm_t, sp_t, q_norm, wq_p, kv_norm, wuk_p, wuvt):
    cq_col0 = 2 * RG_W // 256
    hp = MLA_HEADS * HEAD_PAD
    full = lambda shape: pl.BlockSpec(shape, lambda i: (0,) * len(shape))
    tab = pl.BlockSpec((TM, LANE), lambda i: (_pos_block(i, TM), 0))
    return pl.pallas_call(
        _mla_prep_kernel,
        out_shape=(
            jax.ShapeDtypeStruct((hp, N_TOK), BF16),
            jax.ShapeDtypeStruct((N_TOK, hp), BF16),
            jax.ShapeDtypeStruct((N_TOK // KC, MLA_HEADS * V_HEAD, KC), BF16),
            jax.ShapeDtypeStruct((N_PROMPT, KV_LORA), F32),
        ),
        grid=(N_TOK // TM,),
        in_specs=[
            pl.BlockSpec((TM, 256), lambda i: (i, cq_col0)),
            pl.BlockSpec((TM, 256), lambda i: (i, cq_col0 + 1)),
            pl.BlockSpec((TM, 256), lambda i: (i, cq_col0 + 2)),
            pl.BlockSpec((TM, 256), lambda i: (i, cq_col0 + 3)),
            pl.BlockSpec((TM, LANE), lambda i: (i, 0)),
            tab, tab, tab,
            full((1, Q_LORA)), full((Q_LORA, hp)), full((1, KV_LORA)), full((KV_LORA, hp)),
            full((MLA_HEADS * V_HEAD, KV_LORA)),
        ],
        out_specs=(
            pl.BlockSpec((hp, TM), lambda i: (0, i)),
            pl.BlockSpec((TM, hp), lambda i: (i, 0)),
            pl.BlockSpec((TM // KC, MLA_HEADS * V_HEAD, KC), lambda i: (i, 0, 0)),
            pl.BlockSpec((TM, KV_LORA), lambda i: (jnp.minimum(i, N_PROMPT // TM - 1), 0)),
        ),
        compiler_params=_cparams(("arbitrary",)),
        name="mla_prep",
    )(main, main, main, main, krp, cos_t, sm_t, sp_t, q_norm, wq_p, kv_norm, wuk_p, wuvt)


def _mla_ctx_kernel(ckv_ref, kr_ref, wuk_ref, wuvt_ref, k_ref, vt_ref):
    ckv_b = ckv_ref[...].astype(BF16)
    k_ref[...] = (_dot(ckv_b, wuk_ref[...]) + jnp.concatenate([kr_ref[...]] * MLA_HEADS, axis=1)).astype(BF16)
    vt_ref[...] = _dot_nt(wuvt_ref[...], ckv_b).astype(BF16)


def _mla_ctx(ctx_ckv, ctx_krp, wuk_p, wuvt):
    n = ctx_ckv.shape[0]
    hp = MLA_HEADS * HEAD_PAD
    tm = KC
    full = lambda shape: pl.BlockSpec(shape, lambda i: (0,) * len(shape))
    return pl.pallas_call(
        _mla_ctx_kernel,
        out_shape=(jax.ShapeDtypeStruct((n, hp), BF16),
                   jax.ShapeDtypeStruct((n // tm, MLA_HEADS * V_HEAD, tm), BF16)),
        grid=(n // tm,),
        in_specs=[
            pl.BlockSpec((tm, KV_LORA), lambda i: (i, 0)),
            pl.BlockSpec((tm, LANE), lambda i: (i, 0)),
            full((KV_LORA, hp)), full((MLA_HEADS * V_HEAD, KV_LORA)),
        ],
        out_specs=(pl.BlockSpec((tm, hp), lambda i: (i, 0)),
                   pl.BlockSpec((None, MLA_HEADS * V_HEAD, tm), lambda i: (i, 0, 0))),
        compiler_params=_cparams(("arbitrary",)),
        name="mla_ctx",
    )(ctx_ckv, ctx_krp, wuk_p, wuvt)


def _attn_kernel(*refs, seq, tq, n_ctx):
    if n_ctx:
        q_ref, k_ref, vt_ref, kc_ref, vtc_ref, o_ref, s_scr, p_scr, k_all, vt_all = refs
    else:
        q_ref, k_ref, vt_ref, o_ref, s_scr, p_scr, k_all, vt_all = refs
    has_ctx = 1 if n_ctx else 0
    n_own = seq // KC
    n = n_own + has_ctx
    qs = [q_ref[h * HEAD_PAD:(h + 1) * HEAD_PAD, :] for h in range(AH)]

    @pl.when(pl.program_id(2) == 0)
    def _():
        k_all[0:seq, :] = k_ref[...]
        if has_ctx:
            k_all[seq:seq + KC, :] = kc_ref[...]
        for h in range(AH):
            vt_all[0:n_own, h, 0:V_HEAD, :] = vt_ref[:, h * V_HEAD:(h + 1) * V_HEAD, :]
            if has_ctx:
                vt_all[n_own, h, 0:V_HEAD, :] = vtc_ref[h * V_HEAD:(h + 1) * V_HEAD, :]
            vt_all[:, h, V_HEAD:V_HEAD + V_ONES, :] = jnp.ones((n, V_ONES, KC), BF16)

    def k_chunk(c, h):
        t0 = c * KC if isinstance(c, int) else pl.multiple_of(c * KC, KC)
        return k_all[pl.ds(t0, KC), h * HEAD_PAD:(h + 1) * HEAD_PAD]

    def scores(c, slot):
        for h in range(AH):
            s_scr[slot, h] = _dot(k_chunk(c, h), qs[h])

    def softmax_chunk(slot, st):
        out = []
        for h in range(AH):
            m, a1, _, acc = st[h]
            t = s_scr[slot, h]
            m_new = jnp.maximum(m, jnp.max(t, axis=0, keepdims=True))
            p_scr[slot, h] = jnp.exp2(t - m_new).astype(BF16)
            out.append((m_new, jnp.exp2(m - m_new), a1, acc))
        return out

    def weighted_values(c, slot, st, alphas):
        return [st[h][:3] + (alphas[h] * st[h][3] + _dot(vt_all[c, h], p_scr[slot, h]),) for h in range(AH)]

    def step(c, r, st, with_s, with_v):
        if with_s:
            scores(c + 2, (r + 2) % 3)
        alphas = [st[h][2] for h in range(AH)]
        st = softmax_chunk(r, st)
        if with_v:
            prev = max(c - 2, 0) if isinstance(c, int) else jnp.maximum(c - 2, 0)
            st = weighted_values(prev, (r + 1) % 3, st, alphas)
        return st

    p_scr[1] = jnp.zeros(p_scr.shape[1:], BF16)
    p_scr[2] = jnp.zeros(p_scr.shape[1:], BF16)
    one = jnp.ones((1, tq), F32)
    st = [(jnp.full((1, tq), NEG, F32), one, one, jnp.zeros((V_HEAD + V_ONES, tq), F32)) for _ in range(AH)]
    scores(0, 0)
    if n > 1:
        scores(1, 1)
    n_trip = 0 if ATTN_UNROLL else max(n - 2, 0) // 3

    def trip(j, flat):
        st = [tuple(flat[4 * h:4 * h + 4]) for h in range(AH)]
        for r in range(3):
            st = step(3 * j + r, r, st, True, True)
        return sum((tuple(s) for s in st), ())

    if n_trip:
        flat = lax.fori_loop(0, n_trip, trip, sum((tuple(s) for s in st), ()))
        st = [tuple(flat[4 * h:4 * h + 4]) for h in range(AH)]
    for c in range(3 * n_trip, n):
        st = step(c, c % 3, st, c + 2 < n, c >= 2)
    if n > 1:
        st = weighted_values(n - 2, (n - 2) % 3, st, [st[h][2] for h in range(AH)])
    st = weighted_values(n - 1, (n - 1) % 3, st, [st[h][1] for h in range(AH)])
    for h in range(AH):
        acc = st[h][3]
        o_ref[:, h * V_HEAD:(h + 1) * V_HEAD] = (acc[0:V_HEAD] / acc[V_HEAD:V_HEAD + 1]).T.astype(o_ref.dtype)


def _attention(q, k, vt, kc, vtc, *, n_seq, seq, row_block0, tq):
    n_ctx = 0 if kc is None else PAST_LEN
    n_hp = MLA_HEADS // AH
    nq = seq // tq
    kern = functools.partial(_attn_kernel, seq=seq, tq=tq, n_ctx=n_ctx)
    in_specs = [
        pl.BlockSpec((AH * HEAD_PAD, tq), lambda b, j, i: (j, (row_block0 + b) * nq + i)),
        pl.BlockSpec((seq, AH * HEAD_PAD), lambda b, j, i: (row_block0 + b, j)),
        pl.BlockSpec((seq // KC, AH * V_HEAD, KC), lambda b, j, i: (row_block0 + b, j, 0)),
    ]
    args = [q, k, vt]
    if n_ctx:
        in_specs += [
            pl.BlockSpec((n_ctx, AH * HEAD_PAD), lambda b, j, i: (b, j)),
            pl.BlockSpec((None, AH * V_HEAD, KC), lambda b, j, i: (b, j, 0)),
        ]
        args += [kc, vtc]
    return pl.pallas_call(
        kern,
        out_shape=jax.ShapeDtypeStruct((n_seq * seq, MLA_HEADS * V_HEAD), BF16),
        grid=(n_seq, n_hp, nq),
        in_specs=in_specs,
        out_specs=pl.BlockSpec((tq, AH * V_HEAD), lambda b, j, i: (b * nq + i, j)),
        scratch_shapes=[pltpu.VMEM((3, AH, KC, tq), F32), pltpu.VMEM((3, AH, KC, tq), BF16),
                        pltpu.VMEM((seq + n_ctx, AH * HEAD_PAD), BF16),
                        pltpu.VMEM(((seq + n_ctx) // KC, AH, V_HEAD + V_ONES, KC), BF16)],
        compiler_params=_cparams(("arbitrary", "arbitrary", "arbitrary")),
        name=f"mla_attention_s{seq}",
    )(*args)


def _post_mixer(y, x, mod_ref, lng_ref, lnb_ref, wrh_ref, wrl_ref, x1_ref, h2_ref, lgt_ref):
    z = ALPHA * x + mod_ref[2:3, :] * y
    x1 = _layernorm_rows(z, lng_ref[...], lnb_ref[...])
    x1_ref[...] = x1
    h2 = _modulated(x1, mod_ref, 3, 4)
    h_hi, h_lo = _split_hi_lo(h2)
    h2_ref[...] = h_hi
    w_hi, w_lo = wrh_ref[...], wrl_ref[...]
    lgt_ref[...] = _dot_nt(w_hi, h_hi) + _dot_nt(w_hi, h_lo) + _dot_nt(w_lo, h_hi)


def _out_ab_kernel(yrgp_ref, yrgs_ref, op_ref, os_ref, wa_ref, wb_ref, xp_ref, xs_ref,
                   mod_ref, lng_ref, lnb_ref, wrh_ref, wrl_ref, x1_ref, h2_ref, lgt_ref):
    y = (_dot(_pick_rows(yrgp_ref, yrgs_ref, TM).astype(BF16), wa_ref[...])
         + _dot(_pick_rows(op_ref, os_ref, TM), wb_ref[...]))
    _post_mixer(y, _pick_rows(xp_ref, xs_ref, TM), mod_ref, lng_ref, lnb_ref, wrh_ref, wrl_ref,
                x1_ref, h2_ref, lgt_ref)


def _post_specs():
    full = lambda shape: pl.BlockSpec(shape, lambda i: (0,) * len(shape))
    in_specs = [
        pl.BlockSpec((None, MOD_ROWS, D_MODEL), lambda i: (_cond_block(i, TM), 0, 0)),
        full((1, D_MODEL)), full((1, D_MODEL)),
        full((N_EXPERTS, D_MODEL)), full((N_EXPERTS, D_MODEL)),
    ]
    out_shape = (
        jax.ShapeDtypeStruct((N_TOK, D_MODEL), F32),
        jax.ShapeDtypeStruct((N_TOK, D_MODEL), BF16),
        jax.ShapeDtypeStruct((N_EXPERTS, N_TOK), F32),
    )
    out_specs = (
        pl.BlockSpec((TM, D_MODEL), lambda i: (i, 0)),
        pl.BlockSpec((TM, D_MODEL), lambda i: (i, 0)),
        pl.BlockSpec((N_EXPERTS, TM), lambda i: (0, i)),
    )
    return in_specs, out_shape, out_specs


def _out_ab(yrg_p, yrg_s, o_p, o_s, w_a, w_b, xp, xs, mods, lng, lnb, wr_hi, wr_lo):
    full = lambda shape: pl.BlockSpec(shape, lambda i: (0,) * len(shape))
    post_in, out_shape, out_specs = _post_specs()
    return pl.pallas_call(
        _out_ab_kernel,
        out_shape=out_shape,
        grid=(N_TOK // TM,),
        in_specs=(_row_pair_specs(TM, RG_W) + _row_pair_specs(TM, MLA_HEADS * V_HEAD)
                  + [full((RG_W, D_MODEL)), full((MLA_HEADS * V_HEAD, D_MODEL))]
                  + _row_pair_specs(TM, D_MODEL) + post_in),
        out_specs=out_specs,
        compiler_params=_cparams(("arbitrary",)),
        name="out_ab",
    )(yrg_p, yrg_s, o_p, o_s, w_a, w_b, xp, xs, mods, lng, lnb, wr_hi, wr_lo)


def _proj_c_kernel(x_ref, mod_ref, cos_ref, sin_ref, wq_ref, wk_ref, wv_ref, wg_ref, q_ref, k_ref, v_ref, g_ref):
    h = _modulated(x_ref[...], mod_ref, 0, 1).astype(BF16)
    cos, sin = cos_ref[...], sin_ref[...]
    half = RET_DK // 2
    for hd in range(RET_HEADS):
        for w_ref, is_k in ((wq_ref, False), (wk_ref, True)):
            p = _dot(h, w_ref[:, hd * RET_DK:(hd + 1) * RET_DK])
            x1, x2 = p[:, :half], p[:, half:]
            r1 = x1 * cos - x2 * sin
            r2 = x2 * cos + x1 * sin
            if not is_k:
                q_ref[:, hd * RET_DK:hd * RET_DK + half] = r1.astype(BF16)
                q_ref[:, hd * RET_DK + half:(hd + 1) * RET_DK] = r2.astype(BF16)
            else:
                t1 = (r1 * RET_DK ** -0.5).T.astype(BF16)
                t2 = (r2 * RET_DK ** -0.5).T.astype(BF16)
                for c in range(k_ref.shape[0]):
                    k_ref[c, hd * RET_DK:hd * RET_DK + half, :] = t1[:, c * RET_CHUNK:(c + 1) * RET_CHUNK]
                    k_ref[c, hd * RET_DK + half:(hd + 1) * RET_DK, :] = t2[:, c * RET_CHUNK:(c + 1) * RET_CHUNK]
    step = 512
    for j in range(MIX_C // step):
        v_ref[:, j * step:(j + 1) * step] = _dot(h, wv_ref[:, j * step:(j + 1) * step]).astype(BF16)
        g_ref[:, j * step:(j + 1) * step] = _dot(h, wg_ref[:, j * step:(j + 1) * step])


def _proj_c(x, mods, cos_t, sin_t, wq, wk, wv, wg):
    full = lambda shape: pl.BlockSpec(shape, lambda i: (0,) * len(shape))
    qk = RET_HEADS * RET_DK
    tab = pl.BlockSpec((TM, RET_DK // 2), lambda i: (_pos_block(i, TM), 0))
    return pl.pallas_call(
        _proj_c_kernel,
        out_shape=(
            jax.ShapeDtypeStruct((N_TOK, qk), BF16), jax.ShapeDtypeStruct((N_TOK // RET_CHUNK, qk, RET_CHUNK), BF16),
            jax.ShapeDtypeStruct((N_TOK, MIX_C), BF16), jax.ShapeDtypeStruct((N_TOK, MIX_C), F32),
        ),
        grid=(N_TOK // TM,),
        in_specs=[
            pl.BlockSpec((TM, D_MODEL), lambda i: (i, 0)),
            pl.BlockSpec((None, MOD_ROWS, D_MODEL), lambda i: (_cond_block(i, TM), 0, 0)),
            tab, tab,
            full((D_MODEL, qk)), full((D_MODEL, qk)), full((D_MODEL, MIX_C)), full((D_MODEL, MIX_C)),
        ],
        out_specs=(
            pl.BlockSpec((TM, qk), lambda i: (i, 0)),
            pl.BlockSpec((TM // RET_CHUNK, qk, RET_CHUNK), lambda i: (i, 0, 0)),
            pl.BlockSpec((TM, MIX_C), lambda i: (i, 0)), pl.BlockSpec((TM, MIX_C), lambda i: (i, 0)),
        ),
        compiler_params=_cparams(("arbitrary",), 56),
        name="proj_c",
    )(x, mods, cos_t, sin_t, wq, wk, wv, wg)


def _retention_kernel(*refs, seq, with_state):
    if with_state:
        q_ref, kt_ref, v_ref, gam_ref, r0_ref, o_ref, rfin_ref, r_s = refs
    else:
        q_ref, kt_ref, v_ref, gam_ref, r0_ref, o_ref, r_s = refs
        rfin_ref = None
    c = RET_CHUNK
    n = seq // c
    ii = lax.broadcasted_iota(I32, (c, c), 0).astype(F32)
    jj = lax.broadcasted_iota(I32, (c, c), 1).astype(F32)
    ci = lax.broadcasted_iota(I32, (c, 1), 0).astype(F32)
    li = lax.broadcasted_iota(I32, (1, c), 1).astype(F32)

    consts = []
    for d in range(2):
        gam = gam_ref[d]
        lg_row = -_softplus(-gam[0:1, :])
        lg = jnp.broadcast_to(lg_row, (c, c))
        lg_col = jnp.broadcast_to(lg_row[:, 0:1], (c, 1))
        if d == 0:
            diff = ii - jj
            xi = jnp.exp((ci + 1.0) * lg_col)
            zeta = jnp.exp((c - 1.0 - li) * lg_row)
        else:
            diff = jj - ii
            xi = jnp.exp((c - ci) * lg_col)
            zeta = jnp.exp(li * lg_row)
        dmat = jnp.where(diff >= 0, jnp.exp(jnp.maximum(diff, 0.0) * lg), 0.0)
        g_chunk = jnp.exp(float(c) * lg_row[:, 0:1])
        consts.append((dmat, xi, zeta, g_chunk))
        r_s[d] = r0_ref[d]

    def chunk(d, idx, accumulate):
        dmat, xi, zeta, g_chunk = consts[d]
        t0 = pl.multiple_of(idx * c, c)
        qb = q_ref[pl.ds(t0, c), :]
        kt = kt_ref[idx]
        vb = v_ref[pl.ds(t0, c), :]
        r = r_s[d]
        inner = _dot(qb, kt) * dmat
        o = _dot(inner.astype(BF16), vb) + _dot((qb.astype(F32) * xi).astype(BF16), r.astype(BF16))
        r_s[d] = r * g_chunk + _dot((kt.astype(F32) * zeta).astype(BF16), vb)
        if accumulate:
            o_ref[pl.ds(t0, c), :] = o_ref[pl.ds(t0, c), :] + o
        else:
            o_ref[pl.ds(t0, c), :] = o

    def first_half(s, carry):
        chunk(0, s, False)
        chunk(1, n - 1 - s, False)
        return carry

    def second_half(s, carry):
        chunk(0, s, True)
        chunk(1, n - 1 - s, True)
        return carry

    unroll = min(RET_UNROLL, n // 2)
    lax.fori_loop(0, n // 2, first_half, 0, unroll=unroll)
    lax.fori_loop(n // 2, n, second_half, 0, unroll=unroll)
    if with_state:
        for d in range(2):
            rfin_ref[d] = r_s[d]


def _retention(q, k, v, gam, r0, *, n_seq, seq, row_block0, with_state):
    kern = functools.partial(_retention_kernel, seq=seq, with_state=with_state)
    out_shape = [jax.ShapeDtypeStruct((n_seq * seq, MIX_C), F32)]
    out_specs = [pl.BlockSpec((seq, RET_DV), lambda b, h: (b, h))]
    if with_state:
        out_shape.append(jax.ShapeDtypeStruct((n_seq, 2, RET_HEADS, RET_DK, RET_DV), F32))
        out_specs.append(pl.BlockSpec((None, 2, None, RET_DK, RET_DV), lambda b, h: (b, 0, h, 0, 0)))
    return pl.pallas_call(
        kern,
        out_shape=tuple(out_shape),
        grid=(n_seq, RET_HEADS),
        in_specs=[
            pl.BlockSpec((seq, RET_DK), lambda b, h: (row_block0 + b, h)),
            pl.BlockSpec((seq // RET_CHUNK, RET_DK, RET_CHUNK), lambda b, h: (row_block0 + b, h, 0)),
            pl.BlockSpec((seq, RET_DV), lambda b, h: (row_block0 + b, h)),
            pl.BlockSpec((2, None, SUBLANE, LANE), lambda b, h: (0, h, 0, 0)),
            pl.BlockSpec((None, 2, None, RET_DK, RET_DV), lambda b, h: (b, 0, h, 0, 0)),
        ],
        out_specs=tuple(out_specs),
        scratch_shapes=[pltpu.VMEM((2, RET_DK, RET_DV), F32)],
        compiler_params=_cparams(("arbitrary", "arbitrary"), 56),
        name=f"retention_s{seq}",
    )(q, k, v, gam, r0)


def _out_c_kernel(op_ref, os_ref, g_ref, w_ref, x_ref, mod_ref, lng_ref, lnb_ref, wrh_ref, wrl_ref,
                  x1_ref, h2_ref, lgt_ref):
    y = None
    o_all = _pick_rows(op_ref, os_ref, TM)
    for hd in range(RET_HEADS):
        o = o_all[:, hd * RET_DV:(hd + 1) * RET_DV]
        mu = jnp.mean(o, axis=-1, keepdims=True)
        oc = o - mu
        var = jnp.mean(oc * oc, axis=-1, keepdims=True)
        on = oc * lax.rsqrt(var + EPS)
        a = (on * _silu(g_ref[:, hd * RET_DV:(hd + 1) * RET_DV])).astype(BF16)
        part = _dot(a, w_ref[hd * RET_DV:(hd + 1) * RET_DV, :])
        y = part if y is None else y + part
    _post_mixer(y, x_ref[...], mod_ref, lng_ref, lnb_ref, wrh_ref, wrl_ref, x1_ref, h2_ref, lgt_ref)


def _out_c(o_p, o_s, g, w, x, mods, lng, lnb, wr_hi, wr_lo):
    full = lambda shape: pl.BlockSpec(shape, lambda i: (0,) * len(shape))
    post_in, out_shape, out_specs = _post_specs()
    return pl.pallas_call(
        _out_c_kernel,
        out_shape=out_shape,
        grid=(N_TOK // TM,),
        in_specs=_row_pair_specs(TM, MIX_C) + [
            pl.BlockSpec((TM, MIX_C), lambda i: (i, 0)),
            full((MIX_C, D_MODEL)),
            pl.BlockSpec((TM, D_MODEL), lambda i: (i, 0)),
        ] + post_in,
        out_specs=out_specs,
        compiler_params=_cparams(("arbitrary",), 56),
        name="out_c",
    )(o_p, o_s, g, w, x, mods, lng, lnb, wr_hi, wr_lo)


def _route_kernel(lgt_ref, bias_ref, tri_ref, w_ref, lpos_ref, p16_ref):
    tt = lgt_ref.shape[1]
    scores = jax.nn.sigmoid(lgt_ref[...])
    sel = scores + bias_ref[...]
    srow = lax.broadcasted_iota(I32, (GROUP_SIZE, tt), 0).astype(F32)
    ninf = -jnp.inf

    gs = []
    for g in range(N_GROUPS):
        sg = sel[g * GROUP_SIZE:(g + 1) * GROUP_SIZE, :]
        m1 = jnp.max(sg, axis=0, keepdims=True)
        i1 = jnp.min(jnp.where(sg == m1, srow, float(GROUP_SIZE)), axis=0, keepdims=True)
        m2 = jnp.max(jnp.where(srow == i1, ninf, sg), axis=0, keepdims=True)
        gs.append(m1 + m2)
    gs = jnp.concatenate(gs, axis=0)
    chosen = jnp.zeros((N_GROUPS, tt), F32)
    for _ in range(TOPK_GROUPS):
        mg = jnp.max(gs, axis=0, keepdims=True)
        gi = jnp.min(jnp.where(gs == mg, srow, float(N_GROUPS)), axis=0, keepdims=True)
        hit = srow == gi
        chosen = jnp.where(hit, 1.0, chosen)
        gs = jnp.where(hit, ninf, gs)
    sel = jnp.concatenate(
        [jnp.where(jnp.broadcast_to(chosen[g:g + 1, :], (GROUP_SIZE, tt)) > 0.5,
                   sel[g * GROUP_SIZE:(g + 1) * GROUP_SIZE, :], ninf) for g in range(N_GROUPS)], axis=0)

    erow = lax.broadcasted_iota(I32, (N_EXPERTS, tt), 0).astype(F32)
    ids, ws = [], []
    for _ in range(TOP_K):
        m = jnp.max(sel, axis=0, keepdims=True)
        ei = jnp.min(jnp.where(sel == m, erow, float(N_EXPERTS)), axis=0, keepdims=True)
        hit = erow == ei
        ids.append(ei)
        ws.append(jnp.sum(jnp.where(hit, scores, 0.0), axis=0, keepdims=True))
        sel = jnp.where(hit, ninf, sel)
    wsum = ws[0]
    for k in range(1, TOP_K):
        wsum = wsum + ws[k]
    w_ref[...] = jnp.concatenate([w / wsum * ROUTED_SCALE for w in ws], axis=0)

    member_f = jnp.zeros((N_EXPERTS, tt), F32)
    for k in range(TOP_K):
        member_f = jnp.where(erow == ids[k], 1.0, member_f)
    member_b = member_f.astype(BF16)
    cnt_row = _dot_nt(jnp.ones((SUBLANE, tt), BF16), member_b)[0:1, :]
    p16_row = jnp.ceil(cnt_row * (1.0 / PIECE)) * PIECE
    lane_e = lax.broadcasted_iota(I32, (N_EXPERTS, N_EXPERTS), 1)
    sub_e = lax.broadcasted_iota(I32, (N_EXPERTS, N_EXPERTS), 0)
    run_start = jnp.sum(jnp.where(lane_e < sub_e, jnp.broadcast_to(p16_row, (N_EXPERTS, N_EXPERTS)), 0.0),
                        axis=1, keepdims=True)
    rank = _dot(member_b, tri_ref[...]) + run_start
    lpos_ref[...] = jnp.concatenate(
        [jnp.sum(jnp.where(erow == ids[k], rank, 0.0), axis=0, keepdims=True) for k in range(TOP_K)],
        axis=0).astype(I32)
    p16_ref[pl.ds(pl.program_id(0), 1), :] = jnp.concatenate(
        [p16_row, jnp.zeros((1, LANE - N_EXPERTS), F32)], axis=1)


def _route(lgt, bias, tri):
    return pl.pallas_call(
        _route_kernel,
        out_shape=(
            jax.ShapeDtypeStruct((TOP_K, N_TOK), F32), jax.ShapeDtypeStruct((TOP_K, N_TOK), I32),
            jax.ShapeDtypeStruct((N_WIN, LANE), F32),
        ),
        grid=(N_WIN,),
        in_specs=[
            pl.BlockSpec((N_EXPERTS, WIN), lambda i: (0, i)),
            pl.BlockSpec((N_EXPERTS, 1), lambda i: (0, 0)),
            pl.BlockSpec((WIN, WIN), lambda i: (0, 0)),
        ],
        out_specs=(
            pl.BlockSpec((TOP_K, WIN), lambda i: (0, i)), pl.BlockSpec((TOP_K, WIN), lambda i: (0, i)),
            pl.BlockSpec((N_WIN, LANE), lambda i: (0, 0)),
        ),
        compiler_params=_cparams(("arbitrary",)),
        name="moe_route",
    )(lgt, bias, tri)


def _sort_kernel(lpos_ref, h_ref, xl_ref, oh_all):
    lp = jnp.where(pl.program_id(0) < N_WIN, lpos_ref[...], -1)
    riota = lax.broadcasted_iota(I32, (SORT_ROWS, WIN), 0).astype(jnp.int16)
    one = jnp.ones((SORT_ROWS, WIN), BF16)
    for j in range(RL // SORT_ROWS):
        rel = (lp - j * SORT_ROWS).astype(jnp.int16)
        oh = jnp.zeros((SORT_ROWS, WIN), BF16)
        for k in range(TOP_K):
            oh = jnp.where(rel[k:k + 1, :] == riota, one, oh)
        oh_all[j * SORT_ROWS:(j + 1) * SORT_ROWS, :] = oh
    x = h_ref[...]
    for nt in range(D_MODEL // SORT_ROWS):
        cols = slice(nt * SORT_ROWS, (nt + 1) * SORT_ROWS)
        xl_ref[:, cols] = _dot(oh_all[...], x[:, cols]).astype(BF16)


def _sort_rows(lpos, h2):
    last = N_WIN - 1
    return pl.pallas_call(
        _sort_kernel,
        out_shape=jax.ShapeDtypeStruct(((N_WIN + 1) * RL, D_MODEL), BF16),
        grid=(N_WIN + 1,),
        in_specs=[
            pl.BlockSpec((TOP_K, WIN), lambda i: (0, jnp.minimum(i, last))),
            pl.BlockSpec((WIN, D_MODEL), lambda i: (jnp.minimum(i, last), 0)),
        ],
        out_specs=pl.BlockSpec((RL, D_MODEL), lambda i: (i, 0)),
        scratch_shapes=[pltpu.VMEM((RL, WIN), BF16)],
        compiler_params=_cparams(("arbitrary",)),
        name="moe_sort",
    )(lpos, h2)


def _moe_tables(p16):
    n_w = jnp.arange(N_WIN, dtype=I32)
    run_start = jnp.cumsum(p16, axis=1) - p16
    rw = jnp.sum(p16, axis=1)
    cum_w = jnp.cumsum(p16, axis=0) - p16
    tot = jnp.sum(p16, axis=0)
    nblk = (tot + TME - 1) // TME
    blk_end = jnp.cumsum(nblk)
    blk0 = blk_end - nblk
    n_used = blk_end[-1]
    b = jnp.arange(N_EBLOCKS + 1, dtype=I32)
    block_e = jnp.minimum(jnp.sum((blk_end[None, :] <= b[:, None]).astype(I32), axis=1), N_EXPERTS - 1)
    piece = jnp.arange(PIECES, dtype=I32) * PIECE
    rp = (b - blk0[block_e])[:, None] * TME + piece[None, :]
    valid = (b[:, None] < n_used) & (rp < tot[block_e][:, None])
    cum_e = cum_w.T[block_e]
    len_e = p16.T[block_e]
    start_e = run_start.T[block_e]
    in_win = (cum_e[:, None, :] <= rp[:, :, None]) & (rp[:, :, None] < (cum_e + len_e)[:, None, :])
    row = n_w[None, None, :] * RL + start_e[:, None, :] + rp[:, :, None] - cum_e[:, None, :]
    row = jnp.sum(jnp.where(in_win, row, 0), axis=2)
    zero_src = N_WIN * RL
    assert 3 * TME <= RL
    trash = N_WIN * RL + (1 + b % 2)[:, None] * TME + piece[None, :]
    gather_row = (jnp.where(valid, row, zero_src) // PIECE).reshape(-1).astype(I32)
    scatter_row = (jnp.where(valid, row, trash) // PIECE).reshape(-1).astype(I32)
    rw = jnp.concatenate([rw, jnp.zeros((1,), rw.dtype)])
    return rw.astype(I32), block_e.astype(I32), n_used.astype(I32).reshape(1), gather_row, scatter_row


def _expert_kernel(be_ref, nb_ref, grow_ref, srow_ref, xl_hbm, wg_ref, wu_ref, wd_ref, yl_hbm,
                   xbuf, ybuf, wg_b, wu_b, wd_b, gsem, ssem):
    b = pl.program_id(0)
    nb = nb_ref[0]

    @pl.when(jnp.logical_and(b < nb, jnp.logical_or(b == 0, be_ref[b] != be_ref[jnp.maximum(b - 1, 0)])))
    def _():
        wg_b[...] = wg_ref[...].astype(BF16)
        wu_b[...] = wu_ref[...].astype(BF16)
        wd_b[...] = wd_ref[...].astype(BF16)

    def gather_start(blk, slot):
        for p in range(PIECES):
            pltpu.make_async_copy(xl_hbm.at[grow_ref[blk * PIECES + p]], xbuf.at[slot, p], gsem.at[slot]).start()

    def gather_wait(slot):
        pltpu.make_async_copy(xl_hbm.at[pl.ds(0, PIECES)], xbuf.at[slot], gsem.at[slot]).wait()

    def scatter_wait(slot):
        pltpu.make_async_copy(ybuf.at[slot], yl_hbm.at[pl.ds(0, PIECES)], ssem.at[slot]).wait()

    @pl.when(b < nb)
    def _():
        slot = b % 2

        @pl.when(b == 0)
        def _():
            gather_start(0, 0)

        @pl.when(b + 1 < nb)
        def _():
            gather_start(b + 1, 1 - slot)

        gather_wait(slot)

        @pl.when(b >= 2)
        def _():
            scatter_wait(slot)

        x = xbuf[slot].reshape(TME, D_MODEL)
        hb = _silu(_dot(x, wg_b[...])) * _dot(x, wu_b[...])
        ybuf[slot] = _dot(hb.astype(BF16), wd_b[...]).astype(BF16).reshape(PIECES, PIECE, D_MODEL)
        for p in range(PIECES):
            pltpu.make_async_copy(ybuf.at[slot, p], yl_hbm.at[srow_ref[b * PIECES + p]], ssem.at[slot]).start()

        @pl.when(b == nb - 1)
        def _():
            scatter_wait(slot)

            @pl.when(b >= 1)
            def _():
                scatter_wait(1 - slot)


def _experts(block_e, n_used, gather_row, scatter_row, xl, wg, wu, wd, layer):
    def w_map(i, be, nb, gr, sr):
        return (layer, be[jnp.minimum(i, nb[0] - 1)], 0, 0)

    n_pieces = (N_WIN + 1) * RL // PIECE
    yl = pl.pallas_call(
        _expert_kernel,
        out_shape=jax.ShapeDtypeStruct((n_pieces, PIECE, D_MODEL), BF16),
        input_output_aliases={4: 0},
        grid_spec=pltpu.PrefetchScalarGridSpec(
            num_scalar_prefetch=4,
            grid=(N_EBLOCKS,),
            in_specs=[
                pl.BlockSpec(memory_space=pl.ANY),
                pl.BlockSpec((None, None, D_MODEL, D_EXPERT), w_map),
                pl.BlockSpec((None, None, D_MODEL, D_EXPERT), w_map),
                pl.BlockSpec((None, None, D_EXPERT, D_MODEL), w_map),
            ],
            out_specs=pl.BlockSpec(memory_space=pl.ANY),
            scratch_shapes=[
                pltpu.VMEM((2, PIECES, PIECE, D_MODEL), BF16), pltpu.VMEM((2, PIECES, PIECE, D_MODEL), BF16),
                pltpu.VMEM((D_MODEL, D_EXPERT), BF16), pltpu.VMEM((D_MODEL, D_EXPERT), BF16),
                pltpu.VMEM((D_EXPERT, D_MODEL), BF16),
                pltpu.SemaphoreType.DMA((2,)), pltpu.SemaphoreType.DMA((2,)),
            ],
        ),
        compiler_params=_cparams(("arbitrary",)),
        name="moe_experts",
    )(block_e, n_used, gather_row, scatter_row, xl.reshape(n_pieces, PIECE, D_MODEL), wg, wu, wd)
    return yl.reshape(n_pieces * PIECE, D_MODEL)


def _combine_kernel(rw_ref, yl_ref, lpt_ref, wt_ref, h_ref, x1_ref, mod_ref, lng_ref, lnb_ref,
                    wsg_ref, wsu_ref, wsd_ref, *rest, split):
    del rw_ref
    if split:
        outp_ref, outs_ref, p_w, lp_b, wt_b = rest
    else:
        out_ref, p_w, lp_b, wt_b = rest
    hb = h_ref[...]
    shared = _dot((_silu(_dot(hb, wsg_ref[...])) * _dot(hb, wsu_ref[...])).astype(BF16), wsd_ref[...])
    lp = lpt_ref[...]
    wt = wt_ref[...]
    for k in range(TOP_K):
        lp_b[k] = jnp.broadcast_to(lp[:, k:k + 1], (WIN, SORT_ROWS)).astype(jnp.int16)
        wt_b[k] = jnp.broadcast_to(wt[:, k:k + 1], (WIN, SORT_ROWS)).astype(BF16)
    ciota = lax.broadcasted_iota(I32, (WIN, SORT_ROWS), 1)
    for j in range(RL // SORT_ROWS):
        col = (ciota + j * SORT_ROWS).astype(jnp.int16)
        pm = jnp.zeros((WIN, SORT_ROWS), BF16)
        for k in range(TOP_K):
            pm = jnp.where(lp_b[k] == col, wt_b[k], pm)
        p_w[:, j * SORT_ROWS:(j + 1) * SORT_ROWS] = pm
    routed = _dot(p_w[...], yl_ref[...])
    z = ALPHA * x1_ref[...] + mod_ref[5:6, :] * (routed + shared)
    out = _layernorm_rows(z, lng_ref[...], lnb_ref[...])
    if split:
        @pl.when(pl.program_id(0) < N_PROMPT // WIN)
        def _():
            outp_ref[...] = out

        @pl.when(pl.program_id(0) >= N_PROMPT // WIN)
        def _():
            outs_ref[...] = out
    else:
        out_ref[...] = out


def _combine(rw, yl, lpos_t, wt, h2, x1, mods, lng, lnb, wsg, wsu, wsd, *, split):
    full = lambda shape: pl.BlockSpec(shape, lambda i, rw: (0,) * len(shape))
    if split:
        out_shape = (jax.ShapeDtypeStruct((N_PROMPT, D_MODEL), F32), jax.ShapeDtypeStruct((N_SAMPLE, D_MODEL), F32))
        out_specs = tuple(_row_pair_specs(WIN, D_MODEL))
    else:
        out_shape = jax.ShapeDtypeStruct((N_TOK, D_MODEL), F32)
        out_specs = pl.BlockSpec((WIN, D_MODEL), lambda i, rw: (i, 0))
    return pl.pallas_call(
        functools.partial(_combine_kernel, split=split),
        out_shape=out_shape,
        grid_spec=pltpu.PrefetchScalarGridSpec(
            num_scalar_prefetch=1,
            grid=(N_WIN,),
            in_specs=[
                pl.BlockSpec((RL, D_MODEL), lambda i, rw: (i, 0)),
                pl.BlockSpec((WIN, TOP_K), lambda i, rw: (i, 0)),
                pl.BlockSpec((WIN, TOP_K), lambda i, rw: (i, 0)),
                pl.BlockSpec((WIN, D_MODEL), lambda i, rw: (i, 0)),
                pl.BlockSpec((WIN, D_MODEL), lambda i, rw: (i, 0)),
                pl.BlockSpec((None, MOD_ROWS, D_MODEL), lambda i, rw: (_cond_block(i, WIN), 0, 0)),
                full((1, D_MODEL)), full((1, D_MODEL)),
                full((D_MODEL, D_EXPERT)), full((D_MODEL, D_EXPERT)), full((D_EXPERT, D_MODEL)),
            ],
            out_specs=out_specs,
            scratch_shapes=[pltpu.VMEM((WIN, RL), BF16),
                            pltpu.VMEM((TOP_K, WIN, SORT_ROWS), jnp.int16),
                            pltpu.VMEM((TOP_K, WIN, SORT_ROWS), BF16)],
        ),
        compiler_params=_cparams(("arbitrary",)),
        name="moe_combine",
    )(rw, yl, lpos_t, wt, h2, x1, mods, lng, lnb, wsg, wsu, wsd)


def _moe_and_norm(x1, h2, lgt, mods, lng, lnb, router_bias, tri, wg, wu, wd, wsg, wsu, wsd, *, layer, split):
    wts, lpos, p16 = _route(lgt, router_bias.reshape(N_EXPERTS, 1), tri)
    rw, block_e, n_used, gather_row, scatter_row = _moe_tables(p16[:, :N_EXPERTS].astype(I32))
    xl = _sort_rows(lpos, h2)
    yl = _experts(block_e, n_used, gather_row, scatter_row, xl, wg, wu, wd, layer)
    return _combine(rw, yl, lpos.T, wts.T, h2, x1, mods, lng, lnb, wsg, wsu, wsd, split=split)


def _rope_tables_mla():
    t = jnp.arange(DEC_SEQ)
    row = (t // GRID_W).astype(F32)
    col = (t % GRID_W).astype(F32)
    n = QK_ROPE // 4
    inv = ROPE_BASE ** (-jnp.arange(n, dtype=F32) / n)
    ang_r = row[:, None] * inv
    ang_c = col[:, None] * inv
    cos = jnp.ones((DEC_SEQ, LANE), F32)
    sin_m = jnp.zeros((DEC_SEQ, LANE), F32)
    sin_p = jnp.zeros((DEC_SEQ, LANE), F32)
    l0 = ROPE_LANE0
    for base, ang in ((l0, ang_r), (l0 + 2 * n, ang_c)):
        c, s = jnp.cos(ang), jnp.sin(ang)
        cos = cos.at[:, base:base + n].set(c).at[:, base + n:base + 2 * n].set(c)
        sin_m = sin_m.at[:, base:base + n].set(-s)
        sin_p = sin_p.at[:, base + n:base + 2 * n].set(s)
    ident = (jnp.ones((TM, LANE), F32), jnp.zeros((TM, LANE), F32), jnp.zeros((TM, LANE), F32))
    return tuple(jnp.concatenate([i, tbl], axis=0) for i, tbl in zip(ident, (cos, sin_m, sin_p)))


def _rope_tables_ret():
    half = RET_DK // 2
    theta = ROPE_BASE ** (-jnp.linspace(0.0, 1.0, half, dtype=F32))
    ang = jnp.arange(DEC_SEQ, dtype=F32)[:, None] * theta
    cos = jnp.concatenate([jnp.ones((TM, half), F32), jnp.cos(ang)], axis=0)
    sin = jnp.concatenate([jnp.zeros((TM, half), F32), jnp.sin(ang)], axis=0)
    return cos, sin


def _pad_heads(w, width, lane0=0):
    k = w.shape[0]
    w = w.reshape(k, MLA_HEADS, width)
    out = jnp.zeros((k, MLA_HEADS, HEAD_PAD), w.dtype).at[:, :, lane0:lane0 + width].set(w)
    return out.reshape(k, MLA_HEADS * HEAD_PAD)


def _rg_gate_weights(wa, ba, wx, bx):
    n_ct = RG_W // LANE
    per = LANE // RG_BW
    tiles_w, tiles_b = [], []
    for c in range(n_ct):
        cols_w, cols_b = [], []
        for d in range(2):
            for w, b in ((wa, ba), (wx, bx)):
                m = jnp.zeros((LANE, LANE), F32)
                for p in range(per):
                    m = m.at[p * RG_BW:(p + 1) * RG_BW, p * RG_BW:(p + 1) * RG_BW].set(w[d, c * per + p])
                cols_w.append(m)
                cols_b.append(b[d, c * LANE:(c + 1) * LANE])
        tiles_w.append(jnp.concatenate(cols_w, axis=1))
        tiles_b.append(jnp.concatenate(cols_b, axis=0)[None, :])
    return jnp.stack(tiles_w).astype(BF16), jnp.stack(tiles_b)


def kernel(x_prompt, x_sample, cache_mla_ckv, cache_mla_krope, state_rglru, state_ret, c, c_ctx, w_ada, b_ada,
           ln_g, ln_b, w_in_ab, rg_conv_w, rg_conv_b, rg_wa, rg_ba, rg_wx, rg_bx, rg_lambda, mla_q_norm, mla_w_uq,
           mla_kv_norm, mla_w_ukv, w_out_ab, w_in_c, ret_gamma_logit, w_out_c, w_router, router_bias,
           w_exp_gate, w_exp_up, w_exp_down, w_sh_gate, w_sh_up, w_sh_down):
    xp = x_prompt.reshape(N_PROMPT, D_MODEL)
    xs = x_sample.reshape(N_SAMPLE, D_MODEL)
    cond = jnp.zeros((16, D_MODEL), F32).at[0].set(c_ctx).at[1:1 + DEC_BATCH].set(c)
    mods_all = _ada_modulation(cond, w_ada, b_ada).reshape(DEPTH, 16, 6, D_MODEL)[:, :N_COND]
    mods_all = jnp.pad(mods_all, ((0, 0), (0, 0), (0, MOD_ROWS - 6), (0, 0)))

    tri = (jnp.arange(WIN)[:, None] < jnp.arange(WIN)[None, :]).astype(BF16)
    wr_t = jnp.swapaxes(w_router, 1, 2)
    wr_hi = wr_t.astype(BF16)
    wr_lo = (wr_t - wr_hi.astype(F32)).astype(BF16)
    wg_e, wu_e, wd_e = w_exp_gate, w_exp_up, w_exp_down
    wsg, wsu, wsd = w_sh_gate.astype(BF16), w_sh_up.astype(BF16), w_sh_down.astype(BF16)

    l, e = 0, 0
    mods = mods_all[l]
    n_main = 2 * RG_W + Q_LORA + KV_LORA
    w_main = w_in_ab[e][:, :n_main].astype(BF16)
    w_kr = jnp.zeros((D_MODEL, LANE), F32).at[:, ROPE_LANE0:ROPE_LANE0 + QK_ROPE].set(w_in_ab[e][:, n_main:]).astype(BF16)
    main, krp = _proj_ab(xp, xs, mods, w_main, w_kr)

    wg_rg, bg_rg = _rg_gate_weights(rg_wa[e], rg_ba[e], rg_wx[e], rg_bx[e])
    h0_p = jnp.zeros((BATCH, 2, RG_W), F32)
    rg_args = (rg_conv_w[e], rg_conv_b[e].reshape(1, RG_W), wg_rg, bg_rg, rg_lambda[e])
    yrg_p, rg_fin = _rglru(main, *rg_args, h0_p, n_seq=BATCH, seq=SEQ, row_block0=0)
    yrg_s, _ = _rglru(main, *rg_args, state_rglru[:, e],
                      n_seq=DEC_BATCH, seq=DEC_SEQ, row_block0=N_PROMPT // DEC_SEQ)

    cos_t, sm_t, sp_t = _rope_tables_mla()
    w_uq = mla_w_uq[e].reshape(Q_LORA, MLA_HEADS, QK_NOPE + QK_ROPE)
    wq_p = _pad_heads(w_uq.reshape(Q_LORA, -1), QK_NOPE + QK_ROPE).astype(BF16)
    w_ukv = mla_w_ukv[e].reshape(KV_LORA, MLA_HEADS, QK_NOPE + V_HEAD)
    wuk_p = _pad_heads(w_ukv[:, :, :QK_NOPE].reshape(KV_LORA, -1), QK_NOPE).astype(BF16)
    wuvt = w_ukv[:, :, QK_NOPE:].reshape(KV_LORA, MLA_HEADS * V_HEAD).T.astype(BF16)
    q_att, k_att, v_att, ckv_n = _mla_prep(main, krp, cos_t, sm_t, sp_t, mla_q_norm[e].reshape(1, Q_LORA), wq_p,
                                           mla_kv_norm[e].reshape(1, KV_LORA), wuk_p, wuvt)
    ctx_ckv = cache_mla_ckv[:, e].reshape(DEC_BATCH * PAST_LEN, KV_LORA)
    ctx_krp = jnp.zeros((DEC_BATCH * PAST_LEN, LANE), F32).at[:, ROPE_LANE0:ROPE_LANE0 + QK_ROPE].set(
        cache_mla_krope[:, e].reshape(DEC_BATCH * PAST_LEN, QK_ROPE))
    kc_att, vc_att = _mla_ctx(ctx_ckv, ctx_krp, wuk_p, wuvt)

    o_att_p = _attention(q_att, k_att, v_att, None, None, n_seq=BATCH, seq=SEQ, row_block0=0, tq=SEQ)
    o_att_s = _attention(q_att, k_att, v_att, kc_att, vc_att,
                         n_seq=DEC_BATCH, seq=DEC_SEQ, row_block0=N_PROMPT // DEC_SEQ, tq=TQ)

    w_out = w_out_ab[e].astype(BF16)
    x1, h2, lgt = _out_ab(yrg_p, yrg_s, o_att_p, o_att_s, w_out[:RG_W], w_out[RG_W:], xp, xs, mods,
                          ln_g[l, 0].reshape(1, D_MODEL), ln_b[l, 0].reshape(1, D_MODEL), wr_hi[l], wr_lo[l])
    x = _moe_and_norm(x1, h2, lgt, mods, ln_g[l, 1].reshape(1, D_MODEL), ln_b[l, 1].reshape(1, D_MODEL),
                      router_bias[l], tri, wg_e, wu_e, wd_e, wsg[l], wsu[l], wsd[l], layer=l, split=False)

    new_ckv = ckv_n.reshape(BATCH, 1, SEQ, KV_LORA)
    new_krope = krp[:N_PROMPT, ROPE_LANE0:ROPE_LANE0 + QK_ROPE].reshape(BATCH, 1, SEQ, QK_ROPE)
    new_rg = rg_fin.reshape(BATCH, 1, 2, RG_W)

    l, o = 1, 0
    mods = mods_all[l]
    qk = RET_HEADS * RET_DK
    w_c = w_in_c[o].astype(BF16)
    cos_r, sin_r = _rope_tables_ret()
    q_r, k_r, v_r, g_r = _proj_c(x, mods, cos_r, sin_r, w_c[:, :qk], w_c[:, qk:2 * qk],
                                 w_c[:, 2 * qk:2 * qk + MIX_C], w_c[:, 2 * qk + MIX_C:])
    gam = jnp.broadcast_to(ret_gamma_logit[o].astype(F32)[:, :, None, None], (2, RET_HEADS, SUBLANE, LANE))
    r0_p = jnp.zeros((BATCH, 2, RET_HEADS, RET_DK, RET_DV), F32)
    o_ret_p, r_fin = _retention(q_r, k_r, v_r, gam, r0_p, n_seq=BATCH, seq=SEQ, row_block0=0, with_state=True)
    (o_ret_s,) = _retention(q_r, k_r, v_r, gam, state_ret[:, o],
                            n_seq=DEC_BATCH, seq=DEC_SEQ, row_block0=N_PROMPT // DEC_SEQ, with_state=False)
    x1, h2, lgt = _out_c(o_ret_p, o_ret_s, g_r, w_out_c[o].astype(BF16), x, mods,
                         ln_g[l, 0].reshape(1, D_MODEL), ln_b[l, 0].reshape(1, D_MODEL), wr_hi[l], wr_lo[l])
    y_p, y_s = _moe_and_norm(x1, h2, lgt, mods, ln_g[l, 1].reshape(1, D_MODEL), ln_b[l, 1].reshape(1, D_MODEL),
                             router_bias[l], tri, wg_e, wu_e, wd_e, wsg[l], wsu[l], wsd[l], layer=l, split=True)

    y_prompt = y_p.reshape(BATCH, SEQ, D_MODEL)
    y_sample = y_s.reshape(DEC_BATCH, DEC_SEQ, D_MODEL)
    new_ret = r_fin.reshape(BATCH, 1, 2, RET_HEADS, RET_DK, RET_DV)
    return (y_prompt, y_sample, new_ckv, new_krope, new_rg, new_ret)
```

```python
import functools
import math

import jax
import jax.numpy as jnp
from jax import lax
from jax.experimental import pallas as pl
from jax.experimental.pallas import tpu as pltpu

F32 = jnp.float32
BF16 = jnp.bfloat16
I32 = jnp.int32

D_MODEL = 1024
BATCH, SEQ = 16, 256
DEC_BATCH, DEC_SEQ = 8, 4096
PAST_LEN = 256
DEPTH = 2
GRID_W = 64
RG_W, RG_BLOCKS = 512, 8
RG_BW = RG_W // RG_BLOCKS
RG_C = 8.0
CONV_W, CONV_LEFT = 4, 2
MLA_HEADS, QK_NOPE, QK_ROPE, V_HEAD = 8, 64, 32, 64
Q_LORA, KV_LORA = 768, 256
ROPE_BASE = 10000.0
ATTN_SCALE = (QK_NOPE + QK_ROPE) ** -0.5
RET_HEADS, RET_DK, RET_DV, RET_CHUNK = 4, 256, 512, 128
MIX_C = RET_HEADS * RET_DV
N_EXPERTS, TOP_K, N_GROUPS, TOPK_GROUPS = 64, 8, 8, 4
GROUP_SIZE = N_EXPERTS // N_GROUPS
D_EXPERT = 256
ROUTED_SCALE = 2.5
ALPHA = (2 * DEPTH) ** 0.25
EPS = 1e-6

N_PROMPT = BATCH * SEQ
N_SAMPLE = DEC_BATCH * DEC_SEQ
N_TOK = N_PROMPT + N_SAMPLE
N_COND = 1 + DEC_BATCH
MOD_ROWS = 8

LANE = 128
SUBLANE = 8
TM = 512
HEAD_PAD = 128
ROPE_LANE0 = QK_NOPE
TQ = 256
KC = 256
AH = 2
ATTN_UNROLL = True
RET_UNROLL = 4
V_ONES = 16
Q_PRESCALE = ATTN_SCALE * math.log2(math.e)
SCAN_ROWS = 64
GATE_ROWS = 256
WIN = 256
N_WIN = N_TOK // WIN
PIECE = 16
SORT_ROWS = 256
RL = 3072
TME = 1024
PIECES = TME // PIECE
N_PAIRS = N_TOK * TOP_K
N_EBLOCKS = (N_PAIRS + N_WIN * N_EXPERTS * (PIECE - 1)) // TME + N_EXPERTS
NEG = -1e30


def _cparams(sem, vmem_mb=48):
    return pltpu.CompilerParams(dimension_semantics=sem, vmem_limit_bytes=vmem_mb * 1024 * 1024)


def _cond_block(i, tm):
    npb = N_PROMPT // tm
    return jnp.where(i < npb, 0, 1 + (i - npb) // (DEC_SEQ // tm))


def _pos_block(i, tm):
    npb = N_PROMPT // tm
    return jnp.where(i < npb, 0, 1 + (i - npb) % (DEC_SEQ // tm))


def _split_hi_lo(a):
    hi = a.astype(BF16)
    lo = (a - hi.astype(F32)).astype(BF16)
    return hi, lo


def _dot(a, b):
    return jnp.dot(a, b, preferred_element_type=F32)


def _dot_nt(a, b):
    return lax.dot_general(a, b, (((1,), (1,)), ((), ())), preferred_element_type=F32)


def _silu(x):
    return x * jax.nn.sigmoid(x)


def _gelu_tanh(x):
    return 0.5 * x * (1.0 + jnp.tanh(math.sqrt(2.0 / math.pi) * (x + 0.044715 * (x * x * x))))


def _softplus(x):
    return jnp.maximum(x, 0.0) + jnp.log1p(jnp.exp(-jnp.abs(x)))


def _layernorm_rows(z, g, b):
    mu = jnp.mean(z, axis=-1, keepdims=True)
    zc = z - mu
    var = jnp.mean(zc * zc, axis=-1, keepdims=True)
    return (zc * lax.rsqrt(var + EPS)) * g + b


def _ada_kernel(c_ref, w_ref, b_ref, o_ref):
    s_hi, s_lo = _split_hi_lo(_silu(c_ref[...]))
    w_hi, w_lo = _split_hi_lo(w_ref[...])
    o_ref[...] = _dot(s_hi, w_hi) + _dot(s_hi, w_lo) + _dot(s_lo, w_hi) + b_ref[...]


def _ada_modulation(cond, w_ada, b_ada):
    n6 = 6 * D_MODEL
    tn = D_MODEL
    return pl.pallas_call(
        _ada_kernel,
        out_shape=jax.ShapeDtypeStruct((DEPTH, 16, n6), F32),
        grid=(DEPTH, n6 // tn),
        in_specs=[
            pl.BlockSpec((16, D_MODEL), lambda l, j: (0, 0)),
            pl.BlockSpec((None, D_MODEL, tn), lambda l, j: (l, 0, j)),
            pl.BlockSpec((None, 1, tn), lambda l, j: (l, 0, j)),
        ],
        out_specs=pl.BlockSpec((None, 16, tn), lambda l, j: (l, 0, j)),
        compiler_params=_cparams(("arbitrary", "arbitrary")),
        name="ada_modulation",
    )(cond, w_ada, b_ada.reshape(DEPTH, 1, n6))


def _modulated(x, mod_ref, shift_row, scale_row):
    return x * (1.0 + mod_ref[scale_row:scale_row + 1, :]) + mod_ref[shift_row:shift_row + 1, :]


def _row_pair_specs(tm, width, col=0):
    npb = N_PROMPT // tm
    return [pl.BlockSpec((tm, width), lambda i, *_: (jnp.minimum(i, npb - 1), col)),
            pl.BlockSpec((tm, width), lambda i, *_: (jnp.maximum(i - npb, 0), col))]


def _pick_rows(p_ref, s_ref, tm):
    return jnp.where(pl.program_id(0) < N_PROMPT // tm, p_ref[...], s_ref[...])


def _proj_ab_kernel(xp_ref, xs_ref, mod_ref, w_ref, wkr_ref, main_ref, kr_ref):
    h = _modulated(_pick_rows(xp_ref, xs_ref, TM), mod_ref, 0, 1).astype(BF16)
    n = w_ref.shape[1]
    step = 512
    for j in range(n // step):
        main_ref[:, j * step:(j + 1) * step] = _dot(h, w_ref[:, j * step:(j + 1) * step])
    kr_ref[...] = _dot(h, wkr_ref[...])


def _proj_ab(xp, xs, mods, w_main, w_kr):
    n = w_main.shape[1]
    return pl.pallas_call(
        _proj_ab_kernel,
        out_shape=(jax.ShapeDtypeStruct((N_TOK, n), F32), jax.ShapeDtypeStruct((N_TOK, LANE), F32)),
        grid=(N_TOK // TM,),
        in_specs=_row_pair_specs(TM, D_MODEL) + [
            pl.BlockSpec((None, MOD_ROWS, D_MODEL), lambda i: (_cond_block(i, TM), 0, 0)),
            pl.BlockSpec((D_MODEL, n), lambda i: (0, 0)),
            pl.BlockSpec((D_MODEL, LANE), lambda i: (0, 0)),
        ],
        out_specs=(pl.BlockSpec((TM, n), lambda i: (i, 0)), pl.BlockSpec((TM, LANE), lambda i: (i, 0))),
        compiler_params=_cparams(("arbitrary",)),
        name="proj_ab",
    )(xp, xs, mods, w_main, w_kr)


def _rglru_kernel(xr_ref, gr_ref, cw_ref, cb_ref, wg_ref, bg_ref, lam_ref, h0_ref,
                  y_ref, hfin_ref, xpad, a_s, b_s, *, seq):
    pad = SUBLANE
    xpad[0:pad, :] = jnp.zeros((pad, LANE), F32)
    xpad[seq + pad:seq + 2 * pad, :] = jnp.zeros((pad, LANE), F32)
    xpad[pad:seq + pad, :] = xr_ref[...]

    sp = _softplus(-lam_ref[...])
    cw = cw_ref[...]
    cb = cb_ref[...]
    wg = wg_ref[...]
    bg = bg_ref[...]

    def gate_step(c, carry):
        t0 = pl.multiple_of(c * GATE_ROWS, GATE_ROWS)
        win = xpad[pl.ds(t0, GATE_ROWS + 2 * pad), :]
        xc = cb
        for j in range(CONV_W):
            off = pad - CONV_LEFT + j
            xc = xc + win[off:off + GATE_ROWS, :] * cw[j:j + 1, :]
        g = _dot(xc.astype(BF16), wg) + bg
        for d in range(2):
            r = jax.nn.sigmoid(g[:, (2 * d) * LANE:(2 * d + 1) * LANE])
            i = jax.nn.sigmoid(g[:, (2 * d + 1) * LANE:(2 * d + 2) * LANE])
            log_a = (-RG_C * r) * sp[d:d + 1, :]
            a = jnp.exp(log_a)
            t = jnp.tanh(log_a)
            bt = jnp.sqrt(2.0 * t / (t - 1.0)) * (i * xc)
            a_s[d, pl.ds(t0, GATE_ROWS), :] = a
            b_s[d, pl.ds(t0, GATE_ROWS), :] = bt
        return carry

    lax.fori_loop(0, seq // GATE_ROWS, gate_step, 0, unroll=min(2, seq // GATE_ROWS))

    row = lax.broadcasted_iota(I32, (SCAN_ROWS, LANE), 0) % SUBLANE
    n_steps = seq // SCAN_ROWS
    tiles = SCAN_ROWS // SUBLANE

    def local_scan(a, b, reverse):
        for k in (1, 2, 4):
            if reverse:
                ok = row < SUBLANE - k
                shift = SCAN_ROWS - k
            else:
                ok = row >= k
                shift = k
            a_sh = jnp.where(ok, pltpu.roll(a, shift, 0), 1.0)
            b_sh = jnp.where(ok, pltpu.roll(b, shift, 0), 0.0)
            b = a * b_sh + b
            a = a * a_sh
        return a, b

    def fwd_step(c, h):
        t0 = pl.multiple_of(c * SCAN_ROWS, SCAN_ROWS)
        a, b = local_scan(a_s[0, pl.ds(t0, SCAN_ROWS), :], b_s[0, pl.ds(t0, SCAN_ROWS), :], False)
        outs = []
        for j in range(tiles):
            hj = a[j * SUBLANE:(j + 1) * SUBLANE, :] * h + b[j * SUBLANE:(j + 1) * SUBLANE, :]
            outs.append(hj)
            h = hj[SUBLANE - 1:SUBLANE, :]
        y_ref[pl.ds(t0, SCAN_ROWS), :] = jnp.concatenate(outs, axis=0)
        return h

    h_f = lax.fori_loop(0, n_steps, fwd_step, h0_ref[0:1, :], unroll=4)

    def bwd_step(c, h):
        t0 = pl.multiple_of((n_steps - 1 - c) * SCAN_ROWS, SCAN_ROWS)
        a, b = local_scan(a_s[1, pl.ds(t0, SCAN_ROWS), :], b_s[1, pl.ds(t0, SCAN_ROWS), :], True)
        outs = [None] * tiles
        for j in reversed(range(tiles)):
            hj = a[j * SUBLANE:(j + 1) * SUBLANE, :] * h + b[j * SUBLANE:(j + 1) * SUBLANE, :]
            outs[j] = hj
            h = hj[0:1, :]
        hb = jnp.concatenate(outs, axis=0)
        y_ref[pl.ds(t0, SCAN_ROWS), :] = (y_ref[pl.ds(t0, SCAN_ROWS), :] + hb) * _gelu_tanh(gr_ref[pl.ds(t0, SCAN_ROWS), :])
        return h

    h_b = lax.fori_loop(0, n_steps, bwd_step, h0_ref[1:2, :], unroll=4)
    hfin_ref[0:1, :] = h_f
    hfin_ref[1:2, :] = h_b


def _rglru(main, cw, cb, wg, bg, lam, h0, *, n_seq, seq, row_block0):
    n_ct = RG_W // LANE
    gr_col0 = RG_W // LANE
    kern = functools.partial(_rglru_kernel, seq=seq)
    return pl.pallas_call(
        kern,
        out_shape=(jax.ShapeDtypeStruct((n_seq * seq, RG_W), F32), jax.ShapeDtypeStruct((n_seq, 2, RG_W), F32)),
        grid=(n_seq, n_ct),
        in_specs=[
            pl.BlockSpec((seq, LANE), lambda b, c: (row_block0 + b, c)),
            pl.BlockSpec((seq, LANE), lambda b, c: (row_block0 + b, gr_col0 + c)),
            pl.BlockSpec((CONV_W, LANE), lambda b, c: (0, c)),
            pl.BlockSpec((1, LANE), lambda b, c: (0, c)),
            pl.BlockSpec((None, LANE, 4 * LANE), lambda b, c: (c, 0, 0)),
            pl.BlockSpec((None, 1, 4 * LANE), lambda b, c: (c, 0, 0)),
            pl.BlockSpec((2, LANE), lambda b, c: (0, c)),
            pl.BlockSpec((None, 2, LANE), lambda b, c: (b, 0, c)),
        ],
        out_specs=(
            pl.BlockSpec((seq, LANE), lambda b, c: (b, c)),
            pl.BlockSpec((None, 2, LANE), lambda b, c: (b, 0, c)),
        ),
        scratch_shapes=[
            pltpu.VMEM((seq + 2 * SUBLANE, LANE), F32),
            pltpu.VMEM((2, seq, LANE), F32),
            pltpu.VMEM((2, seq, LANE), F32),
        ],
        compiler_params=_cparams(("arbitrary", "arbitrary")),
        name=f"rglru_s{seq}",
    )(main, main, cw, cb, wg, bg, lam, h0)


def _rope_lanes(x, cos, sin_m, sin_p):
    n = x.shape[1] // LANE
    half = QK_ROPE // 4
    cos_t = jnp.concatenate([cos] * n, axis=1) if n > 1 else cos
    sm_t = jnp.concatenate([sin_m] * n, axis=1) if n > 1 else sin_m
    sp_t = jnp.concatenate([sin_p] * n, axis=1) if n > 1 else sin_p
    up = pltpu.roll(x, x.shape[1] - half, 1)
    dn = pltpu.roll(x, half, 1)
    return x * cos_t + up * sm_t + dn * sp_t


def _mla_prep_kernel(cq0_ref, cq1_ref, cq2_ref, ckv_ref, kr_ref, cos_ref, sm_ref, sp_ref,
                     qn_ref, wq_ref, kvn_ref, wuk_ref, wuvt_ref,
                     q_ref, k_ref, vt_ref, ckvn_ref):
    cq = [cq0_ref[...], cq1_ref[...], cq2_ref[...]]
    ms = (jnp.sum(cq[0] * cq[0], axis=-1, keepdims=True) + jnp.sum(cq[1] * cq[1], axis=-1, keepdims=True)
          + jnp.sum(cq[2] * cq[2], axis=-1, keepdims=True)) * (1.0 / Q_LORA)
    inv = lax.rsqrt(ms + EPS)
    blk = Q_LORA // 3
    q = None
    for j in range(3):
        cqn = ((cq[j] * inv) * qn_ref[:, j * blk:(j + 1) * blk]).astype(BF16)
        part = _dot(cqn, wq_ref[j * blk:(j + 1) * blk, :])
        q = part if q is None else q + part
    cos, sm, sp = cos_ref[...], sm_ref[...], sp_ref[...]
    q_ref[...] = (_rope_lanes(q, cos, sm, sp) * Q_PRESCALE).T.astype(BF16)

    ckv = ckv_ref[...]
    inv_kv = lax.rsqrt(jnp.mean(ckv * ckv, axis=-1, keepdims=True) + EPS)
    ckvn = (ckv * inv_kv) * kvn_ref[...]

    @pl.when(pl.program_id(0) < N_PROMPT // TM)
    def _():
        ckvn_ref[...] = ckvn

    ckvn_b = ckvn.astype(BF16)
    kr_rot = _rope_lanes(kr_ref[...], cos, sm, sp)
    k_ref[...] = (_dot(ckvn_b, wuk_ref[...]) + jnp.concatenate([kr_rot] * MLA_HEADS, axis=1)).astype(BF16)
    vt = _dot_nt(wuvt_ref[...], ckvn_b).astype(BF16)
    for c in range(vt_ref.shape[0]):
        vt_ref[c] = vt[:, c * KC:(c + 1) * KC]


def _mla_prep(main, krp, cos_t, sm_t, sp_t, q_norm, wq_p, kv_norm, wuk_p, wuvt):
    cq_col0 = 2 * RG_W // 256
    hp = MLA_HEADS * HEAD_PAD
    full = lambda shape: pl.BlockSpec(shape, lambda i: (0,) * len(shape))
    tab = pl.BlockSpec((TM, LANE), lambda i: (_pos_block(i, TM), 0))
    return pl.pallas_call(
        _mla_prep_kernel,
        out_shape=(
            jax.ShapeDtypeStruct((hp, N_TOK), BF16),
            jax.ShapeDtypeStruct((N_TOK, hp), BF16),
            jax.ShapeDtypeStruct((N_TOK // KC, MLA_HEADS * V_HEAD, KC), BF16),
            jax.ShapeDtypeStruct((N_PROMPT, KV_LORA), F32),
        ),
        grid=(N_TOK // TM,),
        in_specs=[
            pl.BlockSpec((TM, 256), lambda i: (i, cq_col0)),
            pl.BlockSpec((TM, 256), lambda i: (i, cq_col0 + 1)),
            pl.BlockSpec((TM, 256), lambda i: (i, cq_col0 + 2)),
            pl.BlockSpec((TM, 256), lambda i: (i, cq_col0 + 3)),
            pl.BlockSpec((TM, LANE), lambda i: (i, 0)),
            tab, tab, tab,
            full((1, Q_LORA)), full((Q_LORA, hp)), full((1, KV_LORA)), full((KV_LORA, hp)),
            full((MLA_HEADS * V_HEAD, KV_LORA)),
        ],
        out_specs=(
            pl.BlockSpec((hp, TM), lambda i: (0, i)),
            pl.BlockSpec((TM, hp), lambda i: (i, 0)),
            pl.BlockSpec((TM // KC, MLA_HEADS * V_HEAD, KC), lambda i: (i, 0, 0)),
            pl.BlockSpec((TM, KV_LORA), lambda i: (jnp.minimum(i, N_PROMPT // TM - 1), 0)),
        ),
        compiler_params=_cparams(("arbitrary",)),
        name="mla_prep",
    )(main, main, main, main, krp, cos_t, sm_t, sp_t, q_norm, wq_p, kv_norm, wuk_p, wuvt)


def _mla_ctx_kernel(ckv_ref, kr_ref, wuk_ref, wuvt_ref, k_ref, vt_ref):
    ckv_b = ckv_ref[...].astype(BF16)
    k_ref[...] = (_dot(ckv_b, wuk_ref[...]) + jnp.concatenate([kr_ref[...]] * MLA_HEADS, axis=1)).astype(BF16)
    vt_ref[...] = _dot_nt(wuvt_ref[...], ckv_b).astype(BF16)


def _mla_ctx(ctx_ckv, ctx_krp, wuk_p, wuvt):
    n = ctx_ckv.shape[0]
    hp = MLA_HEADS * HEAD_PAD
    tm = KC
    full = lambda shape: pl.BlockSpec(shape, lambda i: (0,) * len(shape))
    return pl.pallas_call(
        _mla_ctx_kernel,
        out_shape=(jax.ShapeDtypeStruct((n, hp), BF16),
                   jax.ShapeDtypeStruct((n // tm, MLA_HEADS * V_HEAD, tm), BF16)),
        grid=(n // tm,),
        in_specs=[
            pl.BlockSpec((tm, KV_LORA), lambda i: (i, 0)),
            pl.BlockSpec((tm, LANE), lambda i: (i, 0)),
            full((KV_LORA, hp)), full((MLA_HEADS * V_HEAD, KV_LORA)),
        ],
        out_specs=(pl.BlockSpec((tm, hp), lambda i: (i, 0)),
                   pl.BlockSpec((None, MLA_HEADS * V_HEAD, tm), lambda i: (i, 0, 0))),
        compiler_params=_cparams(("arbitrary",)),
        name="mla_ctx",
    )(ctx_ckv, ctx_krp, wuk_p, wuvt)


def _attn_kernel(*refs, seq, tq, n_ctx):
    if n_ctx:
        q_ref, k_ref, vt_ref, kc_ref, vtc_ref, o_ref, s_scr, p_scr, k_all, vt_all = refs
    else:
        q_ref, k_ref, vt_ref, o_ref, s_scr, p_scr, k_all, vt_all = refs
    has_ctx = 1 if n_ctx else 0
    n_own = seq // KC
    n = n_own + has_ctx
    qs = [q_ref[h * HEAD_PAD:(h + 1) * HEAD_PAD, :] for h in range(AH)]

    @pl.when(pl.program_id(2) == 0)
    def _():
        k_all[0:seq, :] = k_ref[...]
        if has_ctx:
            k_all[seq:seq + KC, :] = kc_ref[...]
        for h in range(AH):
            vt_all[0:n_own, h, 0:V_HEAD, :] = vt_ref[:, h * V_HEAD:(h + 1) * V_HEAD, :]
            if has_ctx:
                vt_all[n_own, h, 0:V_HEAD, :] = vtc_ref[h * V_HEAD:(h + 1) * V_HEAD, :]
            vt_all[:, h, V_HEAD:V_HEAD + V_ONES, :] = jnp.ones((n, V_ONES, KC), BF16)

    def k_chunk(c, h):
        t0 = c * KC if isinstance(c, int) else pl.multiple_of(c * KC, KC)
        return k_all[pl.ds(t0, KC), h * HEAD_PAD:(h + 1) * HEAD_PAD]

    def scores(c, slot):
        for h in range(AH):
            s_scr[slot, h] = _dot(k_chunk(c, h), qs[h])

    def softmax_chunk(slot, st):
        out = []
        for h in range(AH):
            m, a1, _, acc = st[h]
            t = s_scr[slot, h]
            m_new = jnp.maximum(m, jnp.max(t, axis=0, keepdims=True))
            p_scr[slot, h] = jnp.exp2(t - m_new).astype(BF16)
            out.append((m_new, jnp.exp2(m - m_new), a1, acc))
        return out

    def weighted_values(c, slot, st, alphas):
        return [st[h][:3] + (alphas[h] * st[h][3] + _dot(vt_all[c, h], p_scr[slot, h]),) for h in range(AH)]

    def step(c, r, st, with_s, with_v):
        if with_s:
            scores(c + 2, (r + 2) % 3)
        alphas = [st[h][2] for h in range(AH)]
        st = softmax_chunk(r, st)
        if with_v:
            prev = max(c - 2, 0) if isinstance(c, int) else jnp.maximum(c - 2, 0)
            st = weighted_values(prev, (r + 1) % 3, st, alphas)
        return st

    p_scr[1] = jnp.zeros(p_scr.shape[1:], BF16)
    p_scr[2] = jnp.zeros(p_scr.shape[1:], BF16)
    one = jnp.ones((1, tq), F32)
    st = [(jnp.full((1, tq), NEG, F32), one, one, jnp.zeros((V_HEAD + V_ONES, tq), F32)) for _ in range(AH)]
    scores(0, 0)
    if n > 1:
        scores(1, 1)
    n_trip = 0 if ATTN_UNROLL else max(n - 2, 0) // 3

    def trip(j, flat):
        st = [tuple(flat[4 * h:4 * h + 4]) for h in range(AH)]
        for r in range(3):
            st = step(3 * j + r, r, st, True, True)
        return sum((tuple(s) for s in st), ())

    if n_trip:
        flat = lax.fori_loop(0, n_trip, trip, sum((tuple(s) for s in st), ()))
        st = [tuple(flat[4 * h:4 * h + 4]) for h in range(AH)]
    for c in range(3 * n_trip, n):
        st = step(c, c % 3, st, c + 2 < n, c >= 2)
    if n > 1:
        st = weighted_values(n - 2, (n - 2) % 3, st, [st[h][2] for h in range(AH)])
    st = weighted_values(n - 1, (n - 1) % 3, st, [st[h][1] for h in range(AH)])
    for h in range(AH):
        acc = st[h][3]
        o_ref[:, h * V_HEAD:(h + 1) * V_HEAD] = (acc[0:V_HEAD] / acc[V_HEAD:V_HEAD + 1]).T.astype(o_ref.dtype)


def _attention(q, k, vt, kc, vtc, *, n_seq, seq, row_block0, tq):
    n_ctx = 0 if kc is None else PAST_LEN
    n_hp = MLA_HEADS // AH
    nq = seq // tq
    kern = functools.partial(_attn_kernel, seq=seq, tq=tq, n_ctx=n_ctx)
    in_specs = [
        pl.BlockSpec((AH * HEAD_PAD, tq), lambda b, j, i: (j, (row_block0 + b) * nq + i)),
        pl.BlockSpec((seq, AH * HEAD_PAD), lambda b, j, i: (row_block0 + b, j)),
        pl.BlockSpec((seq // KC, AH * V_HEAD, KC), lambda b, j, i: (row_block0 + b, j, 0)),
    ]
    args = [q, k, vt]
    if n_ctx:
        in_specs += [
            pl.BlockSpec((n_ctx, AH * HEAD_PAD), lambda b, j, i: (b, j)),
            pl.BlockSpec((None, AH * V_HEAD, KC), lambda b, j, i: (b, j, 0)),
        ]
        args += [kc, vtc]
    return pl.pallas_call(
        kern,
        out_shape=jax.ShapeDtypeStruct((n_seq * seq, MLA_HEADS * V_HEAD), BF16),
        grid=(n_seq, n_hp, nq),
        in_specs=in_specs,
        out_specs=pl.BlockSpec((tq, AH * V_HEAD), lambda b, j, i: (b * nq + i, j)),
        scratch_shapes=[pltpu.VMEM((3, AH, KC, tq), F32), pltpu.VMEM((3, AH, KC, tq), BF16),
                        pltpu.VMEM((seq + n_ctx, AH * HEAD_PAD), BF16),
                        pltpu.VMEM(((seq + n_ctx) // KC, AH, V_HEAD + V_ONES, KC), BF16)],
        compiler_params=_cparams(("arbitrary", "arbitrary", "arbitrary")),
        name=f"mla_attention_s{seq}",
    )(*args)


def _post_mixer(y, x, mod_ref, lng_ref, lnb_ref, wrh_ref, wrl_ref, x1_ref, h2_ref, lgt_ref):
    z = ALPHA * x + mod_ref[2:3, :] * y
    x1 = _layernorm_rows(z, lng_ref[...], lnb_ref[...])
    x1_ref[...] = x1
    h2 = _modulated(x1, mod_ref, 3, 4)
    h_hi, h_lo = _split_hi_lo(h2)
    h2_ref[...] = h_hi
    w_hi, w_lo = wrh_ref[...], wrl_ref[...]
    lgt_ref[...] = _dot_nt(w_hi, h_hi) + _dot_nt(w_hi, h_lo) + _dot_nt(w_lo, h_hi)


def _out_ab_kernel(yrgp_ref, yrgs_ref, op_ref, os_ref, wa_ref, wb_ref, xp_ref, xs_ref,
                   mod_ref, lng_ref, lnb_ref, wrh_ref, wrl_ref, x1_ref, h2_ref, lgt_ref):
    y = (_dot(_pick_rows(yrgp_ref, yrgs_ref, TM).astype(BF16), wa_ref[...])
         + _dot(_pick_rows(op_ref, os_ref, TM), wb_ref[...]))
    _post_mixer(y, _pick_rows(xp_ref, xs_ref, TM), mod_ref, lng_ref, lnb_ref, wrh_ref, wrl_ref,
                x1_ref, h2_ref, lgt_ref)


def _post_specs():
    full = lambda shape: pl.BlockSpec(shape, lambda i: (0,) * len(shape))
    in_specs = [
        pl.BlockSpec((None, MOD_ROWS, D_MODEL), lambda i: (_cond_block(i, TM), 0, 0)),
        full((1, D_MODEL)), full((1, D_MODEL)),
        full((N_EXPERTS, D_MODEL)), full((N_EXPERTS, D_MODEL)),
    ]
    out_shape = (
        jax.ShapeDtypeStruct((N_TOK, D_MODEL), F32),
        jax.ShapeDtypeStruct((N_TOK, D_MODEL), BF16),
        jax.ShapeDtypeStruct((N_EXPERTS, N_TOK), F32),
    )
    out_specs = (
        pl.BlockSpec((TM, D_MODEL), lambda i: (i, 0)),
        pl.BlockSpec((TM, D_MODEL), lambda i: (i, 0)),
        pl.BlockSpec((N_EXPERTS, TM), lambda i: (0, i)),
    )
    return in_specs, out_shape, out_specs


def _out_ab(yrg_p, yrg_s, o_p, o_s, w_a, w_b, xp, xs, mods, lng, lnb, wr_hi, wr_lo):
    full = lambda shape: pl.BlockSpec(shape, lambda i: (0,) * len(shape))
    post_in, out_shape, out_specs = _post_specs()
    return pl.pallas_call(
        _out_ab_kernel,
        out_shape=out_shape,
        grid=(N_TOK // TM,),
        in_specs=(_row_pair_specs(TM, RG_W) + _row_pair_specs(TM, MLA_HEADS * V_HEAD)
                  + [full((RG_W, D_MODEL)), full((MLA_HEADS * V_HEAD, D_MODEL))]
                  + _row_pair_specs(TM, D_MODEL) + post_in),
        out_specs=out_specs,
        compiler_params=_cparams(("arbitrary",)),
        name="out_ab",
    )(yrg_p, yrg_s, o_p, o_s, w_a, w_b, xp, xs, mods, lng, lnb, wr_hi, wr_lo)


def _proj_c_kernel(x_ref, mod_ref, cos_ref, sin_ref, wq_ref, wk_ref, wv_ref, wg_ref, q_ref, k_ref, v_ref, g_ref):
    h = _modulated(x_ref[...], mod_ref, 0, 1).astype(BF16)
    cos, sin = cos_ref[...], sin_ref[...]
    half = RET_DK // 2
    for hd in range(RET_HEADS):
        for w_ref, is_k in ((wq_ref, False), (wk_ref, True)):
            p = _dot(h, w_ref[:, hd * RET_DK:(hd + 1) * RET_DK])
            x1, x2 = p[:, :half], p[:, half:]
            r1 = x1 * cos - x2 * sin
            r2 = x2 * cos + x1 * sin
            if not is_k:
                q_ref[:, hd * RET_DK:hd * RET_DK + half] = r1.astype(BF16)
                q_ref[:, hd * RET_DK + half:(hd + 1) * RET_DK] = r2.astype(BF16)
            else:
                t1 = (r1 * RET_DK ** -0.5).T.astype(BF16)
                t2 = (r2 * RET_DK ** -0.5).T.astype(BF16)
                for c in range(k_ref.shape[0]):
                    k_ref[c, hd * RET_DK:hd * RET_DK + half, :] = t1[:, c * RET_CHUNK:(c + 1) * RET_CHUNK]
                    k_ref[c, hd * RET_DK + half:(hd + 1) * RET_DK, :] = t2[:, c * RET_CHUNK:(c + 1) * RET_CHUNK]
    step = 512
    for j in range(MIX_C // step):
        v_ref[:, j * step:(j + 1) * step] = _dot(h, wv_ref[:, j * step:(j + 1) * step]).astype(BF16)
        g_ref[:, j * step:(j + 1) * step] = _dot(h, wg_ref[:, j * step:(j + 1) * step])


def _proj_c(x, mods, cos_t, sin_t, wq, wk, wv, wg):
    full = lambda shape: pl.BlockSpec(shape, lambda i: (0,) * len(shape))
    qk = RET_HEADS * RET_DK
    tab = pl.BlockSpec((TM, RET_DK // 2), lambda i: (_pos_block(i, TM), 0))
    return pl.pallas_call(
        _proj_c_kernel,
        out_shape=(
            jax.ShapeDtypeStruct((N_TOK, qk), BF16), jax.ShapeDtypeStruct((N_TOK // RET_CHUNK, qk, RET_CHUNK), BF16),
            jax.ShapeDtypeStruct((N_TOK, MIX_C), BF16), jax.ShapeDtypeStruct((N_TOK, MIX_C), F32),
        ),
        grid=(N_TOK // TM,),
        in_specs=[
            pl.BlockSpec((TM, D_MODEL), lambda i: (i, 0)),
            pl.BlockSpec((None, MOD_ROWS, D_MODEL), lambda i: (_cond_block(i, TM), 0, 0)),
            tab, tab,
            full((D_MODEL, qk)), full((D_MODEL, qk)), full((D_MODEL, MIX_C)), full((D_MODEL, MIX_C)),
        ],
        out_specs=(
            pl.BlockSpec((TM, qk), lambda i: (i, 0)),
            pl.BlockSpec((TM // RET_CHUNK, qk, RET_CHUNK), lambda i: (i, 0, 0)),
            pl.BlockSpec((TM, MIX_C), lambda i: (i, 0)), pl.BlockSpec((TM, MIX_C), lambda i: (i, 0)),
        ),
        compiler_params=_cparams(("arbitrary",), 56),
        name="proj_c",
    )(x, mods, cos_t, sin_t, wq, wk, wv, wg)


def _retention_kernel(*refs, seq, with_state):
    if with_state:
        q_ref, kt_ref, v_ref, gam_ref, r0_ref, o_ref, rfin_ref, r_s = refs
    else:
        q_ref, kt_ref, v_ref, gam_ref, r0_ref, o_ref, r_s = refs
        rfin_ref = None
    c = RET_CHUNK
    n = seq // c
    ii = lax.broadcasted_iota(I32, (c, c), 0).astype(F32)
    jj = lax.broadcasted_iota(I32, (c, c), 1).astype(F32)
    ci = lax.broadcasted_iota(I32, (c, 1), 0).astype(F32)
    li = lax.broadcasted_iota(I32, (1, c), 1).astype(F32)

    consts = []
    for d in range(2):
        gam = gam_ref[d]
        lg_row = -_softplus(-gam[0:1, :])
        lg = jnp.broadcast_to(lg_row, (c, c))
        lg_col = jnp.broadcast_to(lg_row[:, 0:1], (c, 1))
        if d == 0:
            diff = ii - jj
            xi = jnp.exp((ci + 1.0) * lg_col)
            zeta = jnp.exp((c - 1.0 - li) * lg_row)
        else:
            diff = jj - ii
            xi = jnp.exp((c - ci) * lg_col)
            zeta = jnp.exp(li * lg_row)
        dmat = jnp.where(diff >= 0, jnp.exp(jnp.maximum(diff, 0.0) * lg), 0.0)
        g_chunk = jnp.exp(float(c) * lg_row[:, 0:1])
        consts.append((dmat, xi, zeta, g_chunk))
        r_s[d] = r0_ref[d]

    def chunk(d, idx, accumulate):
        dmat, xi, zeta, g_chunk = consts[d]
        t0 = pl.multiple_of(idx * c, c)
        qb = q_ref[pl.ds(t0, c), :]
        kt = kt_ref[idx]
        vb = v_ref[pl.ds(t0, c), :]
        r = r_s[d]
        inner = _dot(qb, kt) * dmat
        o = _dot(inner.astype(BF16), vb) + _dot((qb.astype(F32) * xi).astype(BF16), r.astype(BF16))
        r_s[d] = r * g_chunk + _dot((kt.astype(F32) * zeta).astype(BF16), vb)
        if accumulate:
            o_ref[pl.ds(t0, c), :] = o_ref[pl.ds(t0, c), :] + o
        else:
            o_ref[pl.ds(t0, c), :] = o

    def first_half(s, carry):
        chunk(0, s, False)
        chunk(1, n - 1 - s, False)
        return carry

    def second_half(s, carry):
        chunk(0, s, True)
        chunk(1, n - 1 - s, True)
        return carry

    unroll = min(RET_UNROLL, n // 2)
    lax.fori_loop(0, n // 2, first_half, 0, unroll=unroll)
    lax.fori_loop(n // 2, n, second_half, 0, unroll=unroll)
    if with_state:
        for d in range(2):
            rfin_ref[d] = r_s[d]


def _retention(q, k, v, gam, r0, *, n_seq, seq, row_block0, with_state):
    kern = functools.partial(_retention_kernel, seq=seq, with_state=with_state)
    out_shape = [jax.ShapeDtypeStruct((n_seq * seq, MIX_C), F32)]
    out_specs = [pl.BlockSpec((seq, RET_DV), lambda b, h: (b, h))]
    if with_state:
        out_shape.append(jax.ShapeDtypeStruct((n_seq, 2, RET_HEADS, RET_DK, RET_DV), F32))
        out_specs.append(pl.BlockSpec((None, 2, None, RET_DK, RET_DV), lambda b, h: (b, 0, h, 0, 0)))
    return pl.pallas_call(
        kern,
        out_shape=tuple(out_shape),
        grid=(n_seq, RET_HEADS),
        in_specs=[
            pl.BlockSpec((seq, RET_DK), lambda b, h: (row_block0 + b, h)),
            pl.BlockSpec((seq // RET_CHUNK, RET_DK, RET_CHUNK), lambda b, h: (row_block0 + b, h, 0)),
            pl.BlockSpec((seq, RET_DV), lambda b, h: (row_block0 + b, h)),
            pl.BlockSpec((2, None, SUBLANE, LANE), lambda b, h: (0, h, 0, 0)),
            pl.BlockSpec((None, 2, None, RET_DK, RET_DV), lambda b, h: (b, 0, h, 0, 0)),
        ],
        out_specs=tuple(out_specs),
        scratch_shapes=[pltpu.VMEM((2, RET_DK, RET_DV), F32)],
        compiler_params=_cparams(("arbitrary", "arbitrary"), 56),
        name=f"retention_s{seq}",
    )(q, k, v, gam, r0)


def _out_c_kernel(op_ref, os_ref, g_ref, w_ref, x_ref, mod_ref, lng_ref, lnb_ref, wrh_ref, wrl_ref,
                  x1_ref, h2_ref, lgt_ref):
    y = None
    o_all = _pick_rows(op_ref, os_ref, TM)
    for hd in range(RET_HEADS):
        o = o_all[:, hd * RET_DV:(hd + 1) * RET_DV]
        mu = jnp.mean(o, axis=-1, keepdims=True)
        oc = o - mu
        var = jnp.mean(oc * oc, axis=-1, keepdims=True)
        on = oc * lax.rsqrt(var + EPS)
        a = (on * _silu(g_ref[:, hd * RET_DV:(hd + 1) * RET_DV])).astype(BF16)
        part = _dot(a, w_ref[hd * RET_DV:(hd + 1) * RET_DV, :])
        y = part if y is None else y + part
    _post_mixer(y, x_ref[...], mod_ref, lng_ref, lnb_ref, wrh_ref, wrl_ref, x1_ref, h2_ref, lgt_ref)


def _out_c(o_p, o_s, g, w, x, mods, lng, lnb, wr_hi, wr_lo):
    full = lambda shape: pl.BlockSpec(shape, lambda i: (0,) * len(shape))
    post_in, out_shape, out_specs = _post_specs()
    return pl.pallas_call(
        _out_c_kernel,
        out_shape=out_shape,
        grid=(N_TOK // TM,),
        in_specs=_row_pair_specs(TM, MIX_C) + [
            pl.BlockSpec((TM, MIX_C), lambda i: (i, 0)),
            full((MIX_C, D_MODEL)),
            pl.BlockSpec((TM, D_MODEL), lambda i: (i, 0)),
        ] + post_in,
        out_specs=out_specs,
        compiler_params=_cparams(("arbitrary",), 56),
        name="out_c",
    )(o_p, o_s, g, w, x, mods, lng, lnb, wr_hi, wr_lo)


def _route_kernel(lgt_ref, bias_ref, tri_ref, w_ref, lpos_ref, p16_ref):
    tt = lgt_ref.shape[1]
    scores = jax.nn.sigmoid(lgt_ref[...])
    sel = scores + bias_ref[...]
    srow = lax.broadcasted_iota(I32, (GROUP_SIZE, tt), 0).astype(F32)
    ninf = -jnp.inf

    gs = []
    for g in range(N_GROUPS):
        sg = sel[g * GROUP_SIZE:(g + 1) * GROUP_SIZE, :]
        m1 = jnp.max(sg, axis=0, keepdims=True)
        i1 = jnp.min(jnp.where(sg == m1, srow, float(GROUP_SIZE)), axis=0, keepdims=True)
        m2 = jnp.max(jnp.where(srow == i1, ninf, sg), axis=0, keepdims=True)
        gs.append(m1 + m2)
    gs = jnp.concatenate(gs, axis=0)
    chosen = jnp.zeros((N_GROUPS, tt), F32)
    for _ in range(TOPK_GROUPS):
        mg = jnp.max(gs, axis=0, keepdims=True)
        gi = jnp.min(jnp.where(gs == mg, srow, float(N_GROUPS)), axis=0, keepdims=True)
        hit = srow == gi
        chosen = jnp.where(hit, 1.0, chosen)
        gs = jnp.where(hit, ninf, gs)
    sel = jnp.concatenate(
        [jnp.where(jnp.broadcast_to(chosen[g:g + 1, :], (GROUP_SIZE, tt)) > 0.5,
                   sel[g * GROUP_SIZE:(g + 1) * GROUP_SIZE, :], ninf) for g in range(N_GROUPS)], axis=0)

    erow = lax.broadcasted_iota(I32, (N_EXPERTS, tt), 0).astype(F32)
    ids, ws = [], []
    for _ in range(TOP_K):
        m = jnp.max(sel, axis=0, keepdims=True)
        ei = jnp.min(jnp.where(sel == m, erow, float(N_EXPERTS)), axis=0, keepdims=True)
        hit = erow == ei
        ids.append(ei)
        ws.append(jnp.sum(jnp.where(hit, scores, 0.0), axis=0, keepdims=True))
        sel = jnp.where(hit, ninf, sel)
    wsum = ws[0]
    for k in range(1, TOP_K):
        wsum = wsum + ws[k]
    w_ref[...] = jnp.concatenate([w / wsum * ROUTED_SCALE for w in ws], axis=0)

    member_f = jnp.zeros((N_EXPERTS, tt), F32)
    for k in range(TOP_K):
        member_f = jnp.where(erow == ids[k], 1.0, member_f)
    member_b = member_f.astype(BF16)
    cnt_row = _dot_nt(jnp.ones((SUBLANE, tt), BF16), member_b)[0:1, :]
    p16_row = jnp.ceil(cnt_row * (1.0 / PIECE)) * PIECE
    lane_e = lax.broadcasted_iota(I32, (N_EXPERTS, N_EXPERTS), 1)
    sub_e = lax.broadcasted_iota(I32, (N_EXPERTS, N_EXPERTS), 0)
    run_start = jnp.sum(jnp.where(lane_e < sub_e, jnp.broadcast_to(p16_row, (N_EXPERTS, N_EXPERTS)), 0.0),
                        axis=1, keepdims=True)
    rank = _dot(member_b, tri_ref[...]) + run_start
    lpos_ref[...] = jnp.concatenate(
        [jnp.sum(jnp.where(erow == ids[k], rank, 0.0), axis=0, keepdims=True) for k in range(TOP_K)],
        axis=0).astype(I32)
    p16_ref[pl.ds(pl.program_id(0), 1), :] = jnp.concatenate(
        [p16_row, jnp.zeros((1, LANE - N_EXPERTS), F32)], axis=1)


def _route(lgt, bias, tri):
    return pl.pallas_call(
        _route_kernel,
        out_shape=(
            jax.ShapeDtypeStruct((TOP_K, N_TOK), F32), jax.ShapeDtypeStruct((TOP_K, N_TOK), I32),
            jax.ShapeDtypeStruct((N_WIN, LANE), F32),
        ),
        grid=(N_WIN,),
        in_specs=[
            pl.BlockSpec((N_EXPERTS, WIN), lambda i: (0, i)),
            pl.BlockSpec((N_EXPERTS, 1), lambda i: (0, 0)),
            pl.BlockSpec((WIN, WIN), lambda i: (0, 0)),
        ],
        out_specs=(
            pl.BlockSpec((TOP_K, WIN), lambda i: (0, i)), pl.BlockSpec((TOP_K, WIN), lambda i: (0, i)),
            pl.BlockSpec((N_WIN, LANE), lambda i: (0, 0)),
        ),
        compiler_params=_cparams(("arbitrary",)),
        name="moe_route",
    )(lgt, bias, tri)


def _sort_kernel(lpos_ref, h_ref, xl_ref, oh_all):
    lp = jnp.where(pl.program_id(0) < N_WIN, lpos_ref[...], -1)
    riota = lax.broadcasted_iota(I32, (SORT_ROWS, WIN), 0).astype(jnp.int16)
    one = jnp.ones((SORT_ROWS, WIN), BF16)
    for j in range(RL // SORT_ROWS):
        rel = (lp - j * SORT_ROWS).astype(jnp.int16)
        oh = jnp.zeros((SORT_ROWS, WIN), BF16)
        for k in range(TOP_K):
            oh = jnp.where(rel[k:k + 1, :] == riota, one, oh)
        oh_all[j * SORT_ROWS:(j + 1) * SORT_ROWS, :] = oh
    x = h_ref[...]
    for nt in range(D_MODEL // SORT_ROWS):
        cols = slice(nt * SORT_ROWS, (nt + 1) * SORT_ROWS)
        xl_ref[:, cols] = _dot(oh_all[...], x[:, cols]).astype(BF16)


def _sort_rows(lpos, h2):
    last = N_WIN - 1
    return pl.pallas_call(
        _sort_kernel,
        out_shape=jax.ShapeDtypeStruct(((N_WIN + 1) * RL, D_MODEL), BF16),
        grid=(N_WIN + 1,),
        in_specs=[
            pl.BlockSpec((TOP_K, WIN), lambda i: (0, jnp.minimum(i, last))),
            pl.BlockSpec((WIN, D_MODEL), lambda i: (jnp.minimum(i, last), 0)),
        ],
        out_specs=pl.BlockSpec((RL, D_MODEL), lambda i: (i, 0)),
        scratch_shapes=[pltpu.VMEM((RL, WIN), BF16)],
        compiler_params=_cparams(("arbitrary",)),
        name="moe_sort",
    )(lpos, h2)


def _moe_tables(p16):
    n_w = jnp.arange(N_WIN, dtype=I32)
    run_start = jnp.cumsum(p16, axis=1) - p16
    rw = jnp.sum(p16, axis=1)
    cum_w = jnp.cumsum(p16, axis=0) - p16
    tot = jnp.sum(p16, axis=0)
    nblk = (tot + TME - 1) // TME
    blk_end = jnp.cumsum(nblk)
    blk0 = blk_end - nblk
    n_used = blk_end[-1]
    b = jnp.arange(N_EBLOCKS + 1, dtype=I32)
    block_e = jnp.minimum(jnp.sum((blk_end[None, :] <= b[:, None]).astype(I32), axis=1), N_EXPERTS - 1)
    piece = jnp.arange(PIECES, dtype=I32) * PIECE
    rp = (b - blk0[block_e])[:, None] * TME + piece[None, :]
    valid = (b[:, None] < n_used) & (rp < tot[block_e][:, None])
    cum_e = cum_w.T[block_e]
    len_e = p16.T[block_e]
    start_e = run_start.T[block_e]
    in_win = (cum_e[:, None, :] <= rp[:, :, None]) & (rp[:, :, None] < (cum_e + len_e)[:, None, :])
    row = n_w[None, None, :] * RL + start_e[:, None, :] + rp[:, :, None] - cum_e[:, None, :]
    row = jnp.sum(jnp.where(in_win, row, 0), axis=2)
    zero_src = N_WIN * RL
    assert 3 * TME <= RL
    trash = N_WIN * RL + (1 + b % 2)[:, None] * TME + piece[None, :]
    gather_row = (jnp.where(valid, row, zero_src) // PIECE).reshape(-1).astype(I32)
    scatter_row = (jnp.where(valid, row, trash) // PIECE).reshape(-1).astype(I32)
    rw = jnp.concatenate([rw, jnp.zeros((1,), rw.dtype)])
    return rw.astype(I32), block_e.astype(I32), n_used.astype(I32).reshape(1), gather_row, scatter_row


def _expert_kernel(be_ref, nb_ref, grow_ref, srow_ref, xl_hbm, wg_ref, wu_ref, wd_ref, yl_hbm,
                   xbuf, ybuf, wg_b, wu_b, wd_b, gsem, ssem):
    b = pl.program_id(0)
    nb = nb_ref[0]

    @pl.when(jnp.logical_and(b < nb, jnp.logical_or(b == 0, be_ref[b] != be_ref[jnp.maximum(b - 1, 0)])))
    def _():
        wg_b[...] = wg_ref[...].astype(BF16)
        wu_b[...] = wu_ref[...].astype(BF16)
        wd_b[...] = wd_ref[...].astype(BF16)

    def gather_start(blk, slot):
        for p in range(PIECES):
            pltpu.make_async_copy(xl_hbm.at[grow_ref[blk * PIECES + p]], xbuf.at[slot, p], gsem.at[slot]).start()

    def scatter_start(blk, slot):
        for p in range(PIECES):
            pltpu.make_async_copy(ybuf.at[slot, p], yl_hbm.at[srow_ref[blk * PIECES + p]], ssem.at[slot]).start()

    def gather_wait(slot):
        pltpu.make_async_copy(xl_hbm.at[pl.ds(0, PIECES)], xbuf.at[slot], gsem.at[slot]).wait()

    def scatter_wait(slot):
        pltpu.make_async_copy(ybuf.at[slot], yl_hbm.at[pl.ds(0, PIECES)], ssem.at[slot]).wait()

    @pl.when(b < nb)
    def _():
        slot = b % 2

        @pl.when(b == 0)
        def _():
            gather_start(0, 0)
            ybuf[1] = jnp.zeros(ybuf.shape[1:], BF16)

        gather_wait(slot)

        @pl.when(b >= 1)
        def _():
            scatter_wait(slot)

        x = xbuf[slot].reshape(TME, D_MODEL)
        hb = _silu(_dot(x, wg_b[...])) * _dot(x, wu_b[...])
        gather_start(b + 1, 1 - slot)
        scatter_start(jnp.where(b == 0, N_EBLOCKS, b - 1), 1 - slot)
        ybuf[slot] = _dot(hb.astype(BF16), wd_b[...]).astype(BF16).reshape(PIECES, PIECE, D_MODEL)

        @pl.when(b == nb - 1)
        def _():
            scatter_start(b, slot)
            gather_wait(1 - slot)
            scatter_wait(1 - slot)
            scatter_wait(slot)


def _experts(block_e, n_used, gather_row, scatter_row, xl, wg, wu, wd, layer):
    def w_map(i, be, nb, gr, sr):
        return (layer, be[jnp.minimum(i, nb[0] - 1)], 0, 0)

    n_pieces = (N_WIN + 1) * RL // PIECE
    yl = pl.pallas_call(
        _expert_kernel,
        out_shape=jax.ShapeDtypeStruct((n_pieces, PIECE, D_MODEL), BF16),
        input_output_aliases={4: 0},
        grid_spec=pltpu.PrefetchScalarGridSpec(
            num_scalar_prefetch=4,
            grid=(N_EBLOCKS,),
            in_specs=[
                pl.BlockSpec(memory_space=pl.ANY),
                pl.BlockSpec((None, None, D_MODEL, D_EXPERT), w_map),
                pl.BlockSpec((None, None, D_MODEL, D_EXPERT), w_map),
                pl.BlockSpec((None, None, D_EXPERT, D_MODEL), w_map),
            ],
            out_specs=pl.BlockSpec(memory_space=pl.ANY),
            scratch_shapes=[
                pltpu.VMEM((2, PIECES, PIECE, D_MODEL), BF16), pltpu.VMEM((2, PIECES, PIECE, D_MODEL), BF16),
                pltpu.VMEM((D_MODEL, D_EXPERT), BF16), pltpu.VMEM((D_MODEL, D_EXPERT), BF16),
                pltpu.VMEM((D_EXPERT, D_MODEL), BF16),
                pltpu.SemaphoreType.DMA((2,)), pltpu.SemaphoreType.DMA((2,)),
            ],
        ),
        compiler_params=_cparams(("arbitrary",)),
        name="moe_experts",
    )(block_e, n_used, gather_row, scatter_row, xl.reshape(n_pieces, PIECE, D_MODEL), wg, wu, wd)
    return yl.reshape(n_pieces * PIECE, D_MODEL)


def _combine_kernel(rw_ref, yl_ref, lpt_ref, wt_ref, h_ref, x1_ref, mod_ref, lng_ref, lnb_ref,
                    wsg_ref, wsu_ref, wsd_ref, *rest, split):
    del rw_ref
    if split:
        outp_ref, outs_ref, p_w, lp_b, wt_b = rest
    else:
        out_ref, p_w, lp_b, wt_b = rest
    hb = h_ref[...]
    shared = _dot((_silu(_dot(hb, wsg_ref[...])) * _dot(hb, wsu_ref[...])).astype(BF16), wsd_ref[...])
    lp = lpt_ref[...]
    wt = wt_ref[...]
    for k in range(TOP_K):
        lp_b[k] = jnp.broadcast_to(lp[:, k:k + 1], (WIN, SORT_ROWS)).astype(jnp.int16)
        wt_b[k] = jnp.broadcast_to(wt[:, k:k + 1], (WIN, SORT_ROWS)).astype(BF16)
    ciota = lax.broadcasted_iota(I32, (WIN, SORT_ROWS), 1)
    for j in range(RL // SORT_ROWS):
        col = (ciota + j * SORT_ROWS).astype(jnp.int16)
        pm = jnp.zeros((WIN, SORT_ROWS), BF16)
        for k in range(TOP_K):
            pm = jnp.where(lp_b[k] == col, wt_b[k], pm)
        p_w[:, j * SORT_ROWS:(j + 1) * SORT_ROWS] = pm
    routed = _dot(p_w[...], yl_ref[...])
    z = ALPHA * x1_ref[...] + mod_ref[5:6, :] * (routed + shared)
    out = _layernorm_rows(z, lng_ref[...], lnb_ref[...])
    if split:
        @pl.when(pl.program_id(0) < N_PROMPT // WIN)
        def _():
            outp_ref[...] = out

        @pl.when(pl.program_id(0) >= N_PROMPT // WIN)
        def _():
            outs_ref[...] = out
    else:
        out_ref[...] = out


def _combine(rw, yl, lpos_t, wt, h2, x1, mods, lng, lnb, wsg, wsu, wsd, *, split):
    full = lambda shape: pl.BlockSpec(shape, lambda i, rw: (0,) * len(shape))
    if split:
        out_shape = (jax.ShapeDtypeStruct((N_PROMPT, D_MODEL), F32), jax.ShapeDtypeStruct((N_SAMPLE, D_MODEL), F32))
        out_specs = tuple(_row_pair_specs(WIN, D_MODEL))
    else:
        out_shape = jax.ShapeDtypeStruct((N_TOK, D_MODEL), F32)
        out_specs = pl.BlockSpec((WIN, D_MODEL), lambda i, rw: (i, 0))
    return pl.pallas_call(
        functools.partial(_combine_kernel, split=split),
        out_shape=out_shape,
        grid_spec=pltpu.PrefetchScalarGridSpec(
            num_scalar_prefetch=1,
            grid=(N_WIN,),
            in_specs=[
                pl.BlockSpec((RL, D_MODEL), lambda i, rw: (i, 0)),
                pl.BlockSpec((WIN, TOP_K), lambda i, rw: (i, 0)),
                pl.BlockSpec((WIN, TOP_K), lambda i, rw: (i, 0)),
                pl.BlockSpec((WIN, D_MODEL), lambda i, rw: (i, 0)),
                pl.BlockSpec((WIN, D_MODEL), lambda i, rw: (i, 0)),
                pl.BlockSpec((None, MOD_ROWS, D_MODEL), lambda i, rw: (_cond_block(i, WIN), 0, 0)),
                full((1, D_MODEL)), full((1, D_MODEL)),
                full((D_MODEL, D_EXPERT)), full((D_MODEL, D_EXPERT)), full((D_EXPERT, D_MODEL)),
            ],
            out_specs=out_specs,
            scratch_shapes=[pltpu.VMEM((WIN, RL), BF16),
                            pltpu.VMEM((TOP_K, WIN, SORT_ROWS), jnp.int16),
                            pltpu.VMEM((TOP_K, WIN, SORT_ROWS), BF16)],
        ),
        compiler_params=_cparams(("arbitrary",)),
        name="moe_combine",
    )(rw, yl, lpos_t, wt, h2, x1, mods, lng, lnb, wsg, wsu, wsd)


def _moe_and_norm(x1, h2, lgt, mods, lng, lnb, router_bias, tri, wg, wu, wd, wsg, wsu, wsd, *, layer, split):
    wts, lpos, p16 = _route(lgt, router_bias.reshape(N_EXPERTS, 1), tri)
    rw, block_e, n_used, gather_row, scatter_row = _moe_tables(p16[:, :N_EXPERTS].astype(I32))
    xl = _sort_rows(lpos, h2)
    yl = _experts(block_e, n_used, gather_row, scatter_row, xl, wg, wu, wd, layer)
    return _combine(rw, yl, lpos.T, wts.T, h2, x1, mods, lng, lnb, wsg, wsu, wsd, split=split)


def _rope_tables_mla():
    t = jnp.arange(DEC_SEQ)
    row = (t // GRID_W).astype(F32)
    col = (t % GRID_W).astype(F32)
    n = QK_ROPE // 4
    inv = ROPE_BASE ** (-jnp.arange(n, dtype=F32) / n)
    ang_r = row[:, None] * inv
    ang_c = col[:, None] * inv
    cos = jnp.ones((DEC_SEQ, LANE), F32)
    sin_m = jnp.zeros((DEC_SEQ, LANE), F32)
    sin_p = jnp.zeros((DEC_SEQ, LANE), F32)
    l0 = ROPE_LANE0
    for base, ang in ((l0, ang_r), (l0 + 2 * n, ang_c)):
        c, s = jnp.cos(ang), jnp.sin(ang)
        cos = cos.at[:, base:base + n].set(c).at[:, base + n:base + 2 * n].set(c)
        sin_m = sin_m.at[:, base:base + n].set(-s)
        sin_p = sin_p.at[:, base + n:base + 2 * n].set(s)
    ident = (jnp.ones((TM, LANE), F32), jnp.zeros((TM, LANE), F32), jnp.zeros((TM, LANE), F32))
    return tuple(jnp.concatenate([i, tbl], axis=0) for i, tbl in zip(ident, (cos, sin_m, sin_p)))


def _rope_tables_ret():
    half = RET_DK // 2
    theta = ROPE_BASE ** (-jnp.linspace(0.0, 1.0, half, dtype=F32))
    ang = jnp.arange(DEC_SEQ, dtype=F32)[:, None] * theta
    cos = jnp.concatenate([jnp.ones((TM, half), F32), jnp.cos(ang)], axis=0)
    sin = jnp.concatenate([jnp.zeros((TM, half), F32), jnp.sin(ang)], axis=0)
    return cos, sin


def _pad_heads(w, width, lane0=0):
    k = w.shape[0]
    w = w.reshape(k, MLA_HEADS, width)
    out = jnp.zeros((k, MLA_HEADS, HEAD_PAD), w.dtype).at[:, :, lane0:lane0 + width].set(w)
    return out.reshape(k, MLA_HEADS * HEAD_PAD)


def _rg_gate_weights(wa, ba, wx, bx):
    n_ct = RG_W // LANE
    per = LANE // RG_BW
    tiles_w, tiles_b = [], []
    for c in range(n_ct):
        cols_w, cols_b = [], []
        for d in range(2):
            for w, b in ((wa, ba), (wx, bx)):
                m = jnp.zeros((LANE, LANE), F32)
                for p in range(per):
                    m = m.at[p * RG_BW:(p + 1) * RG_BW, p * RG_BW:(p + 1) * RG_BW].set(w[d, c * per + p])
                cols_w.append(m)
                cols_b.append(b[d, c * LANE:(c + 1) * LANE])
        tiles_w.append(jnp.concatenate(cols_w, axis=1))
        tiles_b.append(jnp.concatenate(cols_b, axis=0)[None, :])
    return jnp.stack(tiles_w).astype(BF16), jnp.stack(tiles_b)


def kernel(x_prompt, x_sample, cache_mla_ckv, cache_mla_krope, state_rglru, state_ret, c, c_ctx, w_ada, b_ada,
           ln_g, ln_b, w_in_ab, rg_conv_w, rg_conv_b, rg_wa, rg_ba, rg_wx, rg_bx, rg_lambda, mla_q_norm, mla_w_uq,
           mla_kv_norm, mla_w_ukv, w_out_ab, w_in_c, ret_gamma_logit, w_out_c, w_router, router_bias,
           w_exp_gate, w_exp_up, w_exp_down, w_sh_gate, w_sh_up, w_sh_down):
    xp = x_prompt.reshape(N_PROMPT, D_MODEL)
    xs = x_sample.reshape(N_SAMPLE, D_MODEL)
    cond = jnp.zeros((16, D_MODEL), F32).at[0].set(c_ctx).at[1:1 + DEC_BATCH].set(c)
    mods_all = _ada_modulation(cond, w_ada, b_ada).reshape(DEPTH, 16, 6, D_MODEL)[:, :N_COND]
    mods_all = jnp.pad(mods_all, ((0, 0), (0, 0), (0, MOD_ROWS - 6), (0, 0)))

    tri = (jnp.arange(WIN)[:, None] < jnp.arange(WIN)[None, :]).astype(BF16)
    wr_t = jnp.swapaxes(w_router, 1, 2)
    wr_hi = wr_t.astype(BF16)
    wr_lo = (wr_t - wr_hi.astype(F32)).astype(BF16)
    wg_e, wu_e, wd_e = w_exp_gate, w_exp_up, w_exp_down
    wsg, wsu, wsd = w_sh_gate.astype(BF16), w_sh_up.astype(BF16), w_sh_down.astype(BF16)

    l, e = 0, 0
    mods = mods_all[l]
    n_main = 2 * RG_W + Q_LORA + KV_LORA
    w_main = w_in_ab[e][:, :n_main].astype(BF16)
    w_kr = jnp.zeros((D_MODEL, LANE), F32).at[:, ROPE_LANE0:ROPE_LANE0 + QK_ROPE].set(w_in_ab[e][:, n_main:]).astype(BF16)
    main, krp = _proj_ab(xp, xs, mods, w_main, w_kr)

    wg_rg, bg_rg = _rg_gate_weights(rg_wa[e], rg_ba[e], rg_wx[e], rg_bx[e])
    h0_p = jnp.zeros((BATCH, 2, RG_W), F32)
    rg_args = (rg_conv_w[e], rg_conv_b[e].reshape(1, RG_W), wg_rg, bg_rg, rg_lambda[e])
    yrg_p, rg_fin = _rglru(main, *rg_args, h0_p, n_seq=BATCH, seq=SEQ, row_block0=0)
    yrg_s, _ = _rglru(main, *rg_args, state_rglru[:, e],
                      n_seq=DEC_BATCH, seq=DEC_SEQ, row_block0=N_PROMPT // DEC_SEQ)

    cos_t, sm_t, sp_t = _rope_tables_mla()
    w_uq = mla_w_uq[e].reshape(Q_LORA, MLA_HEADS, QK_NOPE + QK_ROPE)
    wq_p = _pad_heads(w_uq.reshape(Q_LORA, -1), QK_NOPE + QK_ROPE).astype(BF16)
    w_ukv = mla_w_ukv[e].reshape(KV_LORA, MLA_HEADS, QK_NOPE + V_HEAD)
    wuk_p = _pad_heads(w_ukv[:, :, :QK_NOPE].reshape(KV_LORA, -1), QK_NOPE).astype(BF16)
    wuvt = w_ukv[:, :, QK_NOPE:].reshape(KV_LORA, MLA_HEADS * V_HEAD).T.astype(BF16)
    q_att, k_att, v_att, ckv_n = _mla_prep(main, krp, cos_t, sm_t, sp_t, mla_q_norm[e].reshape(1, Q_LORA), wq_p,
                                           mla_kv_norm[e].reshape(1, KV_LORA), wuk_p, wuvt)
    ctx_ckv = cache_mla_ckv[:, e].reshape(DEC_BATCH * PAST_LEN, KV_LORA)
    ctx_krp = jnp.zeros((DEC_BATCH * PAST_LEN, LANE), F32).at[:, ROPE_LANE0:ROPE_LANE0 + QK_ROPE].set(
        cache_mla_krope[:, e].reshape(DEC_BATCH * PAST_LEN, QK_ROPE))
    kc_att, vc_att = _mla_ctx(ctx_ckv, ctx_krp, wuk_p, wuvt)

    o_att_p = _attention(q_att, k_att, v_att, None, None, n_seq=BATCH, seq=SEQ, row_block0=0, tq=SEQ)
    o_att_s = _attention(q_att, k_att, v_att, kc_att, vc_att,
                         n_seq=DEC_BATCH, seq=DEC_SEQ, row_block0=N_PROMPT // DEC_SEQ, tq=TQ)

    w_out = w_out_ab[e].astype(BF16)
    x1, h2, lgt = _out_ab(yrg_p, yrg_s, o_att_p, o_att_s, w_out[:RG_W], w_out[RG_W:], xp, xs, mods,
                          ln_g[l, 0].reshape(1, D_MODEL), ln_b[l, 0].reshape(1, D_MODEL), wr_hi[l], wr_lo[l])
    x = _moe_and_norm(x1, h2, lgt, mods, ln_g[l, 1].reshape(1, D_MODEL), ln_b[l, 1].reshape(1, D_MODEL),
                      router_bias[l], tri, wg_e, wu_e, wd_e, wsg[l], wsu[l], wsd[l], layer=l, split=False)

    new_ckv = ckv_n.reshape(BATCH, 1, SEQ, KV_LORA)
    new_krope = krp[:N_PROMPT, ROPE_LANE0:ROPE_LANE0 + QK_ROPE].reshape(BATCH, 1, SEQ, QK_ROPE)
    new_rg = rg_fin.reshape(BATCH, 1, 2, RG_W)

    l, o = 1, 0
    mods = mods_all[l]
    qk = RET_HEADS * RET_DK
    w_c = w_in_c[o].astype(BF16)
    cos_r, sin_r = _rope_tables_ret()
    q_r, k_r, v_r, g_r = _proj_c(x, mods, cos_r, sin_r, w_c[:, :qk], w_c[:, qk:2 * qk],
                                 w_c[:, 2 * qk:2 * qk + MIX_C], w_c[:, 2 * qk + MIX_C:])
    gam = jnp.broadcast_to(ret_gamma_logit[o].astype(F32)[:, :, None, None], (2, RET_HEADS, SUBLANE, LANE))
    r0_p = jnp.zeros((BATCH, 2, RET_HEADS, RET_DK, RET_DV), F32)
    o_ret_p, r_fin = _retention(q_r, k_r, v_r, gam, r0_p, n_seq=BATCH, seq=SEQ, row_block0=0, with_state=True)
    (o_ret_s,) = _retention(q_r, k_r, v_r, gam, state_ret[:, o],
                            n_seq=DEC_BATCH, seq=DEC_SEQ, row_block0=N_PROMPT // DEC_SEQ, with_state=False)
    x1, h2, lgt = _out_c(o_ret_p, o_ret_s, g_r, w_out_c[o].astype(BF16), x, mods,
                         ln_g[l, 0].reshape(1, D_MODEL), ln_b[l, 0].reshape(1, D_MODEL), wr_hi[l], wr_lo[l])
    y_p, y_s = _moe_and_norm(x1, h2, lgt, mods, ln_g[l, 1].reshape(1, D_MODEL), ln_b[l, 1].reshape(1, D_MODEL),
                             router_bias[l], tri, wg_e, wu_e, wd_e, wsg[l], wsu[l], wsd[l], layer=l, split=True)

    y_prompt = y_p.reshape(BATCH, SEQ, D_MODEL)
    y_sample = y_s.reshape(DEC_BATCH, DEC_SEQ, D_MODEL)
    new_ret = r_fin.reshape(BATCH, 1, 2, RET_HEADS, RET_DK, RET_DV)
    return (y_prompt, y_sample, new_ckv, new_krope, new_rg, new_ret)
```

```python
import functools
import math

import jax
import jax.numpy as jnp
from jax import lax
from jax.experimental import pallas as pl
from jax.experimental.pallas import tpu as pltpu

F32 = jnp.float32
BF16 = jnp.bfloat16
I32 = jnp.int32

D_MODEL = 1024
BATCH, SEQ = 16, 256
DEC_BATCH, DEC_SEQ = 8, 4096
PAST_LEN = 256
DEPTH = 2
GRID_W = 64
RG_W, RG_BLOCKS = 512, 8
RG_BW = RG_W // RG_BLOCKS
RG_C = 8.0
CONV_W, CONV_LEFT = 4, 2
MLA_HEADS, QK_NOPE, QK_ROPE, V_HEAD = 8, 64, 32, 64
Q_LORA, KV_LORA = 768, 256
ROPE_BASE = 10000.0
ATTN_SCALE = (QK_NOPE + QK_ROPE) ** -0.5
RET_HEADS, RET_DK, RET_DV, RET_CHUNK = 4, 256, 512, 128
MIX_C = RET_HEADS * RET_DV
N_EXPERTS, TOP_K, N_GROUPS, TOPK_GROUPS = 64, 8, 8, 4
GROUP_SIZE = N_EXPERTS // N_GROUPS
D_EXPERT = 256
ROUTED_SCALE = 2.5
ALPHA = (2 * DEPTH) ** 0.25
EPS = 1e-6

N_PROMPT = BATCH * SEQ
N_SAMPLE = DEC_BATCH * DEC_SEQ
N_TOK = N_PROMPT + N_SAMPLE
N_COND = 1 + DEC_BATCH
MOD_ROWS = 8

LANE = 128
SUBLANE = 8
TM = 512
HEAD_PAD = 128
ROPE_LANE0 = QK_NOPE
TQ = 256
KC = 256
AH = 2
ATTN_UNROLL = True
RET_UNROLL = 4
V_ONES = 16
Q_PRESCALE = ATTN_SCALE * math.log2(math.e)
SCAN_ROWS = 64
GATE_ROWS = 256
WIN = 256
N_WIN = N_TOK // WIN
PIECE = 16
SORT_ROWS = 256
RL = 3072
TME = 1024
PIECES = TME // PIECE
EXPERT_BUFS = 3
SPARE_BLOCKS = 2
SPARE_WIN = 2
N_PAIRS = N_TOK * TOP_K
N_EBLOCKS = (N_PAIRS + N_WIN * N_EXPERTS * (PIECE - 1)) // TME + N_EXPERTS
NEG = -1e30


def _cparams(sem, vmem_mb=48):
    return pltpu.CompilerParams(dimension_semantics=sem, vmem_limit_bytes=vmem_mb * 1024 * 1024)


def _cond_block(i, tm):
    npb = N_PROMPT // tm
    return jnp.where(i < npb, 0, 1 + (i - npb) // (DEC_SEQ // tm))


def _pos_block(i, tm):
    npb = N_PROMPT // tm
    return jnp.where(i < npb, 0, 1 + (i - npb) % (DEC_SEQ // tm))


def _split_hi_lo(a):
    hi = a.astype(BF16)
    lo = (a - hi.astype(F32)).astype(BF16)
    return hi, lo


def _dot(a, b):
    return jnp.dot(a, b, preferred_element_type=F32)


def _dot_nt(a, b):
    return lax.dot_general(a, b, (((1,), (1,)), ((), ())), preferred_element_type=F32)


def _silu(x):
    return x * jax.nn.sigmoid(x)


def _gelu_tanh(x):
    return 0.5 * x * (1.0 + jnp.tanh(math.sqrt(2.0 / math.pi) * (x + 0.044715 * (x * x * x))))


def _softplus(x):
    return jnp.maximum(x, 0.0) + jnp.log1p(jnp.exp(-jnp.abs(x)))


def _layernorm_rows(z, g, b):
    mu = jnp.mean(z, axis=-1, keepdims=True)
    zc = z - mu
    var = jnp.mean(zc * zc, axis=-1, keepdims=True)
    return (zc * lax.rsqrt(var + EPS)) * g + b


def _ada_kernel(c_ref, w_ref, b_ref, o_ref):
    s_hi, s_lo = _split_hi_lo(_silu(c_ref[...]))
    w_hi, w_lo = _split_hi_lo(w_ref[...])
    o_ref[...] = _dot(s_hi, w_hi) + _dot(s_hi, w_lo) + _dot(s_lo, w_hi) + b_ref[...]


def _ada_modulation(cond, w_ada, b_ada):
    n6 = 6 * D_MODEL
    tn = D_MODEL
    return pl.pallas_call(
        _ada_kernel,
        out_shape=jax.ShapeDtypeStruct((DEPTH, 16, n6), F32),
        grid=(DEPTH, n6 // tn),
        in_specs=[
            pl.BlockSpec((16, D_MODEL), lambda l, j: (0, 0)),
            pl.BlockSpec((None, D_MODEL, tn), lambda l, j: (l, 0, j)),
            pl.BlockSpec((None, 1, tn), lambda l, j: (l, 0, j)),
        ],
        out_specs=pl.BlockSpec((None, 16, tn), lambda l, j: (l, 0, j)),
        compiler_params=_cparams(("arbitrary", "arbitrary")),
        name="ada_modulation",
    )(cond, w_ada, b_ada.reshape(DEPTH, 1, n6))


def _modulated(x, mod_ref, shift_row, scale_row):
    return x * (1.0 + mod_ref[scale_row:scale_row + 1, :]) + mod_ref[shift_row:shift_row + 1, :]


def _row_pair_specs(tm, width, col=0):
    npb = N_PROMPT // tm
    return [pl.BlockSpec((tm, width), lambda i, *_: (jnp.minimum(i, npb - 1), col)),
            pl.BlockSpec((tm, width), lambda i, *_: (jnp.maximum(i - npb, 0), col))]


def _pick_rows(p_ref, s_ref, tm):
    return jnp.where(pl.program_id(0) < N_PROMPT // tm, p_ref[...], s_ref[...])


def _proj_ab_kernel(xp_ref, xs_ref, mod_ref, w_ref, wkr_ref, main_ref, kr_ref):
    h = _modulated(_pick_rows(xp_ref, xs_ref, TM), mod_ref, 0, 1).astype(BF16)
    n = w_ref.shape[1]
    step = 512
    for j in range(n // step):
        main_ref[:, j * step:(j + 1) * step] = _dot(h, w_ref[:, j * step:(j + 1) * step])
    kr_ref[...] = _dot(h, wkr_ref[...])


def _proj_ab(xp, xs, mods, w_main, w_kr):
    n = w_main.shape[1]
    return pl.pallas_call(
        _proj_ab_kernel,
        out_shape=(jax.ShapeDtypeStruct((N_TOK, n), F32), jax.ShapeDtypeStruct((N_TOK, LANE), F32)),
        grid=(N_TOK // TM,),
        in_specs=_row_pair_specs(TM, D_MODEL) + [
            pl.BlockSpec((None, MOD_ROWS, D_MODEL), lambda i: (_cond_block(i, TM), 0, 0)),
            pl.BlockSpec((D_MODEL, n), lambda i: (0, 0)),
            pl.BlockSpec((D_MODEL, LANE), lambda i: (0, 0)),
        ],
        out_specs=(pl.BlockSpec((TM, n), lambda i: (i, 0)), pl.BlockSpec((TM, LANE), lambda i: (i, 0))),
        compiler_params=_cparams(("arbitrary",)),
        name="proj_ab",
    )(xp, xs, mods, w_main, w_kr)


def _rglru_kernel(xr_ref, gr_ref, cw_ref, cb_ref, wg_ref, bg_ref, lam_ref, h0_ref,
                  y_ref, hfin_ref, xpad, a_s, b_s, *, seq):
    pad = SUBLANE
    xpad[0:pad, :] = jnp.zeros((pad, LANE), F32)
    xpad[seq + pad:seq + 2 * pad, :] = jnp.zeros((pad, LANE), F32)
    xpad[pad:seq + pad, :] = xr_ref[...]

    sp = _softplus(-lam_ref[...])
    cw = cw_ref[...]
    cb = cb_ref[...]
    wg = wg_ref[...]
    bg = bg_ref[...]

    def gate_step(c, carry):
        t0 = pl.multiple_of(c * GATE_ROWS, GATE_ROWS)
        win = xpad[pl.ds(t0, GATE_ROWS + 2 * pad), :]
        xc = cb
        for j in range(CONV_W):
            off = pad - CONV_LEFT + j
            xc = xc + win[off:off + GATE_ROWS, :] * cw[j:j + 1, :]
        g = _dot(xc.astype(BF16), wg) + bg
        for d in range(2):
            r = jax.nn.sigmoid(g[:, (2 * d) * LANE:(2 * d + 1) * LANE])
            i = jax.nn.sigmoid(g[:, (2 * d + 1) * LANE:(2 * d + 2) * LANE])
            log_a = (-RG_C * r) * sp[d:d + 1, :]
            a = jnp.exp(log_a)
            t = jnp.tanh(log_a)
            bt = jnp.sqrt(2.0 * t / (t - 1.0)) * (i * xc)
            a_s[d, pl.ds(t0, GATE_ROWS), :] = a
            b_s[d, pl.ds(t0, GATE_ROWS), :] = bt
        return carry

    lax.fori_loop(0, seq // GATE_ROWS, gate_step, 0, unroll=min(2, seq // GATE_ROWS))

    row = lax.broadcasted_iota(I32, (SCAN_ROWS, LANE), 0) % SUBLANE
    n_steps = seq // SCAN_ROWS
    tiles = SCAN_ROWS // SUBLANE

    def local_scan(a, b, reverse):
        for k in (1, 2, 4):
            if reverse:
                ok = row < SUBLANE - k
                shift = SCAN_ROWS - k
            else:
                ok = row >= k
                shift = k
            a_sh = jnp.where(ok, pltpu.roll(a, shift, 0), 1.0)
            b_sh = jnp.where(ok, pltpu.roll(b, shift, 0), 0.0)
            b = a * b_sh + b
            a = a * a_sh
        return a, b

    def fwd_step(c, h):
        t0 = pl.multiple_of(c * SCAN_ROWS, SCAN_ROWS)
        a, b = local_scan(a_s[0, pl.ds(t0, SCAN_ROWS), :], b_s[0, pl.ds(t0, SCAN_ROWS), :], False)
        outs = []
        for j in range(tiles):
            hj = a[j * SUBLANE:(j + 1) * SUBLANE, :] * h + b[j * SUBLANE:(j + 1) * SUBLANE, :]
            outs.append(hj)
            h = hj[SUBLANE - 1:SUBLANE, :]
        y_ref[pl.ds(t0, SCAN_ROWS), :] = jnp.concatenate(outs, axis=0)
        return h

    h_f = lax.fori_loop(0, n_steps, fwd_step, h0_ref[0:1, :], unroll=4)

    def bwd_step(c, h):
        t0 = pl.multiple_of((n_steps - 1 - c) * SCAN_ROWS, SCAN_ROWS)
        a, b = local_scan(a_s[1, pl.ds(t0, SCAN_ROWS), :], b_s[1, pl.ds(t0, SCAN_ROWS), :], True)
        outs = [None] * tiles
        for j in reversed(range(tiles)):
            hj = a[j * SUBLANE:(j + 1) * SUBLANE, :] * h + b[j * SUBLANE:(j + 1) * SUBLANE, :]
            outs[j] = hj
            h = hj[0:1, :]
        hb = jnp.concatenate(outs, axis=0)
        y_ref[pl.ds(t0, SCAN_ROWS), :] = (y_ref[pl.ds(t0, SCAN_ROWS), :] + hb) * _gelu_tanh(gr_ref[pl.ds(t0, SCAN_ROWS), :])
        return h

    h_b = lax.fori_loop(0, n_steps, bwd_step, h0_ref[1:2, :], unroll=4)
    hfin_ref[0:1, :] = h_f
    hfin_ref[1:2, :] = h_b


def _rglru(main, cw, cb, wg, bg, lam, h0, *, n_seq, seq, row_block0):
    n_ct = RG_W // LANE
    gr_col0 = RG_W // LANE
    kern = functools.partial(_rglru_kernel, seq=seq)
    return pl.pallas_call(
        kern,
        out_shape=(jax.ShapeDtypeStruct((n_seq * seq, RG_W), F32), jax.ShapeDtypeStruct((n_seq, 2, RG_W), F32)),
        grid=(n_seq, n_ct),
        in_specs=[
            pl.BlockSpec((seq, LANE), lambda b, c: (row_block0 + b, c)),
            pl.BlockSpec((seq, LANE), lambda b, c: (row_block0 + b, gr_col0 + c)),
            pl.BlockSpec((CONV_W, LANE), lambda b, c: (0, c)),
            pl.BlockSpec((1, LANE), lambda b, c: (0, c)),
            pl.BlockSpec((None, LANE, 4 * LANE), lambda b, c: (c, 0, 0)),
            pl.BlockSpec((None, 1, 4 * LANE), lambda b, c: (c, 0, 0)),
            pl.BlockSpec((2, LANE), lambda b, c: (0, c)),
            pl.BlockSpec((None, 2, LANE), lambda b, c: (b, 0, c)),
        ],
        out_specs=(
            pl.BlockSpec((seq, LANE), lambda b, c: (b, c)),
            pl.BlockSpec((None, 2, LANE), lambda b, c: (b, 0, c)),
        ),
        scratch_shapes=[
            pltpu.VMEM((seq + 2 * SUBLANE, LANE), F32),
            pltpu.VMEM((2, seq, LANE), F32),
            pltpu.VMEM((2, seq, LANE), F32),
        ],
        compiler_params=_cparams(("arbitrary", "arbitrary")),
        name=f"rglru_s{seq}",
    )(main, main, cw, cb, wg, bg, lam, h0)


def _rope_lanes(x, cos, sin_m, sin_p):
    n = x.shape[1] // LANE
    half = QK_ROPE // 4
    cos_t = jnp.concatenate([cos] * n, axis=1) if n > 1 else cos
    sm_t = jnp.concatenate([sin_m] * n, axis=1) if n > 1 else sin_m
    sp_t = jnp.concatenate([sin_p] * n, axis=1) if n > 1 else sin_p
    up = pltpu.roll(x, x.shape[1] - half, 1)
    dn = pltpu.roll(x, half, 1)
    return x * cos_t + up * sm_t + dn * sp_t


def _mla_prep_kernel(cq0_ref, cq1_ref, cq2_ref, ckv_ref, kr_ref, cos_ref, sm_ref, sp_ref,
                     qn_ref, wq_ref, kvn_ref, wuk_ref, wuvt_ref,
                     q_ref, k_ref, vt_ref, ckvn_ref):
    cq = [cq0_ref[...], cq1_ref[...], cq2_ref[...]]
    ms = (jnp.sum(cq[0] * cq[0], axis=-1, keepdims=True) + jnp.sum(cq[1] * cq[1], axis=-1, keepdims=True)
          + jnp.sum(cq[2] * cq[2], axis=-1, keepdims=True)) * (1.0 / Q_LORA)
    inv = lax.rsqrt(ms + EPS)
    blk = Q_LORA // 3
    q = None
    for j in range(3):
        cqn = ((cq[j] * inv) * qn_ref[:, j * blk:(j + 1) * blk]).astype(BF16)
        part = _dot(cqn, wq_ref[j * blk:(j + 1) * blk, :])
        q = part if q is None else q + part
    cos, sm, sp = cos_ref[...], sm_ref[...], sp_ref[...]
    q_ref[...] = (_rope_lanes(q, cos, sm, sp) * Q_PRESCALE).T.astype(BF16)

    ckv = ckv_ref[...]
    inv_kv = lax.rsqrt(jnp.mean(ckv * ckv, axis=-1, keepdims=True) + EPS)
    ckvn = (ckv * inv_kv) * kvn_ref[...]

    @pl.when(pl.program_id(0) < N_PROMPT // TM)
    def _():
        ckvn_ref[...] = ckvn

    ckvn_b = ckvn.astype(BF16)
    kr_rot = _rope_lanes(kr_ref[...], cos, sm, sp)
    k_ref[...] = (_dot(ckvn_b, wuk_ref[...]) + jnp.concatenate([kr_rot] * MLA_HEADS, axis=1)).astype(BF16)
    vt = _dot_nt(wuvt_ref[...], ckvn_b).astype(BF16)
    for c in range(vt_ref.shape[0]):
        vt_ref[c] = vt[:, c * KC:(c + 1) * KC]


def _mla_prep(main, krp, cos_t, sm_t, sp_t, q_norm, wq_p, kv_norm, wuk_p, wuvt):
    cq_col0 = 2 * RG_W // 256
    hp = MLA_HEADS * HEAD_PAD
    full = lambda shape: pl.BlockSpec(shape, lambda i: (0,) * len(shape))
    tab = pl.BlockSpec((TM, LANE), lambda i: (_pos_block(i, TM), 0))
    return pl.pallas_call(
        _mla_prep_kernel,
        out_shape=(
            jax.ShapeDtypeStruct((hp, N_TOK), BF16),
            jax.ShapeDtypeStruct((N_TOK, hp), BF16),
            jax.ShapeDtypeStruct((N_TOK // KC, MLA_HEADS * V_HEAD, KC), BF16),
            jax.ShapeDtypeStruct((N_PROMPT, KV_LORA), F32),
        ),
        grid=(N_TOK // TM,),
        in_specs=[
            pl.BlockSpec((TM, 256), lambda i: (i, cq_col0)),
            pl.BlockSpec((TM, 256), lambda i: (i, cq_col0 + 1)),
            pl.BlockSpec((TM, 256), lambda i: (i, cq_col0 + 2)),
            pl.BlockSpec((TM, 256), lambda i: (i, cq_col0 + 3)),
            pl.BlockSpec((TM, LANE), lambda i: (i, 0)),
            tab, tab, tab,
            full((1, Q_LORA)), full((Q_LORA, hp)), full((1, KV_LORA)), full((KV_LORA, hp)),
            full((MLA_HEADS * V_HEAD, KV_LORA)),
        ],
        out_specs=(
            pl.BlockSpec((hp, TM), lambda i: (0, i)),
            pl.BlockSpec((TM, hp), lambda i: (i, 0)),
            pl.BlockSpec((TM // KC, MLA_HEADS * V_HEAD, KC), lambda i: (i, 0, 0)),
            pl.BlockSpec((TM, KV_LORA), lambda i: (jnp.minimum(i, N_PROMPT // TM - 1), 0)),
        ),
        compiler_params=_cparams(("arbitrary",)),
        name="mla_prep",
    )(main, main, main, main, krp, cos_t, sm_t, sp_t, q_norm, wq_p, kv_norm, wuk_p, wuvt)


def _mla_ctx_kernel(ckv_ref, kr_ref, wuk_ref, wuvt_ref, k_ref, vt_ref):
    ckv_b = ckv_ref[...].astype(BF16)
    k_ref[...] = (_dot(ckv_b, wuk_ref[...]) + jnp.concatenate([kr_ref[...]] * MLA_HEADS, axis=1)).astype(BF16)
    vt_ref[...] = _dot_nt(wuvt_ref[...], ckv_b).astype(BF16)


def _mla_ctx(ctx_ckv, ctx_krp, wuk_p, wuvt):
    n = ctx_ckv.shape[0]
    hp = MLA_HEADS * HEAD_PAD
    tm = KC
    full = lambda shape: pl.BlockSpec(shape, lambda i: (0,) * len(shape))
    return pl.pallas_call(
        _mla_ctx_kernel,
        out_shape=(jax.ShapeDtypeStruct((n, hp), BF16),
                   jax.ShapeDtypeStruct((n // tm, MLA_HEADS * V_HEAD, tm), BF16)),
        grid=(n // tm,),
        in_specs=[
            pl.BlockSpec((tm, KV_LORA), lambda i: (i, 0)),
            pl.BlockSpec((tm, LANE), lambda i: (i, 0)),
            full((KV_LORA, hp)), full((MLA_HEADS * V_HEAD, KV_LORA)),
        ],
        out_specs=(pl.BlockSpec((tm, hp), lambda i: (i, 0)),
                   pl.BlockSpec((None, MLA_HEADS * V_HEAD, tm), lambda i: (i, 0, 0))),
        compiler_params=_cparams(("arbitrary",)),
        name="mla_ctx",
    )(ctx_ckv, ctx_krp, wuk_p, wuvt)


def _attn_kernel(*refs, seq, tq, n_ctx):
    if n_ctx:
        q_ref, k_ref, vt_ref, kc_ref, vtc_ref, o_ref, s_scr, p_scr, k_all, vt_all = refs
    else:
        q_ref, k_ref, vt_ref, o_ref, s_scr, p_scr, k_all, vt_all = refs
    has_ctx = 1 if n_ctx else 0
    n_own = seq // KC
    n = n_own + has_ctx
    qs = [q_ref[h * HEAD_PAD:(h + 1) * HEAD_PAD, :] for h in range(AH)]

    @pl.when(pl.program_id(2) == 0)
    def _():
        k_all[0:seq, :] = k_ref[...]
        if has_ctx:
            k_all[seq:seq + KC, :] = kc_ref[...]
        for h in range(AH):
            vt_all[0:n_own, h, 0:V_HEAD, :] = vt_ref[:, h * V_HEAD:(h + 1) * V_HEAD, :]
            if has_ctx:
                vt_all[n_own, h, 0:V_HEAD, :] = vtc_ref[h * V_HEAD:(h + 1) * V_HEAD, :]
            vt_all[:, h, V_HEAD:V_HEAD + V_ONES, :] = jnp.ones((n, V_ONES, KC), BF16)

    def k_chunk(c, h):
        t0 = c * KC if isinstance(c, int) else pl.multiple_of(c * KC, KC)
        return k_all[pl.ds(t0, KC), h * HEAD_PAD:(h + 1) * HEAD_PAD]

    def scores(c, slot):
        for h in range(AH):
            s_scr[slot, h] = _dot(k_chunk(c, h), qs[h])

    def softmax_chunk(slot, st):
        out = []
        for h in range(AH):
            m, a1, _, acc = st[h]
            t = s_scr[slot, h]
            m_new = jnp.maximum(m, jnp.max(t, axis=0, keepdims=True))
            p_scr[slot, h] = jnp.exp2(t - m_new).astype(BF16)
            out.append((m_new, jnp.exp2(m - m_new), a1, acc))
        return out

    def weighted_values(c, slot, st, alphas):
        return [st[h][:3] + (alphas[h] * st[h][3] + _dot(vt_all[c, h], p_scr[slot, h]),) for h in range(AH)]

    def step(c, r, st, with_s, with_v):
        if with_s:
            scores(c + 2, (r + 2) % 3)
        alphas = [st[h][2] for h in range(AH)]
        st = softmax_chunk(r, st)
        if with_v:
            prev = max(c - 2, 0) if isinstance(c, int) else jnp.maximum(c - 2, 0)
            st = weighted_values(prev, (r + 1) % 3, st, alphas)
        return st

    p_scr[1] = jnp.zeros(p_scr.shape[1:], BF16)
    p_scr[2] = jnp.zeros(p_scr.shape[1:], BF16)
    one = jnp.ones((1, tq), F32)
    st = [(jnp.full((1, tq), NEG, F32), one, one, jnp.zeros((V_HEAD + V_ONES, tq), F32)) for _ in range(AH)]
    scores(0, 0)
    if n > 1:
        scores(1, 1)
    n_trip = 0 if ATTN_UNROLL else max(n - 2, 0) // 3

    def trip(j, flat):
        st = [tuple(flat[4 * h:4 * h + 4]) for h in range(AH)]
        for r in range(3):
            st = step(3 * j + r, r, st, True, True)
        return sum((tuple(s) for s in st), ())

    if n_trip:
        flat = lax.fori_loop(0, n_trip, trip, sum((tuple(s) for s in st), ()))
        st = [tuple(flat[4 * h:4 * h + 4]) for h in range(AH)]
    for c in range(3 * n_trip, n):
        st = step(c, c % 3, st, c + 2 < n, c >= 2)
    if n > 1:
        st = weighted_values(n - 2, (n - 2) % 3, st, [st[h][2] for h in range(AH)])
    st = weighted_values(n - 1, (n - 1) % 3, st, [st[h][1] for h in range(AH)])
    for h in range(AH):
        acc = st[h][3]
        o_ref[:, h * V_HEAD:(h + 1) * V_HEAD] = (acc[0:V_HEAD] / acc[V_HEAD:V_HEAD + 1]).T.astype(o_ref.dtype)


def _attention(q, k, vt, kc, vtc, *, n_seq, seq, row_block0, tq):
    n_ctx = 0 if kc is None else PAST_LEN
    n_hp = MLA_HEADS // AH
    nq = seq // tq
    kern = functools.partial(_attn_kernel, seq=seq, tq=tq, n_ctx=n_ctx)
    in_specs = [
        pl.BlockSpec((AH * HEAD_PAD, tq), lambda b, j, i: (j, (row_block0 + b) * nq + i)),
        pl.BlockSpec((seq, AH * HEAD_PAD), lambda b, j, i: (row_block0 + b, j)),
        pl.BlockSpec((seq // KC, AH * V_HEAD, KC), lambda b, j, i: (row_block0 + b, j, 0)),
    ]
    args = [q, k, vt]
    if n_ctx:
        in_specs += [
            pl.BlockSpec((n_ctx, AH * HEAD_PAD), lambda b, j, i: (b, j)),
            pl.BlockSpec((None, AH * V_HEAD, KC), lambda b, j, i: (b, j, 0)),
        ]
        args += [kc, vtc]
    return pl.pallas_call(
        kern,
        out_shape=jax.ShapeDtypeStruct((n_seq * seq, MLA_HEADS * V_HEAD), BF16),
        grid=(n_seq, n_hp, nq),
        in_specs=in_specs,
        out_specs=pl.BlockSpec((tq, AH * V_HEAD), lambda b, j, i: (b * nq + i, j)),
        scratch_shapes=[pltpu.VMEM((3, AH, KC, tq), F32), pltpu.VMEM((3, AH, KC, tq), BF16),
                        pltpu.VMEM((seq + n_ctx, AH * HEAD_PAD), BF16),
                        pltpu.VMEM(((seq + n_ctx) // KC, AH, V_HEAD + V_ONES, KC), BF16)],
        compiler_params=_cparams(("arbitrary", "arbitrary", "arbitrary")),
        name=f"mla_attention_s{seq}",
    )(*args)


def _post_mixer(y, x, mod_ref, lng_ref, lnb_ref, wrh_ref, wrl_ref, x1_ref, h2_ref, lgt_ref):
    z = ALPHA * x + mod_ref[2:3, :] * y
    x1 = _layernorm_rows(z, lng_ref[...], lnb_ref[...])
    x1_ref[...] = x1
    h2 = _modulated(x1, mod_ref, 3, 4)
    h_hi, h_lo = _split_hi_lo(h2)
    h2_ref[...] = h_hi
    w_hi, w_lo = wrh_ref[...], wrl_ref[...]
    lgt_ref[...] = _dot_nt(w_hi, h_hi) + _dot_nt(w_hi, h_lo) + _dot_nt(w_lo, h_hi)


def _out_ab_kernel(yrgp_ref, yrgs_ref, op_ref, os_ref, wa_ref, wb_ref, xp_ref, xs_ref,
                   mod_ref, lng_ref, lnb_ref, wrh_ref, wrl_ref, x1_ref, h2_ref, lgt_ref):
    y = (_dot(_pick_rows(yrgp_ref, yrgs_ref, TM).astype(BF16), wa_ref[...])
         + _dot(_pick_rows(op_ref, os_ref, TM), wb_ref[...]))
    _post_mixer(y, _pick_rows(xp_ref, xs_ref, TM), mod_ref, lng_ref, lnb_ref, wrh_ref, wrl_ref,
                x1_ref, h2_ref, lgt_ref)


def _post_specs():
    full = lambda shape: pl.BlockSpec(shape, lambda i: (0,) * len(shape))
    in_specs = [
        pl.BlockSpec((None, MOD_ROWS, D_MODEL), lambda i: (_cond_block(i, TM), 0, 0)),
        full((1, D_MODEL)), full((1, D_MODEL)),
        full((N_EXPERTS, D_MODEL)), full((N_EXPERTS, D_MODEL)),
    ]
    out_shape = (
        jax.ShapeDtypeStruct((N_TOK, D_MODEL), F32),
        jax.ShapeDtypeStruct((N_TOK, D_MODEL), BF16),
        jax.ShapeDtypeStruct((N_EXPERTS, N_TOK), F32),
    )
    out_specs = (
        pl.BlockSpec((TM, D_MODEL), lambda i: (i, 0)),
        pl.BlockSpec((TM, D_MODEL), lambda i: (i, 0)),
        pl.BlockSpec((N_EXPERTS, TM), lambda i: (0, i)),
    )
    return in_specs, out_shape, out_specs


def _out_ab(yrg_p, yrg_s, o_p, o_s, w_a, w_b, xp, xs, mods, lng, lnb, wr_hi, wr_lo):
    full = lambda shape: pl.BlockSpec(shape, lambda i: (0,) * len(shape))
    post_in, out_shape, out_specs = _post_specs()
    return pl.pallas_call(
        _out_ab_kernel,
        out_shape=out_shape,
        grid=(N_TOK // TM,),
        in_specs=(_row_pair_specs(TM, RG_W) + _row_pair_specs(TM, MLA_HEADS * V_HEAD)
                  + [full((RG_W, D_MODEL)), full((MLA_HEADS * V_HEAD, D_MODEL))]
                  + _row_pair_specs(TM, D_MODEL) + post_in),
        out_specs=out_specs,
        compiler_params=_cparams(("arbitrary",)),
        name="out_ab",
    )(yrg_p, yrg_s, o_p, o_s, w_a, w_b, xp, xs, mods, lng, lnb, wr_hi, wr_lo)


def _proj_c_kernel(x_ref, mod_ref, cos_ref, sin_ref, wq_ref, wk_ref, wv_ref, wg_ref, q_ref, k_ref, v_ref, g_ref):
    h = _modulated(x_ref[...], mod_ref, 0, 1).astype(BF16)
    cos, sin = cos_ref[...], sin_ref[...]
    half = RET_DK // 2
    for hd in range(RET_HEADS):
        for w_ref, is_k in ((wq_ref, False), (wk_ref, True)):
            p = _dot(h, w_ref[:, hd * RET_DK:(hd + 1) * RET_DK])
            x1, x2 = p[:, :half], p[:, half:]
            r1 = x1 * cos - x2 * sin
            r2 = x2 * cos + x1 * sin
            if not is_k:
                q_ref[:, hd * RET_DK:hd * RET_DK + half] = r1.astype(BF16)
                q_ref[:, hd * RET_DK + half:(hd + 1) * RET_DK] = r2.astype(BF16)
            else:
                t1 = (r1 * RET_DK ** -0.5).T.astype(BF16)
                t2 = (r2 * RET_DK ** -0.5).T.astype(BF16)
                for c in range(k_ref.shape[0]):
                    k_ref[c, hd * RET_DK:hd * RET_DK + half, :] = t1[:, c * RET_CHUNK:(c + 1) * RET_CHUNK]
                    k_ref[c, hd * RET_DK + half:(hd + 1) * RET_DK, :] = t2[:, c * RET_CHUNK:(c + 1) * RET_CHUNK]
    step = 512
    for j in range(MIX_C // step):
        v_ref[:, j * step:(j + 1) * step] = _dot(h, wv_ref[:, j * step:(j + 1) * step]).astype(BF16)
        g_ref[:, j * step:(j + 1) * step] = _dot(h, wg_ref[:, j * step:(j + 1) * step])


def _proj_c(x, mods, cos_t, sin_t, wq, wk, wv, wg):
    full = lambda shape: pl.BlockSpec(shape, lambda i: (0,) * len(shape))
    qk = RET_HEADS * RET_DK
    tab = pl.BlockSpec((TM, RET_DK // 2), lambda i: (_pos_block(i, TM), 0))
    return pl.pallas_call(
        _proj_c_kernel,
        out_shape=(
            jax.ShapeDtypeStruct((N_TOK, qk), BF16), jax.ShapeDtypeStruct((N_TOK // RET_CHUNK, qk, RET_CHUNK), BF16),
            jax.ShapeDtypeStruct((N_TOK, MIX_C), BF16), jax.ShapeDtypeStruct((N_TOK, MIX_C), F32),
        ),
        grid=(N_TOK // TM,),
        in_specs=[
            pl.BlockSpec((TM, D_MODEL), lambda i: (i, 0)),
            pl.BlockSpec((None, MOD_ROWS, D_MODEL), lambda i: (_cond_block(i, TM), 0, 0)),
            tab, tab,
            full((D_MODEL, qk)), full((D_MODEL, qk)), full((D_MODEL, MIX_C)), full((D_MODEL, MIX_C)),
        ],
        out_specs=(
            pl.BlockSpec((TM, qk), lambda i: (i, 0)),
            pl.BlockSpec((TM // RET_CHUNK, qk, RET_CHUNK), lambda i: (i, 0, 0)),
            pl.BlockSpec((TM, MIX_C), lambda i: (i, 0)), pl.BlockSpec((TM, MIX_C), lambda i: (i, 0)),
        ),
        compiler_params=_cparams(("arbitrary",), 56),
        name="proj_c",
    )(x, mods, cos_t, sin_t, wq, wk, wv, wg)


def _retention_kernel(*refs, seq, with_state):
    if with_state:
        q_ref, kt_ref, v_ref, gam_ref, r0_ref, o_ref, rfin_ref, r_s = refs
    else:
        q_ref, kt_ref, v_ref, gam_ref, r0_ref, o_ref, r_s = refs
        rfin_ref = None
    c = RET_CHUNK
    n = seq // c
    ii = lax.broadcasted_iota(I32, (c, c), 0).astype(F32)
    jj = lax.broadcasted_iota(I32, (c, c), 1).astype(F32)
    ci = lax.broadcasted_iota(I32, (c, 1), 0).astype(F32)
    li = lax.broadcasted_iota(I32, (1, c), 1).astype(F32)

    consts = []
    for d in range(2):
        gam = gam_ref[d]
        lg_row = -_softplus(-gam[0:1, :])
        lg = jnp.broadcast_to(lg_row, (c, c))
        lg_col = jnp.broadcast_to(lg_row[:, 0:1], (c, 1))
        if d == 0:
            diff = ii - jj
            xi = jnp.exp((ci + 1.0) * lg_col)
            zeta = jnp.exp((c - 1.0 - li) * lg_row)
        else:
            diff = jj - ii
            xi = jnp.exp((c - ci) * lg_col)
            zeta = jnp.exp(li * lg_row)
        dmat = jnp.where(diff >= 0, jnp.exp(jnp.maximum(diff, 0.0) * lg), 0.0)
        g_chunk = jnp.exp(float(c) * lg_row[:, 0:1])
        consts.append((dmat, xi, zeta, g_chunk))
        r_s[d] = r0_ref[d]

    def chunk(d, idx, accumulate):
        dmat, xi, zeta, g_chunk = consts[d]
        t0 = pl.multiple_of(idx * c, c)
        qb = q_ref[pl.ds(t0, c), :]
        kt = kt_ref[idx]
        vb = v_ref[pl.ds(t0, c), :]
        r = r_s[d]
        inner = _dot(qb, kt) * dmat
        o = _dot(inner.astype(BF16), vb) + _dot((qb.astype(F32) * xi).astype(BF16), r.astype(BF16))
        r_s[d] = r * g_chunk + _dot((kt.astype(F32) * zeta).astype(BF16), vb)
        if accumulate:
            o_ref[pl.ds(t0, c), :] = o_ref[pl.ds(t0, c), :] + o
        else:
            o_ref[pl.ds(t0, c), :] = o

    def first_half(s, carry):
        chunk(0, s, False)
        chunk(1, n - 1 - s, False)
        return carry

    def second_half(s, carry):
        chunk(0, s, True)
        chunk(1, n - 1 - s, True)
        return carry

    unroll = min(RET_UNROLL, n // 2)
    lax.fori_loop(0, n // 2, first_half, 0, unroll=unroll)
    lax.fori_loop(n // 2, n, second_half, 0, unroll=unroll)
    if with_state:
        for d in range(2):
            rfin_ref[d] = r_s[d]


def _retention(q, k, v, gam, r0, *, n_seq, seq, row_block0, with_state):
    kern = functools.partial(_retention_kernel, seq=seq, with_state=with_state)
    out_shape = [jax.ShapeDtypeStruct((n_seq * seq, MIX_C), F32)]
    out_specs = [pl.BlockSpec((seq, RET_DV), lambda b, h: (b, h))]
    if with_state:
        out_shape.append(jax.ShapeDtypeStruct((n_seq, 2, RET_HEADS, RET_DK, RET_DV), F32))
        out_specs.append(pl.BlockSpec((None, 2, None, RET_DK, RET_DV), lambda b, h: (b, 0, h, 0, 0)))
    return pl.pallas_call(
        kern,
        out_shape=tuple(out_shape),
        grid=(n_seq, RET_HEADS),
        in_specs=[
            pl.BlockSpec((seq, RET_DK), lambda b, h: (row_block0 + b, h)),
            pl.BlockSpec((seq // RET_CHUNK, RET_DK, RET_CHUNK), lambda b, h: (row_block0 + b, h, 0)),
            pl.BlockSpec((seq, RET_DV), lambda b, h: (row_block0 + b, h)),
            pl.BlockSpec((2, None, SUBLANE, LANE), lambda b, h: (0, h, 0, 0)),
            pl.BlockSpec((None, 2, None, RET_DK, RET_DV), lambda b, h: (b, 0, h, 0, 0)),
        ],
        out_specs=tuple(out_specs),
        scratch_shapes=[pltpu.VMEM((2, RET_DK, RET_DV), F32)],
        compiler_params=_cparams(("arbitrary", "arbitrary"), 56),
        name=f"retention_s{seq}",
    )(q, k, v, gam, r0)


def _out_c_kernel(op_ref, os_ref, g_ref, w_ref, x_ref, mod_ref, lng_ref, lnb_ref, wrh_ref, wrl_ref,
                  x1_ref, h2_ref, lgt_ref):
    y = None
    o_all = _pick_rows(op_ref, os_ref, TM)
    for hd in range(RET_HEADS):
        o = o_all[:, hd * RET_DV:(hd + 1) * RET_DV]
        mu = jnp.mean(o, axis=-1, keepdims=True)
        oc = o - mu
        var = jnp.mean(oc * oc, axis=-1, keepdims=True)
        on = oc * lax.rsqrt(var + EPS)
        a = (on * _silu(g_ref[:, hd * RET_DV:(hd + 1) * RET_DV])).astype(BF16)
        part = _dot(a, w_ref[hd * RET_DV:(hd + 1) * RET_DV, :])
        y = part if y is None else y + part
    _post_mixer(y, x_ref[...], mod_ref, lng_ref, lnb_ref, wrh_ref, wrl_ref, x1_ref, h2_ref, lgt_ref)


def _out_c(o_p, o_s, g, w, x, mods, lng, lnb, wr_hi, wr_lo):
    full = lambda shape: pl.BlockSpec(shape, lambda i: (0,) * len(shape))
    post_in, out_shape, out_specs = _post_specs()
    return pl.pallas_call(
        _out_c_kernel,
        out_shape=out_shape,
        grid=(N_TOK // TM,),
        in_specs=_row_pair_specs(TM, MIX_C) + [
            pl.BlockSpec((TM, MIX_C), lambda i: (i, 0)),
            full((MIX_C, D_MODEL)),
            pl.BlockSpec((TM, D_MODEL), lambda i: (i, 0)),
        ] + post_in,
        out_specs=out_specs,
        compiler_params=_cparams(("arbitrary",), 56),
        name="out_c",
    )(o_p, o_s, g, w, x, mods, lng, lnb, wr_hi, wr_lo)


def _route_kernel(lgt_ref, bias_ref, tri_ref, w_ref, lpos_ref, p16_ref):
    tt = lgt_ref.shape[1]
    scores = jax.nn.sigmoid(lgt_ref[...])
    sel = scores + bias_ref[...]
    srow = lax.broadcasted_iota(I32, (GROUP_SIZE, tt), 0).astype(F32)
    ninf = -jnp.inf

    gs = []
    for g in range(N_GROUPS):
        sg = sel[g * GROUP_SIZE:(g + 1) * GROUP_SIZE, :]
        m1 = jnp.max(sg, axis=0, keepdims=True)
        i1 = jnp.min(jnp.where(sg == m1, srow, float(GROUP_SIZE)), axis=0, keepdims=True)
        m2 = jnp.max(jnp.where(srow == i1, ninf, sg), axis=0, keepdims=True)
        gs.append(m1 + m2)
    gs = jnp.concatenate(gs, axis=0)
    chosen = jnp.zeros((N_GROUPS, tt), F32)
    for _ in range(TOPK_GROUPS):
        mg = jnp.max(gs, axis=0, keepdims=True)
        gi = jnp.min(jnp.where(gs == mg, srow, float(N_GROUPS)), axis=0, keepdims=True)
        hit = srow == gi
        chosen = jnp.where(hit, 1.0, chosen)
        gs = jnp.where(hit, ninf, gs)
    sel = jnp.concatenate(
        [jnp.where(jnp.broadcast_to(chosen[g:g + 1, :], (GROUP_SIZE, tt)) > 0.5,
                   sel[g * GROUP_SIZE:(g + 1) * GROUP_SIZE, :], ninf) for g in range(N_GROUPS)], axis=0)

    erow = lax.broadcasted_iota(I32, (N_EXPERTS, tt), 0).astype(F32)
    ids, ws = [], []
    for _ in range(TOP_K):
        m = jnp.max(sel, axis=0, keepdims=True)
        ei = jnp.min(jnp.where(sel == m, erow, float(N_EXPERTS)), axis=0, keepdims=True)
        hit = erow == ei
        ids.append(ei)
        ws.append(jnp.sum(jnp.where(hit, scores, 0.0), axis=0, keepdims=True))
        sel = jnp.where(hit, ninf, sel)
    wsum = ws[0]
    for k in range(1, TOP_K):
        wsum = wsum + ws[k]
    w_ref[...] = jnp.concatenate([w / wsum * ROUTED_SCALE for w in ws], axis=0)

    member_f = jnp.zeros((N_EXPERTS, tt), F32)
    for k in range(TOP_K):
        member_f = jnp.where(erow == ids[k], 1.0, member_f)
    member_b = member_f.astype(BF16)
    cnt_row = _dot_nt(jnp.ones((SUBLANE, tt), BF16), member_b)[0:1, :]
    p16_row = jnp.ceil(cnt_row * (1.0 / PIECE)) * PIECE
    lane_e = lax.broadcasted_iota(I32, (N_EXPERTS, N_EXPERTS), 1)
    sub_e = lax.broadcasted_iota(I32, (N_EXPERTS, N_EXPERTS), 0)
    run_start = jnp.sum(jnp.where(lane_e < sub_e, jnp.broadcast_to(p16_row, (N_EXPERTS, N_EXPERTS)), 0.0),
                        axis=1, keepdims=True)
    rank = _dot(member_b, tri_ref[...]) + run_start
    lpos_ref[...] = jnp.concatenate(
        [jnp.sum(jnp.where(erow == ids[k], rank, 0.0), axis=0, keepdims=True) for k in range(TOP_K)],
        axis=0).astype(I32)
    p16_ref[pl.ds(pl.program_id(0), 1), :] = jnp.concatenate(
        [p16_row, jnp.zeros((1, LANE - N_EXPERTS), F32)], axis=1)


def _route(lgt, bias, tri):
    return pl.pallas_call(
        _route_kernel,
        out_shape=(
            jax.ShapeDtypeStruct((TOP_K, N_TOK), F32), jax.ShapeDtypeStruct((TOP_K, N_TOK), I32),
            jax.ShapeDtypeStruct((N_WIN, LANE), F32),
        ),
        grid=(N_WIN,),
        in_specs=[
            pl.BlockSpec((N_EXPERTS, WIN), lambda i: (0, i)),
            pl.BlockSpec((N_EXPERTS, 1), lambda i: (0, 0)),
            pl.BlockSpec((WIN, WIN), lambda i: (0, 0)),
        ],
        out_specs=(
            pl.BlockSpec((TOP_K, WIN), lambda i: (0, i)), pl.BlockSpec((TOP_K, WIN), lambda i: (0, i)),
            pl.BlockSpec((N_WIN, LANE), lambda i: (0, 0)),
        ),
        compiler_params=_cparams(("arbitrary",)),
        name="moe_route",
    )(lgt, bias, tri)


def _sort_kernel(lpos_ref, h_ref, xl_ref, oh_all):
    lp = jnp.where(pl.program_id(0) < N_WIN, lpos_ref[...], -1)
    riota = lax.broadcasted_iota(I32, (SORT_ROWS, WIN), 0).astype(jnp.int16)
    one = jnp.ones((SORT_ROWS, WIN), BF16)
    for j in range(RL // SORT_ROWS):
        rel = (lp - j * SORT_ROWS).astype(jnp.int16)
        oh = jnp.zeros((SORT_ROWS, WIN), BF16)
        for k in range(TOP_K):
            oh = jnp.where(rel[k:k + 1, :] == riota, one, oh)
        oh_all[j * SORT_ROWS:(j + 1) * SORT_ROWS, :] = oh
    x = h_ref[...]
    for nt in range(D_MODEL // SORT_ROWS):
        cols = slice(nt * SORT_ROWS, (nt + 1) * SORT_ROWS)
        xl_ref[:, cols] = _dot(oh_all[...], x[:, cols]).astype(BF16)


def _sort_rows(lpos, h2):
    last = N_WIN - 1
    return pl.pallas_call(
        _sort_kernel,
        out_shape=jax.ShapeDtypeStruct(((N_WIN + SPARE_WIN) * RL, D_MODEL), BF16),
        grid=(N_WIN + SPARE_WIN,),
        in_specs=[
            pl.BlockSpec((TOP_K, WIN), lambda i: (0, jnp.minimum(i, last))),
            pl.BlockSpec((WIN, D_MODEL), lambda i: (jnp.minimum(i, last), 0)),
        ],
        out_specs=pl.BlockSpec((RL, D_MODEL), lambda i: (i, 0)),
        scratch_shapes=[pltpu.VMEM((RL, WIN), BF16)],
        compiler_params=_cparams(("arbitrary",)),
        name="moe_sort",
    )(lpos, h2)


def _moe_tables(p16):
    n_w = jnp.arange(N_WIN, dtype=I32)
    run_start = jnp.cumsum(p16, axis=1) - p16
    rw = jnp.sum(p16, axis=1)
    cum_w = jnp.cumsum(p16, axis=0) - p16
    tot = jnp.sum(p16, axis=0)
    nblk = (tot + TME - 1) // TME
    blk_end = jnp.cumsum(nblk)
    blk0 = blk_end - nblk
    n_used = blk_end[-1]
    b = jnp.arange(N_EBLOCKS + SPARE_BLOCKS, dtype=I32)
    block_e = jnp.minimum(jnp.sum((blk_end[None, :] <= b[:, None]).astype(I32), axis=1), N_EXPERTS - 1)
    piece = jnp.arange(PIECES, dtype=I32) * PIECE
    rp = (b - blk0[block_e])[:, None] * TME + piece[None, :]
    valid = (b[:, None] < n_used) & (rp < tot[block_e][:, None])
    cum_e = cum_w.T[block_e]
    len_e = p16.T[block_e]
    start_e = run_start.T[block_e]
    in_win = (cum_e[:, None, :] <= rp[:, :, None]) & (rp[:, :, None] < (cum_e + len_e)[:, None, :])
    row = n_w[None, None, :] * RL + start_e[:, None, :] + rp[:, :, None] - cum_e[:, None, :]
    row = jnp.sum(jnp.where(in_win, row, 0), axis=2)
    zero_src = N_WIN * RL
    assert (1 + EXPERT_BUFS) * TME <= SPARE_WIN * RL
    trash = N_WIN * RL + (1 + b % EXPERT_BUFS)[:, None] * TME + piece[None, :]
    gather_row = (jnp.where(valid, row, zero_src) // PIECE).reshape(-1).astype(I32)
    scatter_row = (jnp.where(valid, row, trash) // PIECE).reshape(-1).astype(I32)
    rw = jnp.concatenate([rw, jnp.zeros((1,), rw.dtype)])
    return rw.astype(I32), block_e.astype(I32), n_used.astype(I32).reshape(1), gather_row, scatter_row


def _expert_kernel(be_ref, nb_ref, grow_ref, srow_ref, xl_hbm, wg_ref, wu_ref, wd_ref, yl_hbm,
                   xbuf, ybuf, wg_b, wu_b, wd_b, gsem, ssem):
    b = pl.program_id(0)
    nb = nb_ref[0]

    @pl.when(jnp.logical_and(b < nb, jnp.logical_or(b == 0, be_ref[b] != be_ref[jnp.maximum(b - 1, 0)])))
    def _():
        wg_b[...] = wg_ref[...].astype(BF16)
        wu_b[...] = wu_ref[...].astype(BF16)
        wd_b[...] = wd_ref[...].astype(BF16)

    def gather_start(blk, slot):
        for p in range(PIECES):
            pltpu.make_async_copy(xl_hbm.at[grow_ref[blk * PIECES + p]], xbuf.at[slot, p], gsem.at[slot]).start()

    def scatter_start(blk, slot):
        for p in range(PIECES):
            pltpu.make_async_copy(ybuf.at[slot, p], yl_hbm.at[srow_ref[blk * PIECES + p]], ssem.at[slot]).start()

    def gather_wait(slot):
        pltpu.make_async_copy(xl_hbm.at[pl.ds(0, PIECES)], xbuf.at[slot], gsem.at[slot]).wait()

    def scatter_wait(slot):
        pltpu.make_async_copy(ybuf.at[slot], yl_hbm.at[pl.ds(0, PIECES)], ssem.at[slot]).wait()

    @pl.when(b < nb)
    def _():
        slot = b % EXPERT_BUFS
        ahead = (b + 2) % EXPERT_BUFS

        @pl.when(b == 0)
        def _():
            gather_start(0, 0)
            gather_start(1, 1)
            ybuf[2] = jnp.zeros(ybuf.shape[1:], BF16)

        gather_wait(slot)

        @pl.when(b >= 2)
        def _():
            scatter_wait(slot)

        x = xbuf[slot].reshape(TME, D_MODEL)
        hb = _silu(_dot(x, wg_b[...])) * _dot(x, wu_b[...])
        gather_start(b + 2, ahead)
        scatter_start(jnp.where(b == 0, N_EBLOCKS + SPARE_BLOCKS - 1, b - 1), ahead)
        ybuf[slot] = _dot(hb.astype(BF16), wd_b[...]).astype(BF16).reshape(PIECES, PIECE, D_MODEL)

        @pl.when(b == nb - 1)
        def _():
            scatter_start(b, slot)
            gather_wait((b + 1) % EXPERT_BUFS)
            gather_wait(ahead)
            for s in range(EXPERT_BUFS):
                scatter_wait(s)


def _experts(block_e, n_used, gather_row, scatter_row, xl, wg, wu, wd, layer):
    def w_map(i, be, nb, gr, sr):
        return (layer, be[jnp.minimum(i, nb[0] - 1)], 0, 0)

    n_pieces = (N_WIN + SPARE_WIN) * RL // PIECE
    yl = pl.pallas_call(
        _expert_kernel,
        out_shape=jax.ShapeDtypeStruct((n_pieces, PIECE, D_MODEL), BF16),
        input_output_aliases={4: 0},
        grid_spec=pltpu.PrefetchScalarGridSpec(
            num_scalar_prefetch=4,
            grid=(N_EBLOCKS,),
            in_specs=[
                pl.BlockSpec(memory_space=pl.ANY),
                pl.BlockSpec((None, None, D_MODEL, D_EXPERT), w_map),
                pl.BlockSpec((None, None, D_MODEL, D_EXPERT), w_map),
                pl.BlockSpec((None, None, D_EXPERT, D_MODEL), w_map),
            ],
            out_specs=pl.BlockSpec(memory_space=pl.ANY),
            scratch_shapes=[
                pltpu.VMEM((EXPERT_BUFS, PIECES, PIECE, D_MODEL), BF16),
                pltpu.VMEM((EXPERT_BUFS, PIECES, PIECE, D_MODEL), BF16),
                pltpu.VMEM((D_MODEL, D_EXPERT), BF16), pltpu.VMEM((D_MODEL, D_EXPERT), BF16),
                pltpu.VMEM((D_EXPERT, D_MODEL), BF16),
                pltpu.SemaphoreType.DMA((EXPERT_BUFS,)), pltpu.SemaphoreType.DMA((EXPERT_BUFS,)),
            ],
        ),
        compiler_params=_cparams(("arbitrary",)),
        name="moe_experts",
    )(block_e, n_used, gather_row, scatter_row, xl.reshape(n_pieces, PIECE, D_MODEL), wg, wu, wd)
    return yl.reshape(n_pieces * PIECE, D_MODEL)


def _combine_kernel(rw_ref, yl_ref, lpt_ref, wt_ref, h_ref, x1_ref, mod_ref, lng_ref, lnb_ref,
                    wsg_ref, wsu_ref, wsd_ref, *rest, split):
    del rw_ref
    if split:
        outp_ref, outs_ref, p_w, lp_b, wt_b = rest
    else:
        out_ref, p_w, lp_b, wt_b = rest
    hb = h_ref[...]
    shared = _dot((_silu(_dot(hb, wsg_ref[...])) * _dot(hb, wsu_ref[...])).astype(BF16), wsd_ref[...])
    lp = lpt_ref[...]
    wt = wt_ref[...]
    for k in range(TOP_K):
        lp_b[k] = jnp.broadcast_to(lp[:, k:k + 1], (WIN, SORT_ROWS)).astype(jnp.int16)
        wt_b[k] = jnp.broadcast_to(wt[:, k:k + 1], (WIN, SORT_ROWS)).astype(BF16)
    ciota = lax.broadcasted_iota(I32, (WIN, SORT_ROWS), 1)
    for j in range(RL // SORT_ROWS):
        col = (ciota + j * SORT_ROWS).astype(jnp.int16)
        pm = jnp.zeros((WIN, SORT_ROWS), BF16)
        for k in range(TOP_K):
            pm = jnp.where(lp_b[k] == col, wt_b[k], pm)
        p_w[:, j * SORT_ROWS:(j + 1) * SORT_ROWS] = pm
    routed = _dot(p_w[...], yl_ref[...])
    z = ALPHA * x1_ref[...] + mod_ref[5:6, :] * (routed + shared)
    out = _layernorm_rows(z, lng_ref[...], lnb_ref[...])
    if split:
        @pl.when(pl.program_id(0) < N_PROMPT // WIN)
        def _():
            outp_ref[...] = out

        @pl.when(pl.program_id(0) >= N_PROMPT // WIN)
        def _():
            outs_ref[...] = out
    else:
        out_ref[...] = out


def _combine(rw, yl, lpos_t, wt, h2, x1, mods, lng, lnb, wsg, wsu, wsd, *, split):
    full = lambda shape: pl.BlockSpec(shape, lambda i, rw: (0,) * len(shape))
    if split:
        out_shape = (jax.ShapeDtypeStruct((N_PROMPT, D_MODEL), F32), jax.ShapeDtypeStruct((N_SAMPLE, D_MODEL), F32))
        out_specs = tuple(_row_pair_specs(WIN, D_MODEL))
    else:
        out_shape = jax.ShapeDtypeStruct((N_TOK, D_MODEL), F32)
        out_specs = pl.BlockSpec((WIN, D_MODEL), lambda i, rw: (i, 0))
    return pl.pallas_call(
        functools.partial(_combine_kernel, split=split),
        out_shape=out_shape,
        grid_spec=pltpu.PrefetchScalarGridSpec(
            num_scalar_prefetch=1,
            grid=(N_WIN,),
            in_specs=[
                pl.BlockSpec((RL, D_MODEL), lambda i, rw: (i, 0)),
                pl.BlockSpec((WIN, TOP_K), lambda i, rw: (i, 0)),
                pl.BlockSpec((WIN, TOP_K), lambda i, rw: (i, 0)),
                pl.BlockSpec((WIN, D_MODEL), lambda i, rw: (i, 0)),
                pl.BlockSpec((WIN, D_MODEL), lambda i, rw: (i, 0)),
                pl.BlockSpec((None, MOD_ROWS, D_MODEL), lambda i, rw: (_cond_block(i, WIN), 0, 0)),
                full((1, D_MODEL)), full((1, D_MODEL)),
                full((D_MODEL, D_EXPERT)), full((D_MODEL, D_EXPERT)), full((D_EXPERT, D_MODEL)),
            ],
            out_specs=out_specs,
            scratch_shapes=[pltpu.VMEM((WIN, RL), BF16),
                            pltpu.VMEM((TOP_K, WIN, SORT_ROWS), jnp.int16),
                            pltpu.VMEM((TOP_K, WIN, SORT_ROWS), BF16)],
        ),
        compiler_params=_cparams(("arbitrary",)),
        name="moe_combine",
    )(rw, yl, lpos_t, wt, h2, x1, mods, lng, lnb, wsg, wsu, wsd)


def _moe_and_norm(x1, h2, lgt, mods, lng, lnb, router_bias, tri, wg, wu, wd, wsg, wsu, wsd, *, layer, split):
    wts, lpos, p16 = _route(lgt, router_bias.reshape(N_EXPERTS, 1), tri)
    rw, block_e, n_used, gather_row, scatter_row = _moe_tables(p16[:, :N_EXPERTS].astype(I32))
    xl = _sort_rows(lpos, h2)
    yl = _experts(block_e, n_used, gather_row, scatter_row, xl, wg, wu, wd, layer)
    return _combine(rw, yl, lpos.T, wts.T, h2, x1, mods, lng, lnb, wsg, wsu, wsd, split=split)


def _rope_tables_mla():
    t = jnp.arange(DEC_SEQ)
    row = (t // GRID_W).astype(F32)
    col = (t % GRID_W).astype(F32)
    n = QK_ROPE // 4
    inv = ROPE_BASE ** (-jnp.arange(n, dtype=F32) / n)
    ang_r = row[:, None] * inv
    ang_c = col[:, None] * inv
    cos = jnp.ones((DEC_SEQ, LANE), F32)
    sin_m = jnp.zeros((DEC_SEQ, LANE), F32)
    sin_p = jnp.zeros((DEC_SEQ, LANE), F32)
    l0 = ROPE_LANE0
    for base, ang in ((l0, ang_r), (l0 + 2 * n, ang_c)):
        c, s = jnp.cos(ang), jnp.sin(ang)
        cos = cos.at[:, base:base + n].set(c).at[:, base + n:base + 2 * n].set(c)
        sin_m = sin_m.at[:, base:base + n].set(-s)
        sin_p = sin_p.at[:, base + n:base + 2 * n].set(s)
    ident = (jnp.ones((TM, LANE), F32), jnp.zeros((TM, LANE), F32), jnp.zeros((TM, LANE), F32))
    return tuple(jnp.concatenate([i, tbl], axis=0) for i, tbl in zip(ident, (cos, sin_m, sin_p)))


def _rope_tables_ret():
    half = RET_DK // 2
    theta = ROPE_BASE ** (-jnp.linspace(0.0, 1.0, half, dtype=F32))
    ang = jnp.arange(DEC_SEQ, dtype=F32)[:, None] * theta
    cos = jnp.concatenate([jnp.ones((TM, half), F32), jnp.cos(ang)], axis=0)
    sin = jnp.concatenate([jnp.zeros((TM, half), F32), jnp.sin(ang)], axis=0)
    return cos, sin


def _pad_heads(w, width, lane0=0):
    k = w.shape[0]
    w = w.reshape(k, MLA_HEADS, width)
    out = jnp.zeros((k, MLA_HEADS, HEAD_PAD), w.dtype).at[:, :, lane0:lane0 + width].set(w)
    return out.reshape(k, MLA_HEADS * HEAD_PAD)


def _rg_gate_weights(wa, ba, wx, bx):
    n_ct = RG_W // LANE
    per = LANE // RG_BW
    tiles_w, tiles_b = [], []
    for c in range(n_ct):
        cols_w, cols_b = [], []
        for d in range(2):
            for w, b in ((wa, ba), (wx, bx)):
                m = jnp.zeros((LANE, LANE), F32)
                for p in range(per):
                    m = m.at[p * RG_BW:(p + 1) * RG_BW, p * RG_BW:(p + 1) * RG_BW].set(w[d, c * per + p])
                cols_w.append(m)
                cols_b.append(b[d, c * LANE:(c + 1) * LANE])
        tiles_w.append(jnp.concatenate(cols_w, axis=1))
        tiles_b.append(jnp.concatenate(cols_b, axis=0)[None, :])
    return jnp.stack(tiles_w).astype(BF16), jnp.stack(tiles_b)


def kernel(x_prompt, x_sample, cache_mla_ckv, cache_mla_krope, state_rglru, state_ret, c, c_ctx, w_ada, b_ada,
           ln_g, ln_b, w_in_ab, rg_conv_w, rg_conv_b, rg_wa, rg_ba, rg_wx, rg_bx, rg_lambda, mla_q_norm, mla_w_uq,
           mla_kv_norm, mla_w_ukv, w_out_ab, w_in_c, ret_gamma_logit, w_out_c, w_router, router_bias,
           w_exp_gate, w_exp_up, w_exp_down, w_sh_gate, w_sh_up, w_sh_down):
    xp = x_prompt.reshape(N_PROMPT, D_MODEL)
    xs = x_sample.reshape(N_SAMPLE, D_MODEL)
    cond = jnp.zeros((16, D_MODEL), F32).at[0].set(c_ctx).at[1:1 + DEC_BATCH].set(c)
    mods_all = _ada_modulation(cond, w_ada, b_ada).reshape(DEPTH, 16, 6, D_MODEL)[:, :N_COND]
    mods_all = jnp.pad(mods_all, ((0, 0), (0, 0), (0, MOD_ROWS - 6), (0, 0)))

    tri = (jnp.arange(WIN)[:, None] < jnp.arange(WIN)[None, :]).astype(BF16)
    wr_t = jnp.swapaxes(w_router, 1, 2)
    wr_hi = wr_t.astype(BF16)
    wr_lo = (wr_t - wr_hi.astype(F32)).astype(BF16)
    wg_e, wu_e, wd_e = w_exp_gate, w_exp_up, w_exp_down
    wsg, wsu, wsd = w_sh_gate.astype(BF16), w_sh_up.astype(BF16), w_sh_down.astype(BF16)

    l, e = 0, 0
    mods = mods_all[l]
    n_main = 2 * RG_W + Q_LORA + KV_LORA
    w_main = w_in_ab[e][:, :n_main].astype(BF16)
    w_kr = jnp.zeros((D_MODEL, LANE), F32).at[:, ROPE_LANE0:ROPE_LANE0 + QK_ROPE].set(w_in_ab[e][:, n_main:]).astype(BF16)
    main, krp = _proj_ab(xp, xs, mods, w_main, w_kr)

    wg_rg, bg_rg = _rg_gate_weights(rg_wa[e], rg_ba[e], rg_wx[e], rg_bx[e])
    h0_p = jnp.zeros((BATCH, 2, RG_W), F32)
    rg_args = (rg_conv_w[e], rg_conv_b[e].reshape(1, RG_W), wg_rg, bg_rg, rg_lambda[e])
    yrg_p, rg_fin = _rglru(main, *rg_args, h0_p, n_seq=BATCH, seq=SEQ, row_block0=0)
    yrg_s, _ = _rglru(main, *rg_args, state_rglru[:, e],
                      n_seq=DEC_BATCH, seq=DEC_SEQ, row_block0=N_PROMPT // DEC_SEQ)

    cos_t, sm_t, sp_t = _rope_tables_mla()
    w_uq = mla_w_uq[e].reshape(Q_LORA, MLA_HEADS, QK_NOPE + QK_ROPE)
    wq_p = _pad_heads(w_uq.reshape(Q_LORA, -1), QK_NOPE + QK_ROPE).astype(BF16)
    w_ukv = mla_w_ukv[e].reshape(KV_LORA, MLA_HEADS, QK_NOPE + V_HEAD)
    wuk_p = _pad_heads(w_ukv[:, :, :QK_NOPE].reshape(KV_LORA, -1), QK_NOPE).astype(BF16)
    wuvt = w_ukv[:, :, QK_NOPE:].reshape(KV_LORA, MLA_HEADS * V_HEAD).T.astype(BF16)
    q_att, k_att, v_att, ckv_n = _mla_prep(main, krp, cos_t, sm_t, sp_t, mla_q_norm[e].reshape(1, Q_LORA), wq_p,
                                           mla_kv_norm[e].reshape(1, KV_LORA), wuk_p, wuvt)
    ctx_ckv = cache_mla_ckv[:, e].reshape(DEC_BATCH * PAST_LEN, KV_LORA)
    ctx_krp = jnp.zeros((DEC_BATCH * PAST_LEN, LANE), F32).at[:, ROPE_LANE0:ROPE_LANE0 + QK_ROPE].set(
        cache_mla_krope[:, e].reshape(DEC_BATCH * PAST_LEN, QK_ROPE))
    kc_att, vc_att = _mla_ctx(ctx_ckv, ctx_krp, wuk_p, wuvt)

    o_att_p = _attention(q_att, k_att, v_att, None, None, n_seq=BATCH, seq=SEQ, row_block0=0, tq=SEQ)
    o_att_s = _attention(q_att, k_att, v_att, kc_att, vc_att,
                         n_seq=DEC_BATCH, seq=DEC_SEQ, row_block0=N_PROMPT // DEC_SEQ, tq=TQ)

    w_out = w_out_ab[e].astype(BF16)
    x1, h2, lgt = _out_ab(yrg_p, yrg_s, o_att_p, o_att_s, w_out[:RG_W], w_out[RG_W:], xp, xs, mods,
                          ln_g[l, 0].reshape(1, D_MODEL), ln_b[l, 0].reshape(1, D_MODEL), wr_hi[l], wr_lo[l])
    x = _moe_and_norm(x1, h2, lgt, mods, ln_g[l, 1].reshape(1, D_MODEL), ln_b[l, 1].reshape(1, D_MODEL),
                      router_bias[l], tri, wg_e, wu_e, wd_e, wsg[l], wsu[l], wsd[l], layer=l, split=False)

    new_ckv = ckv_n.reshape(BATCH, 1, SEQ, KV_LORA)
    new_krope = krp[:N_PROMPT, ROPE_LANE0:ROPE_LANE0 + QK_ROPE].reshape(BATCH, 1, SEQ, QK_ROPE)
    new_rg = rg_fin.reshape(BATCH, 1, 2, RG_W)

    l, o = 1, 0
    mods = mods_all[l]
    qk = RET_HEADS * RET_DK
    w_c = w_in_c[o].astype(BF16)
    cos_r, sin_r = _rope_tables_ret()
    q_r, k_r, v_r, g_r = _proj_c(x, mods, cos_r, sin_r, w_c[:, :qk], w_c[:, qk:2 * qk],
                                 w_c[:, 2 * qk:2 * qk + MIX_C], w_c[:, 2 * qk + MIX_C:])
    gam = jnp.broadcast_to(ret_gamma_logit[o].astype(F32)[:, :, None, None], (2, RET_HEADS, SUBLANE, LANE))
    r0_p = jnp.zeros((BATCH, 2, RET_HEADS, RET_DK, RET_DV), F32)
    o_ret_p, r_fin = _retention(q_r, k_r, v_r, gam, r0_p, n_seq=BATCH, seq=SEQ, row_block0=0, with_state=True)
    (o_ret_s,) = _retention(q_r, k_r, v_r, gam, state_ret[:, o],
                            n_seq=DEC_BATCH, seq=DEC_SEQ, row_block0=N_PROMPT // DEC_SEQ, with_state=False)
    x1, h2, lgt = _out_c(o_ret_p, o_ret_s, g_r, w_out_c[o].astype(BF16), x, mods,
                         ln_g[l, 0].reshape(1, D_MODEL), ln_b[l, 0].reshape(1, D_MODEL), wr_hi[l], wr_lo[l])
    y_p, y_s = _moe_and_norm(x1, h2, lgt, mods, ln_g[l, 1].reshape(1, D_MODEL), ln_b[l, 1].reshape(1, D_MODEL),
                             router_bias[l], tri, wg_e, wu_e, wd_e, wsg[l], wsu[l], wsd[l], layer=l, split=True)

    y_prompt = y_p.reshape(BATCH, SEQ, D_MODEL)
    y_sample = y_s.reshape(DEC_BATCH, DEC_SEQ, D_MODEL)
    new_ret = r_fin.reshape(BATCH, 1, 2, RET_HEADS, RET_DK, RET_DV)
    return (y_prompt, y_sample, new_ckv, new_krope, new_rg, new_ret)
```

```python
import functools
import math

import jax
import jax.numpy as jnp
from jax import lax
from jax.experimental import pallas as pl
from jax.experimental.pallas import tpu as pltpu

F32 = jnp.float32
BF16 = jnp.bfloat16
I32 = jnp.int32

D_MODEL = 1024
BATCH, SEQ = 16, 256
DEC_BATCH, DEC_SEQ = 8, 4096
PAST_LEN = 256
DEPTH = 2
GRID_W = 64
RG_W, RG_BLOCKS = 512, 8
RG_BW = RG_W // RG_BLOCKS
RG_C = 8.0
CONV_W, CONV_LEFT = 4, 2
MLA_HEADS, QK_NOPE, QK_ROPE, V_HEAD = 8, 64, 32, 64
Q_LORA, KV_LORA = 768, 256
ROPE_BASE = 10000.0
ATTN_SCALE = (QK_NOPE + QK_ROPE) ** -0.5
RET_HEADS, RET_DK, RET_DV, RET_CHUNK = 4, 256, 512, 128
MIX_C = RET_HEADS * RET_DV
N_EXPERTS, TOP_K, N_GROUPS, TOPK_GROUPS = 64, 8, 8, 4
GROUP_SIZE = N_EXPERTS // N_GROUPS
D_EXPERT = 256
ROUTED_SCALE = 2.5
ALPHA = (2 * DEPTH) ** 0.25
EPS = 1e-6

N_PROMPT = BATCH * SEQ
N_SAMPLE = DEC_BATCH * DEC_SEQ
N_TOK = N_PROMPT + N_SAMPLE
N_COND = 1 + DEC_BATCH
MOD_ROWS = 8

LANE = 128
SUBLANE = 8
TM = 512
HEAD_PAD = 128
ROPE_LANE0 = QK_NOPE
TQ = 256
KC = 256
AH = 2
RET_UNROLL = 4
V_ONES = 16
Q_PRESCALE = ATTN_SCALE * math.log2(math.e)
SCAN_ROWS = 64
GATE_ROWS = 256
WIN = 256
N_WIN = N_TOK // WIN
PIECE = 16
SORT_ROWS = 256
RL = 3072
TME = 1024
PIECES = TME // PIECE
EXPERT_BUFS = 3
SPARE_BLOCKS = 2
SPARE_WIN = 2
N_PAIRS = N_TOK * TOP_K
N_EBLOCKS = (N_PAIRS + N_WIN * N_EXPERTS * (PIECE - 1)) // TME + N_EXPERTS
NEG = -1e30


def _cparams(sem, vmem_mb=48):
    return pltpu.CompilerParams(dimension_semantics=sem, vmem_limit_bytes=vmem_mb * 1024 * 1024)


def _cond_block(i, tm):
    npb = N_PROMPT // tm
    return jnp.where(i < npb, 0, 1 + (i - npb) // (DEC_SEQ // tm))


def _pos_block(i, tm):
    npb = N_PROMPT // tm
    return jnp.where(i < npb, 0, 1 + (i - npb) % (DEC_SEQ // tm))


def _split_hi_lo(a):
    hi = a.astype(BF16)
    lo = (a - hi.astype(F32)).astype(BF16)
    return hi, lo


def _dot(a, b):
    return jnp.dot(a, b, preferred_element_type=F32)


def _dot_nt(a, b):
    return lax.dot_general(a, b, (((1,), (1,)), ((), ())), preferred_element_type=F32)


def _silu(x):
    return x * jax.nn.sigmoid(x)


def _gelu_tanh(x):
    return 0.5 * x * (1.0 + jnp.tanh(math.sqrt(2.0 / math.pi) * (x + 0.044715 * (x * x * x))))


def _softplus(x):
    return jnp.maximum(x, 0.0) + jnp.log1p(jnp.exp(-jnp.abs(x)))


def _layernorm_rows(z, g, b):
    mu = jnp.mean(z, axis=-1, keepdims=True)
    zc = z - mu
    var = jnp.mean(zc * zc, axis=-1, keepdims=True)
    return (zc * lax.rsqrt(var + EPS)) * g + b


def _ada_kernel(c_ref, w_ref, b_ref, o_ref):
    s_hi, s_lo = _split_hi_lo(_silu(c_ref[...]))
    w_hi, w_lo = _split_hi_lo(w_ref[...])
    o_ref[...] = _dot(s_hi, w_hi) + _dot(s_hi, w_lo) + _dot(s_lo, w_hi) + b_ref[...]


def _ada_modulation(cond, w_ada, b_ada):
    n6 = 6 * D_MODEL
    tn = D_MODEL
    return pl.pallas_call(
        _ada_kernel,
        out_shape=jax.ShapeDtypeStruct((DEPTH, 16, n6), F32),
        grid=(DEPTH, n6 // tn),
        in_specs=[
            pl.BlockSpec((16, D_MODEL), lambda l, j: (0, 0)),
            pl.BlockSpec((None, D_MODEL, tn), lambda l, j: (l, 0, j)),
            pl.BlockSpec((None, 1, tn), lambda l, j: (l, 0, j)),
        ],
        out_specs=pl.BlockSpec((None, 16, tn), lambda l, j: (l, 0, j)),
        compiler_params=_cparams(("arbitrary", "arbitrary")),
        name="ada_modulation",
    )(cond, w_ada, b_ada.reshape(DEPTH, 1, n6))


def _modulated(x, mod_ref, shift_row, scale_row):
    return x * (1.0 + mod_ref[scale_row:scale_row + 1, :]) + mod_ref[shift_row:shift_row + 1, :]


def _row_pair_specs(tm, width, col=0):
    npb = N_PROMPT // tm
    return [pl.BlockSpec((tm, width), lambda i, *_: (jnp.minimum(i, npb - 1), col)),
            pl.BlockSpec((tm, width), lambda i, *_: (jnp.maximum(i - npb, 0), col))]


def _pick_rows(p_ref, s_ref, tm):
    return jnp.where(pl.program_id(0) < N_PROMPT // tm, p_ref[...], s_ref[...])


def _proj_ab_kernel(xp_ref, xs_ref, mod_ref, w_ref, wkr_ref, main_ref, kr_ref):
    h = _modulated(_pick_rows(xp_ref, xs_ref, TM), mod_ref, 0, 1).astype(BF16)
    n = w_ref.shape[1]
    step = 512
    for j in range(n // step):
        main_ref[:, j * step:(j + 1) * step] = _dot(h, w_ref[:, j * step:(j + 1) * step])
    kr_ref[...] = _dot(h, wkr_ref[...])


def _proj_ab(xp, xs, mods, w_main, w_kr):
    n = w_main.shape[1]
    return pl.pallas_call(
        _proj_ab_kernel,
        out_shape=(jax.ShapeDtypeStruct((N_TOK, n), F32), jax.ShapeDtypeStruct((N_TOK, LANE), F32)),
        grid=(N_TOK // TM,),
        in_specs=_row_pair_specs(TM, D_MODEL) + [
            pl.BlockSpec((None, MOD_ROWS, D_MODEL), lambda i: (_cond_block(i, TM), 0, 0)),
            pl.BlockSpec((D_MODEL, n), lambda i: (0, 0)),
            pl.BlockSpec((D_MODEL, LANE), lambda i: (0, 0)),
        ],
        out_specs=(pl.BlockSpec((TM, n), lambda i: (i, 0)), pl.BlockSpec((TM, LANE), lambda i: (i, 0))),
        compiler_params=_cparams(("arbitrary",)),
        name="proj_ab",
    )(xp, xs, mods, w_main, w_kr)


def _rglru_kernel(xr_ref, gr_ref, cw_ref, cb_ref, wg_ref, bg_ref, lam_ref, h0_ref,
                  y_ref, hfin_ref, xpad, a_s, b_s, *, seq):
    pad = SUBLANE
    xpad[0:pad, :] = jnp.zeros((pad, LANE), F32)
    xpad[seq + pad:seq + 2 * pad, :] = jnp.zeros((pad, LANE), F32)
    xpad[pad:seq + pad, :] = xr_ref[...]

    sp = _softplus(-lam_ref[...])
    cw = cw_ref[...]
    cb = cb_ref[...]
    wg = wg_ref[...]
    bg = bg_ref[...]

    def gate_step(c, carry):
        t0 = pl.multiple_of(c * GATE_ROWS, GATE_ROWS)
        win = xpad[pl.ds(t0, GATE_ROWS + 2 * pad), :]
        xc = cb
        for j in range(CONV_W):
            off = pad - CONV_LEFT + j
            xc = xc + win[off:off + GATE_ROWS, :] * cw[j:j + 1, :]
        g = _dot(xc.astype(BF16), wg) + bg
        for d in range(2):
            r = jax.nn.sigmoid(g[:, (2 * d) * LANE:(2 * d + 1) * LANE])
            i = jax.nn.sigmoid(g[:, (2 * d + 1) * LANE:(2 * d + 2) * LANE])
            log_a = (-RG_C * r) * sp[d:d + 1, :]
            a = jnp.exp(log_a)
            t = jnp.tanh(log_a)
            bt = jnp.sqrt(2.0 * t / (t - 1.0)) * (i * xc)
            a_s[d, pl.ds(t0, GATE_ROWS), :] = a
            b_s[d, pl.ds(t0, GATE_ROWS), :] = bt
        return carry

    lax.fori_loop(0, seq // GATE_ROWS, gate_step, 0, unroll=min(2, seq // GATE_ROWS))

    row = lax.broadcasted_iota(I32, (SCAN_ROWS, LANE), 0) % SUBLANE
    n_steps = seq // SCAN_ROWS
    tiles = SCAN_ROWS // SUBLANE

    def local_scan(a, b, reverse):
        for k in (1, 2, 4):
            if reverse:
                ok = row < SUBLANE - k
                shift = SCAN_ROWS - k
            else:
                ok = row >= k
                shift = k
            a_sh = jnp.where(ok, pltpu.roll(a, shift, 0), 1.0)
            b_sh = jnp.where(ok, pltpu.roll(b, shift, 0), 0.0)
            b = a * b_sh + b
            a = a * a_sh
        return a, b

    def fwd_step(c, h):
        t0 = pl.multiple_of(c * SCAN_ROWS, SCAN_ROWS)
        a, b = local_scan(a_s[0, pl.ds(t0, SCAN_ROWS), :], b_s[0, pl.ds(t0, SCAN_ROWS), :], False)
        outs = []
        for j in range(tiles):
            hj = a[j * SUBLANE:(j + 1) * SUBLANE, :] * h + b[j * SUBLANE:(j + 1) * SUBLANE, :]
            outs.append(hj)
            h = hj[SUBLANE - 1:SUBLANE, :]
        y_ref[pl.ds(t0, SCAN_ROWS), :] = jnp.concatenate(outs, axis=0)
        return h

    h_f = lax.fori_loop(0, n_steps, fwd_step, h0_ref[0:1, :], unroll=4)

    def bwd_step(c, h):
        t0 = pl.multiple_of((n_steps - 1 - c) * SCAN_ROWS, SCAN_ROWS)
        a, b = local_scan(a_s[1, pl.ds(t0, SCAN_ROWS), :], b_s[1, pl.ds(t0, SCAN_ROWS), :], True)
        outs = [None] * tiles
        for j in reversed(range(tiles)):
            hj = a[j * SUBLANE:(j + 1) * SUBLANE, :] * h + b[j * SUBLANE:(j + 1) * SUBLANE, :]
            outs[j] = hj
            h = hj[0:1, :]
        hb = jnp.concatenate(outs, axis=0)
        y_ref[pl.ds(t0, SCAN_ROWS), :] = (y_ref[pl.ds(t0, SCAN_ROWS), :] + hb) * _gelu_tanh(gr_ref[pl.ds(t0, SCAN_ROWS), :])
        return h

    h_b = lax.fori_loop(0, n_steps, bwd_step, h0_ref[1:2, :], unroll=4)
    hfin_ref[0:1, :] = h_f
    hfin_ref[1:2, :] = h_b


def _rglru(main, cw, cb, wg, bg, lam, h0, *, n_seq, seq, row_block0):
    n_ct = RG_W // LANE
    gr_col0 = RG_W // LANE
    kern = functools.partial(_rglru_kernel, seq=seq)
    return pl.pallas_call(
        kern,
        out_shape=(jax.ShapeDtypeStruct((n_seq * seq, RG_W), F32), jax.ShapeDtypeStruct((n_seq, 2, RG_W), F32)),
        grid=(n_seq, n_ct),
        in_specs=[
            pl.BlockSpec((seq, LANE), lambda b, c: (row_block0 + b, c)),
            pl.BlockSpec((seq, LANE), lambda b, c: (row_block0 + b, gr_col0 + c)),
            pl.BlockSpec((CONV_W, LANE), lambda b, c: (0, c)),
            pl.BlockSpec((1, LANE), lambda b, c: (0, c)),
            pl.BlockSpec((None, LANE, 4 * LANE), lambda b, c: (c, 0, 0)),
            pl.BlockSpec((None, 1, 4 * LANE), lambda b, c: (c, 0, 0)),
            pl.BlockSpec((2, LANE), lambda b, c: (0, c)),
            pl.BlockSpec((None, 2, LANE), lambda b, c: (b, 0, c)),
        ],
        out_specs=(
            pl.BlockSpec((seq, LANE), lambda b, c: (b, c)),
            pl.BlockSpec((None, 2, LANE), lambda b, c: (b, 0, c)),
        ),
        scratch_shapes=[
            pltpu.VMEM((seq + 2 * SUBLANE, LANE), F32),
            pltpu.VMEM((2, seq, LANE), F32),
            pltpu.VMEM((2, seq, LANE), F32),
        ],
        compiler_params=_cparams(("arbitrary", "arbitrary")),
        name=f"rglru_s{seq}",
    )(main, main, cw, cb, wg, bg, lam, h0)


def _rope_lanes(x, cos, sin_m, sin_p):
    n = x.shape[1] // LANE
    half = QK_ROPE // 4
    cos_t = jnp.concatenate([cos] * n, axis=1) if n > 1 else cos
    sm_t = jnp.concatenate([sin_m] * n, axis=1) if n > 1 else sin_m
    sp_t = jnp.concatenate([sin_p] * n, axis=1) if n > 1 else sin_p
    up = pltpu.roll(x, x.shape[1] - half, 1)
    dn = pltpu.roll(x, half, 1)
    return x * cos_t + up * sm_t + dn * sp_t


def _mla_prep_kernel(cq0_ref, cq1_ref, cq2_ref, ckv_ref, kr_ref, cos_ref, sm_ref, sp_ref,
                     qn_ref, wq_ref, kvn_ref, wuk_ref, wuvt_ref,
                     q_ref, k_ref, vt_ref, ckvn_ref):
    cq = [cq0_ref[...], cq1_ref[...], cq2_ref[...]]
    ms = (jnp.sum(cq[0] * cq[0], axis=-1, keepdims=True) + jnp.sum(cq[1] * cq[1], axis=-1, keepdims=True)
          + jnp.sum(cq[2] * cq[2], axis=-1, keepdims=True)) * (1.0 / Q_LORA)
    inv = lax.rsqrt(ms + EPS)
    blk = Q_LORA // 3
    q = None
    for j in range(3):
        cqn = ((cq[j] * inv) * qn_ref[:, j * blk:(j + 1) * blk]).astype(BF16)
        part = _dot(cqn, wq_ref[j * blk:(j + 1) * blk, :])
        q = part if q is None else q + part
    cos, sm, sp = cos_ref[...], sm_ref[...], sp_ref[...]
    q_ref[...] = (_rope_lanes(q, cos, sm, sp) * Q_PRESCALE).T.astype(BF16)

    ckv = ckv_ref[...]
    inv_kv = lax.rsqrt(jnp.mean(ckv * ckv, axis=-1, keepdims=True) + EPS)
    ckvn = (ckv * inv_kv) * kvn_ref[...]

    @pl.when(pl.program_id(0) < N_PROMPT // TM)
    def _():
        ckvn_ref[...] = ckvn

    ckvn_b = ckvn.astype(BF16)
    kr_rot = _rope_lanes(kr_ref[...], cos, sm, sp)
    k_ref[...] = (_dot(ckvn_b, wuk_ref[...]) + jnp.concatenate([kr_rot] * MLA_HEADS, axis=1)).astype(BF16)
    vt = _dot_nt(wuvt_ref[...], ckvn_b).astype(BF16)
    for c in range(vt_ref.shape[0]):
        vt_ref[c] = vt[:, c * KC:(c + 1) * KC]


def _mla_prep(main, krp, cos_t, sm_t, sp_t, q_norm, wq_p, kv_norm, wuk_p, wuvt):
    cq_col0 = 2 * RG_W // 256
    hp = MLA_HEADS * HEAD_PAD
    full = lambda shape: pl.BlockSpec(shape, lambda i: (0,) * len(shape))
    tab = pl.BlockSpec((TM, LANE), lambda i: (_pos_block(i, TM), 0))
    return pl.pallas_call(
        _mla_prep_kernel,
        out_shape=(
            jax.ShapeDtypeStruct((hp, N_TOK), BF16),
            jax.ShapeDtypeStruct((N_TOK, hp), BF16),
            jax.ShapeDtypeStruct((N_TOK // KC, MLA_HEADS * V_HEAD, KC), BF16),
            jax.ShapeDtypeStruct((N_PROMPT, KV_LORA), F32),
        ),
        grid=(N_TOK // TM,),
        in_specs=[
            pl.BlockSpec((TM, 256), lambda i: (i, cq_col0)),
            pl.BlockSpec((TM, 256), lambda i: (i, cq_col0 + 1)),
            pl.BlockSpec((TM, 256), lambda i: (i, cq_col0 + 2)),
            pl.BlockSpec((TM, 256), lambda i: (i, cq_col0 + 3)),
            pl.BlockSpec((TM, LANE), lambda i: (i, 0)),
            tab, tab, tab,
            full((1, Q_LORA)), full((Q_LORA, hp)), full((1, KV_LORA)), full((KV_LORA, hp)),
            full((MLA_HEADS * V_HEAD, KV_LORA)),
        ],
        out_specs=(
            pl.BlockSpec((hp, TM), lambda i: (0, i)),
            pl.BlockSpec((TM, hp), lambda i: (i, 0)),
            pl.BlockSpec((TM // KC, MLA_HEADS * V_HEAD, KC), lambda i: (i, 0, 0)),
            pl.BlockSpec((TM, KV_LORA), lambda i: (jnp.minimum(i, N_PROMPT // TM - 1), 0)),
        ),
        compiler_params=_cparams(("arbitrary",)),
        name="mla_prep",
    )(main, main, main, main, krp, cos_t, sm_t, sp_t, q_norm, wq_p, kv_norm, wuk_p, wuvt)


def _mla_ctx_kernel(ckv_ref, kr_ref, wuk_ref, wuvt_ref, k_ref, vt_ref):
    ckv_b = ckv_ref[...].astype(BF16)
    k_ref[...] = (_dot(ckv_b, wuk_ref[...]) + jnp.concatenate([kr_ref[...]] * MLA_HEADS, axis=1)).astype(BF16)
    vt_ref[...] = _dot_nt(wuvt_ref[...], ckv_b).astype(BF16)


def _mla_ctx(ctx_ckv, ctx_krp, wuk_p, wuvt):
    n = ctx_ckv.shape[0]
    hp = MLA_HEADS * HEAD_PAD
    tm = KC
    full = lambda shape: pl.BlockSpec(shape, lambda i: (0,) * len(shape))
    return pl.pallas_call(
        _mla_ctx_kernel,
        out_shape=(jax.ShapeDtypeStruct((n, hp), BF16),
                   jax.ShapeDtypeStruct((n // tm, MLA_HEADS * V_HEAD, tm), BF16)),
        grid=(n // tm,),
        in_specs=[
            pl.BlockSpec((tm, KV_LORA), lambda i: (i, 0)),
            pl.BlockSpec((tm, LANE), lambda i: (i, 0)),
            full((KV_LORA, hp)), full((MLA_HEADS * V_HEAD, KV_LORA)),
        ],
        out_specs=(pl.BlockSpec((tm, hp), lambda i: (i, 0)),
                   pl.BlockSpec((None, MLA_HEADS * V_HEAD, tm), lambda i: (i, 0, 0))),
        compiler_params=_cparams(("arbitrary",)),
        name="mla_ctx",
    )(ctx_ckv, ctx_krp, wuk_p, wuvt)


def _attn_kernel(*refs, seq, tq, n_ctx):
    if n_ctx:
        q_ref, k_ref, vt_ref, kc_ref, vtc_ref, o_ref, s_scr, p_scr, k_all, vt_all = refs
    else:
        q_ref, k_ref, vt_ref, o_ref, s_scr, p_scr, k_all, vt_all = refs
    has_ctx = 1 if n_ctx else 0
    n_own = seq // KC
    n = n_own + has_ctx
    qs = [q_ref[h * HEAD_PAD:(h + 1) * HEAD_PAD, :] for h in range(AH)]

    @pl.when(pl.program_id(2) == 0)
    def _():
        k_all[0:seq, :] = k_ref[...]
        if has_ctx:
            k_all[seq:seq + KC, :] = kc_ref[...]
        for h in range(AH):
            vt_all[0:n_own, h, 0:V_HEAD, :] = vt_ref[:, h * V_HEAD:(h + 1) * V_HEAD, :]
            if has_ctx:
                vt_all[n_own, h, 0:V_HEAD, :] = vtc_ref[h * V_HEAD:(h + 1) * V_HEAD, :]
            vt_all[:, h, V_HEAD:V_HEAD + V_ONES, :] = jnp.ones((n, V_ONES, KC), BF16)

    def scores(c, slot):
        for h in range(AH):
            s_scr[slot, h] = _dot(k_all[c * KC:(c + 1) * KC, h * HEAD_PAD:(h + 1) * HEAD_PAD], qs[h])

    def softmax_chunk(slot, st):
        out = []
        for h in range(AH):
            m, a1, _, acc = st[h]
            t = s_scr[slot, h]
            m_new = jnp.maximum(m, jnp.max(t, axis=0, keepdims=True))
            p_scr[slot, h] = jnp.exp2(t - m_new).astype(BF16)
            out.append((m_new, jnp.exp2(m - m_new), a1, acc))
        return out

    def weighted_values(c, slot, st, alphas):
        return [st[h][:3] + (alphas[h] * st[h][3] + _dot(vt_all[c, h], p_scr[slot, h]),) for h in range(AH)]

    one = jnp.ones((1, tq), F32)
    st = [(jnp.full((1, tq), NEG, F32), one, one, jnp.zeros((V_HEAD + V_ONES, tq), F32)) for _ in range(AH)]
    scores(0, 0)
    if n > 1:
        scores(1, 1)
    for c in range(n):
        if c + 2 < n:
            scores(c + 2, (c + 2) % 3)
        alphas = [st[h][2] for h in range(AH)]
        st = softmax_chunk(c % 3, st)
        if c >= 2:
            st = weighted_values(c - 2, (c - 2) % 3, st, alphas)
    if n > 1:
        st = weighted_values(n - 2, (n - 2) % 3, st, [st[h][2] for h in range(AH)])
    st = weighted_values(n - 1, (n - 1) % 3, st, [st[h][1] for h in range(AH)])
    for h in range(AH):
        acc = st[h][3]
        o_ref[:, h * V_HEAD:(h + 1) * V_HEAD] = (acc[0:V_HEAD] / acc[V_HEAD:V_HEAD + 1]).T.astype(o_ref.dtype)


def _attention(q, k, vt, kc, vtc, *, n_seq, seq, row_block0, tq):
    n_ctx = 0 if kc is None else PAST_LEN
    n_hp = MLA_HEADS // AH
    nq = seq // tq
    kern = functools.partial(_attn_kernel, seq=seq, tq=tq, n_ctx=n_ctx)
    in_specs = [
        pl.BlockSpec((AH * HEAD_PAD, tq), lambda b, j, i: (j, (row_block0 + b) * nq + i)),
        pl.BlockSpec((seq, AH * HEAD_PAD), lambda b, j, i: (row_block0 + b, j)),
        pl.BlockSpec((seq // KC, AH * V_HEAD, KC), lambda b, j, i: (row_block0 + b, j, 0)),
    ]
    args = [q, k, vt]
    if n_ctx:
        in_specs += [
            pl.BlockSpec((n_ctx, AH * HEAD_PAD), lambda b, j, i: (b, j)),
            pl.BlockSpec((None, AH * V_HEAD, KC), lambda b, j, i: (b, j, 0)),
        ]
        args += [kc, vtc]
    return pl.pallas_call(
        kern,
        out_shape=jax.ShapeDtypeStruct((n_seq * seq, MLA_HEADS * V_HEAD), BF16),
        grid=(n_seq, n_hp, nq),
        in_specs=in_specs,
        out_specs=pl.BlockSpec((tq, AH * V_HEAD), lambda b, j, i: (b * nq + i, j)),
        scratch_shapes=[pltpu.VMEM((3, AH, KC, tq), F32), pltpu.VMEM((3, AH, KC, tq), BF16),
                        pltpu.VMEM((seq + n_ctx, AH * HEAD_PAD), BF16),
                        pltpu.VMEM(((seq + n_ctx) // KC, AH, V_HEAD + V_ONES, KC), BF16)],
        compiler_params=_cparams(("arbitrary", "arbitrary", "arbitrary")),
        name=f"mla_attention_s{seq}",
    )(*args)


def _post_mixer(y, x, mod_ref, lng_ref, lnb_ref, wrh_ref, wrl_ref, x1_ref, h2_ref, lgt_ref):
    z = ALPHA * x + mod_ref[2:3, :] * y
    x1 = _layernorm_rows(z, lng_ref[...], lnb_ref[...])
    x1_ref[...] = x1
    h2 = _modulated(x1, mod_ref, 3, 4)
    h_hi, h_lo = _split_hi_lo(h2)
    h2_ref[...] = h_hi
    w_hi, w_lo = wrh_ref[...], wrl_ref[...]
    lgt_ref[...] = _dot_nt(w_hi, h_hi) + _dot_nt(w_hi, h_lo) + _dot_nt(w_lo, h_hi)


def _out_ab_kernel(yrgp_ref, yrgs_ref, op_ref, os_ref, wa_ref, wb_ref, xp_ref, xs_ref,
                   mod_ref, lng_ref, lnb_ref, wrh_ref, wrl_ref, x1_ref, h2_ref, lgt_ref):
    y = (_dot(_pick_rows(yrgp_ref, yrgs_ref, TM).astype(BF16), wa_ref[...])
         + _dot(_pick_rows(op_ref, os_ref, TM), wb_ref[...]))
    _post_mixer(y, _pick_rows(xp_ref, xs_ref, TM), mod_ref, lng_ref, lnb_ref, wrh_ref, wrl_ref,
                x1_ref, h2_ref, lgt_ref)


def _post_specs():
    full = lambda shape: pl.BlockSpec(shape, lambda i: (0,) * len(shape))
    in_specs = [
        pl.BlockSpec((None, MOD_ROWS, D_MODEL), lambda i: (_cond_block(i, TM), 0, 0)),
        full((1, D_MODEL)), full((1, D_MODEL)),
        full((N_EXPERTS, D_MODEL)), full((N_EXPERTS, D_MODEL)),
    ]
    out_shape = (
        jax.ShapeDtypeStruct((N_TOK, D_MODEL), F32),
        jax.ShapeDtypeStruct((N_TOK, D_MODEL), BF16),
        jax.ShapeDtypeStruct((N_EXPERTS, N_TOK), F32),
    )
    out_specs = (
        pl.BlockSpec((TM, D_MODEL), lambda i: (i, 0)),
        pl.BlockSpec((TM, D_MODEL), lambda i: (i, 0)),
        pl.BlockSpec((N_EXPERTS, TM), lambda i: (0, i)),
    )
    return in_specs, out_shape, out_specs


def _out_ab(yrg_p, yrg_s, o_p, o_s, w_a, w_b, xp, xs, mods, lng, lnb, wr_hi, wr_lo):
    full = lambda shape: pl.BlockSpec(shape, lambda i: (0,) * len(shape))
    post_in, out_shape, out_specs = _post_specs()
    return pl.pallas_call(
        _out_ab_kernel,
        out_shape=out_shape,
        grid=(N_TOK // TM,),
        in_specs=(_row_pair_specs(TM, RG_W) + _row_pair_specs(TM, MLA_HEADS * V_HEAD)
                  + [full((RG_W, D_MODEL)), full((MLA_HEADS * V_HEAD, D_MODEL))]
                  + _row_pair_specs(TM, D_MODEL) + post_in),
        out_specs=out_specs,
        compiler_params=_cparams(("arbitrary",)),
        name="out_ab",
    )(yrg_p, yrg_s, o_p, o_s, w_a, w_b, xp, xs, mods, lng, lnb, wr_hi, wr_lo)


def _proj_c_kernel(x_ref, mod_ref, cos_ref, sin_ref, wq_ref, wk_ref, wv_ref, wg_ref, q_ref, k_ref, v_ref, g_ref):
    h = _modulated(x_ref[...], mod_ref, 0, 1).astype(BF16)
    cos, sin = cos_ref[...], sin_ref[...]
    half = RET_DK // 2
    for hd in range(RET_HEADS):
        for w_ref, is_k in ((wq_ref, False), (wk_ref, True)):
            p = _dot(h, w_ref[:, hd * RET_DK:(hd + 1) * RET_DK])
            x1, x2 = p[:, :half], p[:, half:]
            r1 = x1 * cos - x2 * sin
            r2 = x2 * cos + x1 * sin
            if not is_k:
                q_ref[:, hd * RET_DK:hd * RET_DK + half] = r1.astype(BF16)
                q_ref[:, hd * RET_DK + half:(hd + 1) * RET_DK] = r2.astype(BF16)
            else:
                t1 = (r1 * RET_DK ** -0.5).T.astype(BF16)
                t2 = (r2 * RET_DK ** -0.5).T.astype(BF16)
                for c in range(k_ref.shape[0]):
                    k_ref[c, hd * RET_DK:hd * RET_DK + half, :] = t1[:, c * RET_CHUNK:(c + 1) * RET_CHUNK]
                    k_ref[c, hd * RET_DK + half:(hd + 1) * RET_DK, :] = t2[:, c * RET_CHUNK:(c + 1) * RET_CHUNK]
    step = 512
    for j in range(MIX_C // step):
        v_ref[:, j * step:(j + 1) * step] = _dot(h, wv_ref[:, j * step:(j + 1) * step]).astype(BF16)
        g_ref[:, j * step:(j + 1) * step] = _dot(h, wg_ref[:, j * step:(j + 1) * step])


def _proj_c(x, mods, cos_t, sin_t, wq, wk, wv, wg):
    full = lambda shape: pl.BlockSpec(shape, lambda i: (0,) * len(shape))
    qk = RET_HEADS * RET_DK
    tab = pl.BlockSpec((TM, RET_DK // 2), lambda i: (_pos_block(i, TM), 0))
    return pl.pallas_call(
        _proj_c_kernel,
        out_shape=(
            jax.ShapeDtypeStruct((N_TOK, qk), BF16), jax.ShapeDtypeStruct((N_TOK // RET_CHUNK, qk, RET_CHUNK), BF16),
            jax.ShapeDtypeStruct((N_TOK, MIX_C), BF16), jax.ShapeDtypeStruct((N_TOK, MIX_C), F32),
        ),
        grid=(N_TOK // TM,),
        in_specs=[
            pl.BlockSpec((TM, D_MODEL), lambda i: (i, 0)),
            pl.BlockSpec((None, MOD_ROWS, D_MODEL), lambda i: (_cond_block(i, TM), 0, 0)),
            tab, tab,
            full((D_MODEL, qk)), full((D_MODEL, qk)), full((D_MODEL, MIX_C)), full((D_MODEL, MIX_C)),
        ],
        out_specs=(
            pl.BlockSpec((TM, qk), lambda i: (i, 0)),
            pl.BlockSpec((TM // RET_CHUNK, qk, RET_CHUNK), lambda i: (i, 0, 0)),
            pl.BlockSpec((TM, MIX_C), lambda i: (i, 0)), pl.BlockSpec((TM, MIX_C), lambda i: (i, 0)),
        ),
        compiler_params=_cparams(("arbitrary",), 56),
        name="proj_c",
    )(x, mods, cos_t, sin_t, wq, wk, wv, wg)


def _retention_kernel(*refs, seq, with_state):
    if with_state:
        q_ref, kt_ref, v_ref, gam_ref, r0_ref, o_ref, rfin_ref, r_s = refs
    else:
        q_ref, kt_ref, v_ref, gam_ref, r0_ref, o_ref, r_s = refs
        rfin_ref = None
    c = RET_CHUNK
    n = seq // c
    ii = lax.broadcasted_iota(I32, (c, c), 0).astype(F32)
    jj = lax.broadcasted_iota(I32, (c, c), 1).astype(F32)
    ci = lax.broadcasted_iota(I32, (c, 1), 0).astype(F32)
    li = lax.broadcasted_iota(I32, (1, c), 1).astype(F32)

    consts = []
    for d in range(2):
        gam = gam_ref[d]
        lg_row = -_softplus(-gam[0:1, :])
        lg = jnp.broadcast_to(lg_row, (c, c))
        lg_col = jnp.broadcast_to(lg_row[:, 0:1], (c, 1))
        if d == 0:
            diff = ii - jj
            xi = jnp.exp((ci + 1.0) * lg_col)
            zeta = jnp.exp((c - 1.0 - li) * lg_row)
        else:
            diff = jj - ii
            xi = jnp.exp((c - ci) * lg_col)
            zeta = jnp.exp(li * lg_row)
        dmat = jnp.where(diff >= 0, jnp.exp(jnp.maximum(diff, 0.0) * lg), 0.0)
        g_chunk = jnp.exp(float(c) * lg_row[:, 0:1])
        consts.append((dmat, xi, zeta, g_chunk))
        r_s[d] = r0_ref[d]

    def chunk(d, idx, accumulate):
        dmat, xi, zeta, g_chunk = consts[d]
        t0 = pl.multiple_of(idx * c, c)
        qb = q_ref[pl.ds(t0, c), :]
        kt = kt_ref[idx]
        vb = v_ref[pl.ds(t0, c), :]
        r = r_s[d]
        inner = _dot(qb, kt) * dmat
        o = _dot(inner.astype(BF16), vb) + _dot((qb.astype(F32) * xi).astype(BF16), r.astype(BF16))
        r_s[d] = r * g_chunk + _dot((kt.astype(F32) * zeta).astype(BF16), vb)
        if accumulate:
            o_ref[pl.ds(t0, c), :] = o_ref[pl.ds(t0, c), :] + o
        else:
            o_ref[pl.ds(t0, c), :] = o

    def first_half(s, carry):
        chunk(0, s, False)
        chunk(1, n - 1 - s, False)
        return carry

    def second_half(s, carry):
        chunk(0, s, True)
        chunk(1, n - 1 - s, True)
        return carry

    unroll = min(RET_UNROLL, n // 2)
    lax.fori_loop(0, n // 2, first_half, 0, unroll=unroll)
    lax.fori_loop(n // 2, n, second_half, 0, unroll=unroll)
    if with_state:
        for d in range(2):
            rfin_ref[d] = r_s[d]


def _retention(q, k, v, gam, r0, *, n_seq, seq, row_block0, with_state):
    kern = functools.partial(_retention_kernel, seq=seq, with_state=with_state)
    out_shape = [jax.ShapeDtypeStruct((n_seq * seq, MIX_C), F32)]
    out_specs = [pl.BlockSpec((seq, RET_DV), lambda b, h: (b, h))]
    if with_state:
        out_shape.append(jax.ShapeDtypeStruct((n_seq, 2, RET_HEADS, RET_DK, RET_DV), F32))
        out_specs.append(pl.BlockSpec((None, 2, None, RET_DK, RET_DV), lambda b, h: (b, 0, h, 0, 0)))
    return pl.pallas_call(
        kern,
        out_shape=tuple(out_shape),
        grid=(n_seq, RET_HEADS),
        in_specs=[
            pl.BlockSpec((seq, RET_DK), lambda b, h: (row_block0 + b, h)),
            pl.BlockSpec((seq // RET_CHUNK, RET_DK, RET_CHUNK), lambda b, h: (row_block0 + b, h, 0)),
            pl.BlockSpec((seq, RET_DV), lambda b, h: (row_block0 + b, h)),
            pl.BlockSpec((2, None, SUBLANE, LANE), lambda b, h: (0, h, 0, 0)),
            pl.BlockSpec((None, 2, None, RET_DK, RET_DV), lambda b, h: (b, 0, h, 0, 0)),
        ],
        out_specs=tuple(out_specs),
        scratch_shapes=[pltpu.VMEM((2, RET_DK, RET_DV), F32)],
        compiler_params=_cparams(("arbitrary", "arbitrary"), 56),
        name=f"retention_s{seq}",
    )(q, k, v, gam, r0)


def _out_c_kernel(op_ref, os_ref, g_ref, w_ref, x_ref, mod_ref, lng_ref, lnb_ref, wrh_ref, wrl_ref,
                  x1_ref, h2_ref, lgt_ref):
    y = None
    o_all = _pick_rows(op_ref, os_ref, TM)
    for hd in range(RET_HEADS):
        o = o_all[:, hd * RET_DV:(hd + 1) * RET_DV]
        mu = jnp.mean(o, axis=-1, keepdims=True)
        oc = o - mu
        var = jnp.mean(oc * oc, axis=-1, keepdims=True)
        on = oc * lax.rsqrt(var + EPS)
        a = (on * _silu(g_ref[:, hd * RET_DV:(hd + 1) * RET_DV])).astype(BF16)
        part = _dot(a, w_ref[hd * RET_DV:(hd + 1) * RET_DV, :])
        y = part if y is None else y + part
    _post_mixer(y, x_ref[...], mod_ref, lng_ref, lnb_ref, wrh_ref, wrl_ref, x1_ref, h2_ref, lgt_ref)


def _out_c(o_p, o_s, g, w, x, mods, lng, lnb, wr_hi, wr_lo):
    full = lambda shape: pl.BlockSpec(shape, lambda i: (0,) * len(shape))
    post_in, out_shape, out_specs = _post_specs()
    return pl.pallas_call(
        _out_c_kernel,
        out_shape=out_shape,
        grid=(N_TOK // TM,),
        in_specs=_row_pair_specs(TM, MIX_C) + [
            pl.BlockSpec((TM, MIX_C), lambda i: (i, 0)),
            full((MIX_C, D_MODEL)),
            pl.BlockSpec((TM, D_MODEL), lambda i: (i, 0)),
        ] + post_in,
        out_specs=out_specs,
        compiler_params=_cparams(("arbitrary",), 56),
        name="out_c",
    )(o_p, o_s, g, w, x, mods, lng, lnb, wr_hi, wr_lo)


def _route_kernel(lgt_ref, bias_ref, tri_ref, w_ref, lpos_ref, p16_ref):
    tt = lgt_ref.shape[1]
    scores = jax.nn.sigmoid(lgt_ref[...])
    sel = scores + bias_ref[...]
    srow = lax.broadcasted_iota(I32, (GROUP_SIZE, tt), 0).astype(F32)
    ninf = -jnp.inf

    gs = []
    for g in range(N_GROUPS):
        sg = sel[g * GROUP_SIZE:(g + 1) * GROUP_SIZE, :]
        m1 = jnp.max(sg, axis=0, keepdims=True)
        i1 = jnp.min(jnp.where(sg == m1, srow, float(GROUP_SIZE)), axis=0, keepdims=True)
        m2 = jnp.max(jnp.where(srow == i1, ninf, sg), axis=0, keepdims=True)
        gs.append(m1 + m2)
    gs = jnp.concatenate(gs, axis=0)
    chosen = jnp.zeros((N_GROUPS, tt), F32)
    for _ in range(TOPK_GROUPS):
        mg = jnp.max(gs, axis=0, keepdims=True)
        gi = jnp.min(jnp.where(gs == mg, srow, float(N_GROUPS)), axis=0, keepdims=True)
        hit = srow == gi
        chosen = jnp.where(hit, 1.0, chosen)
        gs = jnp.where(hit, ninf, gs)
    sel = jnp.concatenate(
        [jnp.where(jnp.broadcast_to(chosen[g:g + 1, :], (GROUP_SIZE, tt)) > 0.5,
                   sel[g * GROUP_SIZE:(g + 1) * GROUP_SIZE, :], ninf) for g in range(N_GROUPS)], axis=0)

    erow = lax.broadcasted_iota(I32, (N_EXPERTS, tt), 0).astype(F32)
    ids, ws = [], []
    for _ in range(TOP_K):
        m = jnp.max(sel, axis=0, keepdims=True)
        ei = jnp.min(jnp.where(sel == m, erow, float(N_EXPERTS)), axis=0, keepdims=True)
        hit = erow == ei
        ids.append(ei)
        ws.append(jnp.sum(jnp.where(hit, scores, 0.0), axis=0, keepdims=True))
        sel = jnp.where(hit, ninf, sel)
    wsum = ws[0]
    for k in range(1, TOP_K):
        wsum = wsum + ws[k]
    w_ref[...] = jnp.concatenate([w / wsum * ROUTED_SCALE for w in ws], axis=0)

    member_f = jnp.zeros((N_EXPERTS, tt), F32)
    for k in range(TOP_K):
        member_f = jnp.where(erow == ids[k], 1.0, member_f)
    member_b = member_f.astype(BF16)
    cnt_row = _dot_nt(jnp.ones((SUBLANE, tt), BF16), member_b)[0:1, :]
    p16_row = jnp.ceil(cnt_row * (1.0 / PIECE)) * PIECE
    lane_e = lax.broadcasted_iota(I32, (N_EXPERTS, N_EXPERTS), 1)
    sub_e = lax.broadcasted_iota(I32, (N_EXPERTS, N_EXPERTS), 0)
    run_start = jnp.sum(jnp.where(lane_e < sub_e, jnp.broadcast_to(p16_row, (N_EXPERTS, N_EXPERTS)), 0.0),
                        axis=1, keepdims=True)
    rank = _dot(member_b, tri_ref[...]) + run_start
    lpos_ref[...] = jnp.concatenate(
        [jnp.sum(jnp.where(erow == ids[k], rank, 0.0), axis=0, keepdims=True) for k in range(TOP_K)],
        axis=0).astype(I32)
    p16_ref[pl.ds(pl.program_id(0), 1), :] = jnp.concatenate(
        [p16_row, jnp.zeros((1, LANE - N_EXPERTS), F32)], axis=1)


def _route(lgt, bias, tri):
    return pl.pallas_call(
        _route_kernel,
        out_shape=(
            jax.ShapeDtypeStruct((TOP_K, N_TOK), F32), jax.ShapeDtypeStruct((TOP_K, N_TOK), I32),
            jax.ShapeDtypeStruct((N_WIN, LANE), F32),
        ),
        grid=(N_WIN,),
        in_specs=[
            pl.BlockSpec((N_EXPERTS, WIN), lambda i: (0, i)),
            pl.BlockSpec((N_EXPERTS, 1), lambda i: (0, 0)),
            pl.BlockSpec((WIN, WIN), lambda i: (0, 0)),
        ],
        out_specs=(
            pl.BlockSpec((TOP_K, WIN), lambda i: (0, i)), pl.BlockSpec((TOP_K, WIN), lambda i: (0, i)),
            pl.BlockSpec((N_WIN, LANE), lambda i: (0, 0)),
        ),
        compiler_params=_cparams(("arbitrary",)),
        name="moe_route",
    )(lgt, bias, tri)


def _sort_kernel(lpos_ref, h_ref, xl_ref, oh_all):
    lp = jnp.where(pl.program_id(0) < N_WIN, lpos_ref[...], -1)
    riota = lax.broadcasted_iota(I32, (SORT_ROWS, WIN), 0).astype(jnp.int16)
    one = jnp.ones((SORT_ROWS, WIN), BF16)
    for j in range(RL // SORT_ROWS):
        rel = (lp - j * SORT_ROWS).astype(jnp.int16)
        oh = jnp.zeros((SORT_ROWS, WIN), BF16)
        for k in range(TOP_K):
            oh = jnp.where(rel[k:k + 1, :] == riota, one, oh)
        oh_all[j * SORT_ROWS:(j + 1) * SORT_ROWS, :] = oh
    x = h_ref[...]
    for nt in range(D_MODEL // SORT_ROWS):
        cols = slice(nt * SORT_ROWS, (nt + 1) * SORT_ROWS)
        xl_ref[:, cols] = _dot(oh_all[...], x[:, cols]).astype(BF16)


def _sort_rows(lpos, h2):
    last = N_WIN - 1
    return pl.pallas_call(
        _sort_kernel,
        out_shape=jax.ShapeDtypeStruct(((N_WIN + SPARE_WIN) * RL, D_MODEL), BF16),
        grid=(N_WIN + SPARE_WIN,),
        in_specs=[
            pl.BlockSpec((TOP_K, WIN), lambda i: (0, jnp.minimum(i, last))),
            pl.BlockSpec((WIN, D_MODEL), lambda i: (jnp.minimum(i, last), 0)),
        ],
        out_specs=pl.BlockSpec((RL, D_MODEL), lambda i: (i, 0)),
        scratch_shapes=[pltpu.VMEM((RL, WIN), BF16)],
        compiler_params=_cparams(("arbitrary",)),
        name="moe_sort",
    )(lpos, h2)


def _moe_tables(p16):
    n_w = jnp.arange(N_WIN, dtype=I32)
    run_start = jnp.cumsum(p16, axis=1) - p16
    rw = jnp.sum(p16, axis=1)
    cum_w = jnp.cumsum(p16, axis=0) - p16
    tot = jnp.sum(p16, axis=0)
    nblk = (tot + TME - 1) // TME
    blk_end = jnp.cumsum(nblk)
    blk0 = blk_end - nblk
    n_used = blk_end[-1]
    b = jnp.arange(N_EBLOCKS + SPARE_BLOCKS, dtype=I32)
    block_e = jnp.minimum(jnp.sum((blk_end[None, :] <= b[:, None]).astype(I32), axis=1), N_EXPERTS - 1)
    piece = jnp.arange(PIECES, dtype=I32) * PIECE
    rp = (b - blk0[block_e])[:, None] * TME + piece[None, :]
    valid = (b[:, None] < n_used) & (rp < tot[block_e][:, None])
    cum_e = cum_w.T[block_e]
    len_e = p16.T[block_e]
    start_e = run_start.T[block_e]
    in_win = (cum_e[:, None, :] <= rp[:, :, None]) & (rp[:, :, None] < (cum_e + len_e)[:, None, :])
    row = n_w[None, None, :] * RL + start_e[:, None, :] + rp[:, :, None] - cum_e[:, None, :]
    row = jnp.sum(jnp.where(in_win, row, 0), axis=2)
    zero_src = N_WIN * RL
    assert (1 + EXPERT_BUFS) * TME <= SPARE_WIN * RL
    trash = N_WIN * RL + (1 + b % EXPERT_BUFS)[:, None] * TME + piece[None, :]
    gather_row = (jnp.where(valid, row, zero_src) // PIECE).reshape(-1).astype(I32)
    scatter_row = (jnp.where(valid, row, trash) // PIECE).reshape(-1).astype(I32)
    rw = jnp.concatenate([rw, jnp.zeros((1,), rw.dtype)])
    return rw.astype(I32), block_e.astype(I32), n_used.astype(I32).reshape(1), gather_row, scatter_row


def _expert_kernel(be_ref, nb_ref, grow_ref, srow_ref, xl_hbm, wg_ref, wu_ref, wd_ref, yl_hbm,
                   xbuf, ybuf, wg_b, wu_b, wd_b, gsem, ssem):
    b = pl.program_id(0)
    nb = nb_ref[0]

    @pl.when(jnp.logical_and(b < nb, jnp.logical_or(b == 0, be_ref[b] != be_ref[jnp.maximum(b - 1, 0)])))
    def _():
        wg_b[...] = wg_ref[...].astype(BF16)
        wu_b[...] = wu_ref[...].astype(BF16)
        wd_b[...] = wd_ref[...].astype(BF16)

    def gather_start(blk, slot):
        for p in range(PIECES):
            pltpu.make_async_copy(xl_hbm.at[grow_ref[blk * PIECES + p]], xbuf.at[slot, p], gsem.at[slot]).start()

    def scatter_start(blk, slot):
        for p in range(PIECES):
            pltpu.make_async_copy(ybuf.at[slot, p], yl_hbm.at[srow_ref[blk * PIECES + p]], ssem.at[slot]).start()

    def gather_wait(slot):
        pltpu.make_async_copy(xl_hbm.at[pl.ds(0, PIECES)], xbuf.at[slot], gsem.at[slot]).wait()

    def scatter_wait(slot):
        pltpu.make_async_copy(ybuf.at[slot], yl_hbm.at[pl.ds(0, PIECES)], ssem.at[slot]).wait()

    @pl.when(b < nb)
    def _():
        slot = b % EXPERT_BUFS
        ahead = (b + 2) % EXPERT_BUFS

        @pl.when(b == 0)
        def _():
            gather_start(0, 0)
            gather_start(1, 1)
            ybuf[2] = jnp.zeros(ybuf.shape[1:], BF16)

        gather_wait(slot)

        @pl.when(b >= 2)
        def _():
            scatter_wait(slot)

        x = xbuf[slot].reshape(TME, D_MODEL)
        hb = _silu(_dot(x, wg_b[...])) * _dot(x, wu_b[...])
        gather_start(b + 2, ahead)
        scatter_start(jnp.where(b == 0, N_EBLOCKS + SPARE_BLOCKS - 1, b - 1), ahead)
        ybuf[slot] = _dot(hb.astype(BF16), wd_b[...]).astype(BF16).reshape(PIECES, PIECE, D_MODEL)

        @pl.when(b == nb - 1)
        def _():
            scatter_start(b, slot)
            gather_wait((b + 1) % EXPERT_BUFS)
            gather_wait(ahead)
            for s in range(EXPERT_BUFS):
                scatter_wait(s)


def _experts(block_e, n_used, gather_row, scatter_row, xl, wg, wu, wd, layer):
    def w_map(i, be, nb, gr, sr):
        return (layer, be[jnp.minimum(i, nb[0] - 1)], 0, 0)

    n_pieces = (N_WIN + SPARE_WIN) * RL // PIECE
    yl = pl.pallas_call(
        _expert_kernel,
        out_shape=jax.ShapeDtypeStruct((n_pieces, PIECE, D_MODEL), BF16),
        input_output_aliases={4: 0},
        grid_spec=pltpu.PrefetchScalarGridSpec(
            num_scalar_prefetch=4,
            grid=(N_EBLOCKS,),
            in_specs=[
                pl.BlockSpec(memory_space=pl.ANY),
                pl.BlockSpec((None, None, D_MODEL, D_EXPERT), w_map),
                pl.BlockSpec((None, None, D_MODEL, D_EXPERT), w_map),
                pl.BlockSpec((None, None, D_EXPERT, D_MODEL), w_map),
            ],
            out_specs=pl.BlockSpec(memory_space=pl.ANY),
            scratch_shapes=[
                pltpu.VMEM((EXPERT_BUFS, PIECES, PIECE, D_MODEL), BF16),
                pltpu.VMEM((EXPERT_BUFS, PIECES, PIECE, D_MODEL), BF16),
                pltpu.VMEM((D_MODEL, D_EXPERT), BF16), pltpu.VMEM((D_MODEL, D_EXPERT), BF16),
                pltpu.VMEM((D_EXPERT, D_MODEL), BF16),
                pltpu.SemaphoreType.DMA((EXPERT_BUFS,)), pltpu.SemaphoreType.DMA((EXPERT_BUFS,)),
            ],
        ),
        compiler_params=_cparams(("arbitrary",)),
        name="moe_experts",
    )(block_e, n_used, gather_row, scatter_row, xl.reshape(n_pieces, PIECE, D_MODEL), wg, wu, wd)
    return yl.reshape(n_pieces * PIECE, D_MODEL)


def _combine_kernel(rw_ref, yl_ref, lpt_ref, wt_ref, h_ref, x1_ref, mod_ref, lng_ref, lnb_ref,
                    wsg_ref, wsu_ref, wsd_ref, *rest, split):
    del rw_ref
    if split:
        outp_ref, outs_ref, p_w, lp_b, wt_b = rest
    else:
        out_ref, p_w, lp_b, wt_b = rest
    hb = h_ref[...]
    shared = _dot((_silu(_dot(hb, wsg_ref[...])) * _dot(hb, wsu_ref[...])).astype(BF16), wsd_ref[...])
    lp = lpt_ref[...]
    wt = wt_ref[...]
    for k in range(TOP_K):
        lp_b[k] = jnp.broadcast_to(lp[:, k:k + 1], (WIN, SORT_ROWS)).astype(jnp.int16)
        wt_b[k] = jnp.broadcast_to(wt[:, k:k + 1], (WIN, SORT_ROWS)).astype(BF16)
    ciota = lax.broadcasted_iota(I32, (WIN, SORT_ROWS), 1)
    for j in range(RL // SORT_ROWS):
        col = (ciota + j * SORT_ROWS).astype(jnp.int16)
        pm = jnp.zeros((WIN, SORT_ROWS), BF16)
        for k in range(TOP_K):
            pm = jnp.where(lp_b[k] == col, wt_b[k], pm)
        p_w[:, j * SORT_ROWS:(j + 1) * SORT_ROWS] = pm
    routed = _dot(p_w[...], yl_ref[...])
    z = ALPHA * x1_ref[...] + mod_ref[5:6, :] * (routed + shared)
    out = _layernorm_rows(z, lng_ref[...], lnb_ref[...])
    if split:
        @pl.when(pl.program_id(0) < N_PROMPT // WIN)
        def _():
            outp_ref[...] = out

        @pl.when(pl.program_id(0) >= N_PROMPT // WIN)
        def _():
            outs_ref[...] = out
    else:
        out_ref[...] = out


def _combine(rw, yl, lpos_t, wt, h2, x1, mods, lng, lnb, wsg, wsu, wsd, *, split):
    full = lambda shape: pl.BlockSpec(shape, lambda i, rw: (0,) * len(shape))
    if split:
        out_shape = (jax.ShapeDtypeStruct((N_PROMPT, D_MODEL), F32), jax.ShapeDtypeStruct((N_SAMPLE, D_MODEL), F32))
        out_specs = tuple(_row_pair_specs(WIN, D_MODEL))
    else:
        out_shape = jax.ShapeDtypeStruct((N_TOK, D_MODEL), F32)
        out_specs = pl.BlockSpec((WIN, D_MODEL), lambda i, rw: (i, 0))
    return pl.pallas_call(
        functools.partial(_combine_kernel, split=split),
        out_shape=out_shape,
        grid_spec=pltpu.PrefetchScalarGridSpec(
            num_scalar_prefetch=1,
            grid=(N_WIN,),
            in_specs=[
                pl.BlockSpec((RL, D_MODEL), lambda i, rw: (i, 0)),
                pl.BlockSpec((WIN, TOP_K), lambda i, rw: (i, 0)),
                pl.BlockSpec((WIN, TOP_K), lambda i, rw: (i, 0)),
                pl.BlockSpec((WIN, D_MODEL), lambda i, rw: (i, 0)),
                pl.BlockSpec((WIN, D_MODEL), lambda i, rw: (i, 0)),
                pl.BlockSpec((None, MOD_ROWS, D_MODEL), lambda i, rw: (_cond_block(i, WIN), 0, 0)),
                full((1, D_MODEL)), full((1, D_MODEL)),
                full((D_MODEL, D_EXPERT)), full((D_MODEL, D_EXPERT)), full((D_EXPERT, D_MODEL)),
            ],
            out_specs=out_specs,
            scratch_shapes=[pltpu.VMEM((WIN, RL), BF16),
                            pltpu.VMEM((TOP_K, WIN, SORT_ROWS), jnp.int16),
                            pltpu.VMEM((TOP_K, WIN, SORT_ROWS), BF16)],
        ),
        compiler_params=_cparams(("arbitrary",)),
        name="moe_combine",
    )(rw, yl, lpos_t, wt, h2, x1, mods, lng, lnb, wsg, wsu, wsd)


def _moe_and_norm(x1, h2, lgt, mods, lng, lnb, router_bias, tri, wg, wu, wd, wsg, wsu, wsd, *, layer, split):
    wts, lpos, p16 = _route(lgt, router_bias.reshape(N_EXPERTS, 1), tri)
    rw, block_e, n_used, gather_row, scatter_row = _moe_tables(p16[:, :N_EXPERTS].astype(I32))
    xl = _sort_rows(lpos, h2)
    yl = _experts(block_e, n_used, gather_row, scatter_row, xl, wg, wu, wd, layer)
    return _combine(rw, yl, lpos.T, wts.T, h2, x1, mods, lng, lnb, wsg, wsu, wsd, split=split)


def _rope_tables_mla():
    t = jnp.arange(DEC_SEQ)
    row = (t // GRID_W).astype(F32)
    col = (t % GRID_W).astype(F32)
    n = QK_ROPE // 4
    inv = ROPE_BASE ** (-jnp.arange(n, dtype=F32) / n)
    ang_r = row[:, None] * inv
    ang_c = col[:, None] * inv
    cos = jnp.ones((DEC_SEQ, LANE), F32)
    sin_m = jnp.zeros((DEC_SEQ, LANE), F32)
    sin_p = jnp.zeros((DEC_SEQ, LANE), F32)
    l0 = ROPE_LANE0
    for base, ang in ((l0, ang_r), (l0 + 2 * n, ang_c)):
        c, s = jnp.cos(ang), jnp.sin(ang)
        cos = cos.at[:, base:base + n].set(c).at[:, base + n:base + 2 * n].set(c)
        sin_m = sin_m.at[:, base:base + n].set(-s)
        sin_p = sin_p.at[:, base + n:base + 2 * n].set(s)
    ident = (jnp.ones((TM, LANE), F32), jnp.zeros((TM, LANE), F32), jnp.zeros((TM, LANE), F32))
    return tuple(jnp.concatenate([i, tbl], axis=0) for i, tbl in zip(ident, (cos, sin_m, sin_p)))


def _rope_tables_ret():
    half = RET_DK // 2
    theta = ROPE_BASE ** (-jnp.linspace(0.0, 1.0, half, dtype=F32))
    ang = jnp.arange(DEC_SEQ, dtype=F32)[:, None] * theta
    cos = jnp.concatenate([jnp.ones((TM, half), F32), jnp.cos(ang)], axis=0)
    sin = jnp.concatenate([jnp.zeros((TM, half), F32), jnp.sin(ang)], axis=0)
    return cos, sin


def _pad_heads(w, width, lane0=0):
    k = w.shape[0]
    w = w.reshape(k, MLA_HEADS, width)
    out = jnp.zeros((k, MLA_HEADS, HEAD_PAD), w.dtype).at[:, :, lane0:lane0 + width].set(w)
    return out.reshape(k, MLA_HEADS * HEAD_PAD)


def _rg_gate_weights(wa, ba, wx, bx):
    n_ct = RG_W // LANE
    per = LANE // RG_BW
    tiles_w, tiles_b = [], []
    for c in range(n_ct):
        cols_w, cols_b = [], []
        for d in range(2):
            for w, b in ((wa, ba), (wx, bx)):
                m = jnp.zeros((LANE, LANE), F32)
                for p in range(per):
                    m = m.at[p * RG_BW:(p + 1) * RG_BW, p * RG_BW:(p + 1) * RG_BW].set(w[d, c * per + p])
                cols_w.append(m)
                cols_b.append(b[d, c * LANE:(c + 1) * LANE])
        tiles_w.append(jnp.concatenate(cols_w, axis=1))
        tiles_b.append(jnp.concatenate(cols_b, axis=0)[None, :])
    return jnp.stack(tiles_w).astype(BF16), jnp.stack(tiles_b)


def kernel(x_prompt, x_sample, cache_mla_ckv, cache_mla_krope, state_rglru, state_ret, c, c_ctx, w_ada, b_ada,
           ln_g, ln_b, w_in_ab, rg_conv_w, rg_conv_b, rg_wa, rg_ba, rg_wx, rg_bx, rg_lambda, mla_q_norm, mla_w_uq,
           mla_kv_norm, mla_w_ukv, w_out_ab, w_in_c, ret_gamma_logit, w_out_c, w_router, router_bias,
           w_exp_gate, w_exp_up, w_exp_down, w_sh_gate, w_sh_up, w_sh_down):
    xp = x_prompt.reshape(N_PROMPT, D_MODEL)
    xs = x_sample.reshape(N_SAMPLE, D_MODEL)
    cond = jnp.zeros((16, D_MODEL), F32).at[0].set(c_ctx).at[1:1 + DEC_BATCH].set(c)
    mods_all = _ada_modulation(cond, w_ada, b_ada).reshape(DEPTH, 16, 6, D_MODEL)[:, :N_COND]
    mods_all = jnp.pad(mods_all, ((0, 0), (0, 0), (0, MOD_ROWS - 6), (0, 0)))

    tri = (jnp.arange(WIN)[:, None] < jnp.arange(WIN)[None, :]).astype(BF16)
    wr_t = jnp.swapaxes(w_router, 1, 2)
    wr_hi = wr_t.astype(BF16)
    wr_lo = (wr_t - wr_hi.astype(F32)).astype(BF16)
    wg_e, wu_e, wd_e = w_exp_gate, w_exp_up, w_exp_down
    wsg, wsu, wsd = w_sh_gate.astype(BF16), w_sh_up.astype(BF16), w_sh_down.astype(BF16)

    l, e = 0, 0
    mods = mods_all[l]
    n_main = 2 * RG_W + Q_LORA + KV_LORA
    w_main = w_in_ab[e][:, :n_main].astype(BF16)
    w_kr = jnp.zeros((D_MODEL, LANE), F32).at[:, ROPE_LANE0:ROPE_LANE0 + QK_ROPE].set(w_in_ab[e][:, n_main:]).astype(BF16)
    main, krp = _proj_ab(xp, xs, mods, w_main, w_kr)

    wg_rg, bg_rg = _rg_gate_weights(rg_wa[e], rg_ba[e], rg_wx[e], rg_bx[e])
    h0_p = jnp.zeros((BATCH, 2, RG_W), F32)
    rg_args = (rg_conv_w[e], rg_conv_b[e].reshape(1, RG_W), wg_rg, bg_rg, rg_lambda[e])
    yrg_p, rg_fin = _rglru(main, *rg_args, h0_p, n_seq=BATCH, seq=SEQ, row_block0=0)
    yrg_s, _ = _rglru(main, *rg_args, state_rglru[:, e],
                      n_seq=DEC_BATCH, seq=DEC_SEQ, row_block0=N_PROMPT // DEC_SEQ)

    cos_t, sm_t, sp_t = _rope_tables_mla()
    w_uq = mla_w_uq[e].reshape(Q_LORA, MLA_HEADS, QK_NOPE + QK_ROPE)
    wq_p = _pad_heads(w_uq.reshape(Q_LORA, -1), QK_NOPE + QK_ROPE).astype(BF16)
    w_ukv = mla_w_ukv[e].reshape(KV_LORA, MLA_HEADS, QK_NOPE + V_HEAD)
    wuk_p = _pad_heads(w_ukv[:, :, :QK_NOPE].reshape(KV_LORA, -1), QK_NOPE).astype(BF16)
    wuvt = w_ukv[:, :, QK_NOPE:].reshape(KV_LORA, MLA_HEADS * V_HEAD).T.astype(BF16)
    q_att, k_att, v_att, ckv_n = _mla_prep(main, krp, cos_t, sm_t, sp_t, mla_q_norm[e].reshape(1, Q_LORA), wq_p,
                                           mla_kv_norm[e].reshape(1, KV_LORA), wuk_p, wuvt)
    ctx_ckv = cache_mla_ckv[:, e].reshape(DEC_BATCH * PAST_LEN, KV_LORA)
    ctx_krp = jnp.zeros((DEC_BATCH * PAST_LEN, LANE), F32).at[:, ROPE_LANE0:ROPE_LANE0 + QK_ROPE].set(
        cache_mla_krope[:, e].reshape(DEC_BATCH * PAST_LEN, QK_ROPE))
    kc_att, vc_att = _mla_ctx(ctx_ckv, ctx_krp, wuk_p, wuvt)

    o_att_p = _attention(q_att, k_att, v_att, None, None, n_seq=BATCH, seq=SEQ, row_block0=0, tq=SEQ)
    o_att_s = _attention(q_att, k_att, v_att, kc_att, vc_att,
                         n_seq=DEC_BATCH, seq=DEC_SEQ, row_block0=N_PROMPT // DEC_SEQ, tq=TQ)

    w_out = w_out_ab[e].astype(BF16)
    x1, h2, lgt = _out_ab(yrg_p, yrg_s, o_att_p, o_att_s, w_out[:RG_W], w_out[RG_W:], xp, xs, mods,
                          ln_g[l, 0].reshape(1, D_MODEL), ln_b[l, 0].reshape(1, D_MODEL), wr_hi[l], wr_lo[l])
    x = _moe_and_norm(x1, h2, lgt, mods, ln_g[l, 1].reshape(1, D_MODEL), ln_b[l, 1].reshape(1, D_MODEL),
                      router_bias[l], tri, wg_e, wu_e, wd_e, wsg[l], wsu[l], wsd[l], layer=l, split=False)

    new_ckv = ckv_n.reshape(BATCH, 1, SEQ, KV_LORA)
    new_krope = krp[:N_PROMPT, ROPE_LANE0:ROPE_LANE0 + QK_ROPE].reshape(BATCH, 1, SEQ, QK_ROPE)
    new_rg = rg_fin.reshape(BATCH, 1, 2, RG_W)

    l, o = 1, 0
    mods = mods_all[l]
    qk = RET_HEADS * RET_DK
    w_c = w_in_c[o].astype(BF16)
    cos_r, sin_r = _rope_tables_ret()
    q_r, k_r, v_r, g_r = _proj_c(x, mods, cos_r, sin_r, w_c[:, :qk], w_c[:, qk:2 * qk],
                                 w_c[:, 2 * qk:2 * qk + MIX_C], w_c[:, 2 * qk + MIX_C:])
    gam = jnp.broadcast_to(ret_gamma_logit[o].astype(F32)[:, :, None, None], (2, RET_HEADS, SUBLANE, LANE))
    r0_p = jnp.zeros((BATCH, 2, RET_HEADS, RET_DK, RET_DV), F32)
    o_ret_p, r_fin = _retention(q_r, k_r, v_r, gam, r0_p, n_seq=BATCH, seq=SEQ, row_block0=0, with_state=True)
    (o_ret_s,) = _retention(q_r, k_r, v_r, gam, state_ret[:, o],
                            n_seq=DEC_BATCH, seq=DEC_SEQ, row_block0=N_PROMPT // DEC_SEQ, with_state=False)
    x1, h2, lgt = _out_c(o_ret_p, o_ret_s, g_r, w_out_c[o].astype(BF16), x, mods,
                         ln_g[l, 0].reshape(1, D_MODEL), ln_b[l, 0].reshape(1, D_MODEL), wr_hi[l], wr_lo[l])
    y_p, y_s = _moe_and_norm(x1, h2, lgt, mods, ln_g[l, 1].reshape(1, D_MODEL), ln_b[l, 1].reshape(1, D_MODEL),
                             router_bias[l], tri, wg_e, wu_e, wd_e, wsg[l], wsu[l], wsd[l], layer=l, split=True)

    y_prompt = y_p.reshape(BATCH, SEQ, D_MODEL)
    y_sample = y_s.reshape(DEC_BATCH, DEC_SEQ, D_MODEL)
    new_ret = r_fin.reshape(BATCH, 1, 2, RET_HEADS, RET_DK, RET_DV)
    return (y_prompt, y_sample, new_ckv, new_krope, new_rg, new_ret)
```

```python
import functools
import math

import jax
import jax.numpy as jnp
from jax import lax
from jax.experimental import pallas as pl
from jax.experimental.pallas import tpu as pltpu

F32 = jnp.float32
BF16 = jnp.bfloat16
I32 = jnp.int32

D_MODEL = 1024
BATCH, SEQ = 16, 256
DEC_BATCH, DEC_SEQ = 8, 4096
PAST_LEN = 256
DEPTH = 2
GRID_W = 64
RG_W, RG_BLOCKS = 512, 8
RG_BW = RG_W // RG_BLOCKS
RG_C = 8.0
CONV_W, CONV_LEFT = 4, 2
MLA_HEADS, QK_NOPE, QK_ROPE, V_HEAD = 8, 64, 32, 64
Q_LORA, KV_LORA = 768, 256
ROPE_BASE = 10000.0
ATTN_SCALE = (QK_NOPE + QK_ROPE) ** -0.5
RET_HEADS, RET_DK, RET_DV, RET_CHUNK = 4, 256, 512, 128
MIX_C = RET_HEADS * RET_DV
N_EXPERTS, TOP_K, N_GROUPS, TOPK_GROUPS = 64, 8, 8, 4
GROUP_SIZE = N_EXPERTS // N_GROUPS
D_EXPERT = 256
ROUTED_SCALE = 2.5
ALPHA = (2 * DEPTH) ** 0.25
EPS = 1e-6

N_PROMPT = BATCH * SEQ
N_SAMPLE = DEC_BATCH * DEC_SEQ
N_TOK = N_PROMPT + N_SAMPLE
N_COND = 1 + DEC_BATCH
MOD_ROWS = 8

LANE = 128
SUBLANE = 8
TM = 512
HEAD_PAD = 128
ROPE_LANE0 = QK_NOPE
TQ = 256
KC = 256
AH = 2
RET_UNROLL = 4
V_ONES = 16
Q_PRESCALE = ATTN_SCALE * math.log2(math.e)
SCAN_ROWS = 64
GATE_ROWS = 256
WIN = 256
N_WIN = N_TOK // WIN
PIECE = 16
SORT_ROWS = 256
RL = 3072
TME = 1024
PIECES = TME // PIECE
EXPERT_BUFS = 3
SPARE_BLOCKS = 2
SPARE_WIN = 2
N_PAIRS = N_TOK * TOP_K
N_EBLOCKS = (N_PAIRS + N_WIN * N_EXPERTS * (PIECE - 1)) // TME + N_EXPERTS
NEG = -1e30


def _cparams(sem, vmem_mb=48):
    return pltpu.CompilerParams(dimension_semantics=sem, vmem_limit_bytes=vmem_mb * 1024 * 1024)


def _cond_block(i, tm):
    npb = N_PROMPT // tm
    return jnp.where(i < npb, 0, 1 + (i - npb) // (DEC_SEQ // tm))


def _pos_block(i, tm):
    npb = N_PROMPT // tm
    return jnp.where(i < npb, 0, 1 + (i - npb) % (DEC_SEQ // tm))


def _split_hi_lo(a):
    hi = a.astype(BF16)
    lo = (a - hi.astype(F32)).astype(BF16)
    return hi, lo


def _dot(a, b):
    return jnp.dot(a, b, preferred_element_type=F32)


def _dot_nt(a, b):
    return lax.dot_general(a, b, (((1,), (1,)), ((), ())), preferred_element_type=F32)


def _silu(x):
    return x * jax.nn.sigmoid(x)


def _gelu_tanh(x):
    return 0.5 * x * (1.0 + jnp.tanh(math.sqrt(2.0 / math.pi) * (x + 0.044715 * (x * x * x))))


def _softplus(x):
    return jnp.maximum(x, 0.0) + jnp.log1p(jnp.exp(-jnp.abs(x)))


def _layernorm_rows(z, g, b):
    mu = jnp.mean(z, axis=-1, keepdims=True)
    zc = z - mu
    var = jnp.mean(zc * zc, axis=-1, keepdims=True)
    return (zc * lax.rsqrt(var + EPS)) * g + b


def _ada_kernel(c_ref, w_ref, b_ref, o_ref):
    s_hi, s_lo = _split_hi_lo(_silu(c_ref[...]))
    w_hi, w_lo = _split_hi_lo(w_ref[...])
    o_ref[...] = _dot(s_hi, w_hi) + _dot(s_hi, w_lo) + _dot(s_lo, w_hi) + b_ref[...]


def _ada_modulation(cond, w_ada, b_ada):
    n6 = 6 * D_MODEL
    tn = D_MODEL
    return pl.pallas_call(
        _ada_kernel,
        out_shape=jax.ShapeDtypeStruct((DEPTH, 16, n6), F32),
        grid=(DEPTH, n6 // tn),
        in_specs=[
            pl.BlockSpec((16, D_MODEL), lambda l, j: (0, 0)),
            pl.BlockSpec((None, D_MODEL, tn), lambda l, j: (l, 0, j)),
            pl.BlockSpec((None, 1, tn), lambda l, j: (l, 0, j)),
        ],
        out_specs=pl.BlockSpec((None, 16, tn), lambda l, j: (l, 0, j)),
        compiler_params=_cparams(("arbitrary", "arbitrary")),
        name="ada_modulation",
    )(cond, w_ada, b_ada.reshape(DEPTH, 1, n6))


def _modulated(x, mod_ref, shift_row, scale_row):
    return x * (1.0 + mod_ref[scale_row:scale_row + 1, :]) + mod_ref[shift_row:shift_row + 1, :]


def _row_pair_specs(tm, width, col=0):
    npb = N_PROMPT // tm
    return [pl.BlockSpec((tm, width), lambda i, *_: (jnp.minimum(i, npb - 1), col)),
            pl.BlockSpec((tm, width), lambda i, *_: (jnp.maximum(i - npb, 0), col))]


def _pick_rows(p_ref, s_ref, tm):
    return jnp.where(pl.program_id(0) < N_PROMPT // tm, p_ref[...], s_ref[...])


def _proj_ab_kernel(xp_ref, xs_ref, mod_ref, w_ref, wkr_ref, main_ref, kr_ref):
    h = _modulated(_pick_rows(xp_ref, xs_ref, TM), mod_ref, 0, 1).astype(BF16)
    n = w_ref.shape[1]
    step = 512
    for j in range(n // step):
        main_ref[:, j * step:(j + 1) * step] = _dot(h, w_ref[:, j * step:(j + 1) * step])
    kr_ref[...] = _dot(h, wkr_ref[...])


def _proj_ab(xp, xs, mods, w_main, w_kr):
    n = w_main.shape[1]
    return pl.pallas_call(
        _proj_ab_kernel,
        out_shape=(jax.ShapeDtypeStruct((N_TOK, n), F32), jax.ShapeDtypeStruct((N_TOK, LANE), F32)),
        grid=(N_TOK // TM,),
        in_specs=_row_pair_specs(TM, D_MODEL) + [
            pl.BlockSpec((None, MOD_ROWS, D_MODEL), lambda i: (_cond_block(i, TM), 0, 0)),
            pl.BlockSpec((D_MODEL, n), lambda i: (0, 0)),
            pl.BlockSpec((D_MODEL, LANE), lambda i: (0, 0)),
        ],
        out_specs=(pl.BlockSpec((TM, n), lambda i: (i, 0)), pl.BlockSpec((TM, LANE), lambda i: (i, 0))),
        compiler_params=_cparams(("arbitrary",)),
        name="proj_ab",
    )(xp, xs, mods, w_main, w_kr)


def _rglru_kernel(xr_ref, gr_ref, cw_ref, cb_ref, wg_ref, bg_ref, lam_ref, h0_ref,
                  y_ref, hfin_ref, xpad, a_s, b_s, *, seq):
    pad = SUBLANE
    xpad[0:pad, :] = jnp.zeros((pad, LANE), F32)
    xpad[seq + pad:seq + 2 * pad, :] = jnp.zeros((pad, LANE), F32)
    xpad[pad:seq + pad, :] = xr_ref[...]

    sp = _softplus(-lam_ref[...])
    cw = cw_ref[...]
    cb = cb_ref[...]
    wg = wg_ref[...]
    bg = bg_ref[...]

    def gate_step(c, carry):
        t0 = pl.multiple_of(c * GATE_ROWS, GATE_ROWS)
        win = xpad[pl.ds(t0, GATE_ROWS + 2 * pad), :]
        xc = cb
        for j in range(CONV_W):
            off = pad - CONV_LEFT + j
            xc = xc + win[off:off + GATE_ROWS, :] * cw[j:j + 1, :]
        g = _dot(xc.astype(BF16), wg) + bg
        for d in range(2):
            r = jax.nn.sigmoid(g[:, (2 * d) * LANE:(2 * d + 1) * LANE])
            i = jax.nn.sigmoid(g[:, (2 * d + 1) * LANE:(2 * d + 2) * LANE])
            log_a = (-RG_C * r) * sp[d:d + 1, :]
            a = jnp.exp(log_a)
            t = jnp.tanh(log_a)
            bt = jnp.sqrt(2.0 * t / (t - 1.0)) * (i * xc)
            a_s[d, pl.ds(t0, GATE_ROWS), :] = a
            b_s[d, pl.ds(t0, GATE_ROWS), :] = bt
        return carry

    lax.fori_loop(0, seq // GATE_ROWS, gate_step, 0, unroll=min(2, seq // GATE_ROWS))

    row = lax.broadcasted_iota(I32, (SCAN_ROWS, LANE), 0) % SUBLANE
    n_steps = seq // SCAN_ROWS
    tiles = SCAN_ROWS // SUBLANE

    def local_scan(a, b, reverse):
        for k in (1, 2, 4):
            if reverse:
                ok = row < SUBLANE - k
                shift = SCAN_ROWS - k
            else:
                ok = row >= k
                shift = k
            a_sh = jnp.where(ok, pltpu.roll(a, shift, 0), 1.0)
            b_sh = jnp.where(ok, pltpu.roll(b, shift, 0), 0.0)
            b = a * b_sh + b
            a = a * a_sh
        return a, b

    def fwd_step(c, h):
        t0 = pl.multiple_of(c * SCAN_ROWS, SCAN_ROWS)
        a, b = local_scan(a_s[0, pl.ds(t0, SCAN_ROWS), :], b_s[0, pl.ds(t0, SCAN_ROWS), :], False)
        outs = []
        for j in range(tiles):
            hj = a[j * SUBLANE:(j + 1) * SUBLANE, :] * h + b[j * SUBLANE:(j + 1) * SUBLANE, :]
            outs.append(hj)
            h = hj[SUBLANE - 1:SUBLANE, :]
        y_ref[pl.ds(t0, SCAN_ROWS), :] = jnp.concatenate(outs, axis=0)
        return h

    h_f = lax.fori_loop(0, n_steps, fwd_step, h0_ref[0:1, :], unroll=4)

    def bwd_step(c, h):
        t0 = pl.multiple_of((n_steps - 1 - c) * SCAN_ROWS, SCAN_ROWS)
        a, b = local_scan(a_s[1, pl.ds(t0, SCAN_ROWS), :], b_s[1, pl.ds(t0, SCAN_ROWS), :], True)
        outs = [None] * tiles
        for j in reversed(range(tiles)):
            hj = a[j * SUBLANE:(j + 1) * SUBLANE, :] * h + b[j * SUBLANE:(j + 1) * SUBLANE, :]
            outs[j] = hj
            h = hj[0:1, :]
        hb = jnp.concatenate(outs, axis=0)
        y_ref[pl.ds(t0, SCAN_ROWS), :] = (y_ref[pl.ds(t0, SCAN_ROWS), :] + hb) * _gelu_tanh(gr_ref[pl.ds(t0, SCAN_ROWS), :])
        return h

    h_b = lax.fori_loop(0, n_steps, bwd_step, h0_ref[1:2, :], unroll=4)
    hfin_ref[0:1, :] = h_f
    hfin_ref[1:2, :] = h_b


def _rglru(main, cw, cb, wg, bg, lam, h0, *, n_seq, seq, row_block0):
    n_ct = RG_W // LANE
    gr_col0 = RG_W // LANE
    kern = functools.partial(_rglru_kernel, seq=seq)
    return pl.pallas_call(
        kern,
        out_shape=(jax.ShapeDtypeStruct((n_seq * seq, RG_W), F32), jax.ShapeDtypeStruct((n_seq, 2, RG_W), F32)),
        grid=(n_seq, n_ct),
        in_specs=[
            pl.BlockSpec((seq, LANE), lambda b, c: (row_block0 + b, c)),
            pl.BlockSpec((seq, LANE), lambda b, c: (row_block0 + b, gr_col0 + c)),
            pl.BlockSpec((CONV_W, LANE), lambda b, c: (0, c)),
            pl.BlockSpec((1, LANE), lambda b, c: (0, c)),
            pl.BlockSpec((None, LANE, 4 * LANE), lambda b, c: (c, 0, 0)),
            pl.BlockSpec((None, 1, 4 * LANE), lambda b, c: (c, 0, 0)),
            pl.BlockSpec((2, LANE), lambda b, c: (0, c)),
            pl.BlockSpec((None, 2, LANE), lambda b, c: (b, 0, c)),
        ],
        out_specs=(
            pl.BlockSpec((seq, LANE), lambda b, c: (b, c)),
            pl.BlockSpec((None, 2, LANE), lambda b, c: (b, 0, c)),
        ),
        scratch_shapes=[
            pltpu.VMEM((seq + 2 * SUBLANE, LANE), F32),
            pltpu.VMEM((2, seq, LANE), F32),
            pltpu.VMEM((2, seq, LANE), F32),
        ],
        compiler_params=_cparams(("arbitrary", "arbitrary")),
        name=f"rglru_s{seq}",
    )(main, main, cw, cb, wg, bg, lam, h0)


def _rope_lanes(x, cos, sin_m, sin_p):
    n = x.shape[1] // LANE
    half = QK_ROPE // 4
    cos_t = jnp.concatenate([cos] * n, axis=1) if n > 1 else cos
    sm_t = jnp.concatenate([sin_m] * n, axis=1) if n > 1 else sin_m
    sp_t = jnp.concatenate([sin_p] * n, axis=1) if n > 1 else sin_p
    up = pltpu.roll(x, x.shape[1] - half, 1)
    dn = pltpu.roll(x, half, 1)
    return x * cos_t + up * sm_t + dn * sp_t


def _rope_sublanes(xt, cos_t, sin_m_t, sin_p_t):
    n = xt.shape[0] // LANE
    half = QK_ROPE // 4
    cos_a = jnp.concatenate([cos_t] * n, axis=0)
    sm_a = jnp.concatenate([sin_m_t] * n, axis=0)
    sp_a = jnp.concatenate([sin_p_t] * n, axis=0)
    up = pltpu.roll(xt, xt.shape[0] - half, 0)
    dn = pltpu.roll(xt, half, 0)
    return xt * cos_a + up * sm_a + dn * sp_a


def _mla_prep_kernel(cq0_ref, cq1_ref, cq2_ref, ckv_ref, kr_ref, cos_ref, sm_ref, sp_ref,
                     cost_ref, smt_ref, spt_ref, qn_ref, wq_ref, kvn_ref, wuk_ref, wuvt_ref,
                     q_ref, k_ref, vt_ref, ckvn_ref):
    cq = [cq0_ref[...], cq1_ref[...], cq2_ref[...]]
    ms = (jnp.sum(cq[0] * cq[0], axis=-1, keepdims=True) + jnp.sum(cq[1] * cq[1], axis=-1, keepdims=True)
          + jnp.sum(cq[2] * cq[2], axis=-1, keepdims=True)) * (1.0 / Q_LORA)
    inv = lax.rsqrt(ms + EPS)
    blk = Q_LORA // 3
    q = None
    for j in range(3):
        cqn = ((cq[j] * inv) * qn_ref[:, j * blk:(j + 1) * blk]).astype(BF16)
        part = _dot(cqn, wq_ref[j * blk:(j + 1) * blk, :])
        q = part if q is None else q + part
    cos, sm, sp = cos_ref[...], sm_ref[...], sp_ref[...]
    q_t = _rope_sublanes(q.T, cost_ref[...], smt_ref[...], spt_ref[...])
    q_ref[...] = (q_t * Q_PRESCALE).astype(BF16)

    ckv = ckv_ref[...]
    inv_kv = lax.rsqrt(jnp.mean(ckv * ckv, axis=-1, keepdims=True) + EPS)
    ckvn = (ckv * inv_kv) * kvn_ref[...]

    @pl.when(pl.program_id(0) < N_PROMPT // TM)
    def _():
        ckvn_ref[...] = ckvn

    ckvn_b = ckvn.astype(BF16)
    kr_rot = _rope_lanes(kr_ref[...], cos, sm, sp)
    k_ref[...] = (_dot(ckvn_b, wuk_ref[...]) + jnp.concatenate([kr_rot] * MLA_HEADS, axis=1)).astype(BF16)
    vt = _dot_nt(wuvt_ref[...], ckvn_b).astype(BF16)
    for c in range(vt_ref.shape[0]):
        vt_ref[c] = vt[:, c * KC:(c + 1) * KC]


def _mla_prep(main, krp, cos_t, sm_t, sp_t, q_norm, wq_p, kv_norm, wuk_p, wuvt):
    cq_col0 = 2 * RG_W // 256
    hp = MLA_HEADS * HEAD_PAD
    full = lambda shape: pl.BlockSpec(shape, lambda i: (0,) * len(shape))
    tab = pl.BlockSpec((TM, LANE), lambda i: (_pos_block(i, TM), 0))
    tab_t = pl.BlockSpec((LANE, TM), lambda i: (0, _pos_block(i, TM)))
    return pl.pallas_call(
        _mla_prep_kernel,
        out_shape=(
            jax.ShapeDtypeStruct((hp, N_TOK), BF16),
            jax.ShapeDtypeStruct((N_TOK, hp), BF16),
            jax.ShapeDtypeStruct((N_TOK // KC, MLA_HEADS * V_HEAD, KC), BF16),
            jax.ShapeDtypeStruct((N_PROMPT, KV_LORA), F32),
        ),
        grid=(N_TOK // TM,),
        in_specs=[
            pl.BlockSpec((TM, 256), lambda i: (i, cq_col0)),
            pl.BlockSpec((TM, 256), lambda i: (i, cq_col0 + 1)),
            pl.BlockSpec((TM, 256), lambda i: (i, cq_col0 + 2)),
            pl.BlockSpec((TM, 256), lambda i: (i, cq_col0 + 3)),
            pl.BlockSpec((TM, LANE), lambda i: (i, 0)),
            tab, tab, tab, tab_t, tab_t, tab_t,
            full((1, Q_LORA)), full((Q_LORA, hp)), full((1, KV_LORA)), full((KV_LORA, hp)),
            full((MLA_HEADS * V_HEAD, KV_LORA)),
        ],
        out_specs=(
            pl.BlockSpec((hp, TM), lambda i: (0, i)),
            pl.BlockSpec((TM, hp), lambda i: (i, 0)),
            pl.BlockSpec((TM // KC, MLA_HEADS * V_HEAD, KC), lambda i: (i, 0, 0)),
            pl.BlockSpec((TM, KV_LORA), lambda i: (jnp.minimum(i, N_PROMPT // TM - 1), 0)),
        ),
        compiler_params=_cparams(("arbitrary",)),
        name="mla_prep",
    )(main, main, main, main, krp, cos_t, sm_t, sp_t, cos_t.T, sm_t.T, sp_t.T, q_norm, wq_p, kv_norm, wuk_p, wuvt)


def _mla_ctx_kernel(ckv_ref, kr_ref, wuk_ref, wuvt_ref, k_ref, vt_ref):
    ckv_b = ckv_ref[...].astype(BF16)
    k_ref[...] = (_dot(ckv_b, wuk_ref[...]) + jnp.concatenate([kr_ref[...]] * MLA_HEADS, axis=1)).astype(BF16)
    vt_ref[...] = _dot_nt(wuvt_ref[...], ckv_b).astype(BF16)


def _mla_ctx(ctx_ckv, ctx_krp, wuk_p, wuvt):
    n = ctx_ckv.shape[0]
    hp = MLA_HEADS * HEAD_PAD
    tm = KC
    full = lambda shape: pl.BlockSpec(shape, lambda i: (0,) * len(shape))
    return pl.pallas_call(
        _mla_ctx_kernel,
        out_shape=(jax.ShapeDtypeStruct((n, hp), BF16),
                   jax.ShapeDtypeStruct((n // tm, MLA_HEADS * V_HEAD, tm), BF16)),
        grid=(n // tm,),
        in_specs=[
            pl.BlockSpec((tm, KV_LORA), lambda i: (i, 0)),
            pl.BlockSpec((tm, LANE), lambda i: (i, 0)),
            full((KV_LORA, hp)), full((MLA_HEADS * V_HEAD, KV_LORA)),
        ],
        out_specs=(pl.BlockSpec((tm, hp), lambda i: (i, 0)),
                   pl.BlockSpec((None, MLA_HEADS * V_HEAD, tm), lambda i: (i, 0, 0))),
        compiler_params=_cparams(("arbitrary",)),
        name="mla_ctx",
    )(ctx_ckv, ctx_krp, wuk_p, wuvt)


def _attn_kernel(*refs, seq, tq, n_ctx):
    if n_ctx:
        q_ref, k_ref, vt_ref, kc_ref, vtc_ref, o_ref, s_scr, p_scr, k_all, vt_all = refs
    else:
        q_ref, k_ref, vt_ref, o_ref, s_scr, p_scr, k_all, vt_all = refs
    has_ctx = 1 if n_ctx else 0
    n_own = seq // KC
    n = n_own + has_ctx
    qs = [q_ref[h * HEAD_PAD:(h + 1) * HEAD_PAD, :] for h in range(AH)]

    @pl.when(pl.program_id(2) == 0)
    def _():
        k_all[0:seq, :] = k_ref[...]
        if has_ctx:
            k_all[seq:seq + KC, :] = kc_ref[...]
        for h in range(AH):
            vt_all[0:n_own, h, 0:V_HEAD, :] = vt_ref[:, h * V_HEAD:(h + 1) * V_HEAD, :]
            if has_ctx:
                vt_all[n_own, h, 0:V_HEAD, :] = vtc_ref[h * V_HEAD:(h + 1) * V_HEAD, :]
            vt_all[:, h, V_HEAD:V_HEAD + V_ONES, :] = jnp.ones((n, V_ONES, KC), BF16)

    def scores(c, slot):
        for h in range(AH):
            s_scr[slot, h] = _dot(k_all[c * KC:(c + 1) * KC, h * HEAD_PAD:(h + 1) * HEAD_PAD], qs[h])

    def softmax_chunk(slot, st):
        out = []
        for h in range(AH):
            m, a1, _, acc = st[h]
            t = s_scr[slot, h]
            m_new = jnp.maximum(m, jnp.max(t, axis=0, keepdims=True))
            p_scr[slot, h] = jnp.exp2(t - m_new).astype(BF16)
            out.append((m_new, jnp.exp2(m - m_new), a1, acc))
        return out

    def weighted_values(c, slot, st, alphas):
        return [st[h][:3] + (alphas[h] * st[h][3] + _dot(vt_all[c, h], p_scr[slot, h]),) for h in range(AH)]

    one = jnp.ones((1, tq), F32)
    st = [(jnp.full((1, tq), NEG, F32), one, one, jnp.zeros((V_HEAD + V_ONES, tq), F32)) for _ in range(AH)]
    scores(0, 0)
    if n > 1:
        scores(1, 1)
    for c in range(n):
        if c + 2 < n:
            scores(c + 2, (c + 2) % 3)
        alphas = [st[h][2] for h in range(AH)]
        st = softmax_chunk(c % 3, st)
        if c >= 2:
            st = weighted_values(c - 2, (c - 2) % 3, st, alphas)
    if n > 1:
        st = weighted_values(n - 2, (n - 2) % 3, st, [st[h][2] for h in range(AH)])
    st = weighted_values(n - 1, (n - 1) % 3, st, [st[h][1] for h in range(AH)])
    for h in range(AH):
        acc = st[h][3]
        o_ref[:, h * V_HEAD:(h + 1) * V_HEAD] = (acc[0:V_HEAD] / acc[V_HEAD:V_HEAD + 1]).T.astype(o_ref.dtype)


def _attention(q, k, vt, kc, vtc, *, n_seq, seq, row_block0, tq):
    n_ctx = 0 if kc is None else PAST_LEN
    n_hp = MLA_HEADS // AH
    nq = seq // tq
    kern = functools.partial(_attn_kernel, seq=seq, tq=tq, n_ctx=n_ctx)
    in_specs = [
        pl.BlockSpec((AH * HEAD_PAD, tq), lambda b, j, i: (j, (row_block0 + b) * nq + i)),
        pl.BlockSpec((seq, AH * HEAD_PAD), lambda b, j, i: (row_block0 + b, j)),
        pl.BlockSpec((seq // KC, AH * V_HEAD, KC), lambda b, j, i: (row_block0 + b, j, 0)),
    ]
    args = [q, k, vt]
    if n_ctx:
        in_specs += [
            pl.BlockSpec((n_ctx, AH * HEAD_PAD), lambda b, j, i: (b, j)),
            pl.BlockSpec((None, AH * V_HEAD, KC), lambda b, j, i: (b, j, 0)),
        ]
        args += [kc, vtc]
    return pl.pallas_call(
        kern,
        out_shape=jax.ShapeDtypeStruct((n_seq * seq, MLA_HEADS * V_HEAD), BF16),
        grid=(n_seq, n_hp, nq),
        in_specs=in_specs,
        out_specs=pl.BlockSpec((tq, AH * V_HEAD), lambda b, j, i: (b * nq + i, j)),
        scratch_shapes=[pltpu.VMEM((3, AH, KC, tq), F32), pltpu.VMEM((3, AH, KC, tq), BF16),
                        pltpu.VMEM((seq + n_ctx, AH * HEAD_PAD), BF16),
                        pltpu.VMEM(((seq + n_ctx) // KC, AH, V_HEAD + V_ONES, KC), BF16)],
        compiler_params=_cparams(("arbitrary", "arbitrary", "arbitrary")),
        name=f"mla_attention_s{seq}",
    )(*args)


def _post_mixer(y, x, mod_ref, lng_ref, lnb_ref, wrh_ref, wrl_ref, x1_ref, h2_ref, lgt_ref):
    z = ALPHA * x + mod_ref[2:3, :] * y
    x1 = _layernorm_rows(z, lng_ref[...], lnb_ref[...])
    x1_ref[...] = x1
    h2 = _modulated(x1, mod_ref, 3, 4)
    h_hi, h_lo = _split_hi_lo(h2)
    h2_ref[...] = h_hi
    w_hi, w_lo = wrh_ref[...], wrl_ref[...]
    lgt_ref[...] = _dot_nt(w_hi, h_hi) + _dot_nt(w_hi, h_lo) + _dot_nt(w_lo, h_hi)


def _out_ab_kernel(yrgp_ref, yrgs_ref, op_ref, os_ref, wa_ref, wb_ref, xp_ref, xs_ref,
                   mod_ref, lng_ref, lnb_ref, wrh_ref, wrl_ref, x1_ref, h2_ref, lgt_ref):
    y = (_dot(_pick_rows(yrgp_ref, yrgs_ref, TM).astype(BF16), wa_ref[...])
         + _dot(_pick_rows(op_ref, os_ref, TM), wb_ref[...]))
    _post_mixer(y, _pick_rows(xp_ref, xs_ref, TM), mod_ref, lng_ref, lnb_ref, wrh_ref, wrl_ref,
                x1_ref, h2_ref, lgt_ref)


def _post_specs():
    full = lambda shape: pl.BlockSpec(shape, lambda i: (0,) * len(shape))
    in_specs = [
        pl.BlockSpec((None, MOD_ROWS, D_MODEL), lambda i: (_cond_block(i, TM), 0, 0)),
        full((1, D_MODEL)), full((1, D_MODEL)),
        full((N_EXPERTS, D_MODEL)), full((N_EXPERTS, D_MODEL)),
    ]
    out_shape = (
        jax.ShapeDtypeStruct((N_TOK, D_MODEL), F32),
        jax.ShapeDtypeStruct((N_TOK, D_MODEL), BF16),
        jax.ShapeDtypeStruct((N_EXPERTS, N_TOK), F32),
    )
    out_specs = (
        pl.BlockSpec((TM, D_MODEL), lambda i: (i, 0)),
        pl.BlockSpec((TM, D_MODEL), lambda i: (i, 0)),
        pl.BlockSpec((N_EXPERTS, TM), lambda i: (0, i)),
    )
    return in_specs, out_shape, out_specs


def _out_ab(yrg_p, yrg_s, o_p, o_s, w_a, w_b, xp, xs, mods, lng, lnb, wr_hi, wr_lo):
    full = lambda shape: pl.BlockSpec(shape, lambda i: (0,) * len(shape))
    post_in, out_shape, out_specs = _post_specs()
    return pl.pallas_call(
        _out_ab_kernel,
        out_shape=out_shape,
        grid=(N_TOK // TM,),
        in_specs=(_row_pair_specs(TM, RG_W) + _row_pair_specs(TM, MLA_HEADS * V_HEAD)
                  + [full((RG_W, D_MODEL)), full((MLA_HEADS * V_HEAD, D_MODEL))]
                  + _row_pair_specs(TM, D_MODEL) + post_in),
        out_specs=out_specs,
        compiler_params=_cparams(("arbitrary",)),
        name="out_ab",
    )(yrg_p, yrg_s, o_p, o_s, w_a, w_b, xp, xs, mods, lng, lnb, wr_hi, wr_lo)


def _proj_c_kernel(x_ref, mod_ref, cos_ref, sin_ref, wq_ref, wk_ref, wv_ref, wg_ref, q_ref, k_ref, v_ref, g_ref):
    h = _modulated(x_ref[...], mod_ref, 0, 1).astype(BF16)
    cos, sin = cos_ref[...], sin_ref[...]
    half = RET_DK // 2
    for hd in range(RET_HEADS):
        for w_ref, is_k in ((wq_ref, False), (wk_ref, True)):
            p = _dot(h, w_ref[:, hd * RET_DK:(hd + 1) * RET_DK])
            x1, x2 = p[:, :half], p[:, half:]
            r1 = x1 * cos - x2 * sin
            r2 = x2 * cos + x1 * sin
            if not is_k:
                q_ref[:, hd * RET_DK:hd * RET_DK + half] = r1.astype(BF16)
                q_ref[:, hd * RET_DK + half:(hd + 1) * RET_DK] = r2.astype(BF16)
            else:
                t1 = (r1 * RET_DK ** -0.5).T.astype(BF16)
                t2 = (r2 * RET_DK ** -0.5).T.astype(BF16)
                for c in range(k_ref.shape[0]):
                    k_ref[c, hd * RET_DK:hd * RET_DK + half, :] = t1[:, c * RET_CHUNK:(c + 1) * RET_CHUNK]
                    k_ref[c, hd * RET_DK + half:(hd + 1) * RET_DK, :] = t2[:, c * RET_CHUNK:(c + 1) * RET_CHUNK]
    step = 512
    for j in range(MIX_C // step):
        v_ref[:, j * step:(j + 1) * step] = _dot(h, wv_ref[:, j * step:(j + 1) * step]).astype(BF16)
        g_ref[:, j * step:(j + 1) * step] = _dot(h, wg_ref[:, j * step:(j + 1) * step])


def _proj_c(x, mods, cos_t, sin_t, wq, wk, wv, wg):
    full = lambda shape: pl.BlockSpec(shape, lambda i: (0,) * len(shape))
    qk = RET_HEADS * RET_DK
    tab = pl.BlockSpec((TM, RET_DK // 2), lambda i: (_pos_block(i, TM), 0))
    return pl.pallas_call(
        _proj_c_kernel,
        out_shape=(
            jax.ShapeDtypeStruct((N_TOK, qk), BF16), jax.ShapeDtypeStruct((N_TOK // RET_CHUNK, qk, RET_CHUNK), BF16),
            jax.ShapeDtypeStruct((N_TOK, MIX_C), BF16), jax.ShapeDtypeStruct((N_TOK, MIX_C), F32),
        ),
        grid=(N_TOK // TM,),
        in_specs=[
            pl.BlockSpec((TM, D_MODEL), lambda i: (i, 0)),
            pl.BlockSpec((None, MOD_ROWS, D_MODEL), lambda i: (_cond_block(i, TM), 0, 0)),
            tab, tab,
            full((D_MODEL, qk)), full((D_MODEL, qk)), full((D_MODEL, MIX_C)), full((D_MODEL, MIX_C)),
        ],
        out_specs=(
            pl.BlockSpec((TM, qk), lambda i: (i, 0)),
            pl.BlockSpec((TM // RET_CHUNK, qk, RET_CHUNK), lambda i: (i, 0, 0)),
            pl.BlockSpec((TM, MIX_C), lambda i: (i, 0)), pl.BlockSpec((TM, MIX_C), lambda i: (i, 0)),
        ),
        compiler_params=_cparams(("arbitrary",), 56),
        name="proj_c",
    )(x, mods, cos_t, sin_t, wq, wk, wv, wg)


def _retention_kernel(*refs, seq, with_state):
    if with_state:
        q_ref, kt_ref, v_ref, gam_ref, r0_ref, o_ref, rfin_ref, r_s = refs
    else:
        q_ref, kt_ref, v_ref, gam_ref, r0_ref, o_ref, r_s = refs
        rfin_ref = None
    c = RET_CHUNK
    n = seq // c
    ii = lax.broadcasted_iota(I32, (c, c), 0).astype(F32)
    jj = lax.broadcasted_iota(I32, (c, c), 1).astype(F32)
    ci = lax.broadcasted_iota(I32, (c, 1), 0).astype(F32)
    li = lax.broadcasted_iota(I32, (1, c), 1).astype(F32)

    consts = []
    for d in range(2):
        gam = gam_ref[d]
        lg_row = -_softplus(-gam[0:1, :])
        lg = jnp.broadcast_to(lg_row, (c, c))
        lg_col = jnp.broadcast_to(lg_row[:, 0:1], (c, 1))
        if d == 0:
            diff = ii - jj
            xi = jnp.exp((ci + 1.0) * lg_col)
            zeta = jnp.exp((c - 1.0 - li) * lg_row)
        else:
            diff = jj - ii
            xi = jnp.exp((c - ci) * lg_col)
            zeta = jnp.exp(li * lg_row)
        dmat = jnp.where(diff >= 0, jnp.exp(jnp.maximum(diff, 0.0) * lg), 0.0)
        g_chunk = jnp.exp(float(c) * lg_row[:, 0:1])
        consts.append((dmat, xi, zeta, g_chunk))
        r_s[d] = r0_ref[d]

    def chunk(d, idx, accumulate):
        dmat, xi, zeta, g_chunk = consts[d]
        t0 = pl.multiple_of(idx * c, c)
        qb = q_ref[pl.ds(t0, c), :]
        kt = kt_ref[idx]
        vb = v_ref[pl.ds(t0, c), :]
        r = r_s[d]
        inner = _dot(qb, kt) * dmat
        o = _dot(inner.astype(BF16), vb) + _dot((qb.astype(F32) * xi).astype(BF16), r.astype(BF16))
        r_s[d] = r * g_chunk + _dot((kt.astype(F32) * zeta).astype(BF16), vb)
        if accumulate:
            o_ref[pl.ds(t0, c), :] = o_ref[pl.ds(t0, c), :] + o
        else:
            o_ref[pl.ds(t0, c), :] = o

    def first_half(s, carry):
        chunk(0, s, False)
        chunk(1, n - 1 - s, False)
        return carry

    def second_half(s, carry):
        chunk(0, s, True)
        chunk(1, n - 1 - s, True)
        return carry

    unroll = min(RET_UNROLL, n // 2)
    lax.fori_loop(0, n // 2, first_half, 0, unroll=unroll)
    lax.fori_loop(n // 2, n, second_half, 0, unroll=unroll)
    if with_state:
        for d in range(2):
            rfin_ref[d] = r_s[d]


def _retention(q, k, v, gam, r0, *, n_seq, seq, row_block0, with_state):
    kern = functools.partial(_retention_kernel, seq=seq, with_state=with_state)
    out_shape = [jax.ShapeDtypeStruct((n_seq * seq, MIX_C), F32)]
    out_specs = [pl.BlockSpec((seq, RET_DV), lambda b, h: (b, h))]
    if with_state:
        out_shape.append(jax.ShapeDtypeStruct((n_seq, 2, RET_HEADS, RET_DK, RET_DV), F32))
        out_specs.append(pl.BlockSpec((None, 2, None, RET_DK, RET_DV), lambda b, h: (b, 0, h, 0, 0)))
    return pl.pallas_call(
        kern,
        out_shape=tuple(out_shape),
        grid=(n_seq, RET_HEADS),
        in_specs=[
            pl.BlockSpec((seq, RET_DK), lambda b, h: (row_block0 + b, h)),
            pl.BlockSpec((seq // RET_CHUNK, RET_DK, RET_CHUNK), lambda b, h: (row_block0 + b, h, 0)),
            pl.BlockSpec((seq, RET_DV), lambda b, h: (row_block0 + b, h)),
            pl.BlockSpec((2, None, SUBLANE, LANE), lambda b, h: (0, h, 0, 0)),
            pl.BlockSpec((None, 2, None, RET_DK, RET_DV), lambda b, h: (b, 0, h, 0, 0)),
        ],
        out_specs=tuple(out_specs),
        scratch_shapes=[pltpu.VMEM((2, RET_DK, RET_DV), F32)],
        compiler_params=_cparams(("arbitrary", "arbitrary"), 56),
        name=f"retention_s{seq}",
    )(q, k, v, gam, r0)


def _out_c_kernel(op_ref, os_ref, g_ref, w_ref, x_ref, mod_ref, lng_ref, lnb_ref, wrh_ref, wrl_ref,
                  x1_ref, h2_ref, lgt_ref):
    y = None
    o_all = _pick_rows(op_ref, os_ref, TM)
    for hd in range(RET_HEADS):
        o = o_all[:, hd * RET_DV:(hd + 1) * RET_DV]
        mu = jnp.mean(o, axis=-1, keepdims=True)
        oc = o - mu
        var = jnp.mean(oc * oc, axis=-1, keepdims=True)
        on = oc * lax.rsqrt(var + EPS)
        a = (on * _silu(g_ref[:, hd * RET_DV:(hd + 1) * RET_DV])).astype(BF16)
        part = _dot(a, w_ref[hd * RET_DV:(hd + 1) * RET_DV, :])
        y = part if y is None else y + part
    _post_mixer(y, x_ref[...], mod_ref, lng_ref, lnb_ref, wrh_ref, wrl_ref, x1_ref, h2_ref, lgt_ref)


def _out_c(o_p, o_s, g, w, x, mods, lng, lnb, wr_hi, wr_lo):
    full = lambda shape: pl.BlockSpec(shape, lambda i: (0,) * len(shape))
    post_in, out_shape, out_specs = _post_specs()
    return pl.pallas_call(
        _out_c_kernel,
        out_shape=out_shape,
        grid=(N_TOK // TM,),
        in_specs=_row_pair_specs(TM, MIX_C) + [
            pl.BlockSpec((TM, MIX_C), lambda i: (i, 0)),
            full((MIX_C, D_MODEL)),
            pl.BlockSpec((TM, D_MODEL), lambda i: (i, 0)),
        ] + post_in,
        out_specs=out_specs,
        compiler_params=_cparams(("arbitrary",), 56),
        name="out_c",
    )(o_p, o_s, g, w, x, mods, lng, lnb, wr_hi, wr_lo)


def _route_kernel(lgt_ref, bias_ref, tri_ref, w_ref, lpos_ref, p16_ref):
    tt = lgt_ref.shape[1]
    scores = jax.nn.sigmoid(lgt_ref[...])
    sel = scores + bias_ref[...]
    srow = lax.broadcasted_iota(I32, (GROUP_SIZE, tt), 0).astype(F32)
    ninf = -jnp.inf

    gs = []
    for g in range(N_GROUPS):
        sg = sel[g * GROUP_SIZE:(g + 1) * GROUP_SIZE, :]
        m1 = jnp.max(sg, axis=0, keepdims=True)
        i1 = jnp.min(jnp.where(sg == m1, srow, float(GROUP_SIZE)), axis=0, keepdims=True)
        m2 = jnp.max(jnp.where(srow == i1, ninf, sg), axis=0, keepdims=True)
        gs.append(m1 + m2)
    gs = jnp.concatenate(gs, axis=0)
    chosen = jnp.zeros((N_GROUPS, tt), F32)
    for _ in range(TOPK_GROUPS):
        mg = jnp.max(gs, axis=0, keepdims=True)
        gi = jnp.min(jnp.where(gs == mg, srow, float(N_GROUPS)), axis=0, keepdims=True)
        hit = srow == gi
        chosen = jnp.where(hit, 1.0, chosen)
        gs = jnp.where(hit, ninf, gs)
    sel = jnp.concatenate(
        [jnp.where(jnp.broadcast_to(chosen[g:g + 1, :], (GROUP_SIZE, tt)) > 0.5,
                   sel[g * GROUP_SIZE:(g + 1) * GROUP_SIZE, :], ninf) for g in range(N_GROUPS)], axis=0)

    erow = lax.broadcasted_iota(I32, (N_EXPERTS, tt), 0).astype(F32)
    ids, ws = [], []
    for _ in range(TOP_K):
        m = jnp.max(sel, axis=0, keepdims=True)
        ei = jnp.min(jnp.where(sel == m, erow, float(N_EXPERTS)), axis=0, keepdims=True)
        hit = erow == ei
        ids.append(ei)
        ws.append(jnp.sum(jnp.where(hit, scores, 0.0), axis=0, keepdims=True))
        sel = jnp.where(hit, ninf, sel)
    wsum = ws[0]
    for k in range(1, TOP_K):
        wsum = wsum + ws[k]
    w_ref[...] = jnp.concatenate([w / wsum * ROUTED_SCALE for w in ws], axis=0)

    member_f = jnp.zeros((N_EXPERTS, tt), F32)
    for k in range(TOP_K):
        member_f = jnp.where(erow == ids[k], 1.0, member_f)
    member_b = member_f.astype(BF16)
    cnt_row = _dot_nt(jnp.ones((SUBLANE, tt), BF16), member_b)[0:1, :]
    p16_row = jnp.ceil(cnt_row * (1.0 / PIECE)) * PIECE
    lane_e = lax.broadcasted_iota(I32, (N_EXPERTS, N_EXPERTS), 1)
    sub_e = lax.broadcasted_iota(I32, (N_EXPERTS, N_EXPERTS), 0)
    run_start = jnp.sum(jnp.where(lane_e < sub_e, jnp.broadcast_to(p16_row, (N_EXPERTS, N_EXPERTS)), 0.0),
                        axis=1, keepdims=True)
    rank = _dot(member_b, tri_ref[...]) + run_start
    lpos_ref[...] = jnp.concatenate(
        [jnp.sum(jnp.where(erow == ids[k], rank, 0.0), axis=0, keepdims=True) for k in range(TOP_K)],
        axis=0).astype(I32)
    p16_ref[pl.ds(pl.program_id(0), 1), :] = jnp.concatenate(
        [p16_row, jnp.zeros((1, LANE - N_EXPERTS), F32)], axis=1)


def _route(lgt, bias, tri):
    return pl.pallas_call(
        _route_kernel,
        out_shape=(
            jax.ShapeDtypeStruct((TOP_K, N_TOK), F32), jax.ShapeDtypeStruct((TOP_K, N_TOK), I32),
            jax.ShapeDtypeStruct((N_WIN, LANE), F32),
        ),
        grid=(N_WIN,),
        in_specs=[
            pl.BlockSpec((N_EXPERTS, WIN), lambda i: (0, i)),
            pl.BlockSpec((N_EXPERTS, 1), lambda i: (0, 0)),
            pl.BlockSpec((WIN, WIN), lambda i: (0, 0)),
        ],
        out_specs=(
            pl.BlockSpec((TOP_K, WIN), lambda i: (0, i)), pl.BlockSpec((TOP_K, WIN), lambda i: (0, i)),
            pl.BlockSpec((N_WIN, LANE), lambda i: (0, 0)),
        ),
        compiler_params=_cparams(("arbitrary",)),
        name="moe_route",
    )(lgt, bias, tri)


def _sort_kernel(lpos_ref, h_ref, xl_ref, oh_all):
    lp = jnp.where(pl.program_id(0) < N_WIN, lpos_ref[...], -1)
    riota = lax.broadcasted_iota(I32, (SORT_ROWS, WIN), 0).astype(jnp.int16)
    one = jnp.ones((SORT_ROWS, WIN), BF16)
    for j in range(RL // SORT_ROWS):
        rel = (lp - j * SORT_ROWS).astype(jnp.int16)
        oh = jnp.zeros((SORT_ROWS, WIN), BF16)
        for k in range(TOP_K):
            oh = jnp.where(rel[k:k + 1, :] == riota, one, oh)
        oh_all[j * SORT_ROWS:(j + 1) * SORT_ROWS, :] = oh
    x = h_ref[...]
    for nt in range(D_MODEL // SORT_ROWS):
        cols = slice(nt * SORT_ROWS, (nt + 1) * SORT_ROWS)
        xl_ref[:, cols] = _dot(oh_all[...], x[:, cols]).astype(BF16)


def _sort_rows(lpos, h2):
    last = N_WIN - 1
    return pl.pallas_call(
        _sort_kernel,
        out_shape=jax.ShapeDtypeStruct(((N_WIN + SPARE_WIN) * RL, D_MODEL), BF16),
        grid=(N_WIN + SPARE_WIN,),
        in_specs=[
            pl.BlockSpec((TOP_K, WIN), lambda i: (0, jnp.minimum(i, last))),
            pl.BlockSpec((WIN, D_MODEL), lambda i: (jnp.minimum(i, last), 0)),
        ],
        out_specs=pl.BlockSpec((RL, D_MODEL), lambda i: (i, 0)),
        scratch_shapes=[pltpu.VMEM((RL, WIN), BF16)],
        compiler_params=_cparams(("arbitrary",)),
        name="moe_sort",
    )(lpos, h2)


def _moe_tables(p16):
    n_w = jnp.arange(N_WIN, dtype=I32)
    run_start = jnp.cumsum(p16, axis=1) - p16
    rw = jnp.sum(p16, axis=1)
    cum_w = jnp.cumsum(p16, axis=0) - p16
    tot = jnp.sum(p16, axis=0)
    nblk = (tot + TME - 1) // TME
    blk_end = jnp.cumsum(nblk)
    blk0 = blk_end - nblk
    n_used = blk_end[-1]
    b = jnp.arange(N_EBLOCKS + SPARE_BLOCKS, dtype=I32)
    block_e = jnp.minimum(jnp.sum((blk_end[None, :] <= b[:, None]).astype(I32), axis=1), N_EXPERTS - 1)
    piece = jnp.arange(PIECES, dtype=I32) * PIECE
    rp = (b - blk0[block_e])[:, None] * TME + piece[None, :]
    valid = (b[:, None] < n_used) & (rp < tot[block_e][:, None])
    cum_e = cum_w.T[block_e]
    len_e = p16.T[block_e]
    start_e = run_start.T[block_e]
    in_win = (cum_e[:, None, :] <= rp[:, :, None]) & (rp[:, :, None] < (cum_e + len_e)[:, None, :])
    row = n_w[None, None, :] * RL + start_e[:, None, :] + rp[:, :, None] - cum_e[:, None, :]
    row = jnp.sum(jnp.where(in_win, row, 0), axis=2)
    zero_src = N_WIN * RL
    assert (1 + EXPERT_BUFS) * TME <= SPARE_WIN * RL
    trash = N_WIN * RL + (1 + b % EXPERT_BUFS)[:, None] * TME + piece[None, :]
    gather_row = (jnp.where(valid, row, zero_src) // PIECE).reshape(-1).astype(I32)
    scatter_row = (jnp.where(valid, row, trash) // PIECE).reshape(-1).astype(I32)
    rw = jnp.concatenate([rw, jnp.zeros((1,), rw.dtype)])
    return rw.astype(I32), block_e.astype(I32), n_used.astype(I32).reshape(1), gather_row, scatter_row


def _expert_kernel(be_ref, nb_ref, grow_ref, srow_ref, xl_hbm, wg_ref, wu_ref, wd_ref, yl_hbm,
                   xbuf, ybuf, wg_b, wu_b, wd_b, gsem, ssem):
    b = pl.program_id(0)
    nb = nb_ref[0]

    @pl.when(jnp.logical_and(b < nb, jnp.logical_or(b == 0, be_ref[b] != be_ref[jnp.maximum(b - 1, 0)])))
    def _():
        wg_b[...] = wg_ref[...].astype(BF16)
        wu_b[...] = wu_ref[...].astype(BF16)
        wd_b[...] = wd_ref[...].astype(BF16)

    def gather_start(blk, slot):
        for p in range(PIECES):
            pltpu.make_async_copy(xl_hbm.at[grow_ref[blk * PIECES + p]], xbuf.at[slot, p], gsem.at[slot]).start()

    def scatter_start(blk, slot):
        for p in range(PIECES):
            pltpu.make_async_copy(ybuf.at[slot, p], yl_hbm.at[srow_ref[blk * PIECES + p]], ssem.at[slot]).start()

    def gather_wait(slot):
        pltpu.make_async_copy(xl_hbm.at[pl.ds(0, PIECES)], xbuf.at[slot], gsem.at[slot]).wait()

    def scatter_wait(slot):
        pltpu.make_async_copy(ybuf.at[slot], yl_hbm.at[pl.ds(0, PIECES)], ssem.at[slot]).wait()

    @pl.when(b < nb)
    def _():
        slot = b % EXPERT_BUFS
        ahead = (b + 2) % EXPERT_BUFS

        @pl.when(b == 0)
        def _():
            gather_start(0, 0)
            gather_start(1, 1)
            ybuf[2] = jnp.zeros(ybuf.shape[1:], BF16)

        gather_wait(slot)

        @pl.when(b >= 2)
        def _():
            scatter_wait(slot)

        x = xbuf[slot].reshape(TME, D_MODEL)
        hb = _silu(_dot(x, wg_b[...])) * _dot(x, wu_b[...])
        gather_start(b + 2, ahead)
        scatter_start(jnp.where(b == 0, N_EBLOCKS + SPARE_BLOCKS - 1, b - 1), ahead)
        ybuf[slot] = _dot(hb.astype(BF16), wd_b[...]).astype(BF16).reshape(PIECES, PIECE, D_MODEL)

        @pl.when(b == nb - 1)
        def _():
            scatter_start(b, slot)
            gather_wait((b + 1) % EXPERT_BUFS)
            gather_wait(ahead)
            for s in range(EXPERT_BUFS):
                scatter_wait(s)


def _experts(block_e, n_used, gather_row, scatter_row, xl, wg, wu, wd, layer):
    def w_map(i, be, nb, gr, sr):
        return (layer, be[jnp.minimum(i, nb[0] - 1)], 0, 0)

    n_pieces = (N_WIN + SPARE_WIN) * RL // PIECE
    yl = pl.pallas_call(
        _expert_kernel,
        out_shape=jax.ShapeDtypeStruct((n_pieces, PIECE, D_MODEL), BF16),
        input_output_aliases={4: 0},
        grid_spec=pltpu.PrefetchScalarGridSpec(
            num_scalar_prefetch=4,
            grid=(N_EBLOCKS,),
            in_specs=[
                pl.BlockSpec(memory_space=pl.ANY),
                pl.BlockSpec((None, None, D_MODEL, D_EXPERT), w_map),
                pl.BlockSpec((None, None, D_MODEL, D_EXPERT), w_map),
                pl.BlockSpec((None, None, D_EXPERT, D_MODEL), w_map),
            ],
            out_specs=pl.BlockSpec(memory_space=pl.ANY),
            scratch_shapes=[
                pltpu.VMEM((EXPERT_BUFS, PIECES, PIECE, D_MODEL), BF16),
                pltpu.VMEM((EXPERT_BUFS, PIECES, PIECE, D_MODEL), BF16),
                pltpu.VMEM((D_MODEL, D_EXPERT), BF16), pltpu.VMEM((D_MODEL, D_EXPERT), BF16),
                pltpu.VMEM((D_EXPERT, D_MODEL), BF16),
                pltpu.SemaphoreType.DMA((EXPERT_BUFS,)), pltpu.SemaphoreType.DMA((EXPERT_BUFS,)),
            ],
        ),
        compiler_params=_cparams(("arbitrary",)),
        name="moe_experts",
    )(block_e, n_used, gather_row, scatter_row, xl.reshape(n_pieces, PIECE, D_MODEL), wg, wu, wd)
    return yl.reshape(n_pieces * PIECE, D_MODEL)


def _combine_kernel(rw_ref, yl_ref, lpt_ref, wt_ref, h_ref, x1_ref, mod_ref, lng_ref, lnb_ref,
                    wsg_ref, wsu_ref, wsd_ref, *rest, split):
    del rw_ref
    if split:
        outp_ref, outs_ref, p_w, lp_b, wt_b = rest
    else:
        out_ref, p_w, lp_b, wt_b = rest
    hb = h_ref[...]
    shared = _dot((_silu(_dot(hb, wsg_ref[...])) * _dot(hb, wsu_ref[...])).astype(BF16), wsd_ref[...])
    lp = lpt_ref[...]
    wt = wt_ref[...]
    for k in range(TOP_K):
        lp_b[k] = jnp.broadcast_to(lp[:, k:k + 1], (WIN, SORT_ROWS)).astype(jnp.int16)
        wt_b[k] = jnp.broadcast_to(wt[:, k:k + 1], (WIN, SORT_ROWS)).astype(BF16)
    ciota = lax.broadcasted_iota(I32, (WIN, SORT_ROWS), 1)
    for j in range(RL // SORT_ROWS):
        col = (ciota + j * SORT_ROWS).astype(jnp.int16)
        pm = jnp.zeros((WIN, SORT_ROWS), BF16)
        for k in range(TOP_K):
            pm = jnp.where(lp_b[k] == col, wt_b[k], pm)
        p_w[:, j * SORT_ROWS:(j + 1) * SORT_ROWS] = pm
    routed = _dot(p_w[...], yl_ref[...])
    z = ALPHA * x1_ref[...] + mod_ref[5:6, :] * (routed + shared)
    out = _layernorm_rows(z, lng_ref[...], lnb_ref[...])
    if split:
        @pl.when(pl.program_id(0) < N_PROMPT // WIN)
        def _():
            outp_ref[...] = out

        @pl.when(pl.program_id(0) >= N_PROMPT // WIN)
        def _():
            outs_ref[...] = out
    else:
        out_ref[...] = out


def _combine(rw, yl, lpos_t, wt, h2, x1, mods, lng, lnb, wsg, wsu, wsd, *, split):
    full = lambda shape: pl.BlockSpec(shape, lambda i, rw: (0,) * len(shape))
    if split:
        out_shape = (jax.ShapeDtypeStruct((N_PROMPT, D_MODEL), F32), jax.ShapeDtypeStruct((N_SAMPLE, D_MODEL), F32))
        out_specs = tuple(_row_pair_specs(WIN, D_MODEL))
    else:
        out_shape = jax.ShapeDtypeStruct((N_TOK, D_MODEL), F32)
        out_specs = pl.BlockSpec((WIN, D_MODEL), lambda i, rw: (i, 0))
    return pl.pallas_call(
        functools.partial(_combine_kernel, split=split),
        out_shape=out_shape,
        grid_spec=pltpu.PrefetchScalarGridSpec(
            num_scalar_prefetch=1,
            grid=(N_WIN,),
            in_specs=[
                pl.BlockSpec((RL, D_MODEL), lambda i, rw: (i, 0)),
                pl.BlockSpec((WIN, TOP_K), lambda i, rw: (i, 0)),
                pl.BlockSpec((WIN, TOP_K), lambda i, rw: (i, 0)),
                pl.BlockSpec((WIN, D_MODEL), lambda i, rw: (i, 0)),
                pl.BlockSpec((WIN, D_MODEL), lambda i, rw: (i, 0)),
                pl.BlockSpec((None, MOD_ROWS, D_MODEL), lambda i, rw: (_cond_block(i, WIN), 0, 0)),
                full((1, D_MODEL)), full((1, D_MODEL)),
                full((D_MODEL, D_EXPERT)), full((D_MODEL, D_EXPERT)), full((D_EXPERT, D_MODEL)),
            ],
            out_specs=out_specs,
            scratch_shapes=[pltpu.VMEM((WIN, RL), BF16),
                            pltpu.VMEM((TOP_K, WIN, SORT_ROWS), jnp.int16),
                            pltpu.VMEM((TOP_K, WIN, SORT_ROWS), BF16)],
        ),
        compiler_params=_cparams(("arbitrary",)),
        name="moe_combine",
    )(rw, yl, lpos_t, wt, h2, x1, mods, lng, lnb, wsg, wsu, wsd)


def _moe_and_norm(x1, h2, lgt, mods, lng, lnb, router_bias, tri, wg, wu, wd, wsg, wsu, wsd, *, layer, split):
    wts, lpos, p16 = _route(lgt, router_bias.reshape(N_EXPERTS, 1), tri)
    rw, block_e, n_used, gather_row, scatter_row = _moe_tables(p16[:, :N_EXPERTS].astype(I32))
    xl = _sort_rows(lpos, h2)
    yl = _experts(block_e, n_used, gather_row, scatter_row, xl, wg, wu, wd, layer)
    return _combine(rw, yl, lpos.T, wts.T, h2, x1, mods, lng, lnb, wsg, wsu, wsd, split=split)


def _rope_tables_mla():
    t = jnp.arange(DEC_SEQ)
    row = (t // GRID_W).astype(F32)
    col = (t % GRID_W).astype(F32)
    n = QK_ROPE // 4
    inv = ROPE_BASE ** (-jnp.arange(n, dtype=F32) / n)
    ang_r = row[:, None] * inv
    ang_c = col[:, None] * inv
    cos = jnp.ones((DEC_SEQ, LANE), F32)
    sin_m = jnp.zeros((DEC_SEQ, LANE), F32)
    sin_p = jnp.zeros((DEC_SEQ, LANE), F32)
    l0 = ROPE_LANE0
    for base, ang in ((l0, ang_r), (l0 + 2 * n, ang_c)):
        c, s = jnp.cos(ang), jnp.sin(ang)
        cos = cos.at[:, base:base + n].set(c).at[:, base + n:base + 2 * n].set(c)
        sin_m = sin_m.at[:, base:base + n].set(-s)
        sin_p = sin_p.at[:, base + n:base + 2 * n].set(s)
    ident = (jnp.ones((TM, LANE), F32), jnp.zeros((TM, LANE), F32), jnp.zeros((TM, LANE), F32))
    return tuple(jnp.concatenate([i, tbl], axis=0) for i, tbl in zip(ident, (cos, sin_m, sin_p)))


def _rope_tables_ret():
    half = RET_DK // 2
    theta = ROPE_BASE ** (-jnp.linspace(0.0, 1.0, half, dtype=F32))
    ang = jnp.arange(DEC_SEQ, dtype=F32)[:, None] * theta
    cos = jnp.concatenate([jnp.ones((TM, half), F32), jnp.cos(ang)], axis=0)
    sin = jnp.concatenate([jnp.zeros((TM, half), F32), jnp.sin(ang)], axis=0)
    return cos, sin


def _pad_heads(w, width, lane0=0):
    k = w.shape[0]
    w = w.reshape(k, MLA_HEADS, width)
    out = jnp.zeros((k, MLA_HEADS, HEAD_PAD), w.dtype).at[:, :, lane0:lane0 + width].set(w)
    return out.reshape(k, MLA_HEADS * HEAD_PAD)


def _rg_gate_weights(wa, ba, wx, bx):
    n_ct = RG_W // LANE
    per = LANE // RG_BW
    tiles_w, tiles_b = [], []
    for c in range(n_ct):
        cols_w, cols_b = [], []
        for d in range(2):
            for w, b in ((wa, ba), (wx, bx)):
                m = jnp.zeros((LANE, LANE), F32)
                for p in range(per):
                    m = m.at[p * RG_BW:(p + 1) * RG_BW, p * RG_BW:(p + 1) * RG_BW].set(w[d, c * per + p])
                cols_w.append(m)
                cols_b.append(b[d, c * LANE:(c + 1) * LANE])
        tiles_w.append(jnp.concatenate(cols_w, axis=1))
        tiles_b.append(jnp.concatenate(cols_b, axis=0)[None, :])
    return jnp.stack(tiles_w).astype(BF16), jnp.stack(tiles_b)


def kernel(x_prompt, x_sample, cache_mla_ckv, cache_mla_krope, state_rglru, state_ret, c, c_ctx, w_ada, b_ada,
           ln_g, ln_b, w_in_ab, rg_conv_w, rg_conv_b, rg_wa, rg_ba, rg_wx, rg_bx, rg_lambda, mla_q_norm, mla_w_uq,
           mla_kv_norm, mla_w_ukv, w_out_ab, w_in_c, ret_gamma_logit, w_out_c, w_router, router_bias,
           w_exp_gate, w_exp_up, w_exp_down, w_sh_gate, w_sh_up, w_sh_down):
    xp = x_prompt.reshape(N_PROMPT, D_MODEL)
    xs = x_sample.reshape(N_SAMPLE, D_MODEL)
    cond = jnp.zeros((16, D_MODEL), F32).at[0].set(c_ctx).at[1:1 + DEC_BATCH].set(c)
    mods_all = _ada_modulation(cond, w_ada, b_ada).reshape(DEPTH, 16, 6, D_MODEL)[:, :N_COND]
    mods_all = jnp.pad(mods_all, ((0, 0), (0, 0), (0, MOD_ROWS - 6), (0, 0)))

    tri = (jnp.arange(WIN)[:, None] < jnp.arange(WIN)[None, :]).astype(BF16)
    wr_t = jnp.swapaxes(w_router, 1, 2)
    wr_hi = wr_t.astype(BF16)
    wr_lo = (wr_t - wr_hi.astype(F32)).astype(BF16)
    wg_e, wu_e, wd_e = w_exp_gate, w_exp_up, w_exp_down
    wsg, wsu, wsd = w_sh_gate.astype(BF16), w_sh_up.astype(BF16), w_sh_down.astype(BF16)

    l, e = 0, 0
    mods = mods_all[l]
    n_main = 2 * RG_W + Q_LORA + KV_LORA
    w_main = w_in_ab[e][:, :n_main].astype(BF16)
    w_kr = jnp.zeros((D_MODEL, LANE), F32).at[:, ROPE_LANE0:ROPE_LANE0 + QK_ROPE].set(w_in_ab[e][:, n_main:]).astype(BF16)
    main, krp = _proj_ab(xp, xs, mods, w_main, w_kr)

    wg_rg, bg_rg = _rg_gate_weights(rg_wa[e], rg_ba[e], rg_wx[e], rg_bx[e])
    h0_p = jnp.zeros((BATCH, 2, RG_W), F32)
    rg_args = (rg_conv_w[e], rg_conv_b[e].reshape(1, RG_W), wg_rg, bg_rg, rg_lambda[e])
    yrg_p, rg_fin = _rglru(main, *rg_args, h0_p, n_seq=BATCH, seq=SEQ, row_block0=0)
    yrg_s, _ = _rglru(main, *rg_args, state_rglru[:, e],
                      n_seq=DEC_BATCH, seq=DEC_SEQ, row_block0=N_PROMPT // DEC_SEQ)

    cos_t, sm_t, sp_t = _rope_tables_mla()
    w_uq = mla_w_uq[e].reshape(Q_LORA, MLA_HEADS, QK_NOPE + QK_ROPE)
    wq_p = _pad_heads(w_uq.reshape(Q_LORA, -1), QK_NOPE + QK_ROPE).astype(BF16)
    w_ukv = mla_w_ukv[e].reshape(KV_LORA, MLA_HEADS, QK_NOPE + V_HEAD)
    wuk_p = _pad_heads(w_ukv[:, :, :QK_NOPE].reshape(KV_LORA, -1), QK_NOPE).astype(BF16)
    wuvt = w_ukv[:, :, QK_NOPE:].reshape(KV_LORA, MLA_HEADS * V_HEAD).T.astype(BF16)
    q_att, k_att, v_att, ckv_n = _mla_prep(main, krp, cos_t, sm_t, sp_t, mla_q_norm[e].reshape(1, Q_LORA), wq_p,
                                           mla_kv_norm[e].reshape(1, KV_LORA), wuk_p, wuvt)
    ctx_ckv = cache_mla_ckv[:, e].reshape(DEC_BATCH * PAST_LEN, KV_LORA)
    ctx_krp = jnp.zeros((DEC_BATCH * PAST_LEN, LANE), F32).at[:, ROPE_LANE0:ROPE_LANE0 + QK_ROPE].set(
        cache_mla_krope[:, e].reshape(DEC_BATCH * PAST_LEN, QK_ROPE))
    kc_att, vc_att = _mla_ctx(ctx_ckv, ctx_krp, wuk_p, wuvt)

    o_att_p = _attention(q_att, k_att, v_att, None, None, n_seq=BATCH, seq=SEQ, row_block0=0, tq=SEQ)
    o_att_s = _attention(q_att, k_att, v_att, kc_att, vc_att,
                         n_seq=DEC_BATCH, seq=DEC_SEQ, row_block0=N_PROMPT // DEC_SEQ, tq=TQ)

    w_out = w_out_ab[e].astype(BF16)
    x1, h2, lgt = _out_ab(yrg_p, yrg_s, o_att_p, o_att_s, w_out[:RG_W], w_out[RG_W:], xp, xs, mods,
                          ln_g[l, 0].reshape(1, D_MODEL), ln_b[l, 0].reshape(1, D_MODEL), wr_hi[l], wr_lo[l])
    x = _moe_and_norm(x1, h2, lgt, mods, ln_g[l, 1].reshape(1, D_MODEL), ln_b[l, 1].reshape(1, D_MODEL),
                      router_bias[l], tri, wg_e, wu_e, wd_e, wsg[l], wsu[l], wsd[l], layer=l, split=False)

    new_ckv = ckv_n.reshape(BATCH, 1, SEQ, KV_LORA)
    new_krope = krp[:N_PROMPT, ROPE_LANE0:ROPE_LANE0 + QK_ROPE].reshape(BATCH, 1, SEQ, QK_ROPE)
    new_rg = rg_fin.reshape(BATCH, 1, 2, RG_W)

    l, o = 1, 0
    mods = mods_all[l]
    qk = RET_HEADS * RET_DK
    w_c = w_in_c[o].astype(BF16)
    cos_r, sin_r = _rope_tables_ret()
    q_r, k_r, v_r, g_r = _proj_c(x, mods, cos_r, sin_r, w_c[:, :qk], w_c[:, qk:2 * qk],
                                 w_c[:, 2 * qk:2 * qk + MIX_C], w_c[:, 2 * qk + MIX_C:])
    gam = jnp.broadcast_to(ret_gamma_logit[o].astype(F32)[:, :, None, None], (2, RET_HEADS, SUBLANE, LANE))
    r0_p = jnp.zeros((BATCH, 2, RET_HEADS, RET_DK, RET_DV), F32)
    o_ret_p, r_fin = _retention(q_r, k_r, v_r, gam, r0_p, n_seq=BATCH, seq=SEQ, row_block0=0, with_state=True)
    (o_ret_s,) = _retention(q_r, k_r, v_r, gam, state_ret[:, o],
                            n_seq=DEC_BATCH, seq=DEC_SEQ, row_block0=N_PROMPT // DEC_SEQ, with_state=False)
    x1, h2, lgt = _out_c(o_ret_p, o_ret_s, g_r, w_out_c[o].astype(BF16), x, mods,
                         ln_g[l, 0].reshape(1, D_MODEL), ln_b[l, 0].reshape(1, D_MODEL), wr_hi[l], wr_lo[l])
    y_p, y_s = _moe_and_norm(x1, h2, lgt, mods, ln_g[l, 1].reshape(1, D_MODEL), ln_b[l, 1].reshape(1, D_MODEL),
                             router_bias[l], tri, wg_e, wu_e, wd_e, wsg[l], wsu[l], wsd[l], layer=l, split=True)

    y_prompt = y_p.reshape(BATCH, SEQ, D_MODEL)
    y_sample = y_s.reshape(DEC_BATCH, DEC_SEQ, D_MODEL)
    new_ret = r_fin.reshape(BATCH, 1, 2, RET_HEADS, RET_DK, RET_DV)
    return (y_prompt, y_sample, new_ckv, new_krope, new_rg, new_ret)
```

```python
import functools
import math

import jax
import jax.numpy as jnp
from jax import lax
from jax.experimental import pallas as pl
from jax.experimental.pallas import tpu as pltpu

F32 = jnp.float32
BF16 = jnp.bfloat16
I32 = jnp.int32

D_MODEL = 1024
BATCH, SEQ = 16, 256
DEC_BATCH, DEC_SEQ = 8, 4096
PAST_LEN = 256
DEPTH = 2
GRID_W = 64
RG_W, RG_BLOCKS = 512, 8
RG_BW = RG_W // RG_BLOCKS
RG_C = 8.0
CONV_W, CONV_LEFT = 4, 2
MLA_HEADS, QK_NOPE, QK_ROPE, V_HEAD = 8, 64, 32, 64
Q_LORA, KV_LORA = 768, 256
ROPE_BASE = 10000.0
ATTN_SCALE = (QK_NOPE + QK_ROPE) ** -0.5
RET_HEADS, RET_DK, RET_DV, RET_CHUNK = 4, 256, 512, 128
MIX_C = RET_HEADS * RET_DV
N_EXPERTS, TOP_K, N_GROUPS, TOPK_GROUPS = 64, 8, 8, 4
GROUP_SIZE = N_EXPERTS // N_GROUPS
D_EXPERT = 256
ROUTED_SCALE = 2.5
ALPHA = (2 * DEPTH) ** 0.25
EPS = 1e-6

N_PROMPT = BATCH * SEQ
N_SAMPLE = DEC_BATCH * DEC_SEQ
N_TOK = N_PROMPT + N_SAMPLE
N_COND = 1 + DEC_BATCH
MOD_ROWS = 8

LANE = 128
SUBLANE = 8
TM = 512
HEAD_PAD = 128
ROPE_LANE0 = QK_NOPE
TQ = 256
KC = 256
AH = 2
RET_UNROLL = 8
V_ONES = 16
Q_PRESCALE = ATTN_SCALE * math.log2(math.e)
SCAN_ROWS = 64
GATE_ROWS = 256
WIN = 256
N_WIN = N_TOK // WIN
PIECE = 16
SORT_ROWS = 256
RL = 3072
TME = 1024
PIECES = TME // PIECE
EXPERT_BUFS = 3
SPARE_BLOCKS = 2
SPARE_WIN = 2
N_PAIRS = N_TOK * TOP_K
N_EBLOCKS = (N_PAIRS + N_WIN * N_EXPERTS * (PIECE - 1)) // TME + N_EXPERTS
NEG = -1e30


def _cparams(sem, vmem_mb=48):
    return pltpu.CompilerParams(dimension_semantics=sem, vmem_limit_bytes=vmem_mb * 1024 * 1024)


def _cond_block(i, tm):
    npb = N_PROMPT // tm
    return jnp.where(i < npb, 0, 1 + (i - npb) // (DEC_SEQ // tm))


def _pos_block(i, tm):
    npb = N_PROMPT // tm
    return jnp.where(i < npb, 0, 1 + (i - npb) % (DEC_SEQ // tm))


def _split_hi_lo(a):
    hi = a.astype(BF16)
    lo = (a - hi.astype(F32)).astype(BF16)
    return hi, lo


def _dot(a, b):
    return jnp.dot(a, b, preferred_element_type=F32)


def _dot_nt(a, b):
    return lax.dot_general(a, b, (((1,), (1,)), ((), ())), preferred_element_type=F32)


def _silu(x):
    return x * jax.nn.sigmoid(x)


def _gelu_tanh(x):
    return 0.5 * x * (1.0 + jnp.tanh(math.sqrt(2.0 / math.pi) * (x + 0.044715 * (x * x * x))))


def _softplus(x):
    return jnp.maximum(x, 0.0) + jnp.log1p(jnp.exp(-jnp.abs(x)))


def _layernorm_rows(z, g, b):
    mu = jnp.mean(z, axis=-1, keepdims=True)
    zc = z - mu
    var = jnp.mean(zc * zc, axis=-1, keepdims=True)
    return (zc * lax.rsqrt(var + EPS)) * g + b


def _ada_kernel(c_ref, w_ref, b_ref, o_ref):
    s_hi, s_lo = _split_hi_lo(_silu(c_ref[...]))
    w_hi, w_lo = _split_hi_lo(w_ref[...])
    o_ref[...] = _dot(s_hi, w_hi) + _dot(s_hi, w_lo) + _dot(s_lo, w_hi) + b_ref[...]


def _ada_modulation(cond, w_ada, b_ada):
    n6 = 6 * D_MODEL
    tn = D_MODEL
    return pl.pallas_call(
        _ada_kernel,
        out_shape=jax.ShapeDtypeStruct((DEPTH, 16, n6), F32),
        grid=(DEPTH, n6 // tn),
        in_specs=[
            pl.BlockSpec((16, D_MODEL), lambda l, j: (0, 0)),
            pl.BlockSpec((None, D_MODEL, tn), lambda l, j: (l, 0, j)),
            pl.BlockSpec((None, 1, tn), lambda l, j: (l, 0, j)),
        ],
        out_specs=pl.BlockSpec((None, 16, tn), lambda l, j: (l, 0, j)),
        compiler_params=_cparams(("arbitrary", "arbitrary")),
        name="ada_modulation",
    )(cond, w_ada, b_ada.reshape(DEPTH, 1, n6))


def _modulated(x, mod_ref, shift_row, scale_row):
    return x * (1.0 + mod_ref[scale_row:scale_row + 1, :]) + mod_ref[shift_row:shift_row + 1, :]


def _row_pair_specs(tm, width, col=0):
    npb = N_PROMPT // tm
    return [pl.BlockSpec((tm, width), lambda i, *_: (jnp.minimum(i, npb - 1), col)),
            pl.BlockSpec((tm, width), lambda i, *_: (jnp.maximum(i - npb, 0), col))]


def _pick_rows(p_ref, s_ref, tm):
    return jnp.where(pl.program_id(0) < N_PROMPT // tm, p_ref[...], s_ref[...])


def _proj_ab_kernel(xp_ref, xs_ref, mod_ref, w_ref, wkr_ref, main_ref, kr_ref):
    h = _modulated(_pick_rows(xp_ref, xs_ref, TM), mod_ref, 0, 1).astype(BF16)
    n = w_ref.shape[1]
    step = 512
    for j in range(n // step):
        main_ref[:, j * step:(j + 1) * step] = _dot(h, w_ref[:, j * step:(j + 1) * step])
    kr_ref[...] = _dot(h, wkr_ref[...])


def _proj_ab(xp, xs, mods, w_main, w_kr):
    n = w_main.shape[1]
    return pl.pallas_call(
        _proj_ab_kernel,
        out_shape=(jax.ShapeDtypeStruct((N_TOK, n), F32), jax.ShapeDtypeStruct((N_TOK, LANE), F32)),
        grid=(N_TOK // TM,),
        in_specs=_row_pair_specs(TM, D_MODEL) + [
            pl.BlockSpec((None, MOD_ROWS, D_MODEL), lambda i: (_cond_block(i, TM), 0, 0)),
            pl.BlockSpec((D_MODEL, n), lambda i: (0, 0)),
            pl.BlockSpec((D_MODEL, LANE), lambda i: (0, 0)),
        ],
        out_specs=(pl.BlockSpec((TM, n), lambda i: (i, 0)), pl.BlockSpec((TM, LANE), lambda i: (i, 0))),
        compiler_params=_cparams(("arbitrary",)),
        name="proj_ab",
    )(xp, xs, mods, w_main, w_kr)


def _rglru_kernel(xr_ref, gr_ref, cw_ref, cb_ref, wg_ref, bg_ref, lam_ref, h0_ref,
                  y_ref, hfin_ref, xpad, a_s, b_s, *, seq):
    pad = SUBLANE
    xpad[0:pad, :] = jnp.zeros((pad, LANE), F32)
    xpad[seq + pad:seq + 2 * pad, :] = jnp.zeros((pad, LANE), F32)
    xpad[pad:seq + pad, :] = xr_ref[...]

    sp = _softplus(-lam_ref[...])
    cw = cw_ref[...]
    cb = cb_ref[...]
    wg = wg_ref[...]
    bg = bg_ref[...]

    def gate_step(c, carry):
        t0 = pl.multiple_of(c * GATE_ROWS, GATE_ROWS)
        win = xpad[pl.ds(t0, GATE_ROWS + 2 * pad), :]
        xc = cb
        for j in range(CONV_W):
            off = pad - CONV_LEFT + j
            xc = xc + win[off:off + GATE_ROWS, :] * cw[j:j + 1, :]
        g = _dot(xc.astype(BF16), wg) + bg
        for d in range(2):
            r = jax.nn.sigmoid(g[:, (2 * d) * LANE:(2 * d + 1) * LANE])
            i = jax.nn.sigmoid(g[:, (2 * d + 1) * LANE:(2 * d + 2) * LANE])
            log_a = (-RG_C * r) * sp[d:d + 1, :]
            a = jnp.exp(log_a)
            t = jnp.tanh(log_a)
            bt = jnp.sqrt(2.0 * t / (t - 1.0)) * (i * xc)
            a_s[d, pl.ds(t0, GATE_ROWS), :] = a
            b_s[d, pl.ds(t0, GATE_ROWS), :] = bt
        return carry

    lax.fori_loop(0, seq // GATE_ROWS, gate_step, 0, unroll=min(2, seq // GATE_ROWS))

    row = lax.broadcasted_iota(I32, (SCAN_ROWS, LANE), 0) % SUBLANE
    n_steps = seq // SCAN_ROWS
    tiles = SCAN_ROWS // SUBLANE

    def local_scan(a, b, reverse):
        for k in (1, 2, 4):
            if reverse:
                ok = row < SUBLANE - k
                shift = SCAN_ROWS - k
            else:
                ok = row >= k
                shift = k
            a_sh = jnp.where(ok, pltpu.roll(a, shift, 0), 1.0)
            b_sh = jnp.where(ok, pltpu.roll(b, shift, 0), 0.0)
            b = a * b_sh + b
            a = a * a_sh
        return a, b

    def fwd_step(c, h):
        t0 = pl.multiple_of(c * SCAN_ROWS, SCAN_ROWS)
        a, b = local_scan(a_s[0, pl.ds(t0, SCAN_ROWS), :], b_s[0, pl.ds(t0, SCAN_ROWS), :], False)
        outs = []
        for j in range(tiles):
            hj = a[j * SUBLANE:(j + 1) * SUBLANE, :] * h + b[j * SUBLANE:(j + 1) * SUBLANE, :]
            outs.append(hj)
            h = hj[SUBLANE - 1:SUBLANE, :]
        y_ref[pl.ds(t0, SCAN_ROWS), :] = jnp.concatenate(outs, axis=0)
        return h

    h_f = lax.fori_loop(0, n_steps, fwd_step, h0_ref[0:1, :], unroll=4)

    def bwd_step(c, h):
        t0 = pl.multiple_of((n_steps - 1 - c) * SCAN_ROWS, SCAN_ROWS)
        a, b = local_scan(a_s[1, pl.ds(t0, SCAN_ROWS), :], b_s[1, pl.ds(t0, SCAN_ROWS), :], True)
        outs = [None] * tiles
        for j in reversed(range(tiles)):
            hj = a[j * SUBLANE:(j + 1) * SUBLANE, :] * h + b[j * SUBLANE:(j + 1) * SUBLANE, :]
            outs[j] = hj
            h = hj[0:1, :]
        hb = jnp.concatenate(outs, axis=0)
        y_ref[pl.ds(t0, SCAN_ROWS), :] = (y_ref[pl.ds(t0, SCAN_ROWS), :] + hb) * _gelu_tanh(gr_ref[pl.ds(t0, SCAN_ROWS), :])
        return h

    h_b = lax.fori_loop(0, n_steps, bwd_step, h0_ref[1:2, :], unroll=4)
    hfin_ref[0:1, :] = h_f
    hfin_ref[1:2, :] = h_b


def _rglru(main, cw, cb, wg, bg, lam, h0, *, n_seq, seq, row_block0):
    n_ct = RG_W // LANE
    gr_col0 = RG_W // LANE
    kern = functools.partial(_rglru_kernel, seq=seq)
    return pl.pallas_call(
        kern,
        out_shape=(jax.ShapeDtypeStruct((n_seq * seq, RG_W), F32), jax.ShapeDtypeStruct((n_seq, 2, RG_W), F32)),
        grid=(n_seq, n_ct),
        in_specs=[
            pl.BlockSpec((seq, LANE), lambda b, c: (row_block0 + b, c)),
            pl.BlockSpec((seq, LANE), lambda b, c: (row_block0 + b, gr_col0 + c)),
            pl.BlockSpec((CONV_W, LANE), lambda b, c: (0, c)),
            pl.BlockSpec((1, LANE), lambda b, c: (0, c)),
            pl.BlockSpec((None, LANE, 4 * LANE), lambda b, c: (c, 0, 0)),
            pl.BlockSpec((None, 1, 4 * LANE), lambda b, c: (c, 0, 0)),
            pl.BlockSpec((2, LANE), lambda b, c: (0, c)),
            pl.BlockSpec((None, 2, LANE), lambda b, c: (b, 0, c)),
        ],
        out_specs=(
            pl.BlockSpec((seq, LANE), lambda b, c: (b, c)),
            pl.BlockSpec((None, 2, LANE), lambda b, c: (b, 0, c)),
        ),
        scratch_shapes=[
            pltpu.VMEM((seq + 2 * SUBLANE, LANE), F32),
            pltpu.VMEM((2, seq, LANE), F32),
            pltpu.VMEM((2, seq, LANE), F32),
        ],
        compiler_params=_cparams(("arbitrary", "arbitrary")),
        name=f"rglru_s{seq}",
    )(main, main, cw, cb, wg, bg, lam, h0)


def _rope_lanes(x, cos, sin_m, sin_p):
    n = x.shape[1] // LANE
    half = QK_ROPE // 4
    cos_t = jnp.concatenate([cos] * n, axis=1) if n > 1 else cos
    sm_t = jnp.concatenate([sin_m] * n, axis=1) if n > 1 else sin_m
    sp_t = jnp.concatenate([sin_p] * n, axis=1) if n > 1 else sin_p
    up = pltpu.roll(x, x.shape[1] - half, 1)
    dn = pltpu.roll(x, half, 1)
    return x * cos_t + up * sm_t + dn * sp_t


def _rope_sublanes(xt, cos_t, sin_m_t, sin_p_t):
    n = xt.shape[0] // LANE
    half = QK_ROPE // 4
    cos_a = jnp.concatenate([cos_t] * n, axis=0)
    sm_a = jnp.concatenate([sin_m_t] * n, axis=0)
    sp_a = jnp.concatenate([sin_p_t] * n, axis=0)
    up = pltpu.roll(xt, xt.shape[0] - half, 0)
    dn = pltpu.roll(xt, half, 0)
    return xt * cos_a + up * sm_a + dn * sp_a


def _mla_prep_kernel(cq0_ref, cq1_ref, cq2_ref, ckv_ref, kr_ref, cos_ref, sm_ref, sp_ref,
                     cost_ref, smt_ref, spt_ref, qn_ref, wq_ref, kvn_ref, wuk_ref, wuvt_ref,
                     q_ref, k_ref, vt_ref, ckvn_ref):
    cq = [cq0_ref[...], cq1_ref[...], cq2_ref[...]]
    ms = (jnp.sum(cq[0] * cq[0], axis=-1, keepdims=True) + jnp.sum(cq[1] * cq[1], axis=-1, keepdims=True)
          + jnp.sum(cq[2] * cq[2], axis=-1, keepdims=True)) * (1.0 / Q_LORA)
    inv = lax.rsqrt(ms + EPS)
    blk = Q_LORA // 3
    q = None
    for j in range(3):
        cqn = ((cq[j] * inv) * qn_ref[:, j * blk:(j + 1) * blk]).astype(BF16)
        part = _dot(cqn, wq_ref[j * blk:(j + 1) * blk, :])
        q = part if q is None else q + part
    cos, sm, sp = cos_ref[...], sm_ref[...], sp_ref[...]
    q_t = _rope_sublanes(q.T, cost_ref[...], smt_ref[...], spt_ref[...])
    q_ref[...] = (q_t * Q_PRESCALE).astype(BF16)

    ckv = ckv_ref[...]
    inv_kv = lax.rsqrt(jnp.mean(ckv * ckv, axis=-1, keepdims=True) + EPS)
    ckvn = (ckv * inv_kv) * kvn_ref[...]

    @pl.when(pl.program_id(0) < N_PROMPT // TM)
    def _():
        ckvn_ref[...] = ckvn

    ckvn_b = ckvn.astype(BF16)
    kr_rot = _rope_lanes(kr_ref[...], cos, sm, sp)
    k_ref[...] = (_dot(ckvn_b, wuk_ref[...]) + jnp.concatenate([kr_rot] * MLA_HEADS, axis=1)).astype(BF16)
    vt = _dot_nt(wuvt_ref[...], ckvn_b).astype(BF16)
    for c in range(vt_ref.shape[0]):
        vt_ref[c] = vt[:, c * KC:(c + 1) * KC]


def _mla_prep(main, krp, cos_t, sm_t, sp_t, q_norm, wq_p, kv_norm, wuk_p, wuvt):
    cq_col0 = 2 * RG_W // 256
    hp = MLA_HEADS * HEAD_PAD
    full = lambda shape: pl.BlockSpec(shape, lambda i: (0,) * len(shape))
    tab = pl.BlockSpec((TM, LANE), lambda i: (_pos_block(i, TM), 0))
    tab_t = pl.BlockSpec((LANE, TM), lambda i: (0, _pos_block(i, TM)))
    return pl.pallas_call(
        _mla_prep_kernel,
        out_shape=(
            jax.ShapeDtypeStruct((hp, N_TOK), BF16),
            jax.ShapeDtypeStruct((N_TOK, hp), BF16),
            jax.ShapeDtypeStruct((N_TOK // KC, MLA_HEADS * V_HEAD, KC), BF16),
            jax.ShapeDtypeStruct((N_PROMPT, KV_LORA), F32),
        ),
        grid=(N_TOK // TM,),
        in_specs=[
            pl.BlockSpec((TM, 256), lambda i: (i, cq_col0)),
            pl.BlockSpec((TM, 256), lambda i: (i, cq_col0 + 1)),
            pl.BlockSpec((TM, 256), lambda i: (i, cq_col0 + 2)),
            pl.BlockSpec((TM, 256), lambda i: (i, cq_col0 + 3)),
            pl.BlockSpec((TM, LANE), lambda i: (i, 0)),
            tab, tab, tab, tab_t, tab_t, tab_t,
            full((1, Q_LORA)), full((Q_LORA, hp)), full((1, KV_LORA)), full((KV_LORA, hp)),
            full((MLA_HEADS * V_HEAD, KV_LORA)),
        ],
        out_specs=(
            pl.BlockSpec((hp, TM), lambda i: (0, i)),
            pl.BlockSpec((TM, hp), lambda i: (i, 0)),
            pl.BlockSpec((TM // KC, MLA_HEADS * V_HEAD, KC), lambda i: (i, 0, 0)),
            pl.BlockSpec((TM, KV_LORA), lambda i: (jnp.minimum(i, N_PROMPT // TM - 1), 0)),
        ),
        compiler_params=_cparams(("arbitrary",)),
        name="mla_prep",
    )(main, main, main, main, krp, cos_t, sm_t, sp_t, cos_t.T, sm_t.T, sp_t.T, q_norm, wq_p, kv_norm, wuk_p, wuvt)


def _mla_ctx_kernel(ckv_ref, kr_ref, wuk_ref, wuvt_ref, k_ref, vt_ref):
    ckv_b = ckv_ref[...].astype(BF16)
    k_ref[...] = (_dot(ckv_b, wuk_ref[...]) + jnp.concatenate([kr_ref[...]] * MLA_HEADS, axis=1)).astype(BF16)
    vt_ref[...] = _dot_nt(wuvt_ref[...], ckv_b).astype(BF16)


def _mla_ctx(ctx_ckv, ctx_krp, wuk_p, wuvt):
    n = ctx_ckv.shape[0]
    hp = MLA_HEADS * HEAD_PAD
    tm = KC
    full = lambda shape: pl.BlockSpec(shape, lambda i: (0,) * len(shape))
    return pl.pallas_call(
        _mla_ctx_kernel,
        out_shape=(jax.ShapeDtypeStruct((n, hp), BF16),
                   jax.ShapeDtypeStruct((n // tm, MLA_HEADS * V_HEAD, tm), BF16)),
        grid=(n // tm,),
        in_specs=[
            pl.BlockSpec((tm, KV_LORA), lambda i: (i, 0)),
            pl.BlockSpec((tm, LANE), lambda i: (i, 0)),
            full((KV_LORA, hp)), full((MLA_HEADS * V_HEAD, KV_LORA)),
        ],
        out_specs=(pl.BlockSpec((tm, hp), lambda i: (i, 0)),
                   pl.BlockSpec((None, MLA_HEADS * V_HEAD, tm), lambda i: (i, 0, 0))),
        compiler_params=_cparams(("arbitrary",)),
        name="mla_ctx",
    )(ctx_ckv, ctx_krp, wuk_p, wuvt)


def _attn_kernel(*refs, seq, tq, n_ctx):
    if n_ctx:
        q_ref, k_ref, vt_ref, kc_ref, vtc_ref, o_ref, s_scr, p_scr, k_all, vt_all = refs
    else:
        q_ref, k_ref, vt_ref, o_ref, s_scr, p_scr, k_all, vt_all = refs
    has_ctx = 1 if n_ctx else 0
    n_own = seq // KC
    n = n_own + has_ctx
    qs = [q_ref[h * HEAD_PAD:(h + 1) * HEAD_PAD, :] for h in range(AH)]

    @pl.when(pl.program_id(2) == 0)
    def _():
        k_all[0:seq, :] = k_ref[...]
        if has_ctx:
            k_all[seq:seq + KC, :] = kc_ref[...]
        for h in range(AH):
            vt_all[0:n_own, h, 0:V_HEAD, :] = vt_ref[:, h * V_HEAD:(h + 1) * V_HEAD, :]
            if has_ctx:
                vt_all[n_own, h, 0:V_HEAD, :] = vtc_ref[h * V_HEAD:(h + 1) * V_HEAD, :]
            vt_all[:, h, V_HEAD:V_HEAD + V_ONES, :] = jnp.ones((n, V_ONES, KC), BF16)

    def scores(c, slot):
        for h in range(AH):
            s_scr[slot, h] = _dot(k_all[c * KC:(c + 1) * KC, h * HEAD_PAD:(h + 1) * HEAD_PAD], qs[h])

    def softmax_chunk(slot, st):
        out = []
        for h in range(AH):
            m, a1, _, acc = st[h]
            t = s_scr[slot, h]
            m_new = jnp.maximum(m, jnp.max(t, axis=0, keepdims=True))
            p_scr[slot, h] = jnp.exp2(t - m_new).astype(BF16)
            out.append((m_new, jnp.exp2(m - m_new), a1, acc))
        return out

    def weighted_values(c, slot, st, alphas):
        return [st[h][:3] + (alphas[h] * st[h][3] + _dot(vt_all[c, h], p_scr[slot, h]),) for h in range(AH)]

    one = jnp.ones((1, tq), F32)
    st = [(jnp.full((1, tq), NEG, F32), one, one, jnp.zeros((V_HEAD + V_ONES, tq), F32)) for _ in range(AH)]
    scores(0, 0)
    if n > 1:
        scores(1, 1)
    for c in range(n):
        if c + 2 < n:
            scores(c + 2, (c + 2) % 3)
        alphas = [st[h][2] for h in range(AH)]
        st = softmax_chunk(c % 3, st)
        if c >= 2:
            st = weighted_values(c - 2, (c - 2) % 3, st, alphas)
    if n > 1:
        st = weighted_values(n - 2, (n - 2) % 3, st, [st[h][2] for h in range(AH)])
    st = weighted_values(n - 1, (n - 1) % 3, st, [st[h][1] for h in range(AH)])
    for h in range(AH):
        acc = st[h][3]
        o_ref[:, h * V_HEAD:(h + 1) * V_HEAD] = (acc[0:V_HEAD] / acc[V_HEAD:V_HEAD + 1]).T.astype(o_ref.dtype)


def _attention(q, k, vt, kc, vtc, *, n_seq, seq, row_block0, tq):
    n_ctx = 0 if kc is None else PAST_LEN
    n_hp = MLA_HEADS // AH
    nq = seq // tq
    kern = functools.partial(_attn_kernel, seq=seq, tq=tq, n_ctx=n_ctx)
    in_specs = [
        pl.BlockSpec((AH * HEAD_PAD, tq), lambda b, j, i: (j, (row_block0 + b) * nq + i)),
        pl.BlockSpec((seq, AH * HEAD_PAD), lambda b, j, i: (row_block0 + b, j)),
        pl.BlockSpec((seq // KC, AH * V_HEAD, KC), lambda b, j, i: (row_block0 + b, j, 0)),
    ]
    args = [q, k, vt]
    if n_ctx:
        in_specs += [
            pl.BlockSpec((n_ctx, AH * HEAD_PAD), lambda b, j, i: (b, j)),
            pl.BlockSpec((None, AH * V_HEAD, KC), lambda b, j, i: (b, j, 0)),
        ]
        args += [kc, vtc]
    return pl.pallas_call(
        kern,
        out_shape=jax.ShapeDtypeStruct((n_seq * seq, MLA_HEADS * V_HEAD), BF16),
        grid=(n_seq, n_hp, nq),
        in_specs=in_specs,
        out_specs=pl.BlockSpec((tq, AH * V_HEAD), lambda b, j, i: (b * nq + i, j)),
        scratch_shapes=[pltpu.VMEM((3, AH, KC, tq), F32), pltpu.VMEM((3, AH, KC, tq), BF16),
                        pltpu.VMEM((seq + n_ctx, AH * HEAD_PAD), BF16),
                        pltpu.VMEM(((seq + n_ctx) // KC, AH, V_HEAD + V_ONES, KC), BF16)],
        compiler_params=_cparams(("arbitrary", "arbitrary", "arbitrary")),
        name=f"mla_attention_s{seq}",
    )(*args)


def _post_mixer(y, x, mod_ref, lng_ref, lnb_ref, wrh_ref, wrl_ref, x1_ref, h2_ref, lgt_ref):
    z = ALPHA * x + mod_ref[2:3, :] * y
    x1 = _layernorm_rows(z, lng_ref[...], lnb_ref[...])
    x1_ref[...] = x1
    h2 = _modulated(x1, mod_ref, 3, 4)
    h_hi, h_lo = _split_hi_lo(h2)
    h2_ref[...] = h_hi
    w_hi, w_lo = wrh_ref[...], wrl_ref[...]
    lgt_ref[...] = _dot_nt(w_hi, h_hi) + _dot_nt(w_hi, h_lo) + _dot_nt(w_lo, h_hi)


def _out_ab_kernel(yrgp_ref, yrgs_ref, op_ref, os_ref, wa_ref, wb_ref, xp_ref, xs_ref,
                   mod_ref, lng_ref, lnb_ref, wrh_ref, wrl_ref, x1_ref, h2_ref, lgt_ref):
    y = (_dot(_pick_rows(yrgp_ref, yrgs_ref, TM).astype(BF16), wa_ref[...])
         + _dot(_pick_rows(op_ref, os_ref, TM), wb_ref[...]))
    _post_mixer(y, _pick_rows(xp_ref, xs_ref, TM), mod_ref, lng_ref, lnb_ref, wrh_ref, wrl_ref,
                x1_ref, h2_ref, lgt_ref)


def _post_specs():
    full = lambda shape: pl.BlockSpec(shape, lambda i: (0,) * len(shape))
    in_specs = [
        pl.BlockSpec((None, MOD_ROWS, D_MODEL), lambda i: (_cond_block(i, TM), 0, 0)),
        full((1, D_MODEL)), full((1, D_MODEL)),
        full((N_EXPERTS, D_MODEL)), full((N_EXPERTS, D_MODEL)),
    ]
    out_shape = (
        jax.ShapeDtypeStruct((N_TOK, D_MODEL), F32),
        jax.ShapeDtypeStruct((N_TOK, D_MODEL), BF16),
        jax.ShapeDtypeStruct((N_EXPERTS, N_TOK), F32),
    )
    out_specs = (
        pl.BlockSpec((TM, D_MODEL), lambda i: (i, 0)),
        pl.BlockSpec((TM, D_MODEL), lambda i: (i, 0)),
        pl.BlockSpec((N_EXPERTS, TM), lambda i: (0, i)),
    )
    return in_specs, out_shape, out_specs


def _out_ab(yrg_p, yrg_s, o_p, o_s, w_a, w_b, xp, xs, mods, lng, lnb, wr_hi, wr_lo):
    full = lambda shape: pl.BlockSpec(shape, lambda i: (0,) * len(shape))
    post_in, out_shape, out_specs = _post_specs()
    return pl.pallas_call(
        _out_ab_kernel,
        out_shape=out_shape,
        grid=(N_TOK // TM,),
        in_specs=(_row_pair_specs(TM, RG_W) + _row_pair_specs(TM, MLA_HEADS * V_HEAD)
                  + [full((RG_W, D_MODEL)), full((MLA_HEADS * V_HEAD, D_MODEL))]
                  + _row_pair_specs(TM, D_MODEL) + post_in),
        out_specs=out_specs,
        compiler_params=_cparams(("arbitrary",)),
        name="out_ab",
    )(yrg_p, yrg_s, o_p, o_s, w_a, w_b, xp, xs, mods, lng, lnb, wr_hi, wr_lo)


def _proj_c_kernel(x_ref, mod_ref, cos_ref, sin_ref, wq_ref, wk_ref, wv_ref, wg_ref, q_ref, k_ref, v_ref, g_ref):
    h = _modulated(x_ref[...], mod_ref, 0, 1).astype(BF16)
    cos, sin = cos_ref[...], sin_ref[...]
    half = RET_DK // 2
    for hd in range(RET_HEADS):
        for w_ref, is_k in ((wq_ref, False), (wk_ref, True)):
            p = _dot(h, w_ref[:, hd * RET_DK:(hd + 1) * RET_DK])
            x1, x2 = p[:, :half], p[:, half:]
            r1 = x1 * cos - x2 * sin
            r2 = x2 * cos + x1 * sin
            if not is_k:
                q_ref[:, hd * RET_DK:hd * RET_DK + half] = r1.astype(BF16)
                q_ref[:, hd * RET_DK + half:(hd + 1) * RET_DK] = r2.astype(BF16)
            else:
                t1 = (r1 * RET_DK ** -0.5).T.astype(BF16)
                t2 = (r2 * RET_DK ** -0.5).T.astype(BF16)
                for c in range(k_ref.shape[0]):
                    k_ref[c, hd * RET_DK:hd * RET_DK + half, :] = t1[:, c * RET_CHUNK:(c + 1) * RET_CHUNK]
                    k_ref[c, hd * RET_DK + half:(hd + 1) * RET_DK, :] = t2[:, c * RET_CHUNK:(c + 1) * RET_CHUNK]
    step = 512
    for j in range(MIX_C // step):
        v_ref[:, j * step:(j + 1) * step] = _dot(h, wv_ref[:, j * step:(j + 1) * step]).astype(BF16)
        g_ref[:, j * step:(j + 1) * step] = _dot(h, wg_ref[:, j * step:(j + 1) * step])


def _proj_c(x, mods, cos_t, sin_t, wq, wk, wv, wg):
    full = lambda shape: pl.BlockSpec(shape, lambda i: (0,) * len(shape))
    qk = RET_HEADS * RET_DK
    tab = pl.BlockSpec((TM, RET_DK // 2), lambda i: (_pos_block(i, TM), 0))
    return pl.pallas_call(
        _proj_c_kernel,
        out_shape=(
            jax.ShapeDtypeStruct((N_TOK, qk), BF16), jax.ShapeDtypeStruct((N_TOK // RET_CHUNK, qk, RET_CHUNK), BF16),
            jax.ShapeDtypeStruct((N_TOK, MIX_C), BF16), jax.ShapeDtypeStruct((N_TOK, MIX_C), F32),
        ),
        grid=(N_TOK // TM,),
        in_specs=[
            pl.BlockSpec((TM, D_MODEL), lambda i: (i, 0)),
            pl.BlockSpec((None, MOD_ROWS, D_MODEL), lambda i: (_cond_block(i, TM), 0, 0)),
            tab, tab,
            full((D_MODEL, qk)), full((D_MODEL, qk)), full((D_MODEL, MIX_C)), full((D_MODEL, MIX_C)),
        ],
        out_specs=(
            pl.BlockSpec((TM, qk), lambda i: (i, 0)),
            pl.BlockSpec((TM // RET_CHUNK, qk, RET_CHUNK), lambda i: (i, 0, 0)),
            pl.BlockSpec((TM, MIX_C), lambda i: (i, 0)), pl.BlockSpec((TM, MIX_C), lambda i: (i, 0)),
        ),
        compiler_params=_cparams(("arbitrary",), 56),
        name="proj_c",
    )(x, mods, cos_t, sin_t, wq, wk, wv, wg)


def _retention_kernel(*refs, seq, with_state):
    if with_state:
        q_ref, kt_ref, v_ref, gam_ref, r0_ref, o_ref, rfin_ref, r_s = refs
    else:
        q_ref, kt_ref, v_ref, gam_ref, r0_ref, o_ref, r_s = refs
        rfin_ref = None
    c = RET_CHUNK
    n = seq // c
    ii = lax.broadcasted_iota(I32, (c, c), 0).astype(F32)
    jj = lax.broadcasted_iota(I32, (c, c), 1).astype(F32)
    ci = lax.broadcasted_iota(I32, (c, 1), 0).astype(F32)
    li = lax.broadcasted_iota(I32, (1, c), 1).astype(F32)

    consts = []
    for d in range(2):
        gam = gam_ref[d]
        lg_row = -_softplus(-gam[0:1, :])
        lg = jnp.broadcast_to(lg_row, (c, c))
        lg_col = jnp.broadcast_to(lg_row[:, 0:1], (c, 1))
        if d == 0:
            diff = ii - jj
            xi = jnp.exp((ci + 1.0) * lg_col)
            zeta = jnp.exp((c - 1.0 - li) * lg_row)
        else:
            diff = jj - ii
            xi = jnp.exp((c - ci) * lg_col)
            zeta = jnp.exp(li * lg_row)
        dmat = jnp.where(diff >= 0, jnp.exp(jnp.maximum(diff, 0.0) * lg), 0.0)
        g_chunk = jnp.exp(float(c) * lg_row[:, 0:1])
        consts.append((dmat, xi, zeta, g_chunk))
        r_s[d] = r0_ref[d]

    def chunk(d, idx, accumulate):
        dmat, xi, zeta, g_chunk = consts[d]
        t0 = pl.multiple_of(idx * c, c)
        qb = q_ref[pl.ds(t0, c), :]
        kt = kt_ref[idx]
        vb = v_ref[pl.ds(t0, c), :]
        r = r_s[d]
        inner = _dot(qb, kt) * dmat
        o = _dot(inner.astype(BF16), vb) + _dot((qb.astype(F32) * xi).astype(BF16), r.astype(BF16))
        r_s[d] = r * g_chunk + _dot((kt.astype(F32) * zeta).astype(BF16), vb)
        if accumulate:
            o_ref[pl.ds(t0, c), :] = o_ref[pl.ds(t0, c), :] + o
        else:
            o_ref[pl.ds(t0, c), :] = o

    def first_half(s, carry):
        chunk(0, s, False)
        chunk(1, n - 1 - s, False)
        return carry

    def second_half(s, carry):
        chunk(0, s, True)
        chunk(1, n - 1 - s, True)
        return carry

    unroll = min(RET_UNROLL, n // 2)
    lax.fori_loop(0, n // 2, first_half, 0, unroll=unroll)
    lax.fori_loop(n // 2, n, second_half, 0, unroll=unroll)
    if with_state:
        for d in range(2):
            rfin_ref[d] = r_s[d]


def _retention(q, k, v, gam, r0, *, n_seq, seq, row_block0, with_state):
    kern = functools.partial(_retention_kernel, seq=seq, with_state=with_state)
    out_shape = [jax.ShapeDtypeStruct((n_seq * seq, MIX_C), F32)]
    out_specs = [pl.BlockSpec((seq, RET_DV), lambda b, h: (b, h))]
    if with_state:
        out_shape.append(jax.ShapeDtypeStruct((n_seq, 2, RET_HEADS, RET_DK, RET_DV), F32))
        out_specs.append(pl.BlockSpec((None, 2, None, RET_DK, RET_DV), lambda b, h: (b, 0, h, 0, 0)))
    return pl.pallas_call(
        kern,
        out_shape=tuple(out_shape),
        grid=(n_seq, RET_HEADS),
        in_specs=[
            pl.BlockSpec((seq, RET_DK), lambda b, h: (row_block0 + b, h)),
            pl.BlockSpec((seq // RET_CHUNK, RET_DK, RET_CHUNK), lambda b, h: (row_block0 + b, h, 0)),
            pl.BlockSpec((seq, RET_DV), lambda b, h: (row_block0 + b, h)),
            pl.BlockSpec((2, None, SUBLANE, LANE), lambda b, h: (0, h, 0, 0)),
            pl.BlockSpec((None, 2, None, RET_DK, RET_DV), lambda b, h: (b, 0, h, 0, 0)),
        ],
        out_specs=tuple(out_specs),
        scratch_shapes=[pltpu.VMEM((2, RET_DK, RET_DV), F32)],
        compiler_params=_cparams(("arbitrary", "arbitrary"), 56),
        name=f"retention_s{seq}",
    )(q, k, v, gam, r0)


def _out_c_kernel(op_ref, os_ref, g_ref, w_ref, x_ref, mod_ref, lng_ref, lnb_ref, wrh_ref, wrl_ref,
                  x1_ref, h2_ref, lgt_ref):
    y = None
    o_all = _pick_rows(op_ref, os_ref, TM)
    for hd in range(RET_HEADS):
        o = o_all[:, hd * RET_DV:(hd + 1) * RET_DV]
        mu = jnp.mean(o, axis=-1, keepdims=True)
        oc = o - mu
        var = jnp.mean(oc * oc, axis=-1, keepdims=True)
        on = oc * lax.rsqrt(var + EPS)
        a = (on * _silu(g_ref[:, hd * RET_DV:(hd + 1) * RET_DV])).astype(BF16)
        part = _dot(a, w_ref[hd * RET_DV:(hd + 1) * RET_DV, :])
        y = part if y is None else y + part
    _post_mixer(y, x_ref[...], mod_ref, lng_ref, lnb_ref, wrh_ref, wrl_ref, x1_ref, h2_ref, lgt_ref)


def _out_c(o_p, o_s, g, w, x, mods, lng, lnb, wr_hi, wr_lo):
    full = lambda shape: pl.BlockSpec(shape, lambda i: (0,) * len(shape))
    post_in, out_shape, out_specs = _post_specs()
    return pl.pallas_call(
        _out_c_kernel,
        out_shape=out_shape,
        grid=(N_TOK // TM,),
        in_specs=_row_pair_specs(TM, MIX_C) + [
            pl.BlockSpec((TM, MIX_C), lambda i: (i, 0)),
            full((MIX_C, D_MODEL)),
            pl.BlockSpec((TM, D_MODEL), lambda i: (i, 0)),
        ] + post_in,
        out_specs=out_specs,
        compiler_params=_cparams(("arbitrary",), 56),
        name="out_c",
    )(o_p, o_s, g, w, x, mods, lng, lnb, wr_hi, wr_lo)


def _route_kernel(lgt_ref, bias_ref, tri_ref, w_ref, lpos_ref, p16_ref):
    tt = lgt_ref.shape[1]
    scores = jax.nn.sigmoid(lgt_ref[...])
    sel = scores + bias_ref[...]
    srow = lax.broadcasted_iota(I32, (GROUP_SIZE, tt), 0).astype(F32)
    ninf = -jnp.inf

    gs = []
    for g in range(N_GROUPS):
        sg = sel[g * GROUP_SIZE:(g + 1) * GROUP_SIZE, :]
        m1 = jnp.max(sg, axis=0, keepdims=True)
        i1 = jnp.min(jnp.where(sg == m1, srow, float(GROUP_SIZE)), axis=0, keepdims=True)
        m2 = jnp.max(jnp.where(srow == i1, ninf, sg), axis=0, keepdims=True)
        gs.append(m1 + m2)
    gs = jnp.concatenate(gs, axis=0)
    chosen = jnp.zeros((N_GROUPS, tt), F32)
    for _ in range(TOPK_GROUPS):
        mg = jnp.max(gs, axis=0, keepdims=True)
        gi = jnp.min(jnp.where(gs == mg, srow, float(N_GROUPS)), axis=0, keepdims=True)
        hit = srow == gi
        chosen = jnp.where(hit, 1.0, chosen)
        gs = jnp.where(hit, ninf, gs)
    sel = jnp.concatenate(
        [jnp.where(jnp.broadcast_to(chosen[g:g + 1, :], (GROUP_SIZE, tt)) > 0.5,
                   sel[g * GROUP_SIZE:(g + 1) * GROUP_SIZE, :], ninf) for g in range(N_GROUPS)], axis=0)

    erow = lax.broadcasted_iota(I32, (N_EXPERTS, tt), 0).astype(F32)
    ids, ws = [], []
    for _ in range(TOP_K):
        m = jnp.max(sel, axis=0, keepdims=True)
        ei = jnp.min(jnp.where(sel == m, erow, float(N_EXPERTS)), axis=0, keepdims=True)
        hit = erow == ei
        ids.append(ei)
        ws.append(jnp.sum(jnp.where(hit, scores, 0.0), axis=0, keepdims=True))
        sel = jnp.where(hit, ninf, sel)
    wsum = ws[0]
    for k in range(1, TOP_K):
        wsum = wsum + ws[k]
    w_ref[...] = jnp.concatenate([w / wsum * ROUTED_SCALE for w in ws], axis=0)

    member_f = jnp.zeros((N_EXPERTS, tt), F32)
    for k in range(TOP_K):
        member_f = jnp.where(erow == ids[k], 1.0, member_f)
    member_b = member_f.astype(BF16)
    cnt_row = _dot_nt(jnp.ones((SUBLANE, tt), BF16), member_b)[0:1, :]
    p16_row = jnp.ceil(cnt_row * (1.0 / PIECE)) * PIECE
    lane_e = lax.broadcasted_iota(I32, (N_EXPERTS, N_EXPERTS), 1)
    sub_e = lax.broadcasted_iota(I32, (N_EXPERTS, N_EXPERTS), 0)
    run_start = jnp.sum(jnp.where(lane_e < sub_e, jnp.broadcast_to(p16_row, (N_EXPERTS, N_EXPERTS)), 0.0),
                        axis=1, keepdims=True)
    rank = _dot(member_b, tri_ref[...]) + run_start
    lpos_ref[...] = jnp.concatenate(
        [jnp.sum(jnp.where(erow == ids[k], rank, 0.0), axis=0, keepdims=True) for k in range(TOP_K)],
        axis=0).astype(I32)
    p16_ref[pl.ds(pl.program_id(0), 1), :] = jnp.concatenate(
        [p16_row, jnp.zeros((1, LANE - N_EXPERTS), F32)], axis=1)


def _route(lgt, bias, tri):
    return pl.pallas_call(
        _route_kernel,
        out_shape=(
            jax.ShapeDtypeStruct((TOP_K, N_TOK), F32), jax.ShapeDtypeStruct((TOP_K, N_TOK), I32),
            jax.ShapeDtypeStruct((N_WIN, LANE), F32),
        ),
        grid=(N_WIN,),
        in_specs=[
            pl.BlockSpec((N_EXPERTS, WIN), lambda i: (0, i)),
            pl.BlockSpec((N_EXPERTS, 1), lambda i: (0, 0)),
            pl.BlockSpec((WIN, WIN), lambda i: (0, 0)),
        ],
        out_specs=(
            pl.BlockSpec((TOP_K, WIN), lambda i: (0, i)), pl.BlockSpec((TOP_K, WIN), lambda i: (0, i)),
            pl.BlockSpec((N_WIN, LANE), lambda i: (0, 0)),
        ),
        compiler_params=_cparams(("arbitrary",)),
        name="moe_route",
    )(lgt, bias, tri)


def _sort_kernel(lpos_ref, h_ref, xl_ref, oh_all):
    lp = jnp.where(pl.program_id(0) < N_WIN, lpos_ref[...], -1)
    riota = lax.broadcasted_iota(I32, (SORT_ROWS, WIN), 0).astype(jnp.int16)
    one = jnp.ones((SORT_ROWS, WIN), BF16)
    for j in range(RL // SORT_ROWS):
        rel = (lp - j * SORT_ROWS).astype(jnp.int16)
        oh = jnp.zeros((SORT_ROWS, WIN), BF16)
        for k in range(TOP_K):
            oh = jnp.where(rel[k:k + 1, :] == riota, one, oh)
        oh_all[j * SORT_ROWS:(j + 1) * SORT_ROWS, :] = oh
    x = h_ref[...]
    for nt in range(D_MODEL // SORT_ROWS):
        cols = slice(nt * SORT_ROWS, (nt + 1) * SORT_ROWS)
        xl_ref[:, cols] = _dot(oh_all[...], x[:, cols]).astype(BF16)


def _sort_rows(lpos, h2):
    last = N_WIN - 1
    return pl.pallas_call(
        _sort_kernel,
        out_shape=jax.ShapeDtypeStruct(((N_WIN + SPARE_WIN) * RL, D_MODEL), BF16),
        grid=(N_WIN + SPARE_WIN,),
        in_specs=[
            pl.BlockSpec((TOP_K, WIN), lambda i: (0, jnp.minimum(i, last))),
            pl.BlockSpec((WIN, D_MODEL), lambda i: (jnp.minimum(i, last), 0)),
        ],
        out_specs=pl.BlockSpec((RL, D_MODEL), lambda i: (i, 0)),
        scratch_shapes=[pltpu.VMEM((RL, WIN), BF16)],
        compiler_params=_cparams(("arbitrary",)),
        name="moe_sort",
    )(lpos, h2)


def _moe_tables(p16):
    n_w = jnp.arange(N_WIN, dtype=I32)
    run_start = jnp.cumsum(p16, axis=1) - p16
    rw = jnp.sum(p16, axis=1)
    cum_w = jnp.cumsum(p16, axis=0) - p16
    tot = jnp.sum(p16, axis=0)
    nblk = (tot + TME - 1) // TME
    blk_end = jnp.cumsum(nblk)
    blk0 = blk_end - nblk
    n_used = blk_end[-1]
    b = jnp.arange(N_EBLOCKS + SPARE_BLOCKS, dtype=I32)
    block_e = jnp.minimum(jnp.sum((blk_end[None, :] <= b[:, None]).astype(I32), axis=1), N_EXPERTS - 1)
    piece = jnp.arange(PIECES, dtype=I32) * PIECE
    rp = (b - blk0[block_e])[:, None] * TME + piece[None, :]
    valid = (b[:, None] < n_used) & (rp < tot[block_e][:, None])
    cum_e = cum_w.T[block_e]
    len_e = p16.T[block_e]
    start_e = run_start.T[block_e]
    in_win = (cum_e[:, None, :] <= rp[:, :, None]) & (rp[:, :, None] < (cum_e + len_e)[:, None, :])
    row = n_w[None, None, :] * RL + start_e[:, None, :] + rp[:, :, None] - cum_e[:, None, :]
    row = jnp.sum(jnp.where(in_win, row, 0), axis=2)
    zero_src = N_WIN * RL
    assert (1 + EXPERT_BUFS) * TME <= SPARE_WIN * RL
    trash = N_WIN * RL + (1 + b % EXPERT_BUFS)[:, None] * TME + piece[None, :]
    gather_row = (jnp.where(valid, row, zero_src) // PIECE).reshape(-1).astype(I32)
    scatter_row = (jnp.where(valid, row, trash) // PIECE).reshape(-1).astype(I32)
    rw = jnp.concatenate([rw, jnp.zeros((1,), rw.dtype)])
    return rw.astype(I32), block_e.astype(I32), n_used.astype(I32).reshape(1), gather_row, scatter_row


def _expert_kernel(be_ref, nb_ref, grow_ref, srow_ref, xl_hbm, wg_ref, wu_ref, wd_ref, yl_hbm,
                   xbuf, ybuf, wg_b, wu_b, wd_b, gsem, ssem):
    b = pl.program_id(0)
    nb = nb_ref[0]

    @pl.when(jnp.logical_and(b < nb, jnp.logical_or(b == 0, be_ref[b] != be_ref[jnp.maximum(b - 1, 0)])))
    def _():
        wg_b[...] = wg_ref[...].astype(BF16)
        wu_b[...] = wu_ref[...].astype(BF16)
        wd_b[...] = wd_ref[...].astype(BF16)

    def gather_start(blk, slot):
        for p in range(PIECES):
            pltpu.make_async_copy(xl_hbm.at[grow_ref[blk * PIECES + p]], xbuf.at[slot, p], gsem.at[slot]).start()

    def scatter_start(blk, slot):
        for p in range(PIECES):
            pltpu.make_async_copy(ybuf.at[slot, p], yl_hbm.at[srow_ref[blk * PIECES + p]], ssem.at[slot]).start()

    def gather_wait(slot):
        pltpu.make_async_copy(xl_hbm.at[pl.ds(0, PIECES)], xbuf.at[slot], gsem.at[slot]).wait()

    def scatter_wait(slot):
        pltpu.make_async_copy(ybuf.at[slot], yl_hbm.at[pl.ds(0, PIECES)], ssem.at[slot]).wait()

    @pl.when(b < nb)
    def _():
        slot = b % EXPERT_BUFS
        ahead = (b + 2) % EXPERT_BUFS

        @pl.when(b == 0)
        def _():
            gather_start(0, 0)
            gather_start(1, 1)
            ybuf[2] = jnp.zeros(ybuf.shape[1:], BF16)

        gather_wait(slot)

        @pl.when(b >= 2)
        def _():
            scatter_wait(slot)

        x = xbuf[slot].reshape(TME, D_MODEL)
        hb = _silu(_dot(x, wg_b[...])) * _dot(x, wu_b[...])
        gather_start(b + 2, ahead)
        scatter_start(jnp.where(b == 0, N_EBLOCKS + SPARE_BLOCKS - 1, b - 1), ahead)
        ybuf[slot] = _dot(hb.astype(BF16), wd_b[...]).astype(BF16).reshape(PIECES, PIECE, D_MODEL)

        @pl.when(b == nb - 1)
        def _():
            scatter_start(b, slot)
            gather_wait((b + 1) % EXPERT_BUFS)
            gather_wait(ahead)
            for s in range(EXPERT_BUFS):
                scatter_wait(s)


def _experts(block_e, n_used, gather_row, scatter_row, xl, wg, wu, wd, layer):
    def w_map(i, be, nb, gr, sr):
        return (layer, be[jnp.minimum(i, nb[0] - 1)], 0, 0)

    n_pieces = (N_WIN + SPARE_WIN) * RL // PIECE
    yl = pl.pallas_call(
        _expert_kernel,
        out_shape=jax.ShapeDtypeStruct((n_pieces, PIECE, D_MODEL), BF16),
        input_output_aliases={4: 0},
        grid_spec=pltpu.PrefetchScalarGridSpec(
            num_scalar_prefetch=4,
            grid=(N_EBLOCKS,),
            in_specs=[
                pl.BlockSpec(memory_space=pl.ANY),
                pl.BlockSpec((None, None, D_MODEL, D_EXPERT), w_map),
                pl.BlockSpec((None, None, D_MODEL, D_EXPERT), w_map),
                pl.BlockSpec((None, None, D_EXPERT, D_MODEL), w_map),
            ],
            out_specs=pl.BlockSpec(memory_space=pl.ANY),
            scratch_shapes=[
                pltpu.VMEM((EXPERT_BUFS, PIECES, PIECE, D_MODEL), BF16),
                pltpu.VMEM((EXPERT_BUFS, PIECES, PIECE, D_MODEL), BF16),
                pltpu.VMEM((D_MODEL, D_EXPERT), BF16), pltpu.VMEM((D_MODEL, D_EXPERT), BF16),
                pltpu.VMEM((D_EXPERT, D_MODEL), BF16),
                pltpu.SemaphoreType.DMA((EXPERT_BUFS,)), pltpu.SemaphoreType.DMA((EXPERT_BUFS,)),
            ],
        ),
        compiler_params=_cparams(("arbitrary",)),
        name="moe_experts",
    )(block_e, n_used, gather_row, scatter_row, xl.reshape(n_pieces, PIECE, D_MODEL), wg, wu, wd)
    return yl.reshape(n_pieces * PIECE, D_MODEL)


def _combine_kernel(rw_ref, yl_ref, lpt_ref, wt_ref, h_ref, x1_ref, mod_ref, lng_ref, lnb_ref,
                    wsg_ref, wsu_ref, wsd_ref, *rest, split):
    del rw_ref
    if split:
        outp_ref, outs_ref, p_w, lp_b, wt_b = rest
    else:
        out_ref, p_w, lp_b, wt_b = rest
    hb = h_ref[...]
    shared = _dot((_silu(_dot(hb, wsg_ref[...])) * _dot(hb, wsu_ref[...])).astype(BF16), wsd_ref[...])
    lp = lpt_ref[...]
    wt = wt_ref[...]
    for k in range(TOP_K):
        lp_b[k] = jnp.broadcast_to(lp[:, k:k + 1], (WIN, SORT_ROWS)).astype(jnp.int16)
        wt_b[k] = jnp.broadcast_to(wt[:, k:k + 1], (WIN, SORT_ROWS)).astype(BF16)
    ciota = lax.broadcasted_iota(I32, (WIN, SORT_ROWS), 1)
    for j in range(RL // SORT_ROWS):
        col = (ciota + j * SORT_ROWS).astype(jnp.int16)
        pm = jnp.zeros((WIN, SORT_ROWS), BF16)
        for k in range(TOP_K):
            pm = jnp.where(lp_b[k] == col, wt_b[k], pm)
        p_w[:, j * SORT_ROWS:(j + 1) * SORT_ROWS] = pm
    routed = _dot(p_w[...], yl_ref[...])
    z = ALPHA * x1_ref[...] + mod_ref[5:6, :] * (routed + shared)
    out = _layernorm_rows(z, lng_ref[...], lnb_ref[...])
    if split:
        @pl.when(pl.program_id(0) < N_PROMPT // WIN)
        def _():
            outp_ref[...] = out

        @pl.when(pl.program_id(0) >= N_PROMPT // WIN)
        def _():
            outs_ref[...] = out
    else:
        out_ref[...] = out


def _combine(rw, yl, lpos_t, wt, h2, x1, mods, lng, lnb, wsg, wsu, wsd, *, split):
    full = lambda shape: pl.BlockSpec(shape, lambda i, rw: (0,) * len(shape))
    if split:
        out_shape = (jax.ShapeDtypeStruct((N_PROMPT, D_MODEL), F32), jax.ShapeDtypeStruct((N_SAMPLE, D_MODEL), F32))
        out_specs = tuple(_row_pair_specs(WIN, D_MODEL))
    else:
        out_shape = jax.ShapeDtypeStruct((N_TOK, D_MODEL), F32)
        out_specs = pl.BlockSpec((WIN, D_MODEL), lambda i, rw: (i, 0))
    return pl.pallas_call(
        functools.partial(_combine_kernel, split=split),
        out_shape=out_shape,
        grid_spec=pltpu.PrefetchScalarGridSpec(
            num_scalar_prefetch=1,
            grid=(N_WIN,),
            in_specs=[
                pl.BlockSpec((RL, D_MODEL), lambda i, rw: (i, 0)),
                pl.BlockSpec((WIN, TOP_K), lambda i, rw: (i, 0)),
                pl.BlockSpec((WIN, TOP_K), lambda i, rw: (i, 0)),
                pl.BlockSpec((WIN, D_MODEL), lambda i, rw: (i, 0)),
                pl.BlockSpec((WIN, D_MODEL), lambda i, rw: (i, 0)),
                pl.BlockSpec((None, MOD_ROWS, D_MODEL), lambda i, rw: (_cond_block(i, WIN), 0, 0)),
                full((1, D_MODEL)), full((1, D_MODEL)),
                full((D_MODEL, D_EXPERT)), full((D_MODEL, D_EXPERT)), full((D_EXPERT, D_MODEL)),
            ],
            out_specs=out_specs,
            scratch_shapes=[pltpu.VMEM((WIN, RL), BF16),
                            pltpu.VMEM((TOP_K, WIN, SORT_ROWS), jnp.int16),
                            pltpu.VMEM((TOP_K, WIN, SORT_ROWS), BF16)],
        ),
        compiler_params=_cparams(("arbitrary",)),
        name="moe_combine",
    )(rw, yl, lpos_t, wt, h2, x1, mods, lng, lnb, wsg, wsu, wsd)


def _moe_and_norm(x1, h2, lgt, mods, lng, lnb, router_bias, tri, wg, wu, wd, wsg, wsu, wsd, *, layer, split):
    wts, lpos, p16 = _route(lgt, router_bias.reshape(N_EXPERTS, 1), tri)
    rw, block_e, n_used, gather_row, scatter_row = _moe_tables(p16[:, :N_EXPERTS].astype(I32))
    xl = _sort_rows(lpos, h2)
    yl = _experts(block_e, n_used, gather_row, scatter_row, xl, wg, wu, wd, layer)
    return _combine(rw, yl, lpos.T, wts.T, h2, x1, mods, lng, lnb, wsg, wsu, wsd, split=split)


def _rope_tables_mla():
    t = jnp.arange(DEC_SEQ)
    row = (t // GRID_W).astype(F32)
    col = (t % GRID_W).astype(F32)
    n = QK_ROPE // 4
    inv = ROPE_BASE ** (-jnp.arange(n, dtype=F32) / n)
    ang_r = row[:, None] * inv
    ang_c = col[:, None] * inv
    cos = jnp.ones((DEC_SEQ, LANE), F32)
    sin_m = jnp.zeros((DEC_SEQ, LANE), F32)
    sin_p = jnp.zeros((DEC_SEQ, LANE), F32)
    l0 = ROPE_LANE0
    for base, ang in ((l0, ang_r), (l0 + 2 * n, ang_c)):
        c, s = jnp.cos(ang), jnp.sin(ang)
        cos = cos.at[:, base:base + n].set(c).at[:, base + n:base + 2 * n].set(c)
        sin_m = sin_m.at[:, base:base + n].set(-s)
        sin_p = sin_p.at[:, base + n:base + 2 * n].set(s)
    ident = (jnp.ones((TM, LANE), F32), jnp.zeros((TM, LANE), F32), jnp.zeros((TM, LANE), F32))
    return tuple(jnp.concatenate([i, tbl], axis=0) for i, tbl in zip(ident, (cos, sin_m, sin_p)))


def _rope_tables_ret():
    half = RET_DK // 2
    theta = ROPE_BASE ** (-jnp.linspace(0.0, 1.0, half, dtype=F32))
    ang = jnp.arange(DEC_SEQ, dtype=F32)[:, None] * theta
    cos = jnp.concatenate([jnp.ones((TM, half), F32), jnp.cos(ang)], axis=0)
    sin = jnp.concatenate([jnp.zeros((TM, half), F32), jnp.sin(ang)], axis=0)
    return cos, sin


def _pad_heads(w, width, lane0=0):
    k = w.shape[0]
    w = w.reshape(k, MLA_HEADS, width)
    out = jnp.zeros((k, MLA_HEADS, HEAD_PAD), w.dtype).at[:, :, lane0:lane0 + width].set(w)
    return out.reshape(k, MLA_HEADS * HEAD_PAD)


def _rg_gate_weights(wa, ba, wx, bx):
    n_ct = RG_W // LANE
    per = LANE // RG_BW
    tiles_w, tiles_b = [], []
    for c in range(n_ct):
        cols_w, cols_b = [], []
        for d in range(2):
            for w, b in ((wa, ba), (wx, bx)):
                m = jnp.zeros((LANE, LANE), F32)
                for p in range(per):
                    m = m.at[p * RG_BW:(p + 1) * RG_BW, p * RG_BW:(p + 1) * RG_BW].set(w[d, c * per + p])
                cols_w.append(m)
                cols_b.append(b[d, c * LANE:(c + 1) * LANE])
        tiles_w.append(jnp.concatenate(cols_w, axis=1))
        tiles_b.append(jnp.concatenate(cols_b, axis=0)[None, :])
    return jnp.stack(tiles_w).astype(BF16), jnp.stack(tiles_b)


def kernel(x_prompt, x_sample, cache_mla_ckv, cache_mla_krope, state_rglru, state_ret, c, c_ctx, w_ada, b_ada,
           ln_g, ln_b, w_in_ab, rg_conv_w, rg_conv_b, rg_wa, rg_ba, rg_wx, rg_bx, rg_lambda, mla_q_norm, mla_w_uq,
           mla_kv_norm, mla_w_ukv, w_out_ab, w_in_c, ret_gamma_logit, w_out_c, w_router, router_bias,
           w_exp_gate, w_exp_up, w_exp_down, w_sh_gate, w_sh_up, w_sh_down):
    xp = x_prompt.reshape(N_PROMPT, D_MODEL)
    xs = x_sample.reshape(N_SAMPLE, D_MODEL)
    cond = jnp.zeros((16, D_MODEL), F32).at[0].set(c_ctx).at[1:1 + DEC_BATCH].set(c)
    mods_all = _ada_modulation(cond, w_ada, b_ada).reshape(DEPTH, 16, 6, D_MODEL)[:, :N_COND]
    mods_all = jnp.pad(mods_all, ((0, 0), (0, 0), (0, MOD_ROWS - 6), (0, 0)))

    tri = (jnp.arange(WIN)[:, None] < jnp.arange(WIN)[None, :]).astype(BF16)
    wr_t = jnp.swapaxes(w_router, 1, 2)
    wr_hi = wr_t.astype(BF16)
    wr_lo = (wr_t - wr_hi.astype(F32)).astype(BF16)
    wg_e, wu_e, wd_e = w_exp_gate, w_exp_up, w_exp_down
    wsg, wsu, wsd = w_sh_gate.astype(BF16), w_sh_up.astype(BF16), w_sh_down.astype(BF16)

    l, e = 0, 0
    mods = mods_all[l]
    n_main = 2 * RG_W + Q_LORA + KV_LORA
    w_main = w_in_ab[e][:, :n_main].astype(BF16)
    w_kr = jnp.zeros((D_MODEL, LANE), F32).at[:, ROPE_LANE0:ROPE_LANE0 + QK_ROPE].set(w_in_ab[e][:, n_main:]).astype(BF16)
    main, krp = _proj_ab(xp, xs, mods, w_main, w_kr)

    wg_rg, bg_rg = _rg_gate_weights(rg_wa[e], rg_ba[e], rg_wx[e], rg_bx[e])
    h0_p = jnp.zeros((BATCH, 2, RG_W), F32)
    rg_args = (rg_conv_w[e], rg_conv_b[e].reshape(1, RG_W), wg_rg, bg_rg, rg_lambda[e])
    yrg_p, rg_fin = _rglru(main, *rg_args, h0_p, n_seq=BATCH, seq=SEQ, row_block0=0)
    yrg_s, _ = _rglru(main, *rg_args, state_rglru[:, e],
                      n_seq=DEC_BATCH, seq=DEC_SEQ, row_block0=N_PROMPT // DEC_SEQ)

    cos_t, sm_t, sp_t = _rope_tables_mla()
    w_uq = mla_w_uq[e].reshape(Q_LORA, MLA_HEADS, QK_NOPE + QK_ROPE)
    wq_p = _pad_heads(w_uq.reshape(Q_LORA, -1), QK_NOPE + QK_ROPE).astype(BF16)
    w_ukv = mla_w_ukv[e].reshape(KV_LORA, MLA_HEADS, QK_NOPE + V_HEAD)
    wuk_p = _pad_heads(w_ukv[:, :, :QK_NOPE].reshape(KV_LORA, -1), QK_NOPE).astype(BF16)
    wuvt = w_ukv[:, :, QK_NOPE:].reshape(KV_LORA, MLA_HEADS * V_HEAD).T.astype(BF16)
    q_att, k_att, v_att, ckv_n = _mla_prep(main, krp, cos_t, sm_t, sp_t, mla_q_norm[e].reshape(1, Q_LORA), wq_p,
                                           mla_kv_norm[e].reshape(1, KV_LORA), wuk_p, wuvt)
    ctx_ckv = cache_mla_ckv[:, e].reshape(DEC_BATCH * PAST_LEN, KV_LORA)
    ctx_krp = jnp.zeros((DEC_BATCH * PAST_LEN, LANE), F32).at[:, ROPE_LANE0:ROPE_LANE0 + QK_ROPE].set(
        cache_mla_krope[:, e].reshape(DEC_BATCH * PAST_LEN, QK_ROPE))
    kc_att, vc_att = _mla_ctx(ctx_ckv, ctx_krp, wuk_p, wuvt)

    o_att_p = _attention(q_att, k_att, v_att, None, None, n_seq=BATCH, seq=SEQ, row_block0=0, tq=SEQ)
    o_att_s = _attention(q_att, k_att, v_att, kc_att, vc_att,
                         n_seq=DEC_BATCH, seq=DEC_SEQ, row_block0=N_PROMPT // DEC_SEQ, tq=TQ)

    w_out = w_out_ab[e].astype(BF16)
    x1, h2, lgt = _out_ab(yrg_p, yrg_s, o_att_p, o_att_s, w_out[:RG_W], w_out[RG_W:], xp, xs, mods,
                          ln_g[l, 0].reshape(1, D_MODEL), ln_b[l, 0].reshape(1, D_MODEL), wr_hi[l], wr_lo[l])
    x = _moe_and_norm(x1, h2, lgt, mods, ln_g[l, 1].reshape(1, D_MODEL), ln_b[l, 1].reshape(1, D_MODEL),
                      router_bias[l], tri, wg_e, wu_e, wd_e, wsg[l], wsu[l], wsd[l], layer=l, split=False)

    new_ckv = ckv_n.reshape(BATCH, 1, SEQ, KV_LORA)
    new_krope = krp[:N_PROMPT, ROPE_LANE0:ROPE_LANE0 + QK_ROPE].reshape(BATCH, 1, SEQ, QK_ROPE)
    new_rg = rg_fin.reshape(BATCH, 1, 2, RG_W)

    l, o = 1, 0
    mods = mods_all[l]
    qk = RET_HEADS * RET_DK
    w_c = w_in_c[o].astype(BF16)
    cos_r, sin_r = _rope_tables_ret()
    q_r, k_r, v_r, g_r = _proj_c(x, mods, cos_r, sin_r, w_c[:, :qk], w_c[:, qk:2 * qk],
                                 w_c[:, 2 * qk:2 * qk + MIX_C], w_c[:, 2 * qk + MIX_C:])
    gam = jnp.broadcast_to(ret_gamma_logit[o].astype(F32)[:, :, None, None], (2, RET_HEADS, SUBLANE, LANE))
    r0_p = jnp.zeros((BATCH, 2, RET_HEADS, RET_DK, RET_DV), F32)
    o_ret_p, r_fin = _retention(q_r, k_r, v_r, gam, r0_p, n_seq=BATCH, seq=SEQ, row_block0=0, with_state=True)
    (o_ret_s,) = _retention(q_r, k_r, v_r, gam, state_ret[:, o],
                            n_seq=DEC_BATCH, seq=DEC_SEQ, row_block0=N_PROMPT // DEC_SEQ, with_state=False)
    x1, h2, lgt = _out_c(o_ret_p, o_ret_s, g_r, w_out_c[o].astype(BF16), x, mods,
                         ln_g[l, 0].reshape(1, D_MODEL), ln_b[l, 0].reshape(1, D_MODEL), wr_hi[l], wr_lo[l])
    y_p, y_s = _moe_and_norm(x1, h2, lgt, mods, ln_g[l, 1].reshape(1, D_MODEL), ln_b[l, 1].reshape(1, D_MODEL),
                             router_bias[l], tri, wg_e, wu_e, wd_e, wsg[l], wsu[l], wsd[l], layer=l, split=True)

    y_prompt = y_p.reshape(BATCH, SEQ, D_MODEL)
    y_sample = y_s.reshape(DEC_BATCH, DEC_SEQ, D_MODEL)
    new_ret = r_fin.reshape(BATCH, 1, 2, RET_HEADS, RET_DK, RET_DV)
    return (y_prompt, y_sample, new_ckv, new_krope, new_rg, new_ret)
```

```python
import functools
import math

import jax
import jax.numpy as jnp
from jax import lax
from jax.experimental import pallas as pl
from jax.experimental.pallas import tpu as pltpu

F32 = jnp.float32
BF16 = jnp.bfloat16
I32 = jnp.int32

D_MODEL = 1024
BATCH, SEQ = 16, 256
DEC_BATCH, DEC_SEQ = 8, 4096
PAST_LEN = 256
DEPTH = 2
GRID_W = 64
RG_W, RG_BLOCKS = 512, 8
RG_BW = RG_W // RG_BLOCKS
RG_C = 8.0
CONV_W, CONV_LEFT = 4, 2
MLA_HEADS, QK_NOPE, QK_ROPE, V_HEAD = 8, 64, 32, 64
Q_LORA, KV_LORA = 768, 256
ROPE_BASE = 10000.0
ATTN_SCALE = (QK_NOPE + QK_ROPE) ** -0.5
RET_HEADS, RET_DK, RET_DV, RET_CHUNK = 4, 256, 512, 128
MIX_C = RET_HEADS * RET_DV
N_EXPERTS, TOP_K, N_GROUPS, TOPK_GROUPS = 64, 8, 8, 4
GROUP_SIZE = N_EXPERTS // N_GROUPS
D_EXPERT = 256
ROUTED_SCALE = 2.5
ALPHA = (2 * DEPTH) ** 0.25
EPS = 1e-6

N_PROMPT = BATCH * SEQ
N_SAMPLE = DEC_BATCH * DEC_SEQ
N_TOK = N_PROMPT + N_SAMPLE
N_COND = 1 + DEC_BATCH
MOD_ROWS = 8

LANE = 128
SUBLANE = 8
TM = 512
HEAD_PAD = 128
ROPE_LANE0 = QK_NOPE
TQ = 256
KC = 256
AH = 2
RET_UNROLL = 4
V_ONES = 16
Q_PRESCALE = ATTN_SCALE * math.log2(math.e)
SCAN_ROWS = 64
GATE_ROWS = 256
WIN = 256
N_WIN = N_TOK // WIN
PIECE = 16
SORT_ROWS = 256
RL = 3072
TME = 1024
PIECES = TME // PIECE
EXPERT_BUFS = 3
SPARE_BLOCKS = 2
SPARE_WIN = 2
N_PAIRS = N_TOK * TOP_K
N_EBLOCKS = (N_PAIRS + N_WIN * N_EXPERTS * (PIECE - 1)) // TME + N_EXPERTS
NEG = -1e30


def _cparams(sem, vmem_mb=48):
    return pltpu.CompilerParams(dimension_semantics=sem, vmem_limit_bytes=vmem_mb * 1024 * 1024)


def _cond_block(i, tm):
    npb = N_PROMPT // tm
    return jnp.where(i < npb, 0, 1 + (i - npb) // (DEC_SEQ // tm))


def _pos_block(i, tm):
    npb = N_PROMPT // tm
    return jnp.where(i < npb, 0, 1 + (i - npb) % (DEC_SEQ // tm))


def _split_hi_lo(a):
    hi = a.astype(BF16)
    lo = (a - hi.astype(F32)).astype(BF16)
    return hi, lo


def _dot(a, b):
    return jnp.dot(a, b, preferred_element_type=F32)


def _dot_nt(a, b):
    return lax.dot_general(a, b, (((1,), (1,)), ((), ())), preferred_element_type=F32)


def _silu(x):
    return x * jax.nn.sigmoid(x)


def _gelu_tanh(x):
    return 0.5 * x * (1.0 + jnp.tanh(math.sqrt(2.0 / math.pi) * (x + 0.044715 * (x * x * x))))


def _softplus(x):
    return jnp.maximum(x, 0.0) + jnp.log1p(jnp.exp(-jnp.abs(x)))


def _layernorm_rows(z, g, b):
    mu = jnp.mean(z, axis=-1, keepdims=True)
    zc = z - mu
    var = jnp.mean(zc * zc, axis=-1, keepdims=True)
    return (zc * lax.rsqrt(var + EPS)) * g + b


def _ada_kernel(c_ref, w_ref, b_ref, o_ref):
    s_hi, s_lo = _split_hi_lo(_silu(c_ref[...]))
    w_hi, w_lo = _split_hi_lo(w_ref[...])
    o_ref[...] = _dot(s_hi, w_hi) + _dot(s_hi, w_lo) + _dot(s_lo, w_hi) + b_ref[...]


def _ada_modulation(cond, w_ada, b_ada):
    n6 = 6 * D_MODEL
    tn = D_MODEL
    return pl.pallas_call(
        _ada_kernel,
        out_shape=jax.ShapeDtypeStruct((DEPTH, 16, n6), F32),
        grid=(DEPTH, n6 // tn),
        in_specs=[
            pl.BlockSpec((16, D_MODEL), lambda l, j: (0, 0)),
            pl.BlockSpec((None, D_MODEL, tn), lambda l, j: (l, 0, j)),
            pl.BlockSpec((None, 1, tn), lambda l, j: (l, 0, j)),
        ],
        out_specs=pl.BlockSpec((None, 16, tn), lambda l, j: (l, 0, j)),
        compiler_params=_cparams(("arbitrary", "arbitrary")),
        name="ada_modulation",
    )(cond, w_ada, b_ada.reshape(DEPTH, 1, n6))


def _modulated(x, mod_ref, shift_row, scale_row):
    return x * (1.0 + mod_ref[scale_row:scale_row + 1, :]) + mod_ref[shift_row:shift_row + 1, :]


def _row_pair_specs(tm, width, col=0):
    npb = N_PROMPT // tm
    return [pl.BlockSpec((tm, width), lambda i, *_: (jnp.minimum(i, npb - 1), col)),
            pl.BlockSpec((tm, width), lambda i, *_: (jnp.maximum(i - npb, 0), col))]


def _pick_rows(p_ref, s_ref, tm):
    return jnp.where(pl.program_id(0) < N_PROMPT // tm, p_ref[...], s_ref[...])


def _proj_ab_kernel(xp_ref, xs_ref, mod_ref, w_ref, wkr_ref, main_ref, kr_ref):
    h = _modulated(_pick_rows(xp_ref, xs_ref, TM), mod_ref, 0, 1).astype(BF16)
    n = w_ref.shape[1]
    step = 512
    for j in range(n // step):
        main_ref[:, j * step:(j + 1) * step] = _dot(h, w_ref[:, j * step:(j + 1) * step])
    kr_ref[...] = _dot(h, wkr_ref[...])


def _proj_ab(xp, xs, mods, w_main, w_kr):
    n = w_main.shape[1]
    return pl.pallas_call(
        _proj_ab_kernel,
        out_shape=(jax.ShapeDtypeStruct((N_TOK, n), F32), jax.ShapeDtypeStruct((N_TOK, LANE), F32)),
        grid=(N_TOK // TM,),
        in_specs=_row_pair_specs(TM, D_MODEL) + [
            pl.BlockSpec((None, MOD_ROWS, D_MODEL), lambda i: (_cond_block(i, TM), 0, 0)),
            pl.BlockSpec((D_MODEL, n), lambda i: (0, 0)),
            pl.BlockSpec((D_MODEL, LANE), lambda i: (0, 0)),
        ],
        out_specs=(pl.BlockSpec((TM, n), lambda i: (i, 0)), pl.BlockSpec((TM, LANE), lambda i: (i, 0))),
        compiler_params=_cparams(("arbitrary",)),
        name="proj_ab",
    )(xp, xs, mods, w_main, w_kr)


def _rglru_kernel(xr_ref, gr_ref, cw_ref, cb_ref, wg_ref, bg_ref, lam_ref, h0_ref,
                  y_ref, hfin_ref, xpad, a_s, b_s, *, seq):
    pad = SUBLANE
    xpad[0:pad, :] = jnp.zeros((pad, LANE), F32)
    xpad[seq + pad:seq + 2 * pad, :] = jnp.zeros((pad, LANE), F32)
    xpad[pad:seq + pad, :] = xr_ref[...]

    sp = _softplus(-lam_ref[...])
    cw = cw_ref[...]
    cb = cb_ref[...]
    wg = wg_ref[...]
    bg = bg_ref[...]

    def gate_step(c, carry):
        t0 = pl.multiple_of(c * GATE_ROWS, GATE_ROWS)
        win = xpad[pl.ds(t0, GATE_ROWS + 2 * pad), :]
        xc = cb
        for j in range(CONV_W):
            off = pad - CONV_LEFT + j
            xc = xc + win[off:off + GATE_ROWS, :] * cw[j:j + 1, :]
        g = _dot(xc.astype(BF16), wg) + bg
        for d in range(2):
            r = jax.nn.sigmoid(g[:, (2 * d) * LANE:(2 * d + 1) * LANE])
            i = jax.nn.sigmoid(g[:, (2 * d + 1) * LANE:(2 * d + 2) * LANE])
            log_a = (-RG_C * r) * sp[d:d + 1, :]
            a = jnp.exp(log_a)
            t = jnp.tanh(log_a)
            bt = jnp.sqrt(2.0 * t / (t - 1.0)) * (i * xc)
            a_s[d, pl.ds(t0, GATE_ROWS), :] = a
            b_s[d, pl.ds(t0, GATE_ROWS), :] = bt
        return carry

    lax.fori_loop(0, seq // GATE_ROWS, gate_step, 0, unroll=min(2, seq // GATE_ROWS))

    row = lax.broadcasted_iota(I32, (SCAN_ROWS, LANE), 0) % SUBLANE
    n_steps = seq // SCAN_ROWS
    tiles = SCAN_ROWS // SUBLANE

    def local_scan(a, b, reverse):
        for k in (1, 2, 4):
            if reverse:
                ok = row < SUBLANE - k
                shift = SCAN_ROWS - k
            else:
                ok = row >= k
                shift = k
            a_sh = jnp.where(ok, pltpu.roll(a, shift, 0), 1.0)
            b_sh = jnp.where(ok, pltpu.roll(b, shift, 0), 0.0)
            b = a * b_sh + b
            a = a * a_sh
        return a, b

    def fwd_step(c, h):
        t0 = pl.multiple_of(c * SCAN_ROWS, SCAN_ROWS)
        a, b = local_scan(a_s[0, pl.ds(t0, SCAN_ROWS), :], b_s[0, pl.ds(t0, SCAN_ROWS), :], False)
        outs = []
        for j in range(tiles):
            hj = a[j * SUBLANE:(j + 1) * SUBLANE, :] * h + b[j * SUBLANE:(j + 1) * SUBLANE, :]
            outs.append(hj)
            h = hj[SUBLANE - 1:SUBLANE, :]
        y_ref[pl.ds(t0, SCAN_ROWS), :] = jnp.concatenate(outs, axis=0)
        return h

    h_f = lax.fori_loop(0, n_steps, fwd_step, h0_ref[0:1, :], unroll=4)

    def bwd_step(c, h):
        t0 = pl.multiple_of((n_steps - 1 - c) * SCAN_ROWS, SCAN_ROWS)
        a, b = local_scan(a_s[1, pl.ds(t0, SCAN_ROWS), :], b_s[1, pl.ds(t0, SCAN_ROWS), :], True)
        outs = [None] * tiles
        for j in reversed(range(tiles)):
            hj = a[j * SUBLANE:(j + 1) * SUBLANE, :] * h + b[j * SUBLANE:(j + 1) * SUBLANE, :]
            outs[j] = hj
            h = hj[0:1, :]
        hb = jnp.concatenate(outs, axis=0)
        y_ref[pl.ds(t0, SCAN_ROWS), :] = (y_ref[pl.ds(t0, SCAN_ROWS), :] + hb) * _gelu_tanh(gr_ref[pl.ds(t0, SCAN_ROWS), :])
        return h

    h_b = lax.fori_loop(0, n_steps, bwd_step, h0_ref[1:2, :], unroll=4)
    hfin_ref[0:1, :] = h_f
    hfin_ref[1:2, :] = h_b


def _rglru(main, cw, cb, wg, bg, lam, h0, *, n_seq, seq, row_block0):
    n_ct = RG_W // LANE
    gr_col0 = RG_W // LANE
    kern = functools.partial(_rglru_kernel, seq=seq)
    return pl.pallas_call(
        kern,
        out_shape=(jax.ShapeDtypeStruct((n_seq * seq, RG_W), F32), jax.ShapeDtypeStruct((n_seq, 2, RG_W), F32)),
        grid=(n_seq, n_ct),
        in_specs=[
            pl.BlockSpec((seq, LANE), lambda b, c: (row_block0 + b, c)),
            pl.BlockSpec((seq, LANE), lambda b, c: (row_block0 + b, gr_col0 + c)),
            pl.BlockSpec((CONV_W, LANE), lambda b, c: (0, c)),
            pl.BlockSpec((1, LANE), lambda b, c: (0, c)),
            pl.BlockSpec((None, LANE, 4 * LANE), lambda b, c: (c, 0, 0)),
            pl.BlockSpec((None, 1, 4 * LANE), lambda b, c: (c, 0, 0)),
            pl.BlockSpec((2, LANE), lambda b, c: (0, c)),
            pl.BlockSpec((None, 2, LANE), lambda b, c: (b, 0, c)),
        ],
        out_specs=(
            pl.BlockSpec((seq, LANE), lambda b, c: (b, c)),
            pl.BlockSpec((None, 2, LANE), lambda b, c: (b, 0, c)),
        ),
        scratch_shapes=[
            pltpu.VMEM((seq + 2 * SUBLANE, LANE), F32),
            pltpu.VMEM((2, seq, LANE), F32),
            pltpu.VMEM((2, seq, LANE), F32),
        ],
        compiler_params=_cparams(("arbitrary", "arbitrary")),
        name=f"rglru_s{seq}",
    )(main, main, cw, cb, wg, bg, lam, h0)


def _rope_lanes(x, cos, sin_m, sin_p):
    n = x.shape[1] // LANE
    half = QK_ROPE // 4
    cos_t = jnp.concatenate([cos] * n, axis=1) if n > 1 else cos
    sm_t = jnp.concatenate([sin_m] * n, axis=1) if n > 1 else sin_m
    sp_t = jnp.concatenate([sin_p] * n, axis=1) if n > 1 else sin_p
    up = pltpu.roll(x, x.shape[1] - half, 1)
    dn = pltpu.roll(x, half, 1)
    return x * cos_t + up * sm_t + dn * sp_t


def _mla_prep_kernel(cq0_ref, cq1_ref, cq2_ref, ckv_ref, kr_ref, cos_ref, sm_ref, sp_ref,
                     qn_ref, wq_ref, kvn_ref, wuk_ref, wuvt_ref,
                     q_ref, k_ref, vt_ref, ckvn_ref):
    cq = [cq0_ref[...], cq1_ref[...], cq2_ref[...]]
    ms = (jnp.sum(cq[0] * cq[0], axis=-1, keepdims=True) + jnp.sum(cq[1] * cq[1], axis=-1, keepdims=True)
          + jnp.sum(cq[2] * cq[2], axis=-1, keepdims=True)) * (1.0 / Q_LORA)
    inv = lax.rsqrt(ms + EPS)
    blk = Q_LORA // 3
    q = None
    for j in range(3):
        cqn = ((cq[j] * inv) * qn_ref[:, j * blk:(j + 1) * blk]).astype(BF16)
        part = _dot(cqn, wq_ref[j * blk:(j + 1) * blk, :])
        q = part if q is None else q + part
    cos, sm, sp = cos_ref[...], sm_ref[...], sp_ref[...]
    q_ref[...] = (_rope_lanes(q, cos, sm, sp) * Q_PRESCALE).T.astype(BF16)

    ckv = ckv_ref[...]
    inv_kv = lax.rsqrt(jnp.mean(ckv * ckv, axis=-1, keepdims=True) + EPS)
    ckvn = (ckv * inv_kv) * kvn_ref[...]

    @pl.when(pl.program_id(0) < N_PROMPT // TM)
    def _():
        ckvn_ref[...] = ckvn

    ckvn_b = ckvn.astype(BF16)
    kr_rot = _rope_lanes(kr_ref[...], cos, sm, sp)
    k_ref[...] = (_dot(ckvn_b, wuk_ref[...]) + jnp.concatenate([kr_rot] * MLA_HEADS, axis=1)).astype(BF16)
    vt = _dot_nt(wuvt_ref[...], ckvn_b).astype(BF16)
    for c in range(vt_ref.shape[0]):
        vt_ref[c] = vt[:, c * KC:(c + 1) * KC]


def _mla_prep(main, krp, cos_t, sm_t, sp_t, q_norm, wq_p, kv_norm, wuk_p, wuvt):
    cq_col0 = 2 * RG_W // 256
    hp = MLA_HEADS * HEAD_PAD
    full = lambda shape: pl.BlockSpec(shape, lambda i: (0,) * len(shape))
    tab = pl.BlockSpec((TM, LANE), lambda i: (_pos_block(i, TM), 0))
    return pl.pallas_call(
        _mla_prep_kernel,
        out_shape=(
            jax.ShapeDtypeStruct((hp, N_TOK), BF16),
            jax.ShapeDtypeStruct((N_TOK, hp), BF16),
            jax.ShapeDtypeStruct((N_TOK // KC, MLA_HEADS * V_HEAD, KC), BF16),
            jax.ShapeDtypeStruct((N_PROMPT, KV_LORA), F32),
        ),
        grid=(N_TOK // TM,),
        in_specs=[
            pl.BlockSpec((TM, 256), lambda i: (i, cq_col0)),
            pl.BlockSpec((TM, 256), lambda i: (i, cq_col0 + 1)),
            pl.BlockSpec((TM, 256), lambda i: (i, cq_col0 + 2)),
            pl.BlockSpec((TM, 256), lambda i: (i, cq_col0 + 3)),
            pl.BlockSpec((TM, LANE), lambda i: (i, 0)),
            tab, tab, tab,
            full((1, Q_LORA)), full((Q_LORA, hp)), full((1, KV_LORA)), full((KV_LORA, hp)),
            full((MLA_HEADS * V_HEAD, KV_LORA)),
        ],
        out_specs=(
            pl.BlockSpec((hp, TM), lambda i: (0, i)),
            pl.BlockSpec((TM, hp), lambda i: (i, 0)),
            pl.BlockSpec((TM // KC, MLA_HEADS * V_HEAD, KC), lambda i: (i, 0, 0)),
            pl.BlockSpec((TM, KV_LORA), lambda i: (jnp.minimum(i, N_PROMPT // TM - 1), 0)),
        ),
        compiler_params=_cparams(("arbitrary",)),
        name="mla_prep",
    )(main, main, main, main, krp, cos_t, sm_t, sp_t, q_norm, wq_p, kv_norm, wuk_p, wuvt)


def _mla_ctx_kernel(ckv_ref, kr_ref, wuk_ref, wuvt_ref, k_ref, vt_ref):
    ckv_b = ckv_ref[...].astype(BF16)
    k_ref[...] = (_dot(ckv_b, wuk_ref[...]) + jnp.concatenate([kr_ref[...]] * MLA_HEADS, axis=1)).astype(BF16)
    vt_ref[...] = _dot_nt(wuvt_ref[...], ckv_b).astype(BF16)


def _mla_ctx(ctx_ckv, ctx_krp, wuk_p, wuvt):
    n = ctx_ckv.shape[0]
    hp = MLA_HEADS * HEAD_PAD
    tm = KC
    full = lambda shape: pl.BlockSpec(shape, lambda i: (0,) * len(shape))
    return pl.pallas_call(
        _mla_ctx_kernel,
        out_shape=(jax.ShapeDtypeStruct((n, hp), BF16),
                   jax.ShapeDtypeStruct((n // tm, MLA_HEADS * V_HEAD, tm), BF16)),
        grid=(n // tm,),
        in_specs=[
            pl.BlockSpec((tm, KV_LORA), lambda i: (i, 0)),
            pl.BlockSpec((tm, LANE), lambda i: (i, 0)),
            full((KV_LORA, hp)), full((MLA_HEADS * V_HEAD, KV_LORA)),
        ],
        out_specs=(pl.BlockSpec((tm, hp), lambda i: (i, 0)),
                   pl.BlockSpec((None, MLA_HEADS * V_HEAD, tm), lambda i: (i, 0, 0))),
        compiler_params=_cparams(("arbitrary",)),
        name="mla_ctx",
    )(ctx_ckv, ctx_krp, wuk_p, wuvt)


def _attn_kernel(*refs, seq, tq, n_ctx):
    if n_ctx:
        q_ref, k_ref, vt_ref, kc_ref, vtc_ref, o_ref, s_scr, p_scr, k_all, vt_all = refs
    else:
        q_ref, k_ref, vt_ref, o_ref, s_scr, p_scr, k_all, vt_all = refs
    has_ctx = 1 if n_ctx else 0
    n_own = seq // KC
    n = n_own + has_ctx
    qs = [q_ref[h * HEAD_PAD:(h + 1) * HEAD_PAD, :] for h in range(AH)]

    @pl.when(pl.program_id(2) == 0)
    def _():
        k_all[0:seq, :] = k_ref[...]
        if has_ctx:
            k_all[seq:seq + KC, :] = kc_ref[...]
        for h in range(AH):
            vt_all[0:n_own, h, 0:V_HEAD, :] = vt_ref[:, h * V_HEAD:(h + 1) * V_HEAD, :]
            if has_ctx:
                vt_all[n_own, h, 0:V_HEAD, :] = vtc_ref[h * V_HEAD:(h + 1) * V_HEAD, :]
            vt_all[:, h, V_HEAD:V_HEAD + V_ONES, :] = jnp.ones((n, V_ONES, KC), BF16)

    def scores(c, slot):
        for h in range(AH):
            s_scr[slot, h] = _dot(k_all[c * KC:(c + 1) * KC, h * HEAD_PAD:(h + 1) * HEAD_PAD], qs[h])

    def softmax_chunk(slot, st):
        out = []
        for h in range(AH):
            m, a1, _, acc = st[h]
            t = s_scr[slot, h]
            m_new = jnp.maximum(m, jnp.max(t, axis=0, keepdims=True))
            p_scr[slot, h] = jnp.exp2(t - m_new).astype(BF16)
            out.append((m_new, jnp.exp2(m - m_new), a1, acc))
        return out

    def weighted_values(c, slot, st, alphas):
        return [st[h][:3] + (alphas[h] * st[h][3] + _dot(vt_all[c, h], p_scr[slot, h]),) for h in range(AH)]

    one = jnp.ones((1, tq), F32)
    st = [(jnp.full((1, tq), NEG, F32), one, one, jnp.zeros((V_HEAD + V_ONES, tq), F32)) for _ in range(AH)]
    scores(0, 0)
    if n > 1:
        scores(1, 1)
    for c in range(n):
        if c + 2 < n:
            scores(c + 2, (c + 2) % 3)
        alphas = [st[h][2] for h in range(AH)]
        st = softmax_chunk(c % 3, st)
        if c >= 2:
            st = weighted_values(c - 2, (c - 2) % 3, st, alphas)
    if n > 1:
        st = weighted_values(n - 2, (n - 2) % 3, st, [st[h][2] for h in range(AH)])
    st = weighted_values(n - 1, (n - 1) % 3, st, [st[h][1] for h in range(AH)])
    for h in range(AH):
        acc = st[h][3]
        o_ref[:, h * V_HEAD:(h + 1) * V_HEAD] = (acc[0:V_HEAD] / acc[V_HEAD:V_HEAD + 1]).T.astype(o_ref.dtype)


def _attention(q, k, vt, kc, vtc, *, n_seq, seq, row_block0, tq):
    n_ctx = 0 if kc is None else PAST_LEN
    n_hp = MLA_HEADS // AH
    nq = seq // tq
    kern = functools.partial(_attn_kernel, seq=seq, tq=tq, n_ctx=n_ctx)
    in_specs = [
        pl.BlockSpec((AH * HEAD_PAD, tq), lambda b, j, i: (j, (row_block0 + b) * nq + i)),
        pl.BlockSpec((seq, AH * HEAD_PAD), lambda b, j, i: (row_block0 + b, j)),
        pl.BlockSpec((seq // KC, AH * V_HEAD, KC), lambda b, j, i: (row_block0 + b, j, 0)),
    ]
    args = [q, k, vt]
    if n_ctx:
        in_specs += [
            pl.BlockSpec((n_ctx, AH * HEAD_PAD), lambda b, j, i: (b, j)),
            pl.BlockSpec((None, AH * V_HEAD, KC), lambda b, j, i: (b, j, 0)),
        ]
        args += [kc, vtc]
    return pl.pallas_call(
        kern,
        out_shape=jax.ShapeDtypeStruct((n_seq * seq, MLA_HEADS * V_HEAD), BF16),
        grid=(n_seq, n_hp, nq),
        in_specs=in_specs,
        out_specs=pl.BlockSpec((tq, AH * V_HEAD), lambda b, j, i: (b * nq + i, j)),
        scratch_shapes=[pltpu.VMEM((3, AH, KC, tq), F32), pltpu.VMEM((3, AH, KC, tq), BF16),
                        pltpu.VMEM((seq + n_ctx, AH * HEAD_PAD), BF16),
                        pltpu.VMEM(((seq + n_ctx) // KC, AH, V_HEAD + V_ONES, KC), BF16)],
        compiler_params=_cparams(("arbitrary", "arbitrary", "arbitrary")),
        name=f"mla_attention_s{seq}",
    )(*args)


def _post_mixer(y, x, mod_ref, lng_ref, lnb_ref, wrh_ref, wrl_ref, x1_ref, h2_ref, lgt_ref):
    z = ALPHA * x + mod_ref[2:3, :] * y
    x1 = _layernorm_rows(z, lng_ref[...], lnb_ref[...])
    x1_ref[...] = x1
    h2 = _modulated(x1, mod_ref, 3, 4)
    h_hi, h_lo = _split_hi_lo(h2)
    h2_ref[...] = h_hi
    w_hi, w_lo = wrh_ref[...], wrl_ref[...]
    lgt_ref[...] = _dot_nt(w_hi, h_hi) + _dot_nt(w_hi, h_lo) + _dot_nt(w_lo, h_hi)


def _out_ab_kernel(yrgp_ref, yrgs_ref, op_ref, os_ref, wa_ref, wb_ref, xp_ref, xs_ref,
                   mod_ref, lng_ref, lnb_ref, wrh_ref, wrl_ref, x1_ref, h2_ref, lgt_ref):
    y = (_dot(_pick_rows(yrgp_ref, yrgs_ref, TM).astype(BF16), wa_ref[...])
         + _dot(_pick_rows(op_ref, os_ref, TM), wb_ref[...]))
    _post_mixer(y, _pick_rows(xp_ref, xs_ref, TM), mod_ref, lng_ref, lnb_ref, wrh_ref, wrl_ref,
                x1_ref, h2_ref, lgt_ref)


def _post_specs():
    full = lambda shape: pl.BlockSpec(shape, lambda i: (0,) * len(shape))
    in_specs = [
        pl.BlockSpec((None, MOD_ROWS, D_MODEL), lambda i: (_cond_block(i, TM), 0, 0)),
        full((1, D_MODEL)), full((1, D_MODEL)),
        full((N_EXPERTS, D_MODEL)), full((N_EXPERTS, D_MODEL)),
    ]
    out_shape = (
        jax.ShapeDtypeStruct((N_TOK, D_MODEL), F32),
        jax.ShapeDtypeStruct((N_TOK, D_MODEL), BF16),
        jax.ShapeDtypeStruct((N_EXPERTS, N_TOK), F32),
    )
    out_specs = (
        pl.BlockSpec((TM, D_MODEL), lambda i: (i, 0)),
        pl.BlockSpec((TM, D_MODEL), lambda i: (i, 0)),
        pl.BlockSpec((N_EXPERTS, TM), lambda i: (0, i)),
    )
    return in_specs, out_shape, out_specs


def _out_ab(yrg_p, yrg_s, o_p, o_s, w_a, w_b, xp, xs, mods, lng, lnb, wr_hi, wr_lo):
    full = lambda shape: pl.BlockSpec(shape, lambda i: (0,) * len(shape))
    post_in, out_shape, out_specs = _post_specs()
    return pl.pallas_call(
        _out_ab_kernel,
        out_shape=out_shape,
        grid=(N_TOK // TM,),
        in_specs=(_row_pair_specs(TM, RG_W) + _row_pair_specs(TM, MLA_HEADS * V_HEAD)
                  + [full((RG_W, D_MODEL)), full((MLA_HEADS * V_HEAD, D_MODEL))]
                  + _row_pair_specs(TM, D_MODEL) + post_in),
        out_specs=out_specs,
        compiler_params=_cparams(("arbitrary",)),
        name="out_ab",
    )(yrg_p, yrg_s, o_p, o_s, w_a, w_b, xp, xs, mods, lng, lnb, wr_hi, wr_lo)


def _proj_c_kernel(x_ref, mod_ref, cos_ref, sin_ref, wq_ref, wk_ref, wv_ref, wg_ref, q_ref, k_ref, v_ref, g_ref):
    h = _modulated(x_ref[...], mod_ref, 0, 1).astype(BF16)
    cos, sin = cos_ref[...], sin_ref[...]
    half = RET_DK // 2
    for hd in range(RET_HEADS):
        for w_ref, is_k in ((wq_ref, False), (wk_ref, True)):
            p = _dot(h, w_ref[:, hd * RET_DK:(hd + 1) * RET_DK])
            x1, x2 = p[:, :half], p[:, half:]
            r1 = x1 * cos - x2 * sin
            r2 = x2 * cos + x1 * sin
            if not is_k:
                q_ref[:, hd * RET_DK:hd * RET_DK + half] = r1.astype(BF16)
                q_ref[:, hd * RET_DK + half:(hd + 1) * RET_DK] = r2.astype(BF16)
            else:
                t1 = (r1 * RET_DK ** -0.5).T.astype(BF16)
                t2 = (r2 * RET_DK ** -0.5).T.astype(BF16)
                for c in range(k_ref.shape[0]):
                    k_ref[c, hd * RET_DK:hd * RET_DK + half, :] = t1[:, c * RET_CHUNK:(c + 1) * RET_CHUNK]
                    k_ref[c, hd * RET_DK + half:(hd + 1) * RET_DK, :] = t2[:, c * RET_CHUNK:(c + 1) * RET_CHUNK]
    step = 512
    for j in range(MIX_C // step):
        v_ref[:, j * step:(j + 1) * step] = _dot(h, wv_ref[:, j * step:(j + 1) * step]).astype(BF16)
        g_ref[:, j * step:(j + 1) * step] = _dot(h, wg_ref[:, j * step:(j + 1) * step])


def _proj_c(x, mods, cos_t, sin_t, wq, wk, wv, wg):
    full = lambda shape: pl.BlockSpec(shape, lambda i: (0,) * len(shape))
    qk = RET_HEADS * RET_DK
    tab = pl.BlockSpec((TM, RET_DK // 2), lambda i: (_pos_block(i, TM), 0))
    return pl.pallas_call(
        _proj_c_kernel,
        out_shape=(
            jax.ShapeDtypeStruct((N_TOK, qk), BF16), jax.ShapeDtypeStruct((N_TOK // RET_CHUNK, qk, RET_CHUNK), BF16),
            jax.ShapeDtypeStruct((N_TOK, MIX_C), BF16), jax.ShapeDtypeStruct((N_TOK, MIX_C), F32),
        ),
        grid=(N_TOK // TM,),
        in_specs=[
            pl.BlockSpec((TM, D_MODEL), lambda i: (i, 0)),
            pl.BlockSpec((None, MOD_ROWS, D_MODEL), lambda i: (_cond_block(i, TM), 0, 0)),
            tab, tab,
            full((D_MODEL, qk)), full((D_MODEL, qk)), full((D_MODEL, MIX_C)), full((D_MODEL, MIX_C)),
        ],
        out_specs=(
            pl.BlockSpec((TM, qk), lambda i: (i, 0)),
            pl.BlockSpec((TM // RET_CHUNK, qk, RET_CHUNK), lambda i: (i, 0, 0)),
            pl.BlockSpec((TM, MIX_C), lambda i: (i, 0)), pl.BlockSpec((TM, MIX_C), lambda i: (i, 0)),
        ),
        compiler_params=_cparams(("arbitrary",), 56),
        name="proj_c",
    )(x, mods, cos_t, sin_t, wq, wk, wv, wg)


def _retention_kernel(*refs, seq, with_state):
    if with_state:
        q_ref, kt_ref, v_ref, gam_ref, r0_ref, o_ref, rfin_ref, r_s = refs
    else:
        q_ref, kt_ref, v_ref, gam_ref, r0_ref, o_ref, r_s = refs
        rfin_ref = None
    c = RET_CHUNK
    n = seq // c
    ii = lax.broadcasted_iota(I32, (c, c), 0).astype(F32)
    jj = lax.broadcasted_iota(I32, (c, c), 1).astype(F32)
    ci = lax.broadcasted_iota(I32, (c, 1), 0).astype(F32)
    li = lax.broadcasted_iota(I32, (1, c), 1).astype(F32)

    consts = []
    for d in range(2):
        gam = gam_ref[d]
        lg_row = -_softplus(-gam[0:1, :])
        lg = jnp.broadcast_to(lg_row, (c, c))
        lg_col = jnp.broadcast_to(lg_row[:, 0:1], (c, 1))
        if d == 0:
            diff = ii - jj
            xi = jnp.exp((ci + 1.0) * lg_col)
            zeta = jnp.exp((c - 1.0 - li) * lg_row)
        else:
            diff = jj - ii
            xi = jnp.exp((c - ci) * lg_col)
            zeta = jnp.exp(li * lg_row)
        dmat = jnp.where(diff >= 0, jnp.exp(jnp.maximum(diff, 0.0) * lg), 0.0)
        g_chunk = jnp.exp(float(c) * lg_row[:, 0:1])
        consts.append((dmat, xi, zeta, g_chunk))
        r_s[d] = r0_ref[d]

    def chunk(d, idx, accumulate):
        dmat, xi, zeta, g_chunk = consts[d]
        t0 = pl.multiple_of(idx * c, c)
        qb = q_ref[pl.ds(t0, c), :]
        kt = kt_ref[idx]
        vb = v_ref[pl.ds(t0, c), :]
        r = r_s[d]
        inner = _dot(qb, kt) * dmat
        o = _dot(inner.astype(BF16), vb) + _dot((qb.astype(F32) * xi).astype(BF16), r.astype(BF16))
        r_s[d] = r * g_chunk + _dot((kt.astype(F32) * zeta).astype(BF16), vb)
        if accumulate:
            o_ref[pl.ds(t0, c), :] = o_ref[pl.ds(t0, c), :] + o
        else:
            o_ref[pl.ds(t0, c), :] = o

    def first_half(s, carry):
        chunk(0, s, False)
        chunk(1, n - 1 - s, False)
        return carry

    def second_half(s, carry):
        chunk(0, s, True)
        chunk(1, n - 1 - s, True)
        return carry

    unroll = min(RET_UNROLL, n // 2)
    lax.fori_loop(0, n // 2, first_half, 0, unroll=unroll)
    lax.fori_loop(n // 2, n, second_half, 0, unroll=unroll)
    if with_state:
        for d in range(2):
            rfin_ref[d] = r_s[d]


def _retention(q, k, v, gam, r0, *, n_seq, seq, row_block0, with_state):
    kern = functools.partial(_retention_kernel, seq=seq, with_state=with_state)
    out_shape = [jax.ShapeDtypeStruct((n_seq * seq, MIX_C), F32)]
    out_specs = [pl.BlockSpec((seq, RET_DV), lambda b, h: (b, h))]
    if with_state:
        out_shape.append(jax.ShapeDtypeStruct((n_seq, 2, RET_HEADS, RET_DK, RET_DV), F32))
        out_specs.append(pl.BlockSpec((None, 2, None, RET_DK, RET_DV), lambda b, h: (b, 0, h, 0, 0)))
    return pl.pallas_call(
        kern,
        out_shape=tuple(out_shape),
        grid=(n_seq, RET_HEADS),
        in_specs=[
            pl.BlockSpec((seq, RET_DK), lambda b, h: (row_block0 + b, h)),
            pl.BlockSpec((seq // RET_CHUNK, RET_DK, RET_CHUNK), lambda b, h: (row_block0 + b, h, 0)),
            pl.BlockSpec((seq, RET_DV), lambda b, h: (row_block0 + b, h)),
            pl.BlockSpec((2, None, SUBLANE, LANE), lambda b, h: (0, h, 0, 0)),
            pl.BlockSpec((None, 2, None, RET_DK, RET_DV), lambda b, h: (b, 0, h, 0, 0)),
        ],
        out_specs=tuple(out_specs),
        scratch_shapes=[pltpu.VMEM((2, RET_DK, RET_DV), F32)],
        compiler_params=_cparams(("arbitrary", "arbitrary"), 56),
        name=f"retention_s{seq}",
    )(q, k, v, gam, r0)


def _out_c_kernel(op_ref, os_ref, g_ref, w_ref, x_ref, mod_ref, lng_ref, lnb_ref, wrh_ref, wrl_ref,
                  x1_ref, h2_ref, lgt_ref):
    y = None
    o_all = _pick_rows(op_ref, os_ref, TM)
    for hd in range(RET_HEADS):
        o = o_all[:, hd * RET_DV:(hd + 1) * RET_DV]
        mu = jnp.mean(o, axis=-1, keepdims=True)
        oc = o - mu
        var = jnp.mean(oc * oc, axis=-1, keepdims=True)
        on = oc * lax.rsqrt(var + EPS)
        a = (on * _silu(g_ref[:, hd * RET_DV:(hd + 1) * RET_DV])).astype(BF16)
        part = _dot(a, w_ref[hd * RET_DV:(hd + 1) * RET_DV, :])
        y = part if y is None else y + part
    _post_mixer(y, x_ref[...], mod_ref, lng_ref, lnb_ref, wrh_ref, wrl_ref, x1_ref, h2_ref, lgt_ref)


def _out_c(o_p, o_s, g, w, x, mods, lng, lnb, wr_hi, wr_lo):
    full = lambda shape: pl.BlockSpec(shape, lambda i: (0,) * len(shape))
    post_in, out_shape, out_specs = _post_specs()
    return pl.pallas_call(
        _out_c_kernel,
        out_shape=out_shape,
        grid=(N_TOK // TM,),
        in_specs=_row_pair_specs(TM, MIX_C) + [
            pl.BlockSpec((TM, MIX_C), lambda i: (i, 0)),
            full((MIX_C, D_MODEL)),
            pl.BlockSpec((TM, D_MODEL), lambda i: (i, 0)),
        ] + post_in,
        out_specs=out_specs,
        compiler_params=_cparams(("arbitrary",), 56),
        name="out_c",
    )(o_p, o_s, g, w, x, mods, lng, lnb, wr_hi, wr_lo)


def _route_kernel(lgt_ref, bias_ref, tri_ref, w_ref, lpos_ref, p16_ref):
    tt = lgt_ref.shape[1]
    scores = jax.nn.sigmoid(lgt_ref[...])
    sel = scores + bias_ref[...]
    srow = lax.broadcasted_iota(I32, (GROUP_SIZE, tt), 0).astype(F32)
    ninf = -jnp.inf

    gs = []
    for g in range(N_GROUPS):
        sg = sel[g * GROUP_SIZE:(g + 1) * GROUP_SIZE, :]
        m1 = jnp.max(sg, axis=0, keepdims=True)
        i1 = jnp.min(jnp.where(sg == m1, srow, float(GROUP_SIZE)), axis=0, keepdims=True)
        m2 = jnp.max(jnp.where(srow == i1, ninf, sg), axis=0, keepdims=True)
        gs.append(m1 + m2)
    gs = jnp.concatenate(gs, axis=0)
    chosen = jnp.zeros((N_GROUPS, tt), F32)
    for _ in range(TOPK_GROUPS):
        mg = jnp.max(gs, axis=0, keepdims=True)
        gi = jnp.min(jnp.where(gs == mg, srow, float(N_GROUPS)), axis=0, keepdims=True)
        hit = srow == gi
        chosen = jnp.where(hit, 1.0, chosen)
        gs = jnp.where(hit, ninf, gs)
    sel = jnp.concatenate(
        [jnp.where(jnp.broadcast_to(chosen[g:g + 1, :], (GROUP_SIZE, tt)) > 0.5,
                   sel[g * GROUP_SIZE:(g + 1) * GROUP_SIZE, :], ninf) for g in range(N_GROUPS)], axis=0)

    erow = lax.broadcasted_iota(I32, (N_EXPERTS, tt), 0).astype(F32)
    ids, ws = [], []
    for _ in range(TOP_K):
        m = jnp.max(sel, axis=0, keepdims=True)
        ei = jnp.min(jnp.where(sel == m, erow, float(N_EXPERTS)), axis=0, keepdims=True)
        hit = erow == ei
        ids.append(ei)
        ws.append(jnp.sum(jnp.where(hit, scores, 0.0), axis=0, keepdims=True))
        sel = jnp.where(hit, ninf, sel)
    wsum = ws[0]
    for k in range(1, TOP_K):
        wsum = wsum + ws[k]
    w_ref[...] = jnp.concatenate([w / wsum * ROUTED_SCALE for w in ws], axis=0)

    member_f = jnp.zeros((N_EXPERTS, tt), F32)
    for k in range(TOP_K):
        member_f = jnp.where(erow == ids[k], 1.0, member_f)
    member_b = member_f.astype(BF16)
    cnt_row = _dot_nt(jnp.ones((SUBLANE, tt), BF16), member_b)[0:1, :]
    p16_row = jnp.ceil(cnt_row * (1.0 / PIECE)) * PIECE
    lane_e = lax.broadcasted_iota(I32, (N_EXPERTS, N_EXPERTS), 1)
    sub_e = lax.broadcasted_iota(I32, (N_EXPERTS, N_EXPERTS), 0)
    run_start = jnp.sum(jnp.where(lane_e < sub_e, jnp.broadcast_to(p16_row, (N_EXPERTS, N_EXPERTS)), 0.0),
                        axis=1, keepdims=True)
    rank = _dot(member_b, tri_ref[...]) + run_start
    lpos_ref[...] = jnp.concatenate(
        [jnp.sum(jnp.where(erow == ids[k], rank, 0.0), axis=0, keepdims=True) for k in range(TOP_K)],
        axis=0).astype(I32)
    p16_ref[pl.ds(pl.program_id(0), 1), :] = jnp.concatenate(
        [p16_row, jnp.zeros((1, LANE - N_EXPERTS), F32)], axis=1)


def _route(lgt, bias, tri):
    return pl.pallas_call(
        _route_kernel,
        out_shape=(
            jax.ShapeDtypeStruct((TOP_K, N_TOK), F32), jax.ShapeDtypeStruct((TOP_K, N_TOK), I32),
            jax.ShapeDtypeStruct((N_WIN, LANE), F32),
        ),
        grid=(N_WIN,),
        in_specs=[
            pl.BlockSpec((N_EXPERTS, WIN), lambda i: (0, i)),
            pl.BlockSpec((N_EXPERTS, 1), lambda i: (0, 0)),
            pl.BlockSpec((WIN, WIN), lambda i: (0, 0)),
        ],
        out_specs=(
            pl.BlockSpec((TOP_K, WIN), lambda i: (0, i)), pl.BlockSpec((TOP_K, WIN), lambda i: (0, i)),
            pl.BlockSpec((N_WIN, LANE), lambda i: (0, 0)),
        ),
        compiler_params=_cparams(("arbitrary",)),
        name="moe_route",
    )(lgt, bias, tri)


def _sort_kernel(lpos_ref, h_ref, xl_ref, oh_all):
    lp = jnp.where(pl.program_id(0) < N_WIN, lpos_ref[...], -1)
    riota = lax.broadcasted_iota(I32, (SORT_ROWS, WIN), 0).astype(jnp.int16)
    one = jnp.ones((SORT_ROWS, WIN), BF16)
    for j in range(RL // SORT_ROWS):
        rel = (lp - j * SORT_ROWS).astype(jnp.int16)
        oh = jnp.zeros((SORT_ROWS, WIN), BF16)
        for k in range(TOP_K):
            oh = jnp.where(rel[k:k + 1, :] == riota, one, oh)
        oh_all[j * SORT_ROWS:(j + 1) * SORT_ROWS, :] = oh
    x = h_ref[...]
    for nt in range(D_MODEL // SORT_ROWS):
        cols = slice(nt * SORT_ROWS, (nt + 1) * SORT_ROWS)
        xl_ref[:, cols] = _dot(oh_all[...], x[:, cols]).astype(BF16)


def _sort_rows(lpos, h2):
    last = N_WIN - 1
    return pl.pallas_call(
        _sort_kernel,
        out_shape=jax.ShapeDtypeStruct(((N_WIN + SPARE_WIN) * RL, D_MODEL), BF16),
        grid=(N_WIN + SPARE_WIN,),
        in_specs=[
            pl.BlockSpec((TOP_K, WIN), lambda i: (0, jnp.minimum(i, last))),
            pl.BlockSpec((WIN, D_MODEL), lambda i: (jnp.minimum(i, last), 0)),
        ],
        out_specs=pl.BlockSpec((RL, D_MODEL), lambda i: (i, 0)),
        scratch_shapes=[pltpu.VMEM((RL, WIN), BF16)],
        compiler_params=_cparams(("arbitrary",)),
        name="moe_sort",
    )(lpos, h2)


def _moe_tables(p16):
    n_w = jnp.arange(N_WIN, dtype=I32)
    run_start = jnp.cumsum(p16, axis=1) - p16
    rw = jnp.sum(p16, axis=1)
    cum_w = jnp.cumsum(p16, axis=0) - p16
    tot = jnp.sum(p16, axis=0)
    nblk = (tot + TME - 1) // TME
    blk_end = jnp.cumsum(nblk)
    blk0 = blk_end - nblk
    n_used = blk_end[-1]
    b = jnp.arange(N_EBLOCKS + SPARE_BLOCKS, dtype=I32)
    block_e = jnp.minimum(jnp.sum((blk_end[None, :] <= b[:, None]).astype(I32), axis=1), N_EXPERTS - 1)
    piece = jnp.arange(PIECES, dtype=I32) * PIECE
    rp = (b - blk0[block_e])[:, None] * TME + piece[None, :]
    valid = (b[:, None] < n_used) & (rp < tot[block_e][:, None])
    cum_e = cum_w.T[block_e]
    len_e = p16.T[block_e]
    start_e = run_start.T[block_e]
    in_win = (cum_e[:, None, :] <= rp[:, :, None]) & (rp[:, :, None] < (cum_e + len_e)[:, None, :])
    row = n_w[None, None, :] * RL + start_e[:, None, :] + rp[:, :, None] - cum_e[:, None, :]
    row = jnp.sum(jnp.where(in_win, row, 0), axis=2)
    zero_src = N_WIN * RL
    assert (1 + EXPERT_BUFS) * TME <= SPARE_WIN * RL
    trash = N_WIN * RL + (1 + b % EXPERT_BUFS)[:, None] * TME + piece[None, :]
    gather_row = (jnp.where(valid, row, zero_src) // PIECE).reshape(-1).astype(I32)
    scatter_row = (jnp.where(valid, row, trash) // PIECE).reshape(-1).astype(I32)
    rw = jnp.concatenate([rw, jnp.zeros((1,), rw.dtype)])
    return rw.astype(I32), block_e.astype(I32), n_used.astype(I32).reshape(1), gather_row, scatter_row


def _expert_kernel(be_ref, nb_ref, grow_ref, srow_ref, xl_hbm, wg_ref, wu_ref, wd_ref, yl_hbm,
                   xbuf, ybuf, wg_b, wu_b, wd_b, gsem, ssem):
    b = pl.program_id(0)
    nb = nb_ref[0]

    @pl.when(jnp.logical_and(b < nb, jnp.logical_or(b == 0, be_ref[b] != be_ref[jnp.maximum(b - 1, 0)])))
    def _():
        wg_b[...] = wg_ref[...].astype(BF16)
        wu_b[...] = wu_ref[...].astype(BF16)
        wd_b[...] = wd_ref[...].astype(BF16)

    def gather_start(blk, slot):
        for p in range(PIECES):
            pltpu.make_async_copy(xl_hbm.at[grow_ref[blk * PIECES + p]], xbuf.at[slot, p],
                                  gsem.at[slot]).start(priority=p % 2)

    def scatter_start(blk, slot):
        for p in range(PIECES):
            pltpu.make_async_copy(ybuf.at[slot, p], yl_hbm.at[srow_ref[blk * PIECES + p]],
                                  ssem.at[slot]).start(priority=p % 2)

    def gather_wait(slot):
        pltpu.make_async_copy(xl_hbm.at[pl.ds(0, PIECES)], xbuf.at[slot], gsem.at[slot]).wait()

    def scatter_wait(slot):
        pltpu.make_async_copy(ybuf.at[slot], yl_hbm.at[pl.ds(0, PIECES)], ssem.at[slot]).wait()

    @pl.when(b < nb)
    def _():
        slot = b % EXPERT_BUFS
        ahead = (b + 2) % EXPERT_BUFS

        @pl.when(b == 0)
        def _():
            gather_start(0, 0)
            gather_start(1, 1)
            ybuf[2] = jnp.zeros(ybuf.shape[1:], BF16)

        gather_wait(slot)

        @pl.when(b >= 2)
        def _():
            scatter_wait(slot)

        x = xbuf[slot].reshape(TME, D_MODEL)
        hb = _silu(_dot(x, wg_b[...])) * _dot(x, wu_b[...])
        gather_start(b + 2, ahead)
        scatter_start(jnp.where(b == 0, N_EBLOCKS + SPARE_BLOCKS - 1, b - 1), ahead)
        ybuf[slot] = _dot(hb.astype(BF16), wd_b[...]).astype(BF16).reshape(PIECES, PIECE, D_MODEL)

        @pl.when(b == nb - 1)
        def _():
            scatter_start(b, slot)
            gather_wait((b + 1) % EXPERT_BUFS)
            gather_wait(ahead)
            for s in range(EXPERT_BUFS):
                scatter_wait(s)


def _experts(block_e, n_used, gather_row, scatter_row, xl, wg, wu, wd, layer):
    def w_map(i, be, nb, gr, sr):
        return (layer, be[jnp.minimum(i, nb[0] - 1)], 0, 0)

    n_pieces = (N_WIN + SPARE_WIN) * RL // PIECE
    yl = pl.pallas_call(
        _expert_kernel,
        out_shape=jax.ShapeDtypeStruct((n_pieces, PIECE, D_MODEL), BF16),
        input_output_aliases={4: 0},
        grid_spec=pltpu.PrefetchScalarGridSpec(
            num_scalar_prefetch=4,
            grid=(N_EBLOCKS,),
            in_specs=[
                pl.BlockSpec(memory_space=pl.ANY),
                pl.BlockSpec((None, None, D_MODEL, D_EXPERT), w_map),
                pl.BlockSpec((None, None, D_MODEL, D_EXPERT), w_map),
                pl.BlockSpec((None, None, D_EXPERT, D_MODEL), w_map),
            ],
            out_specs=pl.BlockSpec(memory_space=pl.ANY),
            scratch_shapes=[
                pltpu.VMEM((EXPERT_BUFS, PIECES, PIECE, D_MODEL), BF16),
                pltpu.VMEM((EXPERT_BUFS, PIECES, PIECE, D_MODEL), BF16),
                pltpu.VMEM((D_MODEL, D_EXPERT), BF16), pltpu.VMEM((D_MODEL, D_EXPERT), BF16),
                pltpu.VMEM((D_EXPERT, D_MODEL), BF16),
                pltpu.SemaphoreType.DMA((EXPERT_BUFS,)), pltpu.SemaphoreType.DMA((EXPERT_BUFS,)),
            ],
        ),
        compiler_params=_cparams(("arbitrary",)),
        name="moe_experts",
    )(block_e, n_used, gather_row, scatter_row, xl.reshape(n_pieces, PIECE, D_MODEL), wg, wu, wd)
    return yl.reshape(n_pieces * PIECE, D_MODEL)


def _combine_kernel(rw_ref, yl_ref, lpt_ref, wt_ref, h_ref, x1_ref, mod_ref, lng_ref, lnb_ref,
                    wsg_ref, wsu_ref, wsd_ref, *rest, split):
    del rw_ref
    if split:
        outp_ref, outs_ref, p_w, lp_b, wt_b = rest
    else:
        out_ref, p_w, lp_b, wt_b = rest
    hb = h_ref[...]
    shared = _dot((_silu(_dot(hb, wsg_ref[...])) * _dot(hb, wsu_ref[...])).astype(BF16), wsd_ref[...])
    lp = lpt_ref[...]
    wt = wt_ref[...]
    for k in range(TOP_K):
        lp_b[k] = jnp.broadcast_to(lp[:, k:k + 1], (WIN, SORT_ROWS)).astype(jnp.int16)
        wt_b[k] = jnp.broadcast_to(wt[:, k:k + 1], (WIN, SORT_ROWS)).astype(BF16)
    ciota = lax.broadcasted_iota(I32, (WIN, SORT_ROWS), 1)
    for j in range(RL // SORT_ROWS):
        col = (ciota + j * SORT_ROWS).astype(jnp.int16)
        pm = jnp.zeros((WIN, SORT_ROWS), BF16)
        for k in range(TOP_K):
            pm = jnp.where(lp_b[k] == col, wt_b[k], pm)
        p_w[:, j * SORT_ROWS:(j + 1) * SORT_ROWS] = pm
    routed = _dot(p_w[...], yl_ref[...])
    z = ALPHA * x1_ref[...] + mod_ref[5:6, :] * (routed + shared)
    out = _layernorm_rows(z, lng_ref[...], lnb_ref[...])
    if split:
        @pl.when(pl.program_id(0) < N_PROMPT // WIN)
        def _():
            outp_ref[...] = out

        @pl.when(pl.program_id(0) >= N_PROMPT // WIN)
        def _():
            outs_ref[...] = out
    else:
        out_ref[...] = out


def _combine(rw, yl, lpos_t, wt, h2, x1, mods, lng, lnb, wsg, wsu, wsd, *, split):
    full = lambda shape: pl.BlockSpec(shape, lambda i, rw: (0,) * len(shape))
    if split:
        out_shape = (jax.ShapeDtypeStruct((N_PROMPT, D_MODEL), F32), jax.ShapeDtypeStruct((N_SAMPLE, D_MODEL), F32))
        out_specs = tuple(_row_pair_specs(WIN, D_MODEL))
    else:
        out_shape = jax.ShapeDtypeStruct((N_TOK, D_MODEL), F32)
        out_specs = pl.BlockSpec((WIN, D_MODEL), lambda i, rw: (i, 0))
    return pl.pallas_call(
        functools.partial(_combine_kernel, split=split),
        out_shape=out_shape,
        grid_spec=pltpu.PrefetchScalarGridSpec(
            num_scalar_prefetch=1,
            grid=(N_WIN,),
            in_specs=[
                pl.BlockSpec((RL, D_MODEL), lambda i, rw: (i, 0)),
                pl.BlockSpec((WIN, TOP_K), lambda i, rw: (i, 0)),
                pl.BlockSpec((WIN, TOP_K), lambda i, rw: (i, 0)),
                pl.BlockSpec((WIN, D_MODEL), lambda i, rw: (i, 0)),
                pl.BlockSpec((WIN, D_MODEL), lambda i, rw: (i, 0)),
                pl.BlockSpec((None, MOD_ROWS, D_MODEL), lambda i, rw: (_cond_block(i, WIN), 0, 0)),
                full((1, D_MODEL)), full((1, D_MODEL)),
                full((D_MODEL, D_EXPERT)), full((D_MODEL, D_EXPERT)), full((D_EXPERT, D_MODEL)),
            ],
            out_specs=out_specs,
            scratch_shapes=[pltpu.VMEM((WIN, RL), BF16),
                            pltpu.VMEM((TOP_K, WIN, SORT_ROWS), jnp.int16),
                            pltpu.VMEM((TOP_K, WIN, SORT_ROWS), BF16)],
        ),
        compiler_params=_cparams(("arbitrary",)),
        name="moe_combine",
    )(rw, yl, lpos_t, wt, h2, x1, mods, lng, lnb, wsg, wsu, wsd)


def _moe_and_norm(x1, h2, lgt, mods, lng, lnb, router_bias, tri, wg, wu, wd, wsg, wsu, wsd, *, layer, split):
    wts, lpos, p16 = _route(lgt, router_bias.reshape(N_EXPERTS, 1), tri)
    rw, block_e, n_used, gather_row, scatter_row = _moe_tables(p16[:, :N_EXPERTS].astype(I32))
    xl = _sort_rows(lpos, h2)
    yl = _experts(block_e, n_used, gather_row, scatter_row, xl, wg, wu, wd, layer)
    return _combine(rw, yl, lpos.T, wts.T, h2, x1, mods, lng, lnb, wsg, wsu, wsd, split=split)


def _rope_tables_mla():
    t = jnp.arange(DEC_SEQ)
    row = (t // GRID_W).astype(F32)
    col = (t % GRID_W).astype(F32)
    n = QK_ROPE // 4
    inv = ROPE_BASE ** (-jnp.arange(n, dtype=F32) / n)
    ang_r = row[:, None] * inv
    ang_c = col[:, None] * inv
    cos = jnp.ones((DEC_SEQ, LANE), F32)
    sin_m = jnp.zeros((DEC_SEQ, LANE), F32)
    sin_p = jnp.zeros((DEC_SEQ, LANE), F32)
    l0 = ROPE_LANE0
    for base, ang in ((l0, ang_r), (l0 + 2 * n, ang_c)):
        c, s = jnp.cos(ang), jnp.sin(ang)
        cos = cos.at[:, base:base + n].set(c).at[:, base + n:base + 2 * n].set(c)
        sin_m = sin_m.at[:, base:base + n].set(-s)
        sin_p = sin_p.at[:, base + n:base + 2 * n].set(s)
    ident = (jnp.ones((TM, LANE), F32), jnp.zeros((TM, LANE), F32), jnp.zeros((TM, LANE), F32))
    return tuple(jnp.concatenate([i, tbl], axis=0) for i, tbl in zip(ident, (cos, sin_m, sin_p)))


def _rope_tables_ret():
    half = RET_DK // 2
    theta = ROPE_BASE ** (-jnp.linspace(0.0, 1.0, half, dtype=F32))
    ang = jnp.arange(DEC_SEQ, dtype=F32)[:, None] * theta
    cos = jnp.concatenate([jnp.ones((TM, half), F32), jnp.cos(ang)], axis=0)
    sin = jnp.concatenate([jnp.zeros((TM, half), F32), jnp.sin(ang)], axis=0)
    return cos, sin


def _pad_heads(w, width, lane0=0):
    k = w.shape[0]
    w = w.reshape(k, MLA_HEADS, width)
    out = jnp.zeros((k, MLA_HEADS, HEAD_PAD), w.dtype).at[:, :, lane0:lane0 + width].set(w)
    return out.reshape(k, MLA_HEADS * HEAD_PAD)


def _rg_gate_weights(wa, ba, wx, bx):
    n_ct = RG_W // LANE
    per = LANE // RG_BW
    tiles_w, tiles_b = [], []
    for c in range(n_ct):
        cols_w, cols_b = [], []
        for d in range(2):
            for w, b in ((wa, ba), (wx, bx)):
                m = jnp.zeros((LANE, LANE), F32)
                for p in range(per):
                    m = m.at[p * RG_BW:(p + 1) * RG_BW, p * RG_BW:(p + 1) * RG_BW].set(w[d, c * per + p])
                cols_w.append(m)
                cols_b.append(b[d, c * LANE:(c + 1) * LANE])
        tiles_w.append(jnp.concatenate(cols_w, axis=1))
        tiles_b.append(jnp.concatenate(cols_b, axis=0)[None, :])
    return jnp.stack(tiles_w).astype(BF16), jnp.stack(tiles_b)


def kernel(x_prompt, x_sample, cache_mla_ckv, cache_mla_krope, state_rglru, state_ret, c, c_ctx, w_ada, b_ada,
           ln_g, ln_b, w_in_ab, rg_conv_w, rg_conv_b, rg_wa, rg_ba, rg_wx, rg_bx, rg_lambda, mla_q_norm, mla_w_uq,
           mla_kv_norm, mla_w_ukv, w_out_ab, w_in_c, ret_gamma_logit, w_out_c, w_router, router_bias,
           w_exp_gate, w_exp_up, w_exp_down, w_sh_gate, w_sh_up, w_sh_down):
    xp = x_prompt.reshape(N_PROMPT, D_MODEL)
    xs = x_sample.reshape(N_SAMPLE, D_MODEL)
    cond = jnp.zeros((16, D_MODEL), F32).at[0].set(c_ctx).at[1:1 + DEC_BATCH].set(c)
    mods_all = _ada_modulation(cond, w_ada, b_ada).reshape(DEPTH, 16, 6, D_MODEL)[:, :N_COND]
    mods_all = jnp.pad(mods_all, ((0, 0), (0, 0), (0, MOD_ROWS - 6), (0, 0)))

    tri = (jnp.arange(WIN)[:, None] < jnp.arange(WIN)[None, :]).astype(BF16)
    wr_t = jnp.swapaxes(w_router, 1, 2)
    wr_hi = wr_t.astype(BF16)
    wr_lo = (wr_t - wr_hi.astype(F32)).astype(BF16)
    wg_e, wu_e, wd_e = w_exp_gate, w_exp_up, w_exp_down
    wsg, wsu, wsd = w_sh_gate.astype(BF16), w_sh_up.astype(BF16), w_sh_down.astype(BF16)

    l, e = 0, 0
    mods = mods_all[l]
    n_main = 2 * RG_W + Q_LORA + KV_LORA
    w_main = w_in_ab[e][:, :n_main].astype(BF16)
    w_kr = jnp.zeros((D_MODEL, LANE), F32).at[:, ROPE_LANE0:ROPE_LANE0 + QK_ROPE].set(w_in_ab[e][:, n_main:]).astype(BF16)
    main, krp = _proj_ab(xp, xs, mods, w_main, w_kr)

    wg_rg, bg_rg = _rg_gate_weights(rg_wa[e], rg_ba[e], rg_wx[e], rg_bx[e])
    h0_p = jnp.zeros((BATCH, 2, RG_W), F32)
    rg_args = (rg_conv_w[e], rg_conv_b[e].reshape(1, RG_W), wg_rg, bg_rg, rg_lambda[e])
    yrg_p, rg_fin = _rglru(main, *rg_args, h0_p, n_seq=BATCH, seq=SEQ, row_block0=0)
    yrg_s, _ = _rglru(main, *rg_args, state_rglru[:, e],
                      n_seq=DEC_BATCH, seq=DEC_SEQ, row_block0=N_PROMPT // DEC_SEQ)

    cos_t, sm_t, sp_t = _rope_tables_mla()
    w_uq = mla_w_uq[e].reshape(Q_LORA, MLA_HEADS, QK_NOPE + QK_ROPE)
    wq_p = _pad_heads(w_uq.reshape(Q_LORA, -1), QK_NOPE + QK_ROPE).astype(BF16)
    w_ukv = mla_w_ukv[e].reshape(KV_LORA, MLA_HEADS, QK_NOPE + V_HEAD)
    wuk_p = _pad_heads(w_ukv[:, :, :QK_NOPE].reshape(KV_LORA, -1), QK_NOPE).astype(BF16)
    wuvt = w_ukv[:, :, QK_NOPE:].reshape(KV_LORA, MLA_HEADS * V_HEAD).T.astype(BF16)
    q_att, k_att, v_att, ckv_n = _mla_prep(main, krp, cos_t, sm_t, sp_t, mla_q_norm[e].reshape(1, Q_LORA), wq_p,
                                           mla_kv_norm[e].reshape(1, KV_LORA), wuk_p, wuvt)
    ctx_ckv = cache_mla_ckv[:, e].reshape(DEC_BATCH * PAST_LEN, KV_LORA)
    ctx_krp = jnp.zeros((DEC_BATCH * PAST_LEN, LANE), F32).at[:, ROPE_LANE0:ROPE_LANE0 + QK_ROPE].set(
        cache_mla_krope[:, e].reshape(DEC_BATCH * PAST_LEN, QK_ROPE))
    kc_att, vc_att = _mla_ctx(ctx_ckv, ctx_krp, wuk_p, wuvt)

    o_att_p = _attention(q_att, k_att, v_att, None, None, n_seq=BATCH, seq=SEQ, row_block0=0, tq=SEQ)
    o_att_s = _attention(q_att, k_att, v_att, kc_att, vc_att,
                         n_seq=DEC_BATCH, seq=DEC_SEQ, row_block0=N_PROMPT // DEC_SEQ, tq=TQ)

    w_out = w_out_ab[e].astype(BF16)
    x1, h2, lgt = _out_ab(yrg_p, yrg_s, o_att_p, o_att_s, w_out[:RG_W], w_out[RG_W:], xp, xs, mods,
                          ln_g[l, 0].reshape(1, D_MODEL), ln_b[l, 0].reshape(1, D_MODEL), wr_hi[l], wr_lo[l])
    x = _moe_and_norm(x1, h2, lgt, mods, ln_g[l, 1].reshape(1, D_MODEL), ln_b[l, 1].reshape(1, D_MODEL),
                      router_bias[l], tri, wg_e, wu_e, wd_e, wsg[l], wsu[l], wsd[l], layer=l, split=False)

    new_ckv = ckv_n.reshape(BATCH, 1, SEQ, KV_LORA)
    new_krope = krp[:N_PROMPT, ROPE_LANE0:ROPE_LANE0 + QK_ROPE].reshape(BATCH, 1, SEQ, QK_ROPE)
    new_rg = rg_fin.reshape(BATCH, 1, 2, RG_W)

    l, o = 1, 0
    mods = mods_all[l]
    qk = RET_HEADS * RET_DK
    w_c = w_in_c[o].astype(BF16)
    cos_r, sin_r = _rope_tables_ret()
    q_r, k_r, v_r, g_r = _proj_c(x, mods, cos_r, sin_r, w_c[:, :qk], w_c[:, qk:2 * qk],
                                 w_c[:, 2 * qk:2 * qk + MIX_C], w_c[:, 2 * qk + MIX_C:])
    gam = jnp.broadcast_to(ret_gamma_logit[o].astype(F32)[:, :, None, None], (2, RET_HEADS, SUBLANE, LANE))
    r0_p = jnp.zeros((BATCH, 2, RET_HEADS, RET_DK, RET_DV), F32)
    o_ret_p, r_fin = _retention(q_r, k_r, v_r, gam, r0_p, n_seq=BATCH, seq=SEQ, row_block0=0, with_state=True)
    (o_ret_s,) = _retention(q_r, k_r, v_r, gam, state_ret[:, o],
                            n_seq=DEC_BATCH, seq=DEC_SEQ, row_block0=N_PROMPT // DEC_SEQ, with_state=False)
    x1, h2, lgt = _out_c(o_ret_p, o_ret_s, g_r, w_out_c[o].astype(BF16), x, mods,
                         ln_g[l, 0].reshape(1, D_MODEL), ln_b[l, 0].reshape(1, D_MODEL), wr_hi[l], wr_lo[l])
    y_p, y_s = _moe_and_norm(x1, h2, lgt, mods, ln_g[l, 1].reshape(1, D_MODEL), ln_b[l, 1].reshape(1, D_MODEL),
                             router_bias[l], tri, wg_e, wu_e, wd_e, wsg[l], wsu[l], wsd[l], layer=l, split=True)

    y_prompt = y_p.reshape(BATCH, SEQ, D_MODEL)
    y_sample = y_s.reshape(DEC_BATCH, DEC_SEQ, D_MODEL)
    new_ret = r_fin.reshape(BATCH, 1, 2, RET_HEADS, RET_DK, RET_DV)
    return (y_prompt, y_sample, new_ckv, new_krope, new_rg, new_ret)
```
